```python
import math
import jax
import jax.numpy as jnp
from jax import lax
import numpy as np

D_MODEL = 1024
BATCH = 16
SEQ = 2048
DEPTH = 2

HEAD_DIM = 64
A_GROUPS = ((128, 1), (512, 4), (2048, 16))
A_HEADS_PER_GROUP = 2
A_HEADS = A_HEADS_PER_GROUP * len(A_GROUPS)
B_HEADS = 6
B_KV_HEADS = 2
B_GROUP = B_HEADS // B_KV_HEADS
C_HEADS = 4
N_HEADS = A_HEADS + B_HEADS + C_HEADS
MIX_WIDTH = N_HEADS * HEAD_DIM
NSA_CMP_BLOCK = 32
NSA_CMP_STRIDE = 16
NSA_CMP_HIDDEN = 256
NSA_SLC_BLOCK = 64
NSA_SLC_TOPK = 16
NSA_WINDOW = 512
MOBA_BLOCK = 256
MOBA_TOPK = 3
REL_BUCKETS = 32
REL_MAX_DIST = 128
D_FF = 2816
CONV_WIDTH = 3
BAND_BLOCK = 128
SEL_Q_CHUNK = 32
RMS_EPS = 1e-6
NEG_INF = -1e30
FORCE_BONUS = 1e4
ATTN_SCALE = HEAD_DIM ** -0.5
IN_SPLITS = ((A_HEADS * HEAD_DIM,) * 3 + (B_HEADS * HEAD_DIM,) + (B_KV_HEADS * HEAD_DIM,) * 6
             + (B_HEADS * 3,) + (C_HEADS * HEAD_DIM,) * 3 + (3 * D_MODEL,))
IN_COLS = sum(IN_SPLITS)

kernel_name = 'hybrid_dilated_nsa_moba_convffn'


def rmsnorm(x, g):
    xf = x.astype(jnp.float32)
    y = xf * lax.rsqrt(jnp.mean(xf * xf, axis=-1, keepdims=True) + RMS_EPS)
    return (y * g.astype(jnp.float32)).astype(x.dtype)


def rel_bucket(dist):
    n = jnp.maximum(dist, 0)
    exact = REL_BUCKETS // 2
    nf = jnp.maximum(n, 1).astype(jnp.float32)
    large = exact + (jnp.log(nf / exact) / math.log(REL_MAX_DIST / exact)
                     * (REL_BUCKETS - exact)).astype(jnp.int32)
    return jnp.where(n < exact, n, jnp.minimum(large, REL_BUCKETS - 1))


def band_attention(q, k, v, tbl, max_rel, dil):
    n, L, hkv, g, dh = q.shape
    n_prev = -(-max_rel // BAND_BLOCK)
    nb = -(-L // BAND_BLOCK)
    pad = nb * BAND_BLOCK - L

    def pad_len(a):
        return jnp.pad(a, [(0, 0), (0, pad)] + [(0, 0)] * (a.ndim - 2))

    qb = pad_len(q).reshape(n, nb, BAND_BLOCK, hkv, g, dh)
    kb = pad_len(k).reshape(n, nb, BAND_BLOCK, hkv, dh)
    vb = pad_len(v).reshape(n, nb, BAND_BLOCK, hkv, dh)

    def window(a):
        ap = jnp.pad(a, [(0, 0), (n_prev, 0), (0, 0), (0, 0), (0, 0)])
        return jnp.concatenate([ap[:, i:i + nb] for i in range(n_prev + 1)], axis=2)

    kw, vw = window(kb), window(vb)
    kw_len = (n_prev + 1) * BAND_BLOCK
    rel = jnp.arange(BAND_BLOCK)[:, None] + n_prev * BAND_BLOCK - jnp.arange(kw_len)[None, :]
    key_pos = (jnp.arange(nb)[:, None] * BAND_BLOCK + jnp.arange(kw_len)[None, :]
               - n_prev * BAND_BLOCK)
    valid = ((rel >= 0) & (rel <= max_rel))[None] & (key_pos >= 0)[:, None, :]
    bias = tbl[rel_bucket(rel * dil)].reshape(BAND_BLOCK, kw_len, hkv, g).transpose(2, 3, 0, 1)
    logits = jnp.einsum('nbqhgd,nbkhd->nbhgqk', qb, kw).astype(jnp.float32) * ATTN_SCALE + bias
    logits = jnp.where(valid[None, :, None, None], logits, NEG_INF)
    lse = jax.nn.logsumexp(logits, axis=-1)
    p = jnp.exp(logits - lse[..., None]).astype(v.dtype)
    out = jnp.einsum('nbhgqk,nbkhd->nbqhgd', p, vw).reshape(n, nb * BAND_BLOCK, hkv, g, dh)[:, :L]
    lse = lse.transpose(0, 1, 4, 2, 3).reshape(n, nb * BAND_BLOCK, hkv, g)[:, :L]
    return out, lse


def dilated_mixer(q, k, v, tbl):
    b, s, _, dh = q.shape
    hg = A_HEADS_PER_GROUP
    outs, lses = [], []
    for gi, (win, dil) in enumerate(A_GROUPS):
        hs = slice(gi * hg, (gi + 1) * hg)
        L = s // dil

        def to_sub(a):
            return a.reshape(b, L, dil, hg, dh).swapaxes(1, 2).reshape(b * dil, L, hg, dh)

        o, lse = band_attention(to_sub(q[:, :, hs])[:, :, :, None], to_sub(k[:, :, hs]),
                                to_sub(v[:, :, hs]), tbl[:, hs], max_rel=win // dil, dil=dil)
        outs.append(o[:, :, :, 0].reshape(b, dil, L, hg, dh).swapaxes(1, 2).reshape(b, s, hg, dh))
        lses.append(lse[..., 0].reshape(b, dil, L, hg).swapaxes(1, 2).reshape(b, s, hg))
    alpha = jax.nn.softmax(jnp.stack(lses), axis=0)
    o = jnp.concatenate([alpha[gi][..., None] * outs[gi] for gi in range(len(A_GROUPS))], axis=2)
    return o.reshape(b, s, A_HEADS * HEAD_DIM).astype(q.dtype)


def nsa_mixer(q, kc, vc, ks, vs, kw, vw, gate_logits, tbl,
              pe_k, w1_k, w2_k, pe_v, w1_v, w2_v):
    b, s = q.shape[:2]
    hkv, g, dh = B_KV_HEADS, B_GROUP, HEAD_DIM
    q = q.reshape(b, s, hkv, g, dh)
    t = jnp.arange(s)
    n_cmp = (s - NSA_CMP_BLOCK) // NSA_CMP_STRIDE + 1
    blk_idx = jnp.arange(n_cmp)[:, None] * NSA_CMP_STRIDE + jnp.arange(NSA_CMP_BLOCK)[None, :]

    def compress(a, pe, w1, w2):
        blocks = a[:, blk_idx] + pe[:, None, :]
        flat = blocks.transpose(0, 1, 3, 2, 4).reshape(b, n_cmp, hkv, NSA_CMP_BLOCK * dh)
        return jax.nn.gelu(flat @ w1) @ w2

    kcmp = compress(kc, pe_k, w1_k, w2_k)
    vcmp = compress(vc, pe_v, w1_v, w2_v)
    cmp_ok = (jnp.arange(n_cmp) * NSA_CMP_STRIDE + NSA_CMP_BLOCK - 1)[None, :] <= t[:, None]
    logits = jnp.einsum('bshgd,bchd->bhgsc', q, kcmp).astype(jnp.float32) * ATTN_SCALE
    p_cmp = jax.nn.softmax(jnp.where(cmp_ok, logits, NEG_INF), axis=-1) * cmp_ok
    o_cmp = jnp.einsum('bhgsc,bchd->bshgd', p_cmp.astype(vcmp.dtype), vcmp)
    n_slc = s // NSA_SLC_BLOCK
    c_start = jnp.arange(n_cmp) * NSA_CMP_STRIDE
    j_start = jnp.arange(n_slc) * NSA_SLC_BLOCK
    overlap = ((c_start[:, None] < j_start[None, :] + NSA_SLC_BLOCK)
               & (c_start[:, None] + NSA_CMP_BLOCK > j_start[None, :])).astype(jnp.float32)
    imp = jnp.einsum('bhgsc,cj->bhsj', p_cmp, overlap)
    j = jnp.arange(n_slc)[None, :]
    jt = (t // NSA_SLC_BLOCK)[:, None]
    forced = (j == 0) | (j == jt) | (j == jt - 1)
    score = jnp.where(j <= jt, imp + FORCE_BONUS * forced, NEG_INF)
    n_top = min(NSA_SLC_TOPK, n_slc)
    top_val, top_idx = lax.top_k(score, n_top)
    top_idx = top_idx.transpose(0, 2, 1, 3)
    top_ok = (top_val > NEG_INF / 2).transpose(0, 2, 1, 3)
    ks_t = ks.reshape(b, n_slc, NSA_SLC_BLOCK, hkv, dh).transpose(0, 3, 1, 2, 4)
    vs_t = vs.reshape(b, n_slc, NSA_SLC_BLOCK, hkv, dh).transpose(0, 3, 1, 2, 4)
    b_ix = jnp.arange(b)[:, None, None, None]
    h_ix = jnp.arange(hkv)[None, None, :, None]
    tbl_t = tbl.reshape(REL_BUCKETS, hkv, g).transpose(1, 0, 2)

    def sel_chunk(c):
        t0 = c * SEL_Q_CHUNK
        q_c = lax.dynamic_slice_in_dim(q, t0, SEL_Q_CHUNK, axis=1)
        idx_c = lax.dynamic_slice_in_dim(top_idx, t0, SEL_Q_CHUNK, axis=1)
        ok_c = lax.dynamic_slice_in_dim(top_ok, t0, SEL_Q_CHUNK, axis=1)
        kg = ks_t[b_ix, h_ix, idx_c]
        vg = vs_t[b_ix, h_ix, idx_c]
        pos = idx_c[..., None] * NSA_SLC_BLOCK + jnp.arange(NSA_SLC_BLOCK)
        dist = (t0 + jnp.arange(SEL_Q_CHUNK))[None, :, None, None, None] - pos
        ok = ok_c[..., None] & (dist >= 0)
        bias = tbl_t[h_ix[..., None], rel_bucket(dist)].transpose(0, 1, 2, 5, 3, 4)
        lg = jnp.einsum('bqhgd,bqhkld->bqhgkl', q_c, kg).astype(jnp.float32) * ATTN_SCALE + bias
        lg = jnp.where(ok[:, :, :, None], lg, NEG_INF)
        lg = lg.reshape(b, SEL_Q_CHUNK, hkv, g, n_top * NSA_SLC_BLOCK)
        p = jax.nn.softmax(lg, axis=-1).astype(vg.dtype)
        return jnp.einsum('bqhgn,bqhnd->bqhgd', p,
                          vg.reshape(b, SEL_Q_CHUNK, hkv, n_top * NSA_SLC_BLOCK, dh))

    o_slc = lax.map(sel_chunk, jnp.arange(s // SEL_Q_CHUNK))
    o_slc = jnp.moveaxis(o_slc, 0, 1).reshape(b, s, hkv, g, dh)
    o_win, _ = band_attention(q, kw, vw, tbl, max_rel=NSA_WINDOW - 1, dil=1)
    gates = jax.nn.sigmoid(gate_logits.astype(jnp.float32)).reshape(b, s, hkv, g, 3)
    o = gates[..., 0:1] * o_cmp + gates[..., 1:2] * o_slc + gates[..., 2:3] * o_win
    return o.reshape(b, s, B_HEADS * HEAD_DIM).astype(q.dtype)


def moba_mixer(q, k, v, tbl):
    b, s, h, dh = q.shape
    n_blk = -(-s // MOBA_BLOCK)
    pad = n_blk * MOBA_BLOCK - s
    k_p = jnp.pad(k, ((0, 0), (0, pad), (0, 0), (0, 0)))
    v_p = jnp.pad(v, ((0, 0), (0, pad), (0, 0), (0, 0)))
    kb = k_p.reshape(b, n_blk, MOBA_BLOCK, h, dh)
    vb = v_p.reshape(b, n_blk, MOBA_BLOCK, h, dh)
    kmean = jnp.mean(kb.astype(jnp.float32), axis=2)
    t = jnp.arange(s)
    past = jnp.arange(n_blk)[None, :] < (t // MOBA_BLOCK)[:, None]
    gate = jnp.einsum('bshd,bnhd->bshn', q.astype(jnp.float32), kmean)
    gate = jnp.where(past[None, :, None, :], gate, NEG_INF)
    n_top = min(MOBA_TOPK, n_blk)
    top_val, top_idx = lax.top_k(gate, n_top)
    top_ok = top_val > NEG_INF / 2
    kb_t = kb.transpose(0, 3, 1, 2, 4)
    vb_t = vb.transpose(0, 3, 1, 2, 4)
    b_ix = jnp.arange(b)[:, None, None, None]
    h_ix = jnp.arange(h)[None, None, :, None]
    tbl_t = tbl.T
    n_sel = n_top * MOBA_BLOCK

    def chunk(c):
        t0 = c * SEL_Q_CHUNK
        tq = t0 + jnp.arange(SEL_Q_CHUNK)
        q_c = lax.dynamic_slice_in_dim(q, t0, SEL_Q_CHUNK, axis=1)
        idx_c = lax.dynamic_slice_in_dim(top_idx, t0, SEL_Q_CHUNK, axis=1)
        ok_c = lax.dynamic_slice_in_dim(top_ok, t0, SEL_Q_CHUNK, axis=1)
        kg = kb_t[b_ix, h_ix, idx_c]
        vg = vb_t[b_ix, h_ix, idx_c]
        dist = tq[None, :, None, None, None] - (idx_c[..., None] * MOBA_BLOCK + jnp.arange(MOBA_BLOCK))
        bias = tbl_t[h_ix[..., None], rel_bucket(dist)]
        l_sel = jnp.einsum('bqhd,bqhkld->bqhkl', q_c, kg).astype(jnp.float32) * ATTN_SCALE + bias
        l_sel = jnp.where(ok_c[..., None], l_sel, NEG_INF).reshape(b, SEL_Q_CHUNK, h, n_sel)
        own0 = (t0 // MOBA_BLOCK) * MOBA_BLOCK
        k_own = lax.dynamic_slice_in_dim(k_p, own0, MOBA_BLOCK, axis=1)
        v_own = lax.dynamic_slice_in_dim(v_p, own0, MOBA_BLOCK, axis=1)
        dist_own = tq[:, None] - (own0 + jnp.arange(MOBA_BLOCK))[None, :]
        l_own = (jnp.einsum('bqhd,blhd->bqhl', q_c, k_own).astype(jnp.float32) * ATTN_SCALE
                 + tbl[rel_bucket(dist_own)].transpose(0, 2, 1))
        l_own = jnp.where((dist_own >= 0)[:, None, :], l_own, NEG_INF)
        p = jax.nn.softmax(jnp.concatenate([l_sel, l_own], axis=-1), axis=-1).astype(v.dtype)
        return (jnp.einsum('bqhn,bqhnd->bqhd', p[..., :n_sel],
                           vg.reshape(b, SEL_Q_CHUNK, h, n_sel, dh))
                + jnp.einsum('bqhl,blhd->bqhd', p[..., n_sel:], v_own))

    o = lax.map(chunk, jnp.arange(s // SEL_Q_CHUNK))
    return jnp.moveaxis(o, 0, 1).reshape(b, s, h * dh).astype(q.dtype)


def conv_ffn(x, w_up, conv_w, conv_b, w_down):
    h = x @ w_up
    h = lax.conv_general_dilated(h, conv_w[:, None, :].astype(h.dtype), window_strides=(1,),
                                 padding=[(CONV_WIDTH - 1, 0)],
                                 dimension_numbers=('NWC', 'WIO', 'NWC'),
                                 feature_group_count=h.shape[-1]) + conv_b
    a, u = jnp.split(h, 2, axis=-1)
    return (jax.nn.silu(a) * u) @ w_down


def setup_inputs(seed: int = 0) -> dict:
    key = jax.random.key(seed)
    k = jax.random.split(key, 18)

    def nrm(kk, shape, scale):
        return jax.random.normal(kk, shape, jnp.float32) * scale

    two_ff = 2 * D_FF
    cmp_in = NSA_CMP_BLOCK * HEAD_DIM
    return {
        'x': nrm(k[0], (BATCH, SEQ, D_MODEL), 1.0),
        'rel_bias': nrm(k[1], (REL_BUCKETS, N_HEADS), 0.3),
        'norm_mix': 1.0 + nrm(k[2], (DEPTH, D_MODEL), 0.01),
        'w_in': nrm(k[3], (DEPTH, D_MODEL, IN_COLS), D_MODEL ** -0.5),
        'cmp_pe_k': nrm(k[4], (DEPTH, NSA_CMP_BLOCK, HEAD_DIM), 0.1),
        'cmp_w1_k': nrm(k[5], (DEPTH, cmp_in, NSA_CMP_HIDDEN), cmp_in ** -0.5),
        'cmp_w2_k': nrm(k[6], (DEPTH, NSA_CMP_HIDDEN, HEAD_DIM), NSA_CMP_HIDDEN ** -0.5),
        'cmp_pe_v': nrm(k[7], (DEPTH, NSA_CMP_BLOCK, HEAD_DIM), 0.1),
        'cmp_w1_v': nrm(k[8], (DEPTH, cmp_in, NSA_CMP_HIDDEN), cmp_in ** -0.5),
        'cmp_w2_v': nrm(k[9], (DEPTH, NSA_CMP_HIDDEN, HEAD_DIM), NSA_CMP_HIDDEN ** -0.5),
        'w_branch': nrm(k[10], (DEPTH, MIX_WIDTH, D_MODEL), MIX_WIDTH ** -0.5),
        'w_out': nrm(k[11], (DEPTH, D_MODEL, D_MODEL), D_MODEL ** -0.5),
        'norm_ffn': 1.0 + nrm(k[12], (DEPTH, D_MODEL), 0.01),
        'w_up': nrm(k[13], (DEPTH, D_MODEL, two_ff), D_MODEL ** -0.5),
        'conv_w': nrm(k[14], (DEPTH, CONV_WIDTH, two_ff), CONV_WIDTH ** -0.5),
        'conv_b': nrm(k[15], (DEPTH, two_ff), 0.01),
        'w_down': nrm(k[16], (DEPTH, D_FF, D_MODEL), D_FF ** -0.5),
        'norm_final': 1.0 + nrm(k[17], (D_MODEL,), 0.01),
    }


def reference(x, rel_bias, norm_mix, w_in, cmp_pe_k, cmp_w1_k, cmp_w2_k, cmp_pe_v, cmp_w1_v,
              cmp_w2_v, w_branch, w_out, norm_ffn, w_up, conv_w, conv_b, w_down, norm_final):
    b, s, _ = x.shape
    split_at = np.cumsum(IN_SPLITS)[:-1].tolist()
    tbl_a = rel_bias[:, :A_HEADS]
    tbl_b = rel_bias[:, A_HEADS:A_HEADS + B_HEADS]
    tbl_c = rel_bias[:, A_HEADS + B_HEADS:]
    r0 = A_HEADS * HEAD_DIM
    r1 = r0 + B_HEADS * HEAD_DIM

    def heads(a):
        return a.reshape(b, s, -1, HEAD_DIM)

    for i in range(DEPTH):
        hn = rmsnorm(x, norm_mix[i])
        (aq, ak, av, bq, bkc, bvc, bks, bvs, bkw, bvw, bg, cq, ck, cv, mg) = jnp.split(
            hn @ w_in[i], split_at, axis=-1)
        o_a = dilated_mixer(heads(aq), heads(ak), heads(av), tbl_a)
        o_b = nsa_mixer(heads(bq), heads(bkc), heads(bvc), heads(bks), heads(bvs), heads(bkw),
                        heads(bvw), bg, tbl_b, cmp_pe_k[i], cmp_w1_k[i], cmp_w2_k[i],
                        cmp_pe_v[i], cmp_w1_v[i], cmp_w2_v[i])
        o_c = moba_mixer(heads(cq), heads(ck), heads(cv), tbl_c)
        wb = w_branch[i]
        gates = jax.nn.sigmoid(mg.astype(jnp.float32)).reshape(b, s, 3, D_MODEL)
        merged = (gates[:, :, 0] * (o_a @ wb[:r0]) + gates[:, :, 1] * (o_b @ wb[r0:r1])
                  + gates[:, :, 2] * (o_c @ wb[r1:]))
        x = x + merged.astype(x.dtype) @ w_out[i]
        x = x + conv_ffn(rmsnorm(x, norm_ffn[i]), w_up[i], conv_w[i], conv_b[i], w_down[i])
    return rmsnorm(x, norm_final)
```

```python
import functools
import math

import jax
import jax.numpy as jnp
import numpy as np
from jax import lax
from jax.experimental import pallas as pl
from jax.experimental.pallas import tpu as pltpu

F32 = jnp.float32
BF16 = jnp.bfloat16

D_MODEL = 1024
HEAD_DIM = 64
A_GROUPS = ((128, 1), (512, 4), (2048, 16))
A_HEADS_PER_GROUP = 2
A_HEADS = 6
B_HEADS = 6
B_KV_HEADS = 2
B_GROUP = 3
C_HEADS = 4
NSA_CMP_BLOCK = 32
NSA_CMP_STRIDE = 16
NSA_CMP_HIDDEN = 256
NSA_SLC_BLOCK = 64
NSA_SLC_TOPK = 16
NSA_WINDOW = 512
MOBA_BLOCK = 256
MOBA_TOPK = 3
REL_BUCKETS = 32
REL_MAX_DIST = 128
D_FF = 2816
RMS_EPS = 1e-6
NEG_INF = -1e30
FORCE_BONUS = 1e4
ATTN_SCALE = HEAD_DIM ** -0.5

LANE = 128
A_TILE = 128
T_TILE = 256
ROW_TILE = 512
FFN_HALO = 8
VMEM_LIMIT = 56 * 1024 * 1024

_A_COLS = 3 * A_HEADS * HEAD_DIM
_PB_COLS = B_HEADS * HEAD_DIM + 4 * B_KV_HEADS * HEAD_DIM
_CMP_COLS = 2 * B_KV_HEADS * HEAD_DIM
_C_COLS = 3 * C_HEADS * HEAD_DIM
_MG_COLS = 3 * D_MODEL
_BG_COLS = LANE
_OFF_A = 0
_OFF_PB = _OFF_A + _A_COLS
_OFF_CMP = _OFF_PB + _PB_COLS
_OFF_C = _OFF_CMP + _CMP_COLS
_OFF_MG = _OFF_C + _C_COLS
_OFF_BG = _OFF_MG + _MG_COLS
_W_COLS = _OFF_BG + _BG_COLS


def _params(n_grid):
    return pltpu.CompilerParams(dimension_semantics=("arbitrary",) * n_grid,
                                vmem_limit_bytes=VMEM_LIMIT)


def _rel_bucket_np(dist):
    n = np.maximum(dist, 0)
    exact = REL_BUCKETS // 2
    nf = np.maximum(n, 1).astype(np.float32)
    large = exact + (np.log(nf / np.float32(exact)) / np.float32(math.log(REL_MAX_DIST / exact))
                     * np.float32(REL_BUCKETS - exact)).astype(np.int32)
    return np.where(n < exact, n, np.minimum(large, REL_BUCKETS - 1)).astype(np.int32)


def _bucket_tile(tile, block_offset, dil, max_rel):
    rel = block_offset * tile + np.arange(tile)[:, None] - np.arange(tile)[None, :]
    ok = (rel >= 0) & (rel <= max_rel)
    return np.where(ok, _rel_bucket_np(rel * dil), -1).astype(np.int32)


def _bias_tiles_kernel(tbl_ref, idx_ref, o_ref, *, head0):
    h = pl.program_id(0) + head0
    idx = idx_ref[...]
    acc = jnp.full(idx.shape, NEG_INF, F32)
    for b in range(REL_BUCKETS):
        acc = jnp.where(idx == b, tbl_ref[b, h], acc)
    o_ref[0] = acc


def _bias_tiles(rel_bias, idx, head0, n_heads):
    r, c = idx.shape
    return pl.pallas_call(
        functools.partial(_bias_tiles_kernel, head0=head0),
        out_shape=jax.ShapeDtypeStruct((n_heads, r, c), F32),
        grid=(n_heads,),
        in_specs=[pl.BlockSpec(memory_space=pltpu.SMEM),
                  pl.BlockSpec((r, c), lambda h: (0, 0))],
        out_specs=pl.BlockSpec((1, r, c), lambda h: (h, 0, 0)),
        compiler_params=_params(1),
        name="bias_tiles",
    )(rel_bias, jnp.asarray(idx))


def _rmsnorm_rows(x, g):
    return x * lax.rsqrt(jnp.mean(x * x, axis=-1, keepdims=True) + RMS_EPS) * g


def _proj_in_kernel(x_ref, g_ref, w_ref, pa_ref, pb_ref, pcmp_ref, pc_ref, mg_ref, bg_ref):
    xn = _rmsnorm_rows(x_ref[...], g_ref[...]).astype(BF16)

    def mm(c0, c1):
        return jnp.dot(xn, w_ref[:, c0:c1], preferred_element_type=F32)

    step = 384
    for c in range(0, _A_COLS, step):
        pa_ref[:, c:c + step] = mm(_OFF_A + c, _OFF_A + c + step).astype(BF16)
    pb_ref[:, 0:384] = mm(_OFF_PB, _OFF_PB + 384).astype(BF16)
    pb_ref[:, 384:896] = mm(_OFF_PB + 384, _OFF_PB + 896).astype(BF16)
    cmp = mm(_OFF_CMP, _OFF_CMP + _CMP_COLS).astype(BF16)
    for i in range(4):
        pcmp_ref[i] = cmp[:, i * HEAD_DIM:(i + 1) * HEAD_DIM]
    for c in range(0, _C_COLS, step):
        pc_ref[:, c:c + step] = mm(_OFF_C + c, _OFF_C + c + step).astype(BF16)
    for c in range(0, _MG_COLS, 512):
        mg_ref[:, c:c + 512] = mm(_OFF_MG + c, _OFF_MG + c + 512)
    bg_ref[...] = mm(_OFF_BG, _OFF_BG + _BG_COLS)


def _proj_in(x2, g, w):
    t = x2.shape[0]
    tm = ROW_TILE
    row = lambda i: (i, 0)
    return pl.pallas_call(
        _proj_in_kernel,
        out_shape=(jax.ShapeDtypeStruct((t, _A_COLS), BF16),
                   jax.ShapeDtypeStruct((t, _PB_COLS), BF16),
                   jax.ShapeDtypeStruct((4, t, HEAD_DIM), BF16),
                   jax.ShapeDtypeStruct((t, _C_COLS), BF16),
                   jax.ShapeDtypeStruct((t, _MG_COLS), F32),
                   jax.ShapeDtypeStruct((t, _BG_COLS), F32)),
        grid=(t // tm,),
        in_specs=[pl.BlockSpec((tm, D_MODEL), row),
                  pl.BlockSpec((1, D_MODEL), lambda i: (0, 0)),
                  pl.BlockSpec((D_MODEL, _W_COLS), lambda i: (0, 0))],
        out_specs=(pl.BlockSpec((tm, _A_COLS), row),
                   pl.BlockSpec((tm, _PB_COLS), row),
                   pl.BlockSpec((4, tm, HEAD_DIM), lambda i: (0, i, 0)),
                   pl.BlockSpec((tm, _C_COLS), row),
                   pl.BlockSpec((tm, _MG_COLS), row),
                   pl.BlockSpec((tm, _BG_COLS), row)),
        compiler_params=_params(1),
        name="proj_in",
    )(x2, g, w)


def _prep_w_in(w):
    a = A_HEADS * HEAD_DIM * 3
    bq0 = a
    bkc0 = bq0 + B_HEADS * HEAD_DIM
    bks0 = bkc0 + 2 * B_KV_HEADS * HEAD_DIM
    bg0 = bks0 + 4 * B_KV_HEADS * HEAD_DIM
    c0 = bg0 + B_HEADS * 3
    mg0 = c0 + _C_COLS
    parts = [w[:, :a], w[:, bq0:bkc0], w[:, bks0:bg0], w[:, bkc0:bks0], w[:, c0:mg0],
             w[:, mg0:mg0 + _MG_COLS], w[:, bg0:c0],
             jnp.zeros((w.shape[0], _BG_COLS - B_HEADS * 3), w.dtype)]
    return jnp.concatenate(parts, axis=1).astype(BF16)


def _band_a_kernel(q_ref, k_ref, v_ref, bias_ref, o_ref, lse_ref, *, seq):
    n_blocks = seq // A_TILE

    def block(r0, first):
        outs, lses = [], []
        for h in range(A_HEADS_PER_GROUP):
            cols = slice(h * HEAD_DIM, (h + 1) * HEAD_DIM)
            q = q_ref[0, pl.ds(r0, A_TILE), cols] * ATTN_SCALE
            if first:
                k = k_ref[0, pl.ds(r0, A_TILE), cols]
                v = v_ref[0, pl.ds(r0, A_TILE), cols]
                bias = bias_ref[h, :, A_TILE:]
            else:
                k = k_ref[0, pl.ds(r0 - A_TILE, 2 * A_TILE), cols]
                v = v_ref[0, pl.ds(r0 - A_TILE, 2 * A_TILE), cols]
                bias = bias_ref[h]
            s = lax.dot_general(q, k, (((1,), (1,)), ((), ())), preferred_element_type=F32) + bias
            m = jnp.max(s, axis=-1, keepdims=True)
            e = jnp.exp(s - m)
            l = jnp.sum(e, axis=-1, keepdims=True)
            o = jnp.dot(e.astype(BF16), v, preferred_element_type=F32) / l
            outs.append(o)
            lses.append(jnp.broadcast_to(m + jnp.log(l), (A_TILE, HEAD_DIM)))
        o_ref[0, pl.ds(r0, A_TILE), :] = jnp.concatenate(outs, axis=-1)
        lse_ref[0, pl.ds(r0, A_TILE), :] = jnp.concatenate(lses, axis=-1)

    block(0, True)
    if n_blocks > 1:
        def body(i, carry):
            block(pl.multiple_of(i * A_TILE, A_TILE), False)
            return carry
        lax.fori_loop(1, n_blocks, body, 0)


def _band_a(q, k, v, bias):
    n, seq, _ = q.shape
    blk = pl.BlockSpec((1, seq, LANE), lambda i: (i, 0, 0))
    return pl.pallas_call(
        functools.partial(_band_a_kernel, seq=seq),
        out_shape=(jax.ShapeDtypeStruct((n, seq, LANE), F32),
                   jax.ShapeDtypeStruct((n, seq, LANE), F32)),
        grid=(n,),
        in_specs=[blk, blk, blk, pl.BlockSpec(bias.shape, lambda i: (0, 0, 0))],
        out_specs=(blk, blk),
        compiler_params=_params(1),
        name="band_a",
    )(q, k, v, bias)


def _compress_kernel(r_ref, pek_ref, pev_ref, w1k_ref, w1v_ref, w2k_ref, w2v_ref, o_ref):
    half = NSA_CMP_STRIDE * HEAD_DIM
    for kv, (pe_ref, w1_ref, w2_ref) in enumerate(((pek_ref, w1k_ref, w2k_ref),
                                                   (pev_ref, w1v_ref, w2v_ref))):
        for h in range(B_KV_HEADS):
            r = r_ref[kv * B_KV_HEADS + h, 0].astype(F32)
            lo = jnp.dot((r + pe_ref[:, :half]).astype(BF16), w1_ref[:half, :],
                         preferred_element_type=F32)
            hi = jnp.dot((r + pe_ref[:, half:]).astype(BF16), w1_ref[half:, :],
                         preferred_element_type=F32)
            hid = lo + pltpu.roll(hi, hi.shape[0] - 1, axis=0)
            act = jax.nn.gelu(hid).astype(BF16)
            o_ref[0, kv * B_KV_HEADS + h] = jnp.dot(act, w2_ref[...],
                                                    preferred_element_type=F32).astype(BF16)


def _compress(r, pe_k, pe_v, w1_k, w1_v, w2_k, w2_v):
    _, b, m, c = r.shape
    const2 = lambda i: (0, 0)
    return pl.pallas_call(
        _compress_kernel,
        out_shape=jax.ShapeDtypeStruct((b, 4, m, HEAD_DIM), BF16),
        grid=(b,),
        in_specs=[pl.BlockSpec((4, 1, m, c), lambda i: (0, i, 0, 0)),
                  pl.BlockSpec(pe_k.shape, const2), pl.BlockSpec(pe_v.shape, const2),
                  pl.BlockSpec(w1_k.shape, const2), pl.BlockSpec(w1_v.shape, const2),
                  pl.BlockSpec(w2_k.shape, const2), pl.BlockSpec(w2_v.shape, const2)],
        out_specs=pl.BlockSpec((1, 4, m, HEAD_DIM), lambda i: (i, 0, 0, 0)),
        compiler_params=_params(1),
        name="nsa_compress",
    )(r, pe_k, pe_v, w1_k, w1_v, w2_k, w2_v)


def _softmax_init(m_ref, l_ref, acc_ref):
    m_ref[...] = jnp.full(m_ref.shape, NEG_INF, F32)
    l_ref[...] = jnp.zeros(l_ref.shape, F32)
    acc_ref[...] = jnp.zeros(acc_ref.shape, F32)


def _softmax_update(s, v, m_ref, l_ref, acc_ref):
    m_prev = m_ref[...]
    m_new = jnp.maximum(m_prev, jnp.max(s, axis=-1, keepdims=True))
    a = jnp.exp(m_prev - m_new)
    e = jnp.exp(s - m_new)
    l_ref[...] = a * l_ref[...] + jnp.sum(e, axis=-1, keepdims=True)
    acc_ref[...] = a * acc_ref[...] + jnp.dot(e.astype(BF16), v, preferred_element_type=F32)
    m_ref[...] = m_new


def _top_mask(score, n_top):
    n = score.shape[-1]
    lane = lax.broadcasted_iota(jnp.int32, score.shape, 1)
    rank = jnp.zeros(score.shape, F32)
    for j in range(n):
        col = score[:, j:j + 1]
        beats = jnp.where(col > score, 1.0, jnp.where((col == score) & (lane > j), 1.0, 0.0))
        rank = rank + beats
    return rank < n_top


def _qk(q, k):
    return lax.dot_general(q, k, (((1,), (1,)), ((), ())), preferred_element_type=F32)


_N_SLC = 32
_SLC_PER_TILE = T_TILE // NSA_SLC_BLOCK
_G_ROWS = B_GROUP * T_TILE
_B_FAR, _B_2, _B_1, _B_0 = 0, T_TILE, 2 * T_TILE, 3 * T_TILE


def _nsa_kernel(pb_ref, cmp_ref, bg_ref, bias_ref, o_ref,
                m_ref, l_ref, acc_ref, ocmp_ref, oslc_ref, *, seq):
    n_tiles = seq // T_TILE
    n_cmp_rows = seq // NSA_CMP_STRIDE
    q_cols = B_HEADS * HEAD_DIM
    ks0, vs0, kw0, vw0 = (q_cols + i * B_KV_HEADS * HEAD_DIM for i in range(4))

    c_id = lax.broadcasted_iota(jnp.int32, (n_cmp_rows, _N_SLC), 0)
    j_id = lax.broadcasted_iota(jnp.int32, (n_cmp_rows, _N_SLC), 1)
    overlap = ((c_id * NSA_CMP_STRIDE < (j_id + 1) * NSA_SLC_BLOCK)
               & (c_id * NSA_CMP_STRIDE + NSA_CMP_BLOCK > j_id * NSA_SLC_BLOCK)).astype(F32)

    def tile_body(i, carry):
        r0 = pl.multiple_of(i * T_TILE, T_TILE)
        rows = pl.ds(r0, T_TILE)
        sig = jax.nn.sigmoid(bg_ref[0, rows, :])
        t_pos = r0 + lax.broadcasted_iota(jnp.int32, (T_TILE, 1), 0)
        t_pos3 = r0 + lax.broadcasted_iota(jnp.int32, (_G_ROWS, 1), 0) % T_TILE

        for h in range(B_KV_HEADS):
            q3 = jnp.concatenate(
                [pb_ref[0, rows, (h * B_GROUP + g) * HEAD_DIM:(h * B_GROUP + g + 1) * HEAD_DIM]
                 for g in range(B_GROUP)], axis=0) * ATTN_SCALE
            kcols = slice(h * HEAD_DIM, (h + 1) * HEAD_DIM)

            kc = cmp_ref[0, h]
            vc = cmp_ref[0, B_KV_HEADS + h]
            c_end = (lax.broadcasted_iota(jnp.int32, (_G_ROWS, n_cmp_rows), 1) * NSA_CMP_STRIDE
                     + NSA_CMP_BLOCK - 1)
            ok = (c_end <= t_pos3) & (c_end < seq)
            lc = jnp.where(ok, _qk(q3, kc), NEG_INF)
            ec = jnp.exp(lc - jnp.max(lc, axis=-1, keepdims=True))
            pc = ec / jnp.sum(ec, axis=-1, keepdims=True) * ok.astype(F32)
            ocmp_ref[...] = jnp.dot(pc.astype(BF16), vc, preferred_element_type=F32)
            p_sum = pc[0:T_TILE] + pc[T_TILE:2 * T_TILE] + pc[2 * T_TILE:3 * T_TILE]
            imp = jnp.dot(p_sum, overlap, precision=lax.Precision.HIGHEST,
                          preferred_element_type=F32)

            j_lane = lax.broadcasted_iota(jnp.int32, (T_TILE, _N_SLC), 1)
            jt = t_pos // NSA_SLC_BLOCK
            forced = (j_lane == 0) | (j_lane == jt) | (j_lane == jt - 1)
            valid = j_lane <= jt
            score = jnp.where(valid, imp + FORCE_BONUS * forced.astype(F32), NEG_INF)
            sel = (valid & _top_mask(score, NSA_SLC_TOPK)).astype(BF16)
            sel3 = jnp.concatenate([sel] * B_GROUP, axis=0)

            def sel_tile(j, bias_col):
                kr = pl.ds(pl.multiple_of(j * T_TILE, T_TILE), T_TILE)
                k = pb_ref[0, kr, ks0 + h * HEAD_DIM:ks0 + (h + 1) * HEAD_DIM]
                v = pb_ref[0, kr, vs0 + h * HEAD_DIM:vs0 + (h + 1) * HEAD_DIM]
                blk = (lax.broadcasted_iota(jnp.int32, (_N_SLC, T_TILE), 1) // NSA_SLC_BLOCK
                       + j * _SLC_PER_TILE)
                expand = (blk == lax.broadcasted_iota(jnp.int32, (_N_SLC, T_TILE), 0)).astype(BF16)
                picked = jnp.dot(sel3, expand, preferred_element_type=F32) > 0.5
                s = _qk(q3, k) + bias_ref[h, :, bias_col:bias_col + T_TILE]
                _softmax_update(jnp.where(picked, s, NEG_INF), v, m_ref, l_ref, acc_ref)

            _softmax_init(m_ref, l_ref, acc_ref)
            sel_tile(i, _B_0)

            @pl.when(i >= 1)
            def _():
                sel_tile(i - 1, _B_1)

            def far_body(j, c):
                sel_tile(j, _B_FAR)
                return c
            lax.fori_loop(0, i - 1, far_body, 0)
            oslc_ref[...] = acc_ref[...] / l_ref[...]

            def win_tile(j, bias_col):
                kr = pl.ds(pl.multiple_of(j * T_TILE, T_TILE), T_TILE)
                k = pb_ref[0, kr, kw0 + h * HEAD_DIM:kw0 + (h + 1) * HEAD_DIM]
                v = pb_ref[0, kr, vw0 + h * HEAD_DIM:vw0 + (h + 1) * HEAD_DIM]
                s = _qk(q3, k) + bias_ref[h, :, bias_col:bias_col + T_TILE]
                _softmax_update(s, v, m_ref, l_ref, acc_ref)

            _softmax_init(m_ref, l_ref, acc_ref)
            win_tile(i, _B_0)

            @pl.when(i >= 1)
            def _():
                win_tile(i - 1, _B_1)

            @pl.when(i >= 2)
            def _():
                win_tile(i - 2, _B_2)
            owin = acc_ref[...] / l_ref[...]

            for g in range(B_GROUP):
                head = h * B_GROUP + g
                gr = slice(g * T_TILE, (g + 1) * T_TILE)
                o = (sig[:, 3 * head:3 * head + 1] * ocmp_ref[gr, :]
                     + sig[:, 3 * head + 1:3 * head + 2] * oslc_ref[gr, :]
                     + sig[:, 3 * head + 2:3 * head + 3] * owin[gr, :])
                o_ref[0, rows, head * HEAD_DIM:(head + 1) * HEAD_DIM] = o.astype(o_ref.dtype)
        return carry

    lax.fori_loop(0, n_tiles, tile_body, 0)


def _nsa(pb, cmp, bg, bias):
    b, seq, _ = pb.shape
    return pl.pallas_call(
        functools.partial(_nsa_kernel, seq=seq),
        out_shape=jax.ShapeDtypeStruct((b, seq, B_HEADS * HEAD_DIM), BF16),
        grid=(b,),
        in_specs=[pl.BlockSpec((1, seq, _PB_COLS), lambda i: (i, 0, 0)),
                  pl.BlockSpec((1,) + cmp.shape[1:], lambda i: (i, 0, 0, 0)),
                  pl.BlockSpec((1, seq, _BG_COLS), lambda i: (i, 0, 0)),
                  pl.BlockSpec(bias.shape, lambda i: (0, 0, 0))],
        out_specs=pl.BlockSpec((1, seq, B_HEADS * HEAD_DIM), lambda i: (i, 0, 0)),
        scratch_shapes=[pltpu.VMEM((_G_ROWS, 1), F32), pltpu.VMEM((_G_ROWS, 1), F32),
                        pltpu.VMEM((_G_ROWS, HEAD_DIM), F32),
                        pltpu.VMEM((_G_ROWS, HEAD_DIM), F32),
                        pltpu.VMEM((_G_ROWS, HEAD_DIM), F32)],
        compiler_params=_params(1),
        name="nsa",
    )(pb, cmp, bg, bias)


def _moba_kernel(pc_ref, bias_ref, o_ref, m_ref, l_ref, acc_ref, *, seq):
    n_blk = seq // MOBA_BLOCK
    width = C_HEADS * HEAD_DIM

    for h in range(C_HEADS):
        qc = slice(h * HEAD_DIM, (h + 1) * HEAD_DIM)
        kc = slice(width + h * HEAD_DIM, width + (h + 1) * HEAD_DIM)
        vc = slice(2 * width + h * HEAD_DIM, 2 * width + (h + 1) * HEAD_DIM)
        kmean = jnp.concatenate(
            [jnp.mean(pc_ref[0, n * MOBA_BLOCK:(n + 1) * MOBA_BLOCK, kc].astype(F32), axis=0,
                      keepdims=True) for n in range(n_blk)], axis=0)

        def tile_body(i, carry):
            rows = pl.ds(pl.multiple_of(i * MOBA_BLOCK, MOBA_BLOCK), MOBA_BLOCK)
            q = pc_ref[0, rows, qc]
            gate = lax.dot_general(q.astype(F32), kmean, (((1,), (1,)), ((), ())),
                                   precision=lax.Precision.HIGHEST, preferred_element_type=F32)
            n_lane = lax.broadcasted_iota(jnp.int32, (MOBA_BLOCK, n_blk), 1)
            past = n_lane < i
            score = jnp.where(past, gate, NEG_INF)
            sel = (past & _top_mask(score, MOBA_TOPK)).astype(F32)
            qs = q * ATTN_SCALE

            def key_tile(j, bias_col, picked):
                kr = pl.ds(pl.multiple_of(j * MOBA_BLOCK, MOBA_BLOCK), MOBA_BLOCK)
                s = _qk(qs, pc_ref[0, kr, kc]) + bias_ref[h, :, bias_col:bias_col + MOBA_BLOCK]
                if picked is not None:
                    s = jnp.where(picked, s, NEG_INF)
                _softmax_update(s, pc_ref[0, kr, vc], m_ref, l_ref, acc_ref)

            def picked_col(j):
                return jnp.sum(jnp.where(n_lane == j, sel, 0.0), axis=-1, keepdims=True) > 0.5

            _softmax_init(m_ref, l_ref, acc_ref)
            key_tile(i, 2 * MOBA_BLOCK, None)

            @pl.when(i >= 1)
            def _():
                key_tile(i - 1, MOBA_BLOCK, picked_col(i - 1))

            def far_body(j, c):
                key_tile(j, 0, picked_col(j))
                return c
            lax.fori_loop(0, i - 1, far_body, 0)
            o_ref[0, rows, qc] = (acc_ref[...] / l_ref[...]).astype(o_ref.dtype)
            return carry

        lax.fori_loop(0, n_blk, tile_body, 0)


def _moba(pc, bias):
    b, seq, _ = pc.shape
    width = C_HEADS * HEAD_DIM
    return pl.pallas_call(
        functools.partial(_moba_kernel, seq=seq),
        out_shape=jax.ShapeDtypeStruct((b, seq, width), BF16),
        grid=(b,),
        in_specs=[pl.BlockSpec((1, seq, _C_COLS), lambda i: (i, 0, 0)),
                  pl.BlockSpec(bias.shape, lambda i: (0, 0, 0))],
        out_specs=pl.BlockSpec((1, seq, width), lambda i: (i, 0, 0)),
        scratch_shapes=[pltpu.VMEM((MOBA_BLOCK, 1), F32), pltpu.VMEM((MOBA_BLOCK, 1), F32),
                        pltpu.VMEM((MOBA_BLOCK, HEAD_DIM), F32)],
        compiler_params=_params(1),
        name="moba",
    )(pc, bias)


def _merge_kernel(oa_ref, lse_ref, ob_ref, oc_ref, mg_ref, x_ref, wb_ref, wo_ref, y_ref):
    n_groups = len(A_GROUPS)
    gw = A_HEADS_PER_GROUP * HEAD_DIM
    lse = [lse_ref[g] for g in range(n_groups)]
    top = functools.reduce(jnp.maximum, lse)
    ex = [jnp.exp(v - top) for v in lse]
    den = functools.reduce(lambda a, b: a + b, ex)
    oa = jnp.concatenate([(ex[g] / den * oa_ref[g]).astype(BF16) for g in range(n_groups)], axis=-1)
    r0 = n_groups * gw
    r1 = r0 + B_HEADS * HEAD_DIM
    ya = jnp.dot(oa, wb_ref[0:r0, :], preferred_element_type=F32)
    yb = jnp.dot(ob_ref[...], wb_ref[r0:r1, :], preferred_element_type=F32)
    yc = jnp.dot(oc_ref[...], wb_ref[r1:, :], preferred_element_type=F32)
    merged = (jax.nn.sigmoid(mg_ref[:, 0:D_MODEL]) * ya
              + jax.nn.sigmoid(mg_ref[:, D_MODEL:2 * D_MODEL]) * yb
              + jax.nn.sigmoid(mg_ref[:, 2 * D_MODEL:]) * yc)
    y_ref[...] = x_ref[...] + jnp.dot(merged.astype(BF16), wo_ref[...], preferred_element_type=F32)


def _merge(oa, lse, ob, oc, mg, x2, wb, wo):
    t = x2.shape[0]
    tm = ROW_TILE
    row = lambda i: (i, 0)
    grp = pl.BlockSpec((len(A_GROUPS), tm, LANE), lambda i: (0, i, 0))
    return pl.pallas_call(
        _merge_kernel,
        out_shape=jax.ShapeDtypeStruct((t, D_MODEL), F32),
        grid=(t // tm,),
        in_specs=[grp, grp,
                  pl.BlockSpec((tm, ob.shape[1]), row), pl.BlockSpec((tm, oc.shape[1]), row),
                  pl.BlockSpec((tm, _MG_COLS), row), pl.BlockSpec((tm, D_MODEL), row),
                  pl.BlockSpec(wb.shape, lambda i: (0, 0)), pl.BlockSpec(wo.shape, lambda i: (0, 0))],
        out_specs=pl.BlockSpec((tm, D_MODEL), row),
        compiler_params=_params(1),
        name="merge",
    )(oa, lse, ob, oc, mg, x2, wb, wo)


_FF_CHUNK = D_FF // 2


def _ffn_kernel(x_ref, halo_ref, g_ref, wu_ref, cw_ref, cb_ref, wd_ref, gf_ref, y_ref, xn_ref,
                *, tiles_per_seq, final_norm):
    i = pl.program_id(0)
    x = x_ref[...]
    g = g_ref[...]
    keep = (i % tiles_per_seq != 0).astype(F32)
    xn_ref[0:FFN_HALO, :] = (_rmsnorm_rows(halo_ref[...], g) * keep).astype(BF16)
    xn_ref[FFN_HALO:, :] = _rmsnorm_rows(x, g).astype(BF16)
    xn = xn_ref[...]
    rows = xn.shape[0]

    def conv(c0):
        hcol = jnp.dot(xn, wu_ref[:, c0:c0 + _FF_CHUNK], preferred_element_type=F32)
        out = (cw_ref[2:3, c0:c0 + _FF_CHUNK] * hcol
               + cw_ref[1:2, c0:c0 + _FF_CHUNK] * pltpu.roll(hcol, 1, axis=0)
               + cw_ref[0:1, c0:c0 + _FF_CHUNK] * pltpu.roll(hcol, 2, axis=0)
               + cb_ref[:, c0:c0 + _FF_CHUNK])
        return out[FFN_HALO:rows]

    acc = x
    for c in range(0, D_FF, _FF_CHUNK):
        a = conv(c)
        u = conv(D_FF + c)
        act = (a * jax.nn.sigmoid(a) * u).astype(BF16)
        acc = acc + jnp.dot(act, wd_ref[c:c + _FF_CHUNK, :], preferred_element_type=F32)
    if final_norm:
        acc = _rmsnorm_rows(acc, gf_ref[...])
    y_ref[...] = acc


def _ffn(x2, g, wu, cw, cb, wd, gf, *, seq, final_norm):
    t = x2.shape[0]
    tm = ROW_TILE
    const = lambda i: (0, 0)
    halo_blocks = tm // FFN_HALO
    return pl.pallas_call(
        functools.partial(_ffn_kernel, tiles_per_seq=seq // tm, final_norm=final_norm),
        out_shape=jax.ShapeDtypeStruct((t, D_MODEL), F32),
        grid=(t // tm,),
        in_specs=[pl.BlockSpec((tm, D_MODEL), lambda i: (i, 0)),
                  pl.BlockSpec((FFN_HALO, D_MODEL), lambda i: (jnp.maximum(i * halo_blocks - 1, 0), 0)),
                  pl.BlockSpec((1, D_MODEL), const),
                  pl.BlockSpec(wu.shape, const), pl.BlockSpec(cw.shape, const),
                  pl.BlockSpec(cb.shape, const), pl.BlockSpec(wd.shape, const),
                  pl.BlockSpec((1, D_MODEL), const)],
        out_specs=pl.BlockSpec((tm, D_MODEL), lambda i: (i, 0)),
        scratch_shapes=[pltpu.VMEM((tm + FFN_HALO, D_MODEL), BF16)],
        compiler_params=_params(1),
        name="conv_ffn",
    )(x2, x2, g, wu, cw, cb, wd, gf)


def _to_sub(a, dil):
    b, s, c = a.shape
    return a.reshape(b, s // dil, dil, c).swapaxes(1, 2).reshape(b * dil, s // dil, c)


def _from_sub(a, dil, b):
    n, l, c = a.shape
    return a.reshape(b, dil, l, c).swapaxes(1, 2).reshape(b * dil * l, c)


def kernel(x, rel_bias, norm_mix, w_in, cmp_pe_k, cmp_w1_k, cmp_w2_k, cmp_pe_v, cmp_w1_v, cmp_w2_v,
           w_branch, w_out, norm_ffn, w_up, conv_w, conv_b, w_down, norm_final):
    b, s, d = x.shape
    depth = w_in.shape[0]
    t = b * s
    assert d == D_MODEL and s % T_TILE == 0 and s // NSA_SLC_BLOCK == _N_SLC and t % ROW_TILE == 0
    assert s % ROW_TILE == 0 and all(s % (dil * A_TILE) == 0 for _, dil in A_GROUPS)
    assert all(win // dil == A_TILE for win, dil in A_GROUPS)

    hg = A_HEADS_PER_GROUP
    bias_a = [
        _bias_tiles(rel_bias,
                    np.concatenate([_bucket_tile(A_TILE, 1, dil, win // dil),
                                    _bucket_tile(A_TILE, 0, dil, win // dil)], axis=1),
                    gi * hg, hg)
        for gi, (win, dil) in enumerate(A_GROUPS)]
    far = np.full((T_TILE, T_TILE), REL_BUCKETS - 1, np.int32)
    idx_b = np.concatenate([far] + [_bucket_tile(T_TILE, off, 1, NSA_WINDOW - 1) for off in (2, 1, 0)],
                           axis=1)
    bias_b = _bias_tiles(rel_bias, idx_b, A_HEADS, B_HEADS).reshape(B_KV_HEADS, _G_ROWS, 4 * T_TILE)
    idx_c = np.concatenate([far] + [_bucket_tile(T_TILE, off, 1, s) for off in (1, 0)], axis=1)
    bias_c = _bias_tiles(rel_bias, idx_c, A_HEADS + B_HEADS, C_HEADS)

    x2 = x.reshape(t, d)
    for i in range(depth):
        pa, pb, pcmp, pc, mg, bg = _proj_in(x2, norm_mix[i][None, :], _prep_w_in(w_in[i]))

        pa3 = pa.reshape(b, s, _A_COLS)
        oa, lse = [], []
        for gi, (win, dil) in enumerate(A_GROUPS):
            qkv = [_to_sub(pa3[:, :, (j * A_HEADS + gi * hg) * HEAD_DIM:
                               (j * A_HEADS + (gi + 1) * hg) * HEAD_DIM], dil) for j in range(3)]
            o_g, lse_g = _band_a(*qkv, bias_a[gi])
            oa.append(_from_sub(o_g, dil, b))
            lse.append(_from_sub(lse_g, dil, b))

        cmp = _compress(pcmp.reshape(4, b, s // NSA_CMP_STRIDE, NSA_CMP_STRIDE * HEAD_DIM),
                        cmp_pe_k[i].reshape(1, -1), cmp_pe_v[i].reshape(1, -1),
                        cmp_w1_k[i].astype(BF16), cmp_w1_v[i].astype(BF16),
                        cmp_w2_k[i].astype(BF16), cmp_w2_v[i].astype(BF16))
        ob = _nsa(pb.reshape(b, s, _PB_COLS), cmp, bg.reshape(b, s, _BG_COLS), bias_b)

        oc = _moba(pc.reshape(b, s, _C_COLS), bias_c)

        x2 = _merge(jnp.stack(oa), jnp.stack(lse), ob.reshape(t, -1), oc.reshape(t, -1), mg, x2,
                    w_branch[i].astype(BF16), w_out[i].astype(BF16))
        x2 = _ffn(x2, norm_ffn[i][None, :], w_up[i].astype(BF16), conv_w[i], conv_b[i][None, :],
                  w_down[i].astype(BF16), norm_final[None, :], seq=s, final_norm=(i == depth - 1))
    return x2.reshape(b, s, d)
```

```python
import functools
import math

import jax
import jax.numpy as jnp
import numpy as np
from jax import lax
from jax.experimental import pallas as pl
from jax.experimental.pallas import tpu as pltpu

F32 = jnp.float32
BF16 = jnp.bfloat16

D_MODEL = 1024
HEAD_DIM = 64
A_GROUPS = ((128, 1), (512, 4), (2048, 16))
A_HEADS_PER_GROUP = 2
A_HEADS = 6
B_HEADS = 6
B_KV_HEADS = 2
B_GROUP = 3
C_HEADS = 4
NSA_CMP_BLOCK = 32
NSA_CMP_STRIDE = 16
NSA_CMP_HIDDEN = 256
NSA_SLC_BLOCK = 64
NSA_SLC_TOPK = 16
NSA_WINDOW = 512
MOBA_BLOCK = 256
MOBA_TOPK = 3
REL_BUCKETS = 32
REL_MAX_DIST = 128
D_FF = 2816
RMS_EPS = 1e-6
NEG_INF = -1e30
FORCE_BONUS = 1e4
ATTN_SCALE = HEAD_DIM ** -0.5

LANE = 128
A_TILE = 128
T_TILE = 256
ROW_TILE = 512
FFN_HALO = 8
VMEM_LIMIT = 56 * 1024 * 1024

_A_COLS = 3 * A_HEADS * HEAD_DIM
_PB_COLS = B_HEADS * HEAD_DIM + 4 * B_KV_HEADS * HEAD_DIM
_CMP_COLS = 2 * B_KV_HEADS * HEAD_DIM
_C_COLS = 3 * C_HEADS * HEAD_DIM
_MG_COLS = 3 * D_MODEL
_BG_COLS = LANE
_OFF_A = 0
_OFF_PB = _OFF_A + _A_COLS
_OFF_CMP = _OFF_PB + _PB_COLS
_OFF_C = _OFF_CMP + _CMP_COLS
_OFF_MG = _OFF_C + _C_COLS
_OFF_BG = _OFF_MG + _MG_COLS
_W_COLS = _OFF_BG + _BG_COLS


def _params(n_grid):
    return pltpu.CompilerParams(dimension_semantics=("arbitrary",) * n_grid,
                                vmem_limit_bytes=VMEM_LIMIT)


def _rel_bucket_np(dist):
    n = np.maximum(dist, 0)
    exact = REL_BUCKETS // 2
    nf = np.maximum(n, 1).astype(np.float32)
    large = exact + (np.log(nf / np.float32(exact)) / np.float32(math.log(REL_MAX_DIST / exact))
                     * np.float32(REL_BUCKETS - exact)).astype(np.int32)
    return np.where(n < exact, n, np.minimum(large, REL_BUCKETS - 1)).astype(np.int32)


def _bucket_tile(tile, block_offset, dil, max_rel):
    rel = block_offset * tile + np.arange(tile)[:, None] - np.arange(tile)[None, :]
    ok = (rel >= 0) & (rel <= max_rel)
    return np.where(ok, _rel_bucket_np(rel * dil), -1).astype(np.int32)


def _bias_tiles_kernel(tbl_ref, idx_ref, o_ref, *, head0, relative):
    h = pl.program_id(0) + head0
    idx = idx_ref[...]
    acc = jnp.full(idx.shape, NEG_INF, F32)
    for b in range(REL_BUCKETS):
        acc = jnp.where(idx == b, tbl_ref[b, h], acc)
    if relative:
        acc = acc - tbl_ref[REL_BUCKETS - 1, h]
    o_ref[0] = acc


def _bias_tiles(rel_bias, idx, head0, n_heads, relative=False):
    r, c = idx.shape
    return pl.pallas_call(
        functools.partial(_bias_tiles_kernel, head0=head0, relative=relative),
        out_shape=jax.ShapeDtypeStruct((n_heads, r, c), F32),
        grid=(n_heads,),
        in_specs=[pl.BlockSpec(memory_space=pltpu.SMEM),
                  pl.BlockSpec((r, c), lambda h: (0, 0))],
        out_specs=pl.BlockSpec((1, r, c), lambda h: (h, 0, 0)),
        compiler_params=_params(1),
        name="bias_tiles",
    )(rel_bias, jnp.asarray(idx))


def _rmsnorm_rows(x, g):
    return x * lax.rsqrt(jnp.mean(x * x, axis=-1, keepdims=True) + RMS_EPS) * g


def _proj_in_kernel(x_ref, g_ref, w_ref, pa_ref, pb_ref, pcmp_ref, pc_ref, mg_ref, bg_ref):
    xn = _rmsnorm_rows(x_ref[...], g_ref[...]).astype(BF16)

    def mm(c0, c1):
        return jnp.dot(xn, w_ref[:, c0:c1], preferred_element_type=F32)

    step = 384
    for c in range(0, _A_COLS, step):
        pa_ref[:, c:c + step] = mm(_OFF_A + c, _OFF_A + c + step).astype(BF16)
    pb_ref[:, 0:384] = mm(_OFF_PB, _OFF_PB + 384).astype(BF16)
    pb_ref[:, 384:896] = mm(_OFF_PB + 384, _OFF_PB + 896).astype(BF16)
    cmp = mm(_OFF_CMP, _OFF_CMP + _CMP_COLS).astype(BF16)
    for i in range(4):
        pcmp_ref[i] = cmp[:, i * HEAD_DIM:(i + 1) * HEAD_DIM]
    for c in range(0, _C_COLS, step):
        pc_ref[:, c:c + step] = mm(_OFF_C + c, _OFF_C + c + step).astype(BF16)
    for c in range(0, _MG_COLS, 512):
        mg_ref[:, c:c + 512] = mm(_OFF_MG + c, _OFF_MG + c + 512)
    bg_ref[...] = mm(_OFF_BG, _OFF_BG + _BG_COLS)


def _proj_in(x2, g, w):
    t = x2.shape[0]
    tm = ROW_TILE
    row = lambda i: (i, 0)
    return pl.pallas_call(
        _proj_in_kernel,
        out_shape=(jax.ShapeDtypeStruct((t, _A_COLS), BF16),
                   jax.ShapeDtypeStruct((t, _PB_COLS), BF16),
                   jax.ShapeDtypeStruct((4, t, HEAD_DIM), BF16),
                   jax.ShapeDtypeStruct((t, _C_COLS), BF16),
                   jax.ShapeDtypeStruct((t, _MG_COLS), F32),
                   jax.ShapeDtypeStruct((t, _BG_COLS), F32)),
        grid=(t // tm,),
        in_specs=[pl.BlockSpec((tm, D_MODEL), row),
                  pl.BlockSpec((1, D_MODEL), lambda i: (0, 0)),
                  pl.BlockSpec((D_MODEL, _W_COLS), lambda i: (0, 0))],
        out_specs=(pl.BlockSpec((tm, _A_COLS), row),
                   pl.BlockSpec((tm, _PB_COLS), row),
                   pl.BlockSpec((4, tm, HEAD_DIM), lambda i: (0, i, 0)),
                   pl.BlockSpec((tm, _C_COLS), row),
                   pl.BlockSpec((tm, _MG_COLS), row),
                   pl.BlockSpec((tm, _BG_COLS), row)),
        compiler_params=_params(1),
        name="proj_in",
    )(x2, g, w)


def _prep_w_in(w):
    a = A_HEADS * HEAD_DIM * 3
    bq0 = a
    bkc0 = bq0 + B_HEADS * HEAD_DIM
    bks0 = bkc0 + 2 * B_KV_HEADS * HEAD_DIM
    bg0 = bks0 + 4 * B_KV_HEADS * HEAD_DIM
    c0 = bg0 + B_HEADS * 3
    mg0 = c0 + _C_COLS
    parts = [w[:, :a], w[:, bq0:bkc0], w[:, bks0:bg0], w[:, bkc0:bks0], w[:, c0:mg0],
             w[:, mg0:mg0 + _MG_COLS], w[:, bg0:c0],
             jnp.zeros((w.shape[0], _BG_COLS - B_HEADS * 3), w.dtype)]
    return jnp.concatenate(parts, axis=1).astype(BF16)


def _band_a_kernel(q_ref, k_ref, v_ref, bias_ref, o_ref, lse_ref, *, seq):
    n_blocks = seq // A_TILE

    def block(r0, first):
        outs, lses = [], []
        for h in range(A_HEADS_PER_GROUP):
            cols = slice(h * HEAD_DIM, (h + 1) * HEAD_DIM)
            q = q_ref[0, pl.ds(r0, A_TILE), cols] * ATTN_SCALE
            if first:
                k = k_ref[0, pl.ds(r0, A_TILE), cols]
                v = v_ref[0, pl.ds(r0, A_TILE), cols]
                bias = bias_ref[h, :, A_TILE:]
            else:
                k = k_ref[0, pl.ds(r0 - A_TILE, 2 * A_TILE), cols]
                v = v_ref[0, pl.ds(r0 - A_TILE, 2 * A_TILE), cols]
                bias = bias_ref[h]
            s = lax.dot_general(q, k, (((1,), (1,)), ((), ())), preferred_element_type=F32) + bias
            m = jnp.max(s, axis=-1, keepdims=True)
            e = jnp.exp(s - m)
            l = jnp.sum(e, axis=-1, keepdims=True)
            o = jnp.dot(e.astype(BF16), v, preferred_element_type=F32) / l
            outs.append(o)
            lses.append(jnp.broadcast_to(m + jnp.log(l), (A_TILE, HEAD_DIM)))
        o_ref[0, pl.ds(r0, A_TILE), :] = jnp.concatenate(outs, axis=-1)
        lse_ref[0, pl.ds(r0, A_TILE), :] = jnp.concatenate(lses, axis=-1)

    block(0, True)
    if n_blocks > 1:
        def body(i, carry):
            block(pl.multiple_of(i * A_TILE, A_TILE), False)
            return carry
        lax.fori_loop(1, n_blocks, body, 0)


def _band_a(q, k, v, bias):
    n, seq, _ = q.shape
    blk = pl.BlockSpec((1, seq, LANE), lambda i: (i, 0, 0))
    return pl.pallas_call(
        functools.partial(_band_a_kernel, seq=seq),
        out_shape=(jax.ShapeDtypeStruct((n, seq, LANE), F32),
                   jax.ShapeDtypeStruct((n, seq, LANE), F32)),
        grid=(n,),
        in_specs=[blk, blk, blk, pl.BlockSpec(bias.shape, lambda i: (0, 0, 0))],
        out_specs=(blk, blk),
        compiler_params=_params(1),
        name="band_a",
    )(q, k, v, bias)


def _compress_kernel(r_ref, pek_ref, pev_ref, w1k_ref, w1v_ref, w2k_ref, w2v_ref, o_ref):
    half = NSA_CMP_STRIDE * HEAD_DIM
    for kv, (pe_ref, w1_ref, w2_ref) in enumerate(((pek_ref, w1k_ref, w2k_ref),
                                                   (pev_ref, w1v_ref, w2v_ref))):
        for h in range(B_KV_HEADS):
            r = r_ref[kv * B_KV_HEADS + h, 0].astype(F32)
            lo = jnp.dot((r + pe_ref[:, :half]).astype(BF16), w1_ref[:half, :],
                         preferred_element_type=F32)
            hi = jnp.dot((r + pe_ref[:, half:]).astype(BF16), w1_ref[half:, :],
                         preferred_element_type=F32)
            hid = lo + pltpu.roll(hi, hi.shape[0] - 1, axis=0)
            act = jax.nn.gelu(hid).astype(BF16)
            o_ref[0, kv * B_KV_HEADS + h] = jnp.dot(act, w2_ref[...],
                                                    preferred_element_type=F32).astype(BF16)


def _compress(r, pe_k, pe_v, w1_k, w1_v, w2_k, w2_v):
    _, b, m, c = r.shape
    const2 = lambda i: (0, 0)
    return pl.pallas_call(
        _compress_kernel,
        out_shape=jax.ShapeDtypeStruct((b, 4, m, HEAD_DIM), BF16),
        grid=(b,),
        in_specs=[pl.BlockSpec((4, 1, m, c), lambda i: (0, i, 0, 0)),
                  pl.BlockSpec(pe_k.shape, const2), pl.BlockSpec(pe_v.shape, const2),
                  pl.BlockSpec(w1_k.shape, const2), pl.BlockSpec(w1_v.shape, const2),
                  pl.BlockSpec(w2_k.shape, const2), pl.BlockSpec(w2_v.shape, const2)],
        out_specs=pl.BlockSpec((1, 4, m, HEAD_DIM), lambda i: (i, 0, 0, 0)),
        compiler_params=_params(1),
        name="nsa_compress",
    )(r, pe_k, pe_v, w1_k, w1_v, w2_k, w2_v)


MASK_BIG = 2.0 ** 100
SEL_PAD = 32


def _nt(a, b, **kw):
    return lax.dot_general(a, b, (((1,), (1,)), ((), ())), preferred_element_type=F32, **kw)


def _top_mask_t(score, n_rows, n_top):
    sub = lax.broadcasted_iota(jnp.int32, score.shape, 0)
    rank = jnp.zeros(score.shape, F32)
    for j in range(n_rows):
        row = score[j:j + 1, :]
        ge = jnp.where(row >= score, 1.0, 0.0)
        gt = jnp.where(row > score, 1.0, 0.0)
        rank = rank + jnp.where(sub > j, ge, gt)
    return rank < n_top


def _two_pass_attention(i, lhs, k_tile, v_ref, bias_tile, s_ref, rm_ref, mb_ref, acc_ref, *, band):
    def logits(j, d):
        s = _nt(lhs, k_tile(pl.ds(pl.multiple_of(j * T_TILE, T_TILE), T_TILE)))
        if d is not None:
            s = s + bias_tile(d)
        s_ref[j] = s
        return s

    rm_ref[...] = logits(i, 0)

    @pl.when(i >= 1)
    def _():
        rm_ref[...] = jnp.maximum(rm_ref[...], logits(i - 1, 1))

    if band:
        @pl.when(i >= 2)
        def _():
            rm_ref[...] = jnp.maximum(rm_ref[...], logits(i - 2, 2))
        lo = jnp.maximum(i - 2, 0)
    else:
        def far(j, c):
            rm_ref[...] = jnp.maximum(rm_ref[...], logits(j, None))
            return c
        lax.fori_loop(0, i - 1, far, 0)
        lo = 0

    mb_ref[...] = jnp.broadcast_to(jnp.max(rm_ref[...], axis=-1, keepdims=True), mb_ref.shape)
    acc_ref[...] = jnp.zeros(acc_ref.shape, F32)

    def pv(j, c):
        e = jnp.exp(s_ref[j] - mb_ref[...])
        kr = pl.ds(pl.multiple_of(j * T_TILE, T_TILE), T_TILE)
        acc_ref[...] += jnp.dot(e.astype(BF16), v_ref[kr, :], preferred_element_type=F32)
        return c
    lax.fori_loop(lo, i + 1, pv, 0)
    acc = acc_ref[...]
    return acc[:, :HEAD_DIM] / acc[:, HEAD_DIM:HEAD_DIM + 1]


def _with_ones_column(v):
    one = (lax.broadcasted_iota(jnp.int32, v.shape, 1) == 0).astype(F32)
    return jnp.concatenate([v.astype(F32), one], axis=1).astype(BF16)


def _with_block_onehot(k, block):
    n = k.shape[0]
    shape = (n, SEL_PAD)
    hot = (lax.broadcasted_iota(jnp.int32, shape, 0) // block
           == lax.broadcasted_iota(jnp.int32, shape, 1)).astype(F32)
    return jnp.concatenate([k.astype(F32), hot, jnp.zeros(shape, F32)], axis=1).astype(BF16)


def _augment_q(q, allowed):
    pen = (allowed - 1.0) * MASK_BIG
    return jnp.concatenate([q.astype(F32), pen, jnp.zeros(pen.shape, F32)], axis=1).astype(BF16)


_N_SLC = 32
_G_ROWS = B_GROUP * T_TILE


def _nsa_kernel(pb_ref, cmp_ref, bg_ref, bias_ref, o_ref,
                kaug_ref, vaug_ref, vwaug_ref, eye_ref, s_ref, rm_ref, mb_ref, acc_ref,
                ocmp_ref, oslc_ref, *, seq):
    n_tiles = seq // T_TILE
    n_cmp_rows = seq // NSA_CMP_STRIDE
    q_cols = B_HEADS * HEAD_DIM
    ks0, vs0, kw0, vw0 = (q_cols + i * B_KV_HEADS * HEAD_DIM for i in range(4))

    def head_cols(c0, h):
        return slice(c0 + h * HEAD_DIM, c0 + (h + 1) * HEAD_DIM)

    j_id = lax.broadcasted_iota(jnp.int32, (_N_SLC, n_cmp_rows), 0)
    c_id = lax.broadcasted_iota(jnp.int32, (_N_SLC, n_cmp_rows), 1)
    overlap_t = ((c_id * NSA_CMP_STRIDE < (j_id + 1) * NSA_SLC_BLOCK)
                 & (c_id * NSA_CMP_STRIDE + NSA_CMP_BLOCK > j_id * NSA_SLC_BLOCK)).astype(F32)
    eye_ref[...] = (lax.broadcasted_iota(jnp.int32, (_G_ROWS, T_TILE), 0) % T_TILE
                    == lax.broadcasted_iota(jnp.int32, (_G_ROWS, T_TILE), 1)).astype(BF16)
    for h in range(B_KV_HEADS):
        kaug_ref[h] = _with_block_onehot(pb_ref[0, :, head_cols(ks0, h)], NSA_SLC_BLOCK)
        vaug_ref[h] = _with_ones_column(pb_ref[0, :, head_cols(vs0, h)])
        vwaug_ref[h] = _with_ones_column(pb_ref[0, :, head_cols(vw0, h)])

    def tile_body(i, carry):
        r0 = pl.multiple_of(i * T_TILE, T_TILE)
        rows = pl.ds(r0, T_TILE)
        sig = jax.nn.sigmoid(bg_ref[0, rows, :])
        t_pos3 = r0 + lax.broadcasted_iota(jnp.int32, (_G_ROWS, 1), 0) % T_TILE

        for h in range(B_KV_HEADS):
            q3 = jnp.concatenate(
                [pb_ref[0, rows, head_cols(0, h * B_GROUP + g)] for g in range(B_GROUP)],
                axis=0) * ATTN_SCALE

            kc = cmp_ref[0, h]
            vc = cmp_ref[0, B_KV_HEADS + h]
            c_end = (lax.broadcasted_iota(jnp.int32, (_G_ROWS, n_cmp_rows), 1) * NSA_CMP_STRIDE
                     + NSA_CMP_BLOCK - 1)
            ok = (c_end <= t_pos3) & (c_end < seq)
            lc = jnp.where(ok, _nt(q3, kc), NEG_INF)
            ec = jnp.exp(lc - jnp.max(lc, axis=-1, keepdims=True))
            pc = ec / jnp.sum(ec, axis=-1, keepdims=True) * ok.astype(F32)
            ocmp_ref[...] = jnp.dot(pc.astype(BF16), vc, preferred_element_type=F32)
            p_sum = pc[0:T_TILE] + pc[T_TILE:2 * T_TILE] + pc[2 * T_TILE:3 * T_TILE]
            imp_t = _nt(overlap_t, p_sum, precision=lax.Precision.HIGHEST)

            j_sub = lax.broadcasted_iota(jnp.int32, (_N_SLC, T_TILE), 0)
            jt = (r0 + lax.broadcasted_iota(jnp.int32, (_N_SLC, T_TILE), 1)) // NSA_SLC_BLOCK
            forced = (j_sub == 0) | (j_sub == jt) | (j_sub == jt - 1)
            valid = j_sub <= jt
            score = jnp.where(valid, imp_t + FORCE_BONUS * forced.astype(F32), NEG_INF)
            sel_t = jnp.where(valid & _top_mask_t(score, _N_SLC, NSA_SLC_TOPK), 1.0, 0.0)
            sel3 = _nt(eye_ref[...], sel_t.astype(BF16))
            qaug = _augment_q(q3, sel3)

            def bias_tile(d):
                c0 = (2 - d) * T_TILE
                return bias_ref[h, :, c0:c0 + T_TILE]

            oslc_ref[...] = _two_pass_attention(
                i, qaug, lambda kr: kaug_ref[h, kr, :], vaug_ref.at[h], bias_tile,
                s_ref, rm_ref, mb_ref, acc_ref, band=False)

            owin = _two_pass_attention(
                i, q3, lambda kr: pb_ref[0, kr, head_cols(kw0, h)], vwaug_ref.at[h], bias_tile,
                s_ref, rm_ref, mb_ref, acc_ref, band=True)

            for g in range(B_GROUP):
                head = h * B_GROUP + g
                gr = slice(g * T_TILE, (g + 1) * T_TILE)
                o = (sig[:, 3 * head:3 * head + 1] * ocmp_ref[gr, :]
                     + sig[:, 3 * head + 1:3 * head + 2] * oslc_ref[gr, :]
                     + sig[:, 3 * head + 2:3 * head + 3] * owin[gr, :])
                o_ref[0, rows, head * HEAD_DIM:(head + 1) * HEAD_DIM] = o.astype(o_ref.dtype)
        return carry

    lax.fori_loop(0, n_tiles, tile_body, 0)


def _nsa(pb, cmp, bg, bias):
    b, seq, _ = pb.shape
    return pl.pallas_call(
        functools.partial(_nsa_kernel, seq=seq),
        out_shape=jax.ShapeDtypeStruct((b, seq, B_HEADS * HEAD_DIM), BF16),
        grid=(b,),
        in_specs=[pl.BlockSpec((1, seq, _PB_COLS), lambda i: (i, 0, 0)),
                  pl.BlockSpec((1,) + cmp.shape[1:], lambda i: (i, 0, 0, 0)),
                  pl.BlockSpec((1, seq, _BG_COLS), lambda i: (i, 0, 0)),
                  pl.BlockSpec(bias.shape, lambda i: (0, 0, 0))],
        out_specs=pl.BlockSpec((1, seq, B_HEADS * HEAD_DIM), lambda i: (i, 0, 0)),
        scratch_shapes=[pltpu.VMEM((B_KV_HEADS, seq, LANE), BF16),
                        pltpu.VMEM((B_KV_HEADS, seq, LANE), BF16),
                        pltpu.VMEM((B_KV_HEADS, seq, LANE), BF16),
                        pltpu.VMEM((_G_ROWS, T_TILE), BF16),
                        pltpu.VMEM((seq // T_TILE, _G_ROWS, T_TILE), F32),
                        pltpu.VMEM((_G_ROWS, T_TILE), F32),
                        pltpu.VMEM((_G_ROWS, T_TILE), F32),
                        pltpu.VMEM((_G_ROWS, LANE), F32),
                        pltpu.VMEM((_G_ROWS, HEAD_DIM), F32),
                        pltpu.VMEM((_G_ROWS, HEAD_DIM), F32)],
        compiler_params=_params(1),
        name="nsa",
    )(pb, cmp, bg, bias)


def _moba_kernel(pc_ref, bias_ref, o_ref, kaug_ref, vaug_ref, eye_ref, s_ref, rm_ref, mb_ref,
                 acc_ref, *, seq):
    n_blk = seq // MOBA_BLOCK
    width = C_HEADS * HEAD_DIM
    eye_ref[...] = (lax.broadcasted_iota(jnp.int32, (T_TILE, T_TILE), 0)
                    == lax.broadcasted_iota(jnp.int32, (T_TILE, T_TILE), 1)).astype(BF16)

    for h in range(C_HEADS):
        qc = slice(h * HEAD_DIM, (h + 1) * HEAD_DIM)
        kc = slice(width + h * HEAD_DIM, width + (h + 1) * HEAD_DIM)
        vc = slice(2 * width + h * HEAD_DIM, 2 * width + (h + 1) * HEAD_DIM)
        kaug_ref[...] = _with_block_onehot(pc_ref[0, :, kc], MOBA_BLOCK)
        vaug_ref[...] = _with_ones_column(pc_ref[0, :, vc])
        kmean = jnp.concatenate(
            [jnp.mean(pc_ref[0, n * MOBA_BLOCK:(n + 1) * MOBA_BLOCK, kc].astype(F32), axis=0,
                      keepdims=True) for n in range(n_blk)]
            + [jnp.zeros((SEL_PAD - n_blk, HEAD_DIM), F32)], axis=0)

        def tile_body(i, carry):
            rows = pl.ds(pl.multiple_of(i * MOBA_BLOCK, MOBA_BLOCK), MOBA_BLOCK)
            q = pc_ref[0, rows, qc]
            gate_t = _nt(kmean, q.astype(F32), precision=lax.Precision.HIGHEST)
            n_sub = lax.broadcasted_iota(jnp.int32, (SEL_PAD, MOBA_BLOCK), 0)
            past = n_sub < i
            score = jnp.where(past, gate_t, NEG_INF)
            picked = past & _top_mask_t(score, n_blk, MOBA_TOPK)
            allowed_t = jnp.where(picked | (n_sub == i), 1.0, 0.0)
            allowed = _nt(eye_ref[...], allowed_t.astype(BF16))
            qaug = _augment_q(q * ATTN_SCALE, allowed)

            def bias_tile(d):
                c0 = (1 - d) * T_TILE
                return bias_ref[h, :, c0:c0 + T_TILE]

            o = _two_pass_attention(i, qaug, lambda kr: kaug_ref[kr, :], vaug_ref, bias_tile,
                                    s_ref, rm_ref, mb_ref, acc_ref, band=False)
            o_ref[0, rows, qc] = o.astype(o_ref.dtype)
            return carry

        lax.fori_loop(0, n_blk, tile_body, 0)


def _moba(pc, bias):
    b, seq, _ = pc.shape
    width = C_HEADS * HEAD_DIM
    return pl.pallas_call(
        functools.partial(_moba_kernel, seq=seq),
        out_shape=jax.ShapeDtypeStruct((b, seq, width), BF16),
        grid=(b,),
        in_specs=[pl.BlockSpec((1, seq, _C_COLS), lambda i: (i, 0, 0)),
                  pl.BlockSpec(bias.shape, lambda i: (0, 0, 0))],
        out_specs=pl.BlockSpec((1, seq, width), lambda i: (i, 0, 0)),
        scratch_shapes=[pltpu.VMEM((seq, LANE), BF16),
                        pltpu.VMEM((seq, LANE), BF16),
                        pltpu.VMEM((T_TILE, T_TILE), BF16),
                        pltpu.VMEM((seq // T_TILE, T_TILE, T_TILE), F32),
                        pltpu.VMEM((T_TILE, T_TILE), F32),
                        pltpu.VMEM((T_TILE, T_TILE), F32),
                        pltpu.VMEM((T_TILE, LANE), F32)],
        compiler_params=_params(1),
        name="moba",
    )(pc, bias)


def _merge_kernel(oa_ref, lse_ref, ob_ref, oc_ref, mg_ref, x_ref, wb_ref, wo_ref, y_ref):
    n_groups = len(A_GROUPS)
    gw = A_HEADS_PER_GROUP * HEAD_DIM
    lse = [lse_ref[g] for g in range(n_groups)]
    top = functools.reduce(jnp.maximum, lse)
    ex = [jnp.exp(v - top) for v in lse]
    den = functools.reduce(lambda a, b: a + b, ex)
    oa = jnp.concatenate([(ex[g] / den * oa_ref[g]).astype(BF16) for g in range(n_groups)], axis=-1)
    r0 = n_groups * gw
    r1 = r0 + B_HEADS * HEAD_DIM
    ya = jnp.dot(oa, wb_ref[0:r0, :], preferred_element_type=F32)
    yb = jnp.dot(ob_ref[...], wb_ref[r0:r1, :], preferred_element_type=F32)
    yc = jnp.dot(oc_ref[...], wb_ref[r1:, :], preferred_element_type=F32)
    merged = (jax.nn.sigmoid(mg_ref[:, 0:D_MODEL]) * ya
              + jax.nn.sigmoid(mg_ref[:, D_MODEL:2 * D_MODEL]) * yb
              + jax.nn.sigmoid(mg_ref[:, 2 * D_MODEL:]) * yc)
    y_ref[...] = x_ref[...] + jnp.dot(merged.astype(BF16), wo_ref[...], preferred_element_type=F32)


def _merge(oa, lse, ob, oc, mg, x2, wb, wo):
    t = x2.shape[0]
    tm = ROW_TILE
    row = lambda i: (i, 0)
    grp = pl.BlockSpec((len(A_GROUPS), tm, LANE), lambda i: (0, i, 0))
    return pl.pallas_call(
        _merge_kernel,
        out_shape=jax.ShapeDtypeStruct((t, D_MODEL), F32),
        grid=(t // tm,),
        in_specs=[grp, grp,
                  pl.BlockSpec((tm, ob.shape[1]), row), pl.BlockSpec((tm, oc.shape[1]), row),
                  pl.BlockSpec((tm, _MG_COLS), row), pl.BlockSpec((tm, D_MODEL), row),
                  pl.BlockSpec(wb.shape, lambda i: (0, 0)), pl.BlockSpec(wo.shape, lambda i: (0, 0))],
        out_specs=pl.BlockSpec((tm, D_MODEL), row),
        compiler_params=_params(1),
        name="merge",
    )(oa, lse, ob, oc, mg, x2, wb, wo)


_FF_CHUNK = D_FF // 2


def _ffn_kernel(x_ref, halo_ref, g_ref, wu_ref, cw_ref, cb_ref, wd_ref, gf_ref, y_ref, xn_ref,
                *, tiles_per_seq, final_norm):
    i = pl.program_id(0)
    x = x_ref[...]
    g = g_ref[...]
    keep = (i % tiles_per_seq != 0).astype(F32)
    xn_ref[0:FFN_HALO, :] = (_rmsnorm_rows(halo_ref[...], g) * keep).astype(BF16)
    xn_ref[FFN_HALO:, :] = _rmsnorm_rows(x, g).astype(BF16)
    xn = xn_ref[...]
    rows = xn.shape[0]

    def conv(c0):
        hcol = jnp.dot(xn, wu_ref[:, c0:c0 + _FF_CHUNK], preferred_element_type=F32)
        out = (cw_ref[2:3, c0:c0 + _FF_CHUNK] * hcol
               + cw_ref[1:2, c0:c0 + _FF_CHUNK] * pltpu.roll(hcol, 1, axis=0)
               + cw_ref[0:1, c0:c0 + _FF_CHUNK] * pltpu.roll(hcol, 2, axis=0)
               + cb_ref[:, c0:c0 + _FF_CHUNK])
        return out[FFN_HALO:rows]

    acc = x
    for c in range(0, D_FF, _FF_CHUNK):
        a = conv(c)
        u = conv(D_FF + c)
        act = (a * jax.nn.sigmoid(a) * u).astype(BF16)
        acc = acc + jnp.dot(act, wd_ref[c:c + _FF_CHUNK, :], preferred_element_type=F32)
    if final_norm:
        acc = _rmsnorm_rows(acc, gf_ref[...])
    y_ref[...] = acc


def _ffn(x2, g, wu, cw, cb, wd, gf, *, seq, final_norm):
    t = x2.shape[0]
    tm = ROW_TILE
    const = lambda i: (0, 0)
    halo_blocks = tm // FFN_HALO
    return pl.pallas_call(
        functools.partial(_ffn_kernel, tiles_per_seq=seq // tm, final_norm=final_norm),
        out_shape=jax.ShapeDtypeStruct((t, D_MODEL), F32),
        grid=(t // tm,),
        in_specs=[pl.BlockSpec((tm, D_MODEL), lambda i: (i, 0)),
                  pl.BlockSpec((FFN_HALO, D_MODEL), lambda i: (jnp.maximum(i * halo_blocks - 1, 0), 0)),
                  pl.BlockSpec((1, D_MODEL), const),
                  pl.BlockSpec(wu.shape, const), pl.BlockSpec(cw.shape, const),
                  pl.BlockSpec(cb.shape, const), pl.BlockSpec(wd.shape, const),
                  pl.BlockSpec((1, D_MODEL), const)],
        out_specs=pl.BlockSpec((tm, D_MODEL), lambda i: (i, 0)),
        scratch_shapes=[pltpu.VMEM((tm + FFN_HALO, D_MODEL), BF16)],
        compiler_params=_params(1),
        name="conv_ffn",
    )(x2, x2, g, wu, cw, cb, wd, gf)


def _to_sub(a, dil):
    b, s, c = a.shape
    return a.reshape(b, s // dil, dil, c).swapaxes(1, 2).reshape(b * dil, s // dil, c)


def _from_sub(a, dil, b):
    n, l, c = a.shape
    return a.reshape(b, dil, l, c).swapaxes(1, 2).reshape(b * dil * l, c)


def kernel(x, rel_bias, norm_mix, w_in, cmp_pe_k, cmp_w1_k, cmp_w2_k, cmp_pe_v, cmp_w1_v, cmp_w2_v,
           w_branch, w_out, norm_ffn, w_up, conv_w, conv_b, w_down, norm_final):
    b, s, d = x.shape
    depth = w_in.shape[0]
    t = b * s
    assert d == D_MODEL and s % T_TILE == 0 and s // NSA_SLC_BLOCK == _N_SLC and t % ROW_TILE == 0
    assert s % ROW_TILE == 0 and all(s % (dil * A_TILE) == 0 for _, dil in A_GROUPS)
    assert all(win // dil == A_TILE for win, dil in A_GROUPS)

    hg = A_HEADS_PER_GROUP
    bias_a = [
        _bias_tiles(rel_bias,
                    np.concatenate([_bucket_tile(A_TILE, 1, dil, win // dil),
                                    _bucket_tile(A_TILE, 0, dil, win // dil)], axis=1),
                    gi * hg, hg)
        for gi, (win, dil) in enumerate(A_GROUPS)]
    assert (_bucket_tile(T_TILE, 2, 1, 3 * T_TILE) == REL_BUCKETS - 1).all()
    idx_b = np.concatenate([_bucket_tile(T_TILE, off, 1, NSA_WINDOW - 1) for off in (2, 1, 0)], axis=1)
    bias_b = _bias_tiles(rel_bias, idx_b, A_HEADS, B_HEADS, relative=True).reshape(
        B_KV_HEADS, _G_ROWS, 3 * T_TILE)
    idx_c = np.concatenate([_bucket_tile(T_TILE, off, 1, s) for off in (1, 0)], axis=1)
    bias_c = _bias_tiles(rel_bias, idx_c, A_HEADS + B_HEADS, C_HEADS, relative=True)

    x2 = x.reshape(t, d)
    for i in range(depth):
        pa, pb, pcmp, pc, mg, bg = _proj_in(x2, norm_mix[i][None, :], _prep_w_in(w_in[i]))

        pa3 = pa.reshape(b, s, _A_COLS)
        oa, lse = [], []
        for gi, (win, dil) in enumerate(A_GROUPS):
            qkv = [_to_sub(pa3[:, :, (j * A_HEADS + gi * hg) * HEAD_DIM:
                               (j * A_HEADS + (gi + 1) * hg) * HEAD_DIM], dil) for j in range(3)]
            o_g, lse_g = _band_a(*qkv, bias_a[gi])
            oa.append(_from_sub(o_g, dil, b))
            lse.append(_from_sub(lse_g, dil, b))

        cmp = _compress(pcmp.reshape(4, b, s // NSA_CMP_STRIDE, NSA_CMP_STRIDE * HEAD_DIM),
                        cmp_pe_k[i].reshape(1, -1), cmp_pe_v[i].reshape(1, -1),
                        cmp_w1_k[i].astype(BF16), cmp_w1_v[i].astype(BF16),
                        cmp_w2_k[i].astype(BF16), cmp_w2_v[i].astype(BF16))
        ob = _nsa(pb.reshape(b, s, _PB_COLS), cmp, bg.reshape(b, s, _BG_COLS), bias_b)

        oc = _moba(pc.reshape(b, s, _C_COLS), bias_c)

        x2 = _merge(jnp.stack(oa), jnp.stack(lse), ob.reshape(t, -1), oc.reshape(t, -1), mg, x2,
                    w_branch[i].astype(BF16), w_out[i].astype(BF16))
        x2 = _ffn(x2, norm_ffn[i][None, :], w_up[i].astype(BF16), conv_w[i], conv_b[i][None, :],
                  w_down[i].astype(BF16), norm_final[None, :], seq=s, final_norm=(i == depth - 1))
    return x2.reshape(b, s, d)
```

```python
import functools
import math

import jax
import jax.numpy as jnp
import numpy as np
from jax import lax
from jax.experimental import pallas as pl
from jax.experimental.pallas import tpu as pltpu

F32 = jnp.float32
BF16 = jnp.bfloat16

D_MODEL = 1024
HEAD_DIM = 64
A_GROUPS = ((128, 1), (512, 4), (2048, 16))
A_HEADS_PER_GROUP = 2
A_HEADS = 6
B_HEADS = 6
B_KV_HEADS = 2
B_GROUP = 3
C_HEADS = 4
NSA_CMP_BLOCK = 32
NSA_CMP_STRIDE = 16
NSA_CMP_HIDDEN = 256
NSA_SLC_BLOCK = 64
NSA_SLC_TOPK = 16
NSA_WINDOW = 512
MOBA_BLOCK = 256
MOBA_TOPK = 3
REL_BUCKETS = 32
REL_MAX_DIST = 128
D_FF = 2816
RMS_EPS = 1e-6
NEG_INF = -1e30
FORCE_BONUS = 1e4
ATTN_SCALE = HEAD_DIM ** -0.5

LANE = 128
A_TILE = 128
T_TILE = 256
ROW_TILE = 512
FFN_HALO = 8
VMEM_LIMIT = 56 * 1024 * 1024

_GRP_COLS = 3 * A_HEADS_PER_GROUP * HEAD_DIM
_A_COLS = len(A_GROUPS) * _GRP_COLS
_PB_COLS = B_HEADS * HEAD_DIM + 4 * B_KV_HEADS * HEAD_DIM
_CMP_COLS = 2 * B_KV_HEADS * HEAD_DIM
_C_COLS = 3 * C_HEADS * HEAD_DIM
_MG_COLS = 3 * D_MODEL
_BG_COLS = LANE
_OFF_A = 0
_OFF_PB = _OFF_A + _A_COLS
_OFF_CMP = _OFF_PB + _PB_COLS
_OFF_C = _OFF_CMP + _CMP_COLS
_OFF_MG = _OFF_C + _C_COLS
_OFF_BG = _OFF_MG + _MG_COLS
_W_COLS = _OFF_BG + _BG_COLS


def _params(n_grid):
    return pltpu.CompilerParams(dimension_semantics=("arbitrary",) * n_grid,
                                vmem_limit_bytes=VMEM_LIMIT)


def _rel_bucket_np(dist):
    n = np.maximum(dist, 0)
    exact = REL_BUCKETS // 2
    nf = np.maximum(n, 1).astype(np.float32)
    large = exact + (np.log(nf / np.float32(exact)) / np.float32(math.log(REL_MAX_DIST / exact))
                     * np.float32(REL_BUCKETS - exact)).astype(np.int32)
    return np.where(n < exact, n, np.minimum(large, REL_BUCKETS - 1)).astype(np.int32)


def _bucket_tile(tile, block_offset, dil, max_rel):
    rel = block_offset * tile + np.arange(tile)[:, None] - np.arange(tile)[None, :]
    ok = (rel >= 0) & (rel <= max_rel)
    return np.where(ok, _rel_bucket_np(rel * dil), -1).astype(np.int32)


def _bias_tiles_kernel(tbl_ref, idx_ref, o_ref, *, head0, relative):
    h = pl.program_id(0) + head0
    idx = idx_ref[...]
    acc = jnp.full(idx.shape, NEG_INF, F32)
    for b in range(REL_BUCKETS):
        acc = jnp.where(idx == b, tbl_ref[b, h], acc)
    if relative:
        acc = acc - tbl_ref[REL_BUCKETS - 1, h]
    o_ref[0] = acc


def _bias_tiles(rel_bias, idx, head0, n_heads, relative=False):
    r, c = idx.shape
    return pl.pallas_call(
        functools.partial(_bias_tiles_kernel, head0=head0, relative=relative),
        out_shape=jax.ShapeDtypeStruct((n_heads, r, c), F32),
        grid=(n_heads,),
        in_specs=[pl.BlockSpec(memory_space=pltpu.SMEM),
                  pl.BlockSpec((r, c), lambda h: (0, 0))],
        out_specs=pl.BlockSpec((1, r, c), lambda h: (h, 0, 0)),
        compiler_params=_params(1),
        name="bias_tiles",
    )(rel_bias, jnp.asarray(idx))


def _rmsnorm_rows(x, g):
    return x * lax.rsqrt(jnp.mean(x * x, axis=-1, keepdims=True) + RMS_EPS) * g


def _proj_in_kernel(x_ref, g_ref, w_ref, pa0_ref, pa1_ref, pa2_ref, pb_ref, pcmp_ref, pc_ref,
                    mg_ref, bg_ref, regroup_ref):
    xn = _rmsnorm_rows(x_ref[...], g_ref[...]).astype(BF16)
    tm = xn.shape[0]

    def mm(c0, c1):
        return jnp.dot(xn, w_ref[:, c0:c1], preferred_element_type=F32)

    pa0_ref[...] = mm(_OFF_A, _OFF_A + _GRP_COLS).astype(BF16)
    for gi, out_ref in ((1, pa1_ref), (2, pa2_ref)):
        dil = A_GROUPS[gi][1]
        res = mm(_OFF_A + gi * _GRP_COLS, _OFF_A + (gi + 1) * _GRP_COLS)
        for j in range(3):
            regroup_ref[j] = res[:, j * LANE:(j + 1) * LANE]
        for r in range(dil):
            for j in range(3):
                out_ref[0, r, :, j * LANE:(j + 1) * LANE] = (
                    regroup_ref[j, pl.ds(r, tm // dil, stride=dil), :].astype(BF16))
    pb_ref[:, 0:384] = mm(_OFF_PB, _OFF_PB + 384).astype(BF16)
    pb_ref[:, 384:896] = mm(_OFF_PB + 384, _OFF_PB + 896).astype(BF16)
    cmp = mm(_OFF_CMP, _OFF_CMP + _CMP_COLS).astype(BF16)
    for i in range(4):
        pcmp_ref[i] = cmp[:, i * HEAD_DIM:(i + 1) * HEAD_DIM]
    for c in range(0, _C_COLS, 384):
        pc_ref[:, c:c + 384] = mm(_OFF_C + c, _OFF_C + c + 384).astype(BF16)
    for c in range(0, _MG_COLS, 512):
        mg_ref[:, c:c + 512] = mm(_OFF_MG + c, _OFF_MG + c + 512)
    bg_ref[...] = mm(_OFF_BG, _OFF_BG + _BG_COLS)


def _proj_in(x2, g, w, *, seq):
    t = x2.shape[0]
    tm = ROW_TILE
    b = t // seq
    tps = seq // tm
    row = lambda i: (i, 0)
    d1, d2 = A_GROUPS[1][1], A_GROUPS[2][1]
    sub = lambda i: (i // tps, 0, i % tps, 0)
    return pl.pallas_call(
        _proj_in_kernel,
        out_shape=(jax.ShapeDtypeStruct((t, _GRP_COLS), BF16),
                   jax.ShapeDtypeStruct((b, d1, seq // d1, _GRP_COLS), BF16),
                   jax.ShapeDtypeStruct((b, d2, seq // d2, _GRP_COLS), BF16),
                   jax.ShapeDtypeStruct((t, _PB_COLS), BF16),
                   jax.ShapeDtypeStruct((4, t, HEAD_DIM), BF16),
                   jax.ShapeDtypeStruct((t, _C_COLS), BF16),
                   jax.ShapeDtypeStruct((t, _MG_COLS), F32),
                   jax.ShapeDtypeStruct((t, _BG_COLS), F32)),
        grid=(t // tm,),
        in_specs=[pl.BlockSpec((tm, D_MODEL), row),
                  pl.BlockSpec((1, D_MODEL), lambda i: (0, 0)),
                  pl.BlockSpec((D_MODEL, _W_COLS), lambda i: (0, 0))],
        out_specs=(pl.BlockSpec((tm, _GRP_COLS), row),
                   pl.BlockSpec((1, d1, tm // d1, _GRP_COLS), sub),
                   pl.BlockSpec((1, d2, tm // d2, _GRP_COLS), sub),
                   pl.BlockSpec((tm, _PB_COLS), row),
                   pl.BlockSpec((4, tm, HEAD_DIM), lambda i: (0, i, 0)),
                   pl.BlockSpec((tm, _C_COLS), row),
                   pl.BlockSpec((tm, _MG_COLS), row),
                   pl.BlockSpec((tm, _BG_COLS), row)),
        scratch_shapes=[pltpu.VMEM((3, tm, LANE), F32)],
        compiler_params=_params(1),
        name="proj_in",
    )(x2, g, w)


def _prep_w_in(w):
    a = A_HEADS * HEAD_DIM * 3
    bq0 = a
    bkc0 = bq0 + B_HEADS * HEAD_DIM
    bks0 = bkc0 + 2 * B_KV_HEADS * HEAD_DIM
    bg0 = bks0 + 4 * B_KV_HEADS * HEAD_DIM
    c0 = bg0 + B_HEADS * 3
    mg0 = c0 + _C_COLS
    gw = A_HEADS_PER_GROUP * HEAD_DIM
    a_parts = [w[:, j * A_HEADS * HEAD_DIM + gi * gw:j * A_HEADS * HEAD_DIM + (gi + 1) * gw]
               for gi in range(len(A_GROUPS)) for j in range(3)]
    parts = a_parts + [w[:, bq0:bkc0], w[:, bks0:bg0], w[:, bkc0:bks0], w[:, c0:mg0],
             w[:, mg0:mg0 + _MG_COLS], w[:, bg0:c0],
             jnp.zeros((w.shape[0], _BG_COLS - B_HEADS * 3), w.dtype)]
    return jnp.concatenate(parts, axis=1).astype(BF16)


_A_UNROLL = 3


def _mixer_a_kernel(pa0_ref, pa1_ref, pa2_ref, bias_ref, o_ref, on_ref, ln_ref, *, seq):
    gw = A_HEADS_PER_GROUP * HEAD_DIM

    def band_block(src, l0, first, gi, rows):
        outs, lses = [], []
        for h in range(A_HEADS_PER_GROUP):
            qc, kc, vc = (slice(j * gw + h * HEAD_DIM, j * gw + (h + 1) * HEAD_DIM) for j in range(3))
            q = src[pl.ds(l0, A_TILE), qc] * ATTN_SCALE
            if first:
                kr = pl.ds(l0, A_TILE)
                bias = bias_ref[A_HEADS_PER_GROUP * gi + h, :, A_TILE:]
            else:
                kr = pl.ds(l0 - A_TILE, 2 * A_TILE)
                bias = bias_ref[A_HEADS_PER_GROUP * gi + h]
            s = _nt(q, src[kr, kc]) + bias
            m = jnp.max(s, axis=-1, keepdims=True)
            e = jnp.exp(s - m)
            l = jnp.sum(e, axis=-1, keepdims=True)
            outs.append(jnp.dot(e.astype(BF16), src[kr, vc], preferred_element_type=F32) / l)
            lses.append(jnp.broadcast_to(m + jnp.log(l), (A_TILE, HEAD_DIM)))
        on_ref[gi, rows, :] = jnp.concatenate(outs, axis=-1)
        ln_ref[gi, rows, :] = jnp.concatenate(lses, axis=-1)

    for gi, ((_, dil), pa_ref) in enumerate(zip(A_GROUPS, (pa0_ref, pa1_ref, pa2_ref))):
        n_blocks = seq // dil // A_TILE
        if dil == 1:
            band_block(pa_ref.at[0], 0, True, gi, pl.ds(0, A_TILE))
            assert (n_blocks - 1) % _A_UNROLL == 0

            def body0(it, c, pa_ref=pa_ref, gi=gi):
                for u in range(_A_UNROLL):
                    l0 = pl.multiple_of((1 + it * _A_UNROLL + u) * A_TILE, A_TILE)
                    band_block(pa_ref.at[0], l0, False, gi, pl.ds(l0, A_TILE))
                return c
            lax.fori_loop(0, (n_blocks - 1) // _A_UNROLL, body0, 0)
        elif n_blocks > 1:
            for r in range(dil):
                band_block(pa_ref.at[0, r], 0, True, gi, pl.ds(r, A_TILE, stride=dil))

            def body1(i, c, pa_ref=pa_ref, gi=gi, dil=dil):
                l0 = pl.multiple_of(i * A_TILE, A_TILE)
                for r in range(dil):
                    band_block(pa_ref.at[0, r], l0, False, gi, pl.ds(l0 * dil + r, A_TILE, stride=dil))
                return c
            lax.fori_loop(1, n_blocks, body1, 0)
        else:
            per_iter = 4

            def body2(it, c, pa_ref=pa_ref, gi=gi, dil=dil):
                for u in range(per_iter):
                    r = it * per_iter + u
                    band_block(pa_ref.at[0, r], 0, True, gi, pl.ds(r, A_TILE, stride=dil))
                return c
            lax.fori_loop(0, dil // per_iter, body2, 0)

    chunk = T_TILE
    for c0 in range(0, seq, chunk):
        rows = slice(c0, c0 + chunk)
        lse = [ln_ref[g, rows, :] for g in range(len(A_GROUPS))]
        top = functools.reduce(jnp.maximum, lse)
        ex = [jnp.exp(v - top) for v in lse]
        den = functools.reduce(lambda a, b: a + b, ex)
        for g in range(len(A_GROUPS)):
            o_ref[0, rows, g * gw:(g + 1) * gw] = (ex[g] / den * on_ref[g, rows, :]).astype(BF16)


def _mixer_a(pa0, pa1, pa2, bias):
    b, seq, _ = pa0.shape
    return pl.pallas_call(
        functools.partial(_mixer_a_kernel, seq=seq),
        out_shape=jax.ShapeDtypeStruct((b, seq, _GRP_COLS), BF16),
        grid=(b,),
        in_specs=[pl.BlockSpec((1, seq, _GRP_COLS), lambda i: (i, 0, 0)),
                  pl.BlockSpec((1,) + pa1.shape[1:], lambda i: (i, 0, 0, 0)),
                  pl.BlockSpec((1,) + pa2.shape[1:], lambda i: (i, 0, 0, 0)),
                  pl.BlockSpec(bias.shape, lambda i: (0, 0, 0))],
        out_specs=pl.BlockSpec((1, seq, _GRP_COLS), lambda i: (i, 0, 0)),
        scratch_shapes=[pltpu.VMEM((len(A_GROUPS), seq, LANE), F32),
                        pltpu.VMEM((len(A_GROUPS), seq, LANE), F32)],
        compiler_params=_params(1),
        name="mixer_a",
    )(pa0, pa1, pa2, bias)


def _compress_kernel(r_ref, pek_ref, pev_ref, w1k_ref, w1v_ref, w2k_ref, w2v_ref, o_ref):
    half = NSA_CMP_STRIDE * HEAD_DIM
    for kv, (pe_ref, w1_ref, w2_ref) in enumerate(((pek_ref, w1k_ref, w2k_ref),
                                                   (pev_ref, w1v_ref, w2v_ref))):
        for h in range(B_KV_HEADS):
            r = r_ref[kv * B_KV_HEADS + h, 0].astype(F32)
            lo = jnp.dot((r + pe_ref[:, :half]).astype(BF16), w1_ref[:half, :],
                         preferred_element_type=F32)
            hi = jnp.dot((r + pe_ref[:, half:]).astype(BF16), w1_ref[half:, :],
                         preferred_element_type=F32)
            hid = lo + pltpu.roll(hi, hi.shape[0] - 1, axis=0)
            act = jax.nn.gelu(hid).astype(BF16)
            o_ref[0, kv * B_KV_HEADS + h] = jnp.dot(act, w2_ref[...],
                                                    preferred_element_type=F32).astype(BF16)


def _compress(r, pe_k, pe_v, w1_k, w1_v, w2_k, w2_v):
    _, b, m, c = r.shape
    const2 = lambda i: (0, 0)
    return pl.pallas_call(
        _compress_kernel,
        out_shape=jax.ShapeDtypeStruct((b, 4, m, HEAD_DIM), BF16),
        grid=(b,),
        in_specs=[pl.BlockSpec((4, 1, m, c), lambda i: (0, i, 0, 0)),
                  pl.BlockSpec(pe_k.shape, const2), pl.BlockSpec(pe_v.shape, const2),
                  pl.BlockSpec(w1_k.shape, const2), pl.BlockSpec(w1_v.shape, const2),
                  pl.BlockSpec(w2_k.shape, const2), pl.BlockSpec(w2_v.shape, const2)],
        out_specs=pl.BlockSpec((1, 4, m, HEAD_DIM), lambda i: (i, 0, 0, 0)),
        compiler_params=_params(1),
        name="nsa_compress",
    )(r, pe_k, pe_v, w1_k, w1_v, w2_k, w2_v)


MASK_BIG = 2.0 ** 100
SEL_PAD = 32


def _nt(a, b, **kw):
    return lax.dot_general(a, b, (((1,), (1,)), ((), ())), preferred_element_type=F32, **kw)


def _top_mask_t(score, n_rows, n_top):
    sub = lax.broadcasted_iota(jnp.int32, score.shape, 0)
    rank = jnp.zeros(score.shape, F32)
    for j in range(n_rows):
        row = score[j:j + 1, :]
        ge = jnp.where(row >= score, 1.0, 0.0)
        gt = jnp.where(row > score, 1.0, 0.0)
        rank = rank + jnp.where(sub > j, ge, gt)
    return rank < n_top


def _two_pass_attention(i, lhs, k_tile, v_ref, bias_tile, s_ref, rm_ref, mb_ref, acc_ref, *, band):
    def logits(j, d):
        s = _nt(lhs, k_tile(pl.ds(pl.multiple_of(j * T_TILE, T_TILE), T_TILE)))
        if d is not None:
            s = s + bias_tile(d)
        s_ref[j] = s
        return s

    rm_ref[...] = logits(i, 0)

    @pl.when(i >= 1)
    def _():
        rm_ref[...] = jnp.maximum(rm_ref[...], logits(i - 1, 1))

    if band:
        @pl.when(i >= 2)
        def _():
            rm_ref[...] = jnp.maximum(rm_ref[...], logits(i - 2, 2))
        lo = jnp.maximum(i - 2, 0)
    else:
        def far(j, c):
            rm_ref[...] = jnp.maximum(rm_ref[...], logits(j, None))
            return c
        lax.fori_loop(0, i - 1, far, 0)
        lo = 0

    mb_ref[...] = jnp.broadcast_to(jnp.max(rm_ref[...], axis=-1, keepdims=True), mb_ref.shape)
    acc_ref[...] = jnp.zeros(acc_ref.shape, F32)

    def pv(j, c):
        e = jnp.exp(s_ref[j] - mb_ref[...])
        kr = pl.ds(pl.multiple_of(j * T_TILE, T_TILE), T_TILE)
        acc_ref[...] += jnp.dot(e.astype(BF16), v_ref[kr, :], preferred_element_type=F32)
        return c
    lax.fori_loop(lo, i + 1, pv, 0)
    acc = acc_ref[...]
    return acc[:, :HEAD_DIM] / acc[:, HEAD_DIM:HEAD_DIM + 1]


def _with_ones_column(v):
    one = (lax.broadcasted_iota(jnp.int32, v.shape, 1) == 0).astype(F32)
    return jnp.concatenate([v.astype(F32), one], axis=1).astype(BF16)


def _with_block_onehot(k, block):
    n = k.shape[0]
    shape = (n, SEL_PAD)
    hot = (lax.broadcasted_iota(jnp.int32, shape, 0) // block
           == lax.broadcasted_iota(jnp.int32, shape, 1)).astype(F32)
    return jnp.concatenate([k.astype(F32), hot, jnp.zeros(shape, F32)], axis=1).astype(BF16)


def _augment_q(q, allowed):
    pen = (allowed - 1.0) * MASK_BIG
    return jnp.concatenate([q.astype(F32), pen, jnp.zeros(pen.shape, F32)], axis=1).astype(BF16)


_N_SLC = 32
_G_ROWS = B_GROUP * T_TILE


def _nsa_kernel(pb_ref, cmp_ref, bg_ref, bias_ref, o_ref,
                kaug_ref, vaug_ref, vwaug_ref, eye_ref, s_ref, rm_ref, mb_ref, acc_ref,
                ocmp_ref, oslc_ref, *, seq):
    n_tiles = seq // T_TILE
    n_cmp_rows = seq // NSA_CMP_STRIDE
    q_cols = B_HEADS * HEAD_DIM
    ks0, vs0, kw0, vw0 = (q_cols + i * B_KV_HEADS * HEAD_DIM for i in range(4))

    def head_cols(c0, h):
        return slice(c0 + h * HEAD_DIM, c0 + (h + 1) * HEAD_DIM)

    j_id = lax.broadcasted_iota(jnp.int32, (_N_SLC, n_cmp_rows), 0)
    c_id = lax.broadcasted_iota(jnp.int32, (_N_SLC, n_cmp_rows), 1)
    overlap_t = ((c_id * NSA_CMP_STRIDE < (j_id + 1) * NSA_SLC_BLOCK)
                 & (c_id * NSA_CMP_STRIDE + NSA_CMP_BLOCK > j_id * NSA_SLC_BLOCK)).astype(F32)
    eye_ref[...] = (lax.broadcasted_iota(jnp.int32, (_G_ROWS, T_TILE), 0) % T_TILE
                    == lax.broadcasted_iota(jnp.int32, (_G_ROWS, T_TILE), 1)).astype(BF16)
    for h in range(B_KV_HEADS):
        kaug_ref[h] = _with_block_onehot(pb_ref[0, :, head_cols(ks0, h)], NSA_SLC_BLOCK)
        vaug_ref[h] = _with_ones_column(pb_ref[0, :, head_cols(vs0, h)])
        vwaug_ref[h] = _with_ones_column(pb_ref[0, :, head_cols(vw0, h)])

    def tile_body(i, carry):
        r0 = pl.multiple_of(i * T_TILE, T_TILE)
        rows = pl.ds(r0, T_TILE)
        sig = jax.nn.sigmoid(bg_ref[0, rows, :])
        t_pos3 = r0 + lax.broadcasted_iota(jnp.int32, (_G_ROWS, 1), 0) % T_TILE

        for h in range(B_KV_HEADS):
            q3 = jnp.concatenate(
                [pb_ref[0, rows, head_cols(0, h * B_GROUP + g)] for g in range(B_GROUP)],
                axis=0) * ATTN_SCALE

            kc = cmp_ref[0, h]
            vc = cmp_ref[0, B_KV_HEADS + h]
            c_end = (lax.broadcasted_iota(jnp.int32, (_G_ROWS, n_cmp_rows), 1) * NSA_CMP_STRIDE
                     + NSA_CMP_BLOCK - 1)
            ok = (c_end <= t_pos3) & (c_end < seq)
            lc = jnp.where(ok, _nt(q3, kc), NEG_INF)
            ec = jnp.exp(lc - jnp.max(lc, axis=-1, keepdims=True))
            pc = ec / jnp.sum(ec, axis=-1, keepdims=True) * ok.astype(F32)
            ocmp_ref[...] = jnp.dot(pc.astype(BF16), vc, preferred_element_type=F32)
            p_sum = pc[0:T_TILE] + pc[T_TILE:2 * T_TILE] + pc[2 * T_TILE:3 * T_TILE]
            imp_t = _nt(overlap_t, p_sum, precision=lax.Precision.HIGHEST)

            j_sub = lax.broadcasted_iota(jnp.int32, (_N_SLC, T_TILE), 0)
            jt = (r0 + lax.broadcasted_iota(jnp.int32, (_N_SLC, T_TILE), 1)) // NSA_SLC_BLOCK
            forced = (j_sub == 0) | (j_sub == jt) | (j_sub == jt - 1)
            valid = j_sub <= jt
            score = jnp.where(valid, imp_t + FORCE_BONUS * forced.astype(F32), NEG_INF)
            sel_t = jnp.where(valid & _top_mask_t(score, _N_SLC, NSA_SLC_TOPK), 1.0, 0.0)
            sel3 = _nt(eye_ref[...], sel_t.astype(BF16))
            qaug = _augment_q(q3, sel3)

            def bias_tile(d):
                c0 = (2 - d) * T_TILE
                return bias_ref[h, :, c0:c0 + T_TILE]

            oslc_ref[...] = _two_pass_attention(
                i, qaug, lambda kr: kaug_ref[h, kr, :], vaug_ref.at[h], bias_tile,
                s_ref, rm_ref, mb_ref, acc_ref, band=False)

            owin = _two_pass_attention(
                i, q3, lambda kr: pb_ref[0, kr, head_cols(kw0, h)], vwaug_ref.at[h], bias_tile,
                s_ref, rm_ref, mb_ref, acc_ref, band=True)

            for g in range(B_GROUP):
                head = h * B_GROUP + g
                gr = slice(g * T_TILE, (g + 1) * T_TILE)
                o = (sig[:, 3 * head:3 * head + 1] * ocmp_ref[gr, :]
                     + sig[:, 3 * head + 1:3 * head + 2] * oslc_ref[gr, :]
                     + sig[:, 3 * head + 2:3 * head + 3] * owin[gr, :])
                o_ref[0, rows, head * HEAD_DIM:(head + 1) * HEAD_DIM] = o.astype(o_ref.dtype)
        return carry

    lax.fori_loop(0, n_tiles, tile_body, 0)


def _nsa(pb, cmp, bg, bias):
    b, seq, _ = pb.shape
    return pl.pallas_call(
        functools.partial(_nsa_kernel, seq=seq),
        out_shape=jax.ShapeDtypeStruct((b, seq, B_HEADS * HEAD_DIM), BF16),
        grid=(b,),
        in_specs=[pl.BlockSpec((1, seq, _PB_COLS), lambda i: (i, 0, 0)),
                  pl.BlockSpec((1,) + cmp.shape[1:], lambda i: (i, 0, 0, 0)),
                  pl.BlockSpec((1, seq, _BG_COLS), lambda i: (i, 0, 0)),
                  pl.BlockSpec(bias.shape, lambda i: (0, 0, 0))],
        out_specs=pl.BlockSpec((1, seq, B_HEADS * HEAD_DIM), lambda i: (i, 0, 0)),
        scratch_shapes=[pltpu.VMEM((B_KV_HEADS, seq, LANE), BF16),
                        pltpu.VMEM((B_KV_HEADS, seq, LANE), BF16),
                        pltpu.VMEM((B_KV_HEADS, seq, LANE), BF16),
                        pltpu.VMEM((_G_ROWS, T_TILE), BF16),
                        pltpu.VMEM((seq // T_TILE, _G_ROWS, T_TILE), F32),
                        pltpu.VMEM((_G_ROWS, T_TILE), F32),
                        pltpu.VMEM((_G_ROWS, T_TILE), F32),
                        pltpu.VMEM((_G_ROWS, LANE), F32),
                        pltpu.VMEM((_G_ROWS, HEAD_DIM), F32),
                        pltpu.VMEM((_G_ROWS, HEAD_DIM), F32)],
        compiler_params=_params(1),
        name="nsa",
    )(pb, cmp, bg, bias)


def _moba_kernel(pc_ref, bias_ref, o_ref, kaug_ref, vaug_ref, eye_ref, s_ref, rm_ref, mb_ref,
                 acc_ref, *, seq):
    n_blk = seq // MOBA_BLOCK
    width = C_HEADS * HEAD_DIM
    eye_ref[...] = (lax.broadcasted_iota(jnp.int32, (T_TILE, T_TILE), 0)
                    == lax.broadcasted_iota(jnp.int32, (T_TILE, T_TILE), 1)).astype(BF16)

    for h in range(C_HEADS):
        qc = slice(h * HEAD_DIM, (h + 1) * HEAD_DIM)
        kc = slice(width + h * HEAD_DIM, width + (h + 1) * HEAD_DIM)
        vc = slice(2 * width + h * HEAD_DIM, 2 * width + (h + 1) * HEAD_DIM)
        kaug_ref[...] = _with_block_onehot(pc_ref[0, :, kc], MOBA_BLOCK)
        vaug_ref[...] = _with_ones_column(pc_ref[0, :, vc])
        kmean = jnp.concatenate(
            [jnp.mean(pc_ref[0, n * MOBA_BLOCK:(n + 1) * MOBA_BLOCK, kc].astype(F32), axis=0,
                      keepdims=True) for n in range(n_blk)]
            + [jnp.zeros((SEL_PAD - n_blk, HEAD_DIM), F32)], axis=0)

        def tile_body(i, carry):
            rows = pl.ds(pl.multiple_of(i * MOBA_BLOCK, MOBA_BLOCK), MOBA_BLOCK)
            q = pc_ref[0, rows, qc]
            gate_t = _nt(kmean, q.astype(F32), precision=lax.Precision.HIGHEST)
            n_sub = lax.broadcasted_iota(jnp.int32, (SEL_PAD, MOBA_BLOCK), 0)
            past = n_sub < i
            score = jnp.where(past, gate_t, NEG_INF)
            picked = past & _top_mask_t(score, n_blk, MOBA_TOPK)
            allowed_t = jnp.where(picked | (n_sub == i), 1.0, 0.0)
            allowed = _nt(eye_ref[...], allowed_t.astype(BF16))
            qaug = _augment_q(q * ATTN_SCALE, allowed)

            def bias_tile(d):
                c0 = (1 - d) * T_TILE
                return bias_ref[h, :, c0:c0 + T_TILE]

            o = _two_pass_attention(i, qaug, lambda kr: kaug_ref[kr, :], vaug_ref, bias_tile,
                                    s_ref, rm_ref, mb_ref, acc_ref, band=False)
            o_ref[0, rows, qc] = o.astype(o_ref.dtype)
            return carry

        lax.fori_loop(0, n_blk, tile_body, 0)


def _moba(pc, bias):
    b, seq, _ = pc.shape
    width = C_HEADS * HEAD_DIM
    return pl.pallas_call(
        functools.partial(_moba_kernel, seq=seq),
        out_shape=jax.ShapeDtypeStruct((b, seq, width), BF16),
        grid=(b,),
        in_specs=[pl.BlockSpec((1, seq, _C_COLS), lambda i: (i, 0, 0)),
                  pl.BlockSpec(bias.shape, lambda i: (0, 0, 0))],
        out_specs=pl.BlockSpec((1, seq, width), lambda i: (i, 0, 0)),
        scratch_shapes=[pltpu.VMEM((seq, LANE), BF16),
                        pltpu.VMEM((seq, LANE), BF16),
                        pltpu.VMEM((T_TILE, T_TILE), BF16),
                        pltpu.VMEM((seq // T_TILE, T_TILE, T_TILE), F32),
                        pltpu.VMEM((T_TILE, T_TILE), F32),
                        pltpu.VMEM((T_TILE, T_TILE), F32),
                        pltpu.VMEM((T_TILE, LANE), F32)],
        compiler_params=_params(1),
        name="moba",
    )(pc, bias)


def _merge_kernel(oa_ref, ob_ref, oc_ref, mg_ref, x_ref, wb_ref, wo_ref, y_ref):
    r0 = A_HEADS * HEAD_DIM
    r1 = r0 + B_HEADS * HEAD_DIM
    ya = jnp.dot(oa_ref[...], wb_ref[0:r0, :], preferred_element_type=F32)
    yb = jnp.dot(ob_ref[...], wb_ref[r0:r1, :], preferred_element_type=F32)
    yc = jnp.dot(oc_ref[...], wb_ref[r1:, :], preferred_element_type=F32)
    merged = (jax.nn.sigmoid(mg_ref[:, 0:D_MODEL]) * ya
              + jax.nn.sigmoid(mg_ref[:, D_MODEL:2 * D_MODEL]) * yb
              + jax.nn.sigmoid(mg_ref[:, 2 * D_MODEL:]) * yc)
    y_ref[...] = x_ref[...] + jnp.dot(merged.astype(BF16), wo_ref[...], preferred_element_type=F32)


def _merge(oa, ob, oc, mg, x2, wb, wo):
    t = x2.shape[0]
    tm = ROW_TILE
    row = lambda i: (i, 0)
    return pl.pallas_call(
        _merge_kernel,
        out_shape=jax.ShapeDtypeStruct((t, D_MODEL), F32),
        grid=(t // tm,),
        in_specs=[pl.BlockSpec((tm, oa.shape[1]), row),
                  pl.BlockSpec((tm, ob.shape[1]), row), pl.BlockSpec((tm, oc.shape[1]), row),
                  pl.BlockSpec((tm, _MG_COLS), row), pl.BlockSpec((tm, D_MODEL), row),
                  pl.BlockSpec(wb.shape, lambda i: (0, 0)), pl.BlockSpec(wo.shape, lambda i: (0, 0))],
        out_specs=pl.BlockSpec((tm, D_MODEL), row),
        compiler_params=_params(1),
        name="merge",
    )(oa, ob, oc, mg, x2, wb, wo)


_FF_CHUNK = D_FF // 2


def _ffn_kernel(x_ref, halo_ref, g_ref, wu_ref, cw_ref, cb_ref, wd_ref, gf_ref, y_ref, xn_ref,
                *, tiles_per_seq, final_norm):
    i = pl.program_id(0)
    x = x_ref[...]
    g = g_ref[...]
    keep = (i % tiles_per_seq != 0).astype(F32)
    xn_ref[0:FFN_HALO, :] = (_rmsnorm_rows(halo_ref[...], g) * keep).astype(BF16)
    xn_ref[FFN_HALO:, :] = _rmsnorm_rows(x, g).astype(BF16)
    xn = xn_ref[...]
    rows = xn.shape[0]

    def conv(c0):
        hcol = jnp.dot(xn, wu_ref[:, c0:c0 + _FF_CHUNK], preferred_element_type=F32)
        out = (cw_ref[2:3, c0:c0 + _FF_CHUNK] * hcol
               + cw_ref[1:2, c0:c0 + _FF_CHUNK] * pltpu.roll(hcol, 1, axis=0)
               + cw_ref[0:1, c0:c0 + _FF_CHUNK] * pltpu.roll(hcol, 2, axis=0)
               + cb_ref[:, c0:c0 + _FF_CHUNK])
        return out[FFN_HALO:rows]

    acc = x
    for c in range(0, D_FF, _FF_CHUNK):
        a = conv(c)
        u = conv(D_FF + c)
        act = (a * jax.nn.sigmoid(a) * u).astype(BF16)
        acc = acc + jnp.dot(act, wd_ref[c:c + _FF_CHUNK, :], preferred_element_type=F32)
    if final_norm:
        acc = _rmsnorm_rows(acc, gf_ref[...])
    y_ref[...] = acc


def _ffn(x2, g, wu, cw, cb, wd, gf, *, seq, final_norm):
    t = x2.shape[0]
    tm = ROW_TILE
    const = lambda i: (0, 0)
    halo_blocks = tm // FFN_HALO
    return pl.pallas_call(
        functools.partial(_ffn_kernel, tiles_per_seq=seq // tm, final_norm=final_norm),
        out_shape=jax.ShapeDtypeStruct((t, D_MODEL), F32),
        grid=(t // tm,),
        in_specs=[pl.BlockSpec((tm, D_MODEL), lambda i: (i, 0)),
                  pl.BlockSpec((FFN_HALO, D_MODEL), lambda i: (jnp.maximum(i * halo_blocks - 1, 0), 0)),
                  pl.BlockSpec((1, D_MODEL), const),
                  pl.BlockSpec(wu.shape, const), pl.BlockSpec(cw.shape, const),
                  pl.BlockSpec(cb.shape, const), pl.BlockSpec(wd.shape, const),
                  pl.BlockSpec((1, D_MODEL), const)],
        out_specs=pl.BlockSpec((tm, D_MODEL), lambda i: (i, 0)),
        scratch_shapes=[pltpu.VMEM((tm + FFN_HALO, D_MODEL), BF16)],
        compiler_params=_params(1),
        name="conv_ffn",
    )(x2, x2, g, wu, cw, cb, wd, gf)


def kernel(x, rel_bias, norm_mix, w_in, cmp_pe_k, cmp_w1_k, cmp_w2_k, cmp_pe_v, cmp_w1_v, cmp_w2_v,
           w_branch, w_out, norm_ffn, w_up, conv_w, conv_b, w_down, norm_final):
    b, s, d = x.shape
    depth = w_in.shape[0]
    t = b * s
    assert d == D_MODEL and s % T_TILE == 0 and s // NSA_SLC_BLOCK == _N_SLC and t % ROW_TILE == 0
    assert s % ROW_TILE == 0 and all(s % (dil * A_TILE) == 0 for _, dil in A_GROUPS)
    assert all(win // dil == A_TILE for win, dil in A_GROUPS) and A_GROUPS[0][1] == 1
    assert all(ROW_TILE % dil == 0 for _, dil in A_GROUPS)

    hg = A_HEADS_PER_GROUP
    bias_a = jnp.concatenate([
        _bias_tiles(rel_bias,
                    np.concatenate([_bucket_tile(A_TILE, 1, dil, win // dil),
                                    _bucket_tile(A_TILE, 0, dil, win // dil)], axis=1),
                    gi * hg, hg)
        for gi, (win, dil) in enumerate(A_GROUPS)], axis=0)
    assert (_bucket_tile(T_TILE, 2, 1, 3 * T_TILE) == REL_BUCKETS - 1).all()
    idx_b = np.concatenate([_bucket_tile(T_TILE, off, 1, NSA_WINDOW - 1) for off in (2, 1, 0)], axis=1)
    bias_b = _bias_tiles(rel_bias, idx_b, A_HEADS, B_HEADS, relative=True).reshape(
        B_KV_HEADS, _G_ROWS, 3 * T_TILE)
    idx_c = np.concatenate([_bucket_tile(T_TILE, off, 1, s) for off in (1, 0)], axis=1)
    bias_c = _bias_tiles(rel_bias, idx_c, A_HEADS + B_HEADS, C_HEADS, relative=True)

    x2 = x.reshape(t, d)
    for i in range(depth):
        pa0, pa1, pa2, pb, pcmp, pc, mg, bg = _proj_in(x2, norm_mix[i][None, :], _prep_w_in(w_in[i]),
                                                       seq=s)

        oa = _mixer_a(pa0.reshape(b, s, _GRP_COLS), pa1, pa2, bias_a)

        cmp = _compress(pcmp.reshape(4, b, s // NSA_CMP_STRIDE, NSA_CMP_STRIDE * HEAD_DIM),
                        cmp_pe_k[i].reshape(1, -1), cmp_pe_v[i].reshape(1, -1),
                        cmp_w1_k[i].astype(BF16), cmp_w1_v[i].astype(BF16),
                        cmp_w2_k[i].astype(BF16), cmp_w2_v[i].astype(BF16))
        ob = _nsa(pb.reshape(b, s, _PB_COLS), cmp, bg.reshape(b, s, _BG_COLS), bias_b)

        oc = _moba(pc.reshape(b, s, _C_COLS), bias_c)

        x2 = _merge(oa.reshape(t, -1), ob.reshape(t, -1), oc.reshape(t, -1), mg, x2,
                    w_branch[i].astype(BF16), w_out[i].astype(BF16))
        x2 = _ffn(x2, norm_ffn[i][None, :], w_up[i].astype(BF16), conv_w[i], conv_b[i][None, :],
                  w_down[i].astype(BF16), norm_final[None, :], seq=s, final_norm=(i == depth - 1))
    return x2.reshape(b, s, d)
```

```python
import functools
import math

import jax
import jax.numpy as jnp
import numpy as np
from jax import lax
from jax.experimental import pallas as pl
from jax.experimental.pallas import tpu as pltpu

F32 = jnp.float32
BF16 = jnp.bfloat16

D_MODEL = 1024
HEAD_DIM = 64
A_GROUPS = ((128, 1), (512, 4), (2048, 16))
A_HEADS_PER_GROUP = 2
A_HEADS = 6
B_HEADS = 6
B_KV_HEADS = 2
B_GROUP = 3
C_HEADS = 4
NSA_CMP_BLOCK = 32
NSA_CMP_STRIDE = 16
NSA_CMP_HIDDEN = 256
NSA_SLC_BLOCK = 64
NSA_SLC_TOPK = 16
NSA_WINDOW = 512
MOBA_BLOCK = 256
MOBA_TOPK = 3
REL_BUCKETS = 32
REL_MAX_DIST = 128
D_FF = 2816
RMS_EPS = 1e-6
NEG_INF = -1e30
FORCE_BONUS = 1e4
ATTN_SCALE = HEAD_DIM ** -0.5

LANE = 128
A_TILE = 128
T_TILE = 256
ROW_TILE = 512
FFN_HALO = 8
VMEM_LIMIT = 56 * 1024 * 1024

_GRP_COLS = 3 * A_HEADS_PER_GROUP * HEAD_DIM
_A_COLS = len(A_GROUPS) * _GRP_COLS
_PB_COLS = B_HEADS * HEAD_DIM + 4 * B_KV_HEADS * HEAD_DIM
_CMP_COLS = 2 * B_KV_HEADS * HEAD_DIM
_C_COLS = 3 * C_HEADS * HEAD_DIM
_MG_COLS = 3 * D_MODEL
_BG_COLS = LANE
_OFF_A = 0
_OFF_PB = _OFF_A + _A_COLS
_OFF_CMP = _OFF_PB + _PB_COLS
_OFF_C = _OFF_CMP + _CMP_COLS
_OFF_MG = _OFF_C + _C_COLS
_OFF_BG = _OFF_MG + _MG_COLS
_W_COLS = _OFF_BG + _BG_COLS


def _params(n_grid):
    return pltpu.CompilerParams(dimension_semantics=("arbitrary",) * n_grid,
                                vmem_limit_bytes=VMEM_LIMIT)


def _rel_bucket_np(dist):
    n = np.maximum(dist, 0)
    exact = REL_BUCKETS // 2
    nf = np.maximum(n, 1).astype(np.float32)
    large = exact + (np.log(nf / np.float32(exact)) / np.float32(math.log(REL_MAX_DIST / exact))
                     * np.float32(REL_BUCKETS - exact)).astype(np.int32)
    return np.where(n < exact, n, np.minimum(large, REL_BUCKETS - 1)).astype(np.int32)


def _bucket_tile(tile, block_offset, dil, max_rel):
    rel = block_offset * tile + np.arange(tile)[:, None] - np.arange(tile)[None, :]
    ok = (rel >= 0) & (rel <= max_rel)
    return np.where(ok, _rel_bucket_np(rel * dil), -1).astype(np.int32)


def _bias_tiles_kernel(tbl_ref, idx_ref, o_ref, *, head0, relative):
    h = pl.program_id(0) + head0
    idx = idx_ref[...]
    acc = jnp.full(idx.shape, NEG_INF, F32)
    for b in range(REL_BUCKETS):
        acc = jnp.where(idx == b, tbl_ref[b, h], acc)
    if relative:
        acc = acc - tbl_ref[REL_BUCKETS - 1, h]
    o_ref[0] = acc


def _bias_tiles(rel_bias, idx, head0, n_heads, relative=False):
    r, c = idx.shape
    return pl.pallas_call(
        functools.partial(_bias_tiles_kernel, head0=head0, relative=relative),
        out_shape=jax.ShapeDtypeStruct((n_heads, r, c), F32),
        grid=(n_heads,),
        in_specs=[pl.BlockSpec(memory_space=pltpu.SMEM),
                  pl.BlockSpec((r, c), lambda h: (0, 0))],
        out_specs=pl.BlockSpec((1, r, c), lambda h: (h, 0, 0)),
        compiler_params=_params(1),
        name="bias_tiles",
    )(rel_bias, jnp.asarray(idx))


def _rmsnorm_rows(x, g):
    return x * lax.rsqrt(jnp.mean(x * x, axis=-1, keepdims=True) + RMS_EPS) * g


def _proj_in_kernel(x_ref, g_ref, w_ref, pa0_ref, pa1_ref, pa2_ref, pb_ref, pcmp_ref, pc_ref,
                    mg_ref, bg_ref, regroup_ref):
    xn = _rmsnorm_rows(x_ref[...], g_ref[...]).astype(BF16)
    tm = xn.shape[0]

    def mm(c0, c1):
        return jnp.dot(xn, w_ref[:, c0:c1], preferred_element_type=F32)

    pa0_ref[...] = mm(_OFF_A, _OFF_A + _GRP_COLS).astype(BF16)
    for gi, out_ref in ((1, pa1_ref), (2, pa2_ref)):
        dil = A_GROUPS[gi][1]
        res = mm(_OFF_A + gi * _GRP_COLS, _OFF_A + (gi + 1) * _GRP_COLS)
        for j in range(3):
            regroup_ref[j] = res[:, j * LANE:(j + 1) * LANE]
        for r in range(dil):
            for j in range(3):
                out_ref[0, r, :, j * LANE:(j + 1) * LANE] = (
                    regroup_ref[j, pl.ds(r, tm // dil, stride=dil), :].astype(BF16))
    pb_ref[:, 0:384] = mm(_OFF_PB, _OFF_PB + 384).astype(BF16)
    pb_ref[:, 384:896] = mm(_OFF_PB + 384, _OFF_PB + 896).astype(BF16)
    cmp = mm(_OFF_CMP, _OFF_CMP + _CMP_COLS).astype(BF16)
    for i in range(4):
        pcmp_ref[i] = cmp[:, i * HEAD_DIM:(i + 1) * HEAD_DIM]
    for c in range(0, _C_COLS, 384):
        pc_ref[:, c:c + 384] = mm(_OFF_C + c, _OFF_C + c + 384).astype(BF16)
    for c in range(0, _MG_COLS, 512):
        mg_ref[:, c:c + 512] = mm(_OFF_MG + c, _OFF_MG + c + 512)
    bg_ref[...] = mm(_OFF_BG, _OFF_BG + _BG_COLS)


def _proj_in(x2, g, w, *, layer, seq):
    t = x2.shape[0]
    tm = ROW_TILE
    b = t // seq
    tps = seq // tm
    row = lambda i: (i, 0)
    d1, d2 = A_GROUPS[1][1], A_GROUPS[2][1]
    sub = lambda i: (i // tps, 0, i % tps, 0)
    return pl.pallas_call(
        _proj_in_kernel,
        out_shape=(jax.ShapeDtypeStruct((t, _GRP_COLS), BF16),
                   jax.ShapeDtypeStruct((b, d1, seq // d1, _GRP_COLS), BF16),
                   jax.ShapeDtypeStruct((b, d2, seq // d2, _GRP_COLS), BF16),
                   jax.ShapeDtypeStruct((t, _PB_COLS), BF16),
                   jax.ShapeDtypeStruct((4, t, HEAD_DIM), BF16),
                   jax.ShapeDtypeStruct((t, _C_COLS), BF16),
                   jax.ShapeDtypeStruct((t, _MG_COLS), F32),
                   jax.ShapeDtypeStruct((t, _BG_COLS), F32)),
        grid=(t // tm,),
        in_specs=[pl.BlockSpec((tm, D_MODEL), row),
                  pl.BlockSpec((1, D_MODEL), lambda i: (0, 0)),
                  pl.BlockSpec((None, D_MODEL, _W_COLS), lambda i: (layer, 0, 0))],
        out_specs=(pl.BlockSpec((tm, _GRP_COLS), row),
                   pl.BlockSpec((1, d1, tm // d1, _GRP_COLS), sub),
                   pl.BlockSpec((1, d2, tm // d2, _GRP_COLS), sub),
                   pl.BlockSpec((tm, _PB_COLS), row),
                   pl.BlockSpec((4, tm, HEAD_DIM), lambda i: (0, i, 0)),
                   pl.BlockSpec((tm, _C_COLS), row),
                   pl.BlockSpec((tm, _MG_COLS), row),
                   pl.BlockSpec((tm, _BG_COLS), row)),
        scratch_shapes=[pltpu.VMEM((3, tm, LANE), F32)],
        compiler_params=_params(1),
        name="proj_in",
    )(x2, g, w)


def _w_in_segments():
    a = A_HEADS * HEAD_DIM
    bq0 = 3 * a
    bkc0 = bq0 + B_HEADS * HEAD_DIM
    bks0 = bkc0 + 2 * B_KV_HEADS * HEAD_DIM
    bg0 = bks0 + 4 * B_KV_HEADS * HEAD_DIM
    c0 = bg0 + B_HEADS * 3
    mg0 = c0 + _C_COLS
    gw = A_HEADS_PER_GROUP * HEAD_DIM
    segs = [(_OFF_A + (gi * 3 + j) * gw, j * a + gi * gw, gw)
            for gi in range(len(A_GROUPS)) for j in range(3)]
    segs += [(_OFF_PB, bq0, B_HEADS * HEAD_DIM),
             (_OFF_PB + B_HEADS * HEAD_DIM, bks0, bg0 - bks0),
             (_OFF_CMP, bkc0, _CMP_COLS), (_OFF_C, c0, _C_COLS), (_OFF_MG, mg0, _MG_COLS),
             (_OFF_BG, bg0, B_HEADS * 3)]
    return segs


def _prep_w_in_kernel(w_ref, o_ref):
    o_ref[0, :, _OFF_BG:] = jnp.zeros((o_ref.shape[1], _BG_COLS), BF16)
    for dst, src, width in _w_in_segments():
        for c in range(0, width, 768):
            n = min(768, width - c)
            o_ref[0, :, dst + c:dst + c + n] = w_ref[0, :, src + c:src + c + n].astype(BF16)


def _prep_w_in(w):
    depth, rows, cols = w.shape
    rb = 128
    return pl.pallas_call(
        _prep_w_in_kernel,
        out_shape=jax.ShapeDtypeStruct((depth, rows, _W_COLS), BF16),
        grid=(depth, rows // rb),
        in_specs=[pl.BlockSpec((1, rb, cols), lambda l, r: (l, r, 0))],
        out_specs=pl.BlockSpec((1, rb, _W_COLS), lambda l, r: (l, r, 0)),
        compiler_params=_params(2),
        name="prep_w_in",
    )(w)


_A_UNROLL = 3


def _mixer_a_kernel(pa0_ref, pa1_ref, pa2_ref, bias_ref, o_ref, on_ref, ln_ref, *, seq):
    gw = A_HEADS_PER_GROUP * HEAD_DIM

    def band_block(src, l0, first, gi, rows):
        outs, lses = [], []
        for h in range(A_HEADS_PER_GROUP):
            qc, kc, vc = (slice(j * gw + h * HEAD_DIM, j * gw + (h + 1) * HEAD_DIM) for j in range(3))
            q = src[pl.ds(l0, A_TILE), qc] * ATTN_SCALE
            if first:
                kr = pl.ds(l0, A_TILE)
                bias = bias_ref[A_HEADS_PER_GROUP * gi + h, :, A_TILE:]
            else:
                kr = pl.ds(l0 - A_TILE, 2 * A_TILE)
                bias = bias_ref[A_HEADS_PER_GROUP * gi + h]
            s = _nt(q, src[kr, kc]) + bias
            m = jnp.max(s, axis=-1, keepdims=True)
            e = jnp.exp(s - m)
            l = jnp.sum(e, axis=-1, keepdims=True)
            outs.append(jnp.dot(e.astype(BF16), src[kr, vc], preferred_element_type=F32) / l)
            lses.append(jnp.broadcast_to(m + jnp.log(l), (A_TILE, HEAD_DIM)))
        on_ref[gi, rows, :] = jnp.concatenate(outs, axis=-1)
        ln_ref[gi, rows, :] = jnp.concatenate(lses, axis=-1)

    for gi, ((_, dil), pa_ref) in enumerate(zip(A_GROUPS, (pa0_ref, pa1_ref, pa2_ref))):
        n_blocks = seq // dil // A_TILE
        if dil == 1:
            band_block(pa_ref.at[0], 0, True, gi, pl.ds(0, A_TILE))
            assert (n_blocks - 1) % _A_UNROLL == 0

            def body0(it, c, pa_ref=pa_ref, gi=gi):
                for u in range(_A_UNROLL):
                    l0 = pl.multiple_of((1 + it * _A_UNROLL + u) * A_TILE, A_TILE)
                    band_block(pa_ref.at[0], l0, False, gi, pl.ds(l0, A_TILE))
                return c
            lax.fori_loop(0, (n_blocks - 1) // _A_UNROLL, body0, 0)
        elif n_blocks > 1:
            for r in range(dil):
                band_block(pa_ref.at[0, r], 0, True, gi, pl.ds(r, A_TILE, stride=dil))

            def body1(i, c, pa_ref=pa_ref, gi=gi, dil=dil):
                l0 = pl.multiple_of(i * A_TILE, A_TILE)
                for r in range(dil):
                    band_block(pa_ref.at[0, r], l0, False, gi, pl.ds(l0 * dil + r, A_TILE, stride=dil))
                return c
            lax.fori_loop(1, n_blocks, body1, 0)
        else:
            per_iter = 4

            def body2(it, c, pa_ref=pa_ref, gi=gi, dil=dil):
                for u in range(per_iter):
                    r = it * per_iter + u
                    band_block(pa_ref.at[0, r], 0, True, gi, pl.ds(r, A_TILE, stride=dil))
                return c
            lax.fori_loop(0, dil // per_iter, body2, 0)

    chunk = T_TILE
    for c0 in range(0, seq, chunk):
        rows = slice(c0, c0 + chunk)
        lse = [ln_ref[g, rows, :] for g in range(len(A_GROUPS))]
        top = functools.reduce(jnp.maximum, lse)
        ex = [jnp.exp(v - top) for v in lse]
        den = functools.reduce(lambda a, b: a + b, ex)
        for g in range(len(A_GROUPS)):
            o_ref[0, rows, g * gw:(g + 1) * gw] = (ex[g] / den * on_ref[g, rows, :]).astype(BF16)


def _mixer_a(pa0, pa1, pa2, bias):
    b, seq, _ = pa0.shape
    return pl.pallas_call(
        functools.partial(_mixer_a_kernel, seq=seq),
        out_shape=jax.ShapeDtypeStruct((b, seq, _GRP_COLS), BF16),
        grid=(b,),
        in_specs=[pl.BlockSpec((1, seq, _GRP_COLS), lambda i: (i, 0, 0)),
                  pl.BlockSpec((1,) + pa1.shape[1:], lambda i: (i, 0, 0, 0)),
                  pl.BlockSpec((1,) + pa2.shape[1:], lambda i: (i, 0, 0, 0)),
                  pl.BlockSpec(bias.shape, lambda i: (0, 0, 0))],
        out_specs=pl.BlockSpec((1, seq, _GRP_COLS), lambda i: (i, 0, 0)),
        scratch_shapes=[pltpu.VMEM((len(A_GROUPS), seq, LANE), F32),
                        pltpu.VMEM((len(A_GROUPS), seq, LANE), F32)],
        compiler_params=_params(1),
        name="mixer_a",
    )(pa0, pa1, pa2, bias)


def _compress_kernel(r_ref, pek_ref, pev_ref, w1k_ref, w1v_ref, w2k_ref, w2v_ref, o_ref):
    half = NSA_CMP_STRIDE * HEAD_DIM
    for kv, (pe_ref, w1_ref, w2_ref) in enumerate(((pek_ref, w1k_ref, w2k_ref),
                                                   (pev_ref, w1v_ref, w2v_ref))):
        for h in range(B_KV_HEADS):
            r = r_ref[kv * B_KV_HEADS + h, 0].astype(F32)
            lo = jnp.dot((r + pe_ref[:, :half]).astype(BF16), w1_ref[:half, :],
                         preferred_element_type=F32)
            hi = jnp.dot((r + pe_ref[:, half:]).astype(BF16), w1_ref[half:, :],
                         preferred_element_type=F32)
            hid = lo + pltpu.roll(hi, hi.shape[0] - 1, axis=0)
            act = jax.nn.gelu(hid).astype(BF16)
            o_ref[0, kv * B_KV_HEADS + h] = jnp.dot(act, w2_ref[...],
                                                    preferred_element_type=F32).astype(BF16)


def _compress(r, pe_k, pe_v, w1_k, w1_v, w2_k, w2_v):
    _, b, m, c = r.shape
    const2 = lambda i: (0, 0)
    return pl.pallas_call(
        _compress_kernel,
        out_shape=jax.ShapeDtypeStruct((b, 4, m, HEAD_DIM), BF16),
        grid=(b,),
        in_specs=[pl.BlockSpec((4, 1, m, c), lambda i: (0, i, 0, 0)),
                  pl.BlockSpec(pe_k.shape, const2), pl.BlockSpec(pe_v.shape, const2),
                  pl.BlockSpec(w1_k.shape, const2), pl.BlockSpec(w1_v.shape, const2),
                  pl.BlockSpec(w2_k.shape, const2), pl.BlockSpec(w2_v.shape, const2)],
        out_specs=pl.BlockSpec((1, 4, m, HEAD_DIM), lambda i: (i, 0, 0, 0)),
        compiler_params=_params(1),
        name="nsa_compress",
    )(r, pe_k, pe_v, w1_k, w1_v, w2_k, w2_v)


MASK_BIG = 2.0 ** 100
SEL_PAD = 32


def _nt(a, b, **kw):
    return lax.dot_general(a, b, (((1,), (1,)), ((), ())), preferred_element_type=F32, **kw)


def _top_mask_t(score, n_rows, n_top):
    sub = lax.broadcasted_iota(jnp.int32, score.shape, 0)
    rank = jnp.zeros(score.shape, F32)
    for j in range(n_rows):
        row = score[j:j + 1, :]
        ge = jnp.where(row >= score, 1.0, 0.0)
        gt = jnp.where(row > score, 1.0, 0.0)
        rank = rank + jnp.where(sub > j, ge, gt)
    return rank < n_top


def _two_pass_attention(i, lhs, k_tile, v_ref, bias_tile, s_ref, rm_ref, mb_ref, acc_ref, *, band):
    def logits(j, d):
        s = _nt(lhs, k_tile(pl.ds(pl.multiple_of(j * T_TILE, T_TILE), T_TILE)))
        if d is not None:
            s = s + bias_tile(d)
        s_ref[j] = s
        return s

    rm_ref[...] = logits(i, 0)

    @pl.when(i >= 1)
    def _():
        rm_ref[...] = jnp.maximum(rm_ref[...], logits(i - 1, 1))

    if band:
        @pl.when(i >= 2)
        def _():
            rm_ref[...] = jnp.maximum(rm_ref[...], logits(i - 2, 2))
        lo = jnp.maximum(i - 2, 0)
    else:
        def far(j, c):
            rm_ref[...] = jnp.maximum(rm_ref[...], logits(j, None))
            return c
        lax.fori_loop(0, i - 1, far, 0)
        lo = 0

    mb_ref[...] = jnp.broadcast_to(jnp.max(rm_ref[...], axis=-1, keepdims=True), mb_ref.shape)
    acc_ref[...] = jnp.zeros(acc_ref.shape, F32)

    def pv(j, c):
        e = jnp.exp(s_ref[j] - mb_ref[...])
        kr = pl.ds(pl.multiple_of(j * T_TILE, T_TILE), T_TILE)
        acc_ref[...] += jnp.dot(e.astype(BF16), v_ref[kr, :], preferred_element_type=F32)
        return c
    lax.fori_loop(lo, i + 1, pv, 0)
    acc = acc_ref[...]
    return acc[:, :HEAD_DIM] / acc[:, HEAD_DIM:HEAD_DIM + 1]


def _with_ones_column(v):
    one = (lax.broadcasted_iota(jnp.int32, v.shape, 1) == 0).astype(F32)
    return jnp.concatenate([v.astype(F32), one], axis=1).astype(BF16)


def _with_block_onehot(k, block):
    n = k.shape[0]
    shape = (n, SEL_PAD)
    hot = (lax.broadcasted_iota(jnp.int32, shape, 0) // block
           == lax.broadcasted_iota(jnp.int32, shape, 1)).astype(F32)
    return jnp.concatenate([k.astype(F32), hot, jnp.zeros(shape, F32)], axis=1).astype(BF16)


def _augment_q(q, allowed):
    pen = (allowed - 1.0) * MASK_BIG
    return jnp.concatenate([q.astype(F32), pen, jnp.zeros(pen.shape, F32)], axis=1).astype(BF16)


_N_SLC = 32
_G_ROWS = B_GROUP * T_TILE


def _nsa_kernel(pb_ref, cmp_ref, bg_ref, bias_ref, o_ref,
                kaug_ref, vaug_ref, vwaug_ref, eye_ref, s_ref, rm_ref, mb_ref, acc_ref,
                ocmp_ref, oslc_ref, *, seq):
    n_tiles = seq // T_TILE
    n_cmp_rows = seq // NSA_CMP_STRIDE
    q_cols = B_HEADS * HEAD_DIM
    ks0, vs0, kw0, vw0 = (q_cols + i * B_KV_HEADS * HEAD_DIM for i in range(4))

    def head_cols(c0, h):
        return slice(c0 + h * HEAD_DIM, c0 + (h + 1) * HEAD_DIM)

    j_id = lax.broadcasted_iota(jnp.int32, (_N_SLC, n_cmp_rows), 0)
    c_id = lax.broadcasted_iota(jnp.int32, (_N_SLC, n_cmp_rows), 1)
    overlap_t = ((c_id * NSA_CMP_STRIDE < (j_id + 1) * NSA_SLC_BLOCK)
                 & (c_id * NSA_CMP_STRIDE + NSA_CMP_BLOCK > j_id * NSA_SLC_BLOCK)).astype(F32)
    eye_ref[...] = (lax.broadcasted_iota(jnp.int32, (_G_ROWS, T_TILE), 0) % T_TILE
                    == lax.broadcasted_iota(jnp.int32, (_G_ROWS, T_TILE), 1)).astype(BF16)
    for h in range(B_KV_HEADS):
        kaug_ref[h] = _with_block_onehot(pb_ref[0, :, head_cols(ks0, h)], NSA_SLC_BLOCK)
        vaug_ref[h] = _with_ones_column(pb_ref[0, :, head_cols(vs0, h)])
        vwaug_ref[h] = _with_ones_column(pb_ref[0, :, head_cols(vw0, h)])

    def tile_body(i, carry):
        r0 = pl.multiple_of(i * T_TILE, T_TILE)
        rows = pl.ds(r0, T_TILE)
        sig = jax.nn.sigmoid(bg_ref[0, rows, :])
        t_pos3 = r0 + lax.broadcasted_iota(jnp.int32, (_G_ROWS, 1), 0) % T_TILE

        for h in range(B_KV_HEADS):
            q3 = jnp.concatenate(
                [pb_ref[0, rows, head_cols(0, h * B_GROUP + g)] for g in range(B_GROUP)],
                axis=0) * ATTN_SCALE

            kc = cmp_ref[0, h]
            vc = cmp_ref[0, B_KV_HEADS + h]
            c_end = (lax.broadcasted_iota(jnp.int32, (_G_ROWS, n_cmp_rows), 1) * NSA_CMP_STRIDE
                     + NSA_CMP_BLOCK - 1)
            ok = (c_end <= t_pos3) & (c_end < seq)
            lc = jnp.where(ok, _nt(q3, kc), NEG_INF)
            ec = jnp.exp(lc - jnp.max(lc, axis=-1, keepdims=True))
            pc = ec / jnp.sum(ec, axis=-1, keepdims=True) * ok.astype(F32)
            ocmp_ref[...] = jnp.dot(pc.astype(BF16), vc, preferred_element_type=F32)
            p_sum = pc[0:T_TILE] + pc[T_TILE:2 * T_TILE] + pc[2 * T_TILE:3 * T_TILE]
            imp_t = _nt(overlap_t, p_sum, precision=lax.Precision.HIGHEST)

            j_sub = lax.broadcasted_iota(jnp.int32, (_N_SLC, T_TILE), 0)
            jt = (r0 + lax.broadcasted_iota(jnp.int32, (_N_SLC, T_TILE), 1)) // NSA_SLC_BLOCK
            forced = (j_sub == 0) | (j_sub == jt) | (j_sub == jt - 1)
            valid = j_sub <= jt
            score = jnp.where(valid, imp_t + FORCE_BONUS * forced.astype(F32), NEG_INF)
            sel_t = jnp.where(valid & _top_mask_t(score, _N_SLC, NSA_SLC_TOPK), 1.0, 0.0)
            sel3 = _nt(eye_ref[...], sel_t.astype(BF16))
            qaug = _augment_q(q3, sel3)

            def bias_tile(d):
                c0 = (2 - d) * T_TILE
                return bias_ref[h, :, c0:c0 + T_TILE]

            oslc_ref[...] = _two_pass_attention(
                i, qaug, lambda kr: kaug_ref[h, kr, :], vaug_ref.at[h], bias_tile,
                s_ref, rm_ref, mb_ref, acc_ref, band=False)

            owin = _two_pass_attention(
                i, q3, lambda kr: pb_ref[0, kr, head_cols(kw0, h)], vwaug_ref.at[h], bias_tile,
                s_ref, rm_ref, mb_ref, acc_ref, band=True)

            for g in range(B_GROUP):
                head = h * B_GROUP + g
                gr = slice(g * T_TILE, (g + 1) * T_TILE)
                o = (sig[:, 3 * head:3 * head + 1] * ocmp_ref[gr, :]
                     + sig[:, 3 * head + 1:3 * head + 2] * oslc_ref[gr, :]
                     + sig[:, 3 * head + 2:3 * head + 3] * owin[gr, :])
                o_ref[0, rows, head * HEAD_DIM:(head + 1) * HEAD_DIM] = o.astype(o_ref.dtype)
        return carry

    lax.fori_loop(0, n_tiles, tile_body, 0)


def _nsa(pb, cmp, bg, bias):
    b, seq, _ = pb.shape
    return pl.pallas_call(
        functools.partial(_nsa_kernel, seq=seq),
        out_shape=jax.ShapeDtypeStruct((b, seq, B_HEADS * HEAD_DIM), BF16),
        grid=(b,),
        in_specs=[pl.BlockSpec((1, seq, _PB_COLS), lambda i: (i, 0, 0)),
                  pl.BlockSpec((1,) + cmp.shape[1:], lambda i: (i, 0, 0, 0)),
                  pl.BlockSpec((1, seq, _BG_COLS), lambda i: (i, 0, 0)),
                  pl.BlockSpec(bias.shape, lambda i: (0, 0, 0))],
        out_specs=pl.BlockSpec((1, seq, B_HEADS * HEAD_DIM), lambda i: (i, 0, 0)),
        scratch_shapes=[pltpu.VMEM((B_KV_HEADS, seq, LANE), BF16),
                        pltpu.VMEM((B_KV_HEADS, seq, LANE), BF16),
                        pltpu.VMEM((B_KV_HEADS, seq, LANE), BF16),
                        pltpu.VMEM((_G_ROWS, T_TILE), BF16),
                        pltpu.VMEM((seq // T_TILE, _G_ROWS, T_TILE), F32),
                        pltpu.VMEM((_G_ROWS, T_TILE), F32),
                        pltpu.VMEM((_G_ROWS, T_TILE), F32),
                        pltpu.VMEM((_G_ROWS, LANE), F32),
                        pltpu.VMEM((_G_ROWS, HEAD_DIM), F32),
                        pltpu.VMEM((_G_ROWS, HEAD_DIM), F32)],
        compiler_params=_params(1),
        name="nsa",
    )(pb, cmp, bg, bias)


def _moba_kernel(pc_ref, bias_ref, o_ref, kaug_ref, vaug_ref, eye_ref, s_ref, rm_ref, mb_ref,
                 acc_ref, *, seq):
    n_blk = seq // MOBA_BLOCK
    width = C_HEADS * HEAD_DIM
    eye_ref[...] = (lax.broadcasted_iota(jnp.int32, (T_TILE, T_TILE), 0)
                    == lax.broadcasted_iota(jnp.int32, (T_TILE, T_TILE), 1)).astype(BF16)

    for h in range(C_HEADS):
        qc = slice(h * HEAD_DIM, (h + 1) * HEAD_DIM)
        kc = slice(width + h * HEAD_DIM, width + (h + 1) * HEAD_DIM)
        vc = slice(2 * width + h * HEAD_DIM, 2 * width + (h + 1) * HEAD_DIM)
        kaug_ref[...] = _with_block_onehot(pc_ref[0, :, kc], MOBA_BLOCK)
        vaug_ref[...] = _with_ones_column(pc_ref[0, :, vc])
        kmean = jnp.concatenate(
            [jnp.mean(pc_ref[0, n * MOBA_BLOCK:(n + 1) * MOBA_BLOCK, kc].astype(F32), axis=0,
                      keepdims=True) for n in range(n_blk)]
            + [jnp.zeros((SEL_PAD - n_blk, HEAD_DIM), F32)], axis=0)

        def tile_body(i, carry):
            rows = pl.ds(pl.multiple_of(i * MOBA_BLOCK, MOBA_BLOCK), MOBA_BLOCK)
            q = pc_ref[0, rows, qc]
            gate_t = _nt(kmean, q.astype(F32), precision=lax.Precision.HIGHEST)
            n_sub = lax.broadcasted_iota(jnp.int32, (SEL_PAD, MOBA_BLOCK), 0)
            past = n_sub < i
            score = jnp.where(past, gate_t, NEG_INF)
            picked = past & _top_mask_t(score, n_blk, MOBA_TOPK)
            allowed_t = jnp.where(picked | (n_sub == i), 1.0, 0.0)
            allowed = _nt(eye_ref[...], allowed_t.astype(BF16))
            qaug = _augment_q(q * ATTN_SCALE, allowed)

            def bias_tile(d):
                c0 = (1 - d) * T_TILE
                return bias_ref[h, :, c0:c0 + T_TILE]

            o = _two_pass_attention(i, qaug, lambda kr: kaug_ref[kr, :], vaug_ref, bias_tile,
                                    s_ref, rm_ref, mb_ref, acc_ref, band=False)
            o_ref[0, rows, qc] = o.astype(o_ref.dtype)
            return carry

        lax.fori_loop(0, n_blk, tile_body, 0)


def _moba(pc, bias):
    b, seq, _ = pc.shape
    width = C_HEADS * HEAD_DIM
    return pl.pallas_call(
        functools.partial(_moba_kernel, seq=seq),
        out_shape=jax.ShapeDtypeStruct((b, seq, width), BF16),
        grid=(b,),
        in_specs=[pl.BlockSpec((1, seq, _C_COLS), lambda i: (i, 0, 0)),
                  pl.BlockSpec(bias.shape, lambda i: (0, 0, 0))],
        out_specs=pl.BlockSpec((1, seq, width), lambda i: (i, 0, 0)),
        scratch_shapes=[pltpu.VMEM((seq, LANE), BF16),
                        pltpu.VMEM((seq, LANE), BF16),
                        pltpu.VMEM((T_TILE, T_TILE), BF16),
                        pltpu.VMEM((seq // T_TILE, T_TILE, T_TILE), F32),
                        pltpu.VMEM((T_TILE, T_TILE), F32),
                        pltpu.VMEM((T_TILE, T_TILE), F32),
                        pltpu.VMEM((T_TILE, LANE), F32)],
        compiler_params=_params(1),
        name="moba",
    )(pc, bias)


def _merge_kernel(oa_ref, ob_ref, oc_ref, mg_ref, x_ref, wb_ref, wo_ref, y_ref):
    r0 = A_HEADS * HEAD_DIM
    r1 = r0 + B_HEADS * HEAD_DIM
    ya = jnp.dot(oa_ref[...], wb_ref[0:r0, :], preferred_element_type=F32)
    yb = jnp.dot(ob_ref[...], wb_ref[r0:r1, :], preferred_element_type=F32)
    yc = jnp.dot(oc_ref[...], wb_ref[r1:, :], preferred_element_type=F32)
    merged = (jax.nn.sigmoid(mg_ref[:, 0:D_MODEL]) * ya
              + jax.nn.sigmoid(mg_ref[:, D_MODEL:2 * D_MODEL]) * yb
              + jax.nn.sigmoid(mg_ref[:, 2 * D_MODEL:]) * yc)
    y_ref[...] = x_ref[...] + jnp.dot(merged.astype(BF16), wo_ref[...], preferred_element_type=F32)


def _merge(oa, ob, oc, mg, x2, wb, wo):
    t = x2.shape[0]
    tm = ROW_TILE
    row = lambda i: (i, 0)
    return pl.pallas_call(
        _merge_kernel,
        out_shape=jax.ShapeDtypeStruct((t, D_MODEL), F32),
        grid=(t // tm,),
        in_specs=[pl.BlockSpec((tm, oa.shape[1]), row),
                  pl.BlockSpec((tm, ob.shape[1]), row), pl.BlockSpec((tm, oc.shape[1]), row),
                  pl.BlockSpec((tm, _MG_COLS), row), pl.BlockSpec((tm, D_MODEL), row),
                  pl.BlockSpec(wb.shape, lambda i: (0, 0)), pl.BlockSpec(wo.shape, lambda i: (0, 0))],
        out_specs=pl.BlockSpec((tm, D_MODEL), row),
        compiler_params=_params(1),
        name="merge",
    )(oa, ob, oc, mg, x2, wb, wo)


_FF_CHUNK = D_FF // 2


def _ffn_kernel(x_ref, halo_ref, g_ref, wu_ref, cw_ref, cb_ref, wd_ref, gf_ref, y_ref, xn_ref,
                *, tiles_per_seq, final_norm):
    i = pl.program_id(0)
    x = x_ref[...]
    g = g_ref[...]
    keep = (i % tiles_per_seq != 0).astype(F32)
    xn_ref[0:FFN_HALO, :] = (_rmsnorm_rows(halo_ref[...], g) * keep).astype(BF16)
    xn_ref[FFN_HALO:, :] = _rmsnorm_rows(x, g).astype(BF16)
    xn = xn_ref[...]
    rows = xn.shape[0]

    def conv(c0):
        hcol = jnp.dot(xn, wu_ref[:, c0:c0 + _FF_CHUNK], preferred_element_type=F32)
        out = (cw_ref[2:3, c0:c0 + _FF_CHUNK] * hcol
               + cw_ref[1:2, c0:c0 + _FF_CHUNK] * pltpu.roll(hcol, 1, axis=0)
               + cw_ref[0:1, c0:c0 + _FF_CHUNK] * pltpu.roll(hcol, 2, axis=0)
               + cb_ref[:, c0:c0 + _FF_CHUNK])
        return out[FFN_HALO:rows]

    acc = x
    for c in range(0, D_FF, _FF_CHUNK):
        a = conv(c)
        u = conv(D_FF + c)
        act = (a * jax.nn.sigmoid(a) * u).astype(BF16)
        acc = acc + jnp.dot(act, wd_ref[c:c + _FF_CHUNK, :], preferred_element_type=F32)
    if final_norm:
        acc = _rmsnorm_rows(acc, gf_ref[...])
    y_ref[...] = acc


def _ffn(x2, g, wu, cw, cb, wd, gf, *, seq, final_norm):
    t = x2.shape[0]
    tm = ROW_TILE
    const = lambda i: (0, 0)
    halo_blocks = tm // FFN_HALO
    return pl.pallas_call(
        functools.partial(_ffn_kernel, tiles_per_seq=seq // tm, final_norm=final_norm),
        out_shape=jax.ShapeDtypeStruct((t, D_MODEL), F32),
        grid=(t // tm,),
        in_specs=[pl.BlockSpec((tm, D_MODEL), lambda i: (i, 0)),
                  pl.BlockSpec((FFN_HALO, D_MODEL), lambda i: (jnp.maximum(i * halo_blocks - 1, 0), 0)),
                  pl.BlockSpec((1, D_MODEL), const),
                  pl.BlockSpec(wu.shape, const), pl.BlockSpec(cw.shape, const),
                  pl.BlockSpec(cb.shape, const), pl.BlockSpec(wd.shape, const),
                  pl.BlockSpec((1, D_MODEL), const)],
        out_specs=pl.BlockSpec((tm, D_MODEL), lambda i: (i, 0)),
        scratch_shapes=[pltpu.VMEM((tm + FFN_HALO, D_MODEL), BF16)],
        compiler_params=_params(1),
        name="conv_ffn",
    )(x2, x2, g, wu, cw, cb, wd, gf)


def kernel(x, rel_bias, norm_mix, w_in, cmp_pe_k, cmp_w1_k, cmp_w2_k, cmp_pe_v, cmp_w1_v, cmp_w2_v,
           w_branch, w_out, norm_ffn, w_up, conv_w, conv_b, w_down, norm_final):
    b, s, d = x.shape
    depth = w_in.shape[0]
    t = b * s
    assert d == D_MODEL and s % T_TILE == 0 and s // NSA_SLC_BLOCK == _N_SLC and t % ROW_TILE == 0
    assert s % ROW_TILE == 0 and all(s % (dil * A_TILE) == 0 for _, dil in A_GROUPS)
    assert all(win // dil == A_TILE for win, dil in A_GROUPS) and A_GROUPS[0][1] == 1
    assert all(ROW_TILE % dil == 0 for _, dil in A_GROUPS)

    hg = A_HEADS_PER_GROUP
    bias_a = jnp.concatenate([
        _bias_tiles(rel_bias,
                    np.concatenate([_bucket_tile(A_TILE, 1, dil, win // dil),
                                    _bucket_tile(A_TILE, 0, dil, win // dil)], axis=1),
                    gi * hg, hg)
        for gi, (win, dil) in enumerate(A_GROUPS)], axis=0)
    assert (_bucket_tile(T_TILE, 2, 1, 3 * T_TILE) == REL_BUCKETS - 1).all()
    idx_b = np.concatenate([_bucket_tile(T_TILE, off, 1, NSA_WINDOW - 1) for off in (2, 1, 0)], axis=1)
    bias_b = _bias_tiles(rel_bias, idx_b, A_HEADS, B_HEADS, relative=True).reshape(
        B_KV_HEADS, _G_ROWS, 3 * T_TILE)
    idx_c = np.concatenate([_bucket_tile(T_TILE, off, 1, s) for off in (1, 0)], axis=1)
    bias_c = _bias_tiles(rel_bias, idx_c, A_HEADS + B_HEADS, C_HEADS, relative=True)

    w_in_k = _prep_w_in(w_in)
    x2 = x.reshape(t, d)
    for i in range(depth):
        pa0, pa1, pa2, pb, pcmp, pc, mg, bg = _proj_in(x2, norm_mix[i][None, :], w_in_k, layer=i,
                                                       seq=s)

        oa = _mixer_a(pa0.reshape(b, s, _GRP_COLS), pa1, pa2, bias_a)

        cmp = _compress(pcmp.reshape(4, b, s // NSA_CMP_STRIDE, NSA_CMP_STRIDE * HEAD_DIM),
                        cmp_pe_k[i].reshape(1, -1), cmp_pe_v[i].reshape(1, -1),
                        cmp_w1_k[i].astype(BF16), cmp_w1_v[i].astype(BF16),
                        cmp_w2_k[i].astype(BF16), cmp_w2_v[i].astype(BF16))
        ob = _nsa(pb.reshape(b, s, _PB_COLS), cmp, bg.reshape(b, s, _BG_COLS), bias_b)

        oc = _moba(pc.reshape(b, s, _C_COLS), bias_c)

        x2 = _merge(oa.reshape(t, -1), ob.reshape(t, -1), oc.reshape(t, -1), mg, x2,
                    w_branch[i].astype(BF16), w_out[i].astype(BF16))
        x2 = _ffn(x2, norm_ffn[i][None, :], w_up[i].astype(BF16), conv_w[i], conv_b[i][None, :],
                  w_down[i].astype(BF16), norm_final[None, :], seq=s, final_norm=(i == depth - 1))
    return x2.reshape(b, s, d)
```

```python
import functools
import math

import jax
import jax.numpy as jnp
import numpy as np
from jax import lax
from jax.experimental import pallas as pl
from jax.experimental.pallas import tpu as pltpu

F32 = jnp.float32
BF16 = jnp.bfloat16

D_MODEL = 1024
HEAD_DIM = 64
A_GROUPS = ((128, 1), (512, 4), (2048, 16))
A_HEADS_PER_GROUP = 2
A_HEADS = 6
B_HEADS = 6
B_KV_HEADS = 2
B_GROUP = 3
C_HEADS = 4
NSA_CMP_BLOCK = 32
NSA_CMP_STRIDE = 16
NSA_CMP_HIDDEN = 256
NSA_SLC_BLOCK = 64
NSA_SLC_TOPK = 16
NSA_WINDOW = 512
MOBA_BLOCK = 256
MOBA_TOPK = 3
REL_BUCKETS = 32
REL_MAX_DIST = 128
D_FF = 2816
RMS_EPS = 1e-6
NEG_INF = -1e30
FORCE_BONUS = 1e4
ATTN_SCALE = HEAD_DIM ** -0.5

LANE = 128
A_TILE = 128
T_TILE = 256
ROW_TILE = 512
FFN_HALO = 8
VMEM_LIMIT = 56 * 1024 * 1024

_GRP_COLS = 3 * A_HEADS_PER_GROUP * HEAD_DIM
_A_COLS = len(A_GROUPS) * _GRP_COLS
_PB_COLS = B_HEADS * HEAD_DIM + 4 * B_KV_HEADS * HEAD_DIM
_CMP_COLS = 2 * B_KV_HEADS * HEAD_DIM
_C_COLS = 3 * C_HEADS * HEAD_DIM
_MG_COLS = 3 * D_MODEL
_BG_COLS = LANE
_OFF_A = 0
_OFF_PB = _OFF_A + _A_COLS
_OFF_CMP = _OFF_PB + _PB_COLS
_OFF_C = _OFF_CMP + _CMP_COLS
_OFF_MG = _OFF_C + _C_COLS
_OFF_BG = _OFF_MG + _MG_COLS
_W_COLS = _OFF_BG + _BG_COLS


def _params(n_grid):
    return pltpu.CompilerParams(dimension_semantics=("arbitrary",) * n_grid,
                                vmem_limit_bytes=VMEM_LIMIT)


def _rel_bucket_np(dist):
    n = np.maximum(dist, 0)
    exact = REL_BUCKETS // 2
    nf = np.maximum(n, 1).astype(np.float32)
    large = exact + (np.log(nf / np.float32(exact)) / np.float32(math.log(REL_MAX_DIST / exact))
                     * np.float32(REL_BUCKETS - exact)).astype(np.int32)
    return np.where(n < exact, n, np.minimum(large, REL_BUCKETS - 1)).astype(np.int32)


def _bucket_tile(tile, block_offset, dil, max_rel):
    rel = block_offset * tile + np.arange(tile)[:, None] - np.arange(tile)[None, :]
    ok = (rel >= 0) & (rel <= max_rel)
    return np.where(ok, _rel_bucket_np(rel * dil), -1).astype(np.int32)


def _bias_tiles_kernel(tbl_ref, idx_ref, o_ref, *, head0, relative):
    h = pl.program_id(0) + head0
    idx = idx_ref[...]
    acc = jnp.full(idx.shape, NEG_INF, F32)
    for b in range(REL_BUCKETS):
        acc = jnp.where(idx == b, tbl_ref[b, h], acc)
    if relative:
        acc = acc - tbl_ref[REL_BUCKETS - 1, h]
    o_ref[0] = acc


def _bias_tiles(rel_bias, idx, head0, n_heads, relative=False):
    r, c = idx.shape
    return pl.pallas_call(
        functools.partial(_bias_tiles_kernel, head0=head0, relative=relative),
        out_shape=jax.ShapeDtypeStruct((n_heads, r, c), F32),
        grid=(n_heads,),
        in_specs=[pl.BlockSpec(memory_space=pltpu.SMEM),
                  pl.BlockSpec((r, c), lambda h: (0, 0))],
        out_specs=pl.BlockSpec((1, r, c), lambda h: (h, 0, 0)),
        compiler_params=_params(1),
        name="bias_tiles",
    )(rel_bias, jnp.asarray(idx))


def _rmsnorm_rows(x, g):
    return x * lax.rsqrt(jnp.mean(x * x, axis=-1, keepdims=True) + RMS_EPS) * g


def _proj_in_kernel(x_ref, g_ref, w_ref, pa0_ref, pa1_ref, pa2_ref, pb_ref, pcmp_ref, pc_ref,
                    mg_ref, bg_ref, regroup_ref):
    xn = _rmsnorm_rows(x_ref[...], g_ref[...]).astype(BF16)
    tm = xn.shape[0]

    def mm(c0, c1):
        return jnp.dot(xn, w_ref[:, c0:c1], preferred_element_type=F32)

    pa0_ref[...] = mm(_OFF_A, _OFF_A + _GRP_COLS).astype(BF16)
    for gi, out_ref in ((1, pa1_ref), (2, pa2_ref)):
        dil = A_GROUPS[gi][1]
        res = mm(_OFF_A + gi * _GRP_COLS, _OFF_A + (gi + 1) * _GRP_COLS)
        for j in range(3):
            regroup_ref[j] = res[:, j * LANE:(j + 1) * LANE]
        for r in range(dil):
            for j in range(3):
                out_ref[0, r, :, j * LANE:(j + 1) * LANE] = (
                    regroup_ref[j, pl.ds(r, tm // dil, stride=dil), :].astype(BF16))
    pb_ref[:, 0:384] = mm(_OFF_PB, _OFF_PB + 384).astype(BF16)
    pb_ref[:, 384:896] = mm(_OFF_PB + 384, _OFF_PB + 896).astype(BF16)
    cmp = mm(_OFF_CMP, _OFF_CMP + _CMP_COLS).astype(BF16)
    for i in range(4):
        pcmp_ref[i] = cmp[:, i * HEAD_DIM:(i + 1) * HEAD_DIM]
    for c in range(0, _C_COLS, 384):
        pc_ref[:, c:c + 384] = mm(_OFF_C + c, _OFF_C + c + 384).astype(BF16)
    for c in range(0, _MG_COLS, 512):
        mg_ref[:, c:c + 512] = mm(_OFF_MG + c, _OFF_MG + c + 512)
    bg_ref[...] = mm(_OFF_BG, _OFF_BG + _BG_COLS)


def _proj_in(x2, g, w, *, layer, seq):
    t = x2.shape[0]
    tm = ROW_TILE
    b = t // seq
    tps = seq // tm
    row = lambda i: (i, 0)
    d1, d2 = A_GROUPS[1][1], A_GROUPS[2][1]
    sub = lambda i: (i // tps, 0, i % tps, 0)
    return pl.pallas_call(
        _proj_in_kernel,
        out_shape=(jax.ShapeDtypeStruct((t, _GRP_COLS), BF16),
                   jax.ShapeDtypeStruct((b, d1, seq // d1, _GRP_COLS), BF16),
                   jax.ShapeDtypeStruct((b, d2, seq // d2, _GRP_COLS), BF16),
                   jax.ShapeDtypeStruct((t, _PB_COLS), BF16),
                   jax.ShapeDtypeStruct((4, t, HEAD_DIM), BF16),
                   jax.ShapeDtypeStruct((t, _C_COLS), BF16),
                   jax.ShapeDtypeStruct((t, _MG_COLS), F32),
                   jax.ShapeDtypeStruct((t, _BG_COLS), F32)),
        grid=(t // tm,),
        in_specs=[pl.BlockSpec((tm, D_MODEL), row),
                  pl.BlockSpec((1, D_MODEL), lambda i: (0, 0)),
                  pl.BlockSpec((None, D_MODEL, _W_COLS), lambda i: (layer, 0, 0))],
        out_specs=(pl.BlockSpec((tm, _GRP_COLS), row),
                   pl.BlockSpec((1, d1, tm // d1, _GRP_COLS), sub),
                   pl.BlockSpec((1, d2, tm // d2, _GRP_COLS), sub),
                   pl.BlockSpec((tm, _PB_COLS), row),
                   pl.BlockSpec((4, tm, HEAD_DIM), lambda i: (0, i, 0)),
                   pl.BlockSpec((tm, _C_COLS), row),
                   pl.BlockSpec((tm, _MG_COLS), row),
                   pl.BlockSpec((tm, _BG_COLS), row)),
        scratch_shapes=[pltpu.VMEM((3, tm, LANE), F32)],
        compiler_params=_params(1),
        name="proj_in",
    )(x2, g, w)


def _w_in_segments():
    a = A_HEADS * HEAD_DIM
    bq0 = 3 * a
    bkc0 = bq0 + B_HEADS * HEAD_DIM
    bks0 = bkc0 + 2 * B_KV_HEADS * HEAD_DIM
    bg0 = bks0 + 4 * B_KV_HEADS * HEAD_DIM
    c0 = bg0 + B_HEADS * 3
    mg0 = c0 + _C_COLS
    gw = A_HEADS_PER_GROUP * HEAD_DIM
    segs = [(_OFF_A + (gi * 3 + j) * gw, j * a + gi * gw, gw)
            for gi in range(len(A_GROUPS)) for j in range(3)]
    segs += [(_OFF_PB, bq0, B_HEADS * HEAD_DIM),
             (_OFF_PB + B_HEADS * HEAD_DIM, bks0, bg0 - bks0),
             (_OFF_CMP, bkc0, _CMP_COLS), (_OFF_C, c0, _C_COLS), (_OFF_MG, mg0, _MG_COLS),
             (_OFF_BG, bg0, B_HEADS * 3)]
    return segs


def _prep_w_in_kernel(w_ref, o_ref):
    o_ref[0, :, _OFF_BG:] = jnp.zeros((o_ref.shape[1], _BG_COLS), BF16)
    for dst, src, width in _w_in_segments():
        for c in range(0, width, 768):
            n = min(768, width - c)
            o_ref[0, :, dst + c:dst + c + n] = w_ref[0, :, src + c:src + c + n].astype(BF16)


def _prep_w_in(w):
    depth, rows, cols = w.shape
    rb = 128
    return pl.pallas_call(
        _prep_w_in_kernel,
        out_shape=jax.ShapeDtypeStruct((depth, rows, _W_COLS), BF16),
        grid=(depth, rows // rb),
        in_specs=[pl.BlockSpec((1, rb, cols), lambda l, r: (l, r, 0))],
        out_specs=pl.BlockSpec((1, rb, _W_COLS), lambda l, r: (l, r, 0)),
        compiler_params=_params(2),
        name="prep_w_in",
    )(w)


_A_UNROLL = 3


def _mixer_a_kernel(pa0_ref, pa1_ref, pa2_ref, bias_ref, o_ref, on_ref, ln_ref, *, seq):
    gw = A_HEADS_PER_GROUP * HEAD_DIM

    def band_block(src, l0, first, gi, rows):
        outs, lses = [], []
        for h in range(A_HEADS_PER_GROUP):
            qc, kc, vc = (slice(j * gw + h * HEAD_DIM, j * gw + (h + 1) * HEAD_DIM) for j in range(3))
            q = src[pl.ds(l0, A_TILE), qc] * ATTN_SCALE
            if first:
                kr = pl.ds(l0, A_TILE)
                bias = bias_ref[A_HEADS_PER_GROUP * gi + h, :, A_TILE:]
            else:
                kr = pl.ds(l0 - A_TILE, 2 * A_TILE)
                bias = bias_ref[A_HEADS_PER_GROUP * gi + h]
            s = _nt(q, src[kr, kc]) + bias
            m = jnp.max(s, axis=-1, keepdims=True)
            e = jnp.exp(s - m)
            l = jnp.sum(e, axis=-1, keepdims=True)
            outs.append(jnp.dot(e.astype(BF16), src[kr, vc], preferred_element_type=F32) / l)
            lses.append(jnp.broadcast_to(m + jnp.log(l), (A_TILE, HEAD_DIM)))
        on_ref[gi, rows, :] = jnp.concatenate(outs, axis=-1)
        ln_ref[gi, rows, :] = jnp.concatenate(lses, axis=-1)

    for gi, ((_, dil), pa_ref) in enumerate(zip(A_GROUPS, (pa0_ref, pa1_ref, pa2_ref))):
        n_blocks = seq // dil // A_TILE
        if dil == 1:
            band_block(pa_ref.at[0], 0, True, gi, pl.ds(0, A_TILE))
            assert (n_blocks - 1) % _A_UNROLL == 0

            def body0(it, c, pa_ref=pa_ref, gi=gi):
                for u in range(_A_UNROLL):
                    l0 = pl.multiple_of((1 + it * _A_UNROLL + u) * A_TILE, A_TILE)
                    band_block(pa_ref.at[0], l0, False, gi, pl.ds(l0, A_TILE))
                return c
            lax.fori_loop(0, (n_blocks - 1) // _A_UNROLL, body0, 0)
        elif n_blocks > 1:
            for r in range(dil):
                band_block(pa_ref.at[0, r], 0, True, gi, pl.ds(r, A_TILE, stride=dil))

            def body1(i, c, pa_ref=pa_ref, gi=gi, dil=dil):
                l0 = pl.multiple_of(i * A_TILE, A_TILE)
                for r in range(dil):
                    band_block(pa_ref.at[0, r], l0, False, gi, pl.ds(l0 * dil + r, A_TILE, stride=dil))
                return c
            lax.fori_loop(1, n_blocks, body1, 0)
        else:
            per_iter = 4

            def body2(it, c, pa_ref=pa_ref, gi=gi, dil=dil):
                for u in range(per_iter):
                    r = it * per_iter + u
                    band_block(pa_ref.at[0, r], 0, True, gi, pl.ds(r, A_TILE, stride=dil))
                return c
            lax.fori_loop(0, dil // per_iter, body2, 0)

    chunk = T_TILE
    for c0 in range(0, seq, chunk):
        rows = slice(c0, c0 + chunk)
        lse = [ln_ref[g, rows, :] for g in range(len(A_GROUPS))]
        top = functools.reduce(jnp.maximum, lse)
        ex = [jnp.exp(v - top) for v in lse]
        den = functools.reduce(lambda a, b: a + b, ex)
        for g in range(len(A_GROUPS)):
            o_ref[0, rows, g * gw:(g + 1) * gw] = (ex[g] / den * on_ref[g, rows, :]).astype(BF16)


def _mixer_a(pa0, pa1, pa2, bias):
    b, seq, _ = pa0.shape
    return pl.pallas_call(
        functools.partial(_mixer_a_kernel, seq=seq),
        out_shape=jax.ShapeDtypeStruct((b, seq, _GRP_COLS), BF16),
        grid=(b,),
        in_specs=[pl.BlockSpec((1, seq, _GRP_COLS), lambda i: (i, 0, 0)),
                  pl.BlockSpec((1,) + pa1.shape[1:], lambda i: (i, 0, 0, 0)),
                  pl.BlockSpec((1,) + pa2.shape[1:], lambda i: (i, 0, 0, 0)),
                  pl.BlockSpec(bias.shape, lambda i: (0, 0, 0))],
        out_specs=pl.BlockSpec((1, seq, _GRP_COLS), lambda i: (i, 0, 0)),
        scratch_shapes=[pltpu.VMEM((len(A_GROUPS), seq, LANE), F32),
                        pltpu.VMEM((len(A_GROUPS), seq, LANE), F32)],
        compiler_params=_params(1),
        name="mixer_a",
    )(pa0, pa1, pa2, bias)


def _compress_kernel(r_ref, pek_ref, pev_ref, w1k_ref, w1v_ref, w2k_ref, w2v_ref, o_ref):
    half = NSA_CMP_STRIDE * HEAD_DIM
    for kv, (pe_ref, w1_ref, w2_ref) in enumerate(((pek_ref, w1k_ref, w2k_ref),
                                                   (pev_ref, w1v_ref, w2v_ref))):
        for h in range(B_KV_HEADS):
            r = r_ref[kv * B_KV_HEADS + h, 0].astype(F32)
            lo = jnp.dot((r + pe_ref[:, :half]).astype(BF16), w1_ref[:half, :],
                         preferred_element_type=F32)
            hi = jnp.dot((r + pe_ref[:, half:]).astype(BF16), w1_ref[half:, :],
                         preferred_element_type=F32)
            hid = lo + pltpu.roll(hi, hi.shape[0] - 1, axis=0)
            act = jax.nn.gelu(hid).astype(BF16)
            o_ref[0, kv * B_KV_HEADS + h] = jnp.dot(act, w2_ref[...],
                                                    preferred_element_type=F32).astype(BF16)


def _compress(r, pe_k, pe_v, w1_k, w1_v, w2_k, w2_v):
    _, b, m, c = r.shape
    const2 = lambda i: (0, 0)
    return pl.pallas_call(
        _compress_kernel,
        out_shape=jax.ShapeDtypeStruct((b, 4, m, HEAD_DIM), BF16),
        grid=(b,),
        in_specs=[pl.BlockSpec((4, 1, m, c), lambda i: (0, i, 0, 0)),
                  pl.BlockSpec(pe_k.shape, const2), pl.BlockSpec(pe_v.shape, const2),
                  pl.BlockSpec(w1_k.shape, const2), pl.BlockSpec(w1_v.shape, const2),
                  pl.BlockSpec(w2_k.shape, const2), pl.BlockSpec(w2_v.shape, const2)],
        out_specs=pl.BlockSpec((1, 4, m, HEAD_DIM), lambda i: (i, 0, 0, 0)),
        compiler_params=_params(1),
        name="nsa_compress",
    )(r, pe_k, pe_v, w1_k, w1_v, w2_k, w2_v)


MASK_BIG = 2.0 ** 100
SEL_PAD = 32


def _nt(a, b, **kw):
    return lax.dot_general(a, b, (((1,), (1,)), ((), ())), preferred_element_type=F32, **kw)


def _top_mask_t(score, n_rows, n_top):
    sub = lax.broadcasted_iota(jnp.int32, score.shape, 0)
    rank = jnp.zeros(score.shape, F32)
    for j in range(n_rows):
        row = score[j:j + 1, :]
        ge = jnp.where(row >= score, 1.0, 0.0)
        gt = jnp.where(row > score, 1.0, 0.0)
        rank = rank + jnp.where(sub > j, ge, gt)
    return rank < n_top


def _two_pass_attention(i, streams, *, band):
    def key_rows(j):
        return pl.ds(pl.multiple_of(j * T_TILE, T_TILE), T_TILE)

    def logits(j, d, first=False):
        for lhs, k_tile, _, bias_tile, s_ref, rm_ref, _, _ in streams:
            s = _nt(lhs(), k_tile(key_rows(j)))
            if d is not None:
                s = s + bias_tile(d)
            s_ref[j] = s
            rm_ref[...] = s if first else jnp.maximum(rm_ref[...], s)

    logits(i, 0, first=True)

    @pl.when(i >= 1)
    def _():
        logits(i - 1, 1)

    if band:
        @pl.when(i >= 2)
        def _():
            logits(i - 2, 2)
        lo = jnp.maximum(i - 2, 0)
    else:
        def far(j, c):
            logits(j, None)
            return c
        lax.fori_loop(0, i - 1, far, 0)
        lo = 0

    for _, _, _, _, _, rm_ref, mb_ref, acc_ref in streams:
        mb_ref[...] = jnp.broadcast_to(jnp.max(rm_ref[...], axis=-1, keepdims=True), mb_ref.shape)
        acc_ref[...] = jnp.zeros(acc_ref.shape, F32)

    def pv(j, c):
        for _, _, v_ref, _, s_ref, _, mb_ref, acc_ref in streams:
            e = jnp.exp(s_ref[j] - mb_ref[...])
            acc_ref[...] += jnp.dot(e.astype(BF16), v_ref[key_rows(j), :],
                                    preferred_element_type=F32)
        return c
    lax.fori_loop(lo, i + 1, pv, 0)

    outs = []
    for *_, acc_ref in streams:
        acc = acc_ref[...]
        outs.append(acc[:, :HEAD_DIM] / acc[:, HEAD_DIM:HEAD_DIM + 1])
    return outs


def _with_ones_column(v):
    one = (lax.broadcasted_iota(jnp.int32, v.shape, 1) == 0).astype(F32)
    return jnp.concatenate([v.astype(F32), one], axis=1).astype(BF16)


def _with_block_onehot(k, block):
    n = k.shape[0]
    shape = (n, SEL_PAD)
    hot = (lax.broadcasted_iota(jnp.int32, shape, 0) // block
           == lax.broadcasted_iota(jnp.int32, shape, 1)).astype(F32)
    return jnp.concatenate([k.astype(F32), hot, jnp.zeros(shape, F32)], axis=1).astype(BF16)


def _augment_q(q, allowed):
    pen = (allowed - 1.0) * MASK_BIG
    return jnp.concatenate([q.astype(F32), pen, jnp.zeros(pen.shape, F32)], axis=1).astype(BF16)


_N_SLC = 32
_G_ROWS = B_GROUP * T_TILE


def _nsa_kernel(pb_ref, cmp_ref, bg_ref, bias_ref, o_ref,
                kaug_ref, vaug_ref, vwaug_ref, eye_ref, q3_ref, qaug_ref, s_ref, rm_ref, mb_ref,
                acc_ref, ocmp_ref, oslc_ref, *, seq):
    n_tiles = seq // T_TILE
    n_cmp_rows = seq // NSA_CMP_STRIDE
    q_cols = B_HEADS * HEAD_DIM
    ks0, vs0, kw0, vw0 = (q_cols + i * B_KV_HEADS * HEAD_DIM for i in range(4))

    def head_cols(c0, h):
        return slice(c0 + h * HEAD_DIM, c0 + (h + 1) * HEAD_DIM)

    j_id = lax.broadcasted_iota(jnp.int32, (_N_SLC, n_cmp_rows), 0)
    c_id = lax.broadcasted_iota(jnp.int32, (_N_SLC, n_cmp_rows), 1)
    overlap_t = ((c_id * NSA_CMP_STRIDE < (j_id + 1) * NSA_SLC_BLOCK)
                 & (c_id * NSA_CMP_STRIDE + NSA_CMP_BLOCK > j_id * NSA_SLC_BLOCK)).astype(F32)
    eye_ref[...] = (lax.broadcasted_iota(jnp.int32, (_G_ROWS, T_TILE), 0) % T_TILE
                    == lax.broadcasted_iota(jnp.int32, (_G_ROWS, T_TILE), 1)).astype(BF16)
    for h in range(B_KV_HEADS):
        kaug_ref[h] = _with_block_onehot(pb_ref[0, :, head_cols(ks0, h)], NSA_SLC_BLOCK)
        vaug_ref[h] = _with_ones_column(pb_ref[0, :, head_cols(vs0, h)])
        vwaug_ref[h] = _with_ones_column(pb_ref[0, :, head_cols(vw0, h)])

    def tile_body(i, carry):
        r0 = pl.multiple_of(i * T_TILE, T_TILE)
        rows = pl.ds(r0, T_TILE)
        sig = jax.nn.sigmoid(bg_ref[0, rows, :])
        t_pos3 = r0 + lax.broadcasted_iota(jnp.int32, (_G_ROWS, 1), 0) % T_TILE

        for h in range(B_KV_HEADS):
            q3 = jnp.concatenate(
                [pb_ref[0, rows, head_cols(0, h * B_GROUP + g)] for g in range(B_GROUP)],
                axis=0) * ATTN_SCALE
            q3_ref[h] = q3

            kc = cmp_ref[0, h]
            vc = cmp_ref[0, B_KV_HEADS + h]
            c_end = (lax.broadcasted_iota(jnp.int32, (_G_ROWS, n_cmp_rows), 1) * NSA_CMP_STRIDE
                     + NSA_CMP_BLOCK - 1)
            ok = (c_end <= t_pos3) & (c_end < seq)
            lc = jnp.where(ok, _nt(q3, kc), NEG_INF)
            ec = jnp.exp(lc - jnp.max(lc, axis=-1, keepdims=True))
            pc = ec / jnp.sum(ec, axis=-1, keepdims=True) * ok.astype(F32)
            ocmp_ref[h] = jnp.dot(pc.astype(BF16), vc, preferred_element_type=F32)
            p_sum = pc[0:T_TILE] + pc[T_TILE:2 * T_TILE] + pc[2 * T_TILE:3 * T_TILE]
            imp_t = _nt(overlap_t, p_sum, precision=lax.Precision.HIGHEST)

            j_sub = lax.broadcasted_iota(jnp.int32, (_N_SLC, T_TILE), 0)
            jt = (r0 + lax.broadcasted_iota(jnp.int32, (_N_SLC, T_TILE), 1)) // NSA_SLC_BLOCK
            forced = (j_sub == 0) | (j_sub == jt) | (j_sub == jt - 1)
            valid = j_sub <= jt
            score = jnp.where(valid, imp_t + FORCE_BONUS * forced.astype(F32), NEG_INF)
            sel_t = jnp.where(valid & _top_mask_t(score, _N_SLC, NSA_SLC_TOPK), 1.0, 0.0)
            sel3 = _nt(eye_ref[...], sel_t.astype(BF16))
            qaug_ref[h] = _augment_q(q3, sel3)

        def bias_tile(h):
            return lambda d: bias_ref[h, :, (2 - d) * T_TILE:(3 - d) * T_TILE]

        def scratch(h):
            return s_ref.at[h], rm_ref.at[h], mb_ref.at[h], acc_ref.at[h]

        oslc = _two_pass_attention(
            i, [(lambda h=h: qaug_ref[h], lambda kr, h=h: kaug_ref[h, kr, :], vaug_ref.at[h],
                 bias_tile(h)) + scratch(h) for h in range(B_KV_HEADS)], band=False)
        for h in range(B_KV_HEADS):
            oslc_ref[h] = oslc[h]

        owin = _two_pass_attention(
            i, [(lambda h=h: q3_ref[h], lambda kr, h=h: pb_ref[0, kr, head_cols(kw0, h)],
                 vwaug_ref.at[h], bias_tile(h)) + scratch(h) for h in range(B_KV_HEADS)], band=True)

        for h in range(B_KV_HEADS):
            for g in range(B_GROUP):
                head = h * B_GROUP + g
                gr = slice(g * T_TILE, (g + 1) * T_TILE)
                o = (sig[:, 3 * head:3 * head + 1] * ocmp_ref[h, gr, :]
                     + sig[:, 3 * head + 1:3 * head + 2] * oslc_ref[h, gr, :]
                     + sig[:, 3 * head + 2:3 * head + 3] * owin[h][gr, :])
                o_ref[0, rows, head * HEAD_DIM:(head + 1) * HEAD_DIM] = o.astype(o_ref.dtype)
        return carry

    lax.fori_loop(0, n_tiles, tile_body, 0)


def _nsa(pb, cmp, bg, bias):
    b, seq, _ = pb.shape
    return pl.pallas_call(
        functools.partial(_nsa_kernel, seq=seq),
        out_shape=jax.ShapeDtypeStruct((b, seq, B_HEADS * HEAD_DIM), BF16),
        grid=(b,),
        in_specs=[pl.BlockSpec((1, seq, _PB_COLS), lambda i: (i, 0, 0)),
                  pl.BlockSpec((1,) + cmp.shape[1:], lambda i: (i, 0, 0, 0)),
                  pl.BlockSpec((1, seq, _BG_COLS), lambda i: (i, 0, 0)),
                  pl.BlockSpec(bias.shape, lambda i: (0, 0, 0))],
        out_specs=pl.BlockSpec((1, seq, B_HEADS * HEAD_DIM), lambda i: (i, 0, 0)),
        scratch_shapes=[pltpu.VMEM((B_KV_HEADS, seq, LANE), BF16),
                        pltpu.VMEM((B_KV_HEADS, seq, LANE), BF16),
                        pltpu.VMEM((B_KV_HEADS, seq, LANE), BF16),
                        pltpu.VMEM((_G_ROWS, T_TILE), BF16),
                        pltpu.VMEM((B_KV_HEADS, _G_ROWS, HEAD_DIM), BF16),
                        pltpu.VMEM((B_KV_HEADS, _G_ROWS, LANE), BF16),
                        pltpu.VMEM((B_KV_HEADS, seq // T_TILE, _G_ROWS, T_TILE), F32),
                        pltpu.VMEM((B_KV_HEADS, _G_ROWS, T_TILE), F32),
                        pltpu.VMEM((B_KV_HEADS, _G_ROWS, T_TILE), F32),
                        pltpu.VMEM((B_KV_HEADS, _G_ROWS, LANE), F32),
                        pltpu.VMEM((B_KV_HEADS, _G_ROWS, HEAD_DIM), F32),
                        pltpu.VMEM((B_KV_HEADS, _G_ROWS, HEAD_DIM), F32)],
        compiler_params=_params(1),
        name="nsa",
    )(pb, cmp, bg, bias)


def _moba_kernel(pc_ref, bias_ref, o_ref, kaug_ref, vaug_ref, kmean_ref, eye_ref, qaug_ref, s_ref,
                 rm_ref, mb_ref, acc_ref, *, seq):
    n_blk = seq // MOBA_BLOCK
    width = C_HEADS * HEAD_DIM
    eye_ref[...] = (lax.broadcasted_iota(jnp.int32, (T_TILE, T_TILE), 0)
                    == lax.broadcasted_iota(jnp.int32, (T_TILE, T_TILE), 1)).astype(BF16)

    def cols(j, h):
        return slice(j * width + h * HEAD_DIM, j * width + (h + 1) * HEAD_DIM)

    for h in range(C_HEADS):
        kaug_ref[h] = _with_block_onehot(pc_ref[0, :, cols(1, h)], MOBA_BLOCK)
        vaug_ref[h] = _with_ones_column(pc_ref[0, :, cols(2, h)])
        kmean_ref[h] = jnp.concatenate(
            [jnp.mean(pc_ref[0, n * MOBA_BLOCK:(n + 1) * MOBA_BLOCK, cols(1, h)].astype(F32), axis=0,
                      keepdims=True) for n in range(n_blk)]
            + [jnp.zeros((SEL_PAD - n_blk, HEAD_DIM), F32)], axis=0)

    def tile_body(i, carry):
        rows = pl.ds(pl.multiple_of(i * MOBA_BLOCK, MOBA_BLOCK), MOBA_BLOCK)
        n_sub = lax.broadcasted_iota(jnp.int32, (SEL_PAD, MOBA_BLOCK), 0)
        past = n_sub < i
        for h in range(C_HEADS):
            q = pc_ref[0, rows, cols(0, h)]
            gate_t = _nt(kmean_ref[h], q.astype(F32), precision=lax.Precision.HIGHEST)
            score = jnp.where(past, gate_t, NEG_INF)
            picked = past & _top_mask_t(score, n_blk, MOBA_TOPK)
            allowed_t = jnp.where(picked | (n_sub == i), 1.0, 0.0)
            allowed = _nt(eye_ref[...], allowed_t.astype(BF16))
            qaug_ref[h] = _augment_q(q * ATTN_SCALE, allowed)

        outs = _two_pass_attention(
            i, [(lambda h=h: qaug_ref[h], lambda kr, h=h: kaug_ref[h, kr, :], vaug_ref.at[h],
                 lambda d, h=h: bias_ref[h, :, (1 - d) * T_TILE:(2 - d) * T_TILE],
                 s_ref.at[h], rm_ref.at[h], mb_ref.at[h], acc_ref.at[h]) for h in range(C_HEADS)],
            band=False)
        o_ref[0, rows, :] = jnp.concatenate(outs, axis=-1).astype(o_ref.dtype)
        return carry

    lax.fori_loop(0, n_blk, tile_body, 0)


def _moba(pc, bias):
    b, seq, _ = pc.shape
    width = C_HEADS * HEAD_DIM
    return pl.pallas_call(
        functools.partial(_moba_kernel, seq=seq),
        out_shape=jax.ShapeDtypeStruct((b, seq, width), BF16),
        grid=(b,),
        in_specs=[pl.BlockSpec((1, seq, _C_COLS), lambda i: (i, 0, 0)),
                  pl.BlockSpec(bias.shape, lambda i: (0, 0, 0))],
        out_specs=pl.BlockSpec((1, seq, width), lambda i: (i, 0, 0)),
        scratch_shapes=[pltpu.VMEM((C_HEADS, seq, LANE), BF16),
                        pltpu.VMEM((C_HEADS, seq, LANE), BF16),
                        pltpu.VMEM((C_HEADS, SEL_PAD, HEAD_DIM), F32),
                        pltpu.VMEM((T_TILE, T_TILE), BF16),
                        pltpu.VMEM((C_HEADS, T_TILE, LANE), BF16),
                        pltpu.VMEM((C_HEADS, seq // T_TILE, T_TILE, T_TILE), F32),
                        pltpu.VMEM((C_HEADS, T_TILE, T_TILE), F32),
                        pltpu.VMEM((C_HEADS, T_TILE, T_TILE), F32),
                        pltpu.VMEM((C_HEADS, T_TILE, LANE), F32)],
        compiler_params=_params(1),
        name="moba",
    )(pc, bias)


def _merge_kernel(oa_ref, ob_ref, oc_ref, mg_ref, x_ref, wb_ref, wo_ref, y_ref):
    r0 = A_HEADS * HEAD_DIM
    r1 = r0 + B_HEADS * HEAD_DIM
    ya = jnp.dot(oa_ref[...], wb_ref[0:r0, :], preferred_element_type=F32)
    yb = jnp.dot(ob_ref[...], wb_ref[r0:r1, :], preferred_element_type=F32)
    yc = jnp.dot(oc_ref[...], wb_ref[r1:, :], preferred_element_type=F32)
    merged = (jax.nn.sigmoid(mg_ref[:, 0:D_MODEL]) * ya
              + jax.nn.sigmoid(mg_ref[:, D_MODEL:2 * D_MODEL]) * yb
              + jax.nn.sigmoid(mg_ref[:, 2 * D_MODEL:]) * yc)
    y_ref[...] = x_ref[...] + jnp.dot(merged.astype(BF16), wo_ref[...], preferred_element_type=F32)


def _merge(oa, ob, oc, mg, x2, wb, wo):
    t = x2.shape[0]
    tm = ROW_TILE
    row = lambda i: (i, 0)
    return pl.pallas_call(
        _merge_kernel,
        out_shape=jax.ShapeDtypeStruct((t, D_MODEL), F32),
        grid=(t // tm,),
        in_specs=[pl.BlockSpec((tm, oa.shape[1]), row),
                  pl.BlockSpec((tm, ob.shape[1]), row), pl.BlockSpec((tm, oc.shape[1]), row),
                  pl.BlockSpec((tm, _MG_COLS), row), pl.BlockSpec((tm, D_MODEL), row),
                  pl.BlockSpec(wb.shape, lambda i: (0, 0)), pl.BlockSpec(wo.shape, lambda i: (0, 0))],
        out_specs=pl.BlockSpec((tm, D_MODEL), row),
        compiler_params=_params(1),
        name="merge",
    )(oa, ob, oc, mg, x2, wb, wo)


_FF_CHUNK = D_FF // 2


def _ffn_kernel(x_ref, halo_ref, g_ref, wu_ref, cw_ref, cb_ref, wd_ref, gf_ref, y_ref, xn_ref,
                *, tiles_per_seq, final_norm):
    i = pl.program_id(0)
    x = x_ref[...]
    g = g_ref[...]
    keep = (i % tiles_per_seq != 0).astype(F32)
    xn_ref[0:FFN_HALO, :] = (_rmsnorm_rows(halo_ref[...], g) * keep).astype(BF16)
    xn_ref[FFN_HALO:, :] = _rmsnorm_rows(x, g).astype(BF16)
    xn = xn_ref[...]
    rows = xn.shape[0]

    def conv(c0):
        hcol = jnp.dot(xn, wu_ref[:, c0:c0 + _FF_CHUNK], preferred_element_type=F32)
        out = (cw_ref[2:3, c0:c0 + _FF_CHUNK] * hcol
               + cw_ref[1:2, c0:c0 + _FF_CHUNK] * pltpu.roll(hcol, 1, axis=0)
               + cw_ref[0:1, c0:c0 + _FF_CHUNK] * pltpu.roll(hcol, 2, axis=0)
               + cb_ref[:, c0:c0 + _FF_CHUNK])
        return out[FFN_HALO:rows]

    acc = x
    for c in range(0, D_FF, _FF_CHUNK):
        a = conv(c)
        u = conv(D_FF + c)
        act = (a * jax.nn.sigmoid(a) * u).astype(BF16)
        acc = acc + jnp.dot(act, wd_ref[c:c + _FF_CHUNK, :], preferred_element_type=F32)
    if final_norm:
        acc = _rmsnorm_rows(acc, gf_ref[...])
    y_ref[...] = acc


def _ffn(x2, g, wu, cw, cb, wd, gf, *, seq, final_norm):
    t = x2.shape[0]
    tm = ROW_TILE
    const = lambda i: (0, 0)
    halo_blocks = tm // FFN_HALO
    return pl.pallas_call(
        functools.partial(_ffn_kernel, tiles_per_seq=seq // tm, final_norm=final_norm),
        out_shape=jax.ShapeDtypeStruct((t, D_MODEL), F32),
        grid=(t // tm,),
        in_specs=[pl.BlockSpec((tm, D_MODEL), lambda i: (i, 0)),
                  pl.BlockSpec((FFN_HALO, D_MODEL), lambda i: (jnp.maximum(i * halo_blocks - 1, 0), 0)),
                  pl.BlockSpec((1, D_MODEL), const),
                  pl.BlockSpec(wu.shape, const), pl.BlockSpec(cw.shape, const),
                  pl.BlockSpec(cb.shape, const), pl.BlockSpec(wd.shape, const),
                  pl.BlockSpec((1, D_MODEL), const)],
        out_specs=pl.BlockSpec((tm, D_MODEL), lambda i: (i, 0)),
        scratch_shapes=[pltpu.VMEM((tm + FFN_HALO, D_MODEL), BF16)],
        compiler_params=_params(1),
        name="conv_ffn",
    )(x2, x2, g, wu, cw, cb, wd, gf)


def kernel(x, rel_bias, norm_mix, w_in, cmp_pe_k, cmp_w1_k, cmp_w2_k, cmp_pe_v, cmp_w1_v, cmp_w2_v,
           w_branch, w_out, norm_ffn, w_up, conv_w, conv_b, w_down, norm_final):
    b, s, d = x.shape
    depth = w_in.shape[0]
    t = b * s
    assert d == D_MODEL and s % T_TILE == 0 and s // NSA_SLC_BLOCK == _N_SLC and t % ROW_TILE == 0
    assert s % ROW_TILE == 0 and all(s % (dil * A_TILE) == 0 for _, dil in A_GROUPS)
    assert all(win // dil == A_TILE for win, dil in A_GROUPS) and A_GROUPS[0][1] == 1
    assert all(ROW_TILE % dil == 0 for _, dil in A_GROUPS)

    hg = A_HEADS_PER_GROUP
    bias_a = jnp.concatenate([
        _bias_tiles(rel_bias,
                    np.concatenate([_bucket_tile(A_TILE, 1, dil, win // dil),
                                    _bucket_tile(A_TILE, 0, dil, win // dil)], axis=1),
                    gi * hg, hg)
        for gi, (win, dil) in enumerate(A_GROUPS)], axis=0)
    assert (_bucket_tile(T_TILE, 2, 1, 3 * T_TILE) == REL_BUCKETS - 1).all()
    idx_b = np.concatenate([_bucket_tile(T_TILE, off, 1, NSA_WINDOW - 1) for off in (2, 1, 0)], axis=1)
    bias_b = _bias_tiles(rel_bias, idx_b, A_HEADS, B_HEADS, relative=True).reshape(
        B_KV_HEADS, _G_ROWS, 3 * T_TILE)
    idx_c = np.concatenate([_bucket_tile(T_TILE, off, 1, s) for off in (1, 0)], axis=1)
    bias_c = _bias_tiles(rel_bias, idx_c, A_HEADS + B_HEADS, C_HEADS, relative=True)

    w_in_k = _prep_w_in(w_in)
    x2 = x.reshape(t, d)
    for i in range(depth):
        pa0, pa1, pa2, pb, pcmp, pc, mg, bg = _proj_in(x2, norm_mix[i][None, :], w_in_k, layer=i,
                                                       seq=s)

        oa = _mixer_a(pa0.reshape(b, s, _GRP_COLS), pa1, pa2, bias_a)

        cmp = _compress(pcmp.reshape(4, b, s // NSA_CMP_STRIDE, NSA_CMP_STRIDE * HEAD_DIM),
                        cmp_pe_k[i].reshape(1, -1), cmp_pe_v[i].reshape(1, -1),
                        cmp_w1_k[i].astype(BF16), cmp_w1_v[i].astype(BF16),
                        cmp_w2_k[i].astype(BF16), cmp_w2_v[i].astype(BF16))
        ob = _nsa(pb.reshape(b, s, _PB_COLS), cmp, bg.reshape(b, s, _BG_COLS), bias_b)

        oc = _moba(pc.reshape(b, s, _C_COLS), bias_c)

        x2 = _merge(oa.reshape(t, -1), ob.reshape(t, -1), oc.reshape(t, -1), mg, x2,
                    w_branch[i].astype(BF16), w_out[i].astype(BF16))
        x2 = _ffn(x2, norm_ffn[i][None, :], w_up[i].astype(BF16), conv_w[i], conv_b[i][None, :],
                  w_down[i].astype(BF16), norm_final[None, :], seq=s, final_norm=(i == depth - 1))
    return x2.reshape(b, s, d)
```

```python
import functools
import math

import jax
import jax.numpy as jnp
import numpy as np
from jax import lax
from jax.experimental import pallas as pl
from jax.experimental.pallas import tpu as pltpu

F32 = jnp.float32
BF16 = jnp.bfloat16

D_MODEL = 1024
HEAD_DIM = 64
A_GROUPS = ((128, 1), (512, 4), (2048, 16))
A_HEADS_PER_GROUP = 2
A_HEADS = 6
B_HEADS = 6
B_KV_HEADS = 2
B_GROUP = 3
C_HEADS = 4
NSA_CMP_BLOCK = 32
NSA_CMP_STRIDE = 16
NSA_CMP_HIDDEN = 256
NSA_SLC_BLOCK = 64
NSA_SLC_TOPK = 16
NSA_WINDOW = 512
MOBA_BLOCK = 256
MOBA_TOPK = 3
REL_BUCKETS = 32
REL_MAX_DIST = 128
D_FF = 2816
RMS_EPS = 1e-6
NEG_INF = -1e30
FORCE_BONUS = 1e4
ATTN_SCALE = HEAD_DIM ** -0.5

LANE = 128
A_TILE = 128
T_TILE = 256
ROW_TILE = 512
FFN_HALO = 8
VMEM_LIMIT = 56 * 1024 * 1024

_GRP_COLS = 3 * A_HEADS_PER_GROUP * HEAD_DIM
_A_COLS = len(A_GROUPS) * _GRP_COLS
_PB_COLS = B_HEADS * HEAD_DIM + 4 * B_KV_HEADS * HEAD_DIM
_CMP_COLS = 2 * B_KV_HEADS * HEAD_DIM
_C_COLS = 3 * C_HEADS * HEAD_DIM
_MG_COLS = 3 * D_MODEL
_BG_COLS = LANE
_OFF_A = 0
_OFF_PB = _OFF_A + _A_COLS
_OFF_CMP = _OFF_PB + _PB_COLS
_OFF_C = _OFF_CMP + _CMP_COLS
_OFF_MG = _OFF_C + _C_COLS
_OFF_BG = _OFF_MG + _MG_COLS
_W_COLS = _OFF_BG + _BG_COLS


def _params(n_grid):
    return pltpu.CompilerParams(dimension_semantics=("arbitrary",) * n_grid,
                                vmem_limit_bytes=VMEM_LIMIT)


def _rel_bucket_np(dist):
    n = np.maximum(dist, 0)
    exact = REL_BUCKETS // 2
    nf = np.maximum(n, 1).astype(np.float32)
    large = exact + (np.log(nf / np.float32(exact)) / np.float32(math.log(REL_MAX_DIST / exact))
                     * np.float32(REL_BUCKETS - exact)).astype(np.int32)
    return np.where(n < exact, n, np.minimum(large, REL_BUCKETS - 1)).astype(np.int32)


def _bucket_tile(tile, block_offset, dil, max_rel):
    rel = block_offset * tile + np.arange(tile)[:, None] - np.arange(tile)[None, :]
    ok = (rel >= 0) & (rel <= max_rel)
    return np.where(ok, _rel_bucket_np(rel * dil), -1).astype(np.int32)


def _bias_tiles_kernel(tbl_ref, idx_ref, o_ref, *, head0, relative):
    h = pl.program_id(0) + head0
    idx = idx_ref[...]
    acc = jnp.full(idx.shape, NEG_INF, F32)
    for b in range(REL_BUCKETS):
        acc = jnp.where(idx == b, tbl_ref[b, h], acc)
    if relative:
        acc = acc - tbl_ref[REL_BUCKETS - 1, h]
    o_ref[0] = acc


def _bias_tiles(rel_bias, idx, head0, n_heads, relative=False):
    r, c = idx.shape
    return pl.pallas_call(
        functools.partial(_bias_tiles_kernel, head0=head0, relative=relative),
        out_shape=jax.ShapeDtypeStruct((n_heads, r, c), F32),
        grid=(n_heads,),
        in_specs=[pl.BlockSpec(memory_space=pltpu.SMEM),
                  pl.BlockSpec((r, c), lambda h: (0, 0))],
        out_specs=pl.BlockSpec((1, r, c), lambda h: (h, 0, 0)),
        compiler_params=_params(1),
        name="bias_tiles",
    )(rel_bias, jnp.asarray(idx))


def _bias_tiles_t(rel_bias, idx_tiles, head0, n_heads):
    r, c = idx_tiles[0].shape
    flat = _bias_tiles(rel_bias, np.concatenate([t.T for t in idx_tiles], axis=1), head0, n_heads,
                       relative=True)
    return flat.reshape(n_heads, c, len(idx_tiles), r).transpose(0, 2, 1, 3)


def _rmsnorm_rows(x, g):
    return x * lax.rsqrt(jnp.mean(x * x, axis=-1, keepdims=True) + RMS_EPS) * g


def _proj_in_kernel(x_ref, g_ref, w_ref, pa0_ref, pa1_ref, pa2_ref, pb_ref, pcmp_ref, pc_ref,
                    mg_ref, bg_ref, regroup_ref):
    xn = _rmsnorm_rows(x_ref[...], g_ref[...]).astype(BF16)
    tm = xn.shape[0]

    def mm(c0, c1):
        return jnp.dot(xn, w_ref[:, c0:c1], preferred_element_type=F32)

    pa0_ref[...] = mm(_OFF_A, _OFF_A + _GRP_COLS).astype(BF16)
    for gi, out_ref in ((1, pa1_ref), (2, pa2_ref)):
        dil = A_GROUPS[gi][1]
        res = mm(_OFF_A + gi * _GRP_COLS, _OFF_A + (gi + 1) * _GRP_COLS)
        for j in range(3):
            regroup_ref[j] = res[:, j * LANE:(j + 1) * LANE]
        for r in range(dil):
            for j in range(3):
                out_ref[0, r, :, j * LANE:(j + 1) * LANE] = (
                    regroup_ref[j, pl.ds(r, tm // dil, stride=dil), :].astype(BF16))
    pb_ref[:, 0:384] = mm(_OFF_PB, _OFF_PB + 384).astype(BF16)
    pb_ref[:, 384:896] = mm(_OFF_PB + 384, _OFF_PB + 896).astype(BF16)
    cmp = mm(_OFF_CMP, _OFF_CMP + _CMP_COLS).astype(BF16)
    for i in range(4):
        pcmp_ref[i] = cmp[:, i * HEAD_DIM:(i + 1) * HEAD_DIM]
    for c in range(0, _C_COLS, 384):
        pc_ref[:, c:c + 384] = mm(_OFF_C + c, _OFF_C + c + 384).astype(BF16)
    for c in range(0, _MG_COLS, 512):
        mg_ref[:, c:c + 512] = mm(_OFF_MG + c, _OFF_MG + c + 512)
    bg_ref[...] = mm(_OFF_BG, _OFF_BG + _BG_COLS)


def _proj_in(x2, g, w, *, layer, seq):
    t = x2.shape[0]
    tm = ROW_TILE
    b = t // seq
    tps = seq // tm
    row = lambda i: (i, 0)
    d1, d2 = A_GROUPS[1][1], A_GROUPS[2][1]
    sub = lambda i: (i // tps, 0, i % tps, 0)
    return pl.pallas_call(
        _proj_in_kernel,
        out_shape=(jax.ShapeDtypeStruct((t, _GRP_COLS), BF16),
                   jax.ShapeDtypeStruct((b, d1, seq // d1, _GRP_COLS), BF16),
                   jax.ShapeDtypeStruct((b, d2, seq // d2, _GRP_COLS), BF16),
                   jax.ShapeDtypeStruct((t, _PB_COLS), BF16),
                   jax.ShapeDtypeStruct((4, t, HEAD_DIM), BF16),
                   jax.ShapeDtypeStruct((t, _C_COLS), BF16),
                   jax.ShapeDtypeStruct((t, _MG_COLS), F32),
                   jax.ShapeDtypeStruct((t, _BG_COLS), F32)),
        grid=(t // tm,),
        in_specs=[pl.BlockSpec((tm, D_MODEL), row),
                  pl.BlockSpec((1, D_MODEL), lambda i: (0, 0)),
                  pl.BlockSpec((None, D_MODEL, _W_COLS), lambda i: (layer, 0, 0))],
        out_specs=(pl.BlockSpec((tm, _GRP_COLS), row),
                   pl.BlockSpec((1, d1, tm // d1, _GRP_COLS), sub),
                   pl.BlockSpec((1, d2, tm // d2, _GRP_COLS), sub),
                   pl.BlockSpec((tm, _PB_COLS), row),
                   pl.BlockSpec((4, tm, HEAD_DIM), lambda i: (0, i, 0)),
                   pl.BlockSpec((tm, _C_COLS), row),
                   pl.BlockSpec((tm, _MG_COLS), row),
                   pl.BlockSpec((tm, _BG_COLS), row)),
        scratch_shapes=[pltpu.VMEM((3, tm, LANE), F32)],
        compiler_params=_params(1),
        name="proj_in",
    )(x2, g, w)


def _w_in_segments():
    a = A_HEADS * HEAD_DIM
    bq0 = 3 * a
    bkc0 = bq0 + B_HEADS * HEAD_DIM
    bks0 = bkc0 + 2 * B_KV_HEADS * HEAD_DIM
    bg0 = bks0 + 4 * B_KV_HEADS * HEAD_DIM
    c0 = bg0 + B_HEADS * 3
    mg0 = c0 + _C_COLS
    gw = A_HEADS_PER_GROUP * HEAD_DIM
    segs = [(_OFF_A + (gi * 3 + j) * gw, j * a + gi * gw, gw)
            for gi in range(len(A_GROUPS)) for j in range(3)]
    segs += [(_OFF_PB, bq0, B_HEADS * HEAD_DIM),
             (_OFF_PB + B_HEADS * HEAD_DIM, bks0, bg0 - bks0),
             (_OFF_CMP, bkc0, _CMP_COLS), (_OFF_C, c0, _C_COLS), (_OFF_MG, mg0, _MG_COLS),
             (_OFF_BG, bg0, B_HEADS * 3)]
    return segs


def _prep_w_in_kernel(w_ref, o_ref):
    o_ref[0, :, _OFF_BG:] = jnp.zeros((o_ref.shape[1], _BG_COLS), BF16)
    for dst, src, width in _w_in_segments():
        for c in range(0, width, 768):
            n = min(768, width - c)
            o_ref[0, :, dst + c:dst + c + n] = w_ref[0, :, src + c:src + c + n].astype(BF16)


def _prep_w_in(w):
    depth, rows, cols = w.shape
    rb = 128
    return pl.pallas_call(
        _prep_w_in_kernel,
        out_shape=jax.ShapeDtypeStruct((depth, rows, _W_COLS), BF16),
        grid=(depth, rows // rb),
        in_specs=[pl.BlockSpec((1, rb, cols), lambda l, r: (l, r, 0))],
        out_specs=pl.BlockSpec((1, rb, _W_COLS), lambda l, r: (l, r, 0)),
        compiler_params=_params(2),
        name="prep_w_in",
    )(w)


_A_UNROLL = 3


def _mixer_a_kernel(pa0_ref, pa1_ref, pa2_ref, bias_ref, o_ref, on_ref, ln_ref, *, seq):
    gw = A_HEADS_PER_GROUP * HEAD_DIM

    def band_block(src, l0, first, gi, rows):
        outs, lses = [], []
        for h in range(A_HEADS_PER_GROUP):
            qc, kc, vc = (slice(j * gw + h * HEAD_DIM, j * gw + (h + 1) * HEAD_DIM) for j in range(3))
            q = src[pl.ds(l0, A_TILE), qc] * ATTN_SCALE
            if first:
                kr = pl.ds(l0, A_TILE)
                bias = bias_ref[A_HEADS_PER_GROUP * gi + h, :, A_TILE:]
            else:
                kr = pl.ds(l0 - A_TILE, 2 * A_TILE)
                bias = bias_ref[A_HEADS_PER_GROUP * gi + h]
            s = _nt(q, src[kr, kc]) + bias
            m = jnp.max(s, axis=-1, keepdims=True)
            e = jnp.exp(s - m)
            l = jnp.sum(e, axis=-1, keepdims=True)
            outs.append(jnp.dot(e.astype(BF16), src[kr, vc], preferred_element_type=F32) / l)
            lses.append(jnp.broadcast_to(m + jnp.log(l), (A_TILE, HEAD_DIM)))
        on_ref[gi, rows, :] = jnp.concatenate(outs, axis=-1)
        ln_ref[gi, rows, :] = jnp.concatenate(lses, axis=-1)

    for gi, ((_, dil), pa_ref) in enumerate(zip(A_GROUPS, (pa0_ref, pa1_ref, pa2_ref))):
        n_blocks = seq // dil // A_TILE
        if dil == 1:
            band_block(pa_ref.at[0], 0, True, gi, pl.ds(0, A_TILE))
            assert (n_blocks - 1) % _A_UNROLL == 0

            def body0(it, c, pa_ref=pa_ref, gi=gi):
                for u in range(_A_UNROLL):
                    l0 = pl.multiple_of((1 + it * _A_UNROLL + u) * A_TILE, A_TILE)
                    band_block(pa_ref.at[0], l0, False, gi, pl.ds(l0, A_TILE))
                return c
            lax.fori_loop(0, (n_blocks - 1) // _A_UNROLL, body0, 0)
        elif n_blocks > 1:
            for r in range(dil):
                band_block(pa_ref.at[0, r], 0, True, gi, pl.ds(r, A_TILE, stride=dil))

            def body1(i, c, pa_ref=pa_ref, gi=gi, dil=dil):
                l0 = pl.multiple_of(i * A_TILE, A_TILE)
                for r in range(dil):
                    band_block(pa_ref.at[0, r], l0, False, gi, pl.ds(l0 * dil + r, A_TILE, stride=dil))
                return c
            lax.fori_loop(1, n_blocks, body1, 0)
        else:
            per_iter = 4

            def body2(it, c, pa_ref=pa_ref, gi=gi, dil=dil):
                for u in range(per_iter):
                    r = it * per_iter + u
                    band_block(pa_ref.at[0, r], 0, True, gi, pl.ds(r, A_TILE, stride=dil))
                return c
            lax.fori_loop(0, dil // per_iter, body2, 0)

    chunk = T_TILE
    for c0 in range(0, seq, chunk):
        rows = slice(c0, c0 + chunk)
        lse = [ln_ref[g, rows, :] for g in range(len(A_GROUPS))]
        top = functools.reduce(jnp.maximum, lse)
        ex = [jnp.exp(v - top) for v in lse]
        den = functools.reduce(lambda a, b: a + b, ex)
        for g in range(len(A_GROUPS)):
            o_ref[0, rows, g * gw:(g + 1) * gw] = (ex[g] / den * on_ref[g, rows, :]).astype(BF16)


def _mixer_a(pa0, pa1, pa2, bias):
    b, seq, _ = pa0.shape
    return pl.pallas_call(
        functools.partial(_mixer_a_kernel, seq=seq),
        out_shape=jax.ShapeDtypeStruct((b, seq, _GRP_COLS), BF16),
        grid=(b,),
        in_specs=[pl.BlockSpec((1, seq, _GRP_COLS), lambda i: (i, 0, 0)),
                  pl.BlockSpec((1,) + pa1.shape[1:], lambda i: (i, 0, 0, 0)),
                  pl.BlockSpec((1,) + pa2.shape[1:], lambda i: (i, 0, 0, 0)),
                  pl.BlockSpec(bias.shape, lambda i: (0, 0, 0))],
        out_specs=pl.BlockSpec((1, seq, _GRP_COLS), lambda i: (i, 0, 0)),
        scratch_shapes=[pltpu.VMEM((len(A_GROUPS), seq, LANE), F32),
                        pltpu.VMEM((len(A_GROUPS), seq, LANE), F32)],
        compiler_params=_params(1),
        name="mixer_a",
    )(pa0, pa1, pa2, bias)


def _compress_kernel(r_ref, pek_ref, pev_ref, w1k_ref, w1v_ref, w2k_ref, w2v_ref, o_ref):
    half = NSA_CMP_STRIDE * HEAD_DIM
    for kv, (pe_ref, w1_ref, w2_ref) in enumerate(((pek_ref, w1k_ref, w2k_ref),
                                                   (pev_ref, w1v_ref, w2v_ref))):
        for h in range(B_KV_HEADS):
            r = r_ref[kv * B_KV_HEADS + h, 0].astype(F32)
            lo = jnp.dot((r + pe_ref[:, :half]).astype(BF16), w1_ref[:half, :],
                         preferred_element_type=F32)
            hi = jnp.dot((r + pe_ref[:, half:]).astype(BF16), w1_ref[half:, :],
                         preferred_element_type=F32)
            hid = lo + pltpu.roll(hi, hi.shape[0] - 1, axis=0)
            act = jax.nn.gelu(hid).astype(BF16)
            o_ref[0, kv * B_KV_HEADS + h] = jnp.dot(act, w2_ref[...],
                                                    preferred_element_type=F32).astype(BF16)


def _compress(r, pe_k, pe_v, w1_k, w1_v, w2_k, w2_v):
    _, b, m, c = r.shape
    const2 = lambda i: (0, 0)
    return pl.pallas_call(
        _compress_kernel,
        out_shape=jax.ShapeDtypeStruct((b, 4, m, HEAD_DIM), BF16),
        grid=(b,),
        in_specs=[pl.BlockSpec((4, 1, m, c), lambda i: (0, i, 0, 0)),
                  pl.BlockSpec(pe_k.shape, const2), pl.BlockSpec(pe_v.shape, const2),
                  pl.BlockSpec(w1_k.shape, const2), pl.BlockSpec(w1_v.shape, const2),
                  pl.BlockSpec(w2_k.shape, const2), pl.BlockSpec(w2_v.shape, const2)],
        out_specs=pl.BlockSpec((1, 4, m, HEAD_DIM), lambda i: (i, 0, 0, 0)),
        compiler_params=_params(1),
        name="nsa_compress",
    )(r, pe_k, pe_v, w1_k, w1_v, w2_k, w2_v)


MASK_BIG = 2.0 ** 100
SEL_PAD = 32


def _nt(a, b, **kw):
    return lax.dot_general(a, b, (((1,), (1,)), ((), ())), preferred_element_type=F32, **kw)


def _top_mask_t(score, n_rows, n_top):
    sub = lax.broadcasted_iota(jnp.int32, score.shape, 0)
    rank = jnp.zeros(score.shape, F32)
    for j in range(n_rows):
        row = score[j:j + 1, :]
        ge = jnp.where(row >= score, 1.0, 0.0)
        gt = jnp.where(row > score, 1.0, 0.0)
        rank = rank + jnp.where(sub > j, ge, gt)
    return rank < n_top


def _two_pass_attention(i, streams, *, band):
    def key_rows(j):
        return pl.ds(pl.multiple_of(j * T_TILE, T_TILE), T_TILE)

    def logits(j, d, first=False):
        for k_tile, q_t, _, bias_tile, s_ref, rmp_ref, _ in streams:
            s = jnp.dot(k_tile(key_rows(j)), q_t(), preferred_element_type=F32)
            if d is not None:
                s = s + bias_tile(d)
            s_ref[j] = s
            fold = jnp.max(s.reshape(s.shape[0] // 8, 8, s.shape[1]), axis=0)
            rmp_ref[...] = fold if first else jnp.maximum(rmp_ref[...], fold)

    logits(i, 0, first=True)

    @pl.when(i >= 1)
    def _():
        logits(i - 1, 1)

    if band:
        @pl.when(i >= 2)
        def _():
            logits(i - 2, 2)
        lo = jnp.maximum(i - 2, 0)
    else:
        def far(j, c):
            logits(j, None)
            return c
        lax.fori_loop(0, i - 1, far, 0)
        lo = 0

    tops = []
    for *_, rmp_ref, acc_ref in streams:
        tops.append(jnp.max(rmp_ref[...], axis=0, keepdims=True))
        acc_ref[...] = jnp.zeros(acc_ref.shape, F32)

    def pv(j, c):
        for (_, _, vt_ref, _, s_ref, _, acc_ref), top in zip(streams, tops):
            e = jnp.exp(s_ref[j] - top)
            acc_ref[...] += jnp.dot(vt_ref[j], e.astype(BF16), preferred_element_type=F32)
        return c
    lax.fori_loop(lo, i + 1, pv, 0)

    outs = []
    for *_, acc_ref in streams:
        acc = acc_ref[...]
        outs.append(acc[:HEAD_DIM, :] / acc[HEAD_DIM:HEAD_DIM + 1, :])
    return outs


def _eye(n, dtype=BF16):
    return (lax.broadcasted_iota(jnp.int32, (n, n), 0)
            == lax.broadcasted_iota(jnp.int32, (n, n), 1)).astype(dtype)


def _transposed_values(v_ref_rows, n_tiles):
    eye = _eye(LANE)
    for j in range(n_tiles):
        v = _with_ones_column(v_ref_rows(slice(j * T_TILE, (j + 1) * T_TILE)))
        yield j, _nt(eye, v).astype(BF16)


def _with_ones_column(v):
    one = (lax.broadcasted_iota(jnp.int32, v.shape, 1) == 0).astype(F32)
    return jnp.concatenate([v.astype(F32), one], axis=1).astype(BF16)


def _with_block_onehot(k, block):
    n = k.shape[0]
    shape = (n, SEL_PAD)
    hot = (lax.broadcasted_iota(jnp.int32, shape, 0) // block
           == lax.broadcasted_iota(jnp.int32, shape, 1)).astype(F32)
    return jnp.concatenate([k.astype(F32), hot, jnp.zeros(shape, F32)], axis=1).astype(BF16)


def _augment_q_t(q_t, allowed_t):
    pen = (allowed_t - 1.0) * MASK_BIG
    return jnp.concatenate([q_t, pen, jnp.zeros(pen.shape, F32)], axis=0).astype(BF16)


_N_SLC = 32
_G_ROWS = B_GROUP * T_TILE


def _nsa_kernel(pb_ref, cmp_ref, bg_ref, bias_ref, o_ref,
                kaug_ref, vst_ref, vwt_ref, eye_ref, q3t_ref, qaugt_ref, s_ref, rmp_ref,
                acc_ref, ocmp_ref, oslc_ref, *, seq):
    n_tiles = seq // T_TILE
    n_cmp_rows = seq // NSA_CMP_STRIDE
    q_cols = B_HEADS * HEAD_DIM
    ks0, vs0, kw0, vw0 = (q_cols + i * B_KV_HEADS * HEAD_DIM for i in range(4))

    def head_cols(c0, h):
        return slice(c0 + h * HEAD_DIM, c0 + (h + 1) * HEAD_DIM)

    j_id = lax.broadcasted_iota(jnp.int32, (_N_SLC, n_cmp_rows), 0)
    c_id = lax.broadcasted_iota(jnp.int32, (_N_SLC, n_cmp_rows), 1)
    overlap_t = ((c_id * NSA_CMP_STRIDE < (j_id + 1) * NSA_SLC_BLOCK)
                 & (c_id * NSA_CMP_STRIDE + NSA_CMP_BLOCK > j_id * NSA_SLC_BLOCK)).astype(F32)
    eye_ref[...] = _eye(T_TILE)
    for h in range(B_KV_HEADS):
        kaug_ref[h] = _with_block_onehot(pb_ref[0, :, head_cols(ks0, h)], NSA_SLC_BLOCK)
        for j, vt in _transposed_values(lambda r, h=h: pb_ref[0, r, head_cols(vs0, h)], n_tiles):
            vst_ref[h, j] = vt
        for j, vt in _transposed_values(lambda r, h=h: pb_ref[0, r, head_cols(vw0, h)], n_tiles):
            vwt_ref[h, j] = vt

    def tile_body(i, carry):
        r0 = pl.multiple_of(i * T_TILE, T_TILE)
        rows = pl.ds(r0, T_TILE)
        pick = (lax.broadcasted_iota(jnp.int32, (SEL_PAD, _BG_COLS), 0)
                == lax.broadcasted_iota(jnp.int32, (SEL_PAD, _BG_COLS), 1)).astype(F32)
        sig_t = jax.nn.sigmoid(_nt(pick, bg_ref[0, rows, :], precision=lax.Precision.HIGHEST))
        eye_h = _eye(HEAD_DIM)

        for h in range(B_KV_HEADS):
            q3t = jnp.concatenate(
                [_nt(eye_h, pb_ref[0, rows, head_cols(0, h * B_GROUP + g)]) for g in range(B_GROUP)],
                axis=1) * ATTN_SCALE
            q3t_bf = q3t.astype(BF16)
            q3t_ref[h] = q3t_bf

            kc = cmp_ref[0, h]
            vct = _nt(eye_h, cmp_ref[0, B_KV_HEADS + h]).astype(BF16)
            c_end = (lax.broadcasted_iota(jnp.int32, (n_cmp_rows, _G_ROWS), 0) * NSA_CMP_STRIDE
                     + NSA_CMP_BLOCK - 1)
            t_pos = r0 + lax.broadcasted_iota(jnp.int32, (n_cmp_rows, _G_ROWS), 1) % T_TILE
            ok = (c_end <= t_pos) & (c_end < seq)
            lc = jnp.where(ok, jnp.dot(kc, q3t_bf, preferred_element_type=F32), NEG_INF)
            ec = jnp.exp(lc - jnp.max(lc, axis=0, keepdims=True))
            pc = ec / jnp.sum(ec, axis=0, keepdims=True) * ok.astype(F32)
            ocmp_ref[h] = jnp.dot(vct, pc.astype(BF16), preferred_element_type=F32)
            p_sum = pc[:, 0:T_TILE] + pc[:, T_TILE:2 * T_TILE] + pc[:, 2 * T_TILE:3 * T_TILE]
            imp_t = jnp.dot(overlap_t, p_sum, precision=lax.Precision.HIGHEST,
                            preferred_element_type=F32)

            j_sub = lax.broadcasted_iota(jnp.int32, (_N_SLC, T_TILE), 0)
            jt = (r0 + lax.broadcasted_iota(jnp.int32, (_N_SLC, T_TILE), 1)) // NSA_SLC_BLOCK
            forced = (j_sub == 0) | (j_sub == jt) | (j_sub == jt - 1)
            valid = j_sub <= jt
            score = jnp.where(valid, imp_t + FORCE_BONUS * forced.astype(F32), NEG_INF)
            sel_t = jnp.where(valid & _top_mask_t(score, _N_SLC, NSA_SLC_TOPK), 1.0, 0.0)
            qaugt_ref[h] = _augment_q_t(q3t, jnp.concatenate([sel_t] * B_GROUP, axis=1))

        def scratch(h):
            return lambda d: bias_ref[h, d], s_ref.at[h], rmp_ref.at[h], acc_ref.at[h]

        oslc = _two_pass_attention(
            i, [(lambda kr, h=h: kaug_ref[h, kr, :], lambda h=h: qaugt_ref[h], vst_ref.at[h])
                + scratch(h) for h in range(B_KV_HEADS)], band=False)
        for h in range(B_KV_HEADS):
            oslc_ref[h] = oslc[h]

        owin = _two_pass_attention(
            i, [(lambda kr, h=h: pb_ref[0, kr, head_cols(kw0, h)], lambda h=h: q3t_ref[h],
                 vwt_ref.at[h]) + scratch(h) for h in range(B_KV_HEADS)], band=True)

        heads = []
        for h in range(B_KV_HEADS):
            for g in range(B_GROUP):
                head = h * B_GROUP + g
                gc = slice(g * T_TILE, (g + 1) * T_TILE)
                heads.append(sig_t[3 * head:3 * head + 1, :] * ocmp_ref[h, :, gc]
                             + sig_t[3 * head + 1:3 * head + 2, :] * oslc_ref[h, :, gc]
                             + sig_t[3 * head + 2:3 * head + 3, :] * owin[h][:, gc])
        o_t = jnp.concatenate(heads, axis=0).astype(BF16)
        o_ref[0, rows, :] = _nt(eye_ref[...], o_t).astype(o_ref.dtype)
        return carry

    lax.fori_loop(0, n_tiles, tile_body, 0)


def _nsa(pb, cmp, bg, bias):
    b, seq, _ = pb.shape
    n_tiles = seq // T_TILE
    return pl.pallas_call(
        functools.partial(_nsa_kernel, seq=seq),
        out_shape=jax.ShapeDtypeStruct((b, seq, B_HEADS * HEAD_DIM), BF16),
        grid=(b,),
        in_specs=[pl.BlockSpec((1, seq, _PB_COLS), lambda i: (i, 0, 0)),
                  pl.BlockSpec((1,) + cmp.shape[1:], lambda i: (i, 0, 0, 0)),
                  pl.BlockSpec((1, seq, _BG_COLS), lambda i: (i, 0, 0)),
                  pl.BlockSpec(bias.shape, lambda i: (0, 0, 0, 0))],
        out_specs=pl.BlockSpec((1, seq, B_HEADS * HEAD_DIM), lambda i: (i, 0, 0)),
        scratch_shapes=[pltpu.VMEM((B_KV_HEADS, seq, LANE), BF16),
                        pltpu.VMEM((B_KV_HEADS, n_tiles, LANE, T_TILE), BF16),
                        pltpu.VMEM((B_KV_HEADS, n_tiles, LANE, T_TILE), BF16),
                        pltpu.VMEM((T_TILE, T_TILE), BF16),
                        pltpu.VMEM((B_KV_HEADS, HEAD_DIM, _G_ROWS), BF16),
                        pltpu.VMEM((B_KV_HEADS, LANE, _G_ROWS), BF16),
                        pltpu.VMEM((B_KV_HEADS, n_tiles, T_TILE, _G_ROWS), F32),
                        pltpu.VMEM((B_KV_HEADS, 8, _G_ROWS), F32),
                        pltpu.VMEM((B_KV_HEADS, LANE, _G_ROWS), F32),
                        pltpu.VMEM((B_KV_HEADS, HEAD_DIM, _G_ROWS), F32),
                        pltpu.VMEM((B_KV_HEADS, HEAD_DIM, _G_ROWS), F32)],
        compiler_params=_params(1),
        name="nsa",
    )(pb, cmp, bg, bias)


def _moba_kernel(pc_ref, bias_ref, o_ref, kaug_ref, vt_ref, kmean_ref, eye_ref, qaugt_ref, s_ref,
                 rmp_ref, acc_ref, *, seq):
    n_blk = seq // MOBA_BLOCK
    width = C_HEADS * HEAD_DIM
    eye_ref[...] = _eye(T_TILE)

    def cols(j, h):
        return slice(j * width + h * HEAD_DIM, j * width + (h + 1) * HEAD_DIM)

    for h in range(C_HEADS):
        kaug_ref[h] = _with_block_onehot(pc_ref[0, :, cols(1, h)], MOBA_BLOCK)
        for j, vt in _transposed_values(lambda r, h=h: pc_ref[0, r, cols(2, h)], n_blk):
            vt_ref[h, j] = vt
        kmean_ref[h] = jnp.concatenate(
            [jnp.mean(pc_ref[0, n * MOBA_BLOCK:(n + 1) * MOBA_BLOCK, cols(1, h)].astype(F32), axis=0,
                      keepdims=True) for n in range(n_blk)]
            + [jnp.zeros((SEL_PAD - n_blk, HEAD_DIM), F32)], axis=0)

    def tile_body(i, carry):
        rows = pl.ds(pl.multiple_of(i * MOBA_BLOCK, MOBA_BLOCK), MOBA_BLOCK)
        n_sub = lax.broadcasted_iota(jnp.int32, (SEL_PAD, MOBA_BLOCK), 0)
        past = n_sub < i
        eye_h = _eye(HEAD_DIM)
        for h in range(C_HEADS):
            q_t = _nt(eye_h, pc_ref[0, rows, cols(0, h)])
            gate_t = jnp.dot(kmean_ref[h], q_t, precision=lax.Precision.HIGHEST,
                             preferred_element_type=F32)
            score = jnp.where(past, gate_t, NEG_INF)
            picked = past & _top_mask_t(score, n_blk, MOBA_TOPK)
            allowed_t = jnp.where(picked | (n_sub == i), 1.0, 0.0)
            qaugt_ref[h] = _augment_q_t(q_t * ATTN_SCALE, allowed_t)

        outs = _two_pass_attention(
            i, [(lambda kr, h=h: kaug_ref[h, kr, :], lambda h=h: qaugt_ref[h], vt_ref.at[h],
                 lambda d, h=h: bias_ref[h, d], s_ref.at[h], rmp_ref.at[h], acc_ref.at[h])
                for h in range(C_HEADS)], band=False)
        o_t = jnp.concatenate(outs, axis=0).astype(BF16)
        o_ref[0, rows, :] = _nt(eye_ref[...], o_t).astype(o_ref.dtype)
        return carry

    lax.fori_loop(0, n_blk, tile_body, 0)


def _moba(pc, bias):
    b, seq, _ = pc.shape
    width = C_HEADS * HEAD_DIM
    n_tiles = seq // T_TILE
    return pl.pallas_call(
        functools.partial(_moba_kernel, seq=seq),
        out_shape=jax.ShapeDtypeStruct((b, seq, width), BF16),
        grid=(b,),
        in_specs=[pl.BlockSpec((1, seq, _C_COLS), lambda i: (i, 0, 0)),
                  pl.BlockSpec(bias.shape, lambda i: (0, 0, 0, 0))],
        out_specs=pl.BlockSpec((1, seq, width), lambda i: (i, 0, 0)),
        scratch_shapes=[pltpu.VMEM((C_HEADS, seq, LANE), BF16),
                        pltpu.VMEM((C_HEADS, n_tiles, LANE, T_TILE), BF16),
                        pltpu.VMEM((C_HEADS, SEL_PAD, HEAD_DIM), F32),
                        pltpu.VMEM((T_TILE, T_TILE), BF16),
                        pltpu.VMEM((C_HEADS, LANE, T_TILE), BF16),
                        pltpu.VMEM((C_HEADS, n_tiles, T_TILE, T_TILE), F32),
                        pltpu.VMEM((C_HEADS, 8, T_TILE), F32),
                        pltpu.VMEM((C_HEADS, LANE, T_TILE), F32)],
        compiler_params=_params(1),
        name="moba",
    )(pc, bias)


def _merge_kernel(oa_ref, ob_ref, oc_ref, mg_ref, x_ref, wb_ref, wo_ref, y_ref):
    r0 = A_HEADS * HEAD_DIM
    r1 = r0 + B_HEADS * HEAD_DIM
    ya = jnp.dot(oa_ref[...], wb_ref[0:r0, :], preferred_element_type=F32)
    yb = jnp.dot(ob_ref[...], wb_ref[r0:r1, :], preferred_element_type=F32)
    yc = jnp.dot(oc_ref[...], wb_ref[r1:, :], preferred_element_type=F32)
    merged = (jax.nn.sigmoid(mg_ref[:, 0:D_MODEL]) * ya
              + jax.nn.sigmoid(mg_ref[:, D_MODEL:2 * D_MODEL]) * yb
              + jax.nn.sigmoid(mg_ref[:, 2 * D_MODEL:]) * yc)
    y_ref[...] = x_ref[...] + jnp.dot(merged.astype(BF16), wo_ref[...], preferred_element_type=F32)


def _merge(oa, ob, oc, mg, x2, wb, wo):
    t = x2.shape[0]
    tm = ROW_TILE
    row = lambda i: (i, 0)
    return pl.pallas_call(
        _merge_kernel,
        out_shape=jax.ShapeDtypeStruct((t, D_MODEL), F32),
        grid=(t // tm,),
        in_specs=[pl.BlockSpec((tm, oa.shape[1]), row),
                  pl.BlockSpec((tm, ob.shape[1]), row), pl.BlockSpec((tm, oc.shape[1]), row),
                  pl.BlockSpec((tm, _MG_COLS), row), pl.BlockSpec((tm, D_MODEL), row),
                  pl.BlockSpec(wb.shape, lambda i: (0, 0)), pl.BlockSpec(wo.shape, lambda i: (0, 0))],
        out_specs=pl.BlockSpec((tm, D_MODEL), row),
        compiler_params=_params(1),
        name="merge",
    )(oa, ob, oc, mg, x2, wb, wo)


_FF_CHUNK = D_FF // 2


def _ffn_kernel(x_ref, halo_ref, g_ref, wu_ref, cw_ref, cb_ref, wd_ref, gf_ref, y_ref, xn_ref,
                *, tiles_per_seq, final_norm):
    i = pl.program_id(0)
    x = x_ref[...]
    g = g_ref[...]
    keep = (i % tiles_per_seq != 0).astype(F32)
    xn_ref[0:FFN_HALO, :] = (_rmsnorm_rows(halo_ref[...], g) * keep).astype(BF16)
    xn_ref[FFN_HALO:, :] = _rmsnorm_rows(x, g).astype(BF16)
    xn = xn_ref[...]
    rows = xn.shape[0]

    def conv(c0):
        hcol = jnp.dot(xn, wu_ref[:, c0:c0 + _FF_CHUNK], preferred_element_type=F32)
        out = (cw_ref[2:3, c0:c0 + _FF_CHUNK] * hcol
               + cw_ref[1:2, c0:c0 + _FF_CHUNK] * pltpu.roll(hcol, 1, axis=0)
               + cw_ref[0:1, c0:c0 + _FF_CHUNK] * pltpu.roll(hcol, 2, axis=0)
               + cb_ref[:, c0:c0 + _FF_CHUNK])
        return out[FFN_HALO:rows]

    acc = x
    for c in range(0, D_FF, _FF_CHUNK):
        a = conv(c)
        u = conv(D_FF + c)
        act = (a * jax.nn.sigmoid(a) * u).astype(BF16)
        acc = acc + jnp.dot(act, wd_ref[c:c + _FF_CHUNK, :], preferred_element_type=F32)
    if final_norm:
        acc = _rmsnorm_rows(acc, gf_ref[...])
    y_ref[...] = acc


def _ffn(x2, g, wu, cw, cb, wd, gf, *, seq, final_norm):
    t = x2.shape[0]
    tm = ROW_TILE
    const = lambda i: (0, 0)
    halo_blocks = tm // FFN_HALO
    return pl.pallas_call(
        functools.partial(_ffn_kernel, tiles_per_seq=seq // tm, final_norm=final_norm),
        out_shape=jax.ShapeDtypeStruct((t, D_MODEL), F32),
        grid=(t // tm,),
        in_specs=[pl.BlockSpec((tm, D_MODEL), lambda i: (i, 0)),
                  pl.BlockSpec((FFN_HALO, D_MODEL), lambda i: (jnp.maximum(i * halo_blocks - 1, 0), 0)),
                  pl.BlockSpec((1, D_MODEL), const),
                  pl.BlockSpec(wu.shape, const), pl.BlockSpec(cw.shape, const),
                  pl.BlockSpec(cb.shape, const), pl.BlockSpec(wd.shape, const),
                  pl.BlockSpec((1, D_MODEL), const)],
        out_specs=pl.BlockSpec((tm, D_MODEL), lambda i: (i, 0)),
        scratch_shapes=[pltpu.VMEM((tm + FFN_HALO, D_MODEL), BF16)],
        compiler_params=_params(1),
        name="conv_ffn",
    )(x2, x2, g, wu, cw, cb, wd, gf)


def kernel(x, rel_bias, norm_mix, w_in, cmp_pe_k, cmp_w1_k, cmp_w2_k, cmp_pe_v, cmp_w1_v, cmp_w2_v,
           w_branch, w_out, norm_ffn, w_up, conv_w, conv_b, w_down, norm_final):
    b, s, d = x.shape
    depth = w_in.shape[0]
    t = b * s
    assert d == D_MODEL and s % T_TILE == 0 and s // NSA_SLC_BLOCK == _N_SLC and t % ROW_TILE == 0
    assert s % ROW_TILE == 0 and all(s % (dil * A_TILE) == 0 for _, dil in A_GROUPS)
    assert all(win // dil == A_TILE for win, dil in A_GROUPS) and A_GROUPS[0][1] == 1
    assert all(ROW_TILE % dil == 0 for _, dil in A_GROUPS)

    hg = A_HEADS_PER_GROUP
    bias_a = jnp.concatenate([
        _bias_tiles(rel_bias,
                    np.concatenate([_bucket_tile(A_TILE, 1, dil, win // dil),
                                    _bucket_tile(A_TILE, 0, dil, win // dil)], axis=1),
                    gi * hg, hg)
        for gi, (win, dil) in enumerate(A_GROUPS)], axis=0)
    assert (_bucket_tile(T_TILE, 2, 1, 3 * T_TILE) == REL_BUCKETS - 1).all()
    bias_b = _bias_tiles_t(rel_bias, [_bucket_tile(T_TILE, d, 1, NSA_WINDOW - 1) for d in (0, 1, 2)],
                           A_HEADS, B_HEADS)
    bias_b = bias_b.reshape(B_KV_HEADS, B_GROUP, 3, T_TILE, T_TILE).transpose(0, 2, 3, 1, 4).reshape(
        B_KV_HEADS, 3, T_TILE, _G_ROWS)
    bias_c = _bias_tiles_t(rel_bias, [_bucket_tile(T_TILE, d, 1, s) for d in (0, 1)],
                           A_HEADS + B_HEADS, C_HEADS)

    w_in_k = _prep_w_in(w_in)
    x2 = x.reshape(t, d)
    for i in range(depth):
        pa0, pa1, pa2, pb, pcmp, pc, mg, bg = _proj_in(x2, norm_mix[i][None, :], w_in_k, layer=i,
                                                       seq=s)

        oa = _mixer_a(pa0.reshape(b, s, _GRP_COLS), pa1, pa2, bias_a)

        cmp = _compress(pcmp.reshape(4, b, s // NSA_CMP_STRIDE, NSA_CMP_STRIDE * HEAD_DIM),
                        cmp_pe_k[i].reshape(1, -1), cmp_pe_v[i].reshape(1, -1),
                        cmp_w1_k[i].astype(BF16), cmp_w1_v[i].astype(BF16),
                        cmp_w2_k[i].astype(BF16), cmp_w2_v[i].astype(BF16))
        ob = _nsa(pb.reshape(b, s, _PB_COLS), cmp, bg.reshape(b, s, _BG_COLS), bias_b)

        oc = _moba(pc.reshape(b, s, _C_COLS), bias_c)

        x2 = _merge(oa.reshape(t, -1), ob.reshape(t, -1), oc.reshape(t, -1), mg, x2,
                    w_branch[i].astype(BF16), w_out[i].astype(BF16))
        x2 = _ffn(x2, norm_ffn[i][None, :], w_up[i].astype(BF16), conv_w[i], conv_b[i][None, :],
                  w_down[i].astype(BF16), norm_final[None, :], seq=s, final_norm=(i == depth - 1))
    return x2.reshape(b, s, d)
```

```python
import functools
import math

import jax
import jax.numpy as jnp
import numpy as np
from jax import lax
from jax.experimental import pallas as pl
from jax.experimental.pallas import tpu as pltpu

F32 = jnp.float32
BF16 = jnp.bfloat16

D_MODEL = 1024
HEAD_DIM = 64
A_GROUPS = ((128, 1), (512, 4), (2048, 16))
A_HEADS_PER_GROUP = 2
A_HEADS = 6
B_HEADS = 6
B_KV_HEADS = 2
B_GROUP = 3
C_HEADS = 4
NSA_CMP_BLOCK = 32
NSA_CMP_STRIDE = 16
NSA_CMP_HIDDEN = 256
NSA_SLC_BLOCK = 64
NSA_SLC_TOPK = 16
NSA_WINDOW = 512
MOBA_BLOCK = 256
MOBA_TOPK = 3
REL_BUCKETS = 32
REL_MAX_DIST = 128
D_FF = 2816
RMS_EPS = 1e-6
NEG_INF = -1e30
FORCE_BONUS = 1e4
ATTN_SCALE = HEAD_DIM ** -0.5

LANE = 128
A_TILE = 128
T_TILE = 256
ROW_TILE = 512
FFN_HALO = 8
VMEM_LIMIT = 56 * 1024 * 1024

_GRP_COLS = 3 * A_HEADS_PER_GROUP * HEAD_DIM
_A_COLS = len(A_GROUPS) * _GRP_COLS
_PB_COLS = B_HEADS * HEAD_DIM + 4 * B_KV_HEADS * HEAD_DIM
_CMP_COLS = 2 * B_KV_HEADS * HEAD_DIM
_C_COLS = 3 * C_HEADS * HEAD_DIM
_MG_COLS = 3 * D_MODEL
_BG_COLS = LANE
_OFF_A = 0
_OFF_PB = _OFF_A + _A_COLS
_OFF_CMP = _OFF_PB + _PB_COLS
_OFF_C = _OFF_CMP + _CMP_COLS
_OFF_MG = _OFF_C + _C_COLS
_OFF_BG = _OFF_MG + _MG_COLS
_W_COLS = _OFF_BG + _BG_COLS


def _params(n_grid):
    return pltpu.CompilerParams(dimension_semantics=("arbitrary",) * n_grid,
                                vmem_limit_bytes=VMEM_LIMIT)


def _rel_bucket_np(dist):
    n = np.maximum(dist, 0)
    exact = REL_BUCKETS // 2
    nf = np.maximum(n, 1).astype(np.float32)
    large = exact + (np.log(nf / np.float32(exact)) / np.float32(math.log(REL_MAX_DIST / exact))
                     * np.float32(REL_BUCKETS - exact)).astype(np.int32)
    return np.where(n < exact, n, np.minimum(large, REL_BUCKETS - 1)).astype(np.int32)


def _bucket_tile(tile, block_offset, dil, max_rel):
    rel = block_offset * tile + np.arange(tile)[:, None] - np.arange(tile)[None, :]
    ok = (rel >= 0) & (rel <= max_rel)
    return np.where(ok, _rel_bucket_np(rel * dil), -1).astype(np.int32)


def _bias_tiles_kernel(tbl_ref, idx_ref, o_ref, *, head0, relative):
    h = pl.program_id(0) + head0
    idx = idx_ref[...]
    acc = jnp.full(idx.shape, NEG_INF, F32)
    for b in range(REL_BUCKETS):
        acc = jnp.where(idx == b, tbl_ref[b, h], acc)
    if relative:
        acc = acc - tbl_ref[REL_BUCKETS - 1, h]
    o_ref[0] = acc


def _bias_tiles(rel_bias, idx, head0, n_heads, relative=False):
    r, c = idx.shape
    return pl.pallas_call(
        functools.partial(_bias_tiles_kernel, head0=head0, relative=relative),
        out_shape=jax.ShapeDtypeStruct((n_heads, r, c), F32),
        grid=(n_heads,),
        in_specs=[pl.BlockSpec(memory_space=pltpu.SMEM),
                  pl.BlockSpec((r, c), lambda h: (0, 0))],
        out_specs=pl.BlockSpec((1, r, c), lambda h: (h, 0, 0)),
        compiler_params=_params(1),
        name="bias_tiles",
    )(rel_bias, jnp.asarray(idx))


def _bias_tiles_t(rel_bias, idx_tiles, head0, n_heads):
    r, c = idx_tiles[0].shape
    flat = _bias_tiles(rel_bias, np.concatenate([t.T for t in idx_tiles], axis=1), head0, n_heads,
                       relative=True)
    return flat.reshape(n_heads, c, len(idx_tiles), r).transpose(0, 2, 1, 3)


def _rmsnorm_rows(x, g):
    return x * lax.rsqrt(jnp.mean(x * x, axis=-1, keepdims=True) + RMS_EPS) * g


def _proj_in_kernel(x_ref, g_ref, w_ref, pa0_ref, pa1_ref, pa2_ref, pb_ref, pcmp_ref, pc_ref,
                    mg_ref, bg_ref, regroup_ref):
    xn = _rmsnorm_rows(x_ref[...], g_ref[...]).astype(BF16)
    tm = xn.shape[0]

    def mm(c0, c1):
        return jnp.dot(xn, w_ref[:, c0:c1], preferred_element_type=F32)

    pa0_ref[...] = mm(_OFF_A, _OFF_A + _GRP_COLS).astype(BF16)
    for gi, out_ref in ((1, pa1_ref), (2, pa2_ref)):
        dil = A_GROUPS[gi][1]
        res = mm(_OFF_A + gi * _GRP_COLS, _OFF_A + (gi + 1) * _GRP_COLS)
        for j in range(3):
            regroup_ref[j] = res[:, j * LANE:(j + 1) * LANE]
        for r in range(dil):
            for j in range(3):
                out_ref[0, r, :, j * LANE:(j + 1) * LANE] = (
                    regroup_ref[j, pl.ds(r, tm // dil, stride=dil), :].astype(BF16))
    pb_ref[:, 0:384] = mm(_OFF_PB, _OFF_PB + 384).astype(BF16)
    pb_ref[:, 384:896] = mm(_OFF_PB + 384, _OFF_PB + 896).astype(BF16)
    cmp = mm(_OFF_CMP, _OFF_CMP + _CMP_COLS).astype(BF16)
    for i in range(4):
        pcmp_ref[i] = cmp[:, i * HEAD_DIM:(i + 1) * HEAD_DIM]
    for c in range(0, _C_COLS, 384):
        pc_ref[:, c:c + 384] = mm(_OFF_C + c, _OFF_C + c + 384).astype(BF16)
    for c in range(0, _MG_COLS, 512):
        mg_ref[:, c:c + 512] = jax.nn.sigmoid(mm(_OFF_MG + c, _OFF_MG + c + 512)).astype(BF16)
    bg_ref[...] = mm(_OFF_BG, _OFF_BG + _BG_COLS)


def _proj_in(x2, g, w, *, layer, seq):
    t = x2.shape[0]
    tm = ROW_TILE
    b = t // seq
    tps = seq // tm
    row = lambda i: (i, 0)
    d1, d2 = A_GROUPS[1][1], A_GROUPS[2][1]
    sub = lambda i: (i // tps, 0, i % tps, 0)
    return pl.pallas_call(
        _proj_in_kernel,
        out_shape=(jax.ShapeDtypeStruct((t, _GRP_COLS), BF16),
                   jax.ShapeDtypeStruct((b, d1, seq // d1, _GRP_COLS), BF16),
                   jax.ShapeDtypeStruct((b, d2, seq // d2, _GRP_COLS), BF16),
                   jax.ShapeDtypeStruct((t, _PB_COLS), BF16),
                   jax.ShapeDtypeStruct((4, t, HEAD_DIM), BF16),
                   jax.ShapeDtypeStruct((t, _C_COLS), BF16),
                   jax.ShapeDtypeStruct((t, _MG_COLS), BF16),
                   jax.ShapeDtypeStruct((t, _BG_COLS), F32)),
        grid=(t // tm,),
        in_specs=[pl.BlockSpec((tm, D_MODEL), row),
                  pl.BlockSpec((1, D_MODEL), lambda i: (0, 0)),
                  pl.BlockSpec((None, D_MODEL, _W_COLS), lambda i: (layer, 0, 0))],
        out_specs=(pl.BlockSpec((tm, _GRP_COLS), row),
                   pl.BlockSpec((1, d1, tm // d1, _GRP_COLS), sub),
                   pl.BlockSpec((1, d2, tm // d2, _GRP_COLS), sub),
                   pl.BlockSpec((tm, _PB_COLS), row),
                   pl.BlockSpec((4, tm, HEAD_DIM), lambda i: (0, i, 0)),
                   pl.BlockSpec((tm, _C_COLS), row),
                   pl.BlockSpec((tm, _MG_COLS), row),
                   pl.BlockSpec((tm, _BG_COLS), row)),
        scratch_shapes=[pltpu.VMEM((3, tm, LANE), F32)],
        compiler_params=_params(1),
        name="proj_in",
    )(x2, g, w)


def _w_in_segments():
    a = A_HEADS * HEAD_DIM
    bq0 = 3 * a
    bkc0 = bq0 + B_HEADS * HEAD_DIM
    bks0 = bkc0 + 2 * B_KV_HEADS * HEAD_DIM
    bg0 = bks0 + 4 * B_KV_HEADS * HEAD_DIM
    c0 = bg0 + B_HEADS * 3
    mg0 = c0 + _C_COLS
    gw = A_HEADS_PER_GROUP * HEAD_DIM
    segs = [(_OFF_A + (gi * 3 + j) * gw, j * a + gi * gw, gw)
            for gi in range(len(A_GROUPS)) for j in range(3)]
    segs += [(_OFF_PB, bq0, B_HEADS * HEAD_DIM),
             (_OFF_PB + B_HEADS * HEAD_DIM, bks0, bg0 - bks0),
             (_OFF_CMP, bkc0, _CMP_COLS), (_OFF_C, c0, _C_COLS), (_OFF_MG, mg0, _MG_COLS),
             (_OFF_BG, bg0, B_HEADS * 3)]
    return segs


def _prep_w_in_kernel(w_ref, o_ref):
    o_ref[0, :, _OFF_BG:] = jnp.zeros((o_ref.shape[1], _BG_COLS), BF16)
    for dst, src, width in _w_in_segments():
        for c in range(0, width, 768):
            n = min(768, width - c)
            o_ref[0, :, dst + c:dst + c + n] = w_ref[0, :, src + c:src + c + n].astype(BF16)


def _prep_w_in(w):
    depth, rows, cols = w.shape
    rb = 128
    return pl.pallas_call(
        _prep_w_in_kernel,
        out_shape=jax.ShapeDtypeStruct((depth, rows, _W_COLS), BF16),
        grid=(depth, rows // rb),
        in_specs=[pl.BlockSpec((1, rb, cols), lambda l, r: (l, r, 0))],
        out_specs=pl.BlockSpec((1, rb, _W_COLS), lambda l, r: (l, r, 0)),
        compiler_params=_params(2),
        name="prep_w_in",
    )(w)


_A_UNROLL = 3


def _mixer_a_kernel(pa0_ref, pa1_ref, pa2_ref, bias_ref, o_ref, on_ref, ln_ref, *, seq):
    gw = A_HEADS_PER_GROUP * HEAD_DIM

    def band_block(src, l0, first, gi, rows):
        outs, lses = [], []
        for h in range(A_HEADS_PER_GROUP):
            qc, kc, vc = (slice(j * gw + h * HEAD_DIM, j * gw + (h + 1) * HEAD_DIM) for j in range(3))
            q = src[pl.ds(l0, A_TILE), qc] * ATTN_SCALE
            if first:
                kr = pl.ds(l0, A_TILE)
                bias = bias_ref[A_HEADS_PER_GROUP * gi + h, :, A_TILE:]
            else:
                kr = pl.ds(l0 - A_TILE, 2 * A_TILE)
                bias = bias_ref[A_HEADS_PER_GROUP * gi + h]
            s = _nt(q, src[kr, kc]) + bias
            m = jnp.max(s, axis=-1, keepdims=True)
            e = jnp.exp(s - m)
            l = jnp.sum(e, axis=-1, keepdims=True)
            outs.append(jnp.dot(e.astype(BF16), src[kr, vc], preferred_element_type=F32) / l)
            lses.append(jnp.broadcast_to(m + jnp.log(l), (A_TILE, HEAD_DIM)))
        on_ref[gi, rows, :] = jnp.concatenate(outs, axis=-1)
        ln_ref[gi, rows, :] = jnp.concatenate(lses, axis=-1)

    for gi, ((_, dil), pa_ref) in enumerate(zip(A_GROUPS, (pa0_ref, pa1_ref, pa2_ref))):
        n_blocks = seq // dil // A_TILE
        if dil == 1:
            band_block(pa_ref.at[0], 0, True, gi, pl.ds(0, A_TILE))
            assert (n_blocks - 1) % _A_UNROLL == 0

            def body0(it, c, pa_ref=pa_ref, gi=gi):
                for u in range(_A_UNROLL):
                    l0 = pl.multiple_of((1 + it * _A_UNROLL + u) * A_TILE, A_TILE)
                    band_block(pa_ref.at[0], l0, False, gi, pl.ds(l0, A_TILE))
                return c
            lax.fori_loop(0, (n_blocks - 1) // _A_UNROLL, body0, 0)
        elif n_blocks > 1:
            for r in range(dil):
                band_block(pa_ref.at[0, r], 0, True, gi, pl.ds(r, A_TILE, stride=dil))

            def body1(i, c, pa_ref=pa_ref, gi=gi, dil=dil):
                l0 = pl.multiple_of(i * A_TILE, A_TILE)
                for r in range(dil):
                    band_block(pa_ref.at[0, r], l0, False, gi, pl.ds(l0 * dil + r, A_TILE, stride=dil))
                return c
            lax.fori_loop(1, n_blocks, body1, 0)
        else:
            per_iter = 4

            def body2(it, c, pa_ref=pa_ref, gi=gi, dil=dil):
                for u in range(per_iter):
                    r = it * per_iter + u
                    band_block(pa_ref.at[0, r], 0, True, gi, pl.ds(r, A_TILE, stride=dil))
                return c
            lax.fori_loop(0, dil // per_iter, body2, 0)

    chunk = T_TILE
    for c0 in range(0, seq, chunk):
        rows = slice(c0, c0 + chunk)
        lse = [ln_ref[g, rows, :] for g in range(len(A_GROUPS))]
        top = functools.reduce(jnp.maximum, lse)
        ex = [jnp.exp(v - top) for v in lse]
        den = functools.reduce(lambda a, b: a + b, ex)
        for g in range(len(A_GROUPS)):
            o_ref[0, rows, g * gw:(g + 1) * gw] = (ex[g] / den * on_ref[g, rows, :]).astype(BF16)


def _mixer_a(pa0, pa1, pa2, bias):
    b, seq, _ = pa0.shape
    return pl.pallas_call(
        functools.partial(_mixer_a_kernel, seq=seq),
        out_shape=jax.ShapeDtypeStruct((b, seq, _GRP_COLS), BF16),
        grid=(b,),
        in_specs=[pl.BlockSpec((1, seq, _GRP_COLS), lambda i: (i, 0, 0)),
                  pl.BlockSpec((1,) + pa1.shape[1:], lambda i: (i, 0, 0, 0)),
                  pl.BlockSpec((1,) + pa2.shape[1:], lambda i: (i, 0, 0, 0)),
                  pl.BlockSpec(bias.shape, lambda i: (0, 0, 0))],
        out_specs=pl.BlockSpec((1, seq, _GRP_COLS), lambda i: (i, 0, 0)),
        scratch_shapes=[pltpu.VMEM((len(A_GROUPS), seq, LANE), F32),
                        pltpu.VMEM((len(A_GROUPS), seq, LANE), F32)],
        compiler_params=_params(1),
        name="mixer_a",
    )(pa0, pa1, pa2, bias)


def _compress_kernel(r_ref, pek_ref, pev_ref, w1k_ref, w1v_ref, w2k_ref, w2v_ref, o_ref):
    half = NSA_CMP_STRIDE * HEAD_DIM
    for kv, (pe_ref, w1_ref, w2_ref) in enumerate(((pek_ref, w1k_ref, w2k_ref),
                                                   (pev_ref, w1v_ref, w2v_ref))):
        for h in range(B_KV_HEADS):
            r = r_ref[kv * B_KV_HEADS + h, 0].astype(F32)
            lo = jnp.dot((r + pe_ref[:, :half]).astype(BF16), w1_ref[:half, :],
                         preferred_element_type=F32)
            hi = jnp.dot((r + pe_ref[:, half:]).astype(BF16), w1_ref[half:, :],
                         preferred_element_type=F32)
            hid = lo + pltpu.roll(hi, hi.shape[0] - 1, axis=0)
            act = jax.nn.gelu(hid).astype(BF16)
            o_ref[0, kv * B_KV_HEADS + h] = jnp.dot(act, w2_ref[...],
                                                    preferred_element_type=F32).astype(BF16)


def _compress(r, pe_k, pe_v, w1_k, w1_v, w2_k, w2_v):
    _, b, m, c = r.shape
    const2 = lambda i: (0, 0)
    return pl.pallas_call(
        _compress_kernel,
        out_shape=jax.ShapeDtypeStruct((b, 4, m, HEAD_DIM), BF16),
        grid=(b,),
        in_specs=[pl.BlockSpec((4, 1, m, c), lambda i: (0, i, 0, 0)),
                  pl.BlockSpec(pe_k.shape, const2), pl.BlockSpec(pe_v.shape, const2),
                  pl.BlockSpec(w1_k.shape, const2), pl.BlockSpec(w1_v.shape, const2),
                  pl.BlockSpec(w2_k.shape, const2), pl.BlockSpec(w2_v.shape, const2)],
        out_specs=pl.BlockSpec((1, 4, m, HEAD_DIM), lambda i: (i, 0, 0, 0)),
        compiler_params=_params(1),
        name="nsa_compress",
    )(r, pe_k, pe_v, w1_k, w1_v, w2_k, w2_v)


MASK_BIG = 2.0 ** 100
SEL_PAD = 32


def _nt(a, b, **kw):
    return lax.dot_general(a, b, (((1,), (1,)), ((), ())), preferred_element_type=F32, **kw)


def _top_mask_t(score, n_rows, n_top):
    sub = lax.broadcasted_iota(jnp.int32, score.shape, 0)
    rank = jnp.zeros(score.shape, F32)
    for j in range(n_rows):
        row = score[j:j + 1, :]
        ge = jnp.where(row >= score, 1.0, 0.0)
        gt = jnp.where(row > score, 1.0, 0.0)
        rank = rank + jnp.where(sub > j, ge, gt)
    return rank < n_top


def _two_pass_attention(i, streams, *, band):
    def key_rows(j):
        return pl.ds(pl.multiple_of(j * T_TILE, T_TILE), T_TILE)

    def logits(j, d, first=False):
        for k_tile, q_t, _, bias_tile, s_ref, rmp_ref, _ in streams:
            s = jnp.dot(k_tile(key_rows(j)), q_t(), preferred_element_type=F32)
            if d is not None:
                s = s + bias_tile(d)
            s_ref[j] = s
            fold = jnp.max(s.reshape(s.shape[0] // 8, 8, s.shape[1]), axis=0)
            rmp_ref[...] = fold if first else jnp.maximum(rmp_ref[...], fold)

    logits(i, 0, first=True)

    @pl.when(i >= 1)
    def _():
        logits(i - 1, 1)

    if band:
        @pl.when(i >= 2)
        def _():
            logits(i - 2, 2)
        lo = jnp.maximum(i - 2, 0)
    else:
        def far(j, c):
            logits(j, None)
            return c
        lax.fori_loop(0, i - 1, far, 0)
        lo = 0

    tops = []
    for *_, rmp_ref, acc_ref in streams:
        tops.append(jnp.max(rmp_ref[...], axis=0, keepdims=True))
        acc_ref[...] = jnp.zeros(acc_ref.shape, F32)

    def pv(j, c):
        for (_, _, vt_ref, _, s_ref, _, acc_ref), top in zip(streams, tops):
            e = jnp.exp(s_ref[j] - top)
            acc_ref[...] += jnp.dot(vt_ref[j], e.astype(BF16), preferred_element_type=F32)
        return c
    lax.fori_loop(lo, i + 1, pv, 0)

    outs = []
    for *_, acc_ref in streams:
        acc = acc_ref[...]
        outs.append(acc[:HEAD_DIM, :] / acc[HEAD_DIM:HEAD_DIM + 1, :])
    return outs


def _eye(n, dtype=BF16):
    return (lax.broadcasted_iota(jnp.int32, (n, n), 0)
            == lax.broadcasted_iota(jnp.int32, (n, n), 1)).astype(dtype)


def _transposed_values(v_ref_rows, n_tiles):
    eye = _eye(LANE)
    for j in range(n_tiles):
        v = _with_ones_column(v_ref_rows(slice(j * T_TILE, (j + 1) * T_TILE)))
        yield j, _nt(eye, v).astype(BF16)


def _with_ones_column(v):
    one = (lax.broadcasted_iota(jnp.int32, v.shape, 1) == 0).astype(F32)
    return jnp.concatenate([v.astype(F32), one], axis=1).astype(BF16)


def _with_block_onehot(k, block):
    n = k.shape[0]
    shape = (n, SEL_PAD)
    hot = (lax.broadcasted_iota(jnp.int32, shape, 0) // block
           == lax.broadcasted_iota(jnp.int32, shape, 1)).astype(F32)
    return jnp.concatenate([k.astype(F32), hot, jnp.zeros(shape, F32)], axis=1).astype(BF16)


def _augment_q_t(q_t, allowed_t):
    pen = (allowed_t - 1.0) * MASK_BIG
    return jnp.concatenate([q_t, pen, jnp.zeros(pen.shape, F32)], axis=0).astype(BF16)


_N_SLC = 32
_G_ROWS = B_GROUP * T_TILE


def _nsa_kernel(pb_ref, cmp_ref, bg_ref, bias_ref, o_ref,
                kaug_ref, vst_ref, vwt_ref, eye_ref, q3t_ref, qaugt_ref, s_ref, rmp_ref,
                acc_ref, ocmp_ref, oslc_ref, *, seq):
    n_tiles = seq // T_TILE
    n_cmp_rows = seq // NSA_CMP_STRIDE
    q_cols = B_HEADS * HEAD_DIM
    ks0, vs0, kw0, vw0 = (q_cols + i * B_KV_HEADS * HEAD_DIM for i in range(4))

    def head_cols(c0, h):
        return slice(c0 + h * HEAD_DIM, c0 + (h + 1) * HEAD_DIM)

    j_id = lax.broadcasted_iota(jnp.int32, (_N_SLC, n_cmp_rows), 0)
    c_id = lax.broadcasted_iota(jnp.int32, (_N_SLC, n_cmp_rows), 1)
    overlap_t = ((c_id * NSA_CMP_STRIDE < (j_id + 1) * NSA_SLC_BLOCK)
                 & (c_id * NSA_CMP_STRIDE + NSA_CMP_BLOCK > j_id * NSA_SLC_BLOCK)).astype(F32)
    eye_ref[...] = _eye(T_TILE)
    for h in range(B_KV_HEADS):
        kaug_ref[h] = _with_block_onehot(pb_ref[0, :, head_cols(ks0, h)], NSA_SLC_BLOCK)
        for j, vt in _transposed_values(lambda r, h=h: pb_ref[0, r, head_cols(vs0, h)], n_tiles):
            vst_ref[h, j] = vt
        for j, vt in _transposed_values(lambda r, h=h: pb_ref[0, r, head_cols(vw0, h)], n_tiles):
            vwt_ref[h, j] = vt

    def tile_body(i, carry):
        r0 = pl.multiple_of(i * T_TILE, T_TILE)
        rows = pl.ds(r0, T_TILE)
        pick = (lax.broadcasted_iota(jnp.int32, (SEL_PAD, _BG_COLS), 0)
                == lax.broadcasted_iota(jnp.int32, (SEL_PAD, _BG_COLS), 1)).astype(F32)
        sig_t = jax.nn.sigmoid(_nt(pick, bg_ref[0, rows, :], precision=lax.Precision.HIGHEST))
        eye_h = _eye(HEAD_DIM)

        for h in range(B_KV_HEADS):
            q3t = jnp.concatenate(
                [_nt(eye_h, pb_ref[0, rows, head_cols(0, h * B_GROUP + g)]) for g in range(B_GROUP)],
                axis=1) * ATTN_SCALE
            q3t_bf = q3t.astype(BF16)
            q3t_ref[h] = q3t_bf

            kc = cmp_ref[0, h]
            vct = _nt(eye_h, cmp_ref[0, B_KV_HEADS + h]).astype(BF16)
            c_end = (lax.broadcasted_iota(jnp.int32, (n_cmp_rows, _G_ROWS), 0) * NSA_CMP_STRIDE
                     + NSA_CMP_BLOCK - 1)
            t_pos = r0 + lax.broadcasted_iota(jnp.int32, (n_cmp_rows, _G_ROWS), 1) % T_TILE
            ok = (c_end <= t_pos) & (c_end < seq)
            lc = jnp.where(ok, jnp.dot(kc, q3t_bf, preferred_element_type=F32), NEG_INF)
            ec = jnp.exp(lc - jnp.max(lc, axis=0, keepdims=True))
            pc = ec / jnp.sum(ec, axis=0, keepdims=True) * ok.astype(F32)
            ocmp_ref[h] = jnp.dot(vct, pc.astype(BF16), preferred_element_type=F32)
            p_sum = pc[:, 0:T_TILE] + pc[:, T_TILE:2 * T_TILE] + pc[:, 2 * T_TILE:3 * T_TILE]
            imp_t = jnp.dot(overlap_t, p_sum, precision=lax.Precision.HIGHEST,
                            preferred_element_type=F32)

            j_sub = lax.broadcasted_iota(jnp.int32, (_N_SLC, T_TILE), 0)
            jt = (r0 + lax.broadcasted_iota(jnp.int32, (_N_SLC, T_TILE), 1)) // NSA_SLC_BLOCK
            forced = (j_sub == 0) | (j_sub == jt) | (j_sub == jt - 1)
            valid = j_sub <= jt
            score = jnp.where(valid, imp_t + FORCE_BONUS * forced.astype(F32), NEG_INF)
            sel_t = jnp.where(valid & _top_mask_t(score, _N_SLC, NSA_SLC_TOPK), 1.0, 0.0)
            qaugt_ref[h] = _augment_q_t(q3t, jnp.concatenate([sel_t] * B_GROUP, axis=1))

        def scratch(h):
            return lambda d: bias_ref[h, d], s_ref.at[h], rmp_ref.at[h], acc_ref.at[h]

        oslc = _two_pass_attention(
            i, [(lambda kr, h=h: kaug_ref[h, kr, :], lambda h=h: qaugt_ref[h], vst_ref.at[h])
                + scratch(h) for h in range(B_KV_HEADS)], band=False)
        for h in range(B_KV_HEADS):
            oslc_ref[h] = oslc[h]

        owin = _two_pass_attention(
            i, [(lambda kr, h=h: pb_ref[0, kr, head_cols(kw0, h)], lambda h=h: q3t_ref[h],
                 vwt_ref.at[h]) + scratch(h) for h in range(B_KV_HEADS)], band=True)

        heads = []
        for h in range(B_KV_HEADS):
            for g in range(B_GROUP):
                head = h * B_GROUP + g
                gc = slice(g * T_TILE, (g + 1) * T_TILE)
                heads.append(sig_t[3 * head:3 * head + 1, :] * ocmp_ref[h, :, gc]
                             + sig_t[3 * head + 1:3 * head + 2, :] * oslc_ref[h, :, gc]
                             + sig_t[3 * head + 2:3 * head + 3, :] * owin[h][:, gc])
        o_t = jnp.concatenate(heads, axis=0).astype(BF16)
        o_ref[0, rows, :] = _nt(eye_ref[...], o_t).astype(o_ref.dtype)
        return carry

    lax.fori_loop(0, n_tiles, tile_body, 0)


def _nsa(pb, cmp, bg, bias):
    b, seq, _ = pb.shape
    n_tiles = seq // T_TILE
    return pl.pallas_call(
        functools.partial(_nsa_kernel, seq=seq),
        out_shape=jax.ShapeDtypeStruct((b, seq, B_HEADS * HEAD_DIM), BF16),
        grid=(b,),
        in_specs=[pl.BlockSpec((1, seq, _PB_COLS), lambda i: (i, 0, 0)),
                  pl.BlockSpec((1,) + cmp.shape[1:], lambda i: (i, 0, 0, 0)),
                  pl.BlockSpec((1, seq, _BG_COLS), lambda i: (i, 0, 0)),
                  pl.BlockSpec(bias.shape, lambda i: (0, 0, 0, 0))],
        out_specs=pl.BlockSpec((1, seq, B_HEADS * HEAD_DIM), lambda i: (i, 0, 0)),
        scratch_shapes=[pltpu.VMEM((B_KV_HEADS, seq, LANE), BF16),
                        pltpu.VMEM((B_KV_HEADS, n_tiles, LANE, T_TILE), BF16),
                        pltpu.VMEM((B_KV_HEADS, n_tiles, LANE, T_TILE), BF16),
                        pltpu.VMEM((T_TILE, T_TILE), BF16),
                        pltpu.VMEM((B_KV_HEADS, HEAD_DIM, _G_ROWS), BF16),
                        pltpu.VMEM((B_KV_HEADS, LANE, _G_ROWS), BF16),
                        pltpu.VMEM((B_KV_HEADS, n_tiles, T_TILE, _G_ROWS), F32),
                        pltpu.VMEM((B_KV_HEADS, 8, _G_ROWS), F32),
                        pltpu.VMEM((B_KV_HEADS, LANE, _G_ROWS), F32),
                        pltpu.VMEM((B_KV_HEADS, HEAD_DIM, _G_ROWS), F32),
                        pltpu.VMEM((B_KV_HEADS, HEAD_DIM, _G_ROWS), F32)],
        compiler_params=_params(1),
        name="nsa",
    )(pb, cmp, bg, bias)


def _moba_kernel(pc_ref, bias_ref, o_ref, kaug_ref, vt_ref, kmean_ref, eye_ref, qaugt_ref, s_ref,
                 rmp_ref, acc_ref, *, seq):
    n_blk = seq // MOBA_BLOCK
    width = C_HEADS * HEAD_DIM
    eye_ref[...] = _eye(T_TILE)

    def cols(j, h):
        return slice(j * width + h * HEAD_DIM, j * width + (h + 1) * HEAD_DIM)

    for h in range(C_HEADS):
        kaug_ref[h] = _with_block_onehot(pc_ref[0, :, cols(1, h)], MOBA_BLOCK)
        for j, vt in _transposed_values(lambda r, h=h: pc_ref[0, r, cols(2, h)], n_blk):
            vt_ref[h, j] = vt
        kmean_ref[h] = jnp.concatenate(
            [jnp.mean(pc_ref[0, n * MOBA_BLOCK:(n + 1) * MOBA_BLOCK, cols(1, h)].astype(F32), axis=0,
                      keepdims=True) for n in range(n_blk)]
            + [jnp.zeros((SEL_PAD - n_blk, HEAD_DIM), F32)], axis=0)

    def tile_body(i, carry):
        rows = pl.ds(pl.multiple_of(i * MOBA_BLOCK, MOBA_BLOCK), MOBA_BLOCK)
        n_sub = lax.broadcasted_iota(jnp.int32, (SEL_PAD, MOBA_BLOCK), 0)
        past = n_sub < i
        eye_h = _eye(HEAD_DIM)
        for h in range(C_HEADS):
            q_t = _nt(eye_h, pc_ref[0, rows, cols(0, h)])
            gate_t = jnp.dot(kmean_ref[h], q_t, precision=lax.Precision.HIGHEST,
                             preferred_element_type=F32)
            score = jnp.where(past, gate_t, NEG_INF)
            picked = past & _top_mask_t(score, n_blk, MOBA_TOPK)
            allowed_t = jnp.where(picked | (n_sub == i), 1.0, 0.0)
            qaugt_ref[h] = _augment_q_t(q_t * ATTN_SCALE, allowed_t)

        outs = _two_pass_attention(
            i, [(lambda kr, h=h: kaug_ref[h, kr, :], lambda h=h: qaugt_ref[h], vt_ref.at[h],
                 lambda d, h=h: bias_ref[h, d], s_ref.at[h], rmp_ref.at[h], acc_ref.at[h])
                for h in range(C_HEADS)], band=False)
        o_t = jnp.concatenate(outs, axis=0).astype(BF16)
        o_ref[0, rows, :] = _nt(eye_ref[...], o_t).astype(o_ref.dtype)
        return carry

    lax.fori_loop(0, n_blk, tile_body, 0)


def _moba(pc, bias):
    b, seq, _ = pc.shape
    width = C_HEADS * HEAD_DIM
    n_tiles = seq // T_TILE
    return pl.pallas_call(
        functools.partial(_moba_kernel, seq=seq),
        out_shape=jax.ShapeDtypeStruct((b, seq, width), BF16),
        grid=(b,),
        in_specs=[pl.BlockSpec((1, seq, _C_COLS), lambda i: (i, 0, 0)),
                  pl.BlockSpec(bias.shape, lambda i: (0, 0, 0, 0))],
        out_specs=pl.BlockSpec((1, seq, width), lambda i: (i, 0, 0)),
        scratch_shapes=[pltpu.VMEM((C_HEADS, seq, LANE), BF16),
                        pltpu.VMEM((C_HEADS, n_tiles, LANE, T_TILE), BF16),
                        pltpu.VMEM((C_HEADS, SEL_PAD, HEAD_DIM), F32),
                        pltpu.VMEM((T_TILE, T_TILE), BF16),
                        pltpu.VMEM((C_HEADS, LANE, T_TILE), BF16),
                        pltpu.VMEM((C_HEADS, n_tiles, T_TILE, T_TILE), F32),
                        pltpu.VMEM((C_HEADS, 8, T_TILE), F32),
                        pltpu.VMEM((C_HEADS, LANE, T_TILE), F32)],
        compiler_params=_params(1),
        name="moba",
    )(pc, bias)


def _merge_kernel(oa_ref, ob_ref, oc_ref, mg_ref, x_ref, wb_ref, wo_ref, y_ref):
    r0 = A_HEADS * HEAD_DIM
    r1 = r0 + B_HEADS * HEAD_DIM
    ya = jnp.dot(oa_ref[...], wb_ref[0:r0, :], preferred_element_type=F32)
    yb = jnp.dot(ob_ref[...], wb_ref[r0:r1, :], preferred_element_type=F32)
    yc = jnp.dot(oc_ref[...], wb_ref[r1:, :], preferred_element_type=F32)
    merged = (mg_ref[:, 0:D_MODEL].astype(F32) * ya
              + mg_ref[:, D_MODEL:2 * D_MODEL].astype(F32) * yb
              + mg_ref[:, 2 * D_MODEL:].astype(F32) * yc)
    y_ref[...] = x_ref[...] + jnp.dot(merged.astype(BF16), wo_ref[...], preferred_element_type=F32)


def _merge(oa, ob, oc, mg, x2, wb, wo):
    t = x2.shape[0]
    tm = ROW_TILE
    row = lambda i: (i, 0)
    return pl.pallas_call(
        _merge_kernel,
        out_shape=jax.ShapeDtypeStruct((t, D_MODEL), F32),
        grid=(t // tm,),
        in_specs=[pl.BlockSpec((tm, oa.shape[1]), row),
                  pl.BlockSpec((tm, ob.shape[1]), row), pl.BlockSpec((tm, oc.shape[1]), row),
                  pl.BlockSpec((tm, _MG_COLS), row), pl.BlockSpec((tm, D_MODEL), row),
                  pl.BlockSpec(wb.shape, lambda i: (0, 0)), pl.BlockSpec(wo.shape, lambda i: (0, 0))],
        out_specs=pl.BlockSpec((tm, D_MODEL), row),
        compiler_params=_params(1),
        name="merge",
    )(oa, ob, oc, mg, x2, wb, wo)


_FF_CHUNK = D_FF // 2


def _ffn_kernel(x_ref, halo_ref, g_ref, wu_ref, cw_ref, cb_ref, wd_ref, gf_ref, y_ref, xn_ref,
                *, tiles_per_seq, final_norm):
    i = pl.program_id(0)
    x = x_ref[...]
    g = g_ref[...]
    keep = (i % tiles_per_seq != 0).astype(F32)
    xn_ref[0:FFN_HALO, :] = (_rmsnorm_rows(halo_ref[...], g) * keep).astype(BF16)
    xn_ref[FFN_HALO:, :] = _rmsnorm_rows(x, g).astype(BF16)
    xn = xn_ref[...]
    rows = xn.shape[0]

    def conv(c0):
        hcol = jnp.dot(xn, wu_ref[:, c0:c0 + _FF_CHUNK], preferred_element_type=F32)
        out = (cw_ref[2:3, c0:c0 + _FF_CHUNK] * hcol
               + cw_ref[1:2, c0:c0 + _FF_CHUNK] * pltpu.roll(hcol, 1, axis=0)
               + cw_ref[0:1, c0:c0 + _FF_CHUNK] * pltpu.roll(hcol, 2, axis=0)
               + cb_ref[:, c0:c0 + _FF_CHUNK])
        return out[FFN_HALO:rows]

    acc = x
    for c in range(0, D_FF, _FF_CHUNK):
        a = conv(c)
        u = conv(D_FF + c)
        act = (a * jax.nn.sigmoid(a) * u).astype(BF16)
        acc = acc + jnp.dot(act, wd_ref[c:c + _FF_CHUNK, :], preferred_element_type=F32)
    if final_norm:
        acc = _rmsnorm_rows(acc, gf_ref[...])
    y_ref[...] = acc


def _ffn(x2, g, wu, cw, cb, wd, gf, *, seq, final_norm):
    t = x2.shape[0]
    tm = ROW_TILE
    const = lambda i: (0, 0)
    halo_blocks = tm // FFN_HALO
    return pl.pallas_call(
        functools.partial(_ffn_kernel, tiles_per_seq=seq // tm, final_norm=final_norm),
        out_shape=jax.ShapeDtypeStruct((t, D_MODEL), F32),
        grid=(t // tm,),
        in_specs=[pl.BlockSpec((tm, D_MODEL), lambda i: (i, 0)),
                  pl.BlockSpec((FFN_HALO, D_MODEL), lambda i: (jnp.maximum(i * halo_blocks - 1, 0), 0)),
                  pl.BlockSpec((1, D_MODEL), const),
                  pl.BlockSpec(wu.shape, const), pl.BlockSpec(cw.shape, const),
                  pl.BlockSpec(cb.shape, const), pl.BlockSpec(wd.shape, const),
                  pl.BlockSpec((1, D_MODEL), const)],
        out_specs=pl.BlockSpec((tm, D_MODEL), lambda i: (i, 0)),
        scratch_shapes=[pltpu.VMEM((tm + FFN_HALO, D_MODEL), BF16)],
        compiler_params=_params(1),
        name="conv_ffn",
    )(x2, x2, g, wu, cw, cb, wd, gf)


def kernel(x, rel_bias, norm_mix, w_in, cmp_pe_k, cmp_w1_k, cmp_w2_k, cmp_pe_v, cmp_w1_v, cmp_w2_v,
           w_branch, w_out, norm_ffn, w_up, conv_w, conv_b, w_down, norm_final):
    b, s, d = x.shape
    depth = w_in.shape[0]
    t = b * s
    assert d == D_MODEL and s % T_TILE == 0 and s // NSA_SLC_BLOCK == _N_SLC and t % ROW_TILE == 0
    assert s % ROW_TILE == 0 and all(s % (dil * A_TILE) == 0 for _, dil in A_GROUPS)
    assert all(win // dil == A_TILE for win, dil in A_GROUPS) and A_GROUPS[0][1] == 1
    assert all(ROW_TILE % dil == 0 for _, dil in A_GROUPS)

    hg = A_HEADS_PER_GROUP
    bias_a = jnp.concatenate([
        _bias_tiles(rel_bias,
                    np.concatenate([_bucket_tile(A_TILE, 1, dil, win // dil),
                                    _bucket_tile(A_TILE, 0, dil, win // dil)], axis=1),
                    gi * hg, hg)
        for gi, (win, dil) in enumerate(A_GROUPS)], axis=0)
    assert (_bucket_tile(T_TILE, 2, 1, 3 * T_TILE) == REL_BUCKETS - 1).all()
    bias_b = _bias_tiles_t(rel_bias, [_bucket_tile(T_TILE, d, 1, NSA_WINDOW - 1) for d in (0, 1, 2)],
                           A_HEADS, B_HEADS)
    bias_b = bias_b.reshape(B_KV_HEADS, B_GROUP, 3, T_TILE, T_TILE).transpose(0, 2, 3, 1, 4).reshape(
        B_KV_HEADS, 3, T_TILE, _G_ROWS)
    bias_c = _bias_tiles_t(rel_bias, [_bucket_tile(T_TILE, d, 1, s) for d in (0, 1)],
                           A_HEADS + B_HEADS, C_HEADS)

    w_in_k = _prep_w_in(w_in)
    x2 = x.reshape(t, d)
    for i in range(depth):
        pa0, pa1, pa2, pb, pcmp, pc, mg, bg = _proj_in(x2, norm_mix[i][None, :], w_in_k, layer=i,
                                                       seq=s)

        oa = _mixer_a(pa0.reshape(b, s, _GRP_COLS), pa1, pa2, bias_a)

        cmp = _compress(pcmp.reshape(4, b, s // NSA_CMP_STRIDE, NSA_CMP_STRIDE * HEAD_DIM),
                        cmp_pe_k[i].reshape(1, -1), cmp_pe_v[i].reshape(1, -1),
                        cmp_w1_k[i].astype(BF16), cmp_w1_v[i].astype(BF16),
                        cmp_w2_k[i].astype(BF16), cmp_w2_v[i].astype(BF16))
        ob = _nsa(pb.reshape(b, s, _PB_COLS), cmp, bg.reshape(b, s, _BG_COLS), bias_b)

        oc = _moba(pc.reshape(b, s, _C_COLS), bias_c)

        x2 = _merge(oa.reshape(t, -1), ob.reshape(t, -1), oc.reshape(t, -1), mg, x2,
                    w_branch[i].astype(BF16), w_out[i].astype(BF16))
        x2 = _ffn(x2, norm_ffn[i][None, :], w_up[i].astype(BF16), conv_w[i], conv_b[i][None, :],
                  w_down[i].astype(BF16), norm_final[None, :], seq=s, final_norm=(i == depth - 1))
    return x2.reshape(b, s, d)
```

```python
import functools
import math

import jax
import jax.numpy as jnp
import numpy as np
from jax import lax
from jax.experimental import pallas as pl
from jax.experimental.pallas import tpu as pltpu

F32 = jnp.float32
BF16 = jnp.bfloat16

D_MODEL = 1024
HEAD_DIM = 64
A_GROUPS = ((128, 1), (512, 4), (2048, 16))
A_HEADS_PER_GROUP = 2
A_HEADS = 6
B_HEADS = 6
B_KV_HEADS = 2
B_GROUP = 3
C_HEADS = 4
NSA_CMP_BLOCK = 32
NSA_CMP_STRIDE = 16
NSA_CMP_HIDDEN = 256
NSA_SLC_BLOCK = 64
NSA_SLC_TOPK = 16
NSA_WINDOW = 512
MOBA_BLOCK = 256
MOBA_TOPK = 3
REL_BUCKETS = 32
REL_MAX_DIST = 128
D_FF = 2816
RMS_EPS = 1e-6
NEG_INF = -1e30
FORCE_BONUS = 1e4
ATTN_SCALE = HEAD_DIM ** -0.5

LANE = 128
A_TILE = 128
T_TILE = 256
ROW_TILE = 512
FFN_HALO = 8
VMEM_LIMIT = 56 * 1024 * 1024

_GRP_COLS = 3 * A_HEADS_PER_GROUP * HEAD_DIM
_A_COLS = len(A_GROUPS) * _GRP_COLS
_PB_COLS = B_HEADS * HEAD_DIM + 4 * B_KV_HEADS * HEAD_DIM
_CMP_COLS = 2 * B_KV_HEADS * HEAD_DIM
_C_COLS = 3 * C_HEADS * HEAD_DIM
_MG_COLS = 3 * D_MODEL
_BG_COLS = LANE
_OFF_A = 0
_OFF_PB = _OFF_A + _A_COLS
_OFF_CMP = _OFF_PB + _PB_COLS
_OFF_C = _OFF_CMP + _CMP_COLS
_OFF_MG = _OFF_C + _C_COLS
_OFF_BG = _OFF_MG + _MG_COLS
_W_COLS = _OFF_BG + _BG_COLS


def _params(n_grid):
    return pltpu.CompilerParams(dimension_semantics=("arbitrary",) * n_grid,
                                vmem_limit_bytes=VMEM_LIMIT)


def _rel_bucket_np(dist):
    n = np.maximum(dist, 0)
    exact = REL_BUCKETS // 2
    nf = np.maximum(n, 1).astype(np.float32)
    large = exact + (np.log(nf / np.float32(exact)) / np.float32(math.log(REL_MAX_DIST / exact))
                     * np.float32(REL_BUCKETS - exact)).astype(np.int32)
    return np.where(n < exact, n, np.minimum(large, REL_BUCKETS - 1)).astype(np.int32)


def _bucket_tile(tile, block_offset, dil, max_rel):
    rel = block_offset * tile + np.arange(tile)[:, None] - np.arange(tile)[None, :]
    ok = (rel >= 0) & (rel <= max_rel)
    return np.where(ok, _rel_bucket_np(rel * dil), -1).astype(np.int32)


def _bias_tiles_kernel(tbl_ref, idx_ref, o_ref, *, head0, relative):
    h = pl.program_id(0) + head0
    idx = idx_ref[...]
    acc = jnp.full(idx.shape, NEG_INF, F32)
    for b in range(REL_BUCKETS):
        acc = jnp.where(idx == b, tbl_ref[b, h], acc)
    if relative:
        acc = acc - tbl_ref[REL_BUCKETS - 1, h]
    o_ref[0] = acc


def _bias_tiles(rel_bias, idx, head0, n_heads, relative=False):
    r, c = idx.shape
    return pl.pallas_call(
        functools.partial(_bias_tiles_kernel, head0=head0, relative=relative),
        out_shape=jax.ShapeDtypeStruct((n_heads, r, c), F32),
        grid=(n_heads,),
        in_specs=[pl.BlockSpec(memory_space=pltpu.SMEM),
                  pl.BlockSpec((r, c), lambda h: (0, 0))],
        out_specs=pl.BlockSpec((1, r, c), lambda h: (h, 0, 0)),
        compiler_params=_params(1),
        name="bias_tiles",
    )(rel_bias, jnp.asarray(idx))


def _bias_tiles_t(rel_bias, idx_tiles, head0, n_heads):
    r, c = idx_tiles[0].shape
    flat = _bias_tiles(rel_bias, np.concatenate([t.T for t in idx_tiles], axis=1), head0, n_heads,
                       relative=True)
    return flat.reshape(n_heads, c, len(idx_tiles), r).transpose(0, 2, 1, 3)


def _rmsnorm_rows(x, g):
    return x * lax.rsqrt(jnp.mean(x * x, axis=-1, keepdims=True) + RMS_EPS) * g


def _proj_in_kernel(x_ref, g_ref, w_ref, pa0_ref, pa1_ref, pa2_ref, pb_ref, pcmp_ref, pc_ref,
                    mg_ref, bg_ref, regroup_ref):
    xn = _rmsnorm_rows(x_ref[...], g_ref[...]).astype(BF16)
    tm = xn.shape[0]

    def mm(c0, c1):
        return jnp.dot(xn, w_ref[:, c0:c1], preferred_element_type=F32)

    pa0_ref[...] = mm(_OFF_A, _OFF_A + _GRP_COLS).astype(BF16)
    for gi, out_ref in ((1, pa1_ref), (2, pa2_ref)):
        dil = A_GROUPS[gi][1]
        res = mm(_OFF_A + gi * _GRP_COLS, _OFF_A + (gi + 1) * _GRP_COLS)
        for j in range(3):
            regroup_ref[j] = res[:, j * LANE:(j + 1) * LANE]
        for r in range(dil):
            for j in range(3):
                out_ref[0, r, :, j * LANE:(j + 1) * LANE] = (
                    regroup_ref[j, pl.ds(r, tm // dil, stride=dil), :].astype(BF16))
    pb_ref[:, 0:384] = mm(_OFF_PB, _OFF_PB + 384).astype(BF16)
    pb_ref[:, 384:896] = mm(_OFF_PB + 384, _OFF_PB + 896).astype(BF16)
    cmp = mm(_OFF_CMP, _OFF_CMP + _CMP_COLS).astype(BF16)
    for i in range(4):
        pcmp_ref[i] = cmp[:, i * HEAD_DIM:(i + 1) * HEAD_DIM]
    for c in range(0, _C_COLS, 384):
        pc_ref[:, c:c + 384] = mm(_OFF_C + c, _OFF_C + c + 384).astype(BF16)
    for c in range(0, _MG_COLS, 512):
        mg_ref[:, c:c + 512] = mm(_OFF_MG + c, _OFF_MG + c + 512)
    bg_ref[...] = mm(_OFF_BG, _OFF_BG + _BG_COLS)


def _proj_in(x2, g, w, *, layer, seq):
    t = x2.shape[0]
    tm = ROW_TILE
    b = t // seq
    tps = seq // tm
    row = lambda i: (i, 0)
    d1, d2 = A_GROUPS[1][1], A_GROUPS[2][1]
    sub = lambda i: (i // tps, 0, i % tps, 0)
    return pl.pallas_call(
        _proj_in_kernel,
        out_shape=(jax.ShapeDtypeStruct((t, _GRP_COLS), BF16),
                   jax.ShapeDtypeStruct((b, d1, seq // d1, _GRP_COLS), BF16),
                   jax.ShapeDtypeStruct((b, d2, seq // d2, _GRP_COLS), BF16),
                   jax.ShapeDtypeStruct((t, _PB_COLS), BF16),
                   jax.ShapeDtypeStruct((4, t, HEAD_DIM), BF16),
                   jax.ShapeDtypeStruct((t, _C_COLS), BF16),
                   jax.ShapeDtypeStruct((t, _MG_COLS), F32),
                   jax.ShapeDtypeStruct((t, _BG_COLS), F32)),
        grid=(t // tm,),
        in_specs=[pl.BlockSpec((tm, D_MODEL), row),
                  pl.BlockSpec((1, D_MODEL), lambda i: (0, 0)),
                  pl.BlockSpec((None, D_MODEL, _W_COLS), lambda i: (layer, 0, 0))],
        out_specs=(pl.BlockSpec((tm, _GRP_COLS), row),
                   pl.BlockSpec((1, d1, tm // d1, _GRP_COLS), sub),
                   pl.BlockSpec((1, d2, tm // d2, _GRP_COLS), sub),
                   pl.BlockSpec((tm, _PB_COLS), row),
                   pl.BlockSpec((4, tm, HEAD_DIM), lambda i: (0, i, 0)),
                   pl.BlockSpec((tm, _C_COLS), row),
                   pl.BlockSpec((tm, _MG_COLS), row),
                   pl.BlockSpec((tm, _BG_COLS), row)),
        scratch_shapes=[pltpu.VMEM((3, tm, LANE), F32)],
        compiler_params=_params(1),
        name="proj_in",
    )(x2, g, w)


def _w_in_segments():
    a = A_HEADS * HEAD_DIM
    bq0 = 3 * a
    bkc0 = bq0 + B_HEADS * HEAD_DIM
    bks0 = bkc0 + 2 * B_KV_HEADS * HEAD_DIM
    bg0 = bks0 + 4 * B_KV_HEADS * HEAD_DIM
    c0 = bg0 + B_HEADS * 3
    mg0 = c0 + _C_COLS
    gw = A_HEADS_PER_GROUP * HEAD_DIM
    segs = [(_OFF_A + (gi * 3 + j) * gw, j * a + gi * gw, gw)
            for gi in range(len(A_GROUPS)) for j in range(3)]
    segs += [(_OFF_PB, bq0, B_HEADS * HEAD_DIM),
             (_OFF_PB + B_HEADS * HEAD_DIM, bks0, bg0 - bks0),
             (_OFF_CMP, bkc0, _CMP_COLS), (_OFF_C, c0, _C_COLS), (_OFF_MG, mg0, _MG_COLS),
             (_OFF_BG, bg0, B_HEADS * 3)]
    return segs


def _prep_w_in_kernel(w_ref, o_ref):
    o_ref[0, :, _OFF_BG:] = jnp.zeros((o_ref.shape[1], _BG_COLS), BF16)
    for dst, src, width in _w_in_segments():
        for c in range(0, width, 768):
            n = min(768, width - c)
            o_ref[0, :, dst + c:dst + c + n] = w_ref[0, :, src + c:src + c + n].astype(BF16)


def _prep_w_in(w):
    depth, rows, cols = w.shape
    rb = 128
    return pl.pallas_call(
        _prep_w_in_kernel,
        out_shape=jax.ShapeDtypeStruct((depth, rows, _W_COLS), BF16),
        grid=(depth, rows // rb),
        in_specs=[pl.BlockSpec((1, rb, cols), lambda l, r: (l, r, 0))],
        out_specs=pl.BlockSpec((1, rb, _W_COLS), lambda l, r: (l, r, 0)),
        compiler_params=_params(2),
        name="prep_w_in",
    )(w)


_A_UNROLL = 3


def _mixer_a_kernel(pa0_ref, pa1_ref, pa2_ref, bias_ref, o_ref, on_ref, ln_ref, *, seq):
    gw = A_HEADS_PER_GROUP * HEAD_DIM

    def band_block(src, l0, first, gi, rows):
        outs, lses = [], []
        for h in range(A_HEADS_PER_GROUP):
            qc, kc, vc = (slice(j * gw + h * HEAD_DIM, j * gw + (h + 1) * HEAD_DIM) for j in range(3))
            q = src[pl.ds(l0, A_TILE), qc] * ATTN_SCALE
            if first:
                kr = pl.ds(l0, A_TILE)
                bias = bias_ref[A_HEADS_PER_GROUP * gi + h, :, A_TILE:]
            else:
                kr = pl.ds(l0 - A_TILE, 2 * A_TILE)
                bias = bias_ref[A_HEADS_PER_GROUP * gi + h]
            s = _nt(q, src[kr, kc]) + bias
            m = jnp.max(s, axis=-1, keepdims=True)
            e = jnp.exp(s - m)
            l = jnp.sum(e, axis=-1, keepdims=True)
            outs.append(jnp.dot(e.astype(BF16), src[kr, vc], preferred_element_type=F32) / l)
            lses.append(jnp.broadcast_to(m + jnp.log(l), (A_TILE, HEAD_DIM)))
        on_ref[gi, rows, :] = jnp.concatenate(outs, axis=-1)
        ln_ref[gi, rows, :] = jnp.concatenate(lses, axis=-1)

    for gi, ((_, dil), pa_ref) in enumerate(zip(A_GROUPS, (pa0_ref, pa1_ref, pa2_ref))):
        n_blocks = seq // dil // A_TILE
        if dil == 1:
            band_block(pa_ref.at[0], 0, True, gi, pl.ds(0, A_TILE))
            assert (n_blocks - 1) % _A_UNROLL == 0

            def body0(it, c, pa_ref=pa_ref, gi=gi):
                for u in range(_A_UNROLL):
                    l0 = pl.multiple_of((1 + it * _A_UNROLL + u) * A_TILE, A_TILE)
                    band_block(pa_ref.at[0], l0, False, gi, pl.ds(l0, A_TILE))
                return c
            lax.fori_loop(0, (n_blocks - 1) // _A_UNROLL, body0, 0)
        elif n_blocks > 1:
            for r in range(dil):
                band_block(pa_ref.at[0, r], 0, True, gi, pl.ds(r, A_TILE, stride=dil))

            def body1(i, c, pa_ref=pa_ref, gi=gi, dil=dil):
                l0 = pl.multiple_of(i * A_TILE, A_TILE)
                for r in range(dil):
                    band_block(pa_ref.at[0, r], l0, False, gi, pl.ds(l0 * dil + r, A_TILE, stride=dil))
                return c
            lax.fori_loop(1, n_blocks, body1, 0)
        else:
            per_iter = 4

            def body2(it, c, pa_ref=pa_ref, gi=gi, dil=dil):
                for u in range(per_iter):
                    r = it * per_iter + u
                    band_block(pa_ref.at[0, r], 0, True, gi, pl.ds(r, A_TILE, stride=dil))
                return c
            lax.fori_loop(0, dil // per_iter, body2, 0)

    chunk = T_TILE
    for c0 in range(0, seq, chunk):
        rows = slice(c0, c0 + chunk)
        lse = [ln_ref[g, rows, :] for g in range(len(A_GROUPS))]
        top = functools.reduce(jnp.maximum, lse)
        ex = [jnp.exp(v - top) for v in lse]
        den = functools.reduce(lambda a, b: a + b, ex)
        for g in range(len(A_GROUPS)):
            o_ref[0, rows, g * gw:(g + 1) * gw] = (ex[g] / den * on_ref[g, rows, :]).astype(BF16)


def _mixer_a(pa0, pa1, pa2, bias):
    b, seq, _ = pa0.shape
    return pl.pallas_call(
        functools.partial(_mixer_a_kernel, seq=seq),
        out_shape=jax.ShapeDtypeStruct((b, seq, _GRP_COLS), BF16),
        grid=(b,),
        in_specs=[pl.BlockSpec((1, seq, _GRP_COLS), lambda i: (i, 0, 0)),
                  pl.BlockSpec((1,) + pa1.shape[1:], lambda i: (i, 0, 0, 0)),
                  pl.BlockSpec((1,) + pa2.shape[1:], lambda i: (i, 0, 0, 0)),
                  pl.BlockSpec(bias.shape, lambda i: (0, 0, 0))],
        out_specs=pl.BlockSpec((1, seq, _GRP_COLS), lambda i: (i, 0, 0)),
        scratch_shapes=[pltpu.VMEM((len(A_GROUPS), seq, LANE), F32),
                        pltpu.VMEM((len(A_GROUPS), seq, LANE), F32)],
        compiler_params=_params(1),
        name="mixer_a",
    )(pa0, pa1, pa2, bias)


def _compress_kernel(r_ref, pek_ref, pev_ref, w1k_ref, w1v_ref, w2k_ref, w2v_ref, o_ref):
    half = NSA_CMP_STRIDE * HEAD_DIM
    for kv, (pe_ref, w1_ref, w2_ref) in enumerate(((pek_ref, w1k_ref, w2k_ref),
                                                   (pev_ref, w1v_ref, w2v_ref))):
        for h in range(B_KV_HEADS):
            r = r_ref[kv * B_KV_HEADS + h, 0].astype(F32)
            lo = jnp.dot((r + pe_ref[:, :half]).astype(BF16), w1_ref[:half, :],
                         preferred_element_type=F32)
            hi = jnp.dot((r + pe_ref[:, half:]).astype(BF16), w1_ref[half:, :],
                         preferred_element_type=F32)
            hid = lo + pltpu.roll(hi, hi.shape[0] - 1, axis=0)
            act = jax.nn.gelu(hid).astype(BF16)
            o_ref[0, kv * B_KV_HEADS + h] = jnp.dot(act, w2_ref[...],
                                                    preferred_element_type=F32).astype(BF16)


def _compress(r, pe_k, pe_v, w1_k, w1_v, w2_k, w2_v):
    _, b, m, c = r.shape
    const2 = lambda i: (0, 0)
    return pl.pallas_call(
        _compress_kernel,
        out_shape=jax.ShapeDtypeStruct((b, 4, m, HEAD_DIM), BF16),
        grid=(b,),
        in_specs=[pl.BlockSpec((4, 1, m, c), lambda i: (0, i, 0, 0)),
                  pl.BlockSpec(pe_k.shape, const2), pl.BlockSpec(pe_v.shape, const2),
                  pl.BlockSpec(w1_k.shape, const2), pl.BlockSpec(w1_v.shape, const2),
                  pl.BlockSpec(w2_k.shape, const2), pl.BlockSpec(w2_v.shape, const2)],
        out_specs=pl.BlockSpec((1, 4, m, HEAD_DIM), lambda i: (i, 0, 0, 0)),
        compiler_params=_params(1),
        name="nsa_compress",
    )(r, pe_k, pe_v, w1_k, w1_v, w2_k, w2_v)


MASK_BIG = 2.0 ** 100
SEL_PAD = 32


def _nt(a, b, **kw):
    return lax.dot_general(a, b, (((1,), (1,)), ((), ())), preferred_element_type=F32, **kw)


def _top_mask_t(score, n_rows, n_top):
    sub = lax.broadcasted_iota(jnp.int32, score.shape, 0)
    rank = jnp.zeros(score.shape, F32)
    for j in range(n_rows):
        row = score[j:j + 1, :]
        ge = jnp.where(row >= score, 1.0, 0.0)
        gt = jnp.where(row > score, 1.0, 0.0)
        rank = rank + jnp.where(sub > j, ge, gt)
    return rank < n_top


BAND_SLOTS = 4


def _two_pass_attention(i, streams, band_streams=()):
    groups = ((streams, lambda j: j), (band_streams, lambda j: j % BAND_SLOTS))

    def key_rows(j):
        return pl.ds(pl.multiple_of(j * T_TILE, T_TILE), T_TILE)

    def logits(group, j, d):
        some, slot = group
        for k_tile, q_t, _, bias_tile, s_ref, rmp_ref, _ in some:
            s = jnp.dot(k_tile(key_rows(j)), q_t(), preferred_element_type=F32)
            if d is not None:
                s = s + bias_tile(d)
            s_ref[slot(j)] = s
            fold = jnp.max(s.reshape(s.shape[0] // 8, 8, s.shape[1]), axis=0)
            rmp_ref[...] = fold if d == 0 else jnp.maximum(rmp_ref[...], fold)

    def near(n_full, n_band):
        for d in range(n_full):
            logits(groups[0], i - d, d)
        for d in range(n_band):
            logits(groups[1], i - d, d)

    if band_streams:
        pl.when(i == 0)(lambda: near(1, 1))
        pl.when(i == 1)(lambda: near(2, 2))
        pl.when(i >= 2)(lambda: near(2, 3))
    else:
        logits(groups[0], i, 0)
        pl.when(i >= 1)(lambda: logits(groups[0], i - 1, 1))

    def far(j, c):
        logits(groups[0], j, None)
        return c
    lax.fori_loop(0, i - 1, far, 0)

    tops = {}
    for some, _ in groups:
        for stream in some:
            rmp_ref, acc_ref = stream[-2:]
            tops[id(rmp_ref)] = jnp.max(rmp_ref[...], axis=0, keepdims=True)
            acc_ref[...] = jnp.zeros(acc_ref.shape, F32)

    def pv(active):
        def body(j, c):
            for some, slot in active:
                for _, _, vt_ref, _, s_ref, rmp_ref, acc_ref in some:
                    e = jnp.exp(s_ref[slot(j)] - tops[id(rmp_ref)])
                    acc_ref[...] += jnp.dot(vt_ref[j], e.astype(BF16), preferred_element_type=F32)
            return c
        return body

    if band_streams:
        band_lo = jnp.maximum(i - 2, 0)
        lax.fori_loop(0, band_lo, pv(groups[:1]), 0)
        lax.fori_loop(band_lo, i + 1, pv(groups), 0)
    else:
        lax.fori_loop(0, i + 1, pv(groups[:1]), 0)

    outs = []
    for some, _ in groups:
        for *_, acc_ref in some:
            acc = acc_ref[...]
            outs.append(acc[:HEAD_DIM, :] / acc[HEAD_DIM:HEAD_DIM + 1, :])
    return outs


def _eye(n, dtype=BF16):
    return (lax.broadcasted_iota(jnp.int32, (n, n), 0)
            == lax.broadcasted_iota(jnp.int32, (n, n), 1)).astype(dtype)


def _transposed_values(v_ref_rows, n_tiles):
    eye = _eye(LANE)
    for j in range(n_tiles):
        v = _with_ones_column(v_ref_rows(slice(j * T_TILE, (j + 1) * T_TILE)))
        yield j, _nt(eye, v).astype(BF16)


def _with_ones_column(v):
    one = (lax.broadcasted_iota(jnp.int32, v.shape, 1) == 0).astype(F32)
    return jnp.concatenate([v.astype(F32), one], axis=1).astype(BF16)


def _with_block_onehot(k, block):
    n = k.shape[0]
    shape = (n, SEL_PAD)
    hot = (lax.broadcasted_iota(jnp.int32, shape, 0) // block
           == lax.broadcasted_iota(jnp.int32, shape, 1)).astype(F32)
    return jnp.concatenate([k.astype(F32), hot, jnp.zeros(shape, F32)], axis=1).astype(BF16)


def _augment_q_t(q_t, allowed_t):
    pen = (allowed_t - 1.0) * MASK_BIG
    return jnp.concatenate([q_t, pen, jnp.zeros(pen.shape, F32)], axis=0).astype(BF16)


_N_SLC = 32
_G_ROWS = B_GROUP * T_TILE


def _nsa_kernel(pb_ref, cmp_ref, bg_ref, bias_ref, o_ref,
                kaug_ref, vst_ref, vwt_ref, eye_ref, q3t_ref, qaugt_ref, s_ref, sw_ref, rmp_ref,
                acc_ref, ocmp_ref, *, seq):
    n_tiles = seq // T_TILE
    n_cmp_rows = seq // NSA_CMP_STRIDE
    q_cols = B_HEADS * HEAD_DIM
    ks0, vs0, kw0, vw0 = (q_cols + i * B_KV_HEADS * HEAD_DIM for i in range(4))

    def head_cols(c0, h):
        return slice(c0 + h * HEAD_DIM, c0 + (h + 1) * HEAD_DIM)

    j_id = lax.broadcasted_iota(jnp.int32, (_N_SLC, n_cmp_rows), 0)
    c_id = lax.broadcasted_iota(jnp.int32, (_N_SLC, n_cmp_rows), 1)
    overlap_t = ((c_id * NSA_CMP_STRIDE < (j_id + 1) * NSA_SLC_BLOCK)
                 & (c_id * NSA_CMP_STRIDE + NSA_CMP_BLOCK > j_id * NSA_SLC_BLOCK)).astype(F32)
    eye_ref[...] = _eye(T_TILE)
    for h in range(B_KV_HEADS):
        kaug_ref[h] = _with_block_onehot(pb_ref[0, :, head_cols(ks0, h)], NSA_SLC_BLOCK)
        for j, vt in _transposed_values(lambda r, h=h: pb_ref[0, r, head_cols(vs0, h)], n_tiles):
            vst_ref[h, j] = vt
        for j, vt in _transposed_values(lambda r, h=h: pb_ref[0, r, head_cols(vw0, h)], n_tiles):
            vwt_ref[h, j] = vt

    def tile_body(i, carry):
        r0 = pl.multiple_of(i * T_TILE, T_TILE)
        rows = pl.ds(r0, T_TILE)
        pick = (lax.broadcasted_iota(jnp.int32, (SEL_PAD, _BG_COLS), 0)
                == lax.broadcasted_iota(jnp.int32, (SEL_PAD, _BG_COLS), 1)).astype(F32)
        sig_t = jax.nn.sigmoid(_nt(pick, bg_ref[0, rows, :], precision=lax.Precision.HIGHEST))
        eye_h = _eye(HEAD_DIM)

        for h in range(B_KV_HEADS):
            q3t = jnp.concatenate(
                [_nt(eye_h, pb_ref[0, rows, head_cols(0, h * B_GROUP + g)]) for g in range(B_GROUP)],
                axis=1) * ATTN_SCALE
            q3t_bf = q3t.astype(BF16)
            q3t_ref[h] = q3t_bf

            kc = cmp_ref[0, h]
            vct = _nt(eye_h, cmp_ref[0, B_KV_HEADS + h]).astype(BF16)
            c_end = (lax.broadcasted_iota(jnp.int32, (n_cmp_rows, _G_ROWS), 0) * NSA_CMP_STRIDE
                     + NSA_CMP_BLOCK - 1)
            t_pos = r0 + lax.broadcasted_iota(jnp.int32, (n_cmp_rows, _G_ROWS), 1) % T_TILE
            ok = (c_end <= t_pos) & (c_end < seq)
            lc = jnp.where(ok, jnp.dot(kc, q3t_bf, preferred_element_type=F32), NEG_INF)
            ec = jnp.exp(lc - jnp.max(lc, axis=0, keepdims=True))
            pc = ec / jnp.sum(ec, axis=0, keepdims=True) * ok.astype(F32)
            ocmp_ref[h] = jnp.dot(vct, pc.astype(BF16), preferred_element_type=F32)
            p_sum = pc[:, 0:T_TILE] + pc[:, T_TILE:2 * T_TILE] + pc[:, 2 * T_TILE:3 * T_TILE]
            imp_t = jnp.dot(overlap_t, p_sum, precision=lax.Precision.HIGHEST,
                            preferred_element_type=F32)

            j_sub = lax.broadcasted_iota(jnp.int32, (_N_SLC, T_TILE), 0)
            jt = (r0 + lax.broadcasted_iota(jnp.int32, (_N_SLC, T_TILE), 1)) // NSA_SLC_BLOCK
            forced = (j_sub == 0) | (j_sub == jt) | (j_sub == jt - 1)
            valid = j_sub <= jt
            score = jnp.where(valid, imp_t + FORCE_BONUS * forced.astype(F32), NEG_INF)
            sel_t = jnp.where(valid & _top_mask_t(score, _N_SLC, NSA_SLC_TOPK), 1.0, 0.0)
            qaugt_ref[h] = _augment_q_t(q3t, jnp.concatenate([sel_t] * B_GROUP, axis=1))

        outs = _two_pass_attention(
            i,
            [(lambda kr, h=h: kaug_ref[h, kr, :], lambda h=h: qaugt_ref[h], vst_ref.at[h],
              lambda d, h=h: bias_ref[h, d], s_ref.at[h], rmp_ref.at[0, h], acc_ref.at[0, h])
             for h in range(B_KV_HEADS)],
            [(lambda kr, h=h: pb_ref[0, kr, head_cols(kw0, h)], lambda h=h: q3t_ref[h],
              vwt_ref.at[h], lambda d, h=h: bias_ref[h, d], sw_ref.at[h], rmp_ref.at[1, h],
              acc_ref.at[1, h]) for h in range(B_KV_HEADS)])
        oslc, owin = outs[:B_KV_HEADS], outs[B_KV_HEADS:]

        heads = []
        for h in range(B_KV_HEADS):
            for g in range(B_GROUP):
                head = h * B_GROUP + g
                gc = slice(g * T_TILE, (g + 1) * T_TILE)
                heads.append(sig_t[3 * head:3 * head + 1, :] * ocmp_ref[h, :, gc]
                             + sig_t[3 * head + 1:3 * head + 2, :] * oslc[h][:, gc]
                             + sig_t[3 * head + 2:3 * head + 3, :] * owin[h][:, gc])
        o_t = jnp.concatenate(heads, axis=0).astype(BF16)
        o_ref[0, rows, :] = _nt(eye_ref[...], o_t).astype(o_ref.dtype)
        return carry

    lax.fori_loop(0, n_tiles, tile_body, 0)


def _nsa(pb, cmp, bg, bias):
    b, seq, _ = pb.shape
    n_tiles = seq // T_TILE
    return pl.pallas_call(
        functools.partial(_nsa_kernel, seq=seq),
        out_shape=jax.ShapeDtypeStruct((b, seq, B_HEADS * HEAD_DIM), BF16),
        grid=(b,),
        in_specs=[pl.BlockSpec((1, seq, _PB_COLS), lambda i: (i, 0, 0)),
                  pl.BlockSpec((1,) + cmp.shape[1:], lambda i: (i, 0, 0, 0)),
                  pl.BlockSpec((1, seq, _BG_COLS), lambda i: (i, 0, 0)),
                  pl.BlockSpec(bias.shape, lambda i: (0, 0, 0, 0))],
        out_specs=pl.BlockSpec((1, seq, B_HEADS * HEAD_DIM), lambda i: (i, 0, 0)),
        scratch_shapes=[pltpu.VMEM((B_KV_HEADS, seq, LANE), BF16),
                        pltpu.VMEM((B_KV_HEADS, n_tiles, LANE, T_TILE), BF16),
                        pltpu.VMEM((B_KV_HEADS, n_tiles, LANE, T_TILE), BF16),
                        pltpu.VMEM((T_TILE, T_TILE), BF16),
                        pltpu.VMEM((B_KV_HEADS, HEAD_DIM, _G_ROWS), BF16),
                        pltpu.VMEM((B_KV_HEADS, LANE, _G_ROWS), BF16),
                        pltpu.VMEM((B_KV_HEADS, n_tiles, T_TILE, _G_ROWS), F32),
                        pltpu.VMEM((B_KV_HEADS, BAND_SLOTS, T_TILE, _G_ROWS), F32),
                        pltpu.VMEM((2, B_KV_HEADS, 8, _G_ROWS), F32),
                        pltpu.VMEM((2, B_KV_HEADS, LANE, _G_ROWS), F32),
                        pltpu.VMEM((B_KV_HEADS, HEAD_DIM, _G_ROWS), F32)],
        compiler_params=_params(1),
        name="nsa",
    )(pb, cmp, bg, bias)


def _moba_kernel(pc_ref, bias_ref, o_ref, kaug_ref, vt_ref, kmean_ref, eye_ref, qaugt_ref, s_ref,
                 rmp_ref, acc_ref, *, seq):
    n_blk = seq // MOBA_BLOCK
    width = C_HEADS * HEAD_DIM
    eye_ref[...] = _eye(T_TILE)

    def cols(j, h):
        return slice(j * width + h * HEAD_DIM, j * width + (h + 1) * HEAD_DIM)

    for h in range(C_HEADS):
        kaug_ref[h] = _with_block_onehot(pc_ref[0, :, cols(1, h)], MOBA_BLOCK)
        for j, vt in _transposed_values(lambda r, h=h: pc_ref[0, r, cols(2, h)], n_blk):
            vt_ref[h, j] = vt
        kmean_ref[h] = jnp.concatenate(
            [jnp.mean(pc_ref[0, n * MOBA_BLOCK:(n + 1) * MOBA_BLOCK, cols(1, h)].astype(F32), axis=0,
                      keepdims=True) for n in range(n_blk)]
            + [jnp.zeros((SEL_PAD - n_blk, HEAD_DIM), F32)], axis=0)

    def tile_body(i, carry):
        rows = pl.ds(pl.multiple_of(i * MOBA_BLOCK, MOBA_BLOCK), MOBA_BLOCK)
        n_sub = lax.broadcasted_iota(jnp.int32, (SEL_PAD, MOBA_BLOCK), 0)
        past = n_sub < i
        eye_h = _eye(HEAD_DIM)
        for h in range(C_HEADS):
            q_t = _nt(eye_h, pc_ref[0, rows, cols(0, h)])
            gate_t = jnp.dot(kmean_ref[h], q_t, precision=lax.Precision.HIGHEST,
                             preferred_element_type=F32)
            score = jnp.where(past, gate_t, NEG_INF)
            picked = past & _top_mask_t(score, n_blk, MOBA_TOPK)
            allowed_t = jnp.where(picked | (n_sub == i), 1.0, 0.0)
            qaugt_ref[h] = _augment_q_t(q_t * ATTN_SCALE, allowed_t)

        outs = _two_pass_attention(
            i, [(lambda kr, h=h: kaug_ref[h, kr, :], lambda h=h: qaugt_ref[h], vt_ref.at[h],
                 lambda d, h=h: bias_ref[h, d], s_ref.at[h], rmp_ref.at[h], acc_ref.at[h])
                for h in range(C_HEADS)])
        o_t = jnp.concatenate(outs, axis=0).astype(BF16)
        o_ref[0, rows, :] = _nt(eye_ref[...], o_t).astype(o_ref.dtype)
        return carry

    lax.fori_loop(0, n_blk, tile_body, 0)


def _moba(pc, bias):
    b, seq, _ = pc.shape
    width = C_HEADS * HEAD_DIM
    n_tiles = seq // T_TILE
    return pl.pallas_call(
        functools.partial(_moba_kernel, seq=seq),
        out_shape=jax.ShapeDtypeStruct((b, seq, width), BF16),
        grid=(b,),
        in_specs=[pl.BlockSpec((1, seq, _C_COLS), lambda i: (i, 0, 0)),
                  pl.BlockSpec(bias.shape, lambda i: (0, 0, 0, 0))],
        out_specs=pl.BlockSpec((1, seq, width), lambda i: (i, 0, 0)),
        scratch_shapes=[pltpu.VMEM((C_HEADS, seq, LANE), BF16),
                        pltpu.VMEM((C_HEADS, n_tiles, LANE, T_TILE), BF16),
                        pltpu.VMEM((C_HEADS, SEL_PAD, HEAD_DIM), F32),
                        pltpu.VMEM((T_TILE, T_TILE), BF16),
                        pltpu.VMEM((C_HEADS, LANE, T_TILE), BF16),
                        pltpu.VMEM((C_HEADS, n_tiles, T_TILE, T_TILE), F32),
                        pltpu.VMEM((C_HEADS, 8, T_TILE), F32),
                        pltpu.VMEM((C_HEADS, LANE, T_TILE), F32)],
        compiler_params=_params(1),
        name="moba",
    )(pc, bias)


def _merge_kernel(oa_ref, ob_ref, oc_ref, mg_ref, x_ref, wb_ref, wo_ref, y_ref):
    r0 = A_HEADS * HEAD_DIM
    r1 = r0 + B_HEADS * HEAD_DIM
    ya = jnp.dot(oa_ref[...], wb_ref[0:r0, :], preferred_element_type=F32)
    yb = jnp.dot(ob_ref[...], wb_ref[r0:r1, :], preferred_element_type=F32)
    yc = jnp.dot(oc_ref[...], wb_ref[r1:, :], preferred_element_type=F32)
    merged = (jax.nn.sigmoid(mg_ref[:, 0:D_MODEL]) * ya
              + jax.nn.sigmoid(mg_ref[:, D_MODEL:2 * D_MODEL]) * yb
              + jax.nn.sigmoid(mg_ref[:, 2 * D_MODEL:]) * yc)
    y_ref[...] = x_ref[...] + jnp.dot(merged.astype(BF16), wo_ref[...], preferred_element_type=F32)


def _merge(oa, ob, oc, mg, x2, wb, wo):
    t = x2.shape[0]
    tm = ROW_TILE
    row = lambda i: (i, 0)
    return pl.pallas_call(
        _merge_kernel,
        out_shape=jax.ShapeDtypeStruct((t, D_MODEL), F32),
        grid=(t // tm,),
        in_specs=[pl.BlockSpec((tm, oa.shape[1]), row),
                  pl.BlockSpec((tm, ob.shape[1]), row), pl.BlockSpec((tm, oc.shape[1]), row),
                  pl.BlockSpec((tm, _MG_COLS), row), pl.BlockSpec((tm, D_MODEL), row),
                  pl.BlockSpec(wb.shape, lambda i: (0, 0)), pl.BlockSpec(wo.shape, lambda i: (0, 0))],
        out_specs=pl.BlockSpec((tm, D_MODEL), row),
        compiler_params=_params(1),
        name="merge",
    )(oa, ob, oc, mg, x2, wb, wo)


_FF_CHUNK = D_FF // 2


def _ffn_kernel(x_ref, halo_ref, g_ref, wu_ref, cw_ref, cb_ref, wd_ref, gf_ref, y_ref, xn_ref,
                *, tiles_per_seq, final_norm):
    i = pl.program_id(0)
    x = x_ref[...]
    g = g_ref[...]
    keep = (i % tiles_per_seq != 0).astype(F32)
    xn_ref[0:FFN_HALO, :] = (_rmsnorm_rows(halo_ref[...], g) * keep).astype(BF16)
    xn_ref[FFN_HALO:, :] = _rmsnorm_rows(x, g).astype(BF16)
    xn = xn_ref[...]
    rows = xn.shape[0]

    def conv(c0):
        hcol = jnp.dot(xn, wu_ref[:, c0:c0 + _FF_CHUNK], preferred_element_type=F32)
        out = (cw_ref[2:3, c0:c0 + _FF_CHUNK] * hcol
               + cw_ref[1:2, c0:c0 + _FF_CHUNK] * pltpu.roll(hcol, 1, axis=0)
               + cw_ref[0:1, c0:c0 + _FF_CHUNK] * pltpu.roll(hcol, 2, axis=0)
               + cb_ref[:, c0:c0 + _FF_CHUNK])
        return out[FFN_HALO:rows]

    acc = x
    for c in range(0, D_FF, _FF_CHUNK):
        a = conv(c)
        u = conv(D_FF + c)
        act = (a * jax.nn.sigmoid(a) * u).astype(BF16)
        acc = acc + jnp.dot(act, wd_ref[c:c + _FF_CHUNK, :], preferred_element_type=F32)
    if final_norm:
        acc = _rmsnorm_rows(acc, gf_ref[...])
    y_ref[...] = acc


def _ffn(x2, g, wu, cw, cb, wd, gf, *, seq, final_norm):
    t = x2.shape[0]
    tm = ROW_TILE
    const = lambda i: (0, 0)
    halo_blocks = tm // FFN_HALO
    return pl.pallas_call(
        functools.partial(_ffn_kernel, tiles_per_seq=seq // tm, final_norm=final_norm),
        out_shape=jax.ShapeDtypeStruct((t, D_MODEL), F32),
        grid=(t // tm,),
        in_specs=[pl.BlockSpec((tm, D_MODEL), lambda i: (i, 0)),
                  pl.BlockSpec((FFN_HALO, D_MODEL), lambda i: (jnp.maximum(i * halo_blocks - 1, 0), 0)),
                  pl.BlockSpec((1, D_MODEL), const),
                  pl.BlockSpec(wu.shape, const), pl.BlockSpec(cw.shape, const),
                  pl.BlockSpec(cb.shape, const), pl.BlockSpec(wd.shape, const),
                  pl.BlockSpec((1, D_MODEL), const)],
        out_specs=pl.BlockSpec((tm, D_MODEL), lambda i: (i, 0)),
        scratch_shapes=[pltpu.VMEM((tm + FFN_HALO, D_MODEL), BF16)],
        compiler_params=_params(1),
        name="conv_ffn",
    )(x2, x2, g, wu, cw, cb, wd, gf)


def kernel(x, rel_bias, norm_mix, w_in, cmp_pe_k, cmp_w1_k, cmp_w2_k, cmp_pe_v, cmp_w1_v, cmp_w2_v,
           w_branch, w_out, norm_ffn, w_up, conv_w, conv_b, w_down, norm_final):
    b, s, d = x.shape
    depth = w_in.shape[0]
    t = b * s
    assert d == D_MODEL and s % T_TILE == 0 and s // NSA_SLC_BLOCK == _N_SLC and t % ROW_TILE == 0
    assert s % ROW_TILE == 0 and all(s % (dil * A_TILE) == 0 for _, dil in A_GROUPS)
    assert all(win // dil == A_TILE for win, dil in A_GROUPS) and A_GROUPS[0][1] == 1
    assert all(ROW_TILE % dil == 0 for _, dil in A_GROUPS)

    hg = A_HEADS_PER_GROUP
    bias_a = jnp.concatenate([
        _bias_tiles(rel_bias,
                    np.concatenate([_bucket_tile(A_TILE, 1, dil, win // dil),
                                    _bucket_tile(A_TILE, 0, dil, win // dil)], axis=1),
                    gi * hg, hg)
        for gi, (win, dil) in enumerate(A_GROUPS)], axis=0)
    assert (_bucket_tile(T_TILE, 2, 1, 3 * T_TILE) == REL_BUCKETS - 1).all()
    bias_b = _bias_tiles_t(rel_bias, [_bucket_tile(T_TILE, d, 1, NSA_WINDOW - 1) for d in (0, 1, 2)],
                           A_HEADS, B_HEADS)
    bias_b = bias_b.reshape(B_KV_HEADS, B_GROUP, 3, T_TILE, T_TILE).transpose(0, 2, 3, 1, 4).reshape(
        B_KV_HEADS, 3, T_TILE, _G_ROWS)
    bias_c = _bias_tiles_t(rel_bias, [_bucket_tile(T_TILE, d, 1, s) for d in (0, 1)],
                           A_HEADS + B_HEADS, C_HEADS)

    w_in_k = _prep_w_in(w_in)
    x2 = x.reshape(t, d)
    for i in range(depth):
        pa0, pa1, pa2, pb, pcmp, pc, mg, bg = _proj_in(x2, norm_mix[i][None, :], w_in_k, layer=i,
                                                       seq=s)

        oa = _mixer_a(pa0.reshape(b, s, _GRP_COLS), pa1, pa2, bias_a)

        cmp = _compress(pcmp.reshape(4, b, s // NSA_CMP_STRIDE, NSA_CMP_STRIDE * HEAD_DIM),
                        cmp_pe_k[i].reshape(1, -1), cmp_pe_v[i].reshape(1, -1),
                        cmp_w1_k[i].astype(BF16), cmp_w1_v[i].astype(BF16),
                        cmp_w2_k[i].astype(BF16), cmp_w2_v[i].astype(BF16))
        ob = _nsa(pb.reshape(b, s, _PB_COLS), cmp, bg.reshape(b, s, _BG_COLS), bias_b)

        oc = _moba(pc.reshape(b, s, _C_COLS), bias_c)

        x2 = _merge(oa.reshape(t, -1), ob.reshape(t, -1), oc.reshape(t, -1), mg, x2,
                    w_branch[i].astype(BF16), w_out[i].astype(BF16))
        x2 = _ffn(x2, norm_ffn[i][None, :], w_up[i].astype(BF16), conv_w[i], conv_b[i][None, :],
                  w_down[i].astype(BF16), norm_final[None, :], seq=s, final_norm=(i == depth - 1))
    return x2.reshape(b, s, d)
```

```python
import functools
import math

import jax
import jax.numpy as jnp
import numpy as np
from jax import lax
from jax.experimental import pallas as pl
from jax.experimental.pallas import tpu as pltpu

F32 = jnp.float32
BF16 = jnp.bfloat16

D_MODEL = 1024
HEAD_DIM = 64
A_GROUPS = ((128, 1), (512, 4), (2048, 16))
A_HEADS_PER_GROUP = 2
A_HEADS = 6
B_HEADS = 6
B_KV_HEADS = 2
B_GROUP = 3
C_HEADS = 4
NSA_CMP_BLOCK = 32
NSA_CMP_STRIDE = 16
NSA_CMP_HIDDEN = 256
NSA_SLC_BLOCK = 64
NSA_SLC_TOPK = 16
NSA_WINDOW = 512
MOBA_BLOCK = 256
MOBA_TOPK = 3
REL_BUCKETS = 32
REL_MAX_DIST = 128
D_FF = 2816
RMS_EPS = 1e-6
NEG_INF = -1e30
FORCE_BONUS = 1e4
ATTN_SCALE = HEAD_DIM ** -0.5

LANE = 128
A_TILE = 128
T_TILE = 256
ROW_TILE = 512
FFN_HALO = 8
VMEM_LIMIT = 56 * 1024 * 1024

_GRP_COLS = 3 * A_HEADS_PER_GROUP * HEAD_DIM
_A_COLS = len(A_GROUPS) * _GRP_COLS
_PB_COLS = B_HEADS * HEAD_DIM + 4 * B_KV_HEADS * HEAD_DIM
_CMP_COLS = 2 * B_KV_HEADS * HEAD_DIM
_C_COLS = 3 * C_HEADS * HEAD_DIM
_MG_COLS = 3 * D_MODEL
_BG_COLS = LANE
_OFF_A = 0
_OFF_PB = _OFF_A + _A_COLS
_OFF_CMP = _OFF_PB + _PB_COLS
_OFF_C = _OFF_CMP + _CMP_COLS
_OFF_MG = _OFF_C + _C_COLS
_OFF_BG = _OFF_MG + _MG_COLS
_W_COLS = _OFF_BG + _BG_COLS


def _params(n_grid):
    return pltpu.CompilerParams(dimension_semantics=("arbitrary",) * n_grid,
                                vmem_limit_bytes=VMEM_LIMIT)


def _rel_bucket_np(dist):
    n = np.maximum(dist, 0)
    exact = REL_BUCKETS // 2
    nf = np.maximum(n, 1).astype(np.float32)
    large = exact + (np.log(nf / np.float32(exact)) / np.float32(math.log(REL_MAX_DIST / exact))
                     * np.float32(REL_BUCKETS - exact)).astype(np.int32)
    return np.where(n < exact, n, np.minimum(large, REL_BUCKETS - 1)).astype(np.int32)


def _bucket_tile(tile, block_offset, dil, max_rel):
    rel = block_offset * tile + np.arange(tile)[:, None] - np.arange(tile)[None, :]
    ok = (rel >= 0) & (rel <= max_rel)
    return np.where(ok, _rel_bucket_np(rel * dil), -1).astype(np.int32)


def _bias_tiles_kernel(tbl_ref, idx_ref, o_ref, *, head0, relative):
    h = pl.program_id(0) + head0
    idx = idx_ref[...]
    acc = jnp.full(idx.shape, NEG_INF, F32)
    for b in range(REL_BUCKETS):
        acc = jnp.where(idx == b, tbl_ref[b, h], acc)
    if relative:
        acc = acc - tbl_ref[REL_BUCKETS - 1, h]
    o_ref[0] = acc


def _bias_tiles(rel_bias, idx, head0, n_heads, relative=False):
    r, c = idx.shape
    return pl.pallas_call(
        functools.partial(_bias_tiles_kernel, head0=head0, relative=relative),
        out_shape=jax.ShapeDtypeStruct((n_heads, r, c), F32),
        grid=(n_heads,),
        in_specs=[pl.BlockSpec(memory_space=pltpu.SMEM),
                  pl.BlockSpec((r, c), lambda h: (0, 0))],
        out_specs=pl.BlockSpec((1, r, c), lambda h: (h, 0, 0)),
        compiler_params=_params(1),
        name="bias_tiles",
    )(rel_bias, jnp.asarray(idx))


def _bias_tiles_t(rel_bias, idx_tiles, head0, n_heads):
    r, c = idx_tiles[0].shape
    flat = _bias_tiles(rel_bias, np.concatenate([t.T for t in idx_tiles], axis=1), head0, n_heads,
                       relative=True)
    return flat.reshape(n_heads, c, len(idx_tiles), r).transpose(0, 2, 1, 3)


def _rmsnorm_rows(x, g):
    return x * lax.rsqrt(jnp.mean(x * x, axis=-1, keepdims=True) + RMS_EPS) * g


def _proj_in_kernel(x_ref, g_ref, w_ref, pa0_ref, pa1_ref, pa2_ref, pb_ref, pcmp_ref, pc_ref,
                    mg_ref, bg_ref, regroup_ref):
    xn = _rmsnorm_rows(x_ref[...], g_ref[...]).astype(BF16)
    tm = xn.shape[0]

    def mm(c0, c1):
        return jnp.dot(xn, w_ref[:, c0:c1], preferred_element_type=F32)

    pa0_ref[...] = mm(_OFF_A, _OFF_A + _GRP_COLS).astype(BF16)
    for gi, out_ref in ((1, pa1_ref), (2, pa2_ref)):
        dil = A_GROUPS[gi][1]
        res = mm(_OFF_A + gi * _GRP_COLS, _OFF_A + (gi + 1) * _GRP_COLS)
        for j in range(3):
            regroup_ref[j] = res[:, j * LANE:(j + 1) * LANE]
        for r in range(dil):
            for j in range(3):
                out_ref[0, r, :, j * LANE:(j + 1) * LANE] = (
                    regroup_ref[j, pl.ds(r, tm // dil, stride=dil), :].astype(BF16))
    pb_ref[:, 0:384] = mm(_OFF_PB, _OFF_PB + 384).astype(BF16)
    pb_ref[:, 384:896] = mm(_OFF_PB + 384, _OFF_PB + 896).astype(BF16)
    cmp = mm(_OFF_CMP, _OFF_CMP + _CMP_COLS).astype(BF16)
    for i in range(4):
        pcmp_ref[i] = cmp[:, i * HEAD_DIM:(i + 1) * HEAD_DIM]
    for c in range(0, _C_COLS, 384):
        pc_ref[:, c:c + 384] = mm(_OFF_C + c, _OFF_C + c + 384).astype(BF16)
    for c in range(0, _MG_COLS, 512):
        mg_ref[:, c:c + 512] = mm(_OFF_MG + c, _OFF_MG + c + 512)
    bg_ref[...] = mm(_OFF_BG, _OFF_BG + _BG_COLS)


def _proj_in(x2, g, w, *, layer, seq):
    t = x2.shape[0]
    tm = ROW_TILE
    b = t // seq
    tps = seq // tm
    row = lambda i: (i, 0)
    d1, d2 = A_GROUPS[1][1], A_GROUPS[2][1]
    sub = lambda i: (i // tps, 0, i % tps, 0)
    return pl.pallas_call(
        _proj_in_kernel,
        out_shape=(jax.ShapeDtypeStruct((t, _GRP_COLS), BF16),
                   jax.ShapeDtypeStruct((b, d1, seq // d1, _GRP_COLS), BF16),
                   jax.ShapeDtypeStruct((b, d2, seq // d2, _GRP_COLS), BF16),
                   jax.ShapeDtypeStruct((t, _PB_COLS), BF16),
                   jax.ShapeDtypeStruct((4, t, HEAD_DIM), BF16),
                   jax.ShapeDtypeStruct((t, _C_COLS), BF16),
                   jax.ShapeDtypeStruct((t, _MG_COLS), F32),
                   jax.ShapeDtypeStruct((t, _BG_COLS), F32)),
        grid=(t // tm,),
        in_specs=[pl.BlockSpec((tm, D_MODEL), row),
                  pl.BlockSpec((1, D_MODEL), lambda i: (0, 0)),
                  pl.BlockSpec((None, D_MODEL, _W_COLS), lambda i: (layer, 0, 0))],
        out_specs=(pl.BlockSpec((tm, _GRP_COLS), row),
                   pl.BlockSpec((1, d1, tm // d1, _GRP_COLS), sub),
                   pl.BlockSpec((1, d2, tm // d2, _GRP_COLS), sub),
                   pl.BlockSpec((tm, _PB_COLS), row),
                   pl.BlockSpec((4, tm, HEAD_DIM), lambda i: (0, i, 0)),
                   pl.BlockSpec((tm, _C_COLS), row),
                   pl.BlockSpec((tm, _MG_COLS), row),
                   pl.BlockSpec((tm, _BG_COLS), row)),
        scratch_shapes=[pltpu.VMEM((3, tm, LANE), F32)],
        compiler_params=_params(1),
        name="proj_in",
    )(x2, g, w)


def _w_in_segments():
    a = A_HEADS * HEAD_DIM
    bq0 = 3 * a
    bkc0 = bq0 + B_HEADS * HEAD_DIM
    bks0 = bkc0 + 2 * B_KV_HEADS * HEAD_DIM
    bg0 = bks0 + 4 * B_KV_HEADS * HEAD_DIM
    c0 = bg0 + B_HEADS * 3
    mg0 = c0 + _C_COLS
    gw = A_HEADS_PER_GROUP * HEAD_DIM
    segs = [(_OFF_A + (gi * 3 + j) * gw, j * a + gi * gw, gw)
            for gi in range(len(A_GROUPS)) for j in range(3)]
    segs += [(_OFF_PB, bq0, B_HEADS * HEAD_DIM),
             (_OFF_PB + B_HEADS * HEAD_DIM, bks0, bg0 - bks0),
             (_OFF_CMP, bkc0, _CMP_COLS), (_OFF_C, c0, _C_COLS), (_OFF_MG, mg0, _MG_COLS),
             (_OFF_BG, bg0, B_HEADS * 3)]
    return segs


def _prep_w_in_kernel(w_ref, o_ref):
    o_ref[0, :, _OFF_BG:] = jnp.zeros((o_ref.shape[1], _BG_COLS), BF16)
    for dst, src, width in _w_in_segments():
        for c in range(0, width, 768):
            n = min(768, width - c)
            o_ref[0, :, dst + c:dst + c + n] = w_ref[0, :, src + c:src + c + n].astype(BF16)


def _prep_w_in(w):
    depth, rows, cols = w.shape
    rb = 128
    return pl.pallas_call(
        _prep_w_in_kernel,
        out_shape=jax.ShapeDtypeStruct((depth, rows, _W_COLS), BF16),
        grid=(depth, rows // rb),
        in_specs=[pl.BlockSpec((1, rb, cols), lambda l, r: (l, r, 0))],
        out_specs=pl.BlockSpec((1, rb, _W_COLS), lambda l, r: (l, r, 0)),
        compiler_params=_params(2),
        name="prep_w_in",
    )(w)


_A_UNROLL = 3


def _mixer_a_kernel(pa0_ref, pa1_ref, pa2_ref, bias_ref, o_ref, on_ref, ln_ref, *, seq):
    gw = A_HEADS_PER_GROUP * HEAD_DIM

    def band_block(src, l0, first, gi, rows):
        outs, lses = [], []
        for h in range(A_HEADS_PER_GROUP):
            qc, kc, vc = (slice(j * gw + h * HEAD_DIM, j * gw + (h + 1) * HEAD_DIM) for j in range(3))
            q = src[pl.ds(l0, A_TILE), qc] * ATTN_SCALE
            if first:
                kr = pl.ds(l0, A_TILE)
                bias = bias_ref[A_HEADS_PER_GROUP * gi + h, :, A_TILE:]
            else:
                kr = pl.ds(l0 - A_TILE, 2 * A_TILE)
                bias = bias_ref[A_HEADS_PER_GROUP * gi + h]
            s = _nt(q, src[kr, kc]) + bias
            m = jnp.max(s, axis=-1, keepdims=True)
            e = jnp.exp(s - m)
            l = jnp.sum(e, axis=-1, keepdims=True)
            outs.append(jnp.dot(e.astype(BF16), src[kr, vc], preferred_element_type=F32) / l)
            lses.append(jnp.broadcast_to(m + jnp.log(l), (A_TILE, HEAD_DIM)))
        on_ref[gi, rows, :] = jnp.concatenate(outs, axis=-1)
        ln_ref[gi, rows, :] = jnp.concatenate(lses, axis=-1)

    for gi, ((_, dil), pa_ref) in enumerate(zip(A_GROUPS, (pa0_ref, pa1_ref, pa2_ref))):
        n_blocks = seq // dil // A_TILE
        if dil == 1:
            band_block(pa_ref.at[0], 0, True, gi, pl.ds(0, A_TILE))
            assert (n_blocks - 1) % _A_UNROLL == 0

            def body0(it, c, pa_ref=pa_ref, gi=gi):
                for u in range(_A_UNROLL):
                    l0 = pl.multiple_of((1 + it * _A_UNROLL + u) * A_TILE, A_TILE)
                    band_block(pa_ref.at[0], l0, False, gi, pl.ds(l0, A_TILE))
                return c
            lax.fori_loop(0, (n_blocks - 1) // _A_UNROLL, body0, 0)
        elif n_blocks > 1:
            for r in range(dil):
                band_block(pa_ref.at[0, r], 0, True, gi, pl.ds(r, A_TILE, stride=dil))

            def body1(i, c, pa_ref=pa_ref, gi=gi, dil=dil):
                l0 = pl.multiple_of(i * A_TILE, A_TILE)
                for r in range(dil):
                    band_block(pa_ref.at[0, r], l0, False, gi, pl.ds(l0 * dil + r, A_TILE, stride=dil))
                return c
            lax.fori_loop(1, n_blocks, body1, 0)
        else:
            per_iter = 4

            def body2(it, c, pa_ref=pa_ref, gi=gi, dil=dil):
                for u in range(per_iter):
                    r = it * per_iter + u
                    band_block(pa_ref.at[0, r], 0, True, gi, pl.ds(r, A_TILE, stride=dil))
                return c
            lax.fori_loop(0, dil // per_iter, body2, 0)

    chunk = T_TILE
    for c0 in range(0, seq, chunk):
        rows = slice(c0, c0 + chunk)
        lse = [ln_ref[g, rows, :] for g in range(len(A_GROUPS))]
        top = functools.reduce(jnp.maximum, lse)
        ex = [jnp.exp(v - top) for v in lse]
        den = functools.reduce(lambda a, b: a + b, ex)
        for g in range(len(A_GROUPS)):
            o_ref[0, rows, g * gw:(g + 1) * gw] = (ex[g] / den * on_ref[g, rows, :]).astype(BF16)


def _mixer_a(pa0, pa1, pa2, bias):
    b, seq, _ = pa0.shape
    return pl.pallas_call(
        functools.partial(_mixer_a_kernel, seq=seq),
        out_shape=jax.ShapeDtypeStruct((b, seq, _GRP_COLS), BF16),
        grid=(b,),
        in_specs=[pl.BlockSpec((1, seq, _GRP_COLS), lambda i: (i, 0, 0)),
                  pl.BlockSpec((1,) + pa1.shape[1:], lambda i: (i, 0, 0, 0)),
                  pl.BlockSpec((1,) + pa2.shape[1:], lambda i: (i, 0, 0, 0)),
                  pl.BlockSpec(bias.shape, lambda i: (0, 0, 0))],
        out_specs=pl.BlockSpec((1, seq, _GRP_COLS), lambda i: (i, 0, 0)),
        scratch_shapes=[pltpu.VMEM((len(A_GROUPS), seq, LANE), F32),
                        pltpu.VMEM((len(A_GROUPS), seq, LANE), F32)],
        compiler_params=_params(1),
        name="mixer_a",
    )(pa0, pa1, pa2, bias)


def _compress_kernel(r_ref, pek_ref, pev_ref, w1k_ref, w1v_ref, w2k_ref, w2v_ref, o_ref):
    half = NSA_CMP_STRIDE * HEAD_DIM
    for kv, (pe_ref, w1_ref, w2_ref) in enumerate(((pek_ref, w1k_ref, w2k_ref),
                                                   (pev_ref, w1v_ref, w2v_ref))):
        for h in range(B_KV_HEADS):
            r = r_ref[kv * B_KV_HEADS + h, 0].astype(F32)
            lo = jnp.dot((r + pe_ref[:, :half]).astype(BF16), w1_ref[:half, :],
                         preferred_element_type=F32)
            hi = jnp.dot((r + pe_ref[:, half:]).astype(BF16), w1_ref[half:, :],
                         preferred_element_type=F32)
            hid = lo + pltpu.roll(hi, hi.shape[0] - 1, axis=0)
            act = jax.nn.gelu(hid).astype(BF16)
            o_ref[0, kv * B_KV_HEADS + h] = jnp.dot(act, w2_ref[...],
                                                    preferred_element_type=F32).astype(BF16)


def _compress(r, pe_k, pe_v, w1_k, w1_v, w2_k, w2_v):
    _, b, m, c = r.shape
    const2 = lambda i: (0, 0)
    return pl.pallas_call(
        _compress_kernel,
        out_shape=jax.ShapeDtypeStruct((b, 4, m, HEAD_DIM), BF16),
        grid=(b,),
        in_specs=[pl.BlockSpec((4, 1, m, c), lambda i: (0, i, 0, 0)),
                  pl.BlockSpec(pe_k.shape, const2), pl.BlockSpec(pe_v.shape, const2),
                  pl.BlockSpec(w1_k.shape, const2), pl.BlockSpec(w1_v.shape, const2),
                  pl.BlockSpec(w2_k.shape, const2), pl.BlockSpec(w2_v.shape, const2)],
        out_specs=pl.BlockSpec((1, 4, m, HEAD_DIM), lambda i: (i, 0, 0, 0)),
        compiler_params=_params(1),
        name="nsa_compress",
    )(r, pe_k, pe_v, w1_k, w1_v, w2_k, w2_v)


MASK_BIG = 2.0 ** 100
SEL_PAD = 32


def _nt(a, b, **kw):
    return lax.dot_general(a, b, (((1,), (1,)), ((), ())), preferred_element_type=F32, **kw)


def _top_mask_t(score, n_rows, n_top):
    sub = lax.broadcasted_iota(jnp.int32, score.shape, 0)
    rank = jnp.zeros(score.shape, F32)
    for j in range(n_rows):
        row = score[j:j + 1, :]
        ge = jnp.where(row >= score, 1.0, 0.0)
        gt = jnp.where(row > score, 1.0, 0.0)
        rank = rank + jnp.where(sub > j, ge, gt)
    return rank < n_top


BAND_SLOTS = 4


def _loop_pairs(lo, hi, body):
    n = jnp.maximum(hi - lo, 0)

    def pair(p, c):
        body(lo + 2 * p)
        body(lo + 2 * p + 1)
        return c
    lax.fori_loop(0, n // 2, pair, 0)
    pl.when(n % 2 == 1)(lambda: body(hi - 1))


def _two_pass_attention(i, streams, band_streams=()):
    groups = ((streams, lambda j: j), (band_streams, lambda j: j % BAND_SLOTS))

    def key_rows(j):
        return pl.ds(pl.multiple_of(j * T_TILE, T_TILE), T_TILE)

    def logits(group, j, d):
        some, slot = group
        for k_tile, q_t, _, bias_tile, s_ref, rmp_ref, _ in some:
            s = jnp.dot(k_tile(key_rows(j)), q_t(), preferred_element_type=F32)
            if d is not None:
                s = s + bias_tile(d)
            s_ref[slot(j)] = s
            fold = jnp.max(s.reshape(s.shape[0] // 8, 8, s.shape[1]), axis=0)
            rmp_ref[...] = fold if d == 0 else jnp.maximum(rmp_ref[...], fold)

    def near(n_full, n_band):
        for d in range(n_full):
            logits(groups[0], i - d, d)
        for d in range(n_band):
            logits(groups[1], i - d, d)

    if band_streams:
        pl.when(i == 0)(lambda: near(1, 1))
        pl.when(i == 1)(lambda: near(2, 2))
        pl.when(i >= 2)(lambda: near(2, 3))
    else:
        logits(groups[0], i, 0)
        pl.when(i >= 1)(lambda: logits(groups[0], i - 1, 1))

    _loop_pairs(0, i - 1, lambda j: logits(groups[0], j, None))

    tops = {}
    for some, _ in groups:
        for stream in some:
            rmp_ref, acc_ref = stream[-2:]
            tops[id(rmp_ref)] = jnp.max(rmp_ref[...], axis=0, keepdims=True)
            acc_ref[...] = jnp.zeros(acc_ref.shape, F32)

    def pv(active):
        def body(j):
            for some, slot in active:
                for _, _, vt_ref, _, s_ref, rmp_ref, acc_ref in some:
                    e = jnp.exp(s_ref[slot(j)] - tops[id(rmp_ref)])
                    acc_ref[...] += jnp.dot(vt_ref[j], e.astype(BF16), preferred_element_type=F32)
        return body

    if band_streams:
        band_lo = jnp.maximum(i - 2, 0)
        _loop_pairs(0, band_lo, pv(groups[:1]))
        pv_all = pv(groups)

        def merged(j, c):
            pv_all(j)
            return c
        lax.fori_loop(band_lo, i + 1, merged, 0)
    else:
        _loop_pairs(0, i + 1, pv(groups[:1]))

    outs = []
    for some, _ in groups:
        for *_, acc_ref in some:
            acc = acc_ref[...]
            outs.append(acc[:HEAD_DIM, :] / acc[HEAD_DIM:HEAD_DIM + 1, :])
    return outs


def _eye(n, dtype=BF16):
    return (lax.broadcasted_iota(jnp.int32, (n, n), 0)
            == lax.broadcasted_iota(jnp.int32, (n, n), 1)).astype(dtype)


def _transposed_values(v_ref_rows, n_tiles):
    eye = _eye(LANE)
    for j in range(n_tiles):
        v = _with_ones_column(v_ref_rows(slice(j * T_TILE, (j + 1) * T_TILE)))
        yield j, _nt(eye, v).astype(BF16)


def _with_ones_column(v):
    one = (lax.broadcasted_iota(jnp.int32, v.shape, 1) == 0).astype(F32)
    return jnp.concatenate([v.astype(F32), one], axis=1).astype(BF16)


def _with_block_onehot(k, block):
    n = k.shape[0]
    shape = (n, SEL_PAD)
    hot = (lax.broadcasted_iota(jnp.int32, shape, 0) // block
           == lax.broadcasted_iota(jnp.int32, shape, 1)).astype(F32)
    return jnp.concatenate([k.astype(F32), hot, jnp.zeros(shape, F32)], axis=1).astype(BF16)


def _augment_q_t(q_t, allowed_t):
    pen = (allowed_t - 1.0) * MASK_BIG
    return jnp.concatenate([q_t, pen, jnp.zeros(pen.shape, F32)], axis=0).astype(BF16)


_N_SLC = 32
_G_ROWS = B_GROUP * T_TILE


def _nsa_kernel(pb_ref, cmp_ref, bg_ref, bias_ref, o_ref,
                kaug_ref, vst_ref, vwt_ref, eye_ref, q3t_ref, qaugt_ref, s_ref, sw_ref, rmp_ref,
                acc_ref, ocmp_ref, *, seq):
    n_tiles = seq // T_TILE
    n_cmp_rows = seq // NSA_CMP_STRIDE
    q_cols = B_HEADS * HEAD_DIM
    ks0, vs0, kw0, vw0 = (q_cols + i * B_KV_HEADS * HEAD_DIM for i in range(4))

    def head_cols(c0, h):
        return slice(c0 + h * HEAD_DIM, c0 + (h + 1) * HEAD_DIM)

    j_id = lax.broadcasted_iota(jnp.int32, (_N_SLC, n_cmp_rows), 0)
    c_id = lax.broadcasted_iota(jnp.int32, (_N_SLC, n_cmp_rows), 1)
    overlap_t = ((c_id * NSA_CMP_STRIDE < (j_id + 1) * NSA_SLC_BLOCK)
                 & (c_id * NSA_CMP_STRIDE + NSA_CMP_BLOCK > j_id * NSA_SLC_BLOCK)).astype(F32)
    eye_ref[...] = _eye(T_TILE)
    for h in range(B_KV_HEADS):
        kaug_ref[h] = _with_block_onehot(pb_ref[0, :, head_cols(ks0, h)], NSA_SLC_BLOCK)
        for j, vt in _transposed_values(lambda r, h=h: pb_ref[0, r, head_cols(vs0, h)], n_tiles):
            vst_ref[h, j] = vt
        for j, vt in _transposed_values(lambda r, h=h: pb_ref[0, r, head_cols(vw0, h)], n_tiles):
            vwt_ref[h, j] = vt

    def tile_body(i, carry):
        r0 = pl.multiple_of(i * T_TILE, T_TILE)
        rows = pl.ds(r0, T_TILE)
        pick = (lax.broadcasted_iota(jnp.int32, (SEL_PAD, _BG_COLS), 0)
                == lax.broadcasted_iota(jnp.int32, (SEL_PAD, _BG_COLS), 1)).astype(F32)
        sig_t = jax.nn.sigmoid(_nt(pick, bg_ref[0, rows, :], precision=lax.Precision.HIGHEST))
        eye_h = _eye(HEAD_DIM)
        c_end = (lax.broadcasted_iota(jnp.int32, (n_cmp_rows, _G_ROWS), 0) * NSA_CMP_STRIDE
                 + NSA_CMP_BLOCK - 1)
        t_pos = r0 + lax.broadcasted_iota(jnp.int32, (n_cmp_rows, _G_ROWS), 1) % T_TILE
        ok = (c_end <= t_pos) & (c_end < seq)
        ok_f = ok.astype(F32)

        for h in range(B_KV_HEADS):
            q3t = jnp.concatenate(
                [_nt(eye_h, pb_ref[0, rows, head_cols(0, h * B_GROUP + g)]) for g in range(B_GROUP)],
                axis=1) * ATTN_SCALE
            q3t_bf = q3t.astype(BF16)
            q3t_ref[h] = q3t_bf

            kc = cmp_ref[0, h]
            vct = _nt(eye_h, cmp_ref[0, B_KV_HEADS + h]).astype(BF16)
            lc = jnp.where(ok, jnp.dot(kc, q3t_bf, preferred_element_type=F32), NEG_INF)
            ec = jnp.exp(lc - jnp.max(lc, axis=0, keepdims=True))
            pc = ec / jnp.sum(ec, axis=0, keepdims=True) * ok_f
            ocmp_ref[h] = jnp.dot(vct, pc.astype(BF16), preferred_element_type=F32)
            p_sum = pc[:, 0:T_TILE] + pc[:, T_TILE:2 * T_TILE] + pc[:, 2 * T_TILE:3 * T_TILE]
            imp_t = jnp.dot(overlap_t, p_sum, precision=lax.Precision.HIGHEST,
                            preferred_element_type=F32)

            j_sub = lax.broadcasted_iota(jnp.int32, (_N_SLC, T_TILE), 0)
            jt = (r0 + lax.broadcasted_iota(jnp.int32, (_N_SLC, T_TILE), 1)) // NSA_SLC_BLOCK
            forced = (j_sub == 0) | (j_sub == jt) | (j_sub == jt - 1)
            valid = j_sub <= jt
            score = jnp.where(valid, imp_t + FORCE_BONUS * forced.astype(F32), NEG_INF)
            sel_t = jnp.where(valid & _top_mask_t(score, _N_SLC, NSA_SLC_TOPK), 1.0, 0.0)
            qaugt_ref[h] = _augment_q_t(q3t, jnp.concatenate([sel_t] * B_GROUP, axis=1))

        outs = _two_pass_attention(
            i,
            [(lambda kr, h=h: kaug_ref[h, kr, :], lambda h=h: qaugt_ref[h], vst_ref.at[h],
              lambda d, h=h: bias_ref[h, d], s_ref.at[h], rmp_ref.at[0, h], acc_ref.at[0, h])
             for h in range(B_KV_HEADS)],
            [(lambda kr, h=h: pb_ref[0, kr, head_cols(kw0, h)], lambda h=h: q3t_ref[h],
              vwt_ref.at[h], lambda d, h=h: bias_ref[h, d], sw_ref.at[h], rmp_ref.at[1, h],
              acc_ref.at[1, h]) for h in range(B_KV_HEADS)])
        oslc, owin = outs[:B_KV_HEADS], outs[B_KV_HEADS:]

        heads = []
        for h in range(B_KV_HEADS):
            for g in range(B_GROUP):
                head = h * B_GROUP + g
                gc = slice(g * T_TILE, (g + 1) * T_TILE)
                heads.append(sig_t[3 * head:3 * head + 1, :] * ocmp_ref[h, :, gc]
                             + sig_t[3 * head + 1:3 * head + 2, :] * oslc[h][:, gc]
                             + sig_t[3 * head + 2:3 * head + 3, :] * owin[h][:, gc])
        o_t = jnp.concatenate(heads, axis=0).astype(BF16)
        o_ref[0, rows, :] = _nt(eye_ref[...], o_t).astype(o_ref.dtype)
        return carry

    lax.fori_loop(0, n_tiles, tile_body, 0)


def _nsa(pb, cmp, bg, bias):
    b, seq, _ = pb.shape
    n_tiles = seq // T_TILE
    return pl.pallas_call(
        functools.partial(_nsa_kernel, seq=seq),
        out_shape=jax.ShapeDtypeStruct((b, seq, B_HEADS * HEAD_DIM), BF16),
        grid=(b,),
        in_specs=[pl.BlockSpec((1, seq, _PB_COLS), lambda i: (i, 0, 0)),
                  pl.BlockSpec((1,) + cmp.shape[1:], lambda i: (i, 0, 0, 0)),
                  pl.BlockSpec((1, seq, _BG_COLS), lambda i: (i, 0, 0)),
                  pl.BlockSpec(bias.shape, lambda i: (0, 0, 0, 0))],
        out_specs=pl.BlockSpec((1, seq, B_HEADS * HEAD_DIM), lambda i: (i, 0, 0)),
        scratch_shapes=[pltpu.VMEM((B_KV_HEADS, seq, LANE), BF16),
                        pltpu.VMEM((B_KV_HEADS, n_tiles, LANE, T_TILE), BF16),
                        pltpu.VMEM((B_KV_HEADS, n_tiles, LANE, T_TILE), BF16),
                        pltpu.VMEM((T_TILE, T_TILE), BF16),
                        pltpu.VMEM((B_KV_HEADS, HEAD_DIM, _G_ROWS), BF16),
                        pltpu.VMEM((B_KV_HEADS, LANE, _G_ROWS), BF16),
                        pltpu.VMEM((B_KV_HEADS, n_tiles, T_TILE, _G_ROWS), F32),
                        pltpu.VMEM((B_KV_HEADS, BAND_SLOTS, T_TILE, _G_ROWS), F32),
                        pltpu.VMEM((2, B_KV_HEADS, 8, _G_ROWS), F32),
                        pltpu.VMEM((2, B_KV_HEADS, LANE, _G_ROWS), F32),
                        pltpu.VMEM((B_KV_HEADS, HEAD_DIM, _G_ROWS), F32)],
        compiler_params=_params(1),
        name="nsa",
    )(pb, cmp, bg, bias)


def _moba_kernel(pc_ref, bias_ref, o_ref, kaug_ref, vt_ref, kmean_ref, eye_ref, qaugt_ref, s_ref,
                 rmp_ref, acc_ref, *, seq):
    n_blk = seq // MOBA_BLOCK
    width = C_HEADS * HEAD_DIM
    eye_ref[...] = _eye(T_TILE)

    def cols(j, h):
        return slice(j * width + h * HEAD_DIM, j * width + (h + 1) * HEAD_DIM)

    for h in range(C_HEADS):
        kaug_ref[h] = _with_block_onehot(pc_ref[0, :, cols(1, h)], MOBA_BLOCK)
        for j, vt in _transposed_values(lambda r, h=h: pc_ref[0, r, cols(2, h)], n_blk):
            vt_ref[h, j] = vt
        kmean_ref[h] = jnp.concatenate(
            [jnp.mean(pc_ref[0, n * MOBA_BLOCK:(n + 1) * MOBA_BLOCK, cols(1, h)].astype(F32), axis=0,
                      keepdims=True) for n in range(n_blk)]
            + [jnp.zeros((SEL_PAD - n_blk, HEAD_DIM), F32)], axis=0)

    def tile_body(i, carry):
        rows = pl.ds(pl.multiple_of(i * MOBA_BLOCK, MOBA_BLOCK), MOBA_BLOCK)
        n_sub = lax.broadcasted_iota(jnp.int32, (SEL_PAD, MOBA_BLOCK), 0)
        past = n_sub < i
        eye_h = _eye(HEAD_DIM)
        for h in range(C_HEADS):
            q_t = _nt(eye_h, pc_ref[0, rows, cols(0, h)])
            gate_t = jnp.dot(kmean_ref[h], q_t, precision=lax.Precision.HIGHEST,
                             preferred_element_type=F32)
            score = jnp.where(past, gate_t, NEG_INF)
            picked = past & _top_mask_t(score, n_blk, MOBA_TOPK)
            allowed_t = jnp.where(picked | (n_sub == i), 1.0, 0.0)
            qaugt_ref[h] = _augment_q_t(q_t * ATTN_SCALE, allowed_t)

        outs = _two_pass_attention(
            i, [(lambda kr, h=h: kaug_ref[h, kr, :], lambda h=h: qaugt_ref[h], vt_ref.at[h],
                 lambda d, h=h: bias_ref[h, d], s_ref.at[h], rmp_ref.at[h], acc_ref.at[h])
                for h in range(C_HEADS)])
        o_t = jnp.concatenate(outs, axis=0).astype(BF16)
        o_ref[0, rows, :] = _nt(eye_ref[...], o_t).astype(o_ref.dtype)
        return carry

    lax.fori_loop(0, n_blk, tile_body, 0)


def _moba(pc, bias):
    b, seq, _ = pc.shape
    width = C_HEADS * HEAD_DIM
    n_tiles = seq // T_TILE
    return pl.pallas_call(
        functools.partial(_moba_kernel, seq=seq),
        out_shape=jax.ShapeDtypeStruct((b, seq, width), BF16),
        grid=(b,),
        in_specs=[pl.BlockSpec((1, seq, _C_COLS), lambda i: (i, 0, 0)),
                  pl.BlockSpec(bias.shape, lambda i: (0, 0, 0, 0))],
        out_specs=pl.BlockSpec((1, seq, width), lambda i: (i, 0, 0)),
        scratch_shapes=[pltpu.VMEM((C_HEADS, seq, LANE), BF16),
                        pltpu.VMEM((C_HEADS, n_tiles, LANE, T_TILE), BF16),
                        pltpu.VMEM((C_HEADS, SEL_PAD, HEAD_DIM), F32),
                        pltpu.VMEM((T_TILE, T_TILE), BF16),
                        pltpu.VMEM((C_HEADS, LANE, T_TILE), BF16),
                        pltpu.VMEM((C_HEADS, n_tiles, T_TILE, T_TILE), F32),
                        pltpu.VMEM((C_HEADS, 8, T_TILE), F32),
                        pltpu.VMEM((C_HEADS, LANE, T_TILE), F32)],
        compiler_params=_params(1),
        name="moba",
    )(pc, bias)


def _merge_kernel(oa_ref, ob_ref, oc_ref, mg_ref, x_ref, wb_ref, wo_ref, y_ref):
    r0 = A_HEADS * HEAD_DIM
    r1 = r0 + B_HEADS * HEAD_DIM
    ya = jnp.dot(oa_ref[...], wb_ref[0:r0, :], preferred_element_type=F32)
    yb = jnp.dot(ob_ref[...], wb_ref[r0:r1, :], preferred_element_type=F32)
    yc = jnp.dot(oc_ref[...], wb_ref[r1:, :], preferred_element_type=F32)
    merged = (jax.nn.sigmoid(mg_ref[:, 0:D_MODEL]) * ya
              + jax.nn.sigmoid(mg_ref[:, D_MODEL:2 * D_MODEL]) * yb
              + jax.nn.sigmoid(mg_ref[:, 2 * D_MODEL:]) * yc)
    y_ref[...] = x_ref[...] + jnp.dot(merged.astype(BF16), wo_ref[...], preferred_element_type=F32)


def _merge(oa, ob, oc, mg, x2, wb, wo):
    t = x2.shape[0]
    tm = ROW_TILE
    row = lambda i: (i, 0)
    return pl.pallas_call(
        _merge_kernel,
        out_shape=jax.ShapeDtypeStruct((t, D_MODEL), F32),
        grid=(t // tm,),
        in_specs=[pl.BlockSpec((tm, oa.shape[1]), row),
                  pl.BlockSpec((tm, ob.shape[1]), row), pl.BlockSpec((tm, oc.shape[1]), row),
                  pl.BlockSpec((tm, _MG_COLS), row), pl.BlockSpec((tm, D_MODEL), row),
                  pl.BlockSpec(wb.shape, lambda i: (0, 0)), pl.BlockSpec(wo.shape, lambda i: (0, 0))],
        out_specs=pl.BlockSpec((tm, D_MODEL), row),
        compiler_params=_params(1),
        name="merge",
    )(oa, ob, oc, mg, x2, wb, wo)


_FF_CHUNK = D_FF // 2


def _ffn_kernel(x_ref, halo_ref, g_ref, wu_ref, cw_ref, cb_ref, wd_ref, gf_ref, y_ref, xn_ref,
                *, tiles_per_seq, final_norm):
    i = pl.program_id(0)
    x = x_ref[...]
    g = g_ref[...]
    keep = (i % tiles_per_seq != 0).astype(F32)
    xn_ref[0:FFN_HALO, :] = (_rmsnorm_rows(halo_ref[...], g) * keep).astype(BF16)
    xn_ref[FFN_HALO:, :] = _rmsnorm_rows(x, g).astype(BF16)
    xn = xn_ref[...]
    rows = xn.shape[0]

    def conv(c0):
        hcol = jnp.dot(xn, wu_ref[:, c0:c0 + _FF_CHUNK], preferred_element_type=F32)
        out = (cw_ref[2:3, c0:c0 + _FF_CHUNK] * hcol
               + cw_ref[1:2, c0:c0 + _FF_CHUNK] * pltpu.roll(hcol, 1, axis=0)
               + cw_ref[0:1, c0:c0 + _FF_CHUNK] * pltpu.roll(hcol, 2, axis=0)
               + cb_ref[:, c0:c0 + _FF_CHUNK])
        return out[FFN_HALO:rows]

    acc = x
    for c in range(0, D_FF, _FF_CHUNK):
        a = conv(c)
        u = conv(D_FF + c)
        act = (a * jax.nn.sigmoid(a) * u).astype(BF16)
        acc = acc + jnp.dot(act, wd_ref[c:c + _FF_CHUNK, :], preferred_element_type=F32)
    if final_norm:
        acc = _rmsnorm_rows(acc, gf_ref[...])
    y_ref[...] = acc


def _ffn(x2, g, wu, cw, cb, wd, gf, *, seq, final_norm):
    t = x2.shape[0]
    tm = ROW_TILE
    const = lambda i: (0, 0)
    halo_blocks = tm // FFN_HALO
    return pl.pallas_call(
        functools.partial(_ffn_kernel, tiles_per_seq=seq // tm, final_norm=final_norm),
        out_shape=jax.ShapeDtypeStruct((t, D_MODEL), F32),
        grid=(t // tm,),
        in_specs=[pl.BlockSpec((tm, D_MODEL), lambda i: (i, 0)),
                  pl.BlockSpec((FFN_HALO, D_MODEL), lambda i: (jnp.maximum(i * halo_blocks - 1, 0), 0)),
                  pl.BlockSpec((1, D_MODEL), const),
                  pl.BlockSpec(wu.shape, const), pl.BlockSpec(cw.shape, const),
                  pl.BlockSpec(cb.shape, const), pl.BlockSpec(wd.shape, const),
                  pl.BlockSpec((1, D_MODEL), const)],
        out_specs=pl.BlockSpec((tm, D_MODEL), lambda i: (i, 0)),
        scratch_shapes=[pltpu.VMEM((tm + FFN_HALO, D_MODEL), BF16)],
        compiler_params=_params(1),
        name="conv_ffn",
    )(x2, x2, g, wu, cw, cb, wd, gf)


def kernel(x, rel_bias, norm_mix, w_in, cmp_pe_k, cmp_w1_k, cmp_w2_k, cmp_pe_v, cmp_w1_v, cmp_w2_v,
           w_branch, w_out, norm_ffn, w_up, conv_w, conv_b, w_down, norm_final):
    b, s, d = x.shape
    depth = w_in.shape[0]
    t = b * s
    assert d == D_MODEL and s % T_TILE == 0 and s // NSA_SLC_BLOCK == _N_SLC and t % ROW_TILE == 0
    assert s % ROW_TILE == 0 and all(s % (dil * A_TILE) == 0 for _, dil in A_GROUPS)
    assert all(win // dil == A_TILE for win, dil in A_GROUPS) and A_GROUPS[0][1] == 1
    assert all(ROW_TILE % dil == 0 for _, dil in A_GROUPS)

    hg = A_HEADS_PER_GROUP
    bias_a = jnp.concatenate([
        _bias_tiles(rel_bias,
                    np.concatenate([_bucket_tile(A_TILE, 1, dil, win // dil),
                                    _bucket_tile(A_TILE, 0, dil, win // dil)], axis=1),
                    gi * hg, hg)
        for gi, (win, dil) in enumerate(A_GROUPS)], axis=0)
    assert (_bucket_tile(T_TILE, 2, 1, 3 * T_TILE) == REL_BUCKETS - 1).all()
    bias_b = _bias_tiles_t(rel_bias, [_bucket_tile(T_TILE, d, 1, NSA_WINDOW - 1) for d in (0, 1, 2)],
                           A_HEADS, B_HEADS)
    bias_b = bias_b.reshape(B_KV_HEADS, B_GROUP, 3, T_TILE, T_TILE).transpose(0, 2, 3, 1, 4).reshape(
        B_KV_HEADS, 3, T_TILE, _G_ROWS)
    bias_c = _bias_tiles_t(rel_bias, [_bucket_tile(T_TILE, d, 1, s) for d in (0, 1)],
                           A_HEADS + B_HEADS, C_HEADS)

    w_in_k = _prep_w_in(w_in)
    x2 = x.reshape(t, d)
    for i in range(depth):
        pa0, pa1, pa2, pb, pcmp, pc, mg, bg = _proj_in(x2, norm_mix[i][None, :], w_in_k, layer=i,
                                                       seq=s)

        oa = _mixer_a(pa0.reshape(b, s, _GRP_COLS), pa1, pa2, bias_a)

        cmp = _compress(pcmp.reshape(4, b, s // NSA_CMP_STRIDE, NSA_CMP_STRIDE * HEAD_DIM),
                        cmp_pe_k[i].reshape(1, -1), cmp_pe_v[i].reshape(1, -1),
                        cmp_w1_k[i].astype(BF16), cmp_w1_v[i].astype(BF16),
                        cmp_w2_k[i].astype(BF16), cmp_w2_v[i].astype(BF16))
        ob = _nsa(pb.reshape(b, s, _PB_COLS), cmp, bg.reshape(b, s, _BG_COLS), bias_b)

        oc = _moba(pc.reshape(b, s, _C_COLS), bias_c)

        x2 = _merge(oa.reshape(t, -1), ob.reshape(t, -1), oc.reshape(t, -1), mg, x2,
                    w_branch[i].astype(BF16), w_out[i].astype(BF16))
        x2 = _ffn(x2, norm_ffn[i][None, :], w_up[i].astype(BF16), conv_w[i], conv_b[i][None, :],
                  w_down[i].astype(BF16), norm_final[None, :], seq=s, final_norm=(i == depth - 1))
    return x2.reshape(b, s, d)
```

```python
import functools
import math

import jax
import jax.numpy as jnp
import numpy as np
from jax import lax
from jax.experimental import pallas as pl
from jax.experimental.pallas import tpu as pltpu

F32 = jnp.float32
BF16 = jnp.bfloat16

D_MODEL = 1024
HEAD_DIM = 64
A_GROUPS = ((128, 1), (512, 4), (2048, 16))
A_HEADS_PER_GROUP = 2
A_HEADS = 6
B_HEADS = 6
B_KV_HEADS = 2
B_GROUP = 3
C_HEADS = 4
NSA_CMP_BLOCK = 32
NSA_CMP_STRIDE = 16
NSA_CMP_HIDDEN = 256
NSA_SLC_BLOCK = 64
NSA_SLC_TOPK = 16
NSA_WINDOW = 512
MOBA_BLOCK = 256
MOBA_TOPK = 3
REL_BUCKETS = 32
REL_MAX_DIST = 128
D_FF = 2816
RMS_EPS = 1e-6
NEG_INF = -1e30
FORCE_BONUS = 1e4
ATTN_SCALE = HEAD_DIM ** -0.5

LANE = 128
A_TILE = 128
T_TILE = 256
ROW_TILE = 512
FFN_HALO = 8
VMEM_LIMIT = 56 * 1024 * 1024

_GRP_COLS = 3 * A_HEADS_PER_GROUP * HEAD_DIM
_A_COLS = len(A_GROUPS) * _GRP_COLS
_PB_COLS = B_HEADS * HEAD_DIM + 4 * B_KV_HEADS * HEAD_DIM
_CMP_COLS = 2 * B_KV_HEADS * HEAD_DIM
_C_COLS = 3 * C_HEADS * HEAD_DIM
_MG_COLS = 3 * D_MODEL
_BG_COLS = LANE
_OFF_A = 0
_OFF_PB = _OFF_A + _A_COLS
_OFF_CMP = _OFF_PB + _PB_COLS
_OFF_C = _OFF_CMP + _CMP_COLS
_OFF_MG = _OFF_C + _C_COLS
_OFF_BG = _OFF_MG + _MG_COLS
_W_COLS = _OFF_BG + _BG_COLS


def _params(n_grid):
    return pltpu.CompilerParams(dimension_semantics=("arbitrary",) * n_grid,
                                vmem_limit_bytes=VMEM_LIMIT)


def _rel_bucket_np(dist):
    n = np.maximum(dist, 0)
    exact = REL_BUCKETS // 2
    nf = np.maximum(n, 1).astype(np.float32)
    large = exact + (np.log(nf / np.float32(exact)) / np.float32(math.log(REL_MAX_DIST / exact))
                     * np.float32(REL_BUCKETS - exact)).astype(np.int32)
    return np.where(n < exact, n, np.minimum(large, REL_BUCKETS - 1)).astype(np.int32)


def _bucket_tile(tile, block_offset, dil, max_rel):
    rel = block_offset * tile + np.arange(tile)[:, None] - np.arange(tile)[None, :]
    ok = (rel >= 0) & (rel <= max_rel)
    return np.where(ok, _rel_bucket_np(rel * dil), -1).astype(np.int32)


def _bias_tiles_kernel(tbl_ref, idx_ref, o_ref, *, head0, relative):
    h = pl.program_id(0) + head0
    idx = idx_ref[...]
    acc = jnp.full(idx.shape, NEG_INF, F32)
    for b in range(REL_BUCKETS):
        acc = jnp.where(idx == b, tbl_ref[b, h], acc)
    if relative:
        acc = acc - tbl_ref[REL_BUCKETS - 1, h]
    o_ref[0] = acc


def _bias_tiles(rel_bias, idx, head0, n_heads, relative=False):
    r, c = idx.shape
    return pl.pallas_call(
        functools.partial(_bias_tiles_kernel, head0=head0, relative=relative),
        out_shape=jax.ShapeDtypeStruct((n_heads, r, c), F32),
        grid=(n_heads,),
        in_specs=[pl.BlockSpec(memory_space=pltpu.SMEM),
                  pl.BlockSpec((r, c), lambda h: (0, 0))],
        out_specs=pl.BlockSpec((1, r, c), lambda h: (h, 0, 0)),
        compiler_params=_params(1),
        name="bias_tiles",
    )(rel_bias, jnp.asarray(idx))


def _bias_tiles_t(rel_bias, idx_tiles, head0, n_heads):
    r, c = idx_tiles[0].shape
    flat = _bias_tiles(rel_bias, np.concatenate([t.T for t in idx_tiles], axis=1), head0, n_heads,
                       relative=True)
    return flat.reshape(n_heads, c, len(idx_tiles), r).transpose(0, 2, 1, 3)


def _rmsnorm_rows(x, g):
    return x * lax.rsqrt(jnp.mean(x * x, axis=-1, keepdims=True) + RMS_EPS) * g


def _proj_in_kernel(x_ref, g_ref, w_ref, pa0_ref, pa1_ref, pa2_ref, pb_ref, pcmp_ref, pc_ref,
                    mg_ref, bg_ref, regroup_ref):
    xn = _rmsnorm_rows(x_ref[...], g_ref[...]).astype(BF16)
    tm = xn.shape[0]

    def mm(c0, c1):
        return jnp.dot(xn, w_ref[:, c0:c1], preferred_element_type=F32)

    pa0_ref[...] = mm(_OFF_A, _OFF_A + _GRP_COLS).astype(BF16)
    for gi, out_ref in ((1, pa1_ref), (2, pa2_ref)):
        dil = A_GROUPS[gi][1]
        res = mm(_OFF_A + gi * _GRP_COLS, _OFF_A + (gi + 1) * _GRP_COLS)
        for j in range(3):
            regroup_ref[j] = res[:, j * LANE:(j + 1) * LANE]
        for r in range(dil):
            for j in range(3):
                out_ref[0, r, :, j * LANE:(j + 1) * LANE] = (
                    regroup_ref[j, pl.ds(r, tm // dil, stride=dil), :].astype(BF16))
    pb_ref[:, 0:384] = mm(_OFF_PB, _OFF_PB + 384).astype(BF16)
    pb_ref[:, 384:896] = mm(_OFF_PB + 384, _OFF_PB + 896).astype(BF16)
    cmp = mm(_OFF_CMP, _OFF_CMP + _CMP_COLS).astype(BF16)
    for i in range(4):
        pcmp_ref[i] = cmp[:, i * HEAD_DIM:(i + 1) * HEAD_DIM]
    for c in range(0, _C_COLS, 384):
        pc_ref[:, c:c + 384] = mm(_OFF_C + c, _OFF_C + c + 384).astype(BF16)
    for c in range(0, _MG_COLS, 512):
        mg_ref[:, c:c + 512] = mm(_OFF_MG + c, _OFF_MG + c + 512)
    bg_ref[...] = mm(_OFF_BG, _OFF_BG + _BG_COLS)


def _proj_in(x2, g, w, *, layer, seq):
    t = x2.shape[0]
    tm = ROW_TILE
    b = t // seq
    tps = seq // tm
    row = lambda i: (i, 0)
    d1, d2 = A_GROUPS[1][1], A_GROUPS[2][1]
    sub = lambda i: (i // tps, 0, i % tps, 0)
    return pl.pallas_call(
        _proj_in_kernel,
        out_shape=(jax.ShapeDtypeStruct((t, _GRP_COLS), BF16),
                   jax.ShapeDtypeStruct((b, d1, seq // d1, _GRP_COLS), BF16),
                   jax.ShapeDtypeStruct((b, d2, seq // d2, _GRP_COLS), BF16),
                   jax.ShapeDtypeStruct((t, _PB_COLS), BF16),
                   jax.ShapeDtypeStruct((4, t, HEAD_DIM), BF16),
                   jax.ShapeDtypeStruct((t, _C_COLS), BF16),
                   jax.ShapeDtypeStruct((t, _MG_COLS), F32),
                   jax.ShapeDtypeStruct((t, _BG_COLS), F32)),
        grid=(t // tm,),
        in_specs=[pl.BlockSpec((tm, D_MODEL), row),
                  pl.BlockSpec((1, D_MODEL), lambda i: (0, 0)),
                  pl.BlockSpec((None, D_MODEL, _W_COLS), lambda i: (layer, 0, 0))],
        out_specs=(pl.BlockSpec((tm, _GRP_COLS), row),
                   pl.BlockSpec((1, d1, tm // d1, _GRP_COLS), sub),
                   pl.BlockSpec((1, d2, tm // d2, _GRP_COLS), sub),
                   pl.BlockSpec((tm, _PB_COLS), row),
                   pl.BlockSpec((4, tm, HEAD_DIM), lambda i: (0, i, 0)),
                   pl.BlockSpec((tm, _C_COLS), row),
                   pl.BlockSpec((tm, _MG_COLS), row),
                   pl.BlockSpec((tm, _BG_COLS), row)),
        scratch_shapes=[pltpu.VMEM((3, tm, LANE), F32)],
        compiler_params=_params(1),
        name="proj_in",
    )(x2, g, w)


def _w_in_segments():
    a = A_HEADS * HEAD_DIM
    bq0 = 3 * a
    bkc0 = bq0 + B_HEADS * HEAD_DIM
    bks0 = bkc0 + 2 * B_KV_HEADS * HEAD_DIM
    bg0 = bks0 + 4 * B_KV_HEADS * HEAD_DIM
    c0 = bg0 + B_HEADS * 3
    mg0 = c0 + _C_COLS
    gw = A_HEADS_PER_GROUP * HEAD_DIM
    segs = [(_OFF_A + (gi * 3 + j) * gw, j * a + gi * gw, gw)
            for gi in range(len(A_GROUPS)) for j in range(3)]
    segs += [(_OFF_PB, bq0, B_HEADS * HEAD_DIM),
             (_OFF_PB + B_HEADS * HEAD_DIM, bks0, bg0 - bks0),
             (_OFF_CMP, bkc0, _CMP_COLS), (_OFF_C, c0, _C_COLS), (_OFF_MG, mg0, _MG_COLS),
             (_OFF_BG, bg0, B_HEADS * 3)]
    return segs


def _prep_w_in_kernel(w_ref, o_ref):
    o_ref[0, :, _OFF_BG:] = jnp.zeros((o_ref.shape[1], _BG_COLS), BF16)
    for dst, src, width in _w_in_segments():
        for c in range(0, width, 768):
            n = min(768, width - c)
            o_ref[0, :, dst + c:dst + c + n] = w_ref[0, :, src + c:src + c + n].astype(BF16)


def _prep_w_in(w):
    depth, rows, cols = w.shape
    rb = 128
    return pl.pallas_call(
        _prep_w_in_kernel,
        out_shape=jax.ShapeDtypeStruct((depth, rows, _W_COLS), BF16),
        grid=(depth, rows // rb),
        in_specs=[pl.BlockSpec((1, rb, cols), lambda l, r: (l, r, 0))],
        out_specs=pl.BlockSpec((1, rb, _W_COLS), lambda l, r: (l, r, 0)),
        compiler_params=_params(2),
        name="prep_w_in",
    )(w)


_A_UNROLL = 3


def _mixer_a_kernel(pa0_ref, pa1_ref, pa2_ref, bias_ref, o_ref, on_ref, ln_ref, *, seq):
    gw = A_HEADS_PER_GROUP * HEAD_DIM

    def band_block(src, l0, first, gi, rows):
        outs, lses = [], []
        for h in range(A_HEADS_PER_GROUP):
            qc, kc, vc = (slice(j * gw + h * HEAD_DIM, j * gw + (h + 1) * HEAD_DIM) for j in range(3))
            q = src[pl.ds(l0, A_TILE), qc] * ATTN_SCALE
            if first:
                kr = pl.ds(l0, A_TILE)
                bias = bias_ref[A_HEADS_PER_GROUP * gi + h, :, A_TILE:]
            else:
                kr = pl.ds(l0 - A_TILE, 2 * A_TILE)
                bias = bias_ref[A_HEADS_PER_GROUP * gi + h]
            s = _nt(q, src[kr, kc]) + bias
            m = jnp.max(s, axis=-1, keepdims=True)
            e = jnp.exp(s - m)
            l = jnp.sum(e, axis=-1, keepdims=True)
            outs.append(jnp.dot(e.astype(BF16), src[kr, vc], preferred_element_type=F32) / l)
            lses.append(jnp.broadcast_to(m + jnp.log(l), (A_TILE, HEAD_DIM)))
        on_ref[gi, rows, :] = jnp.concatenate(outs, axis=-1)
        ln_ref[gi, rows, :] = jnp.concatenate(lses, axis=-1)

    for gi, ((_, dil), pa_ref) in enumerate(zip(A_GROUPS, (pa0_ref, pa1_ref, pa2_ref))):
        n_blocks = seq // dil // A_TILE
        if dil == 1:
            band_block(pa_ref.at[0], 0, True, gi, pl.ds(0, A_TILE))
            assert (n_blocks - 1) % _A_UNROLL == 0

            def body0(it, c, pa_ref=pa_ref, gi=gi):
                for u in range(_A_UNROLL):
                    l0 = pl.multiple_of((1 + it * _A_UNROLL + u) * A_TILE, A_TILE)
                    band_block(pa_ref.at[0], l0, False, gi, pl.ds(l0, A_TILE))
                return c
            lax.fori_loop(0, (n_blocks - 1) // _A_UNROLL, body0, 0)
        elif n_blocks > 1:
            for r in range(dil):
                band_block(pa_ref.at[0, r], 0, True, gi, pl.ds(r, A_TILE, stride=dil))

            def body1(i, c, pa_ref=pa_ref, gi=gi, dil=dil):
                l0 = pl.multiple_of(i * A_TILE, A_TILE)
                for r in range(dil):
                    band_block(pa_ref.at[0, r], l0, False, gi, pl.ds(l0 * dil + r, A_TILE, stride=dil))
                return c
            lax.fori_loop(1, n_blocks, body1, 0)
        else:
            per_iter = 4

            def body2(it, c, pa_ref=pa_ref, gi=gi, dil=dil):
                for u in range(per_iter):
                    r = it * per_iter + u
                    band_block(pa_ref.at[0, r], 0, True, gi, pl.ds(r, A_TILE, stride=dil))
                return c
            lax.fori_loop(0, dil // per_iter, body2, 0)

    chunk = T_TILE
    for c0 in range(0, seq, chunk):
        rows = slice(c0, c0 + chunk)
        lse = [ln_ref[g, rows, :] for g in range(len(A_GROUPS))]
        top = functools.reduce(jnp.maximum, lse)
        ex = [jnp.exp(v - top) for v in lse]
        den = functools.reduce(lambda a, b: a + b, ex)
        for g in range(len(A_GROUPS)):
            o_ref[0, rows, g * gw:(g + 1) * gw] = (ex[g] / den * on_ref[g, rows, :]).astype(BF16)


def _mixer_a(pa0, pa1, pa2, bias):
    b, seq, _ = pa0.shape
    return pl.pallas_call(
        functools.partial(_mixer_a_kernel, seq=seq),
        out_shape=jax.ShapeDtypeStruct((b, seq, _GRP_COLS), BF16),
        grid=(b,),
        in_specs=[pl.BlockSpec((1, seq, _GRP_COLS), lambda i: (i, 0, 0)),
                  pl.BlockSpec((1,) + pa1.shape[1:], lambda i: (i, 0, 0, 0)),
                  pl.BlockSpec((1,) + pa2.shape[1:], lambda i: (i, 0, 0, 0)),
                  pl.BlockSpec(bias.shape, lambda i: (0, 0, 0))],
        out_specs=pl.BlockSpec((1, seq, _GRP_COLS), lambda i: (i, 0, 0)),
        scratch_shapes=[pltpu.VMEM((len(A_GROUPS), seq, LANE), F32),
                        pltpu.VMEM((len(A_GROUPS), seq, LANE), F32)],
        compiler_params=_params(1),
        name="mixer_a",
    )(pa0, pa1, pa2, bias)


def _compress_kernel(r_ref, pek_ref, pev_ref, w1k_ref, w1v_ref, w2k_ref, w2v_ref, o_ref):
    half = NSA_CMP_STRIDE * HEAD_DIM
    for kv, (pe_ref, w1_ref, w2_ref) in enumerate(((pek_ref, w1k_ref, w2k_ref),
                                                   (pev_ref, w1v_ref, w2v_ref))):
        for h in range(B_KV_HEADS):
            r = r_ref[kv * B_KV_HEADS + h, 0].astype(F32)
            lo = jnp.dot((r + pe_ref[:, :half]).astype(BF16), w1_ref[:half, :],
                         preferred_element_type=F32)
            hi = jnp.dot((r + pe_ref[:, half:]).astype(BF16), w1_ref[half:, :],
                         preferred_element_type=F32)
            hid = lo + pltpu.roll(hi, hi.shape[0] - 1, axis=0)
            act = jax.nn.gelu(hid).astype(BF16)
            o_ref[0, kv * B_KV_HEADS + h] = jnp.dot(act, w2_ref[...],
                                                    preferred_element_type=F32).astype(BF16)


def _compress(r, pe_k, pe_v, w1_k, w1_v, w2_k, w2_v):
    _, b, m, c = r.shape
    const2 = lambda i: (0, 0)
    return pl.pallas_call(
        _compress_kernel,
        out_shape=jax.ShapeDtypeStruct((b, 4, m, HEAD_DIM), BF16),
        grid=(b,),
        in_specs=[pl.BlockSpec((4, 1, m, c), lambda i: (0, i, 0, 0)),
                  pl.BlockSpec(pe_k.shape, const2), pl.BlockSpec(pe_v.shape, const2),
                  pl.BlockSpec(w1_k.shape, const2), pl.BlockSpec(w1_v.shape, const2),
                  pl.BlockSpec(w2_k.shape, const2), pl.BlockSpec(w2_v.shape, const2)],
        out_specs=pl.BlockSpec((1, 4, m, HEAD_DIM), lambda i: (i, 0, 0, 0)),
        compiler_params=_params(1),
        name="nsa_compress",
    )(r, pe_k, pe_v, w1_k, w1_v, w2_k, w2_v)


MASK_BIG = 2.0 ** 100
SEL_PAD = 32


def _nt(a, b, **kw):
    return lax.dot_general(a, b, (((1,), (1,)), ((), ())), preferred_element_type=F32, **kw)


def _top_mask_t(score, n_rows, n_top):
    sub = lax.broadcasted_iota(jnp.int32, score.shape, 0)
    rank = jnp.zeros(score.shape, F32)
    for j in range(n_rows):
        row = score[j:j + 1, :]
        ge = jnp.where(row >= score, 1.0, 0.0)
        gt = jnp.where(row > score, 1.0, 0.0)
        rank = rank + jnp.where(sub > j, ge, gt)
    return rank < n_top


BAND_SLOTS = 4


def _loop_pairs(lo, hi, body):
    n = jnp.maximum(hi - lo, 0)

    def pair(p, c):
        body(lo + 2 * p)
        body(lo + 2 * p + 1)
        return c
    lax.fori_loop(0, n // 2, pair, 0)
    pl.when(n % 2 == 1)(lambda: body(hi - 1))


def _two_pass_attention(i, streams, band_streams=()):
    groups = ((streams, lambda j: j), (band_streams, lambda j: j % BAND_SLOTS))

    def key_rows(j):
        return pl.ds(pl.multiple_of(j * T_TILE, T_TILE), T_TILE)

    def logits(group, j, d):
        some, slot = group
        for k_tile, q_t, _, bias_tile, s_ref, rmp_ref, _ in some:
            s = jnp.dot(k_tile(key_rows(j)), q_t(), preferred_element_type=F32)
            if d is not None:
                s = s + bias_tile(d)
            s_ref[slot(j)] = s
            fold = jnp.max(s.reshape(s.shape[0] // 8, 8, s.shape[1]), axis=0)
            rmp_ref[...] = fold if d == 0 else jnp.maximum(rmp_ref[...], fold)

    def near(n_full, n_band):
        for d in range(n_full):
            logits(groups[0], i - d, d)
        for d in range(n_band):
            logits(groups[1], i - d, d)

    if band_streams:
        pl.when(i == 0)(lambda: near(1, 1))
        pl.when(i == 1)(lambda: near(2, 2))
        pl.when(i >= 2)(lambda: near(2, 3))
    else:
        logits(groups[0], i, 0)
        pl.when(i >= 1)(lambda: logits(groups[0], i - 1, 1))

    _loop_pairs(0, i - 1, lambda j: logits(groups[0], j, None))

    tops = {}
    for some, _ in groups:
        for stream in some:
            rmp_ref, acc_ref = stream[-2:]
            tops[id(rmp_ref)] = jnp.max(rmp_ref[...], axis=0, keepdims=True)
            acc_ref[...] = jnp.zeros(acc_ref.shape, F32)

    def pv(active):
        def body(j):
            for some, slot in active:
                for _, _, vt_ref, _, s_ref, rmp_ref, acc_ref in some:
                    e = jnp.exp(s_ref[slot(j)] - tops[id(rmp_ref)])
                    acc_ref[...] += jnp.dot(vt_ref[j], e.astype(BF16), preferred_element_type=F32)
        return body

    if band_streams:
        band_lo = jnp.maximum(i - 2, 0)
        _loop_pairs(0, band_lo, pv(groups[:1]))
        pv_all = pv(groups)

        def merged(j, c):
            pv_all(j)
            return c
        lax.fori_loop(band_lo, i + 1, merged, 0)
    else:
        _loop_pairs(0, i + 1, pv(groups[:1]))

    outs = []
    for some, _ in groups:
        for *_, acc_ref in some:
            acc = acc_ref[...]
            outs.append(acc[:HEAD_DIM, :] / acc[HEAD_DIM:HEAD_DIM + 1, :])
    return outs


def _eye(n, dtype=BF16):
    return (lax.broadcasted_iota(jnp.int32, (n, n), 0)
            == lax.broadcasted_iota(jnp.int32, (n, n), 1)).astype(dtype)


def _transposed_values(v_ref_rows, n_tiles):
    eye = _eye(LANE)
    for j in range(n_tiles):
        v = _with_ones_column(v_ref_rows(slice(j * T_TILE, (j + 1) * T_TILE)))
        yield j, _nt(eye, v).astype(BF16)


def _with_ones_column(v):
    one = (lax.broadcasted_iota(jnp.int32, v.shape, 1) == 0).astype(F32)
    return jnp.concatenate([v.astype(F32), one], axis=1).astype(BF16)


def _with_block_onehot(k, block):
    n = k.shape[0]
    shape = (n, SEL_PAD)
    hot = (lax.broadcasted_iota(jnp.int32, shape, 0) // block
           == lax.broadcasted_iota(jnp.int32, shape, 1)).astype(F32)
    return jnp.concatenate([k.astype(F32), hot, jnp.zeros(shape, F32)], axis=1).astype(BF16)


def _augment_q_t(q_t, allowed_t):
    pen = (allowed_t - 1.0) * MASK_BIG
    return jnp.concatenate([q_t, pen, jnp.zeros(pen.shape, F32)], axis=0).astype(BF16)


_N_SLC = 32
_G_ROWS = B_GROUP * T_TILE


def _nsa_kernel(pb_ref, cmp_ref, bg_ref, bias_ref, o_ref,
                kaug_ref, vst_ref, vwt_ref, eye_ref, sig_ref, q3t_ref, qaugt_ref, s_ref, sw_ref,
                rmp_ref, acc_ref, ocmp_ref, *, seq):
    n_tiles = seq // T_TILE
    n_cmp_rows = seq // NSA_CMP_STRIDE
    q_cols = B_HEADS * HEAD_DIM
    ks0, vs0, kw0, vw0 = (q_cols + i * B_KV_HEADS * HEAD_DIM for i in range(4))

    def head_cols(c0, h):
        return slice(c0 + h * HEAD_DIM, c0 + (h + 1) * HEAD_DIM)

    j_id = lax.broadcasted_iota(jnp.int32, (_N_SLC, n_cmp_rows), 0)
    c_id = lax.broadcasted_iota(jnp.int32, (_N_SLC, n_cmp_rows), 1)
    overlap_t = ((c_id * NSA_CMP_STRIDE < (j_id + 1) * NSA_SLC_BLOCK)
                 & (c_id * NSA_CMP_STRIDE + NSA_CMP_BLOCK > j_id * NSA_SLC_BLOCK)).astype(F32)
    eye_ref[...] = _eye(T_TILE)
    for h in range(B_KV_HEADS):
        kaug_ref[h] = _with_block_onehot(pb_ref[0, :, head_cols(ks0, h)], NSA_SLC_BLOCK)
        for j, vt in _transposed_values(lambda r, h=h: pb_ref[0, r, head_cols(vs0, h)], n_tiles):
            vst_ref[h, j] = vt
        for j, vt in _transposed_values(lambda r, h=h: pb_ref[0, r, head_cols(vw0, h)], n_tiles):
            vwt_ref[h, j] = vt

    def tiles(a):
        return [a[:, j * T_TILE:(j + 1) * T_TILE] for j in range(n_tiles)]

    pick = (lax.broadcasted_iota(jnp.int32, (SEL_PAD, _BG_COLS), 0)
            == lax.broadcasted_iota(jnp.int32, (SEL_PAD, _BG_COLS), 1)).astype(F32)
    sig_all = jax.nn.sigmoid(_nt(pick, bg_ref[0], precision=lax.Precision.HIGHEST))
    for j, t in enumerate(tiles(sig_all)):
        sig_ref[j] = t

    eye_h = _eye(HEAD_DIM)
    t_lane = lax.broadcasted_iota(jnp.int32, (n_cmp_rows, seq), 1)
    c_end = (lax.broadcasted_iota(jnp.int32, (n_cmp_rows, seq), 0) * NSA_CMP_STRIDE
             + NSA_CMP_BLOCK - 1)
    ok = (c_end <= t_lane) & (c_end < seq)
    ok_f = ok.astype(F32)
    j_sub = lax.broadcasted_iota(jnp.int32, (_N_SLC, seq), 0)
    jt = lax.broadcasted_iota(jnp.int32, (_N_SLC, seq), 1) // NSA_SLC_BLOCK
    forced = ((j_sub == 0) | (j_sub == jt) | (j_sub == jt - 1)).astype(F32)
    valid = j_sub <= jt

    for h in range(B_KV_HEADS):
        kc = cmp_ref[0, h]
        vct = _nt(eye_h, cmp_ref[0, B_KV_HEADS + h]).astype(BF16)
        q_ts, p_sum = [], None
        for g in range(B_GROUP):
            q_t = _nt(eye_h, pb_ref[0, :, head_cols(0, h * B_GROUP + g)]) * ATTN_SCALE
            q_ts.append(q_t)
            lc = jnp.where(ok, jnp.dot(kc, q_t.astype(BF16), preferred_element_type=F32), NEG_INF)
            ec = jnp.exp(lc - jnp.max(lc, axis=0, keepdims=True))
            pc = ec / jnp.sum(ec, axis=0, keepdims=True) * ok_f
            o_cmp = jnp.dot(vct, pc.astype(BF16), preferred_element_type=F32)
            for j, t in enumerate(tiles(o_cmp)):
                ocmp_ref[h, j, :, g * T_TILE:(g + 1) * T_TILE] = t
            p_sum = pc if p_sum is None else p_sum + pc
        imp_t = jnp.dot(overlap_t, p_sum, precision=lax.Precision.HIGHEST,
                        preferred_element_type=F32)
        score = jnp.where(valid, imp_t + FORCE_BONUS * forced, NEG_INF)
        sel_t = jnp.where(valid & _top_mask_t(score, _N_SLC, NSA_SLC_TOPK), 1.0, 0.0)
        pen_tiles = tiles((sel_t - 1.0) * MASK_BIG)
        q_tiles = [tiles(q_t) for q_t in q_ts]
        for j in range(n_tiles):
            q3t = jnp.concatenate([q_tiles[g][j] for g in range(B_GROUP)], axis=1)
            pen = jnp.concatenate([pen_tiles[j]] * B_GROUP, axis=1)
            q3t_ref[h, j] = q3t.astype(BF16)
            qaugt_ref[h, j] = jnp.concatenate([q3t, pen, jnp.zeros(pen.shape, F32)],
                                              axis=0).astype(BF16)

    def tile_body(i, carry):
        rows = pl.ds(pl.multiple_of(i * T_TILE, T_TILE), T_TILE)

        outs = _two_pass_attention(
            i,
            [(lambda kr, h=h: kaug_ref[h, kr, :], lambda h=h: qaugt_ref[h, i], vst_ref.at[h],
              lambda d, h=h: bias_ref[h, d], s_ref.at[h], rmp_ref.at[0, h], acc_ref.at[0, h])
             for h in range(B_KV_HEADS)],
            [(lambda kr, h=h: pb_ref[0, kr, head_cols(kw0, h)], lambda h=h: q3t_ref[h, i],
              vwt_ref.at[h], lambda d, h=h: bias_ref[h, d], sw_ref.at[h], rmp_ref.at[1, h],
              acc_ref.at[1, h]) for h in range(B_KV_HEADS)])
        oslc, owin = outs[:B_KV_HEADS], outs[B_KV_HEADS:]

        sig_t = sig_ref[i]
        heads = []
        for h in range(B_KV_HEADS):
            for g in range(B_GROUP):
                head = h * B_GROUP + g
                gc = slice(g * T_TILE, (g + 1) * T_TILE)
                heads.append(sig_t[3 * head:3 * head + 1, :] * ocmp_ref[h, i, :, gc]
                             + sig_t[3 * head + 1:3 * head + 2, :] * oslc[h][:, gc]
                             + sig_t[3 * head + 2:3 * head + 3, :] * owin[h][:, gc])
        o_t = jnp.concatenate(heads, axis=0).astype(BF16)
        o_ref[0, rows, :] = _nt(eye_ref[...], o_t).astype(o_ref.dtype)
        return carry

    lax.fori_loop(0, n_tiles, tile_body, 0)


def _nsa(pb, cmp, bg, bias):
    b, seq, _ = pb.shape
    n_tiles = seq // T_TILE
    return pl.pallas_call(
        functools.partial(_nsa_kernel, seq=seq),
        out_shape=jax.ShapeDtypeStruct((b, seq, B_HEADS * HEAD_DIM), BF16),
        grid=(b,),
        in_specs=[pl.BlockSpec((1, seq, _PB_COLS), lambda i: (i, 0, 0)),
                  pl.BlockSpec((1,) + cmp.shape[1:], lambda i: (i, 0, 0, 0)),
                  pl.BlockSpec((1, seq, _BG_COLS), lambda i: (i, 0, 0)),
                  pl.BlockSpec(bias.shape, lambda i: (0, 0, 0, 0))],
        out_specs=pl.BlockSpec((1, seq, B_HEADS * HEAD_DIM), lambda i: (i, 0, 0)),
        scratch_shapes=[pltpu.VMEM((B_KV_HEADS, seq, LANE), BF16),
                        pltpu.VMEM((B_KV_HEADS, n_tiles, LANE, T_TILE), BF16),
                        pltpu.VMEM((B_KV_HEADS, n_tiles, LANE, T_TILE), BF16),
                        pltpu.VMEM((T_TILE, T_TILE), BF16),
                        pltpu.VMEM((n_tiles, SEL_PAD, T_TILE), F32),
                        pltpu.VMEM((B_KV_HEADS, n_tiles, HEAD_DIM, _G_ROWS), BF16),
                        pltpu.VMEM((B_KV_HEADS, n_tiles, LANE, _G_ROWS), BF16),
                        pltpu.VMEM((B_KV_HEADS, n_tiles, T_TILE, _G_ROWS), F32),
                        pltpu.VMEM((B_KV_HEADS, BAND_SLOTS, T_TILE, _G_ROWS), F32),
                        pltpu.VMEM((2, B_KV_HEADS, 8, _G_ROWS), F32),
                        pltpu.VMEM((2, B_KV_HEADS, LANE, _G_ROWS), F32),
                        pltpu.VMEM((B_KV_HEADS, n_tiles, HEAD_DIM, _G_ROWS), F32)],
        compiler_params=_params(1),
        name="nsa",
    )(pb, cmp, bg, bias)


def _moba_kernel(pc_ref, bias_ref, o_ref, kaug_ref, vt_ref, eye_ref, qaugt_ref, s_ref,
                 rmp_ref, acc_ref, *, seq):
    n_blk = seq // MOBA_BLOCK
    width = C_HEADS * HEAD_DIM
    eye_ref[...] = _eye(T_TILE)

    def cols(j, h):
        return slice(j * width + h * HEAD_DIM, j * width + (h + 1) * HEAD_DIM)

    n_sub = lax.broadcasted_iota(jnp.int32, (SEL_PAD, seq), 0)
    own = lax.broadcasted_iota(jnp.int32, (SEL_PAD, seq), 1) // MOBA_BLOCK
    past = n_sub < own
    eye_h = _eye(HEAD_DIM)
    for h in range(C_HEADS):
        kaug_ref[h] = _with_block_onehot(pc_ref[0, :, cols(1, h)], MOBA_BLOCK)
        for j, vt in _transposed_values(lambda r, h=h: pc_ref[0, r, cols(2, h)], n_blk):
            vt_ref[h, j] = vt
        kmean = jnp.concatenate(
            [jnp.mean(pc_ref[0, n * MOBA_BLOCK:(n + 1) * MOBA_BLOCK, cols(1, h)].astype(F32), axis=0,
                      keepdims=True) for n in range(n_blk)]
            + [jnp.zeros((SEL_PAD - n_blk, HEAD_DIM), F32)], axis=0)
        q_t = _nt(eye_h, pc_ref[0, :, cols(0, h)])
        gate_t = jnp.dot(kmean, q_t, precision=lax.Precision.HIGHEST,
                         preferred_element_type=F32)
        score = jnp.where(past, gate_t, NEG_INF)
        picked = past & _top_mask_t(score, n_blk, MOBA_TOPK)
        allowed_t = jnp.where(picked | (n_sub == own), 1.0, 0.0)
        qaug_t = _augment_q_t(q_t * ATTN_SCALE, allowed_t)
        for j in range(n_blk):
            qaugt_ref[h, j] = qaug_t[:, j * T_TILE:(j + 1) * T_TILE]

    def tile_body(i, carry):
        rows = pl.ds(pl.multiple_of(i * MOBA_BLOCK, MOBA_BLOCK), MOBA_BLOCK)
        outs = _two_pass_attention(
            i, [(lambda kr, h=h: kaug_ref[h, kr, :], lambda h=h: qaugt_ref[h, i], vt_ref.at[h],
                 lambda d, h=h: bias_ref[h, d], s_ref.at[h], rmp_ref.at[h], acc_ref.at[h])
                for h in range(C_HEADS)])
        o_t = jnp.concatenate(outs, axis=0).astype(BF16)
        o_ref[0, rows, :] = _nt(eye_ref[...], o_t).astype(o_ref.dtype)
        return carry

    lax.fori_loop(0, n_blk, tile_body, 0)


def _moba(pc, bias):
    b, seq, _ = pc.shape
    width = C_HEADS * HEAD_DIM
    n_tiles = seq // T_TILE
    return pl.pallas_call(
        functools.partial(_moba_kernel, seq=seq),
        out_shape=jax.ShapeDtypeStruct((b, seq, width), BF16),
        grid=(b,),
        in_specs=[pl.BlockSpec((1, seq, _C_COLS), lambda i: (i, 0, 0)),
                  pl.BlockSpec(bias.shape, lambda i: (0, 0, 0, 0))],
        out_specs=pl.BlockSpec((1, seq, width), lambda i: (i, 0, 0)),
        scratch_shapes=[pltpu.VMEM((C_HEADS, seq, LANE), BF16),
                        pltpu.VMEM((C_HEADS, n_tiles, LANE, T_TILE), BF16),
                        pltpu.VMEM((T_TILE, T_TILE), BF16),
                        pltpu.VMEM((C_HEADS, n_tiles, LANE, T_TILE), BF16),
                        pltpu.VMEM((C_HEADS, n_tiles, T_TILE, T_TILE), F32),
                        pltpu.VMEM((C_HEADS, 8, T_TILE), F32),
                        pltpu.VMEM((C_HEADS, LANE, T_TILE), F32)],
        compiler_params=_params(1),
        name="moba",
    )(pc, bias)


def _merge_kernel(oa_ref, ob_ref, oc_ref, mg_ref, x_ref, wb_ref, wo_ref, y_ref):
    r0 = A_HEADS * HEAD_DIM
    r1 = r0 + B_HEADS * HEAD_DIM
    ya = jnp.dot(oa_ref[...], wb_ref[0:r0, :], preferred_element_type=F32)
    yb = jnp.dot(ob_ref[...], wb_ref[r0:r1, :], preferred_element_type=F32)
    yc = jnp.dot(oc_ref[...], wb_ref[r1:, :], preferred_element_type=F32)
    merged = (jax.nn.sigmoid(mg_ref[:, 0:D_MODEL]) * ya
              + jax.nn.sigmoid(mg_ref[:, D_MODEL:2 * D_MODEL]) * yb
              + jax.nn.sigmoid(mg_ref[:, 2 * D_MODEL:]) * yc)
    y_ref[...] = x_ref[...] + jnp.dot(merged.astype(BF16), wo_ref[...], preferred_element_type=F32)


def _merge(oa, ob, oc, mg, x2, wb, wo):
    t = x2.shape[0]
    tm = ROW_TILE
    row = lambda i: (i, 0)
    return pl.pallas_call(
        _merge_kernel,
        out_shape=jax.ShapeDtypeStruct((t, D_MODEL), F32),
        grid=(t // tm,),
        in_specs=[pl.BlockSpec((tm, oa.shape[1]), row),
                  pl.BlockSpec((tm, ob.shape[1]), row), pl.BlockSpec((tm, oc.shape[1]), row),
                  pl.BlockSpec((tm, _MG_COLS), row), pl.BlockSpec((tm, D_MODEL), row),
                  pl.BlockSpec(wb.shape, lambda i: (0, 0)), pl.BlockSpec(wo.shape, lambda i: (0, 0))],
        out_specs=pl.BlockSpec((tm, D_MODEL), row),
        compiler_params=_params(1),
        name="merge",
    )(oa, ob, oc, mg, x2, wb, wo)


_FF_CHUNK = D_FF // 2


def _ffn_kernel(x_ref, halo_ref, g_ref, wu_ref, cw_ref, cb_ref, wd_ref, gf_ref, y_ref, xn_ref,
                *, tiles_per_seq, final_norm):
    i = pl.program_id(0)
    x = x_ref[...]
    g = g_ref[...]
    keep = (i % tiles_per_seq != 0).astype(F32)
    xn_ref[0:FFN_HALO, :] = (_rmsnorm_rows(halo_ref[...], g) * keep).astype(BF16)
    xn_ref[FFN_HALO:, :] = _rmsnorm_rows(x, g).astype(BF16)
    xn = xn_ref[...]
    rows = xn.shape[0]

    def conv(c0):
        hcol = jnp.dot(xn, wu_ref[:, c0:c0 + _FF_CHUNK], preferred_element_type=F32)
        out = (cw_ref[2:3, c0:c0 + _FF_CHUNK] * hcol
               + cw_ref[1:2, c0:c0 + _FF_CHUNK] * pltpu.roll(hcol, 1, axis=0)
               + cw_ref[0:1, c0:c0 + _FF_CHUNK] * pltpu.roll(hcol, 2, axis=0)
               + cb_ref[:, c0:c0 + _FF_CHUNK])
        return out[FFN_HALO:rows]

    acc = x
    for c in range(0, D_FF, _FF_CHUNK):
        a = conv(c)
        u = conv(D_FF + c)
        act = (a * jax.nn.sigmoid(a) * u).astype(BF16)
        acc = acc + jnp.dot(act, wd_ref[c:c + _FF_CHUNK, :], preferred_element_type=F32)
    if final_norm:
        acc = _rmsnorm_rows(acc, gf_ref[...])
    y_ref[...] = acc


def _ffn(x2, g, wu, cw, cb, wd, gf, *, seq, final_norm):
    t = x2.shape[0]
    tm = ROW_TILE
    const = lambda i: (0, 0)
    halo_blocks = tm // FFN_HALO
    return pl.pallas_call(
        functools.partial(_ffn_kernel, tiles_per_seq=seq // tm, final_norm=final_norm),
        out_shape=jax.ShapeDtypeStruct((t, D_MODEL), F32),
        grid=(t // tm,),
        in_specs=[pl.BlockSpec((tm, D_MODEL), lambda i: (i, 0)),
                  pl.BlockSpec((FFN_HALO, D_MODEL), lambda i: (jnp.maximum(i * halo_blocks - 1, 0), 0)),
                  pl.BlockSpec((1, D_MODEL), const),
                  pl.BlockSpec(wu.shape, const), pl.BlockSpec(cw.shape, const),
                  pl.BlockSpec(cb.shape, const), pl.BlockSpec(wd.shape, const),
                  pl.BlockSpec((1, D_MODEL), const)],
        out_specs=pl.BlockSpec((tm, D_MODEL), lambda i: (i, 0)),
        scratch_shapes=[pltpu.VMEM((tm + FFN_HALO, D_MODEL), BF16)],
        compiler_params=_params(1),
        name="conv_ffn",
    )(x2, x2, g, wu, cw, cb, wd, gf)


def kernel(x, rel_bias, norm_mix, w_in, cmp_pe_k, cmp_w1_k, cmp_w2_k, cmp_pe_v, cmp_w1_v, cmp_w2_v,
           w_branch, w_out, norm_ffn, w_up, conv_w, conv_b, w_down, norm_final):
    b, s, d = x.shape
    depth = w_in.shape[0]
    t = b * s
    assert d == D_MODEL and s % T_TILE == 0 and s // NSA_SLC_BLOCK == _N_SLC and t % ROW_TILE == 0
    assert s % ROW_TILE == 0 and all(s % (dil * A_TILE) == 0 for _, dil in A_GROUPS)
    assert all(win // dil == A_TILE for win, dil in A_GROUPS) and A_GROUPS[0][1] == 1
    assert all(ROW_TILE % dil == 0 for _, dil in A_GROUPS)

    hg = A_HEADS_PER_GROUP
    bias_a = jnp.concatenate([
        _bias_tiles(rel_bias,
                    np.concatenate([_bucket_tile(A_TILE, 1, dil, win // dil),
                                    _bucket_tile(A_TILE, 0, dil, win // dil)], axis=1),
                    gi * hg, hg)
        for gi, (win, dil) in enumerate(A_GROUPS)], axis=0)
    assert (_bucket_tile(T_TILE, 2, 1, 3 * T_TILE) == REL_BUCKETS - 1).all()
    bias_b = _bias_tiles_t(rel_bias, [_bucket_tile(T_TILE, d, 1, NSA_WINDOW - 1) for d in (0, 1, 2)],
                           A_HEADS, B_HEADS)
    bias_b = bias_b.reshape(B_KV_HEADS, B_GROUP, 3, T_TILE, T_TILE).transpose(0, 2, 3, 1, 4).reshape(
        B_KV_HEADS, 3, T_TILE, _G_ROWS)
    bias_c = _bias_tiles_t(rel_bias, [_bucket_tile(T_TILE, d, 1, s) for d in (0, 1)],
                           A_HEADS + B_HEADS, C_HEADS)

    w_in_k = _prep_w_in(w_in)
    x2 = x.reshape(t, d)
    for i in range(depth):
        pa0, pa1, pa2, pb, pcmp, pc, mg, bg = _proj_in(x2, norm_mix[i][None, :], w_in_k, layer=i,
                                                       seq=s)

        oa = _mixer_a(pa0.reshape(b, s, _GRP_COLS), pa1, pa2, bias_a)

        cmp = _compress(pcmp.reshape(4, b, s // NSA_CMP_STRIDE, NSA_CMP_STRIDE * HEAD_DIM),
                        cmp_pe_k[i].reshape(1, -1), cmp_pe_v[i].reshape(1, -1),
                        cmp_w1_k[i].astype(BF16), cmp_w1_v[i].astype(BF16),
                        cmp_w2_k[i].astype(BF16), cmp_w2_v[i].astype(BF16))
        ob = _nsa(pb.reshape(b, s, _PB_COLS), cmp, bg.reshape(b, s, _BG_COLS), bias_b)

        oc = _moba(pc.reshape(b, s, _C_COLS), bias_c)

        x2 = _merge(oa.reshape(t, -1), ob.reshape(t, -1), oc.reshape(t, -1), mg, x2,
                    w_branch[i].astype(BF16), w_out[i].astype(BF16))
        x2 = _ffn(x2, norm_ffn[i][None, :], w_up[i].astype(BF16), conv_w[i], conv_b[i][None, :],
                  w_down[i].astype(BF16), norm_final[None, :], seq=s, final_norm=(i == depth - 1))
    return x2.reshape(b, s, d)
```

```python
import functools
import math

import jax
import jax.numpy as jnp
import numpy as np
from jax import lax
from jax.experimental import pallas as pl
from jax.experimental.pallas import tpu as pltpu

F32 = jnp.float32
BF16 = jnp.bfloat16

D_MODEL = 1024
HEAD_DIM = 64
A_GROUPS = ((128, 1), (512, 4), (2048, 16))
A_HEADS_PER_GROUP = 2
A_HEADS = 6
B_HEADS = 6
B_KV_HEADS = 2
B_GROUP = 3
C_HEADS = 4
NSA_CMP_BLOCK = 32
NSA_CMP_STRIDE = 16
NSA_CMP_HIDDEN = 256
NSA_SLC_BLOCK = 64
NSA_SLC_TOPK = 16
NSA_WINDOW = 512
MOBA_BLOCK = 256
MOBA_TOPK = 3
REL_BUCKETS = 32
REL_MAX_DIST = 128
D_FF = 2816
RMS_EPS = 1e-6
NEG_INF = -1e30
FORCE_BONUS = 1e4
ATTN_SCALE = HEAD_DIM ** -0.5

LANE = 128
A_TILE = 128
T_TILE = 256
ROW_TILE = 512
FFN_HALO = 8
VMEM_LIMIT = 56 * 1024 * 1024

_GRP_COLS = 3 * A_HEADS_PER_GROUP * HEAD_DIM
_A_COLS = len(A_GROUPS) * _GRP_COLS
_PB_COLS = B_HEADS * HEAD_DIM + 4 * B_KV_HEADS * HEAD_DIM
_CMP_COLS = 2 * B_KV_HEADS * HEAD_DIM
_C_COLS = 3 * C_HEADS * HEAD_DIM
_MG_COLS = 3 * D_MODEL
_BG_COLS = LANE
_OFF_A = 0
_OFF_PB = _OFF_A + _A_COLS
_OFF_CMP = _OFF_PB + _PB_COLS
_OFF_C = _OFF_CMP + _CMP_COLS
_OFF_MG = _OFF_C + _C_COLS
_OFF_BG = _OFF_MG + _MG_COLS
_W_COLS = _OFF_BG + _BG_COLS


def _params(n_grid):
    return pltpu.CompilerParams(dimension_semantics=("arbitrary",) * n_grid,
                                vmem_limit_bytes=VMEM_LIMIT)


def _rel_bucket_np(dist):
    n = np.maximum(dist, 0)
    exact = REL_BUCKETS // 2
    nf = np.maximum(n, 1).astype(np.float32)
    large = exact + (np.log(nf / np.float32(exact)) / np.float32(math.log(REL_MAX_DIST / exact))
                     * np.float32(REL_BUCKETS - exact)).astype(np.int32)
    return np.where(n < exact, n, np.minimum(large, REL_BUCKETS - 1)).astype(np.int32)


def _bucket_tile(tile, block_offset, dil, max_rel):
    rel = block_offset * tile + np.arange(tile)[:, None] - np.arange(tile)[None, :]
    ok = (rel >= 0) & (rel <= max_rel)
    return np.where(ok, _rel_bucket_np(rel * dil), -1).astype(np.int32)


def _bias_tiles_kernel(tbl_ref, idx_ref, o_ref, *, head0, relative):
    h = pl.program_id(0) + head0
    idx = idx_ref[...]
    acc = jnp.full(idx.shape, NEG_INF, F32)
    for b in range(REL_BUCKETS):
        acc = jnp.where(idx == b, tbl_ref[b, h], acc)
    if relative:
        acc = acc - tbl_ref[REL_BUCKETS - 1, h]
    o_ref[0] = acc


def _bias_tiles(rel_bias, idx, head0, n_heads, relative=False):
    r, c = idx.shape
    return pl.pallas_call(
        functools.partial(_bias_tiles_kernel, head0=head0, relative=relative),
        out_shape=jax.ShapeDtypeStruct((n_heads, r, c), F32),
        grid=(n_heads,),
        in_specs=[pl.BlockSpec(memory_space=pltpu.SMEM),
                  pl.BlockSpec((r, c), lambda h: (0, 0))],
        out_specs=pl.BlockSpec((1, r, c), lambda h: (h, 0, 0)),
        compiler_params=_params(1),
        name="bias_tiles",
    )(rel_bias, jnp.asarray(idx))


def _bias_tiles_t(rel_bias, idx_tiles, head0, n_heads):
    r, c = idx_tiles[0].shape
    flat = _bias_tiles(rel_bias, np.concatenate([t.T for t in idx_tiles], axis=1), head0, n_heads,
                       relative=True)
    return flat.reshape(n_heads, c, len(idx_tiles), r).transpose(0, 2, 1, 3)


def _rmsnorm_rows(x, g):
    return x * lax.rsqrt(jnp.mean(x * x, axis=-1, keepdims=True) + RMS_EPS) * g


def _proj_in_kernel(x_ref, g_ref, w_ref, pa0_ref, pa1_ref, pa2_ref, pb_ref, pcmp_ref, pc_ref,
                    mg_ref, bg_ref, regroup_ref):
    xn = _rmsnorm_rows(x_ref[...], g_ref[...]).astype(BF16)
    tm = xn.shape[0]

    def mm(c0, c1):
        return jnp.dot(xn, w_ref[:, c0:c1], preferred_element_type=F32)

    pa0_ref[...] = mm(_OFF_A, _OFF_A + _GRP_COLS).astype(BF16)
    for gi, out_ref in ((1, pa1_ref), (2, pa2_ref)):
        dil = A_GROUPS[gi][1]
        res = mm(_OFF_A + gi * _GRP_COLS, _OFF_A + (gi + 1) * _GRP_COLS)
        for j in range(3):
            regroup_ref[j] = res[:, j * LANE:(j + 1) * LANE]
        for r in range(dil):
            for j in range(3):
                out_ref[0, r, :, j * LANE:(j + 1) * LANE] = (
                    regroup_ref[j, pl.ds(r, tm // dil, stride=dil), :].astype(BF16))
    pb_ref[:, 0:384] = mm(_OFF_PB, _OFF_PB + 384).astype(BF16)
    pb_ref[:, 384:896] = mm(_OFF_PB + 384, _OFF_PB + 896).astype(BF16)
    cmp = mm(_OFF_CMP, _OFF_CMP + _CMP_COLS).astype(BF16)
    for i in range(4):
        pcmp_ref[i] = cmp[:, i * HEAD_DIM:(i + 1) * HEAD_DIM]
    for c in range(0, _C_COLS, 384):
        pc_ref[:, c:c + 384] = mm(_OFF_C + c, _OFF_C + c + 384).astype(BF16)
    for c in range(0, _MG_COLS, 512):
        mg_ref[:, c:c + 512] = mm(_OFF_MG + c, _OFF_MG + c + 512).astype(BF16)
    bg_ref[...] = mm(_OFF_BG, _OFF_BG + _BG_COLS)


def _proj_in(x2, g, w, *, layer, seq):
    t = x2.shape[0]
    tm = ROW_TILE
    b = t // seq
    tps = seq // tm
    row = lambda i: (i, 0)
    d1, d2 = A_GROUPS[1][1], A_GROUPS[2][1]
    sub = lambda i: (i // tps, 0, i % tps, 0)
    return pl.pallas_call(
        _proj_in_kernel,
        out_shape=(jax.ShapeDtypeStruct((t, _GRP_COLS), BF16),
                   jax.ShapeDtypeStruct((b, d1, seq // d1, _GRP_COLS), BF16),
                   jax.ShapeDtypeStruct((b, d2, seq // d2, _GRP_COLS), BF16),
                   jax.ShapeDtypeStruct((t, _PB_COLS), BF16),
                   jax.ShapeDtypeStruct((4, t, HEAD_DIM), BF16),
                   jax.ShapeDtypeStruct((t, _C_COLS), BF16),
                   jax.ShapeDtypeStruct((t, _MG_COLS), BF16),
                   jax.ShapeDtypeStruct((t, _BG_COLS), F32)),
        grid=(t // tm,),
        in_specs=[pl.BlockSpec((tm, D_MODEL), row),
                  pl.BlockSpec((1, D_MODEL), lambda i: (0, 0)),
                  pl.BlockSpec((None, D_MODEL, _W_COLS), lambda i: (layer, 0, 0))],
        out_specs=(pl.BlockSpec((tm, _GRP_COLS), row),
                   pl.BlockSpec((1, d1, tm // d1, _GRP_COLS), sub),
                   pl.BlockSpec((1, d2, tm // d2, _GRP_COLS), sub),
                   pl.BlockSpec((tm, _PB_COLS), row),
                   pl.BlockSpec((4, tm, HEAD_DIM), lambda i: (0, i, 0)),
                   pl.BlockSpec((tm, _C_COLS), row),
                   pl.BlockSpec((tm, _MG_COLS), row),
                   pl.BlockSpec((tm, _BG_COLS), row)),
        scratch_shapes=[pltpu.VMEM((3, tm, LANE), F32)],
        compiler_params=_params(1),
        name="proj_in",
    )(x2, g, w)


def _w_in_segments():
    a = A_HEADS * HEAD_DIM
    bq0 = 3 * a
    bkc0 = bq0 + B_HEADS * HEAD_DIM
    bks0 = bkc0 + 2 * B_KV_HEADS * HEAD_DIM
    bg0 = bks0 + 4 * B_KV_HEADS * HEAD_DIM
    c0 = bg0 + B_HEADS * 3
    mg0 = c0 + _C_COLS
    gw = A_HEADS_PER_GROUP * HEAD_DIM
    segs = [(_OFF_A + (gi * 3 + j) * gw, j * a + gi * gw, gw)
            for gi in range(len(A_GROUPS)) for j in range(3)]
    segs += [(_OFF_PB, bq0, B_HEADS * HEAD_DIM),
             (_OFF_PB + B_HEADS * HEAD_DIM, bks0, bg0 - bks0),
             (_OFF_CMP, bkc0, _CMP_COLS), (_OFF_C, c0, _C_COLS), (_OFF_MG, mg0, _MG_COLS),
             (_OFF_BG, bg0, B_HEADS * 3)]
    return segs


def _prep_w_in_kernel(w_ref, o_ref):
    o_ref[0, :, _OFF_BG:] = jnp.zeros((o_ref.shape[1], _BG_COLS), BF16)
    for dst, src, width in _w_in_segments():
        for c in range(0, width, 768):
            n = min(768, width - c)
            o_ref[0, :, dst + c:dst + c + n] = w_ref[0, :, src + c:src + c + n].astype(BF16)


def _prep_w_in(w):
    depth, rows, cols = w.shape
    rb = 128
    return pl.pallas_call(
        _prep_w_in_kernel,
        out_shape=jax.ShapeDtypeStruct((depth, rows, _W_COLS), BF16),
        grid=(depth, rows // rb),
        in_specs=[pl.BlockSpec((1, rb, cols), lambda l, r: (l, r, 0))],
        out_specs=pl.BlockSpec((1, rb, _W_COLS), lambda l, r: (l, r, 0)),
        compiler_params=_params(2),
        name="prep_w_in",
    )(w)


_A_UNROLL = 3


def _mixer_a_kernel(pa0_ref, pa1_ref, pa2_ref, bias_ref, o_ref, on_ref, ln_ref, *, seq):
    gw = A_HEADS_PER_GROUP * HEAD_DIM

    def band_block(src, l0, first, gi, rows):
        outs, lses = [], []
        for h in range(A_HEADS_PER_GROUP):
            qc, kc, vc = (slice(j * gw + h * HEAD_DIM, j * gw + (h + 1) * HEAD_DIM) for j in range(3))
            q = src[pl.ds(l0, A_TILE), qc] * ATTN_SCALE
            if first:
                kr = pl.ds(l0, A_TILE)
                bias = bias_ref[A_HEADS_PER_GROUP * gi + h, :, A_TILE:]
            else:
                kr = pl.ds(l0 - A_TILE, 2 * A_TILE)
                bias = bias_ref[A_HEADS_PER_GROUP * gi + h]
            s = _nt(q, src[kr, kc]) + bias
            m = jnp.max(s, axis=-1, keepdims=True)
            e = jnp.exp(s - m)
            l = jnp.sum(e, axis=-1, keepdims=True)
            outs.append(jnp.dot(e.astype(BF16), src[kr, vc], preferred_element_type=F32) / l)
            lses.append(jnp.broadcast_to(m + jnp.log(l), (A_TILE, HEAD_DIM)))
        on_ref[gi, rows, :] = jnp.concatenate(outs, axis=-1)
        ln_ref[gi, rows, :] = jnp.concatenate(lses, axis=-1)

    for gi, ((_, dil), pa_ref) in enumerate(zip(A_GROUPS, (pa0_ref, pa1_ref, pa2_ref))):
        n_blocks = seq // dil // A_TILE
        if dil == 1:
            band_block(pa_ref.at[0], 0, True, gi, pl.ds(0, A_TILE))
            assert (n_blocks - 1) % _A_UNROLL == 0

            def body0(it, c, pa_ref=pa_ref, gi=gi):
                for u in range(_A_UNROLL):
                    l0 = pl.multiple_of((1 + it * _A_UNROLL + u) * A_TILE, A_TILE)
                    band_block(pa_ref.at[0], l0, False, gi, pl.ds(l0, A_TILE))
                return c
            lax.fori_loop(0, (n_blocks - 1) // _A_UNROLL, body0, 0)
        elif n_blocks > 1:
            for r in range(dil):
                band_block(pa_ref.at[0, r], 0, True, gi, pl.ds(r, A_TILE, stride=dil))

            def body1(i, c, pa_ref=pa_ref, gi=gi, dil=dil):
                l0 = pl.multiple_of(i * A_TILE, A_TILE)
                for r in range(dil):
                    band_block(pa_ref.at[0, r], l0, False, gi, pl.ds(l0 * dil + r, A_TILE, stride=dil))
                return c
            lax.fori_loop(1, n_blocks, body1, 0)
        else:
            per_iter = 4

            def body2(it, c, pa_ref=pa_ref, gi=gi, dil=dil):
                for u in range(per_iter):
                    r = it * per_iter + u
                    band_block(pa_ref.at[0, r], 0, True, gi, pl.ds(r, A_TILE, stride=dil))
                return c
            lax.fori_loop(0, dil // per_iter, body2, 0)

    chunk = T_TILE
    for c0 in range(0, seq, chunk):
        rows = slice(c0, c0 + chunk)
        lse = [ln_ref[g, rows, :] for g in range(len(A_GROUPS))]
        top = functools.reduce(jnp.maximum, lse)
        ex = [jnp.exp(v - top) for v in lse]
        den = functools.reduce(lambda a, b: a + b, ex)
        for g in range(len(A_GROUPS)):
            o_ref[0, rows, g * gw:(g + 1) * gw] = (ex[g] / den * on_ref[g, rows, :]).astype(BF16)


def _mixer_a(pa0, pa1, pa2, bias):
    b, seq, _ = pa0.shape
    return pl.pallas_call(
        functools.partial(_mixer_a_kernel, seq=seq),
        out_shape=jax.ShapeDtypeStruct((b, seq, _GRP_COLS), BF16),
        grid=(b,),
        in_specs=[pl.BlockSpec((1, seq, _GRP_COLS), lambda i: (i, 0, 0)),
                  pl.BlockSpec((1,) + pa1.shape[1:], lambda i: (i, 0, 0, 0)),
                  pl.BlockSpec((1,) + pa2.shape[1:], lambda i: (i, 0, 0, 0)),
                  pl.BlockSpec(bias.shape, lambda i: (0, 0, 0))],
        out_specs=pl.BlockSpec((1, seq, _GRP_COLS), lambda i: (i, 0, 0)),
        scratch_shapes=[pltpu.VMEM((len(A_GROUPS), seq, LANE), F32),
                        pltpu.VMEM((len(A_GROUPS), seq, LANE), F32)],
        compiler_params=_params(1),
        name="mixer_a",
    )(pa0, pa1, pa2, bias)


def _compress_kernel(r_ref, pek_ref, pev_ref, w1k_ref, w1v_ref, w2k_ref, w2v_ref, o_ref):
    half = NSA_CMP_STRIDE * HEAD_DIM
    for kv, (pe_ref, w1_ref, w2_ref) in enumerate(((pek_ref, w1k_ref, w2k_ref),
                                                   (pev_ref, w1v_ref, w2v_ref))):
        for h in range(B_KV_HEADS):
            r = r_ref[kv * B_KV_HEADS + h, 0].astype(F32)
            lo = jnp.dot((r + pe_ref[:, :half]).astype(BF16), w1_ref[:half, :],
                         preferred_element_type=F32)
            hi = jnp.dot((r + pe_ref[:, half:]).astype(BF16), w1_ref[half:, :],
                         preferred_element_type=F32)
            hid = lo + pltpu.roll(hi, hi.shape[0] - 1, axis=0)
            act = jax.nn.gelu(hid).astype(BF16)
            o_ref[0, kv * B_KV_HEADS + h] = jnp.dot(act, w2_ref[...],
                                                    preferred_element_type=F32).astype(BF16)


def _compress(r, pe_k, pe_v, w1_k, w1_v, w2_k, w2_v):
    _, b, m, c = r.shape
    const2 = lambda i: (0, 0)
    return pl.pallas_call(
        _compress_kernel,
        out_shape=jax.ShapeDtypeStruct((b, 4, m, HEAD_DIM), BF16),
        grid=(b,),
        in_specs=[pl.BlockSpec((4, 1, m, c), lambda i: (0, i, 0, 0)),
                  pl.BlockSpec(pe_k.shape, const2), pl.BlockSpec(pe_v.shape, const2),
                  pl.BlockSpec(w1_k.shape, const2), pl.BlockSpec(w1_v.shape, const2),
                  pl.BlockSpec(w2_k.shape, const2), pl.BlockSpec(w2_v.shape, const2)],
        out_specs=pl.BlockSpec((1, 4, m, HEAD_DIM), lambda i: (i, 0, 0, 0)),
        compiler_params=_params(1),
        name="nsa_compress",
    )(r, pe_k, pe_v, w1_k, w1_v, w2_k, w2_v)


MASK_BIG = 2.0 ** 100
SEL_PAD = 32


def _nt(a, b, **kw):
    return lax.dot_general(a, b, (((1,), (1,)), ((), ())), preferred_element_type=F32, **kw)


def _top_mask_t(score, n_rows, n_top):
    sub = lax.broadcasted_iota(jnp.int32, score.shape, 0)
    rank = jnp.zeros(score.shape, F32)
    for j in range(n_rows):
        row = score[j:j + 1, :]
        ge = jnp.where(row >= score, 1.0, 0.0)
        gt = jnp.where(row > score, 1.0, 0.0)
        rank = rank + jnp.where(sub > j, ge, gt)
    return rank < n_top


def _top_keep(score, n_rows, n_top, block):
    free = n_top * block
    ranked = jnp.where(_top_mask_t(score[:, free:], n_rows, n_top), 1.0, 0.0)
    return jnp.concatenate([jnp.ones((score.shape[0], free), F32), ranked], axis=1)


BAND_SLOTS = 4


def _loop_pairs(lo, hi, body):
    n = jnp.maximum(hi - lo, 0)

    def pair(p, c):
        body(lo + 2 * p)
        body(lo + 2 * p + 1)
        return c
    lax.fori_loop(0, n // 2, pair, 0)
    pl.when(n % 2 == 1)(lambda: body(hi - 1))


def _two_pass_attention(i, streams, band_streams=()):
    groups = ((streams, lambda j: j), (band_streams, lambda j: j % BAND_SLOTS))

    def key_rows(j):
        return pl.ds(pl.multiple_of(j * T_TILE, T_TILE), T_TILE)

    def logits(group, j, d):
        some, slot = group
        for k_tile, q_t, _, bias_tile, s_ref, rmp_ref, _ in some:
            s = jnp.dot(k_tile(key_rows(j)), q_t(), preferred_element_type=F32)
            if d is not None:
                s = s + bias_tile(d)
            s_ref[slot(j)] = s
            fold = jnp.max(s.reshape(s.shape[0] // 8, 8, s.shape[1]), axis=0)
            rmp_ref[...] = fold if d == 0 else jnp.maximum(rmp_ref[...], fold)

    def near(n_full, n_band):
        for d in range(n_full):
            logits(groups[0], i - d, d)
        for d in range(n_band):
            logits(groups[1], i - d, d)

    if band_streams:
        pl.when(i == 0)(lambda: near(1, 1))
        pl.when(i == 1)(lambda: near(2, 2))
        pl.when(i >= 2)(lambda: near(2, 3))
    else:
        logits(groups[0], i, 0)
        pl.when(i >= 1)(lambda: logits(groups[0], i - 1, 1))

    _loop_pairs(0, i - 1, lambda j: logits(groups[0], j, None))

    tops = {}
    for some, _ in groups:
        for stream in some:
            rmp_ref, acc_ref = stream[-2:]
            tops[id(rmp_ref)] = jnp.max(rmp_ref[...], axis=0, keepdims=True)
            acc_ref[...] = jnp.zeros(acc_ref.shape, F32)

    def pv(active):
        def body(j):
            for some, slot in active:
                for _, _, vt_ref, _, s_ref, rmp_ref, acc_ref in some:
                    e = jnp.exp(s_ref[slot(j)] - tops[id(rmp_ref)])
                    acc_ref[...] += jnp.dot(vt_ref[j], e.astype(BF16), preferred_element_type=F32)
        return body

    if band_streams:
        band_lo = jnp.maximum(i - 2, 0)
        _loop_pairs(0, band_lo, pv(groups[:1]))
        pv_all = pv(groups)

        def merged(j, c):
            pv_all(j)
            return c
        lax.fori_loop(band_lo, i + 1, merged, 0)
    else:
        _loop_pairs(0, i + 1, pv(groups[:1]))

    outs = []
    for some, _ in groups:
        for *_, acc_ref in some:
            acc = acc_ref[...]
            outs.append(acc[:HEAD_DIM, :] / acc[HEAD_DIM:HEAD_DIM + 1, :])
    return outs


def _eye(n, dtype=BF16):
    return (lax.broadcasted_iota(jnp.int32, (n, n), 0)
            == lax.broadcasted_iota(jnp.int32, (n, n), 1)).astype(dtype)


def _transposed_values(v_ref_rows, n_tiles):
    eye = _eye(LANE)
    for j in range(n_tiles):
        v = _with_ones_column(v_ref_rows(slice(j * T_TILE, (j + 1) * T_TILE)))
        yield j, _nt(eye, v).astype(BF16)


def _with_ones_column(v):
    one = (lax.broadcasted_iota(jnp.int32, v.shape, 1) == 0).astype(F32)
    return jnp.concatenate([v.astype(F32), one], axis=1).astype(BF16)


def _with_block_onehot(k, block):
    n = k.shape[0]
    shape = (n, SEL_PAD)
    hot = (lax.broadcasted_iota(jnp.int32, shape, 0) // block
           == lax.broadcasted_iota(jnp.int32, shape, 1)).astype(F32)
    return jnp.concatenate([k.astype(F32), hot, jnp.zeros(shape, F32)], axis=1).astype(BF16)


def _augment_q_t(q_t, allowed_t):
    pen = (allowed_t - 1.0) * MASK_BIG
    return jnp.concatenate([q_t, pen, jnp.zeros(pen.shape, F32)], axis=0).astype(BF16)


_N_SLC = 32
_G_ROWS = B_GROUP * T_TILE


def _nsa_kernel(pb_ref, cmp_ref, bg_ref, bias_ref, o_ref,
                kaug_ref, vst_ref, vwt_ref, eye_ref, sig_ref, q3t_ref, qaugt_ref, s_ref, sw_ref,
                rmp_ref, acc_ref, ocmp_ref, *, seq):
    n_tiles = seq // T_TILE
    n_cmp_rows = seq // NSA_CMP_STRIDE
    q_cols = B_HEADS * HEAD_DIM
    ks0, vs0, kw0, vw0 = (q_cols + i * B_KV_HEADS * HEAD_DIM for i in range(4))

    def head_cols(c0, h):
        return slice(c0 + h * HEAD_DIM, c0 + (h + 1) * HEAD_DIM)

    j_id = lax.broadcasted_iota(jnp.int32, (_N_SLC, n_cmp_rows), 0)
    c_id = lax.broadcasted_iota(jnp.int32, (_N_SLC, n_cmp_rows), 1)
    overlap_t = ((c_id * NSA_CMP_STRIDE < (j_id + 1) * NSA_SLC_BLOCK)
                 & (c_id * NSA_CMP_STRIDE + NSA_CMP_BLOCK > j_id * NSA_SLC_BLOCK)).astype(F32)
    eye_ref[...] = _eye(T_TILE)
    for h in range(B_KV_HEADS):
        kaug_ref[h] = _with_block_onehot(pb_ref[0, :, head_cols(ks0, h)], NSA_SLC_BLOCK)
        for j, vt in _transposed_values(lambda r, h=h: pb_ref[0, r, head_cols(vs0, h)], n_tiles):
            vst_ref[h, j] = vt
        for j, vt in _transposed_values(lambda r, h=h: pb_ref[0, r, head_cols(vw0, h)], n_tiles):
            vwt_ref[h, j] = vt

    def tiles(a):
        return [a[:, j * T_TILE:(j + 1) * T_TILE] for j in range(n_tiles)]

    pick = (lax.broadcasted_iota(jnp.int32, (SEL_PAD, _BG_COLS), 0)
            == lax.broadcasted_iota(jnp.int32, (SEL_PAD, _BG_COLS), 1)).astype(F32)
    sig_all = jax.nn.sigmoid(_nt(pick, bg_ref[0], precision=lax.Precision.HIGHEST))
    for j, t in enumerate(tiles(sig_all)):
        sig_ref[j] = t

    eye_h = _eye(HEAD_DIM)
    t_lane = lax.broadcasted_iota(jnp.int32, (n_cmp_rows, seq), 1)
    c_end = (lax.broadcasted_iota(jnp.int32, (n_cmp_rows, seq), 0) * NSA_CMP_STRIDE
             + NSA_CMP_BLOCK - 1)
    ok = (c_end <= t_lane) & (c_end < seq)
    ok_f = ok.astype(F32)
    j_sub = lax.broadcasted_iota(jnp.int32, (_N_SLC, seq), 0)
    jt = lax.broadcasted_iota(jnp.int32, (_N_SLC, seq), 1) // NSA_SLC_BLOCK
    forced = ((j_sub == 0) | (j_sub == jt) | (j_sub == jt - 1)).astype(F32)
    valid = j_sub <= jt

    for h in range(B_KV_HEADS):
        kc = cmp_ref[0, h]
        vct = _nt(eye_h, cmp_ref[0, B_KV_HEADS + h]).astype(BF16)
        q_ts, p_sum = [], None
        for g in range(B_GROUP):
            q_t = _nt(eye_h, pb_ref[0, :, head_cols(0, h * B_GROUP + g)]) * ATTN_SCALE
            q_ts.append(q_t)
            lc = jnp.where(ok, jnp.dot(kc, q_t.astype(BF16), preferred_element_type=F32), NEG_INF)
            ec = jnp.exp(lc - jnp.max(lc, axis=0, keepdims=True))
            pc = ec / jnp.sum(ec, axis=0, keepdims=True) * ok_f
            o_cmp = jnp.dot(vct, pc.astype(BF16), preferred_element_type=F32)
            for j, t in enumerate(tiles(o_cmp)):
                ocmp_ref[h, j, :, g * T_TILE:(g + 1) * T_TILE] = t
            p_sum = pc if p_sum is None else p_sum + pc
        imp_t = jnp.dot(overlap_t, p_sum, precision=lax.Precision.HIGHEST,
                        preferred_element_type=F32)
        score = jnp.where(valid, imp_t + FORCE_BONUS * forced, NEG_INF)
        sel_t = jnp.where(valid, _top_keep(score, _N_SLC, NSA_SLC_TOPK, NSA_SLC_BLOCK), 0.0)
        pen_tiles = tiles((sel_t - 1.0) * MASK_BIG)
        q_tiles = [tiles(q_t) for q_t in q_ts]
        for j in range(n_tiles):
            q3t = jnp.concatenate([q_tiles[g][j] for g in range(B_GROUP)], axis=1)
            pen = jnp.concatenate([pen_tiles[j]] * B_GROUP, axis=1)
            q3t_ref[h, j] = q3t.astype(BF16)
            qaugt_ref[h, j] = jnp.concatenate([q3t, pen, jnp.zeros(pen.shape, F32)],
                                              axis=0).astype(BF16)

    def tile_body(i, carry):
        rows = pl.ds(pl.multiple_of(i * T_TILE, T_TILE), T_TILE)

        outs = _two_pass_attention(
            i,
            [(lambda kr, h=h: kaug_ref[h, kr, :], lambda h=h: qaugt_ref[h, i], vst_ref.at[h],
              lambda d, h=h: bias_ref[h, d], s_ref.at[h], rmp_ref.at[0, h], acc_ref.at[0, h])
             for h in range(B_KV_HEADS)],
            [(lambda kr, h=h: pb_ref[0, kr, head_cols(kw0, h)], lambda h=h: q3t_ref[h, i],
              vwt_ref.at[h], lambda d, h=h: bias_ref[h, d], sw_ref.at[h], rmp_ref.at[1, h],
              acc_ref.at[1, h]) for h in range(B_KV_HEADS)])
        oslc, owin = outs[:B_KV_HEADS], outs[B_KV_HEADS:]

        sig_t = sig_ref[i]
        heads = []
        for h in range(B_KV_HEADS):
            for g in range(B_GROUP):
                head = h * B_GROUP + g
                gc = slice(g * T_TILE, (g + 1) * T_TILE)
                heads.append(sig_t[3 * head:3 * head + 1, :] * ocmp_ref[h, i, :, gc]
                             + sig_t[3 * head + 1:3 * head + 2, :] * oslc[h][:, gc]
                             + sig_t[3 * head + 2:3 * head + 3, :] * owin[h][:, gc])
        o_t = jnp.concatenate(heads, axis=0).astype(BF16)
        o_ref[0, rows, :] = _nt(eye_ref[...], o_t).astype(o_ref.dtype)
        return carry

    lax.fori_loop(0, n_tiles, tile_body, 0)


def _nsa(pb, cmp, bg, bias):
    b, seq, _ = pb.shape
    n_tiles = seq // T_TILE
    return pl.pallas_call(
        functools.partial(_nsa_kernel, seq=seq),
        out_shape=jax.ShapeDtypeStruct((b, seq, B_HEADS * HEAD_DIM), BF16),
        grid=(b,),
        in_specs=[pl.BlockSpec((1, seq, _PB_COLS), lambda i: (i, 0, 0)),
                  pl.BlockSpec((1,) + cmp.shape[1:], lambda i: (i, 0, 0, 0)),
                  pl.BlockSpec((1, seq, _BG_COLS), lambda i: (i, 0, 0)),
                  pl.BlockSpec(bias.shape, lambda i: (0, 0, 0, 0))],
        out_specs=pl.BlockSpec((1, seq, B_HEADS * HEAD_DIM), lambda i: (i, 0, 0)),
        scratch_shapes=[pltpu.VMEM((B_KV_HEADS, seq, LANE), BF16),
                        pltpu.VMEM((B_KV_HEADS, n_tiles, LANE, T_TILE), BF16),
                        pltpu.VMEM((B_KV_HEADS, n_tiles, LANE, T_TILE), BF16),
                        pltpu.VMEM((T_TILE, T_TILE), BF16),
                        pltpu.VMEM((n_tiles, SEL_PAD, T_TILE), F32),
                        pltpu.VMEM((B_KV_HEADS, n_tiles, HEAD_DIM, _G_ROWS), BF16),
                        pltpu.VMEM((B_KV_HEADS, n_tiles, LANE, _G_ROWS), BF16),
                        pltpu.VMEM((B_KV_HEADS, n_tiles, T_TILE, _G_ROWS), F32),
                        pltpu.VMEM((B_KV_HEADS, BAND_SLOTS, T_TILE, _G_ROWS), F32),
                        pltpu.VMEM((2, B_KV_HEADS, 8, _G_ROWS), F32),
                        pltpu.VMEM((2, B_KV_HEADS, LANE, _G_ROWS), F32),
                        pltpu.VMEM((B_KV_HEADS, n_tiles, HEAD_DIM, _G_ROWS), F32)],
        compiler_params=_params(1),
        name="nsa",
    )(pb, cmp, bg, bias)


def _moba_kernel(pc_ref, bias_ref, o_ref, kaug_ref, vt_ref, eye_ref, qaugt_ref, s_ref,
                 rmp_ref, acc_ref, *, seq):
    n_blk = seq // MOBA_BLOCK
    width = C_HEADS * HEAD_DIM
    eye_ref[...] = _eye(T_TILE)

    def cols(j, h):
        return slice(j * width + h * HEAD_DIM, j * width + (h + 1) * HEAD_DIM)

    n_sub = lax.broadcasted_iota(jnp.int32, (SEL_PAD, seq), 0)
    own = lax.broadcasted_iota(jnp.int32, (SEL_PAD, seq), 1) // MOBA_BLOCK
    past = n_sub < own
    eye_h = _eye(HEAD_DIM)
    for h in range(C_HEADS):
        kaug_ref[h] = _with_block_onehot(pc_ref[0, :, cols(1, h)], MOBA_BLOCK)
        for j, vt in _transposed_values(lambda r, h=h: pc_ref[0, r, cols(2, h)], n_blk):
            vt_ref[h, j] = vt
        kmean = jnp.concatenate(
            [jnp.mean(pc_ref[0, n * MOBA_BLOCK:(n + 1) * MOBA_BLOCK, cols(1, h)].astype(F32), axis=0,
                      keepdims=True) for n in range(n_blk)]
            + [jnp.zeros((SEL_PAD - n_blk, HEAD_DIM), F32)], axis=0)
        q_t = _nt(eye_h, pc_ref[0, :, cols(0, h)])
        gate_t = jnp.dot(kmean, q_t, precision=lax.Precision.HIGHEST,
                         preferred_element_type=F32)
        score = jnp.where(past, gate_t, NEG_INF)
        picked = past & (_top_keep(score, n_blk, MOBA_TOPK, MOBA_BLOCK) > 0.5)
        allowed_t = jnp.where(picked | (n_sub == own), 1.0, 0.0)
        qaug_t = _augment_q_t(q_t * ATTN_SCALE, allowed_t)
        for j in range(n_blk):
            qaugt_ref[h, j] = qaug_t[:, j * T_TILE:(j + 1) * T_TILE]

    def tile_body(i, carry):
        rows = pl.ds(pl.multiple_of(i * MOBA_BLOCK, MOBA_BLOCK), MOBA_BLOCK)
        outs = _two_pass_attention(
            i, [(lambda kr, h=h: kaug_ref[h, kr, :], lambda h=h: qaugt_ref[h, i], vt_ref.at[h],
                 lambda d, h=h: bias_ref[h, d], s_ref.at[h], rmp_ref.at[h], acc_ref.at[h])
                for h in range(C_HEADS)])
        o_t = jnp.concatenate(outs, axis=0).astype(BF16)
        o_ref[0, rows, :] = _nt(eye_ref[...], o_t).astype(o_ref.dtype)
        return carry

    lax.fori_loop(0, n_blk, tile_body, 0)


def _moba(pc, bias):
    b, seq, _ = pc.shape
    width = C_HEADS * HEAD_DIM
    n_tiles = seq // T_TILE
    return pl.pallas_call(
        functools.partial(_moba_kernel, seq=seq),
        out_shape=jax.ShapeDtypeStruct((b, seq, width), BF16),
        grid=(b,),
        in_specs=[pl.BlockSpec((1, seq, _C_COLS), lambda i: (i, 0, 0)),
                  pl.BlockSpec(bias.shape, lambda i: (0, 0, 0, 0))],
        out_specs=pl.BlockSpec((1, seq, width), lambda i: (i, 0, 0)),
        scratch_shapes=[pltpu.VMEM((C_HEADS, seq, LANE), BF16),
                        pltpu.VMEM((C_HEADS, n_tiles, LANE, T_TILE), BF16),
                        pltpu.VMEM((T_TILE, T_TILE), BF16),
                        pltpu.VMEM((C_HEADS, n_tiles, LANE, T_TILE), BF16),
                        pltpu.VMEM((C_HEADS, n_tiles, T_TILE, T_TILE), F32),
                        pltpu.VMEM((C_HEADS, 8, T_TILE), F32),
                        pltpu.VMEM((C_HEADS, LANE, T_TILE), F32)],
        compiler_params=_params(1),
        name="moba",
    )(pc, bias)


def _merge_kernel(oa_ref, ob_ref, oc_ref, mg_ref, x_ref, wb_ref, wo_ref, y_ref):
    r0 = A_HEADS * HEAD_DIM
    r1 = r0 + B_HEADS * HEAD_DIM
    ya = jnp.dot(oa_ref[...], wb_ref[0:r0, :], preferred_element_type=F32)
    yb = jnp.dot(ob_ref[...], wb_ref[r0:r1, :], preferred_element_type=F32)
    yc = jnp.dot(oc_ref[...], wb_ref[r1:, :], preferred_element_type=F32)
    merged = (jax.nn.sigmoid(mg_ref[:, 0:D_MODEL].astype(F32)) * ya
              + jax.nn.sigmoid(mg_ref[:, D_MODEL:2 * D_MODEL].astype(F32)) * yb
              + jax.nn.sigmoid(mg_ref[:, 2 * D_MODEL:].astype(F32)) * yc)
    y_ref[...] = x_ref[...] + jnp.dot(merged.astype(BF16), wo_ref[...], preferred_element_type=F32)


def _merge(oa, ob, oc, mg, x2, wb, wo):
    t = x2.shape[0]
    tm = ROW_TILE
    row = lambda i: (i, 0)
    return pl.pallas_call(
        _merge_kernel,
        out_shape=jax.ShapeDtypeStruct((t, D_MODEL), F32),
        grid=(t // tm,),
        in_specs=[pl.BlockSpec((tm, oa.shape[1]), row),
                  pl.BlockSpec((tm, ob.shape[1]), row), pl.BlockSpec((tm, oc.shape[1]), row),
                  pl.BlockSpec((tm, _MG_COLS), row), pl.BlockSpec((tm, D_MODEL), row),
                  pl.BlockSpec(wb.shape, lambda i: (0, 0)), pl.BlockSpec(wo.shape, lambda i: (0, 0))],
        out_specs=pl.BlockSpec((tm, D_MODEL), row),
        compiler_params=_params(1),
        name="merge",
    )(oa, ob, oc, mg, x2, wb, wo)


_FF_CHUNK = D_FF // 2


def _ffn_kernel(x_ref, halo_ref, g_ref, wu_ref, cw_ref, cb_ref, wd_ref, gf_ref, y_ref, xn_ref,
                *, tiles_per_seq, final_norm):
    i = pl.program_id(0)
    x = x_ref[...]
    g = g_ref[...]
    keep = (i % tiles_per_seq != 0).astype(F32)
    xn_ref[0:FFN_HALO, :] = (_rmsnorm_rows(halo_ref[...], g) * keep).astype(BF16)
    xn_ref[FFN_HALO:, :] = _rmsnorm_rows(x, g).astype(BF16)
    xn = xn_ref[...]
    rows = xn.shape[0]

    def conv(c0):
        hcol = jnp.dot(xn, wu_ref[:, c0:c0 + _FF_CHUNK], preferred_element_type=F32)
        out = (cw_ref[2:3, c0:c0 + _FF_CHUNK] * hcol
               + cw_ref[1:2, c0:c0 + _FF_CHUNK] * pltpu.roll(hcol, 1, axis=0)
               + cw_ref[0:1, c0:c0 + _FF_CHUNK] * pltpu.roll(hcol, 2, axis=0)
               + cb_ref[:, c0:c0 + _FF_CHUNK])
        return out[FFN_HALO:rows]

    acc = x
    for c in range(0, D_FF, _FF_CHUNK):
        a = conv(c)
        u = conv(D_FF + c)
        act = (a * jax.nn.sigmoid(a) * u).astype(BF16)
        acc = acc + jnp.dot(act, wd_ref[c:c + _FF_CHUNK, :], preferred_element_type=F32)
    if final_norm:
        acc = _rmsnorm_rows(acc, gf_ref[...])
    y_ref[...] = acc


def _ffn(x2, g, wu, cw, cb, wd, gf, *, seq, final_norm):
    t = x2.shape[0]
    tm = ROW_TILE
    const = lambda i: (0, 0)
    halo_blocks = tm // FFN_HALO
    return pl.pallas_call(
        functools.partial(_ffn_kernel, tiles_per_seq=seq // tm, final_norm=final_norm),
        out_shape=jax.ShapeDtypeStruct((t, D_MODEL), F32),
        grid=(t // tm,),
        in_specs=[pl.BlockSpec((tm, D_MODEL), lambda i: (i, 0)),
                  pl.BlockSpec((FFN_HALO, D_MODEL), lambda i: (jnp.maximum(i * halo_blocks - 1, 0), 0)),
                  pl.BlockSpec((1, D_MODEL), const),
                  pl.BlockSpec(wu.shape, const), pl.BlockSpec(cw.shape, const),
                  pl.BlockSpec(cb.shape, const), pl.BlockSpec(wd.shape, const),
                  pl.BlockSpec((1, D_MODEL), const)],
        out_specs=pl.BlockSpec((tm, D_MODEL), lambda i: (i, 0)),
        scratch_shapes=[pltpu.VMEM((tm + FFN_HALO, D_MODEL), BF16)],
        compiler_params=_params(1),
        name="conv_ffn",
    )(x2, x2, g, wu, cw, cb, wd, gf)


def kernel(x, rel_bias, norm_mix, w_in, cmp_pe_k, cmp_w1_k, cmp_w2_k, cmp_pe_v, cmp_w1_v, cmp_w2_v,
           w_branch, w_out, norm_ffn, w_up, conv_w, conv_b, w_down, norm_final):
    b, s, d = x.shape
    depth = w_in.shape[0]
    t = b * s
    assert d == D_MODEL and s % T_TILE == 0 and s // NSA_SLC_BLOCK == _N_SLC and t % ROW_TILE == 0
    assert s % ROW_TILE == 0 and all(s % (dil * A_TILE) == 0 for _, dil in A_GROUPS)
    assert all(win // dil == A_TILE for win, dil in A_GROUPS) and A_GROUPS[0][1] == 1
    assert all(ROW_TILE % dil == 0 for _, dil in A_GROUPS)

    hg = A_HEADS_PER_GROUP
    bias_a = jnp.concatenate([
        _bias_tiles(rel_bias,
                    np.concatenate([_bucket_tile(A_TILE, 1, dil, win // dil),
                                    _bucket_tile(A_TILE, 0, dil, win // dil)], axis=1),
                    gi * hg, hg)
        for gi, (win, dil) in enumerate(A_GROUPS)], axis=0)
    assert (_bucket_tile(T_TILE, 2, 1, 3 * T_TILE) == REL_BUCKETS - 1).all()
    bias_b = _bias_tiles_t(rel_bias, [_bucket_tile(T_TILE, d, 1, NSA_WINDOW - 1) for d in (0, 1, 2)],
                           A_HEADS, B_HEADS)
    bias_b = bias_b.reshape(B_KV_HEADS, B_GROUP, 3, T_TILE, T_TILE).transpose(0, 2, 3, 1, 4).reshape(
        B_KV_HEADS, 3, T_TILE, _G_ROWS)
    bias_c = _bias_tiles_t(rel_bias, [_bucket_tile(T_TILE, d, 1, s) for d in (0, 1)],
                           A_HEADS + B_HEADS, C_HEADS)

    w_in_k = _prep_w_in(w_in)
    x2 = x.reshape(t, d)
    for i in range(depth):
        pa0, pa1, pa2, pb, pcmp, pc, mg, bg = _proj_in(x2, norm_mix[i][None, :], w_in_k, layer=i,
                                                       seq=s)

        oa = _mixer_a(pa0.reshape(b, s, _GRP_COLS), pa1, pa2, bias_a)

        cmp = _compress(pcmp.reshape(4, b, s // NSA_CMP_STRIDE, NSA_CMP_STRIDE * HEAD_DIM),
                        cmp_pe_k[i].reshape(1, -1), cmp_pe_v[i].reshape(1, -1),
                        cmp_w1_k[i].astype(BF16), cmp_w1_v[i].astype(BF16),
                        cmp_w2_k[i].astype(BF16), cmp_w2_v[i].astype(BF16))
        ob = _nsa(pb.reshape(b, s, _PB_COLS), cmp, bg.reshape(b, s, _BG_COLS), bias_b)

        oc = _moba(pc.reshape(b, s, _C_COLS), bias_c)

        x2 = _merge(oa.reshape(t, -1), ob.reshape(t, -1), oc.reshape(t, -1), mg, x2,
                    w_branch[i].astype(BF16), w_out[i].astype(BF16))
        x2 = _ffn(x2, norm_ffn[i][None, :], w_up[i].astype(BF16), conv_w[i], conv_b[i][None, :],
                  w_down[i].astype(BF16), norm_final[None, :], seq=s, final_norm=(i == depth - 1))
    return x2.reshape(b, s, d)
```

```python
import functools
import math

import jax
import jax.numpy as jnp
import numpy as np
from jax import lax
from jax.experimental import pallas as pl
from jax.experimental.pallas import tpu as pltpu

F32 = jnp.float32
BF16 = jnp.bfloat16

D_MODEL = 1024
HEAD_DIM = 64
A_GROUPS = ((128, 1), (512, 4), (2048, 16))
A_HEADS_PER_GROUP = 2
A_HEADS = 6
B_HEADS = 6
B_KV_HEADS = 2
B_GROUP = 3
C_HEADS = 4
NSA_CMP_BLOCK = 32
NSA_CMP_STRIDE = 16
NSA_CMP_HIDDEN = 256
NSA_SLC_BLOCK = 64
NSA_SLC_TOPK = 16
NSA_WINDOW = 512
MOBA_BLOCK = 256
MOBA_TOPK = 3
REL_BUCKETS = 32
REL_MAX_DIST = 128
D_FF = 2816
RMS_EPS = 1e-6
NEG_INF = -1e30
FORCE_BONUS = 1e4
ATTN_SCALE = HEAD_DIM ** -0.5

LANE = 128
A_TILE = 128
T_TILE = 256
ROW_TILE = 512
FFN_HALO = 8
VMEM_LIMIT = 56 * 1024 * 1024

_GRP_COLS = 3 * A_HEADS_PER_GROUP * HEAD_DIM
_A_COLS = len(A_GROUPS) * _GRP_COLS
_PB_COLS = B_HEADS * HEAD_DIM + 4 * B_KV_HEADS * HEAD_DIM
_CMP_COLS = 2 * B_KV_HEADS * HEAD_DIM
_C_COLS = 3 * C_HEADS * HEAD_DIM
_MG_COLS = 3 * D_MODEL
_BG_COLS = LANE
_OFF_A = 0
_OFF_PB = _OFF_A + _A_COLS
_OFF_CMP = _OFF_PB + _PB_COLS
_OFF_C = _OFF_CMP + _CMP_COLS
_OFF_MG = _OFF_C + _C_COLS
_OFF_BG = _OFF_MG + _MG_COLS
_W_COLS = _OFF_BG + _BG_COLS


def _params(n_grid):
    return pltpu.CompilerParams(dimension_semantics=("arbitrary",) * n_grid,
                                vmem_limit_bytes=VMEM_LIMIT)


def _rel_bucket_np(dist):
    n = np.maximum(dist, 0)
    exact = REL_BUCKETS // 2
    nf = np.maximum(n, 1).astype(np.float32)
    large = exact + (np.log(nf / np.float32(exact)) / np.float32(math.log(REL_MAX_DIST / exact))
                     * np.float32(REL_BUCKETS - exact)).astype(np.int32)
    return np.where(n < exact, n, np.minimum(large, REL_BUCKETS - 1)).astype(np.int32)


def _bucket_tile(tile, block_offset, dil, max_rel):
    rel = block_offset * tile + np.arange(tile)[:, None] - np.arange(tile)[None, :]
    ok = (rel >= 0) & (rel <= max_rel)
    return np.where(ok, _rel_bucket_np(rel * dil), -1).astype(np.int32)


def _bias_tiles_kernel(tbl_ref, idx_ref, o_ref, *, head0, relative):
    h = pl.program_id(0) + head0
    idx = idx_ref[...]
    acc = jnp.full(idx.shape, NEG_INF, F32)
    for b in range(REL_BUCKETS):
        acc = jnp.where(idx == b, tbl_ref[b, h], acc)
    if relative:
        acc = acc - tbl_ref[REL_BUCKETS - 1, h]
    o_ref[0] = acc


def _bias_tiles(rel_bias, idx, head0, n_heads, relative=False):
    r, c = idx.shape
    return pl.pallas_call(
        functools.partial(_bias_tiles_kernel, head0=head0, relative=relative),
        out_shape=jax.ShapeDtypeStruct((n_heads, r, c), F32),
        grid=(n_heads,),
        in_specs=[pl.BlockSpec(memory_space=pltpu.SMEM),
                  pl.BlockSpec((r, c), lambda h: (0, 0))],
        out_specs=pl.BlockSpec((1, r, c), lambda h: (h, 0, 0)),
        compiler_params=_params(1),
        name="bias_tiles",
    )(rel_bias, jnp.asarray(idx))


def _bias_tiles_t(rel_bias, idx_tiles, head0, n_heads):
    r, c = idx_tiles[0].shape
    flat = _bias_tiles(rel_bias, np.concatenate([t.T for t in idx_tiles], axis=1), head0, n_heads,
                       relative=True)
    return flat.reshape(n_heads, c, len(idx_tiles), r).transpose(0, 2, 1, 3)


def _rmsnorm_rows(x, g):
    return x * lax.rsqrt(jnp.mean(x * x, axis=-1, keepdims=True) + RMS_EPS) * g


def _proj_in_kernel(x_ref, g_ref, w_ref, pa0_ref, pa1_ref, pa2_ref, pb_ref, pcmp_ref, pc_ref,
                    mg_ref, bg_ref, regroup_ref):
    xn = _rmsnorm_rows(x_ref[...], g_ref[...]).astype(BF16)
    tm = xn.shape[0]

    def mm(c0, c1):
        return jnp.dot(xn, w_ref[:, c0:c1], preferred_element_type=F32)

    pa0_ref[...] = mm(_OFF_A, _OFF_A + _GRP_COLS).astype(BF16)
    for gi, out_ref in ((1, pa1_ref), (2, pa2_ref)):
        dil = A_GROUPS[gi][1]
        res = mm(_OFF_A + gi * _GRP_COLS, _OFF_A + (gi + 1) * _GRP_COLS)
        for j in range(3):
            regroup_ref[j] = res[:, j * LANE:(j + 1) * LANE]
        for r in range(dil):
            for j in range(3):
                out_ref[0, r, :, j * LANE:(j + 1) * LANE] = (
                    regroup_ref[j, pl.ds(r, tm // dil, stride=dil), :].astype(BF16))
    pb_ref[:, 0:384] = mm(_OFF_PB, _OFF_PB + 384).astype(BF16)
    pb_ref[:, 384:896] = mm(_OFF_PB + 384, _OFF_PB + 896).astype(BF16)
    cmp = mm(_OFF_CMP, _OFF_CMP + _CMP_COLS)
    for kv in range(2):
        regroup_ref[kv] = cmp[:, kv * LANE:(kv + 1) * LANE]
    for l in range(NSA_CMP_STRIDE):
        for kv in range(2):
            a = regroup_ref[kv, pl.ds(l, tm // NSA_CMP_STRIDE, stride=NSA_CMP_STRIDE), :].astype(BF16)
            for h in range(B_KV_HEADS):
                pcmp_ref[kv * B_KV_HEADS + h, :, l * HEAD_DIM:(l + 1) * HEAD_DIM] = (
                    a[:, h * HEAD_DIM:(h + 1) * HEAD_DIM])
    for c in range(0, _C_COLS, 384):
        pc_ref[:, c:c + 384] = mm(_OFF_C + c, _OFF_C + c + 384).astype(BF16)
    for c in range(0, _MG_COLS, 512):
        mg_ref[:, c:c + 512] = mm(_OFF_MG + c, _OFF_MG + c + 512).astype(BF16)
    bg_ref[...] = mm(_OFF_BG, _OFF_BG + _BG_COLS)


def _proj_in(x2, g, w, *, layer, seq):
    t = x2.shape[0]
    tm = ROW_TILE
    b = t // seq
    tps = seq // tm
    row = lambda i: (i, 0)
    d1, d2 = A_GROUPS[1][1], A_GROUPS[2][1]
    sub = lambda i: (i // tps, 0, i % tps, 0)
    return pl.pallas_call(
        _proj_in_kernel,
        out_shape=(jax.ShapeDtypeStruct((t, _GRP_COLS), BF16),
                   jax.ShapeDtypeStruct((b, d1, seq // d1, _GRP_COLS), BF16),
                   jax.ShapeDtypeStruct((b, d2, seq // d2, _GRP_COLS), BF16),
                   jax.ShapeDtypeStruct((t, _PB_COLS), BF16),
                   jax.ShapeDtypeStruct((4, t // NSA_CMP_STRIDE, NSA_CMP_STRIDE * HEAD_DIM), BF16),
                   jax.ShapeDtypeStruct((t, _C_COLS), BF16),
                   jax.ShapeDtypeStruct((t, _MG_COLS), BF16),
                   jax.ShapeDtypeStruct((t, _BG_COLS), F32)),
        grid=(t // tm,),
        in_specs=[pl.BlockSpec((tm, D_MODEL), row),
                  pl.BlockSpec((1, D_MODEL), lambda i: (0, 0)),
                  pl.BlockSpec((None, D_MODEL, _W_COLS), lambda i: (layer, 0, 0))],
        out_specs=(pl.BlockSpec((tm, _GRP_COLS), row),
                   pl.BlockSpec((1, d1, tm // d1, _GRP_COLS), sub),
                   pl.BlockSpec((1, d2, tm // d2, _GRP_COLS), sub),
                   pl.BlockSpec((tm, _PB_COLS), row),
                   pl.BlockSpec((4, tm // NSA_CMP_STRIDE, NSA_CMP_STRIDE * HEAD_DIM),
                                lambda i: (0, i, 0)),
                   pl.BlockSpec((tm, _C_COLS), row),
                   pl.BlockSpec((tm, _MG_COLS), row),
                   pl.BlockSpec((tm, _BG_COLS), row)),
        scratch_shapes=[pltpu.VMEM((3, tm, LANE), F32)],
        compiler_params=_params(1),
        name="proj_in",
    )(x2, g, w)


def _w_in_segments():
    a = A_HEADS * HEAD_DIM
    bq0 = 3 * a
    bkc0 = bq0 + B_HEADS * HEAD_DIM
    bks0 = bkc0 + 2 * B_KV_HEADS * HEAD_DIM
    bg0 = bks0 + 4 * B_KV_HEADS * HEAD_DIM
    c0 = bg0 + B_HEADS * 3
    mg0 = c0 + _C_COLS
    gw = A_HEADS_PER_GROUP * HEAD_DIM
    segs = [(_OFF_A + (gi * 3 + j) * gw, j * a + gi * gw, gw)
            for gi in range(len(A_GROUPS)) for j in range(3)]
    segs += [(_OFF_PB, bq0, B_HEADS * HEAD_DIM),
             (_OFF_PB + B_HEADS * HEAD_DIM, bks0, bg0 - bks0),
             (_OFF_CMP, bkc0, _CMP_COLS), (_OFF_C, c0, _C_COLS), (_OFF_MG, mg0, _MG_COLS),
             (_OFF_BG, bg0, B_HEADS * 3)]
    return segs


def _prep_w_in_kernel(w_ref, o_ref):
    o_ref[0, :, _OFF_BG:] = jnp.zeros((o_ref.shape[1], _BG_COLS), BF16)
    for dst, src, width in _w_in_segments():
        for c in range(0, width, 768):
            n = min(768, width - c)
            o_ref[0, :, dst + c:dst + c + n] = w_ref[0, :, src + c:src + c + n].astype(BF16)


def _prep_w_in(w):
    depth, rows, cols = w.shape
    rb = 128
    return pl.pallas_call(
        _prep_w_in_kernel,
        out_shape=jax.ShapeDtypeStruct((depth, rows, _W_COLS), BF16),
        grid=(depth, rows // rb),
        in_specs=[pl.BlockSpec((1, rb, cols), lambda l, r: (l, r, 0))],
        out_specs=pl.BlockSpec((1, rb, _W_COLS), lambda l, r: (l, r, 0)),
        compiler_params=_params(2),
        name="prep_w_in",
    )(w)


_A_UNROLL = 3


def _mixer_a_kernel(pa0_ref, pa1_ref, pa2_ref, bias_ref, o_ref, on_ref, ln_ref, *, seq):
    gw = A_HEADS_PER_GROUP * HEAD_DIM

    def band_block(src, l0, first, gi, rows):
        outs, lses = [], []
        for h in range(A_HEADS_PER_GROUP):
            qc, kc, vc = (slice(j * gw + h * HEAD_DIM, j * gw + (h + 1) * HEAD_DIM) for j in range(3))
            q = src[pl.ds(l0, A_TILE), qc] * ATTN_SCALE
            if first:
                kr = pl.ds(l0, A_TILE)
                bias = bias_ref[A_HEADS_PER_GROUP * gi + h, :, A_TILE:]
            else:
                kr = pl.ds(l0 - A_TILE, 2 * A_TILE)
                bias = bias_ref[A_HEADS_PER_GROUP * gi + h]
            s = _nt(q, src[kr, kc]) + bias
            m = jnp.max(s, axis=-1, keepdims=True)
            e = jnp.exp(s - m)
            l = jnp.sum(e, axis=-1, keepdims=True)
            outs.append(jnp.dot(e.astype(BF16), src[kr, vc], preferred_element_type=F32) / l)
            lses.append(jnp.broadcast_to(m + jnp.log(l), (A_TILE, HEAD_DIM)))
        on_ref[gi, rows, :] = jnp.concatenate(outs, axis=-1)
        ln_ref[gi, rows, :] = jnp.concatenate(lses, axis=-1)

    for gi, ((_, dil), pa_ref) in enumerate(zip(A_GROUPS, (pa0_ref, pa1_ref, pa2_ref))):
        n_blocks = seq // dil // A_TILE
        if dil == 1:
            band_block(pa_ref.at[0], 0, True, gi, pl.ds(0, A_TILE))
            assert (n_blocks - 1) % _A_UNROLL == 0

            def body0(it, c, pa_ref=pa_ref, gi=gi):
                for u in range(_A_UNROLL):
                    l0 = pl.multiple_of((1 + it * _A_UNROLL + u) * A_TILE, A_TILE)
                    band_block(pa_ref.at[0], l0, False, gi, pl.ds(l0, A_TILE))
                return c
            lax.fori_loop(0, (n_blocks - 1) // _A_UNROLL, body0, 0)
        elif n_blocks > 1:
            for r in range(dil):
                band_block(pa_ref.at[0, r], 0, True, gi, pl.ds(r, A_TILE, stride=dil))

            def body1(i, c, pa_ref=pa_ref, gi=gi, dil=dil):
                l0 = pl.multiple_of(i * A_TILE, A_TILE)
                for r in range(dil):
                    band_block(pa_ref.at[0, r], l0, False, gi, pl.ds(l0 * dil + r, A_TILE, stride=dil))
                return c
            lax.fori_loop(1, n_blocks, body1, 0)
        else:
            per_iter = 4

            def body2(it, c, pa_ref=pa_ref, gi=gi, dil=dil):
                for u in range(per_iter):
                    r = it * per_iter + u
                    band_block(pa_ref.at[0, r], 0, True, gi, pl.ds(r, A_TILE, stride=dil))
                return c
            lax.fori_loop(0, dil // per_iter, body2, 0)

    chunk = T_TILE
    for c0 in range(0, seq, chunk):
        rows = slice(c0, c0 + chunk)
        lse = [ln_ref[g, rows, :] for g in range(len(A_GROUPS))]
        top = functools.reduce(jnp.maximum, lse)
        ex = [jnp.exp(v - top) for v in lse]
        den = functools.reduce(lambda a, b: a + b, ex)
        for g in range(len(A_GROUPS)):
            o_ref[0, rows, g * gw:(g + 1) * gw] = (ex[g] / den * on_ref[g, rows, :]).astype(BF16)


def _mixer_a(pa0, pa1, pa2, bias):
    b, seq, _ = pa0.shape
    return pl.pallas_call(
        functools.partial(_mixer_a_kernel, seq=seq),
        out_shape=jax.ShapeDtypeStruct((b, seq, _GRP_COLS), BF16),
        grid=(b,),
        in_specs=[pl.BlockSpec((1, seq, _GRP_COLS), lambda i: (i, 0, 0)),
                  pl.BlockSpec((1,) + pa1.shape[1:], lambda i: (i, 0, 0, 0)),
                  pl.BlockSpec((1,) + pa2.shape[1:], lambda i: (i, 0, 0, 0)),
                  pl.BlockSpec(bias.shape, lambda i: (0, 0, 0))],
        out_specs=pl.BlockSpec((1, seq, _GRP_COLS), lambda i: (i, 0, 0)),
        scratch_shapes=[pltpu.VMEM((len(A_GROUPS), seq, LANE), F32),
                        pltpu.VMEM((len(A_GROUPS), seq, LANE), F32)],
        compiler_params=_params(1),
        name="mixer_a",
    )(pa0, pa1, pa2, bias)


def _compress_kernel(r_ref, pek_ref, pev_ref, w1k_ref, w1v_ref, w2k_ref, w2v_ref, o_ref):
    half = NSA_CMP_STRIDE * HEAD_DIM
    for kv, (pe_ref, w1_ref, w2_ref) in enumerate(((pek_ref, w1k_ref, w2k_ref),
                                                   (pev_ref, w1v_ref, w2v_ref))):
        for h in range(B_KV_HEADS):
            r = r_ref[kv * B_KV_HEADS + h, 0].astype(F32)
            lo = jnp.dot((r + pe_ref[:, :half]).astype(BF16), w1_ref[:half, :],
                         preferred_element_type=F32)
            hi = jnp.dot((r + pe_ref[:, half:]).astype(BF16), w1_ref[half:, :],
                         preferred_element_type=F32)
            hid = lo + pltpu.roll(hi, hi.shape[0] - 1, axis=0)
            act = jax.nn.gelu(hid).astype(BF16)
            o_ref[0, kv * B_KV_HEADS + h] = jnp.dot(act, w2_ref[...],
                                                    preferred_element_type=F32).astype(BF16)


def _compress(r, pe_k, pe_v, w1_k, w1_v, w2_k, w2_v):
    _, b, m, c = r.shape
    const2 = lambda i: (0, 0)
    return pl.pallas_call(
        _compress_kernel,
        out_shape=jax.ShapeDtypeStruct((b, 4, m, HEAD_DIM), BF16),
        grid=(b,),
        in_specs=[pl.BlockSpec((4, 1, m, c), lambda i: (0, i, 0, 0)),
                  pl.BlockSpec(pe_k.shape, const2), pl.BlockSpec(pe_v.shape, const2),
                  pl.BlockSpec(w1_k.shape, const2), pl.BlockSpec(w1_v.shape, const2),
                  pl.BlockSpec(w2_k.shape, const2), pl.BlockSpec(w2_v.shape, const2)],
        out_specs=pl.BlockSpec((1, 4, m, HEAD_DIM), lambda i: (i, 0, 0, 0)),
        compiler_params=_params(1),
        name="nsa_compress",
    )(r, pe_k, pe_v, w1_k, w1_v, w2_k, w2_v)


MASK_BIG = 2.0 ** 100
SEL_PAD = 32


def _nt(a, b, **kw):
    return lax.dot_general(a, b, (((1,), (1,)), ((), ())), preferred_element_type=F32, **kw)


def _top_mask_t(score, n_rows, n_top):
    sub = lax.broadcasted_iota(jnp.int32, score.shape, 0)
    rank = jnp.zeros(score.shape, F32)
    for j in range(n_rows):
        row = score[j:j + 1, :]
        ge = jnp.where(row >= score, 1.0, 0.0)
        gt = jnp.where(row > score, 1.0, 0.0)
        rank = rank + jnp.where(sub > j, ge, gt)
    return rank < n_top


def _top_keep(score, n_rows, n_top, block):
    free = n_top * block
    ranked = jnp.where(_top_mask_t(score[:, free:], n_rows, n_top), 1.0, 0.0)
    return jnp.concatenate([jnp.ones((score.shape[0], free), F32), ranked], axis=1)


BAND_SLOTS = 4


def _loop_pairs(lo, hi, body):
    n = jnp.maximum(hi - lo, 0)

    def pair(p, c):
        body(lo + 2 * p)
        body(lo + 2 * p + 1)
        return c
    lax.fori_loop(0, n // 2, pair, 0)
    pl.when(n % 2 == 1)(lambda: body(hi - 1))


def _two_pass_attention(i, streams, band_streams=()):
    groups = ((streams, lambda j: j), (band_streams, lambda j: j % BAND_SLOTS))

    def key_rows(j):
        return pl.ds(pl.multiple_of(j * T_TILE, T_TILE), T_TILE)

    def logits(group, j, d):
        some, slot = group
        for k_tile, q_t, _, bias_tile, s_ref, rmp_ref, _ in some:
            s = jnp.dot(k_tile(key_rows(j)), q_t(), preferred_element_type=F32)
            if d is not None:
                s = s + bias_tile(d)
            s_ref[slot(j)] = s
            fold = jnp.max(s.reshape(s.shape[0] // 8, 8, s.shape[1]), axis=0)
            rmp_ref[...] = fold if d == 0 else jnp.maximum(rmp_ref[...], fold)

    def near(n_full, n_band):
        for d in range(n_full):
            logits(groups[0], i - d, d)
        for d in range(n_band):
            logits(groups[1], i - d, d)

    if band_streams:
        pl.when(i == 0)(lambda: near(1, 1))
        pl.when(i == 1)(lambda: near(2, 2))
        pl.when(i >= 2)(lambda: near(2, 3))
    else:
        logits(groups[0], i, 0)
        pl.when(i >= 1)(lambda: logits(groups[0], i - 1, 1))

    _loop_pairs(0, i - 1, lambda j: logits(groups[0], j, None))

    tops = {}
    for some, _ in groups:
        for stream in some:
            rmp_ref, acc_ref = stream[-2:]
            tops[id(rmp_ref)] = jnp.max(rmp_ref[...], axis=0, keepdims=True)
            acc_ref[...] = jnp.zeros(acc_ref.shape, F32)

    def pv(active):
        def body(j):
            for some, slot in active:
                for _, _, vt_ref, _, s_ref, rmp_ref, acc_ref in some:
                    e = jnp.exp(s_ref[slot(j)] - tops[id(rmp_ref)])
                    acc_ref[...] += jnp.dot(vt_ref[j], e.astype(BF16), preferred_element_type=F32)
        return body

    if band_streams:
        band_lo = jnp.maximum(i - 2, 0)
        _loop_pairs(0, band_lo, pv(groups[:1]))
        pv_all = pv(groups)

        def merged(j, c):
            pv_all(j)
            return c
        lax.fori_loop(band_lo, i + 1, merged, 0)
    else:
        _loop_pairs(0, i + 1, pv(groups[:1]))

    outs = []
    for some, _ in groups:
        for *_, acc_ref in some:
            acc = acc_ref[...]
            outs.append(acc[:HEAD_DIM, :] / acc[HEAD_DIM:HEAD_DIM + 1, :])
    return outs


def _eye(n, dtype=BF16):
    return (lax.broadcasted_iota(jnp.int32, (n, n), 0)
            == lax.broadcasted_iota(jnp.int32, (n, n), 1)).astype(dtype)


def _transposed_values(v_ref_rows, n_tiles):
    eye = _eye(LANE)
    for j in range(n_tiles):
        v = _with_ones_column(v_ref_rows(slice(j * T_TILE, (j + 1) * T_TILE)))
        yield j, _nt(eye, v).astype(BF16)


def _with_ones_column(v):
    one = (lax.broadcasted_iota(jnp.int32, v.shape, 1) == 0).astype(F32)
    return jnp.concatenate([v.astype(F32), one], axis=1).astype(BF16)


def _with_block_onehot(k, block):
    n = k.shape[0]
    shape = (n, SEL_PAD)
    hot = (lax.broadcasted_iota(jnp.int32, shape, 0) // block
           == lax.broadcasted_iota(jnp.int32, shape, 1)).astype(F32)
    return jnp.concatenate([k.astype(F32), hot, jnp.zeros(shape, F32)], axis=1).astype(BF16)


def _augment_q_t(q_t, allowed_t):
    pen = (allowed_t - 1.0) * MASK_BIG
    return jnp.concatenate([q_t, pen, jnp.zeros(pen.shape, F32)], axis=0).astype(BF16)


_N_SLC = 32
_G_ROWS = B_GROUP * T_TILE


def _nsa_kernel(pb_ref, cmp_ref, bg_ref, bias_ref, o_ref,
                kaug_ref, vst_ref, vwt_ref, eye_ref, sig_ref, q3t_ref, qaugt_ref, s_ref, sw_ref,
                rmp_ref, acc_ref, ocmp_ref, *, seq):
    n_tiles = seq // T_TILE
    n_cmp_rows = seq // NSA_CMP_STRIDE
    q_cols = B_HEADS * HEAD_DIM
    ks0, vs0, kw0, vw0 = (q_cols + i * B_KV_HEADS * HEAD_DIM for i in range(4))

    def head_cols(c0, h):
        return slice(c0 + h * HEAD_DIM, c0 + (h + 1) * HEAD_DIM)

    j_id = lax.broadcasted_iota(jnp.int32, (_N_SLC, n_cmp_rows), 0)
    c_id = lax.broadcasted_iota(jnp.int32, (_N_SLC, n_cmp_rows), 1)
    overlap_t = ((c_id * NSA_CMP_STRIDE < (j_id + 1) * NSA_SLC_BLOCK)
                 & (c_id * NSA_CMP_STRIDE + NSA_CMP_BLOCK > j_id * NSA_SLC_BLOCK)).astype(F32)
    eye_ref[...] = _eye(T_TILE)
    for h in range(B_KV_HEADS):
        kaug_ref[h] = _with_block_onehot(pb_ref[0, :, head_cols(ks0, h)], NSA_SLC_BLOCK)
        for j, vt in _transposed_values(lambda r, h=h: pb_ref[0, r, head_cols(vs0, h)], n_tiles):
            vst_ref[h, j] = vt
        for j, vt in _transposed_values(lambda r, h=h: pb_ref[0, r, head_cols(vw0, h)], n_tiles):
            vwt_ref[h, j] = vt

    def tiles(a):
        return [a[:, j * T_TILE:(j + 1) * T_TILE] for j in range(n_tiles)]

    pick = (lax.broadcasted_iota(jnp.int32, (SEL_PAD, _BG_COLS), 0)
            == lax.broadcasted_iota(jnp.int32, (SEL_PAD, _BG_COLS), 1)).astype(F32)
    sig_all = jax.nn.sigmoid(_nt(pick, bg_ref[0], precision=lax.Precision.HIGHEST))
    for j, t in enumerate(tiles(sig_all)):
        sig_ref[j] = t

    eye_h = _eye(HEAD_DIM)
    t_lane = lax.broadcasted_iota(jnp.int32, (n_cmp_rows, seq), 1)
    c_end = (lax.broadcasted_iota(jnp.int32, (n_cmp_rows, seq), 0) * NSA_CMP_STRIDE
             + NSA_CMP_BLOCK - 1)
    ok = (c_end <= t_lane) & (c_end < seq)
    ok_f = ok.astype(F32)
    j_sub = lax.broadcasted_iota(jnp.int32, (_N_SLC, seq), 0)
    jt = lax.broadcasted_iota(jnp.int32, (_N_SLC, seq), 1) // NSA_SLC_BLOCK
    forced = ((j_sub == 0) | (j_sub == jt) | (j_sub == jt - 1)).astype(F32)
    valid = j_sub <= jt

    for h in range(B_KV_HEADS):
        kc = cmp_ref[0, h]
        vct = _nt(eye_h, cmp_ref[0, B_KV_HEADS + h]).astype(BF16)
        q_ts, p_sum = [], None
        for g in range(B_GROUP):
            q_t = _nt(eye_h, pb_ref[0, :, head_cols(0, h * B_GROUP + g)]) * ATTN_SCALE
            q_ts.append(q_t)
            lc = jnp.where(ok, jnp.dot(kc, q_t.astype(BF16), preferred_element_type=F32), NEG_INF)
            ec = jnp.exp(lc - jnp.max(lc, axis=0, keepdims=True))
            pc = ec / jnp.sum(ec, axis=0, keepdims=True) * ok_f
            o_cmp = jnp.dot(vct, pc.astype(BF16), preferred_element_type=F32)
            for j, t in enumerate(tiles(o_cmp)):
                ocmp_ref[h, j, :, g * T_TILE:(g + 1) * T_TILE] = t
            p_sum = pc if p_sum is None else p_sum + pc
        imp_t = jnp.dot(overlap_t, p_sum, precision=lax.Precision.HIGHEST,
                        preferred_element_type=F32)
        score = jnp.where(valid, imp_t + FORCE_BONUS * forced, NEG_INF)
        sel_t = jnp.where(valid, _top_keep(score, _N_SLC, NSA_SLC_TOPK, NSA_SLC_BLOCK), 0.0)
        pen_tiles = tiles((sel_t - 1.0) * MASK_BIG)
        q_tiles = [tiles(q_t) for q_t in q_ts]
        for j in range(n_tiles):
            q3t = jnp.concatenate([q_tiles[g][j] for g in range(B_GROUP)], axis=1)
            pen = jnp.concatenate([pen_tiles[j]] * B_GROUP, axis=1)
            q3t_ref[h, j] = q3t.astype(BF16)
            qaugt_ref[h, j] = jnp.concatenate([q3t, pen, jnp.zeros(pen.shape, F32)],
                                              axis=0).astype(BF16)

    def tile_body(i, carry):
        rows = pl.ds(pl.multiple_of(i * T_TILE, T_TILE), T_TILE)

        outs = _two_pass_attention(
            i,
            [(lambda kr, h=h: kaug_ref[h, kr, :], lambda h=h: qaugt_ref[h, i], vst_ref.at[h],
              lambda d, h=h: bias_ref[h, d], s_ref.at[h], rmp_ref.at[0, h], acc_ref.at[0, h])
             for h in range(B_KV_HEADS)],
            [(lambda kr, h=h: pb_ref[0, kr, head_cols(kw0, h)], lambda h=h: q3t_ref[h, i],
              vwt_ref.at[h], lambda d, h=h: bias_ref[h, d], sw_ref.at[h], rmp_ref.at[1, h],
              acc_ref.at[1, h]) for h in range(B_KV_HEADS)])
        oslc, owin = outs[:B_KV_HEADS], outs[B_KV_HEADS:]

        sig_t = sig_ref[i]
        heads = []
        for h in range(B_KV_HEADS):
            for g in range(B_GROUP):
                head = h * B_GROUP + g
                gc = slice(g * T_TILE, (g + 1) * T_TILE)
                heads.append(sig_t[3 * head:3 * head + 1, :] * ocmp_ref[h, i, :, gc]
                             + sig_t[3 * head + 1:3 * head + 2, :] * oslc[h][:, gc]
                             + sig_t[3 * head + 2:3 * head + 3, :] * owin[h][:, gc])
        o_t = jnp.concatenate(heads, axis=0).astype(BF16)
        o_ref[0, rows, :] = _nt(eye_ref[...], o_t).astype(o_ref.dtype)
        return carry

    lax.fori_loop(0, n_tiles, tile_body, 0)


def _nsa(pb, cmp, bg, bias):
    b, seq, _ = pb.shape
    n_tiles = seq // T_TILE
    return pl.pallas_call(
        functools.partial(_nsa_kernel, seq=seq),
        out_shape=jax.ShapeDtypeStruct((b, seq, B_HEADS * HEAD_DIM), BF16),
        grid=(b,),
        in_specs=[pl.BlockSpec((1, seq, _PB_COLS), lambda i: (i, 0, 0)),
                  pl.BlockSpec((1,) + cmp.shape[1:], lambda i: (i, 0, 0, 0)),
                  pl.BlockSpec((1, seq, _BG_COLS), lambda i: (i, 0, 0)),
                  pl.BlockSpec(bias.shape, lambda i: (0, 0, 0, 0))],
        out_specs=pl.BlockSpec((1, seq, B_HEADS * HEAD_DIM), lambda i: (i, 0, 0)),
        scratch_shapes=[pltpu.VMEM((B_KV_HEADS, seq, LANE), BF16),
                        pltpu.VMEM((B_KV_HEADS, n_tiles, LANE, T_TILE), BF16),
                        pltpu.VMEM((B_KV_HEADS, n_tiles, LANE, T_TILE), BF16),
                        pltpu.VMEM((T_TILE, T_TILE), BF16),
                        pltpu.VMEM((n_tiles, SEL_PAD, T_TILE), F32),
                        pltpu.VMEM((B_KV_HEADS, n_tiles, HEAD_DIM, _G_ROWS), BF16),
                        pltpu.VMEM((B_KV_HEADS, n_tiles, LANE, _G_ROWS), BF16),
                        pltpu.VMEM((B_KV_HEADS, n_tiles, T_TILE, _G_ROWS), F32),
                        pltpu.VMEM((B_KV_HEADS, BAND_SLOTS, T_TILE, _G_ROWS), F32),
                        pltpu.VMEM((2, B_KV_HEADS, 8, _G_ROWS), F32),
                        pltpu.VMEM((2, B_KV_HEADS, LANE, _G_ROWS), F32),
                        pltpu.VMEM((B_KV_HEADS, n_tiles, HEAD_DIM, _G_ROWS), F32)],
        compiler_params=_params(1),
        name="nsa",
    )(pb, cmp, bg, bias)


def _moba_kernel(pc_ref, bias_ref, o_ref, kaug_ref, vt_ref, eye_ref, qaugt_ref, s_ref,
                 rmp_ref, acc_ref, *, seq):
    n_blk = seq // MOBA_BLOCK
    width = C_HEADS * HEAD_DIM
    eye_ref[...] = _eye(T_TILE)

    def cols(j, h):
        return slice(j * width + h * HEAD_DIM, j * width + (h + 1) * HEAD_DIM)

    n_sub = lax.broadcasted_iota(jnp.int32, (SEL_PAD, seq), 0)
    own = lax.broadcasted_iota(jnp.int32, (SEL_PAD, seq), 1) // MOBA_BLOCK
    past = n_sub < own
    eye_h = _eye(HEAD_DIM)
    for h in range(C_HEADS):
        kaug_ref[h] = _with_block_onehot(pc_ref[0, :, cols(1, h)], MOBA_BLOCK)
        for j, vt in _transposed_values(lambda r, h=h: pc_ref[0, r, cols(2, h)], n_blk):
            vt_ref[h, j] = vt
        kmean = jnp.concatenate(
            [jnp.mean(pc_ref[0, n * MOBA_BLOCK:(n + 1) * MOBA_BLOCK, cols(1, h)].astype(F32), axis=0,
                      keepdims=True) for n in range(n_blk)]
            + [jnp.zeros((SEL_PAD - n_blk, HEAD_DIM), F32)], axis=0)
        q_t = _nt(eye_h, pc_ref[0, :, cols(0, h)])
        gate_t = jnp.dot(kmean, q_t, precision=lax.Precision.HIGHEST,
                         preferred_element_type=F32)
        score = jnp.where(past, gate_t, NEG_INF)
        picked = past & (_top_keep(score, n_blk, MOBA_TOPK, MOBA_BLOCK) > 0.5)
        allowed_t = jnp.where(picked | (n_sub == own), 1.0, 0.0)
        qaug_t = _augment_q_t(q_t * ATTN_SCALE, allowed_t)
        for j in range(n_blk):
            qaugt_ref[h, j] = qaug_t[:, j * T_TILE:(j + 1) * T_TILE]

    def tile_body(i, carry):
        rows = pl.ds(pl.multiple_of(i * MOBA_BLOCK, MOBA_BLOCK), MOBA_BLOCK)
        outs = _two_pass_attention(
            i, [(lambda kr, h=h: kaug_ref[h, kr, :], lambda h=h: qaugt_ref[h, i], vt_ref.at[h],
                 lambda d, h=h: bias_ref[h, d], s_ref.at[h], rmp_ref.at[h], acc_ref.at[h])
                for h in range(C_HEADS)])
        o_t = jnp.concatenate(outs, axis=0).astype(BF16)
        o_ref[0, rows, :] = _nt(eye_ref[...], o_t).astype(o_ref.dtype)
        return carry

    lax.fori_loop(0, n_blk, tile_body, 0)


def _moba(pc, bias):
    b, seq, _ = pc.shape
    width = C_HEADS * HEAD_DIM
    n_tiles = seq // T_TILE
    return pl.pallas_call(
        functools.partial(_moba_kernel, seq=seq),
        out_shape=jax.ShapeDtypeStruct((b, seq, width), BF16),
        grid=(b,),
        in_specs=[pl.BlockSpec((1, seq, _C_COLS), lambda i: (i, 0, 0)),
                  pl.BlockSpec(bias.shape, lambda i: (0, 0, 0, 0))],
        out_specs=pl.BlockSpec((1, seq, width), lambda i: (i, 0, 0)),
        scratch_shapes=[pltpu.VMEM((C_HEADS, seq, LANE), BF16),
                        pltpu.VMEM((C_HEADS, n_tiles, LANE, T_TILE), BF16),
                        pltpu.VMEM((T_TILE, T_TILE), BF16),
                        pltpu.VMEM((C_HEADS, n_tiles, LANE, T_TILE), BF16),
                        pltpu.VMEM((C_HEADS, n_tiles, T_TILE, T_TILE), F32),
                        pltpu.VMEM((C_HEADS, 8, T_TILE), F32),
                        pltpu.VMEM((C_HEADS, LANE, T_TILE), F32)],
        compiler_params=_params(1),
        name="moba",
    )(pc, bias)


def _merge_kernel(oa_ref, ob_ref, oc_ref, mg_ref, x_ref, wb_ref, wo_ref, y_ref):
    r0 = A_HEADS * HEAD_DIM
    r1 = r0 + B_HEADS * HEAD_DIM
    ya = jnp.dot(oa_ref[...], wb_ref[0:r0, :], preferred_element_type=F32)
    yb = jnp.dot(ob_ref[...], wb_ref[r0:r1, :], preferred_element_type=F32)
    yc = jnp.dot(oc_ref[...], wb_ref[r1:, :], preferred_element_type=F32)
    merged = (jax.nn.sigmoid(mg_ref[:, 0:D_MODEL].astype(F32)) * ya
              + jax.nn.sigmoid(mg_ref[:, D_MODEL:2 * D_MODEL].astype(F32)) * yb
              + jax.nn.sigmoid(mg_ref[:, 2 * D_MODEL:].astype(F32)) * yc)
    y_ref[...] = x_ref[...] + jnp.dot(merged.astype(BF16), wo_ref[...], preferred_element_type=F32)


def _merge(oa, ob, oc, mg, x2, wb, wo):
    t = x2.shape[0]
    tm = ROW_TILE
    row = lambda i: (i, 0)
    return pl.pallas_call(
        _merge_kernel,
        out_shape=jax.ShapeDtypeStruct((t, D_MODEL), F32),
        grid=(t // tm,),
        in_specs=[pl.BlockSpec((tm, oa.shape[1]), row),
                  pl.BlockSpec((tm, ob.shape[1]), row), pl.BlockSpec((tm, oc.shape[1]), row),
                  pl.BlockSpec((tm, _MG_COLS), row), pl.BlockSpec((tm, D_MODEL), row),
                  pl.BlockSpec(wb.shape, lambda i: (0, 0)), pl.BlockSpec(wo.shape, lambda i: (0, 0))],
        out_specs=pl.BlockSpec((tm, D_MODEL), row),
        compiler_params=_params(1),
        name="merge",
    )(oa, ob, oc, mg, x2, wb, wo)


_FF_CHUNK = D_FF // 2


def _ffn_kernel(x_ref, halo_ref, g_ref, wu_ref, cw_ref, cb_ref, wd_ref, gf_ref, y_ref, xn_ref,
                *, tiles_per_seq, final_norm):
    i = pl.program_id(0)
    x = x_ref[...]
    g = g_ref[...]
    keep = (i % tiles_per_seq != 0).astype(F32)
    xn_ref[0:FFN_HALO, :] = (_rmsnorm_rows(halo_ref[...], g) * keep).astype(BF16)
    xn_ref[FFN_HALO:, :] = _rmsnorm_rows(x, g).astype(BF16)
    xn = xn_ref[...]
    rows = xn.shape[0]

    def conv(c0):
        hcol = jnp.dot(xn, wu_ref[:, c0:c0 + _FF_CHUNK], preferred_element_type=F32)
        out = (cw_ref[2:3, c0:c0 + _FF_CHUNK] * hcol
               + cw_ref[1:2, c0:c0 + _FF_CHUNK] * pltpu.roll(hcol, 1, axis=0)
               + cw_ref[0:1, c0:c0 + _FF_CHUNK] * pltpu.roll(hcol, 2, axis=0)
               + cb_ref[:, c0:c0 + _FF_CHUNK])
        return out[FFN_HALO:rows]

    acc = x
    for c in range(0, D_FF, _FF_CHUNK):
        a = conv(c)
        u = conv(D_FF + c)
        act = (a * jax.nn.sigmoid(a) * u).astype(BF16)
        acc = acc + jnp.dot(act, wd_ref[c:c + _FF_CHUNK, :], preferred_element_type=F32)
    if final_norm:
        acc = _rmsnorm_rows(acc, gf_ref[...])
    y_ref[...] = acc


def _ffn(x2, g, wu, cw, cb, wd, gf, *, seq, final_norm):
    t = x2.shape[0]
    tm = ROW_TILE
    const = lambda i: (0, 0)
    halo_blocks = tm // FFN_HALO
    return pl.pallas_call(
        functools.partial(_ffn_kernel, tiles_per_seq=seq // tm, final_norm=final_norm),
        out_shape=jax.ShapeDtypeStruct((t, D_MODEL), F32),
        grid=(t // tm,),
        in_specs=[pl.BlockSpec((tm, D_MODEL), lambda i: (i, 0)),
                  pl.BlockSpec((FFN_HALO, D_MODEL), lambda i: (jnp.maximum(i * halo_blocks - 1, 0), 0)),
                  pl.BlockSpec((1, D_MODEL), const),
                  pl.BlockSpec(wu.shape, const), pl.BlockSpec(cw.shape, const),
                  pl.BlockSpec(cb.shape, const), pl.BlockSpec(wd.shape, const),
                  pl.BlockSpec((1, D_MODEL), const)],
        out_specs=pl.BlockSpec((tm, D_MODEL), lambda i: (i, 0)),
        scratch_shapes=[pltpu.VMEM((tm + FFN_HALO, D_MODEL), BF16)],
        compiler_params=_params(1),
        name="conv_ffn",
    )(x2, x2, g, wu, cw, cb, wd, gf)


def kernel(x, rel_bias, norm_mix, w_in, cmp_pe_k, cmp_w1_k, cmp_w2_k, cmp_pe_v, cmp_w1_v, cmp_w2_v,
           w_branch, w_out, norm_ffn, w_up, conv_w, conv_b, w_down, norm_final):
    b, s, d = x.shape
    depth = w_in.shape[0]
    t = b * s
    assert d == D_MODEL and s % T_TILE == 0 and s // NSA_SLC_BLOCK == _N_SLC and t % ROW_TILE == 0
    assert s % ROW_TILE == 0 and all(s % (dil * A_TILE) == 0 for _, dil in A_GROUPS)
    assert all(win // dil == A_TILE for win, dil in A_GROUPS) and A_GROUPS[0][1] == 1
    assert all(ROW_TILE % dil == 0 for _, dil in A_GROUPS)

    hg = A_HEADS_PER_GROUP
    bias_a = jnp.concatenate([
        _bias_tiles(rel_bias,
                    np.concatenate([_bucket_tile(A_TILE, 1, dil, win // dil),
                                    _bucket_tile(A_TILE, 0, dil, win // dil)], axis=1),
                    gi * hg, hg)
        for gi, (win, dil) in enumerate(A_GROUPS)], axis=0)
    assert (_bucket_tile(T_TILE, 2, 1, 3 * T_TILE) == REL_BUCKETS - 1).all()
    bias_b = _bias_tiles_t(rel_bias, [_bucket_tile(T_TILE, d, 1, NSA_WINDOW - 1) for d in (0, 1, 2)],
                           A_HEADS, B_HEADS)
    bias_b = bias_b.reshape(B_KV_HEADS, B_GROUP, 3, T_TILE, T_TILE).transpose(0, 2, 3, 1, 4).reshape(
        B_KV_HEADS, 3, T_TILE, _G_ROWS)
    bias_c = _bias_tiles_t(rel_bias, [_bucket_tile(T_TILE, d, 1, s) for d in (0, 1)],
                           A_HEADS + B_HEADS, C_HEADS)

    w_in_k = _prep_w_in(w_in)
    x2 = x.reshape(t, d)
    for i in range(depth):
        pa0, pa1, pa2, pb, pcmp, pc, mg, bg = _proj_in(x2, norm_mix[i][None, :], w_in_k, layer=i,
                                                       seq=s)

        oa = _mixer_a(pa0.reshape(b, s, _GRP_COLS), pa1, pa2, bias_a)

        cmp = _compress(pcmp.reshape(4, b, s // NSA_CMP_STRIDE, NSA_CMP_STRIDE * HEAD_DIM),
                        cmp_pe_k[i].reshape(1, -1), cmp_pe_v[i].reshape(1, -1),
                        cmp_w1_k[i].astype(BF16), cmp_w1_v[i].astype(BF16),
                        cmp_w2_k[i].astype(BF16), cmp_w2_v[i].astype(BF16))
        ob = _nsa(pb.reshape(b, s, _PB_COLS), cmp, bg.reshape(b, s, _BG_COLS), bias_b)

        oc = _moba(pc.reshape(b, s, _C_COLS), bias_c)

        x2 = _merge(oa.reshape(t, -1), ob.reshape(t, -1), oc.reshape(t, -1), mg, x2,
                    w_branch[i].astype(BF16), w_out[i].astype(BF16))
        x2 = _ffn(x2, norm_ffn[i][None, :], w_up[i].astype(BF16), conv_w[i], conv_b[i][None, :],
                  w_down[i].astype(BF16), norm_final[None, :], seq=s, final_norm=(i == depth - 1))
    return x2.reshape(b, s, d)
```

```python
import functools
import math

import jax
import jax.numpy as jnp
import numpy as np
from jax import lax
from jax.experimental import pallas as pl
from jax.experimental.pallas import tpu as pltpu

F32 = jnp.float32
BF16 = jnp.bfloat16

D_MODEL = 1024
HEAD_DIM = 64
A_GROUPS = ((128, 1), (512, 4), (2048, 16))
A_HEADS_PER_GROUP = 2
A_HEADS = 6
B_HEADS = 6
B_KV_HEADS = 2
B_GROUP = 3
C_HEADS = 4
NSA_CMP_BLOCK = 32
NSA_CMP_STRIDE = 16
NSA_CMP_HIDDEN = 256
NSA_SLC_BLOCK = 64
NSA_SLC_TOPK = 16
NSA_WINDOW = 512
MOBA_BLOCK = 256
MOBA_TOPK = 3
REL_BUCKETS = 32
REL_MAX_DIST = 128
D_FF = 2816
RMS_EPS = 1e-6
NEG_INF = -1e30
FORCE_BONUS = 1e4
ATTN_SCALE = HEAD_DIM ** -0.5

LANE = 128
A_TILE = 128
T_TILE = 256
ROW_TILE = 512
FFN_HALO = 8
VMEM_LIMIT = 56 * 1024 * 1024

_GRP_COLS = 3 * A_HEADS_PER_GROUP * HEAD_DIM
_A_COLS = len(A_GROUPS) * _GRP_COLS
_PB_COLS = B_HEADS * HEAD_DIM + 4 * B_KV_HEADS * HEAD_DIM
_CMP_COLS = 2 * B_KV_HEADS * HEAD_DIM
_C_COLS = 3 * C_HEADS * HEAD_DIM
_MG_COLS = 3 * D_MODEL
_BG_COLS = LANE
_OFF_A = 0
_OFF_PB = _OFF_A + _A_COLS
_OFF_CMP = _OFF_PB + _PB_COLS
_OFF_C = _OFF_CMP + _CMP_COLS
_OFF_MG = _OFF_C + _C_COLS
_OFF_BG = _OFF_MG + _MG_COLS
_W_COLS = _OFF_BG + _BG_COLS
_PROJ_CHUNK = 4 * LANE


def _params(n_grid):
    return pltpu.CompilerParams(dimension_semantics=("arbitrary",) * n_grid,
                                vmem_limit_bytes=VMEM_LIMIT)


def _rel_bucket_np(dist):
    n = np.maximum(dist, 0)
    exact = REL_BUCKETS // 2
    nf = np.maximum(n, 1).astype(np.float32)
    large = exact + (np.log(nf / np.float32(exact)) / np.float32(math.log(REL_MAX_DIST / exact))
                     * np.float32(REL_BUCKETS - exact)).astype(np.int32)
    return np.where(n < exact, n, np.minimum(large, REL_BUCKETS - 1)).astype(np.int32)


def _bucket_tile(tile, block_offset, dil, max_rel):
    rel = block_offset * tile + np.arange(tile)[:, None] - np.arange(tile)[None, :]
    ok = (rel >= 0) & (rel <= max_rel)
    return np.where(ok, _rel_bucket_np(rel * dil), -1).astype(np.int32)


def _bias_tiles_kernel(tbl_ref, idx_ref, o_ref, *, head0, relative):
    h = pl.program_id(0) + head0
    idx = idx_ref[...]
    acc = jnp.full(idx.shape, NEG_INF, F32)
    for b in range(REL_BUCKETS):
        acc = jnp.where(idx == b, tbl_ref[b, h], acc)
    if relative:
        acc = acc - tbl_ref[REL_BUCKETS - 1, h]
    o_ref[0] = acc


def _bias_tiles(rel_bias, idx, head0, n_heads, relative=False):
    r, c = idx.shape
    return pl.pallas_call(
        functools.partial(_bias_tiles_kernel, head0=head0, relative=relative),
        out_shape=jax.ShapeDtypeStruct((n_heads, r, c), F32),
        grid=(n_heads,),
        in_specs=[pl.BlockSpec(memory_space=pltpu.SMEM),
                  pl.BlockSpec((r, c), lambda h: (0, 0))],
        out_specs=pl.BlockSpec((1, r, c), lambda h: (h, 0, 0)),
        compiler_params=_params(1),
        name="bias_tiles",
    )(rel_bias, jnp.asarray(idx))


def _bias_tiles_t(rel_bias, idx_tiles, head0, n_heads):
    r, c = idx_tiles[0].shape
    flat = _bias_tiles(rel_bias, np.concatenate([t.T for t in idx_tiles], axis=1), head0, n_heads,
                       relative=True)
    return flat.reshape(n_heads, c, len(idx_tiles), r).transpose(0, 2, 1, 3)


def _rmsnorm_rows(x, g):
    return x * lax.rsqrt(jnp.mean(x * x, axis=-1, keepdims=True) + RMS_EPS) * g


def _proj_in_kernel(x_ref, g_ref, w_ref, pa0_ref, pa1_ref, pa2_ref, pb_ref, pcmp_ref, pc_ref,
                    mg_ref, bg_ref, regroup_ref):
    xn = _rmsnorm_rows(x_ref[...], g_ref[...]).astype(BF16)
    tm = xn.shape[0]

    def mm(c0, c1):
        return jnp.dot(xn, w_ref[:, c0:c1], preferred_element_type=F32)

    pa0_ref[...] = mm(_OFF_A, _OFF_A + _GRP_COLS).astype(BF16)
    for gi, out_ref in ((1, pa1_ref), (2, pa2_ref)):
        dil = A_GROUPS[gi][1]
        res = mm(_OFF_A + gi * _GRP_COLS, _OFF_A + (gi + 1) * _GRP_COLS)
        for j in range(3):
            regroup_ref[j] = res[:, j * LANE:(j + 1) * LANE]
        for r in range(dil):
            for j in range(3):
                out_ref[0, r, :, j * LANE:(j + 1) * LANE] = (
                    regroup_ref[j, pl.ds(r, tm // dil, stride=dil), :].astype(BF16))
    def project(out_ref, off, cols, dtype=BF16):
        for c in range(0, cols, _PROJ_CHUNK):
            n = min(_PROJ_CHUNK, cols - c)
            out_ref[:, c:c + n] = mm(off + c, off + c + n).astype(dtype)

    project(pb_ref, _OFF_PB, _PB_COLS)
    cmp = mm(_OFF_CMP, _OFF_CMP + _CMP_COLS)
    for kv in range(2):
        regroup_ref[kv] = cmp[:, kv * LANE:(kv + 1) * LANE]
    for l in range(NSA_CMP_STRIDE):
        for kv in range(2):
            a = regroup_ref[kv, pl.ds(l, tm // NSA_CMP_STRIDE, stride=NSA_CMP_STRIDE), :].astype(BF16)
            for h in range(B_KV_HEADS):
                pcmp_ref[kv * B_KV_HEADS + h, :, l * HEAD_DIM:(l + 1) * HEAD_DIM] = (
                    a[:, h * HEAD_DIM:(h + 1) * HEAD_DIM])
    project(pc_ref, _OFF_C, _C_COLS)
    project(mg_ref, _OFF_MG, _MG_COLS)
    project(bg_ref, _OFF_BG, _BG_COLS, F32)


def _proj_in(x2, g, w, *, layer, seq):
    t = x2.shape[0]
    tm = ROW_TILE
    b = t // seq
    tps = seq // tm
    row = lambda i: (i, 0)
    d1, d2 = A_GROUPS[1][1], A_GROUPS[2][1]
    sub = lambda i: (i // tps, 0, i % tps, 0)
    return pl.pallas_call(
        _proj_in_kernel,
        out_shape=(jax.ShapeDtypeStruct((t, _GRP_COLS), BF16),
                   jax.ShapeDtypeStruct((b, d1, seq // d1, _GRP_COLS), BF16),
                   jax.ShapeDtypeStruct((b, d2, seq // d2, _GRP_COLS), BF16),
                   jax.ShapeDtypeStruct((t, _PB_COLS), BF16),
                   jax.ShapeDtypeStruct((4, t // NSA_CMP_STRIDE, NSA_CMP_STRIDE * HEAD_DIM), BF16),
                   jax.ShapeDtypeStruct((t, _C_COLS), BF16),
                   jax.ShapeDtypeStruct((t, _MG_COLS), BF16),
                   jax.ShapeDtypeStruct((t, _BG_COLS), F32)),
        grid=(t // tm,),
        in_specs=[pl.BlockSpec((tm, D_MODEL), row),
                  pl.BlockSpec((1, D_MODEL), lambda i: (0, 0)),
                  pl.BlockSpec((None, D_MODEL, _W_COLS), lambda i: (layer, 0, 0))],
        out_specs=(pl.BlockSpec((tm, _GRP_COLS), row),
                   pl.BlockSpec((1, d1, tm // d1, _GRP_COLS), sub),
                   pl.BlockSpec((1, d2, tm // d2, _GRP_COLS), sub),
                   pl.BlockSpec((tm, _PB_COLS), row),
                   pl.BlockSpec((4, tm // NSA_CMP_STRIDE, NSA_CMP_STRIDE * HEAD_DIM),
                                lambda i: (0, i, 0)),
                   pl.BlockSpec((tm, _C_COLS), row),
                   pl.BlockSpec((tm, _MG_COLS), row),
                   pl.BlockSpec((tm, _BG_COLS), row)),
        scratch_shapes=[pltpu.VMEM((3, tm, LANE), F32)],
        compiler_params=_params(1),
        name="proj_in",
    )(x2, g, w)


def _w_in_segments():
    a = A_HEADS * HEAD_DIM
    bq0 = 3 * a
    bkc0 = bq0 + B_HEADS * HEAD_DIM
    bks0 = bkc0 + 2 * B_KV_HEADS * HEAD_DIM
    bg0 = bks0 + 4 * B_KV_HEADS * HEAD_DIM
    c0 = bg0 + B_HEADS * 3
    mg0 = c0 + _C_COLS
    gw = A_HEADS_PER_GROUP * HEAD_DIM
    segs = [(_OFF_A + (gi * 3 + j) * gw, j * a + gi * gw, gw)
            for gi in range(len(A_GROUPS)) for j in range(3)]
    segs += [(_OFF_PB, bq0, B_HEADS * HEAD_DIM),
             (_OFF_PB + B_HEADS * HEAD_DIM, bks0, bg0 - bks0),
             (_OFF_CMP, bkc0, _CMP_COLS), (_OFF_C, c0, _C_COLS), (_OFF_MG, mg0, _MG_COLS),
             (_OFF_BG, bg0, B_HEADS * 3)]
    return segs


def _prep_w_in_kernel(w_ref, o_ref):
    o_ref[0, :, _OFF_BG:] = jnp.zeros((o_ref.shape[1], _BG_COLS), BF16)
    for dst, src, width in _w_in_segments():
        for c in range(0, width, _PROJ_CHUNK):
            n = min(_PROJ_CHUNK, width - c)
            o_ref[0, :, dst + c:dst + c + n] = w_ref[0, :, src + c:src + c + n].astype(BF16)


def _prep_w_in(w):
    depth, rows, cols = w.shape
    rb = 128
    return pl.pallas_call(
        _prep_w_in_kernel,
        out_shape=jax.ShapeDtypeStruct((depth, rows, _W_COLS), BF16),
        grid=(depth, rows // rb),
        in_specs=[pl.BlockSpec((1, rb, cols), lambda l, r: (l, r, 0))],
        out_specs=pl.BlockSpec((1, rb, _W_COLS), lambda l, r: (l, r, 0)),
        compiler_params=_params(2),
        name="prep_w_in",
    )(w)


_A_UNROLL = 3


def _mixer_a_kernel(pa0_ref, pa1_ref, pa2_ref, bias_ref, o_ref, on_ref, ln_ref, *, seq):
    gw = A_HEADS_PER_GROUP * HEAD_DIM

    def band_block(src, l0, first, gi, rows):
        outs, lses = [], []
        for h in range(A_HEADS_PER_GROUP):
            qc, kc, vc = (slice(j * gw + h * HEAD_DIM, j * gw + (h + 1) * HEAD_DIM) for j in range(3))
            q = src[pl.ds(l0, A_TILE), qc] * ATTN_SCALE
            if first:
                kr = pl.ds(l0, A_TILE)
                bias = bias_ref[A_HEADS_PER_GROUP * gi + h, :, A_TILE:]
            else:
                kr = pl.ds(l0 - A_TILE, 2 * A_TILE)
                bias = bias_ref[A_HEADS_PER_GROUP * gi + h]
            s = _nt(q, src[kr, kc]) + bias
            m = jnp.max(s, axis=-1, keepdims=True)
            e = jnp.exp(s - m)
            l = jnp.sum(e, axis=-1, keepdims=True)
            outs.append(jnp.dot(e.astype(BF16), src[kr, vc], preferred_element_type=F32) / l)
            lses.append(jnp.broadcast_to(m + jnp.log(l), (A_TILE, HEAD_DIM)))
        on_ref[gi, rows, :] = jnp.concatenate(outs, axis=-1)
        ln_ref[gi, rows, :] = jnp.concatenate(lses, axis=-1)

    for gi, ((_, dil), pa_ref) in enumerate(zip(A_GROUPS, (pa0_ref, pa1_ref, pa2_ref))):
        n_blocks = seq // dil // A_TILE
        if dil == 1:
            band_block(pa_ref.at[0], 0, True, gi, pl.ds(0, A_TILE))
            assert (n_blocks - 1) % _A_UNROLL == 0

            def body0(it, c, pa_ref=pa_ref, gi=gi):
                for u in range(_A_UNROLL):
                    l0 = pl.multiple_of((1 + it * _A_UNROLL + u) * A_TILE, A_TILE)
                    band_block(pa_ref.at[0], l0, False, gi, pl.ds(l0, A_TILE))
                return c
            lax.fori_loop(0, (n_blocks - 1) // _A_UNROLL, body0, 0)
        elif n_blocks > 1:
            for r in range(dil):
                band_block(pa_ref.at[0, r], 0, True, gi, pl.ds(r, A_TILE, stride=dil))

            def body1(i, c, pa_ref=pa_ref, gi=gi, dil=dil):
                l0 = pl.multiple_of(i * A_TILE, A_TILE)
                for r in range(dil):
                    band_block(pa_ref.at[0, r], l0, False, gi, pl.ds(l0 * dil + r, A_TILE, stride=dil))
                return c
            lax.fori_loop(1, n_blocks, body1, 0)
        else:
            per_iter = 4

            def body2(it, c, pa_ref=pa_ref, gi=gi, dil=dil):
                for u in range(per_iter):
                    r = it * per_iter + u
                    band_block(pa_ref.at[0, r], 0, True, gi, pl.ds(r, A_TILE, stride=dil))
                return c
            lax.fori_loop(0, dil // per_iter, body2, 0)

    chunk = T_TILE
    for c0 in range(0, seq, chunk):
        rows = slice(c0, c0 + chunk)
        lse = [ln_ref[g, rows, :] for g in range(len(A_GROUPS))]
        top = functools.reduce(jnp.maximum, lse)
        ex = [jnp.exp(v - top) for v in lse]
        den = functools.reduce(lambda a, b: a + b, ex)
        for g in range(len(A_GROUPS)):
            o_ref[0, rows, g * gw:(g + 1) * gw] = (ex[g] / den * on_ref[g, rows, :]).astype(BF16)


def _mixer_a(pa0, pa1, pa2, bias):
    b, seq, _ = pa0.shape
    return pl.pallas_call(
        functools.partial(_mixer_a_kernel, seq=seq),
        out_shape=jax.ShapeDtypeStruct((b, seq, _GRP_COLS), BF16),
        grid=(b,),
        in_specs=[pl.BlockSpec((1, seq, _GRP_COLS), lambda i: (i, 0, 0)),
                  pl.BlockSpec((1,) + pa1.shape[1:], lambda i: (i, 0, 0, 0)),
                  pl.BlockSpec((1,) + pa2.shape[1:], lambda i: (i, 0, 0, 0)),
                  pl.BlockSpec(bias.shape, lambda i: (0, 0, 0))],
        out_specs=pl.BlockSpec((1, seq, _GRP_COLS), lambda i: (i, 0, 0)),
        scratch_shapes=[pltpu.VMEM((len(A_GROUPS), seq, LANE), F32),
                        pltpu.VMEM((len(A_GROUPS), seq, LANE), F32)],
        compiler_params=_params(1),
        name="mixer_a",
    )(pa0, pa1, pa2, bias)


def _compress_kernel(r_ref, pek_ref, pev_ref, w1k_ref, w1v_ref, w2k_ref, w2v_ref, o_ref):
    half = NSA_CMP_STRIDE * HEAD_DIM
    for kv, (pe_ref, w1_ref, w2_ref) in enumerate(((pek_ref, w1k_ref, w2k_ref),
                                                   (pev_ref, w1v_ref, w2v_ref))):
        for h in range(B_KV_HEADS):
            r = r_ref[kv * B_KV_HEADS + h, 0].astype(F32)
            lo = jnp.dot((r + pe_ref[:, :half]).astype(BF16), w1_ref[:half, :],
                         preferred_element_type=F32)
            hi = jnp.dot((r + pe_ref[:, half:]).astype(BF16), w1_ref[half:, :],
                         preferred_element_type=F32)
            hid = lo + pltpu.roll(hi, hi.shape[0] - 1, axis=0)
            act = jax.nn.gelu(hid).astype(BF16)
            o_ref[0, kv * B_KV_HEADS + h] = jnp.dot(act, w2_ref[...],
                                                    preferred_element_type=F32).astype(BF16)


def _compress(r, pe_k, pe_v, w1_k, w1_v, w2_k, w2_v):
    _, b, m, c = r.shape
    const2 = lambda i: (0, 0)
    return pl.pallas_call(
        _compress_kernel,
        out_shape=jax.ShapeDtypeStruct((b, 4, m, HEAD_DIM), BF16),
        grid=(b,),
        in_specs=[pl.BlockSpec((4, 1, m, c), lambda i: (0, i, 0, 0)),
                  pl.BlockSpec(pe_k.shape, const2), pl.BlockSpec(pe_v.shape, const2),
                  pl.BlockSpec(w1_k.shape, const2), pl.BlockSpec(w1_v.shape, const2),
                  pl.BlockSpec(w2_k.shape, const2), pl.BlockSpec(w2_v.shape, const2)],
        out_specs=pl.BlockSpec((1, 4, m, HEAD_DIM), lambda i: (i, 0, 0, 0)),
        compiler_params=_params(1),
        name="nsa_compress",
    )(r, pe_k, pe_v, w1_k, w1_v, w2_k, w2_v)


MASK_BIG = 2.0 ** 100
SEL_PAD = 32


def _nt(a, b, **kw):
    return lax.dot_general(a, b, (((1,), (1,)), ((), ())), preferred_element_type=F32, **kw)


def _top_mask_t(score, n_rows, n_top):
    sub = lax.broadcasted_iota(jnp.int32, score.shape, 0)
    rank = jnp.zeros(score.shape, F32)
    for j in range(n_rows):
        row = score[j:j + 1, :]
        ge = jnp.where(row >= score, 1.0, 0.0)
        gt = jnp.where(row > score, 1.0, 0.0)
        rank = rank + jnp.where(sub > j, ge, gt)
    return rank < n_top


def _top_keep(score, n_rows, n_top, block):
    free = n_top * block
    ranked = jnp.where(_top_mask_t(score[:, free:], n_rows, n_top), 1.0, 0.0)
    return jnp.concatenate([jnp.ones((score.shape[0], free), F32), ranked], axis=1)


BAND_SLOTS = 4


def _loop_pairs(lo, hi, body):
    n = jnp.maximum(hi - lo, 0)

    def pair(p, c):
        body(lo + 2 * p)
        body(lo + 2 * p + 1)
        return c
    lax.fori_loop(0, n // 2, pair, 0)
    pl.when(n % 2 == 1)(lambda: body(hi - 1))


def _two_pass_attention(i, streams, band_streams=()):
    groups = ((streams, lambda j: j), (band_streams, lambda j: j % BAND_SLOTS))

    def key_rows(j):
        return pl.ds(pl.multiple_of(j * T_TILE, T_TILE), T_TILE)

    def logits(group, j, d):
        some, slot = group
        for k_tile, q_t, _, bias_tile, s_ref, rmp_ref, _ in some:
            s = jnp.dot(k_tile(key_rows(j)), q_t(), preferred_element_type=F32)
            if d is not None:
                s = s + bias_tile(d)
            s_ref[slot(j)] = s
            fold = jnp.max(s.reshape(s.shape[0] // 8, 8, s.shape[1]), axis=0)
            rmp_ref[...] = fold if d == 0 else jnp.maximum(rmp_ref[...], fold)

    def near(n_full, n_band):
        for d in range(n_full):
            logits(groups[0], i - d, d)
        for d in range(n_band):
            logits(groups[1], i - d, d)

    if band_streams:
        pl.when(i == 0)(lambda: near(1, 1))
        pl.when(i == 1)(lambda: near(2, 2))
        pl.when(i >= 2)(lambda: near(2, 3))
    else:
        logits(groups[0], i, 0)
        pl.when(i >= 1)(lambda: logits(groups[0], i - 1, 1))

    _loop_pairs(0, i - 1, lambda j: logits(groups[0], j, None))

    tops = {}
    for some, _ in groups:
        for stream in some:
            rmp_ref, acc_ref = stream[-2:]
            tops[id(rmp_ref)] = jnp.max(rmp_ref[...], axis=0, keepdims=True)
            acc_ref[...] = jnp.zeros(acc_ref.shape, F32)

    def pv(active):
        def body(j):
            for some, slot in active:
                for _, _, vt_ref, _, s_ref, rmp_ref, acc_ref in some:
                    e = jnp.exp(s_ref[slot(j)] - tops[id(rmp_ref)])
                    acc_ref[...] += jnp.dot(vt_ref[j], e.astype(BF16), preferred_element_type=F32)
        return body

    if band_streams:
        band_lo = jnp.maximum(i - 2, 0)
        _loop_pairs(0, band_lo, pv(groups[:1]))
        pv_all = pv(groups)

        def last_tiles(n):
            for d in reversed(range(n)):
                pv_all(i - d)

        pl.when(i == 0)(lambda: last_tiles(1))
        pl.when(i == 1)(lambda: last_tiles(2))
        pl.when(i >= 2)(lambda: last_tiles(3))
    else:
        _loop_pairs(0, i + 1, pv(groups[:1]))

    outs = []
    for some, _ in groups:
        for *_, acc_ref in some:
            acc = acc_ref[...]
            outs.append(acc[:HEAD_DIM, :] / acc[HEAD_DIM:HEAD_DIM + 1, :])
    return outs


def _eye(n, dtype=BF16):
    return (lax.broadcasted_iota(jnp.int32, (n, n), 0)
            == lax.broadcasted_iota(jnp.int32, (n, n), 1)).astype(dtype)


def _transposed_values(v_ref_rows, n_tiles):
    eye = _eye(LANE)
    for j in range(n_tiles):
        v = _with_ones_column(v_ref_rows(slice(j * T_TILE, (j + 1) * T_TILE)))
        yield j, _nt(eye, v).astype(BF16)


def _with_ones_column(v):
    one = (lax.broadcasted_iota(jnp.int32, v.shape, 1) == 0).astype(F32)
    return jnp.concatenate([v.astype(F32), one], axis=1).astype(BF16)


def _with_block_onehot(k, block):
    n = k.shape[0]
    shape = (n, SEL_PAD)
    hot = (lax.broadcasted_iota(jnp.int32, shape, 0) // block
           == lax.broadcasted_iota(jnp.int32, shape, 1)).astype(F32)
    return jnp.concatenate([k.astype(F32), hot, jnp.zeros(shape, F32)], axis=1).astype(BF16)


def _augment_q_t(q_t, allowed_t):
    pen = (allowed_t - 1.0) * MASK_BIG
    return jnp.concatenate([q_t, pen, jnp.zeros(pen.shape, F32)], axis=0).astype(BF16)


_N_SLC = 32
_G_ROWS = B_GROUP * T_TILE


def _nsa_kernel(pb_ref, cmp_ref, bg_ref, bias_ref, o_ref,
                kaug_ref, vst_ref, vwt_ref, eye_ref, sig_ref, q3t_ref, qaugt_ref, s_ref, sw_ref,
                rmp_ref, acc_ref, ocmp_ref, *, seq):
    n_tiles = seq // T_TILE
    n_cmp_rows = seq // NSA_CMP_STRIDE
    q_cols = B_HEADS * HEAD_DIM
    ks0, vs0, kw0, vw0 = (q_cols + i * B_KV_HEADS * HEAD_DIM for i in range(4))

    def head_cols(c0, h):
        return slice(c0 + h * HEAD_DIM, c0 + (h + 1) * HEAD_DIM)

    j_id = lax.broadcasted_iota(jnp.int32, (_N_SLC, n_cmp_rows), 0)
    c_id = lax.broadcasted_iota(jnp.int32, (_N_SLC, n_cmp_rows), 1)
    overlap_t = ((c_id * NSA_CMP_STRIDE < (j_id + 1) * NSA_SLC_BLOCK)
                 & (c_id * NSA_CMP_STRIDE + NSA_CMP_BLOCK > j_id * NSA_SLC_BLOCK)).astype(F32)
    eye_ref[...] = _eye(T_TILE)
    for h in range(B_KV_HEADS):
        kaug_ref[h] = _with_block_onehot(pb_ref[0, :, head_cols(ks0, h)], NSA_SLC_BLOCK)
        for j, vt in _transposed_values(lambda r, h=h: pb_ref[0, r, head_cols(vs0, h)], n_tiles):
            vst_ref[h, j] = vt
        for j, vt in _transposed_values(lambda r, h=h: pb_ref[0, r, head_cols(vw0, h)], n_tiles):
            vwt_ref[h, j] = vt

    def tiles(a):
        return [a[:, j * T_TILE:(j + 1) * T_TILE] for j in range(n_tiles)]

    pick = (lax.broadcasted_iota(jnp.int32, (SEL_PAD, _BG_COLS), 0)
            == lax.broadcasted_iota(jnp.int32, (SEL_PAD, _BG_COLS), 1)).astype(F32)
    sig_all = jax.nn.sigmoid(_nt(pick, bg_ref[0], precision=lax.Precision.HIGHEST))
    for j, t in enumerate(tiles(sig_all)):
        sig_ref[j] = t

    eye_h = _eye(HEAD_DIM)
    t_lane = lax.broadcasted_iota(jnp.int32, (n_cmp_rows, seq), 1)
    c_end = (lax.broadcasted_iota(jnp.int32, (n_cmp_rows, seq), 0) * NSA_CMP_STRIDE
             + NSA_CMP_BLOCK - 1)
    ok = (c_end <= t_lane) & (c_end < seq)
    ok_f = ok.astype(F32)
    j_sub = lax.broadcasted_iota(jnp.int32, (_N_SLC, seq), 0)
    jt = lax.broadcasted_iota(jnp.int32, (_N_SLC, seq), 1) // NSA_SLC_BLOCK
    forced = ((j_sub == 0) | (j_sub == jt) | (j_sub == jt - 1)).astype(F32)
    valid = j_sub <= jt

    for h in range(B_KV_HEADS):
        kc = cmp_ref[0, h]
        vct = _nt(eye_h, cmp_ref[0, B_KV_HEADS + h]).astype(BF16)
        q_ts, p_sum = [], None
        for g in range(B_GROUP):
            q_t = _nt(eye_h, pb_ref[0, :, head_cols(0, h * B_GROUP + g)]) * ATTN_SCALE
            q_ts.append(q_t)
            lc = jnp.where(ok, jnp.dot(kc, q_t.astype(BF16), preferred_element_type=F32), NEG_INF)
            ec = jnp.exp(lc - jnp.max(lc, axis=0, keepdims=True))
            pc = ec / jnp.sum(ec, axis=0, keepdims=True) * ok_f
            o_cmp = jnp.dot(vct, pc.astype(BF16), preferred_element_type=F32)
            for j, t in enumerate(tiles(o_cmp)):
                ocmp_ref[h, j, :, g * T_TILE:(g + 1) * T_TILE] = t
            p_sum = pc if p_sum is None else p_sum + pc
        imp_t = jnp.dot(overlap_t, p_sum, precision=lax.Precision.HIGHEST,
                        preferred_element_type=F32)
        score = jnp.where(valid, imp_t + FORCE_BONUS * forced, NEG_INF)
        sel_t = jnp.where(valid, _top_keep(score, _N_SLC, NSA_SLC_TOPK, NSA_SLC_BLOCK), 0.0)
        pen_tiles = tiles((sel_t - 1.0) * MASK_BIG)
        q_tiles = [tiles(q_t) for q_t in q_ts]
        for j in range(n_tiles):
            q3t = jnp.concatenate([q_tiles[g][j] for g in range(B_GROUP)], axis=1)
            pen = jnp.concatenate([pen_tiles[j]] * B_GROUP, axis=1)
            q3t_ref[h, j] = q3t.astype(BF16)
            qaugt_ref[h, j] = jnp.concatenate([q3t, pen, jnp.zeros(pen.shape, F32)],
                                              axis=0).astype(BF16)

    def tile_body(i, carry):
        rows = pl.ds(pl.multiple_of(i * T_TILE, T_TILE), T_TILE)

        outs = _two_pass_attention(
            i,
            [(lambda kr, h=h: kaug_ref[h, kr, :], lambda h=h: qaugt_ref[h, i], vst_ref.at[h],
              lambda d, h=h: bias_ref[h, d], s_ref.at[h], rmp_ref.at[0, h], acc_ref.at[0, h])
             for h in range(B_KV_HEADS)],
            [(lambda kr, h=h: pb_ref[0, kr, head_cols(kw0, h)], lambda h=h: q3t_ref[h, i],
              vwt_ref.at[h], lambda d, h=h: bias_ref[h, d], sw_ref.at[h], rmp_ref.at[1, h],
              acc_ref.at[1, h]) for h in range(B_KV_HEADS)])
        oslc, owin = outs[:B_KV_HEADS], outs[B_KV_HEADS:]

        sig_t = sig_ref[i]
        heads = []
        for h in range(B_KV_HEADS):
            for g in range(B_GROUP):
                head = h * B_GROUP + g
                gc = slice(g * T_TILE, (g + 1) * T_TILE)
                heads.append(sig_t[3 * head:3 * head + 1, :] * ocmp_ref[h, i, :, gc]
                             + sig_t[3 * head + 1:3 * head + 2, :] * oslc[h][:, gc]
                             + sig_t[3 * head + 2:3 * head + 3, :] * owin[h][:, gc])
        o_t = jnp.concatenate(heads, axis=0).astype(BF16)
        o_ref[0, rows, :] = _nt(eye_ref[...], o_t).astype(o_ref.dtype)
        return carry

    lax.fori_loop(0, n_tiles, tile_body, 0)


def _nsa(pb, cmp, bg, bias):
    b, seq, _ = pb.shape
    n_tiles = seq // T_TILE
    return pl.pallas_call(
        functools.partial(_nsa_kernel, seq=seq),
        out_shape=jax.ShapeDtypeStruct((b, seq, B_HEADS * HEAD_DIM), BF16),
        grid=(b,),
        in_specs=[pl.BlockSpec((1, seq, _PB_COLS), lambda i: (i, 0, 0)),
                  pl.BlockSpec((1,) + cmp.shape[1:], lambda i: (i, 0, 0, 0)),
                  pl.BlockSpec((1, seq, _BG_COLS), lambda i: (i, 0, 0)),
                  pl.BlockSpec(bias.shape, lambda i: (0, 0, 0, 0))],
        out_specs=pl.BlockSpec((1, seq, B_HEADS * HEAD_DIM), lambda i: (i, 0, 0)),
        scratch_shapes=[pltpu.VMEM((B_KV_HEADS, seq, LANE), BF16),
                        pltpu.VMEM((B_KV_HEADS, n_tiles, LANE, T_TILE), BF16),
                        pltpu.VMEM((B_KV_HEADS, n_tiles, LANE, T_TILE), BF16),
                        pltpu.VMEM((T_TILE, T_TILE), BF16),
                        pltpu.VMEM((n_tiles, SEL_PAD, T_TILE), F32),
                        pltpu.VMEM((B_KV_HEADS, n_tiles, HEAD_DIM, _G_ROWS), BF16),
                        pltpu.VMEM((B_KV_HEADS, n_tiles, LANE, _G_ROWS), BF16),
                        pltpu.VMEM((B_KV_HEADS, n_tiles, T_TILE, _G_ROWS), F32),
                        pltpu.VMEM((B_KV_HEADS, BAND_SLOTS, T_TILE, _G_ROWS), F32),
                        pltpu.VMEM((2, B_KV_HEADS, 8, _G_ROWS), F32),
                        pltpu.VMEM((2, B_KV_HEADS, LANE, _G_ROWS), F32),
                        pltpu.VMEM((B_KV_HEADS, n_tiles, HEAD_DIM, _G_ROWS), F32)],
        compiler_params=_params(1),
        name="nsa",
    )(pb, cmp, bg, bias)


def _moba_kernel(pc_ref, bias_ref, o_ref, kaug_ref, vt_ref, eye_ref, qaugt_ref, s_ref,
                 rmp_ref, acc_ref, *, seq):
    n_blk = seq // MOBA_BLOCK
    width = C_HEADS * HEAD_DIM
    eye_ref[...] = _eye(T_TILE)

    def cols(j, h):
        return slice(j * width + h * HEAD_DIM, j * width + (h + 1) * HEAD_DIM)

    n_sub = lax.broadcasted_iota(jnp.int32, (SEL_PAD, seq), 0)
    own = lax.broadcasted_iota(jnp.int32, (SEL_PAD, seq), 1) // MOBA_BLOCK
    past = n_sub < own
    eye_h = _eye(HEAD_DIM)
    for h in range(C_HEADS):
        kaug_ref[h] = _with_block_onehot(pc_ref[0, :, cols(1, h)], MOBA_BLOCK)
        for j, vt in _transposed_values(lambda r, h=h: pc_ref[0, r, cols(2, h)], n_blk):
            vt_ref[h, j] = vt
        kmean = jnp.concatenate(
            [jnp.mean(pc_ref[0, n * MOBA_BLOCK:(n + 1) * MOBA_BLOCK, cols(1, h)].astype(F32), axis=0,
                      keepdims=True) for n in range(n_blk)]
            + [jnp.zeros((SEL_PAD - n_blk, HEAD_DIM), F32)], axis=0)
        q_t = _nt(eye_h, pc_ref[0, :, cols(0, h)])
        gate_t = jnp.dot(kmean, q_t, precision=lax.Precision.HIGHEST,
                         preferred_element_type=F32)
        score = jnp.where(past, gate_t, NEG_INF)
        picked = past & (_top_keep(score, n_blk, MOBA_TOPK, MOBA_BLOCK) > 0.5)
        allowed_t = jnp.where(picked | (n_sub == own), 1.0, 0.0)
        qaug_t = _augment_q_t(q_t * ATTN_SCALE, allowed_t)
        for j in range(n_blk):
            qaugt_ref[h, j] = qaug_t[:, j * T_TILE:(j + 1) * T_TILE]

    def tile_body(i, carry):
        rows = pl.ds(pl.multiple_of(i * MOBA_BLOCK, MOBA_BLOCK), MOBA_BLOCK)
        outs = _two_pass_attention(
            i, [(lambda kr, h=h: kaug_ref[h, kr, :], lambda h=h: qaugt_ref[h, i], vt_ref.at[h],
                 lambda d, h=h: bias_ref[h, d], s_ref.at[h], rmp_ref.at[h], acc_ref.at[h])
                for h in range(C_HEADS)])
        o_t = jnp.concatenate(outs, axis=0).astype(BF16)
        o_ref[0, rows, :] = _nt(eye_ref[...], o_t).astype(o_ref.dtype)
        return carry

    lax.fori_loop(0, n_blk, tile_body, 0)


def _moba(pc, bias):
    b, seq, _ = pc.shape
    width = C_HEADS * HEAD_DIM
    n_tiles = seq // T_TILE
    return pl.pallas_call(
        functools.partial(_moba_kernel, seq=seq),
        out_shape=jax.ShapeDtypeStruct((b, seq, width), BF16),
        grid=(b,),
        in_specs=[pl.BlockSpec((1, seq, _C_COLS), lambda i: (i, 0, 0)),
                  pl.BlockSpec(bias.shape, lambda i: (0, 0, 0, 0))],
        out_specs=pl.BlockSpec((1, seq, width), lambda i: (i, 0, 0)),
        scratch_shapes=[pltpu.VMEM((C_HEADS, seq, LANE), BF16),
                        pltpu.VMEM((C_HEADS, n_tiles, LANE, T_TILE), BF16),
                        pltpu.VMEM((T_TILE, T_TILE), BF16),
                        pltpu.VMEM((C_HEADS, n_tiles, LANE, T_TILE), BF16),
                        pltpu.VMEM((C_HEADS, n_tiles, T_TILE, T_TILE), F32),
                        pltpu.VMEM((C_HEADS, 8, T_TILE), F32),
                        pltpu.VMEM((C_HEADS, LANE, T_TILE), F32)],
        compiler_params=_params(1),
        name="moba",
    )(pc, bias)


def _merge_kernel(oa_ref, ob_ref, oc_ref, mg_ref, x_ref, wb_ref, wo_ref, y_ref):
    r0 = A_HEADS * HEAD_DIM
    r1 = r0 + B_HEADS * HEAD_DIM
    ya = jnp.dot(oa_ref[...], wb_ref[0:r0, :], preferred_element_type=F32)
    yb = jnp.dot(ob_ref[...], wb_ref[r0:r1, :], preferred_element_type=F32)
    yc = jnp.dot(oc_ref[...], wb_ref[r1:, :], preferred_element_type=F32)
    merged = (jax.nn.sigmoid(mg_ref[:, 0:D_MODEL].astype(F32)) * ya
              + jax.nn.sigmoid(mg_ref[:, D_MODEL:2 * D_MODEL].astype(F32)) * yb
              + jax.nn.sigmoid(mg_ref[:, 2 * D_MODEL:].astype(F32)) * yc)
    y_ref[...] = x_ref[...] + jnp.dot(merged.astype(BF16), wo_ref[...], preferred_element_type=F32)


def _merge(oa, ob, oc, mg, x2, wb, wo):
    t = x2.shape[0]
    tm = ROW_TILE
    row = lambda i: (i, 0)
    return pl.pallas_call(
        _merge_kernel,
        out_shape=jax.ShapeDtypeStruct((t, D_MODEL), F32),
        grid=(t // tm,),
        in_specs=[pl.BlockSpec((tm, oa.shape[1]), row),
                  pl.BlockSpec((tm, ob.shape[1]), row), pl.BlockSpec((tm, oc.shape[1]), row),
                  pl.BlockSpec((tm, _MG_COLS), row), pl.BlockSpec((tm, D_MODEL), row),
                  pl.BlockSpec(wb.shape, lambda i: (0, 0)), pl.BlockSpec(wo.shape, lambda i: (0, 0))],
        out_specs=pl.BlockSpec((tm, D_MODEL), row),
        compiler_params=_params(1),
        name="merge",
    )(oa, ob, oc, mg, x2, wb, wo)


_FF_CHUNK = D_FF // 2


def _ffn_kernel(x_ref, halo_ref, g_ref, wu_ref, cw_ref, cb_ref, wd_ref, gf_ref, y_ref, xn_ref,
                *, tiles_per_seq, final_norm):
    i = pl.program_id(0)
    x = x_ref[...]
    g = g_ref[...]
    keep = (i % tiles_per_seq != 0).astype(F32)
    xn_ref[0:FFN_HALO, :] = (_rmsnorm_rows(halo_ref[...], g) * keep).astype(BF16)
    xn_ref[FFN_HALO:, :] = _rmsnorm_rows(x, g).astype(BF16)
    xn = xn_ref[...]
    rows = xn.shape[0]

    def conv(c0):
        hcol = jnp.dot(xn, wu_ref[:, c0:c0 + _FF_CHUNK], preferred_element_type=F32)
        out = (cw_ref[2:3, c0:c0 + _FF_CHUNK] * hcol
               + cw_ref[1:2, c0:c0 + _FF_CHUNK] * pltpu.roll(hcol, 1, axis=0)
               + cw_ref[0:1, c0:c0 + _FF_CHUNK] * pltpu.roll(hcol, 2, axis=0)
               + cb_ref[:, c0:c0 + _FF_CHUNK])
        return out[FFN_HALO:rows]

    acc = x
    for c in range(0, D_FF, _FF_CHUNK):
        a = conv(c)
        u = conv(D_FF + c)
        act = (a * jax.nn.sigmoid(a) * u).astype(BF16)
        acc = acc + jnp.dot(act, wd_ref[c:c + _FF_CHUNK, :], preferred_element_type=F32)
    if final_norm:
        acc = _rmsnorm_rows(acc, gf_ref[...])
    y_ref[...] = acc


def _ffn(x2, g, wu, cw, cb, wd, gf, *, seq, final_norm):
    t = x2.shape[0]
    tm = ROW_TILE
    const = lambda i: (0, 0)
    halo_blocks = tm // FFN_HALO
    return pl.pallas_call(
        functools.partial(_ffn_kernel, tiles_per_seq=seq // tm, final_norm=final_norm),
        out_shape=jax.ShapeDtypeStruct((t, D_MODEL), F32),
        grid=(t // tm,),
        in_specs=[pl.BlockSpec((tm, D_MODEL), lambda i: (i, 0)),
                  pl.BlockSpec((FFN_HALO, D_MODEL), lambda i: (jnp.maximum(i * halo_blocks - 1, 0), 0)),
                  pl.BlockSpec((1, D_MODEL), const),
                  pl.BlockSpec(wu.shape, const), pl.BlockSpec(cw.shape, const),
                  pl.BlockSpec(cb.shape, const), pl.BlockSpec(wd.shape, const),
                  pl.BlockSpec((1, D_MODEL), const)],
        out_specs=pl.BlockSpec((tm, D_MODEL), lambda i: (i, 0)),
        scratch_shapes=[pltpu.VMEM((tm + FFN_HALO, D_MODEL), BF16)],
        compiler_params=_params(1),
        name="conv_ffn",
    )(x2, x2, g, wu, cw, cb, wd, gf)


def kernel(x, rel_bias, norm_mix, w_in, cmp_pe_k, cmp_w1_k, cmp_w2_k, cmp_pe_v, cmp_w1_v, cmp_w2_v,
           w_branch, w_out, norm_ffn, w_up, conv_w, conv_b, w_down, norm_final):
    b, s, d = x.shape
    depth = w_in.shape[0]
    t = b * s
    assert d == D_MODEL and s % T_TILE == 0 and s // NSA_SLC_BLOCK == _N_SLC and t % ROW_TILE == 0
    assert s % ROW_TILE == 0 and all(s % (dil * A_TILE) == 0 for _, dil in A_GROUPS)
    assert all(win // dil == A_TILE for win, dil in A_GROUPS) and A_GROUPS[0][1] == 1
    assert all(ROW_TILE % dil == 0 for _, dil in A_GROUPS)

    hg = A_HEADS_PER_GROUP
    bias_a = jnp.concatenate([
        _bias_tiles(rel_bias,
                    np.concatenate([_bucket_tile(A_TILE, 1, dil, win // dil),
                                    _bucket_tile(A_TILE, 0, dil, win // dil)], axis=1),
                    gi * hg, hg)
        for gi, (win, dil) in enumerate(A_GROUPS)], axis=0)
    assert (_bucket_tile(T_TILE, 2, 1, 3 * T_TILE) == REL_BUCKETS - 1).all()
    bias_b = _bias_tiles_t(rel_bias, [_bucket_tile(T_TILE, d, 1, NSA_WINDOW - 1) for d in (0, 1, 2)],
                           A_HEADS, B_HEADS)
    bias_b = bias_b.reshape(B_KV_HEADS, B_GROUP, 3, T_TILE, T_TILE).transpose(0, 2, 3, 1, 4).reshape(
        B_KV_HEADS, 3, T_TILE, _G_ROWS)
    bias_c = _bias_tiles_t(rel_bias, [_bucket_tile(T_TILE, d, 1, s) for d in (0, 1)],
                           A_HEADS + B_HEADS, C_HEADS)

    w_in_k = _prep_w_in(w_in)
    x2 = x.reshape(t, d)
    for i in range(depth):
        pa0, pa1, pa2, pb, pcmp, pc, mg, bg = _proj_in(x2, norm_mix[i][None, :], w_in_k, layer=i,
                                                       seq=s)

        oa = _mixer_a(pa0.reshape(b, s, _GRP_COLS), pa1, pa2, bias_a)

        cmp = _compress(pcmp.reshape(4, b, s // NSA_CMP_STRIDE, NSA_CMP_STRIDE * HEAD_DIM),
                        cmp_pe_k[i].reshape(1, -1), cmp_pe_v[i].reshape(1, -1),
                        cmp_w1_k[i].astype(BF16), cmp_w1_v[i].astype(BF16),
                        cmp_w2_k[i].astype(BF16), cmp_w2_v[i].astype(BF16))
        ob = _nsa(pb.reshape(b, s, _PB_COLS), cmp, bg.reshape(b, s, _BG_COLS), bias_b)

        oc = _moba(pc.reshape(b, s, _C_COLS), bias_c)

        x2 = _merge(oa.reshape(t, -1), ob.reshape(t, -1), oc.reshape(t, -1), mg, x2,
                    w_branch[i].astype(BF16), w_out[i].astype(BF16))
        x2 = _ffn(x2, norm_ffn[i][None, :], w_up[i].astype(BF16), conv_w[i], conv_b[i][None, :],
                  w_down[i].astype(BF16), norm_final[None, :], seq=s, final_norm=(i == depth - 1))
    return x2.reshape(b, s, d)
```

```python
import functools
import math

import jax
import jax.numpy as jnp
import numpy as np
from jax import lax
from jax.experimental import pallas as pl
from jax.experimental.pallas import tpu as pltpu

F32 = jnp.float32
BF16 = jnp.bfloat16

D_MODEL = 1024
HEAD_DIM = 64
A_GROUPS = ((128, 1), (512, 4), (2048, 16))
A_HEADS_PER_GROUP = 2
A_HEADS = 6
B_HEADS = 6
B_KV_HEADS = 2
B_GROUP = 3
C_HEADS = 4
NSA_CMP_BLOCK = 32
NSA_CMP_STRIDE = 16
NSA_CMP_HIDDEN = 256
NSA_SLC_BLOCK = 64
NSA_SLC_TOPK = 16
NSA_WINDOW = 512
MOBA_BLOCK = 256
MOBA_TOPK = 3
REL_BUCKETS = 32
REL_MAX_DIST = 128
D_FF = 2816
RMS_EPS = 1e-6
NEG_INF = -1e30
FORCE_BONUS = 1e4
ATTN_SCALE = HEAD_DIM ** -0.5

LANE = 128
A_TILE = 128
T_TILE = 256
ROW_TILE = 512
FFN_HALO = 8
VMEM_LIMIT = 56 * 1024 * 1024

_GRP_COLS = 3 * A_HEADS_PER_GROUP * HEAD_DIM
_A_COLS = len(A_GROUPS) * _GRP_COLS
_PB_COLS = B_HEADS * HEAD_DIM + 4 * B_KV_HEADS * HEAD_DIM
_CMP_COLS = 2 * B_KV_HEADS * HEAD_DIM
_C_COLS = 3 * C_HEADS * HEAD_DIM
_MG_COLS = 3 * D_MODEL
_BG_COLS = LANE
_OFF_A = 0
_OFF_PB = _OFF_A + _A_COLS
_OFF_CMP = _OFF_PB + _PB_COLS
_OFF_C = _OFF_CMP + _CMP_COLS
_OFF_MG = _OFF_C + _C_COLS
_OFF_BG = _OFF_MG + _MG_COLS
_W_COLS = _OFF_BG + _BG_COLS
_PROJ_CHUNK = 4 * LANE


def _params(n_grid):
    return pltpu.CompilerParams(dimension_semantics=("arbitrary",) * n_grid,
                                vmem_limit_bytes=VMEM_LIMIT)


def _rel_bucket_np(dist):
    n = np.maximum(dist, 0)
    exact = REL_BUCKETS // 2
    nf = np.maximum(n, 1).astype(np.float32)
    large = exact + (np.log(nf / np.float32(exact)) / np.float32(math.log(REL_MAX_DIST / exact))
                     * np.float32(REL_BUCKETS - exact)).astype(np.int32)
    return np.where(n < exact, n, np.minimum(large, REL_BUCKETS - 1)).astype(np.int32)


def _bucket_tile(tile, block_offset, dil, max_rel):
    rel = block_offset * tile + np.arange(tile)[:, None] - np.arange(tile)[None, :]
    ok = (rel >= 0) & (rel <= max_rel)
    return np.where(ok, _rel_bucket_np(rel * dil), -1).astype(np.int32)


def _bias_tiles_kernel(tbl_ref, idx_ref, o_ref, *, head0, relative):
    h = pl.program_id(0) + head0
    idx = idx_ref[...]
    acc = jnp.full(idx.shape, NEG_INF, F32)
    for b in range(REL_BUCKETS):
        acc = jnp.where(idx == b, tbl_ref[b, h], acc)
    if relative:
        acc = acc - tbl_ref[REL_BUCKETS - 1, h]
    o_ref[0] = acc


def _bias_tiles(rel_bias, idx, head0, n_heads, relative=False):
    r, c = idx.shape
    return pl.pallas_call(
        functools.partial(_bias_tiles_kernel, head0=head0, relative=relative),
        out_shape=jax.ShapeDtypeStruct((n_heads, r, c), F32),
        grid=(n_heads,),
        in_specs=[pl.BlockSpec(memory_space=pltpu.SMEM),
                  pl.BlockSpec((r, c), lambda h: (0, 0))],
        out_specs=pl.BlockSpec((1, r, c), lambda h: (h, 0, 0)),
        compiler_params=_params(1),
        name="bias_tiles",
    )(rel_bias, jnp.asarray(idx))


def _bias_tiles_t(rel_bias, idx_tiles, head0, n_heads):
    r, c = idx_tiles[0].shape
    flat = _bias_tiles(rel_bias, np.concatenate([t.T for t in idx_tiles], axis=1), head0, n_heads,
                       relative=True)
    return flat.reshape(n_heads, c, len(idx_tiles), r).transpose(0, 2, 1, 3)


def _rmsnorm_rows(x, g):
    return x * lax.rsqrt(jnp.mean(x * x, axis=-1, keepdims=True) + RMS_EPS) * g


def _proj_in_kernel(x_ref, g_ref, w_ref, pa0_ref, pa1_ref, pa2_ref, pb_ref, pcmp_ref, pc_ref,
                    mg_ref, bg_ref, regroup_ref):
    xn = _rmsnorm_rows(x_ref[...], g_ref[...]).astype(BF16)
    tm = xn.shape[0]

    def direct(out_ref):
        def sink(lo, hi, val):
            out_ref[:, lo:hi] = val.astype(out_ref.dtype)
        return sink

    def staged(slot0):
        def sink(lo, hi, val):
            for c in range(lo, hi, LANE):
                regroup_ref[slot0 + c // LANE] = val[:, c - lo:c - lo + LANE]
        return sink

    g1_slot, g2_slot, cmp_slot = 0, 3, 6
    sections = ((_OFF_A, _GRP_COLS, direct(pa0_ref)),
                (_OFF_A + _GRP_COLS, _GRP_COLS, staged(g1_slot)),
                (_OFF_A + 2 * _GRP_COLS, _GRP_COLS, staged(g2_slot)),
                (_OFF_PB, _PB_COLS, direct(pb_ref)), (_OFF_CMP, _CMP_COLS, staged(cmp_slot)),
                (_OFF_C, _C_COLS, direct(pc_ref)), (_OFF_MG, _MG_COLS, direct(mg_ref)),
                (_OFF_BG, _BG_COLS, direct(bg_ref)))
    for c0 in range(0, _W_COLS, _PROJ_CHUNK):
        c1 = min(c0 + _PROJ_CHUNK, _W_COLS)
        res = jnp.dot(xn, w_ref[:, c0:c1], preferred_element_type=F32)
        for off, width, sink in sections:
            lo, hi = max(c0, off), min(c1, off + width)
            if lo < hi:
                sink(lo - off, hi - off, res[:, lo - c0:hi - c0])

    for gi, slot0, out_ref in ((1, g1_slot, pa1_ref), (2, g2_slot, pa2_ref)):
        dil = A_GROUPS[gi][1]
        for r in range(dil):
            for j in range(3):
                out_ref[0, r, :, j * LANE:(j + 1) * LANE] = (
                    regroup_ref[slot0 + j, pl.ds(r, tm // dil, stride=dil), :].astype(BF16))
    for l in range(NSA_CMP_STRIDE):
        for kv in range(2):
            a = regroup_ref[cmp_slot + kv, pl.ds(l, tm // NSA_CMP_STRIDE, stride=NSA_CMP_STRIDE),
                            :].astype(BF16)
            for h in range(B_KV_HEADS):
                pcmp_ref[kv * B_KV_HEADS + h, :, l * HEAD_DIM:(l + 1) * HEAD_DIM] = (
                    a[:, h * HEAD_DIM:(h + 1) * HEAD_DIM])


def _proj_in(x2, g, w, *, layer, seq):
    t = x2.shape[0]
    tm = ROW_TILE
    b = t // seq
    tps = seq // tm
    row = lambda i: (i, 0)
    d1, d2 = A_GROUPS[1][1], A_GROUPS[2][1]
    sub = lambda i: (i // tps, 0, i % tps, 0)
    return pl.pallas_call(
        _proj_in_kernel,
        out_shape=(jax.ShapeDtypeStruct((t, _GRP_COLS), BF16),
                   jax.ShapeDtypeStruct((b, d1, seq // d1, _GRP_COLS), BF16),
                   jax.ShapeDtypeStruct((b, d2, seq // d2, _GRP_COLS), BF16),
                   jax.ShapeDtypeStruct((t, _PB_COLS), BF16),
                   jax.ShapeDtypeStruct((4, t // NSA_CMP_STRIDE, NSA_CMP_STRIDE * HEAD_DIM), BF16),
                   jax.ShapeDtypeStruct((t, _C_COLS), BF16),
                   jax.ShapeDtypeStruct((t, _MG_COLS), BF16),
                   jax.ShapeDtypeStruct((t, _BG_COLS), F32)),
        grid=(t // tm,),
        in_specs=[pl.BlockSpec((tm, D_MODEL), row),
                  pl.BlockSpec((1, D_MODEL), lambda i: (0, 0)),
                  pl.BlockSpec((None, D_MODEL, _W_COLS), lambda i: (layer, 0, 0))],
        out_specs=(pl.BlockSpec((tm, _GRP_COLS), row),
                   pl.BlockSpec((1, d1, tm // d1, _GRP_COLS), sub),
                   pl.BlockSpec((1, d2, tm // d2, _GRP_COLS), sub),
                   pl.BlockSpec((tm, _PB_COLS), row),
                   pl.BlockSpec((4, tm // NSA_CMP_STRIDE, NSA_CMP_STRIDE * HEAD_DIM),
                                lambda i: (0, i, 0)),
                   pl.BlockSpec((tm, _C_COLS), row),
                   pl.BlockSpec((tm, _MG_COLS), row),
                   pl.BlockSpec((tm, _BG_COLS), row)),
        scratch_shapes=[pltpu.VMEM((8, tm, LANE), F32)],
        compiler_params=_params(1),
        name="proj_in",
    )(x2, g, w)


def _w_in_segments():
    a = A_HEADS * HEAD_DIM
    bq0 = 3 * a
    bkc0 = bq0 + B_HEADS * HEAD_DIM
    bks0 = bkc0 + 2 * B_KV_HEADS * HEAD_DIM
    bg0 = bks0 + 4 * B_KV_HEADS * HEAD_DIM
    c0 = bg0 + B_HEADS * 3
    mg0 = c0 + _C_COLS
    gw = A_HEADS_PER_GROUP * HEAD_DIM
    segs = [(_OFF_A + (gi * 3 + j) * gw, j * a + gi * gw, gw)
            for gi in range(len(A_GROUPS)) for j in range(3)]
    segs += [(_OFF_PB, bq0, B_HEADS * HEAD_DIM),
             (_OFF_PB + B_HEADS * HEAD_DIM, bks0, bg0 - bks0),
             (_OFF_CMP, bkc0, _CMP_COLS), (_OFF_C, c0, _C_COLS), (_OFF_MG, mg0, _MG_COLS),
             (_OFF_BG, bg0, B_HEADS * 3)]
    return segs


def _prep_w_in_kernel(w_ref, o_ref):
    o_ref[0, :, _OFF_BG:] = jnp.zeros((o_ref.shape[1], _BG_COLS), BF16)
    for dst, src, width in _w_in_segments():
        for c in range(0, width, _PROJ_CHUNK):
            n = min(_PROJ_CHUNK, width - c)
            o_ref[0, :, dst + c:dst + c + n] = w_ref[0, :, src + c:src + c + n].astype(BF16)


def _prep_w_in(w):
    depth, rows, cols = w.shape
    rb = 128
    return pl.pallas_call(
        _prep_w_in_kernel,
        out_shape=jax.ShapeDtypeStruct((depth, rows, _W_COLS), BF16),
        grid=(depth, rows // rb),
        in_specs=[pl.BlockSpec((1, rb, cols), lambda l, r: (l, r, 0))],
        out_specs=pl.BlockSpec((1, rb, _W_COLS), lambda l, r: (l, r, 0)),
        compiler_params=_params(2),
        name="prep_w_in",
    )(w)


_A_UNROLL = 3


def _mixer_a_kernel(pa0_ref, pa1_ref, pa2_ref, bias_ref, o_ref, on_ref, ln_ref, *, seq):
    gw = A_HEADS_PER_GROUP * HEAD_DIM

    def band_block(src, l0, first, gi, rows):
        outs, lses = [], []
        for h in range(A_HEADS_PER_GROUP):
            qc, kc, vc = (slice(j * gw + h * HEAD_DIM, j * gw + (h + 1) * HEAD_DIM) for j in range(3))
            q = src[pl.ds(l0, A_TILE), qc] * ATTN_SCALE
            if first:
                kr = pl.ds(l0, A_TILE)
                bias = bias_ref[A_HEADS_PER_GROUP * gi + h, :, A_TILE:]
            else:
                kr = pl.ds(l0 - A_TILE, 2 * A_TILE)
                bias = bias_ref[A_HEADS_PER_GROUP * gi + h]
            s = _nt(q, src[kr, kc]) + bias
            m = jnp.max(s, axis=-1, keepdims=True)
            e = jnp.exp(s - m)
            l = jnp.sum(e, axis=-1, keepdims=True)
            outs.append(jnp.dot(e.astype(BF16), src[kr, vc], preferred_element_type=F32) / l)
            lses.append(jnp.broadcast_to(m + jnp.log(l), (A_TILE, HEAD_DIM)))
        on_ref[gi, rows, :] = jnp.concatenate(outs, axis=-1)
        ln_ref[gi, rows, :] = jnp.concatenate(lses, axis=-1)

    for gi, ((_, dil), pa_ref) in enumerate(zip(A_GROUPS, (pa0_ref, pa1_ref, pa2_ref))):
        n_blocks = seq // dil // A_TILE
        if dil == 1:
            band_block(pa_ref.at[0], 0, True, gi, pl.ds(0, A_TILE))
            assert (n_blocks - 1) % _A_UNROLL == 0

            def body0(it, c, pa_ref=pa_ref, gi=gi):
                for u in range(_A_UNROLL):
                    l0 = pl.multiple_of((1 + it * _A_UNROLL + u) * A_TILE, A_TILE)
                    band_block(pa_ref.at[0], l0, False, gi, pl.ds(l0, A_TILE))
                return c
            lax.fori_loop(0, (n_blocks - 1) // _A_UNROLL, body0, 0)
        elif n_blocks > 1:
            for r in range(dil):
                band_block(pa_ref.at[0, r], 0, True, gi, pl.ds(r, A_TILE, stride=dil))

            def body1(i, c, pa_ref=pa_ref, gi=gi, dil=dil):
                l0 = pl.multiple_of(i * A_TILE, A_TILE)
                for r in range(dil):
                    band_block(pa_ref.at[0, r], l0, False, gi, pl.ds(l0 * dil + r, A_TILE, stride=dil))
                return c
            lax.fori_loop(1, n_blocks, body1, 0)
        else:
            per_iter = 4

            def body2(it, c, pa_ref=pa_ref, gi=gi, dil=dil):
                for u in range(per_iter):
                    r = it * per_iter + u
                    band_block(pa_ref.at[0, r], 0, True, gi, pl.ds(r, A_TILE, stride=dil))
                return c
            lax.fori_loop(0, dil // per_iter, body2, 0)

    chunk = T_TILE
    for c0 in range(0, seq, chunk):
        rows = slice(c0, c0 + chunk)
        lse = [ln_ref[g, rows, :] for g in range(len(A_GROUPS))]
        top = functools.reduce(jnp.maximum, lse)
        ex = [jnp.exp(v - top) for v in lse]
        den = functools.reduce(lambda a, b: a + b, ex)
        for g in range(len(A_GROUPS)):
            o_ref[0, rows, g * gw:(g + 1) * gw] = (ex[g] / den * on_ref[g, rows, :]).astype(BF16)


def _mixer_a(pa0, pa1, pa2, bias):
    b, seq, _ = pa0.shape
    return pl.pallas_call(
        functools.partial(_mixer_a_kernel, seq=seq),
        out_shape=jax.ShapeDtypeStruct((b, seq, _GRP_COLS), BF16),
        grid=(b,),
        in_specs=[pl.BlockSpec((1, seq, _GRP_COLS), lambda i: (i, 0, 0)),
                  pl.BlockSpec((1,) + pa1.shape[1:], lambda i: (i, 0, 0, 0)),
                  pl.BlockSpec((1,) + pa2.shape[1:], lambda i: (i, 0, 0, 0)),
                  pl.BlockSpec(bias.shape, lambda i: (0, 0, 0))],
        out_specs=pl.BlockSpec((1, seq, _GRP_COLS), lambda i: (i, 0, 0)),
        scratch_shapes=[pltpu.VMEM((len(A_GROUPS), seq, LANE), F32),
                        pltpu.VMEM((len(A_GROUPS), seq, LANE), F32)],
        compiler_params=_params(1),
        name="mixer_a",
    )(pa0, pa1, pa2, bias)


def _compress_kernel(r_ref, pek_ref, pev_ref, w1k_ref, w1v_ref, w2k_ref, w2v_ref, o_ref):
    half = NSA_CMP_STRIDE * HEAD_DIM
    for kv, (pe_ref, w1_ref, w2_ref) in enumerate(((pek_ref, w1k_ref, w2k_ref),
                                                   (pev_ref, w1v_ref, w2v_ref))):
        for h in range(B_KV_HEADS):
            r = r_ref[kv * B_KV_HEADS + h, 0].astype(F32)
            lo = jnp.dot((r + pe_ref[:, :half]).astype(BF16), w1_ref[:half, :],
                         preferred_element_type=F32)
            hi = jnp.dot((r + pe_ref[:, half:]).astype(BF16), w1_ref[half:, :],
                         preferred_element_type=F32)
            hid = lo + pltpu.roll(hi, hi.shape[0] - 1, axis=0)
            act = jax.nn.gelu(hid).astype(BF16)
            o_ref[0, kv * B_KV_HEADS + h] = jnp.dot(act, w2_ref[...],
                                                    preferred_element_type=F32).astype(BF16)


def _compress(r, pe_k, pe_v, w1_k, w1_v, w2_k, w2_v):
    _, b, m, c = r.shape
    const2 = lambda i: (0, 0)
    return pl.pallas_call(
        _compress_kernel,
        out_shape=jax.ShapeDtypeStruct((b, 4, m, HEAD_DIM), BF16),
        grid=(b,),
        in_specs=[pl.BlockSpec((4, 1, m, c), lambda i: (0, i, 0, 0)),
                  pl.BlockSpec(pe_k.shape, const2), pl.BlockSpec(pe_v.shape, const2),
                  pl.BlockSpec(w1_k.shape, const2), pl.BlockSpec(w1_v.shape, const2),
                  pl.BlockSpec(w2_k.shape, const2), pl.BlockSpec(w2_v.shape, const2)],
        out_specs=pl.BlockSpec((1, 4, m, HEAD_DIM), lambda i: (i, 0, 0, 0)),
        compiler_params=_params(1),
        name="nsa_compress",
    )(r, pe_k, pe_v, w1_k, w1_v, w2_k, w2_v)


MASK_BIG = 2.0 ** 100
SEL_PAD = 32


def _nt(a, b, **kw):
    return lax.dot_general(a, b, (((1,), (1,)), ((), ())), preferred_element_type=F32, **kw)


def _top_mask_t(score, n_rows, n_top):
    sub = lax.broadcasted_iota(jnp.int32, score.shape, 0)
    rank = jnp.zeros(score.shape, F32)
    for j in range(n_rows):
        row = score[j:j + 1, :]
        ge = jnp.where(row >= score, 1.0, 0.0)
        gt = jnp.where(row > score, 1.0, 0.0)
        rank = rank + jnp.where(sub > j, ge, gt)
    return rank < n_top


def _top_keep(score, n_rows, n_top, block):
    free = n_top * block
    ranked = jnp.where(_top_mask_t(score[:, free:], n_rows, n_top), 1.0, 0.0)
    return jnp.concatenate([jnp.ones((score.shape[0], free), F32), ranked], axis=1)


BAND_SLOTS = 4


def _loop_pairs(lo, hi, body):
    n = jnp.maximum(hi - lo, 0)

    def pair(p, c):
        body(lo + 2 * p)
        body(lo + 2 * p + 1)
        return c
    lax.fori_loop(0, n // 2, pair, 0)
    pl.when(n % 2 == 1)(lambda: body(hi - 1))


def _two_pass_attention(i, streams, band_streams=()):
    groups = ((streams, lambda j: j), (band_streams, lambda j: j % BAND_SLOTS))

    def key_rows(j):
        return pl.ds(pl.multiple_of(j * T_TILE, T_TILE), T_TILE)

    def logits(group, j, d):
        some, slot = group
        for k_tile, q_t, _, bias_tile, s_ref, rmp_ref, _ in some:
            s = jnp.dot(k_tile(key_rows(j)), q_t(), preferred_element_type=F32)
            if d is not None:
                s = s + bias_tile(d)
            s_ref[slot(j)] = s
            fold = jnp.max(s.reshape(s.shape[0] // 8, 8, s.shape[1]), axis=0)
            rmp_ref[...] = fold if d == 0 else jnp.maximum(rmp_ref[...], fold)

    def near(n_full, n_band):
        for d in range(n_full):
            logits(groups[0], i - d, d)
        for d in range(n_band):
            logits(groups[1], i - d, d)

    if band_streams:
        pl.when(i == 0)(lambda: near(1, 1))
        pl.when(i == 1)(lambda: near(2, 2))
        pl.when(i >= 2)(lambda: near(2, 3))
    else:
        logits(groups[0], i, 0)
        pl.when(i >= 1)(lambda: logits(groups[0], i - 1, 1))

    _loop_pairs(0, i - 1, lambda j: logits(groups[0], j, None))

    tops = {}
    for some, _ in groups:
        for stream in some:
            rmp_ref, acc_ref = stream[-2:]
            tops[id(rmp_ref)] = jnp.max(rmp_ref[...], axis=0, keepdims=True)
            acc_ref[...] = jnp.zeros(acc_ref.shape, F32)

    def pv(active):
        def body(j):
            for some, slot in active:
                for _, _, vt_ref, _, s_ref, rmp_ref, acc_ref in some:
                    e = jnp.exp(s_ref[slot(j)] - tops[id(rmp_ref)])
                    acc_ref[...] += jnp.dot(vt_ref[j], e.astype(BF16), preferred_element_type=F32)
        return body

    if band_streams:
        band_lo = jnp.maximum(i - 2, 0)
        _loop_pairs(0, band_lo, pv(groups[:1]))
        pv_all = pv(groups)

        def last_tiles(n):
            for d in reversed(range(n)):
                pv_all(i - d)

        pl.when(i == 0)(lambda: last_tiles(1))
        pl.when(i == 1)(lambda: last_tiles(2))
        pl.when(i >= 2)(lambda: last_tiles(3))
    else:
        _loop_pairs(0, i + 1, pv(groups[:1]))

    outs = []
    for some, _ in groups:
        for *_, acc_ref in some:
            acc = acc_ref[...]
            outs.append(acc[:HEAD_DIM, :] / acc[HEAD_DIM:HEAD_DIM + 1, :])
    return outs


def _eye(n, dtype=BF16):
    return (lax.broadcasted_iota(jnp.int32, (n, n), 0)
            == lax.broadcasted_iota(jnp.int32, (n, n), 1)).astype(dtype)


def _transposed_values(v_ref_rows, n_tiles):
    eye = _eye(LANE)
    for j in range(n_tiles):
        v = _with_ones_column(v_ref_rows(slice(j * T_TILE, (j + 1) * T_TILE)))
        yield j, _nt(eye, v).astype(BF16)


def _with_ones_column(v):
    one = (lax.broadcasted_iota(jnp.int32, v.shape, 1) == 0).astype(F32)
    return jnp.concatenate([v.astype(F32), one], axis=1).astype(BF16)


def _with_block_onehot(k, block):
    n = k.shape[0]
    shape = (n, SEL_PAD)
    hot = (lax.broadcasted_iota(jnp.int32, shape, 0) // block
           == lax.broadcasted_iota(jnp.int32, shape, 1)).astype(F32)
    return jnp.concatenate([k.astype(F32), hot, jnp.zeros(shape, F32)], axis=1).astype(BF16)


def _augment_q_t(q_t, allowed_t):
    pen = (allowed_t - 1.0) * MASK_BIG
    return jnp.concatenate([q_t, pen, jnp.zeros(pen.shape, F32)], axis=0).astype(BF16)


_N_SLC = 32
_G_ROWS = B_GROUP * T_TILE


def _nsa_kernel(pb_ref, cmp_ref, bg_ref, bias_ref, o_ref,
                kaug_ref, vst_ref, vwt_ref, eye_ref, sig_ref, q3t_ref, qaugt_ref, s_ref, sw_ref,
                rmp_ref, acc_ref, ocmp_ref, *, seq):
    n_tiles = seq // T_TILE
    n_cmp_rows = seq // NSA_CMP_STRIDE
    q_cols = B_HEADS * HEAD_DIM
    ks0, vs0, kw0, vw0 = (q_cols + i * B_KV_HEADS * HEAD_DIM for i in range(4))

    def head_cols(c0, h):
        return slice(c0 + h * HEAD_DIM, c0 + (h + 1) * HEAD_DIM)

    j_id = lax.broadcasted_iota(jnp.int32, (_N_SLC, n_cmp_rows), 0)
    c_id = lax.broadcasted_iota(jnp.int32, (_N_SLC, n_cmp_rows), 1)
    overlap_t = ((c_id * NSA_CMP_STRIDE < (j_id + 1) * NSA_SLC_BLOCK)
                 & (c_id * NSA_CMP_STRIDE + NSA_CMP_BLOCK > j_id * NSA_SLC_BLOCK)).astype(F32)
    eye_ref[...] = _eye(T_TILE)
    for h in range(B_KV_HEADS):
        kaug_ref[h] = _with_block_onehot(pb_ref[0, :, head_cols(ks0, h)], NSA_SLC_BLOCK)
        for j, vt in _transposed_values(lambda r, h=h: pb_ref[0, r, head_cols(vs0, h)], n_tiles):
            vst_ref[h, j] = vt
        for j, vt in _transposed_values(lambda r, h=h: pb_ref[0, r, head_cols(vw0, h)], n_tiles):
            vwt_ref[h, j] = vt

    def tiles(a):
        return [a[:, j * T_TILE:(j + 1) * T_TILE] for j in range(n_tiles)]

    pick = (lax.broadcasted_iota(jnp.int32, (SEL_PAD, _BG_COLS), 0)
            == lax.broadcasted_iota(jnp.int32, (SEL_PAD, _BG_COLS), 1)).astype(F32)
    sig_all = jax.nn.sigmoid(_nt(pick, bg_ref[0], precision=lax.Precision.HIGHEST))
    for j, t in enumerate(tiles(sig_all)):
        sig_ref[j] = t

    eye_h = _eye(HEAD_DIM)
    t_lane = lax.broadcasted_iota(jnp.int32, (n_cmp_rows, seq), 1)
    c_end = (lax.broadcasted_iota(jnp.int32, (n_cmp_rows, seq), 0) * NSA_CMP_STRIDE
             + NSA_CMP_BLOCK - 1)
    ok = (c_end <= t_lane) & (c_end < seq)
    ok_f = ok.astype(F32)
    j_sub = lax.broadcasted_iota(jnp.int32, (_N_SLC, seq), 0)
    jt = lax.broadcasted_iota(jnp.int32, (_N_SLC, seq), 1) // NSA_SLC_BLOCK
    forced = ((j_sub == 0) | (j_sub == jt) | (j_sub == jt - 1)).astype(F32)
    valid = j_sub <= jt

    for h in range(B_KV_HEADS):
        kc = cmp_ref[0, h]
        vct = _nt(eye_h, cmp_ref[0, B_KV_HEADS + h]).astype(BF16)
        q_ts, p_sum = [], None
        for g in range(B_GROUP):
            q_t = _nt(eye_h, pb_ref[0, :, head_cols(0, h * B_GROUP + g)]) * ATTN_SCALE
            q_ts.append(q_t)
            lc = jnp.where(ok, jnp.dot(kc, q_t.astype(BF16), preferred_element_type=F32), NEG_INF)
            ec = jnp.exp(lc - jnp.max(lc, axis=0, keepdims=True))
            pc = ec / jnp.sum(ec, axis=0, keepdims=True) * ok_f
            o_cmp = jnp.dot(vct, pc.astype(BF16), preferred_element_type=F32)
            for j, t in enumerate(tiles(o_cmp)):
                ocmp_ref[h, j, :, g * T_TILE:(g + 1) * T_TILE] = t
            p_sum = pc if p_sum is None else p_sum + pc
        imp_t = jnp.dot(overlap_t, p_sum, precision=lax.Precision.HIGHEST,
                        preferred_element_type=F32)
        score = jnp.where(valid, imp_t + FORCE_BONUS * forced, NEG_INF)
        sel_t = jnp.where(valid, _top_keep(score, _N_SLC, NSA_SLC_TOPK, NSA_SLC_BLOCK), 0.0)
        pen_tiles = tiles((sel_t - 1.0) * MASK_BIG)
        q_tiles = [tiles(q_t) for q_t in q_ts]
        for j in range(n_tiles):
            q3t = jnp.concatenate([q_tiles[g][j] for g in range(B_GROUP)], axis=1)
            pen = jnp.concatenate([pen_tiles[j]] * B_GROUP, axis=1)
            q3t_ref[h, j] = q3t.astype(BF16)
            qaugt_ref[h, j] = jnp.concatenate([q3t, pen, jnp.zeros(pen.shape, F32)],
                                              axis=0).astype(BF16)

    def tile_body(i, carry):
        rows = pl.ds(pl.multiple_of(i * T_TILE, T_TILE), T_TILE)

        outs = _two_pass_attention(
            i,
            [(lambda kr, h=h: kaug_ref[h, kr, :], lambda h=h: qaugt_ref[h, i], vst_ref.at[h],
              lambda d, h=h: bias_ref[h, d], s_ref.at[h], rmp_ref.at[0, h], acc_ref.at[0, h])
             for h in range(B_KV_HEADS)],
            [(lambda kr, h=h: pb_ref[0, kr, head_cols(kw0, h)], lambda h=h: q3t_ref[h, i],
              vwt_ref.at[h], lambda d, h=h: bias_ref[h, d], sw_ref.at[h], rmp_ref.at[1, h],
              acc_ref.at[1, h]) for h in range(B_KV_HEADS)])
        oslc, owin = outs[:B_KV_HEADS], outs[B_KV_HEADS:]

        sig_t = sig_ref[i]
        heads = []
        for h in range(B_KV_HEADS):
            for g in range(B_GROUP):
                head = h * B_GROUP + g
                gc = slice(g * T_TILE, (g + 1) * T_TILE)
                heads.append(sig_t[3 * head:3 * head + 1, :] * ocmp_ref[h, i, :, gc]
                             + sig_t[3 * head + 1:3 * head + 2, :] * oslc[h][:, gc]
                             + sig_t[3 * head + 2:3 * head + 3, :] * owin[h][:, gc])
        o_t = jnp.concatenate(heads, axis=0).astype(BF16)
        o_ref[0, rows, :] = _nt(eye_ref[...], o_t).astype(o_ref.dtype)
        return carry

    lax.fori_loop(0, n_tiles, tile_body, 0)


def _nsa(pb, cmp, bg, bias):
    b, seq, _ = pb.shape
    n_tiles = seq // T_TILE
    return pl.pallas_call(
        functools.partial(_nsa_kernel, seq=seq),
        out_shape=jax.ShapeDtypeStruct((b, seq, B_HEADS * HEAD_DIM), BF16),
        grid=(b,),
        in_specs=[pl.BlockSpec((1, seq, _PB_COLS), lambda i: (i, 0, 0)),
                  pl.BlockSpec((1,) + cmp.shape[1:], lambda i: (i, 0, 0, 0)),
                  pl.BlockSpec((1, seq, _BG_COLS), lambda i: (i, 0, 0)),
                  pl.BlockSpec(bias.shape, lambda i: (0, 0, 0, 0))],
        out_specs=pl.BlockSpec((1, seq, B_HEADS * HEAD_DIM), lambda i: (i, 0, 0)),
        scratch_shapes=[pltpu.VMEM((B_KV_HEADS, seq, LANE), BF16),
                        pltpu.VMEM((B_KV_HEADS, n_tiles, LANE, T_TILE), BF16),
                        pltpu.VMEM((B_KV_HEADS, n_tiles, LANE, T_TILE), BF16),
                        pltpu.VMEM((T_TILE, T_TILE), BF16),
                        pltpu.VMEM((n_tiles, SEL_PAD, T_TILE), F32),
                        pltpu.VMEM((B_KV_HEADS, n_tiles, HEAD_DIM, _G_ROWS), BF16),
                        pltpu.VMEM((B_KV_HEADS, n_tiles, LANE, _G_ROWS), BF16),
                        pltpu.VMEM((B_KV_HEADS, n_tiles, T_TILE, _G_ROWS), F32),
                        pltpu.VMEM((B_KV_HEADS, BAND_SLOTS, T_TILE, _G_ROWS), F32),
                        pltpu.VMEM((2, B_KV_HEADS, 8, _G_ROWS), F32),
                        pltpu.VMEM((2, B_KV_HEADS, LANE, _G_ROWS), F32),
                        pltpu.VMEM((B_KV_HEADS, n_tiles, HEAD_DIM, _G_ROWS), F32)],
        compiler_params=_params(1),
        name="nsa",
    )(pb, cmp, bg, bias)


def _moba_kernel(pc_ref, bias_ref, o_ref, kaug_ref, vt_ref, eye_ref, qaugt_ref, s_ref,
                 rmp_ref, acc_ref, *, seq):
    n_blk = seq // MOBA_BLOCK
    width = C_HEADS * HEAD_DIM
    eye_ref[...] = _eye(T_TILE)

    def cols(j, h):
        return slice(j * width + h * HEAD_DIM, j * width + (h + 1) * HEAD_DIM)

    n_sub = lax.broadcasted_iota(jnp.int32, (SEL_PAD, seq), 0)
    own = lax.broadcasted_iota(jnp.int32, (SEL_PAD, seq), 1) // MOBA_BLOCK
    past = n_sub < own
    eye_h = _eye(HEAD_DIM)
    for h in range(C_HEADS):
        kaug_ref[h] = _with_block_onehot(pc_ref[0, :, cols(1, h)], MOBA_BLOCK)
        for j, vt in _transposed_values(lambda r, h=h: pc_ref[0, r, cols(2, h)], n_blk):
            vt_ref[h, j] = vt
        kmean = jnp.concatenate(
            [jnp.mean(pc_ref[0, n * MOBA_BLOCK:(n + 1) * MOBA_BLOCK, cols(1, h)].astype(F32), axis=0,
                      keepdims=True) for n in range(n_blk)]
            + [jnp.zeros((SEL_PAD - n_blk, HEAD_DIM), F32)], axis=0)
        q_t = _nt(eye_h, pc_ref[0, :, cols(0, h)])
        gate_t = jnp.dot(kmean, q_t, precision=lax.Precision.HIGHEST,
                         preferred_element_type=F32)
        score = jnp.where(past, gate_t, NEG_INF)
        picked = past & (_top_keep(score, n_blk, MOBA_TOPK, MOBA_BLOCK) > 0.5)
        allowed_t = jnp.where(picked | (n_sub == own), 1.0, 0.0)
        qaug_t = _augment_q_t(q_t * ATTN_SCALE, allowed_t)
        for j in range(n_blk):
            qaugt_ref[h, j] = qaug_t[:, j * T_TILE:(j + 1) * T_TILE]

    def tile_body(i, carry):
        rows = pl.ds(pl.multiple_of(i * MOBA_BLOCK, MOBA_BLOCK), MOBA_BLOCK)
        outs = _two_pass_attention(
            i, [(lambda kr, h=h: kaug_ref[h, kr, :], lambda h=h: qaugt_ref[h, i], vt_ref.at[h],
                 lambda d, h=h: bias_ref[h, d], s_ref.at[h], rmp_ref.at[h], acc_ref.at[h])
                for h in range(C_HEADS)])
        o_t = jnp.concatenate(outs, axis=0).astype(BF16)
        o_ref[0, rows, :] = _nt(eye_ref[...], o_t).astype(o_ref.dtype)
        return carry

    lax.fori_loop(0, n_blk, tile_body, 0)


def _moba(pc, bias):
    b, seq, _ = pc.shape
    width = C_HEADS * HEAD_DIM
    n_tiles = seq // T_TILE
    return pl.pallas_call(
        functools.partial(_moba_kernel, seq=seq),
        out_shape=jax.ShapeDtypeStruct((b, seq, width), BF16),
        grid=(b,),
        in_specs=[pl.BlockSpec((1, seq, _C_COLS), lambda i: (i, 0, 0)),
                  pl.BlockSpec(bias.shape, lambda i: (0, 0, 0, 0))],
        out_specs=pl.BlockSpec((1, seq, width), lambda i: (i, 0, 0)),
        scratch_shapes=[pltpu.VMEM((C_HEADS, seq, LANE), BF16),
                        pltpu.VMEM((C_HEADS, n_tiles, LANE, T_TILE), BF16),
                        pltpu.VMEM((T_TILE, T_TILE), BF16),
                        pltpu.VMEM((C_HEADS, n_tiles, LANE, T_TILE), BF16),
                        pltpu.VMEM((C_HEADS, n_tiles, T_TILE, T_TILE), F32),
                        pltpu.VMEM((C_HEADS, 8, T_TILE), F32),
                        pltpu.VMEM((C_HEADS, LANE, T_TILE), F32)],
        compiler_params=_params(1),
        name="moba",
    )(pc, bias)


def _merge_kernel(oa_ref, ob_ref, oc_ref, mg_ref, x_ref, wb_ref, wo_ref, y_ref):
    r0 = A_HEADS * HEAD_DIM
    r1 = r0 + B_HEADS * HEAD_DIM
    ya = jnp.dot(oa_ref[...], wb_ref[0:r0, :], preferred_element_type=F32)
    yb = jnp.dot(ob_ref[...], wb_ref[r0:r1, :], preferred_element_type=F32)
    yc = jnp.dot(oc_ref[...], wb_ref[r1:, :], preferred_element_type=F32)
    merged = (jax.nn.sigmoid(mg_ref[:, 0:D_MODEL].astype(F32)) * ya
              + jax.nn.sigmoid(mg_ref[:, D_MODEL:2 * D_MODEL].astype(F32)) * yb
              + jax.nn.sigmoid(mg_ref[:, 2 * D_MODEL:].astype(F32)) * yc)
    y_ref[...] = x_ref[...] + jnp.dot(merged.astype(BF16), wo_ref[...], preferred_element_type=F32)


def _merge(oa, ob, oc, mg, x2, wb, wo):
    t = x2.shape[0]
    tm = ROW_TILE
    row = lambda i: (i, 0)
    return pl.pallas_call(
        _merge_kernel,
        out_shape=jax.ShapeDtypeStruct((t, D_MODEL), F32),
        grid=(t // tm,),
        in_specs=[pl.BlockSpec((tm, oa.shape[1]), row),
                  pl.BlockSpec((tm, ob.shape[1]), row), pl.BlockSpec((tm, oc.shape[1]), row),
                  pl.BlockSpec((tm, _MG_COLS), row), pl.BlockSpec((tm, D_MODEL), row),
                  pl.BlockSpec(wb.shape, lambda i: (0, 0)), pl.BlockSpec(wo.shape, lambda i: (0, 0))],
        out_specs=pl.BlockSpec((tm, D_MODEL), row),
        compiler_params=_params(1),
        name="merge",
    )(oa, ob, oc, mg, x2, wb, wo)


MXU_WIDTH = 256
_FF_SPLIT = (D_FF // MXU_WIDTH + 1) // 2 * MXU_WIDTH
_FF_CHUNKS = ((0, _FF_SPLIT), (_FF_SPLIT, D_FF))


def _ffn_kernel(x_ref, halo_ref, g_ref, wu_ref, cw_ref, cb_ref, wd_ref, gf_ref, y_ref, xn_ref,
                *, tiles_per_seq, final_norm):
    i = pl.program_id(0)
    x = x_ref[...]
    g = g_ref[...]
    keep = (i % tiles_per_seq != 0).astype(F32)
    xn_ref[0:FFN_HALO, :] = (_rmsnorm_rows(halo_ref[...], g) * keep).astype(BF16)
    xn_ref[FFN_HALO:, :] = _rmsnorm_rows(x, g).astype(BF16)
    xn = xn_ref[...]
    rows = xn.shape[0]

    def conv(c0, c1):
        hcol = jnp.dot(xn, wu_ref[:, c0:c1], preferred_element_type=F32)
        out = (cw_ref[2:3, c0:c1] * hcol
               + cw_ref[1:2, c0:c1] * pltpu.roll(hcol, 1, axis=0)
               + cw_ref[0:1, c0:c1] * pltpu.roll(hcol, 2, axis=0)
               + cb_ref[:, c0:c1])
        return out[FFN_HALO:rows]

    acc = x
    for c0, c1 in _FF_CHUNKS:
        a = conv(c0, c1)
        u = conv(D_FF + c0, D_FF + c1)
        act = (a * jax.nn.sigmoid(a) * u).astype(BF16)
        acc = acc + jnp.dot(act, wd_ref[c0:c1, :], preferred_element_type=F32)
    if final_norm:
        acc = _rmsnorm_rows(acc, gf_ref[...])
    y_ref[...] = acc


def _ffn(x2, g, wu, cw, cb, wd, gf, *, seq, final_norm):
    t = x2.shape[0]
    tm = ROW_TILE
    const = lambda i: (0, 0)
    halo_blocks = tm // FFN_HALO
    return pl.pallas_call(
        functools.partial(_ffn_kernel, tiles_per_seq=seq // tm, final_norm=final_norm),
        out_shape=jax.ShapeDtypeStruct((t, D_MODEL), F32),
        grid=(t // tm,),
        in_specs=[pl.BlockSpec((tm, D_MODEL), lambda i: (i, 0)),
                  pl.BlockSpec((FFN_HALO, D_MODEL), lambda i: (jnp.maximum(i * halo_blocks - 1, 0), 0)),
                  pl.BlockSpec((1, D_MODEL), const),
                  pl.BlockSpec(wu.shape, const), pl.BlockSpec(cw.shape, const),
                  pl.BlockSpec(cb.shape, const), pl.BlockSpec(wd.shape, const),
                  pl.BlockSpec((1, D_MODEL), const)],
        out_specs=pl.BlockSpec((tm, D_MODEL), lambda i: (i, 0)),
        scratch_shapes=[pltpu.VMEM((tm + FFN_HALO, D_MODEL), BF16)],
        compiler_params=_params(1),
        name="conv_ffn",
    )(x2, x2, g, wu, cw, cb, wd, gf)


def kernel(x, rel_bias, norm_mix, w_in, cmp_pe_k, cmp_w1_k, cmp_w2_k, cmp_pe_v, cmp_w1_v, cmp_w2_v,
           w_branch, w_out, norm_ffn, w_up, conv_w, conv_b, w_down, norm_final):
    b, s, d = x.shape
    depth = w_in.shape[0]
    t = b * s
    assert d == D_MODEL and s % T_TILE == 0 and s // NSA_SLC_BLOCK == _N_SLC and t % ROW_TILE == 0
    assert s % ROW_TILE == 0 and all(s % (dil * A_TILE) == 0 for _, dil in A_GROUPS)
    assert all(win // dil == A_TILE for win, dil in A_GROUPS) and A_GROUPS[0][1] == 1
    assert all(ROW_TILE % dil == 0 for _, dil in A_GROUPS)

    hg = A_HEADS_PER_GROUP
    bias_a = jnp.concatenate([
        _bias_tiles(rel_bias,
                    np.concatenate([_bucket_tile(A_TILE, 1, dil, win // dil),
                                    _bucket_tile(A_TILE, 0, dil, win // dil)], axis=1),
                    gi * hg, hg)
        for gi, (win, dil) in enumerate(A_GROUPS)], axis=0)
    assert (_bucket_tile(T_TILE, 2, 1, 3 * T_TILE) == REL_BUCKETS - 1).all()
    bias_b = _bias_tiles_t(rel_bias, [_bucket_tile(T_TILE, d, 1, NSA_WINDOW - 1) for d in (0, 1, 2)],
                           A_HEADS, B_HEADS)
    bias_b = bias_b.reshape(B_KV_HEADS, B_GROUP, 3, T_TILE, T_TILE).transpose(0, 2, 3, 1, 4).reshape(
        B_KV_HEADS, 3, T_TILE, _G_ROWS)
    bias_c = _bias_tiles_t(rel_bias, [_bucket_tile(T_TILE, d, 1, s) for d in (0, 1)],
                           A_HEADS + B_HEADS, C_HEADS)

    w_in_k = _prep_w_in(w_in)
    x2 = x.reshape(t, d)
    for i in range(depth):
        pa0, pa1, pa2, pb, pcmp, pc, mg, bg = _proj_in(x2, norm_mix[i][None, :], w_in_k, layer=i,
                                                       seq=s)

        oa = _mixer_a(pa0.reshape(b, s, _GRP_COLS), pa1, pa2, bias_a)

        cmp = _compress(pcmp.reshape(4, b, s // NSA_CMP_STRIDE, NSA_CMP_STRIDE * HEAD_DIM),
                        cmp_pe_k[i].reshape(1, -1), cmp_pe_v[i].reshape(1, -1),
                        cmp_w1_k[i].astype(BF16), cmp_w1_v[i].astype(BF16),
                        cmp_w2_k[i].astype(BF16), cmp_w2_v[i].astype(BF16))
        ob = _nsa(pb.reshape(b, s, _PB_COLS), cmp, bg.reshape(b, s, _BG_COLS), bias_b)

        oc = _moba(pc.reshape(b, s, _C_COLS), bias_c)

        x2 = _merge(oa.reshape(t, -1), ob.reshape(t, -1), oc.reshape(t, -1), mg, x2,
                    w_branch[i].astype(BF16), w_out[i].astype(BF16))
        x2 = _ffn(x2, norm_ffn[i][None, :], w_up[i].astype(BF16), conv_w[i], conv_b[i][None, :],
                  w_down[i].astype(BF16), norm_final[None, :], seq=s, final_norm=(i == depth - 1))
    return x2.reshape(b, s, d)
```

```python
import functools
import math

import jax
import jax.numpy as jnp
import numpy as np
from jax import lax
from jax.experimental import pallas as pl
from jax.experimental.pallas import tpu as pltpu

F32 = jnp.float32
BF16 = jnp.bfloat16

D_MODEL = 1024
HEAD_DIM = 64
A_GROUPS = ((128, 1), (512, 4), (2048, 16))
A_HEADS_PER_GROUP = 2
A_HEADS = 6
B_HEADS = 6
B_KV_HEADS = 2
B_GROUP = 3
C_HEADS = 4
NSA_CMP_BLOCK = 32
NSA_CMP_STRIDE = 16
NSA_SLC_BLOCK = 64
NSA_SLC_TOPK = 16
NSA_WINDOW = 512
MOBA_BLOCK = 256
MOBA_TOPK = 3
REL_BUCKETS = 32
REL_MAX_DIST = 128
D_FF = 2816
RMS_EPS = 1e-6
NEG_INF = -1e30
FORCE_BONUS = 1e4
ATTN_SCALE = HEAD_DIM ** -0.5

LANE = 128
A_TILE = 128
T_TILE = 256
ROW_TILE = 512
FFN_HALO = 8
VMEM_LIMIT = 56 * 1024 * 1024

_GRP_COLS = 3 * A_HEADS_PER_GROUP * HEAD_DIM
_A_COLS = len(A_GROUPS) * _GRP_COLS
_PB_COLS = B_HEADS * HEAD_DIM + 4 * B_KV_HEADS * HEAD_DIM
_CMP_COLS = 2 * B_KV_HEADS * HEAD_DIM
_C_COLS = 3 * C_HEADS * HEAD_DIM
_MG_COLS = 3 * D_MODEL
_BG_COLS = LANE
_OFF_A = 0
_OFF_PB = _OFF_A + _A_COLS
_OFF_CMP = _OFF_PB + _PB_COLS
_OFF_C = _OFF_CMP + _CMP_COLS
_OFF_MG = _OFF_C + _C_COLS
_OFF_BG = _OFF_MG + _MG_COLS
_W_COLS = _OFF_BG + _BG_COLS
_PROJ_CHUNK = 4 * LANE


def _params(n_grid):
    return pltpu.CompilerParams(dimension_semantics=("arbitrary",) * n_grid,
                                vmem_limit_bytes=VMEM_LIMIT)


def _rel_bucket_np(dist):
    n = np.maximum(dist, 0)
    exact = REL_BUCKETS // 2
    nf = np.maximum(n, 1).astype(np.float32)
    large = exact + (np.log(nf / np.float32(exact)) / np.float32(math.log(REL_MAX_DIST / exact))
                     * np.float32(REL_BUCKETS - exact)).astype(np.int32)
    return np.where(n < exact, n, np.minimum(large, REL_BUCKETS - 1)).astype(np.int32)


def _bucket_tile(tile, block_offset, dil, max_rel):
    rel = block_offset * tile + np.arange(tile)[:, None] - np.arange(tile)[None, :]
    ok = (rel >= 0) & (rel <= max_rel)
    return np.where(ok, _rel_bucket_np(rel * dil), -1).astype(np.int32)


def _bias_tiles_kernel(tbl_ref, idx_ref, o_ref, *, head0, relative):
    h = pl.program_id(0) + head0
    idx = idx_ref[...]
    acc = jnp.full(idx.shape, NEG_INF, F32)
    for b in range(REL_BUCKETS):
        acc = jnp.where(idx == b, tbl_ref[b, h], acc)
    if relative:
        acc = acc - tbl_ref[REL_BUCKETS - 1, h]
    o_ref[0] = acc


def _bias_tiles(rel_bias, idx, head0, n_heads, relative=False):
    r, c = idx.shape
    return pl.pallas_call(
        functools.partial(_bias_tiles_kernel, head0=head0, relative=relative),
        out_shape=jax.ShapeDtypeStruct((n_heads, r, c), F32),
        grid=(n_heads,),
        in_specs=[pl.BlockSpec(memory_space=pltpu.SMEM),
                  pl.BlockSpec((r, c), lambda h: (0, 0))],
        out_specs=pl.BlockSpec((1, r, c), lambda h: (h, 0, 0)),
        compiler_params=_params(1),
        name="bias_tiles",
    )(rel_bias, jnp.asarray(idx))


def _bias_tiles_t(rel_bias, idx_tiles, head0, n_heads):
    r, c = idx_tiles[0].shape
    flat = _bias_tiles(rel_bias, np.concatenate([t.T for t in idx_tiles], axis=1), head0, n_heads,
                       relative=True)
    return flat.reshape(n_heads, c, len(idx_tiles), r).transpose(0, 2, 1, 3)


def _rmsnorm_rows(x, g):
    return x * lax.rsqrt(jnp.mean(x * x, axis=-1, keepdims=True) + RMS_EPS) * g


def _proj_in_kernel(x_ref, g_ref, w_ref, pa0_ref, pa1_ref, pa2_ref, pb_ref, pcmp_ref, pc_ref,
                    mg_ref, bg_ref, regroup_ref):
    xn = _rmsnorm_rows(x_ref[...], g_ref[...]).astype(BF16)
    tm = xn.shape[0]

    def direct(out_ref):
        def sink(lo, hi, val):
            out_ref[:, lo:hi] = val.astype(out_ref.dtype)
        return sink

    def staged(slot0):
        def sink(lo, hi, val):
            for c in range(lo, hi, LANE):
                regroup_ref[slot0 + c // LANE] = val[:, c - lo:c - lo + LANE]
        return sink

    g1_slot, g2_slot, cmp_slot = 0, 3, 6
    sections = ((_OFF_A, _GRP_COLS, direct(pa0_ref)),
                (_OFF_A + _GRP_COLS, _GRP_COLS, staged(g1_slot)),
                (_OFF_A + 2 * _GRP_COLS, _GRP_COLS, staged(g2_slot)),
                (_OFF_PB, _PB_COLS, direct(pb_ref)), (_OFF_CMP, _CMP_COLS, staged(cmp_slot)),
                (_OFF_C, _C_COLS, direct(pc_ref)), (_OFF_MG, _MG_COLS, direct(mg_ref)),
                (_OFF_BG, _BG_COLS, direct(bg_ref)))
    for c0 in range(0, _W_COLS, _PROJ_CHUNK):
        c1 = min(c0 + _PROJ_CHUNK, _W_COLS)
        res = jnp.dot(xn, w_ref[:, c0:c1], preferred_element_type=F32)
        for off, width, sink in sections:
            lo, hi = max(c0, off), min(c1, off + width)
            if lo < hi:
                sink(lo - off, hi - off, res[:, lo - c0:hi - c0])

    for gi, slot0, out_ref in ((1, g1_slot, pa1_ref), (2, g2_slot, pa2_ref)):
        dil = A_GROUPS[gi][1]
        for r in range(dil):
            for j in range(3):
                out_ref[0, r, :, j * LANE:(j + 1) * LANE] = (
                    regroup_ref[slot0 + j, pl.ds(r, tm // dil, stride=dil), :].astype(BF16))
    for l in range(NSA_CMP_STRIDE):
        for kv in range(2):
            a = regroup_ref[cmp_slot + kv, pl.ds(l, tm // NSA_CMP_STRIDE, stride=NSA_CMP_STRIDE),
                            :].astype(BF16)
            for h in range(B_KV_HEADS):
                pcmp_ref[kv * B_KV_HEADS + h, :, l * HEAD_DIM:(l + 1) * HEAD_DIM] = (
                    a[:, h * HEAD_DIM:(h + 1) * HEAD_DIM])


def _proj_in(x2, g, w, *, layer, seq):
    t = x2.shape[0]
    tm = ROW_TILE
    b = t // seq
    tps = seq // tm
    row = lambda i: (i, 0)
    d1, d2 = A_GROUPS[1][1], A_GROUPS[2][1]
    sub = lambda i: (i // tps, 0, i % tps, 0)
    return pl.pallas_call(
        _proj_in_kernel,
        out_shape=(jax.ShapeDtypeStruct((t, _GRP_COLS), BF16),
                   jax.ShapeDtypeStruct((b, d1, seq // d1, _GRP_COLS), BF16),
                   jax.ShapeDtypeStruct((b, d2, seq // d2, _GRP_COLS), BF16),
                   jax.ShapeDtypeStruct((t, _PB_COLS), BF16),
                   jax.ShapeDtypeStruct((4, t // NSA_CMP_STRIDE, NSA_CMP_STRIDE * HEAD_DIM), BF16),
                   jax.ShapeDtypeStruct((t, _C_COLS), BF16),
                   jax.ShapeDtypeStruct((t, _MG_COLS), BF16),
                   jax.ShapeDtypeStruct((t, _BG_COLS), F32)),
        grid=(t // tm,),
        in_specs=[pl.BlockSpec((tm, D_MODEL), row),
                  pl.BlockSpec((1, D_MODEL), lambda i: (0, 0)),
                  pl.BlockSpec((None, D_MODEL, _W_COLS), lambda i: (layer, 0, 0))],
        out_specs=(pl.BlockSpec((tm, _GRP_COLS), row),
                   pl.BlockSpec((1, d1, tm // d1, _GRP_COLS), sub),
                   pl.BlockSpec((1, d2, tm // d2, _GRP_COLS), sub),
                   pl.BlockSpec((tm, _PB_COLS), row),
                   pl.BlockSpec((4, tm // NSA_CMP_STRIDE, NSA_CMP_STRIDE * HEAD_DIM),
                                lambda i: (0, i, 0)),
                   pl.BlockSpec((tm, _C_COLS), row),
                   pl.BlockSpec((tm, _MG_COLS), row),
                   pl.BlockSpec((tm, _BG_COLS), row)),
        scratch_shapes=[pltpu.VMEM((8, tm, LANE), F32)],
        compiler_params=_params(1),
        name="proj_in",
    )(x2, g, w)


def _w_in_segments():
    a = A_HEADS * HEAD_DIM
    bq0 = 3 * a
    bkc0 = bq0 + B_HEADS * HEAD_DIM
    bks0 = bkc0 + 2 * B_KV_HEADS * HEAD_DIM
    bg0 = bks0 + 4 * B_KV_HEADS * HEAD_DIM
    c0 = bg0 + B_HEADS * 3
    mg0 = c0 + _C_COLS
    gw = A_HEADS_PER_GROUP * HEAD_DIM
    segs = [(_OFF_A + (gi * 3 + j) * gw, j * a + gi * gw, gw)
            for gi in range(len(A_GROUPS)) for j in range(3)]
    segs += [(_OFF_PB, bq0, B_HEADS * HEAD_DIM),
             (_OFF_PB + B_HEADS * HEAD_DIM, bks0, bg0 - bks0),
             (_OFF_CMP, bkc0, _CMP_COLS), (_OFF_C, c0, _C_COLS), (_OFF_MG, mg0, _MG_COLS),
             (_OFF_BG, bg0, B_HEADS * 3)]
    return segs


def _prep_w_in_kernel(w_ref, o_ref):
    o_ref[0, :, _OFF_BG:] = jnp.zeros((o_ref.shape[1], _BG_COLS), BF16)
    for dst, src, width in _w_in_segments():
        for c in range(0, width, _PROJ_CHUNK):
            n = min(_PROJ_CHUNK, width - c)
            o_ref[0, :, dst + c:dst + c + n] = w_ref[0, :, src + c:src + c + n].astype(BF16)


def _prep_w_in(w):
    depth, rows, cols = w.shape
    rb = 128
    return pl.pallas_call(
        _prep_w_in_kernel,
        out_shape=jax.ShapeDtypeStruct((depth, rows, _W_COLS), BF16),
        grid=(depth, rows // rb),
        in_specs=[pl.BlockSpec((1, rb, cols), lambda l, r: (l, r, 0))],
        out_specs=pl.BlockSpec((1, rb, _W_COLS), lambda l, r: (l, r, 0)),
        compiler_params=_params(2),
        name="prep_w_in",
    )(w)


_A_UNROLL = 3


def _mixer_a_kernel(pa0_ref, pa1_ref, pa2_ref, bias_ref, o_ref, on_ref, ln_ref, *, seq):
    gw = A_HEADS_PER_GROUP * HEAD_DIM

    def band_block(src, l0, first, gi, rows):
        outs, lses = [], []
        for h in range(A_HEADS_PER_GROUP):
            qc, kc, vc = (slice(j * gw + h * HEAD_DIM, j * gw + (h + 1) * HEAD_DIM) for j in range(3))
            q = src[pl.ds(l0, A_TILE), qc] * ATTN_SCALE
            if first:
                kr = pl.ds(l0, A_TILE)
                bias = bias_ref[A_HEADS_PER_GROUP * gi + h, :, A_TILE:]
            else:
                kr = pl.ds(l0 - A_TILE, 2 * A_TILE)
                bias = bias_ref[A_HEADS_PER_GROUP * gi + h]
            s = _nt(q, src[kr, kc]) + bias
            m = jnp.max(s, axis=-1, keepdims=True)
            e = jnp.exp(s - m)
            l = jnp.sum(e, axis=-1, keepdims=True)
            outs.append(jnp.dot(e.astype(BF16), src[kr, vc], preferred_element_type=F32) / l)
            lses.append(jnp.broadcast_to(m + jnp.log(l), (A_TILE, HEAD_DIM)))
        on_ref[gi, rows, :] = jnp.concatenate(outs, axis=-1)
        ln_ref[gi, rows, :] = jnp.concatenate(lses, axis=-1)

    for gi, ((_, dil), pa_ref) in enumerate(zip(A_GROUPS, (pa0_ref, pa1_ref, pa2_ref))):
        n_blocks = seq // dil // A_TILE
        if dil == 1:
            band_block(pa_ref.at[0], 0, True, gi, pl.ds(0, A_TILE))
            assert (n_blocks - 1) % _A_UNROLL == 0

            def body0(it, c, pa_ref=pa_ref, gi=gi):
                for u in range(_A_UNROLL):
                    l0 = pl.multiple_of((1 + it * _A_UNROLL + u) * A_TILE, A_TILE)
                    band_block(pa_ref.at[0], l0, False, gi, pl.ds(l0, A_TILE))
                return c
            lax.fori_loop(0, (n_blocks - 1) // _A_UNROLL, body0, 0)
        elif n_blocks > 1:
            for r in range(dil):
                band_block(pa_ref.at[0, r], 0, True, gi, pl.ds(r, A_TILE, stride=dil))

            def body1(i, c, pa_ref=pa_ref, gi=gi, dil=dil):
                l0 = pl.multiple_of(i * A_TILE, A_TILE)
                for r in range(dil):
                    band_block(pa_ref.at[0, r], l0, False, gi, pl.ds(l0 * dil + r, A_TILE, stride=dil))
                return c
            lax.fori_loop(1, n_blocks, body1, 0)
        else:
            per_iter = 4

            def body2(it, c, pa_ref=pa_ref, gi=gi, dil=dil):
                for u in range(per_iter):
                    r = it * per_iter + u
                    band_block(pa_ref.at[0, r], 0, True, gi, pl.ds(r, A_TILE, stride=dil))
                return c
            lax.fori_loop(0, dil // per_iter, body2, 0)

    chunk = T_TILE
    for c0 in range(0, seq, chunk):
        rows = slice(c0, c0 + chunk)
        lse = [ln_ref[g, rows, :] for g in range(len(A_GROUPS))]
        top = functools.reduce(jnp.maximum, lse)
        ex = [jnp.exp(v - top) for v in lse]
        den = functools.reduce(lambda a, b: a + b, ex)
        for g in range(len(A_GROUPS)):
            o_ref[0, rows, g * gw:(g + 1) * gw] = (ex[g] / den * on_ref[g, rows, :]).astype(BF16)


def _mixer_a(pa0, pa1, pa2, bias):
    b, seq, _ = pa0.shape
    return pl.pallas_call(
        functools.partial(_mixer_a_kernel, seq=seq),
        out_shape=jax.ShapeDtypeStruct((b, seq, _GRP_COLS), BF16),
        grid=(b,),
        in_specs=[pl.BlockSpec((1, seq, _GRP_COLS), lambda i: (i, 0, 0)),
                  pl.BlockSpec((1,) + pa1.shape[1:], lambda i: (i, 0, 0, 0)),
                  pl.BlockSpec((1,) + pa2.shape[1:], lambda i: (i, 0, 0, 0)),
                  pl.BlockSpec(bias.shape, lambda i: (0, 0, 0))],
        out_specs=pl.BlockSpec((1, seq, _GRP_COLS), lambda i: (i, 0, 0)),
        scratch_shapes=[pltpu.VMEM((len(A_GROUPS), seq, LANE), F32),
                        pltpu.VMEM((len(A_GROUPS), seq, LANE), F32)],
        compiler_params=_params(1),
        name="mixer_a",
    )(pa0, pa1, pa2, bias)


def _compress_kernel(r_ref, pek_ref, pev_ref, w1k_ref, w1v_ref, w2k_ref, w2v_ref, o_ref):
    half = NSA_CMP_STRIDE * HEAD_DIM
    for kv, (pe_ref, w1_ref, w2_ref) in enumerate(((pek_ref, w1k_ref, w2k_ref),
                                                   (pev_ref, w1v_ref, w2v_ref))):
        for h in range(B_KV_HEADS):
            r = r_ref[kv * B_KV_HEADS + h, 0].astype(F32)
            lo = jnp.dot((r + pe_ref[:, :half]).astype(BF16), w1_ref[:half, :],
                         preferred_element_type=F32)
            hi = jnp.dot((r + pe_ref[:, half:]).astype(BF16), w1_ref[half:, :],
                         preferred_element_type=F32)
            hid = lo + pltpu.roll(hi, hi.shape[0] - 1, axis=0)
            act = jax.nn.gelu(hid).astype(BF16)
            o_ref[0, kv * B_KV_HEADS + h] = jnp.dot(act, w2_ref[...],
                                                    preferred_element_type=F32).astype(BF16)


def _compress(r, pe_k, pe_v, w1_k, w1_v, w2_k, w2_v):
    _, b, m, c = r.shape
    const2 = lambda i: (0, 0)
    return pl.pallas_call(
        _compress_kernel,
        out_shape=jax.ShapeDtypeStruct((b, 4, m, HEAD_DIM), BF16),
        grid=(b,),
        in_specs=[pl.BlockSpec((4, 1, m, c), lambda i: (0, i, 0, 0)),
                  pl.BlockSpec(pe_k.shape, const2), pl.BlockSpec(pe_v.shape, const2),
                  pl.BlockSpec(w1_k.shape, const2), pl.BlockSpec(w1_v.shape, const2),
                  pl.BlockSpec(w2_k.shape, const2), pl.BlockSpec(w2_v.shape, const2)],
        out_specs=pl.BlockSpec((1, 4, m, HEAD_DIM), lambda i: (i, 0, 0, 0)),
        compiler_params=_params(1),
        name="nsa_compress",
    )(r, pe_k, pe_v, w1_k, w1_v, w2_k, w2_v)


MASK_BIG = 2.0 ** 100
SEL_PAD = 32


def _nt(a, b, **kw):
    return lax.dot_general(a, b, (((1,), (1,)), ((), ())), preferred_element_type=F32, **kw)


def _top_mask_t(score, n_rows, n_top):
    sub = lax.broadcasted_iota(jnp.int32, score.shape, 0)
    rank = jnp.zeros(score.shape, F32)
    for j in range(n_rows):
        row = score[j:j + 1, :]
        ge = jnp.where(row >= score, 1.0, 0.0)
        gt = jnp.where(row > score, 1.0, 0.0)
        rank = rank + jnp.where(sub > j, ge, gt)
    return rank < n_top


def _top_keep(score, n_rows, n_top, block):
    free = n_top * block
    ranked = jnp.where(_top_mask_t(score[:, free:], n_rows, n_top), 1.0, 0.0)
    return jnp.concatenate([jnp.ones((score.shape[0], free), F32), ranked], axis=1)


BAND_SLOTS = 4


def _loop_pairs(lo, hi, body):
    n = jnp.maximum(hi - lo, 0)

    def pair(p, c):
        body(lo + 2 * p)
        body(lo + 2 * p + 1)
        return c
    lax.fori_loop(0, n // 2, pair, 0)
    pl.when(n % 2 == 1)(lambda: body(hi - 1))


def _two_pass_attention(i, streams, band_streams=()):
    groups = ((streams, lambda j: j), (band_streams, lambda j: j % BAND_SLOTS))

    def key_rows(j):
        return pl.ds(pl.multiple_of(j * T_TILE, T_TILE), T_TILE)

    def logits(group, j, d):
        some, slot = group
        for k_tile, q_t, _, bias_tile, s_ref, rmp_ref, _ in some:
            s = jnp.dot(k_tile(key_rows(j)), q_t(), preferred_element_type=F32)
            if d is not None:
                s = s + bias_tile(d)
            s_ref[slot(j)] = s
            fold = jnp.max(s.reshape(s.shape[0] // 8, 8, s.shape[1]), axis=0)
            rmp_ref[...] = fold if d == 0 else jnp.maximum(rmp_ref[...], fold)

    def near(n_full, n_band):
        for d in range(n_full):
            logits(groups[0], i - d, d)
        for d in range(n_band):
            logits(groups[1], i - d, d)

    if band_streams:
        pl.when(i == 0)(lambda: near(1, 1))
        pl.when(i == 1)(lambda: near(2, 2))
        pl.when(i >= 2)(lambda: near(2, 3))
    else:
        logits(groups[0], i, 0)
        pl.when(i >= 1)(lambda: logits(groups[0], i - 1, 1))

    _loop_pairs(0, i - 1, lambda j: logits(groups[0], j, None))

    tops = {}
    for some, _ in groups:
        for stream in some:
            rmp_ref, acc_ref = stream[-2:]
            tops[id(rmp_ref)] = jnp.max(rmp_ref[...], axis=0, keepdims=True)
            acc_ref[...] = jnp.zeros(acc_ref.shape, F32)

    def pv(active):
        def body(j):
            for some, slot in active:
                for _, _, vt_ref, _, s_ref, rmp_ref, acc_ref in some:
                    e = jnp.exp(s_ref[slot(j)] - tops[id(rmp_ref)])
                    acc_ref[...] += jnp.dot(vt_ref[j], e.astype(BF16), preferred_element_type=F32)
        return body

    if band_streams:
        band_lo = jnp.maximum(i - 2, 0)
        _loop_pairs(0, band_lo, pv(groups[:1]))
        pv_all = pv(groups)

        def last_tiles(n):
            for d in reversed(range(n)):
                pv_all(i - d)

        pl.when(i == 0)(lambda: last_tiles(1))
        pl.when(i == 1)(lambda: last_tiles(2))
        pl.when(i >= 2)(lambda: last_tiles(3))
    else:
        _loop_pairs(0, i + 1, pv(groups[:1]))

    outs = []
    for some, _ in groups:
        for *_, acc_ref in some:
            acc = acc_ref[...]
            outs.append(acc[:HEAD_DIM, :] / acc[HEAD_DIM:HEAD_DIM + 1, :])
    return outs


def _eye(n, dtype=BF16):
    return (lax.broadcasted_iota(jnp.int32, (n, n), 0)
            == lax.broadcasted_iota(jnp.int32, (n, n), 1)).astype(dtype)


def _transposed_values(v_ref_rows, n_tiles):
    rows = lax.broadcasted_iota(jnp.int32, (LANE, HEAD_DIM), 0)
    cols = lax.broadcasted_iota(jnp.int32, (LANE, HEAD_DIM), 1)
    place = (rows == cols).astype(BF16)
    ones_row = (lax.broadcasted_iota(jnp.int32, (LANE, T_TILE), 0) == HEAD_DIM).astype(F32)
    for j in range(n_tiles):
        v = v_ref_rows(slice(j * T_TILE, (j + 1) * T_TILE))
        yield j, (_nt(place, v) + ones_row).astype(BF16)


def _with_block_onehot(k, block):
    n = k.shape[0]
    rows = lax.broadcasted_iota(jnp.int32, (HEAD_DIM, LANE), 0)
    cols = lax.broadcasted_iota(jnp.int32, (HEAD_DIM, LANE), 1)
    place = (rows == cols).astype(BF16)
    lane = lax.broadcasted_iota(jnp.int32, (n, LANE), 1)
    hot = (lane - HEAD_DIM == lax.broadcasted_iota(jnp.int32, (n, LANE), 0) // block).astype(F32)
    return (jnp.dot(k, place, preferred_element_type=F32) + hot).astype(BF16)


def _augment_q_t(q_t, allowed_t):
    pen = (allowed_t - 1.0) * MASK_BIG
    return jnp.concatenate([q_t, pen, jnp.zeros(pen.shape, F32)], axis=0).astype(BF16)


_N_SLC = 32
_G_ROWS = B_GROUP * T_TILE


def _nsa_kernel(pb_ref, cmp_ref, bg_ref, bias_ref, o_ref,
                kaug_ref, vst_ref, vwt_ref, eye_ref, sig_ref, q3t_ref, qaugt_ref, s_ref, sw_ref,
                rmp_ref, acc_ref, ocmp_ref, *, seq):
    n_tiles = seq // T_TILE
    n_cmp_rows = seq // NSA_CMP_STRIDE
    q_cols = B_HEADS * HEAD_DIM
    ks0, vs0, kw0, vw0 = (q_cols + i * B_KV_HEADS * HEAD_DIM for i in range(4))

    def head_cols(c0, h):
        return slice(c0 + h * HEAD_DIM, c0 + (h + 1) * HEAD_DIM)

    j_id = lax.broadcasted_iota(jnp.int32, (_N_SLC, n_cmp_rows), 0)
    c_id = lax.broadcasted_iota(jnp.int32, (_N_SLC, n_cmp_rows), 1)
    overlap_t = ((c_id * NSA_CMP_STRIDE < (j_id + 1) * NSA_SLC_BLOCK)
                 & (c_id * NSA_CMP_STRIDE + NSA_CMP_BLOCK > j_id * NSA_SLC_BLOCK)).astype(F32)
    eye_ref[...] = _eye(T_TILE)
    for h in range(B_KV_HEADS):
        kaug_ref[h] = _with_block_onehot(pb_ref[0, :, head_cols(ks0, h)], NSA_SLC_BLOCK)
        for j, vt in _transposed_values(lambda r, h=h: pb_ref[0, r, head_cols(vs0, h)], n_tiles):
            vst_ref[h, j] = vt
        for j, vt in _transposed_values(lambda r, h=h: pb_ref[0, r, head_cols(vw0, h)], n_tiles):
            vwt_ref[h, j] = vt

    def tiles(a):
        return [a[:, j * T_TILE:(j + 1) * T_TILE] for j in range(n_tiles)]

    pick = (lax.broadcasted_iota(jnp.int32, (SEL_PAD, _BG_COLS), 0)
            == lax.broadcasted_iota(jnp.int32, (SEL_PAD, _BG_COLS), 1)).astype(F32)
    sig_all = jax.nn.sigmoid(_nt(pick, bg_ref[0], precision=lax.Precision.HIGHEST))
    for j, t in enumerate(tiles(sig_all)):
        sig_ref[j] = t

    eye_h = _eye(HEAD_DIM)
    t_lane = lax.broadcasted_iota(jnp.int32, (n_cmp_rows, seq), 1)
    c_end = (lax.broadcasted_iota(jnp.int32, (n_cmp_rows, seq), 0) * NSA_CMP_STRIDE
             + NSA_CMP_BLOCK - 1)
    ok = (c_end <= t_lane) & (c_end < seq)
    ok_f = ok.astype(F32)
    j_sub = lax.broadcasted_iota(jnp.int32, (_N_SLC, seq), 0)
    jt = lax.broadcasted_iota(jnp.int32, (_N_SLC, seq), 1) // NSA_SLC_BLOCK
    forced = ((j_sub == 0) | (j_sub == jt) | (j_sub == jt - 1)).astype(F32)
    valid = j_sub <= jt

    for h in range(B_KV_HEADS):
        kc = cmp_ref[0, h]
        vct = _nt(eye_h, cmp_ref[0, B_KV_HEADS + h]).astype(BF16)
        q_ts, p_sum = [], None
        for g in range(B_GROUP):
            q_t = _nt(eye_h, pb_ref[0, :, head_cols(0, h * B_GROUP + g)]) * ATTN_SCALE
            q_ts.append(q_t)
            lc = jnp.where(ok, jnp.dot(kc, q_t.astype(BF16), preferred_element_type=F32), NEG_INF)
            ec = jnp.exp(lc - jnp.max(lc, axis=0, keepdims=True))
            pc = ec / jnp.sum(ec, axis=0, keepdims=True) * ok_f
            o_cmp = jnp.dot(vct, pc.astype(BF16), preferred_element_type=F32)
            for j, t in enumerate(tiles(o_cmp)):
                ocmp_ref[h, j, :, g * T_TILE:(g + 1) * T_TILE] = t
            p_sum = pc if p_sum is None else p_sum + pc
        imp_t = jnp.dot(overlap_t, p_sum, precision=lax.Precision.HIGHEST,
                        preferred_element_type=F32)
        score = jnp.where(valid, imp_t + FORCE_BONUS * forced, NEG_INF)
        sel_t = jnp.where(valid, _top_keep(score, _N_SLC, NSA_SLC_TOPK, NSA_SLC_BLOCK), 0.0)
        pen_tiles = tiles((sel_t - 1.0) * MASK_BIG)
        q_tiles = [tiles(q_t) for q_t in q_ts]
        for j in range(n_tiles):
            q3t = jnp.concatenate([q_tiles[g][j] for g in range(B_GROUP)], axis=1)
            pen = jnp.concatenate([pen_tiles[j]] * B_GROUP, axis=1)
            q3t_ref[h, j] = q3t.astype(BF16)
            qaugt_ref[h, j] = jnp.concatenate([q3t, pen, jnp.zeros(pen.shape, F32)],
                                              axis=0).astype(BF16)

    def tile_body(i, carry):
        rows = pl.ds(pl.multiple_of(i * T_TILE, T_TILE), T_TILE)

        outs = _two_pass_attention(
            i,
            [(lambda kr, h=h: kaug_ref[h, kr, :], lambda h=h: qaugt_ref[h, i], vst_ref.at[h],
              lambda d, h=h: bias_ref[h, d], s_ref.at[h], rmp_ref.at[0, h], acc_ref.at[0, h])
             for h in range(B_KV_HEADS)],
            [(lambda kr, h=h: pb_ref[0, kr, head_cols(kw0, h)], lambda h=h: q3t_ref[h, i],
              vwt_ref.at[h], lambda d, h=h: bias_ref[h, d], sw_ref.at[h], rmp_ref.at[1, h],
              acc_ref.at[1, h]) for h in range(B_KV_HEADS)])
        oslc, owin = outs[:B_KV_HEADS], outs[B_KV_HEADS:]

        sig_t = sig_ref[i]
        heads = []
        for h in range(B_KV_HEADS):
            for g in range(B_GROUP):
                head = h * B_GROUP + g
                gc = slice(g * T_TILE, (g + 1) * T_TILE)
                heads.append(sig_t[3 * head:3 * head + 1, :] * ocmp_ref[h, i, :, gc]
                             + sig_t[3 * head + 1:3 * head + 2, :] * oslc[h][:, gc]
                             + sig_t[3 * head + 2:3 * head + 3, :] * owin[h][:, gc])
        o_t = jnp.concatenate(heads, axis=0).astype(BF16)
        o_ref[0, rows, :] = _nt(eye_ref[...], o_t).astype(o_ref.dtype)
        return carry

    lax.fori_loop(0, n_tiles, tile_body, 0)


def _nsa(pb, cmp, bg, bias):
    b, seq, _ = pb.shape
    n_tiles = seq // T_TILE
    return pl.pallas_call(
        functools.partial(_nsa_kernel, seq=seq),
        out_shape=jax.ShapeDtypeStruct((b, seq, B_HEADS * HEAD_DIM), BF16),
        grid=(b,),
        in_specs=[pl.BlockSpec((1, seq, _PB_COLS), lambda i: (i, 0, 0)),
                  pl.BlockSpec((1,) + cmp.shape[1:], lambda i: (i, 0, 0, 0)),
                  pl.BlockSpec((1, seq, _BG_COLS), lambda i: (i, 0, 0)),
                  pl.BlockSpec(bias.shape, lambda i: (0, 0, 0, 0))],
        out_specs=pl.BlockSpec((1, seq, B_HEADS * HEAD_DIM), lambda i: (i, 0, 0)),
        scratch_shapes=[pltpu.VMEM((B_KV_HEADS, seq, LANE), BF16),
                        pltpu.VMEM((B_KV_HEADS, n_tiles, LANE, T_TILE), BF16),
                        pltpu.VMEM((B_KV_HEADS, n_tiles, LANE, T_TILE), BF16),
                        pltpu.VMEM((T_TILE, T_TILE), BF16),
                        pltpu.VMEM((n_tiles, SEL_PAD, T_TILE), F32),
                        pltpu.VMEM((B_KV_HEADS, n_tiles, HEAD_DIM, _G_ROWS), BF16),
                        pltpu.VMEM((B_KV_HEADS, n_tiles, LANE, _G_ROWS), BF16),
                        pltpu.VMEM((B_KV_HEADS, n_tiles, T_TILE, _G_ROWS), F32),
                        pltpu.VMEM((B_KV_HEADS, BAND_SLOTS, T_TILE, _G_ROWS), F32),
                        pltpu.VMEM((2, B_KV_HEADS, 8, _G_ROWS), F32),
                        pltpu.VMEM((2, B_KV_HEADS, LANE, _G_ROWS), F32),
                        pltpu.VMEM((B_KV_HEADS, n_tiles, HEAD_DIM, _G_ROWS), F32)],
        compiler_params=_params(1),
        name="nsa",
    )(pb, cmp, bg, bias)


def _moba_kernel(pc_ref, bias_ref, o_ref, kaug_ref, vt_ref, eye_ref, qaugt_ref, s_ref,
                 rmp_ref, acc_ref, *, seq):
    n_blk = seq // MOBA_BLOCK
    width = C_HEADS * HEAD_DIM
    eye_ref[...] = _eye(T_TILE)

    def cols(j, h):
        return slice(j * width + h * HEAD_DIM, j * width + (h + 1) * HEAD_DIM)

    n_sub = lax.broadcasted_iota(jnp.int32, (SEL_PAD, seq), 0)
    own = lax.broadcasted_iota(jnp.int32, (SEL_PAD, seq), 1) // MOBA_BLOCK
    past = n_sub < own
    eye_h = _eye(HEAD_DIM)
    for h in range(C_HEADS):
        kaug_ref[h] = _with_block_onehot(pc_ref[0, :, cols(1, h)], MOBA_BLOCK)
        for j, vt in _transposed_values(lambda r, h=h: pc_ref[0, r, cols(2, h)], n_blk):
            vt_ref[h, j] = vt
        kmean = jnp.concatenate(
            [jnp.mean(pc_ref[0, n * MOBA_BLOCK:(n + 1) * MOBA_BLOCK, cols(1, h)].astype(F32), axis=0,
                      keepdims=True) for n in range(n_blk)]
            + [jnp.zeros((SEL_PAD - n_blk, HEAD_DIM), F32)], axis=0)
        q_t = _nt(eye_h, pc_ref[0, :, cols(0, h)])
        gate_t = jnp.dot(kmean, q_t, precision=lax.Precision.HIGHEST,
                         preferred_element_type=F32)
        score = jnp.where(past, gate_t, NEG_INF)
        picked = past & (_top_keep(score, n_blk, MOBA_TOPK, MOBA_BLOCK) > 0.5)
        allowed_t = jnp.where(picked | (n_sub == own), 1.0, 0.0)
        qaug_t = _augment_q_t(q_t * ATTN_SCALE, allowed_t)
        for j in range(n_blk):
            qaugt_ref[h, j] = qaug_t[:, j * T_TILE:(j + 1) * T_TILE]

    def tile_body(i, carry):
        rows = pl.ds(pl.multiple_of(i * MOBA_BLOCK, MOBA_BLOCK), MOBA_BLOCK)
        outs = _two_pass_attention(
            i, [(lambda kr, h=h: kaug_ref[h, kr, :], lambda h=h: qaugt_ref[h, i], vt_ref.at[h],
                 lambda d, h=h: bias_ref[h, d], s_ref.at[h], rmp_ref.at[h], acc_ref.at[h])
                for h in range(C_HEADS)])
        o_t = jnp.concatenate(outs, axis=0).astype(BF16)
        o_ref[0, rows, :] = _nt(eye_ref[...], o_t).astype(o_ref.dtype)
        return carry

    lax.fori_loop(0, n_blk, tile_body, 0)


def _moba(pc, bias):
    b, seq, _ = pc.shape
    width = C_HEADS * HEAD_DIM
    n_tiles = seq // T_TILE
    return pl.pallas_call(
        functools.partial(_moba_kernel, seq=seq),
        out_shape=jax.ShapeDtypeStruct((b, seq, width), BF16),
        grid=(b,),
        in_specs=[pl.BlockSpec((1, seq, _C_COLS), lambda i: (i, 0, 0)),
                  pl.BlockSpec(bias.shape, lambda i: (0, 0, 0, 0))],
        out_specs=pl.BlockSpec((1, seq, width), lambda i: (i, 0, 0)),
        scratch_shapes=[pltpu.VMEM((C_HEADS, seq, LANE), BF16),
                        pltpu.VMEM((C_HEADS, n_tiles, LANE, T_TILE), BF16),
                        pltpu.VMEM((T_TILE, T_TILE), BF16),
                        pltpu.VMEM((C_HEADS, n_tiles, LANE, T_TILE), BF16),
                        pltpu.VMEM((C_HEADS, n_tiles, T_TILE, T_TILE), F32),
                        pltpu.VMEM((C_HEADS, 8, T_TILE), F32),
                        pltpu.VMEM((C_HEADS, LANE, T_TILE), F32)],
        compiler_params=_params(1),
        name="moba",
    )(pc, bias)


def _merge_kernel(oa_ref, ob_ref, oc_ref, mg_ref, x_ref, wb_ref, wo_ref, y_ref):
    r0 = A_HEADS * HEAD_DIM
    r1 = r0 + B_HEADS * HEAD_DIM
    ya = jnp.dot(oa_ref[...], wb_ref[0:r0, :], preferred_element_type=F32)
    yb = jnp.dot(ob_ref[...], wb_ref[r0:r1, :], preferred_element_type=F32)
    yc = jnp.dot(oc_ref[...], wb_ref[r1:, :], preferred_element_type=F32)
    merged = (jax.nn.sigmoid(mg_ref[:, 0:D_MODEL].astype(F32)) * ya
              + jax.nn.sigmoid(mg_ref[:, D_MODEL:2 * D_MODEL].astype(F32)) * yb
              + jax.nn.sigmoid(mg_ref[:, 2 * D_MODEL:].astype(F32)) * yc)
    y_ref[...] = x_ref[...] + jnp.dot(merged.astype(BF16), wo_ref[...], preferred_element_type=F32)


def _merge(oa, ob, oc, mg, x2, wb, wo):
    t = x2.shape[0]
    tm = ROW_TILE
    row = lambda i: (i, 0)
    return pl.pallas_call(
        _merge_kernel,
        out_shape=jax.ShapeDtypeStruct((t, D_MODEL), F32),
        grid=(t // tm,),
        in_specs=[pl.BlockSpec((tm, oa.shape[1]), row),
                  pl.BlockSpec((tm, ob.shape[1]), row), pl.BlockSpec((tm, oc.shape[1]), row),
                  pl.BlockSpec((tm, _MG_COLS), row), pl.BlockSpec((tm, D_MODEL), row),
                  pl.BlockSpec(wb.shape, lambda i: (0, 0)), pl.BlockSpec(wo.shape, lambda i: (0, 0))],
        out_specs=pl.BlockSpec((tm, D_MODEL), row),
        compiler_params=_params(1),
        name="merge",
    )(oa, ob, oc, mg, x2, wb, wo)


MXU_WIDTH = 256
_FF_CHUNK_TILES = 6
_FF_BOUNDS = tuple(range(0, D_FF, _FF_CHUNK_TILES * MXU_WIDTH)) + (D_FF,)
_FF_CHUNKS = tuple(zip(_FF_BOUNDS[:-1], _FF_BOUNDS[1:]))


def _ffn_kernel(x_ref, halo_ref, g_ref, wu_ref, cw_ref, cb_ref, wd_ref, gf_ref, y_ref, xn_ref,
                *, tiles_per_seq, final_norm):
    i = pl.program_id(0)
    x = x_ref[...]
    g = g_ref[...]
    keep = (i % tiles_per_seq != 0).astype(F32)
    xn_ref[0:FFN_HALO, :] = (_rmsnorm_rows(halo_ref[...], g) * keep).astype(BF16)
    xn_ref[FFN_HALO:, :] = _rmsnorm_rows(x, g).astype(BF16)
    xn = xn_ref[...]
    rows = xn.shape[0]

    def conv(c0, c1):
        hcol = jnp.dot(xn, wu_ref[:, c0:c1], preferred_element_type=F32)
        out = (cw_ref[2:3, c0:c1] * hcol
               + cw_ref[1:2, c0:c1] * pltpu.roll(hcol, 1, axis=0)
               + cw_ref[0:1, c0:c1] * pltpu.roll(hcol, 2, axis=0)
               + cb_ref[:, c0:c1])
        return out[FFN_HALO:rows]

    acc = x
    for c0, c1 in _FF_CHUNKS:
        a = conv(c0, c1)
        u = conv(D_FF + c0, D_FF + c1)
        act = (a * jax.nn.sigmoid(a) * u).astype(BF16)
        acc = acc + jnp.dot(act, wd_ref[c0:c1, :], preferred_element_type=F32)
    if final_norm:
        acc = _rmsnorm_rows(acc, gf_ref[...])
    y_ref[...] = acc


def _ffn(x2, g, wu, cw, cb, wd, gf, *, seq, final_norm):
    t = x2.shape[0]
    tm = ROW_TILE
    const = lambda i: (0, 0)
    halo_blocks = tm // FFN_HALO
    return pl.pallas_call(
        functools.partial(_ffn_kernel, tiles_per_seq=seq // tm, final_norm=final_norm),
        out_shape=jax.ShapeDtypeStruct((t, D_MODEL), F32),
        grid=(t // tm,),
        in_specs=[pl.BlockSpec((tm, D_MODEL), lambda i: (i, 0)),
                  pl.BlockSpec((FFN_HALO, D_MODEL), lambda i: (jnp.maximum(i * halo_blocks - 1, 0), 0)),
                  pl.BlockSpec((1, D_MODEL), const),
                  pl.BlockSpec(wu.shape, const), pl.BlockSpec(cw.shape, const),
                  pl.BlockSpec(cb.shape, const), pl.BlockSpec(wd.shape, const),
                  pl.BlockSpec((1, D_MODEL), const)],
        out_specs=pl.BlockSpec((tm, D_MODEL), lambda i: (i, 0)),
        scratch_shapes=[pltpu.VMEM((tm + FFN_HALO, D_MODEL), BF16)],
        compiler_params=_params(1),
        name="conv_ffn",
    )(x2, x2, g, wu, cw, cb, wd, gf)


def kernel(x, rel_bias, norm_mix, w_in, cmp_pe_k, cmp_w1_k, cmp_w2_k, cmp_pe_v, cmp_w1_v, cmp_w2_v,
           w_branch, w_out, norm_ffn, w_up, conv_w, conv_b, w_down, norm_final):
    b, s, d = x.shape
    depth = w_in.shape[0]
    t = b * s
    assert d == D_MODEL and s % T_TILE == 0 and s // NSA_SLC_BLOCK == _N_SLC and t % ROW_TILE == 0
    assert s % ROW_TILE == 0 and all(s % (dil * A_TILE) == 0 for _, dil in A_GROUPS)
    assert all(win // dil == A_TILE for win, dil in A_GROUPS) and A_GROUPS[0][1] == 1
    assert all(ROW_TILE % dil == 0 for _, dil in A_GROUPS)

    hg = A_HEADS_PER_GROUP
    bias_a = jnp.concatenate([
        _bias_tiles(rel_bias,
                    np.concatenate([_bucket_tile(A_TILE, 1, dil, win // dil),
                                    _bucket_tile(A_TILE, 0, dil, win // dil)], axis=1),
                    gi * hg, hg)
        for gi, (win, dil) in enumerate(A_GROUPS)], axis=0)
    assert (_bucket_tile(T_TILE, 2, 1, 3 * T_TILE) == REL_BUCKETS - 1).all()
    bias_b = _bias_tiles_t(rel_bias, [_bucket_tile(T_TILE, d, 1, NSA_WINDOW - 1) for d in (0, 1, 2)],
                           A_HEADS, B_HEADS)
    bias_b = bias_b.reshape(B_KV_HEADS, B_GROUP, 3, T_TILE, T_TILE).transpose(0, 2, 3, 1, 4).reshape(
        B_KV_HEADS, 3, T_TILE, _G_ROWS)
    bias_c = _bias_tiles_t(rel_bias, [_bucket_tile(T_TILE, d, 1, s) for d in (0, 1)],
                           A_HEADS + B_HEADS, C_HEADS)

    w_in_k = _prep_w_in(w_in)
    x2 = x.reshape(t, d)
    for i in range(depth):
        pa0, pa1, pa2, pb, pcmp, pc, mg, bg = _proj_in(x2, norm_mix[i][None, :], w_in_k, layer=i,
                                                       seq=s)

        oa = _mixer_a(pa0.reshape(b, s, _GRP_COLS), pa1, pa2, bias_a)

        cmp = _compress(pcmp.reshape(4, b, s // NSA_CMP_STRIDE, NSA_CMP_STRIDE * HEAD_DIM),
                        cmp_pe_k[i].reshape(1, -1), cmp_pe_v[i].reshape(1, -1),
                        cmp_w1_k[i].astype(BF16), cmp_w1_v[i].astype(BF16),
                        cmp_w2_k[i].astype(BF16), cmp_w2_v[i].astype(BF16))
        ob = _nsa(pb.reshape(b, s, _PB_COLS), cmp, bg.reshape(b, s, _BG_COLS), bias_b)

        oc = _moba(pc.reshape(b, s, _C_COLS), bias_c)

        x2 = _merge(oa.reshape(t, -1), ob.reshape(t, -1), oc.reshape(t, -1), mg, x2,
                    w_branch[i].astype(BF16), w_out[i].astype(BF16))
        x2 = _ffn(x2, norm_ffn[i][None, :], w_up[i].astype(BF16), conv_w[i], conv_b[i][None, :],
                  w_down[i].astype(BF16), norm_final[None, :], seq=s, final_norm=(i == depth - 1))
    return x2.reshape(b, s, d)
```

```python
import functools
import math

import jax
import jax.numpy as jnp
import numpy as np
from jax import lax
from jax.experimental import pallas as pl
from jax.experimental.pallas import tpu as pltpu

F32 = jnp.float32
BF16 = jnp.bfloat16

D_MODEL = 1024
HEAD_DIM = 64
A_GROUPS = ((128, 1), (512, 4), (2048, 16))
A_HEADS_PER_GROUP = 2
A_HEADS = 6
B_HEADS = 6
B_KV_HEADS = 2
B_GROUP = 3
C_HEADS = 4
NSA_CMP_BLOCK = 32
NSA_CMP_STRIDE = 16
NSA_SLC_BLOCK = 64
NSA_SLC_TOPK = 16
NSA_WINDOW = 512
MOBA_BLOCK = 256
MOBA_TOPK = 3
REL_BUCKETS = 32
REL_MAX_DIST = 128
D_FF = 2816
RMS_EPS = 1e-6
NEG_INF = -1e30
FORCE_BONUS = 1e4
ATTN_SCALE = HEAD_DIM ** -0.5

LANE = 128
A_TILE = 128
T_TILE = 256
ROW_TILE = 512
FFN_HALO = 8
VMEM_LIMIT = 56 * 1024 * 1024

_GRP_COLS = 3 * A_HEADS_PER_GROUP * HEAD_DIM
_A_COLS = len(A_GROUPS) * _GRP_COLS
_PB_COLS = B_HEADS * HEAD_DIM + 4 * B_KV_HEADS * HEAD_DIM
_CMP_COLS = 2 * B_KV_HEADS * HEAD_DIM
_C_COLS = 3 * C_HEADS * HEAD_DIM
_MG_COLS = 3 * D_MODEL
_BG_COLS = LANE
_OFF_A = 0
_OFF_PB = _OFF_A + _A_COLS
_OFF_CMP = _OFF_PB + _PB_COLS
_OFF_C = _OFF_CMP + _CMP_COLS
_OFF_MG = _OFF_C + _C_COLS
_OFF_BG = _OFF_MG + _MG_COLS
_W_COLS = _OFF_BG + _BG_COLS
_PROJ_CHUNK = 4 * LANE


def _params(n_grid):
    return pltpu.CompilerParams(dimension_semantics=("arbitrary",) * n_grid,
                                vmem_limit_bytes=VMEM_LIMIT)


def _rel_bucket_np(dist):
    n = np.maximum(dist, 0)
    exact = REL_BUCKETS // 2
    nf = np.maximum(n, 1).astype(np.float32)
    large = exact + (np.log(nf / np.float32(exact)) / np.float32(math.log(REL_MAX_DIST / exact))
                     * np.float32(REL_BUCKETS - exact)).astype(np.int32)
    return np.where(n < exact, n, np.minimum(large, REL_BUCKETS - 1)).astype(np.int32)


def _bucket_tile(tile, block_offset, dil, max_rel):
    rel = block_offset * tile + np.arange(tile)[:, None] - np.arange(tile)[None, :]
    ok = (rel >= 0) & (rel <= max_rel)
    return np.where(ok, _rel_bucket_np(rel * dil), -1).astype(np.int32)


def _bias_tiles_kernel(tbl_ref, idx_ref, o_ref, *, head0, relative):
    h = pl.program_id(0) + head0
    idx = idx_ref[...]
    acc = jnp.full(idx.shape, NEG_INF, F32)
    for b in range(REL_BUCKETS):
        acc = jnp.where(idx == b, tbl_ref[b, h], acc)
    if relative:
        acc = acc - tbl_ref[REL_BUCKETS - 1, h]
    o_ref[0] = acc


def _bias_tiles(rel_bias, idx, head0, n_heads, relative=False):
    r, c = idx.shape
    return pl.pallas_call(
        functools.partial(_bias_tiles_kernel, head0=head0, relative=relative),
        out_shape=jax.ShapeDtypeStruct((n_heads, r, c), F32),
        grid=(n_heads,),
        in_specs=[pl.BlockSpec(memory_space=pltpu.SMEM),
                  pl.BlockSpec((r, c), lambda h: (0, 0))],
        out_specs=pl.BlockSpec((1, r, c), lambda h: (h, 0, 0)),
        compiler_params=_params(1),
        name="bias_tiles",
    )(rel_bias, jnp.asarray(idx))


def _bias_tiles_t(rel_bias, idx_tiles, head0, n_heads):
    r, c = idx_tiles[0].shape
    flat = _bias_tiles(rel_bias, np.concatenate([t.T for t in idx_tiles], axis=1), head0, n_heads,
                       relative=True)
    return flat.reshape(n_heads, c, len(idx_tiles), r).transpose(0, 2, 1, 3)


def _rmsnorm_rows(x, g):
    return x * lax.rsqrt(jnp.mean(x * x, axis=-1, keepdims=True) + RMS_EPS) * g


def _proj_in_kernel(x_ref, g_ref, w_ref, pa0_ref, pa1_ref, pa2_ref, pb_ref, pcmp_ref, pc_ref,
                    mg_ref, bg_ref, regroup_ref):
    xn = _rmsnorm_rows(x_ref[...], g_ref[...]).astype(BF16)
    tm = xn.shape[0]

    def direct(out_ref):
        def sink(lo, hi, val):
            out_ref[:, lo:hi] = val.astype(out_ref.dtype)
        return sink

    def staged(slot0):
        def sink(lo, hi, val):
            for c in range(lo, hi, LANE):
                regroup_ref[slot0 + c // LANE] = val[:, c - lo:c - lo + LANE]
        return sink

    g1_slot, g2_slot, cmp_slot = 0, 3, 6
    sections = ((_OFF_A, _GRP_COLS, direct(pa0_ref)),
                (_OFF_A + _GRP_COLS, _GRP_COLS, staged(g1_slot)),
                (_OFF_A + 2 * _GRP_COLS, _GRP_COLS, staged(g2_slot)),
                (_OFF_PB, _PB_COLS, direct(pb_ref)), (_OFF_CMP, _CMP_COLS, staged(cmp_slot)),
                (_OFF_C, _C_COLS, direct(pc_ref)), (_OFF_MG, _MG_COLS, direct(mg_ref)),
                (_OFF_BG, _BG_COLS, direct(bg_ref)))
    for c0 in range(0, _W_COLS, _PROJ_CHUNK):
        c1 = min(c0 + _PROJ_CHUNK, _W_COLS)
        res = jnp.dot(xn, w_ref[:, c0:c1], preferred_element_type=F32)
        for off, width, sink in sections:
            lo, hi = max(c0, off), min(c1, off + width)
            if lo < hi:
                sink(lo - off, hi - off, res[:, lo - c0:hi - c0])

    for gi, slot0, out_ref in ((1, g1_slot, pa1_ref), (2, g2_slot, pa2_ref)):
        dil = A_GROUPS[gi][1]
        for r in range(dil):
            for j in range(3):
                out_ref[0, r, :, j * LANE:(j + 1) * LANE] = (
                    regroup_ref[slot0 + j, pl.ds(r, tm // dil, stride=dil), :].astype(BF16))
    for l in range(NSA_CMP_STRIDE):
        for kv in range(2):
            a = regroup_ref[cmp_slot + kv, pl.ds(l, tm // NSA_CMP_STRIDE, stride=NSA_CMP_STRIDE),
                            :].astype(BF16)
            for h in range(B_KV_HEADS):
                pcmp_ref[kv * B_KV_HEADS + h, :, l * HEAD_DIM:(l + 1) * HEAD_DIM] = (
                    a[:, h * HEAD_DIM:(h + 1) * HEAD_DIM])


def _proj_in(x2, g, w, *, layer, seq):
    t = x2.shape[0]
    tm = ROW_TILE
    b = t // seq
    tps = seq // tm
    row = lambda i: (i, 0)
    d1, d2 = A_GROUPS[1][1], A_GROUPS[2][1]
    sub = lambda i: (i // tps, 0, i % tps, 0)
    return pl.pallas_call(
        _proj_in_kernel,
        out_shape=(jax.ShapeDtypeStruct((t, _GRP_COLS), BF16),
                   jax.ShapeDtypeStruct((b, d1, seq // d1, _GRP_COLS), BF16),
                   jax.ShapeDtypeStruct((b, d2, seq // d2, _GRP_COLS), BF16),
                   jax.ShapeDtypeStruct((t, _PB_COLS), BF16),
                   jax.ShapeDtypeStruct((4, t // NSA_CMP_STRIDE, NSA_CMP_STRIDE * HEAD_DIM), BF16),
                   jax.ShapeDtypeStruct((t, _C_COLS), BF16),
                   jax.ShapeDtypeStruct((t, _MG_COLS), BF16),
                   jax.ShapeDtypeStruct((t, _BG_COLS), F32)),
        grid=(t // tm,),
        in_specs=[pl.BlockSpec((tm, D_MODEL), row),
                  pl.BlockSpec((1, D_MODEL), lambda i: (0, 0)),
                  pl.BlockSpec((None, D_MODEL, _W_COLS), lambda i: (layer, 0, 0))],
        out_specs=(pl.BlockSpec((tm, _GRP_COLS), row),
                   pl.BlockSpec((1, d1, tm // d1, _GRP_COLS), sub),
                   pl.BlockSpec((1, d2, tm // d2, _GRP_COLS), sub),
                   pl.BlockSpec((tm, _PB_COLS), row),
                   pl.BlockSpec((4, tm // NSA_CMP_STRIDE, NSA_CMP_STRIDE * HEAD_DIM),
                                lambda i: (0, i, 0)),
                   pl.BlockSpec((tm, _C_COLS), row),
                   pl.BlockSpec((tm, _MG_COLS), row),
                   pl.BlockSpec((tm, _BG_COLS), row)),
        scratch_shapes=[pltpu.VMEM((8, tm, LANE), F32)],
        compiler_params=_params(1),
        name="proj_in",
    )(x2, g, w)


def _w_in_segments():
    a = A_HEADS * HEAD_DIM
    bq0 = 3 * a
    bkc0 = bq0 + B_HEADS * HEAD_DIM
    bks0 = bkc0 + 2 * B_KV_HEADS * HEAD_DIM
    bg0 = bks0 + 4 * B_KV_HEADS * HEAD_DIM
    c0 = bg0 + B_HEADS * 3
    mg0 = c0 + _C_COLS
    gw = A_HEADS_PER_GROUP * HEAD_DIM
    segs = [(_OFF_A + (gi * 3 + j) * gw, j * a + gi * gw, gw)
            for gi in range(len(A_GROUPS)) for j in range(3)]
    segs += [(_OFF_PB, bq0, B_HEADS * HEAD_DIM),
             (_OFF_PB + B_HEADS * HEAD_DIM, bks0, bg0 - bks0),
             (_OFF_CMP, bkc0, _CMP_COLS), (_OFF_C, c0, _C_COLS), (_OFF_MG, mg0, _MG_COLS),
             (_OFF_BG, bg0, B_HEADS * 3)]
    return segs


def _prep_w_in_kernel(w_ref, o_ref):
    o_ref[0, :, _OFF_BG:] = jnp.zeros((o_ref.shape[1], _BG_COLS), BF16)
    for dst, src, width in _w_in_segments():
        for c in range(0, width, _PROJ_CHUNK):
            n = min(_PROJ_CHUNK, width - c)
            o_ref[0, :, dst + c:dst + c + n] = w_ref[0, :, src + c:src + c + n].astype(BF16)


def _prep_w_in(w):
    depth, rows, cols = w.shape
    rb = 128
    return pl.pallas_call(
        _prep_w_in_kernel,
        out_shape=jax.ShapeDtypeStruct((depth, rows, _W_COLS), BF16),
        grid=(depth, rows // rb),
        in_specs=[pl.BlockSpec((1, rb, cols), lambda l, r: (l, r, 0))],
        out_specs=pl.BlockSpec((1, rb, _W_COLS), lambda l, r: (l, r, 0)),
        compiler_params=_params(2),
        name="prep_w_in",
    )(w)


_A_UNROLL = 3


def _mixer_a_kernel(pa0_ref, pa1_ref, pa2_ref, bias_ref, o_ref, on_ref, ln_ref, *, seq):
    gw = A_HEADS_PER_GROUP * HEAD_DIM

    def band_block(src, l0, first, gi, rows):
        kr = pl.ds(l0, A_TILE) if first else pl.ds(l0 - A_TILE, 2 * A_TILE)
        q = src[pl.ds(l0, A_TILE), 0:gw] * ATTN_SCALE
        k, v = src[kr, gw:2 * gw], src[kr, 2 * gw:3 * gw]
        lane = lax.broadcasted_iota(jnp.int32, (A_TILE, gw), 1)
        out = lse = None
        for h in range(A_HEADS_PER_GROUP):
            own = (lane >= h * HEAD_DIM) & (lane < (h + 1) * HEAD_DIM)
            bias = bias_ref[A_HEADS_PER_GROUP * gi + h]
            s = _nt(q * own[0:1, :].astype(BF16), k) + (bias[:, A_TILE:] if first else bias)
            m = jnp.max(s, axis=-1, keepdims=True)
            e = jnp.exp(s - m)
            l = jnp.sum(e, axis=-1, keepdims=True)
            o_h = jnp.dot(e.astype(BF16), v, preferred_element_type=F32) / l
            lse_h = jnp.broadcast_to(m + jnp.log(l), (A_TILE, gw))
            out = o_h if out is None else jnp.where(own, o_h, out)
            lse = lse_h if lse is None else jnp.where(own, lse_h, lse)
        on_ref[gi, rows, :] = out
        ln_ref[gi, rows, :] = lse

    for gi, ((_, dil), pa_ref) in enumerate(zip(A_GROUPS, (pa0_ref, pa1_ref, pa2_ref))):
        n_blocks = seq // dil // A_TILE
        if dil == 1:
            band_block(pa_ref.at[0], 0, True, gi, pl.ds(0, A_TILE))
            assert (n_blocks - 1) % _A_UNROLL == 0

            def body0(it, c, pa_ref=pa_ref, gi=gi):
                for u in range(_A_UNROLL):
                    l0 = pl.multiple_of((1 + it * _A_UNROLL + u) * A_TILE, A_TILE)
                    band_block(pa_ref.at[0], l0, False, gi, pl.ds(l0, A_TILE))
                return c
            lax.fori_loop(0, (n_blocks - 1) // _A_UNROLL, body0, 0)
        elif n_blocks > 1:
            for r in range(dil):
                band_block(pa_ref.at[0, r], 0, True, gi, pl.ds(r, A_TILE, stride=dil))

            def body1(i, c, pa_ref=pa_ref, gi=gi, dil=dil):
                l0 = pl.multiple_of(i * A_TILE, A_TILE)
                for r in range(dil):
                    band_block(pa_ref.at[0, r], l0, False, gi, pl.ds(l0 * dil + r, A_TILE, stride=dil))
                return c
            lax.fori_loop(1, n_blocks, body1, 0)
        else:
            per_iter = 4

            def body2(it, c, pa_ref=pa_ref, gi=gi, dil=dil):
                for u in range(per_iter):
                    r = it * per_iter + u
                    band_block(pa_ref.at[0, r], 0, True, gi, pl.ds(r, A_TILE, stride=dil))
                return c
            lax.fori_loop(0, dil // per_iter, body2, 0)

    chunk = T_TILE
    for c0 in range(0, seq, chunk):
        rows = slice(c0, c0 + chunk)
        lse = [ln_ref[g, rows, :] for g in range(len(A_GROUPS))]
        top = functools.reduce(jnp.maximum, lse)
        ex = [jnp.exp(v - top) for v in lse]
        den = functools.reduce(lambda a, b: a + b, ex)
        for g in range(len(A_GROUPS)):
            o_ref[0, rows, g * gw:(g + 1) * gw] = (ex[g] / den * on_ref[g, rows, :]).astype(BF16)


def _mixer_a(pa0, pa1, pa2, bias):
    b, seq, _ = pa0.shape
    return pl.pallas_call(
        functools.partial(_mixer_a_kernel, seq=seq),
        out_shape=jax.ShapeDtypeStruct((b, seq, _GRP_COLS), BF16),
        grid=(b,),
        in_specs=[pl.BlockSpec((1, seq, _GRP_COLS), lambda i: (i, 0, 0)),
                  pl.BlockSpec((1,) + pa1.shape[1:], lambda i: (i, 0, 0, 0)),
                  pl.BlockSpec((1,) + pa2.shape[1:], lambda i: (i, 0, 0, 0)),
                  pl.BlockSpec(bias.shape, lambda i: (0, 0, 0))],
        out_specs=pl.BlockSpec((1, seq, _GRP_COLS), lambda i: (i, 0, 0)),
        scratch_shapes=[pltpu.VMEM((len(A_GROUPS), seq, LANE), F32),
                        pltpu.VMEM((len(A_GROUPS), seq, LANE), F32)],
        compiler_params=_params(1),
        name="mixer_a",
    )(pa0, pa1, pa2, bias)


def _compress_kernel(r_ref, pek_ref, pev_ref, w1k_ref, w1v_ref, w2k_ref, w2v_ref, o_ref):
    half = NSA_CMP_STRIDE * HEAD_DIM
    for kv, (pe_ref, w1_ref, w2_ref) in enumerate(((pek_ref, w1k_ref, w2k_ref),
                                                   (pev_ref, w1v_ref, w2v_ref))):
        for h in range(B_KV_HEADS):
            r = r_ref[kv * B_KV_HEADS + h, 0].astype(F32)
            lo = jnp.dot((r + pe_ref[:, :half]).astype(BF16), w1_ref[:half, :],
                         preferred_element_type=F32)
            hi = jnp.dot((r + pe_ref[:, half:]).astype(BF16), w1_ref[half:, :],
                         preferred_element_type=F32)
            hid = lo + pltpu.roll(hi, hi.shape[0] - 1, axis=0)
            act = jax.nn.gelu(hid).astype(BF16)
            o_ref[0, kv * B_KV_HEADS + h] = jnp.dot(act, w2_ref[...],
                                                    preferred_element_type=F32).astype(BF16)


def _compress(r, pe_k, pe_v, w1_k, w1_v, w2_k, w2_v):
    _, b, m, c = r.shape
    const2 = lambda i: (0, 0)
    return pl.pallas_call(
        _compress_kernel,
        out_shape=jax.ShapeDtypeStruct((b, 4, m, HEAD_DIM), BF16),
        grid=(b,),
        in_specs=[pl.BlockSpec((4, 1, m, c), lambda i: (0, i, 0, 0)),
                  pl.BlockSpec(pe_k.shape, const2), pl.BlockSpec(pe_v.shape, const2),
                  pl.BlockSpec(w1_k.shape, const2), pl.BlockSpec(w1_v.shape, const2),
                  pl.BlockSpec(w2_k.shape, const2), pl.BlockSpec(w2_v.shape, const2)],
        out_specs=pl.BlockSpec((1, 4, m, HEAD_DIM), lambda i: (i, 0, 0, 0)),
        compiler_params=_params(1),
        name="nsa_compress",
    )(r, pe_k, pe_v, w1_k, w1_v, w2_k, w2_v)


MASK_BIG = 2.0 ** 100
SEL_PAD = 32


def _nt(a, b, **kw):
    return lax.dot_general(a, b, (((1,), (1,)), ((), ())), preferred_element_type=F32, **kw)


def _top_mask_t(score, n_rows, n_top):
    sub = lax.broadcasted_iota(jnp.int32, score.shape, 0)
    rank = jnp.zeros(score.shape, F32)
    for j in range(n_rows):
        row = score[j:j + 1, :]
        ge = jnp.where(row >= score, 1.0, 0.0)
        gt = jnp.where(row > score, 1.0, 0.0)
        rank = rank + jnp.where(sub > j, ge, gt)
    return rank < n_top


def _top_keep(score, n_rows, n_top, block):
    free = n_top * block
    ranked = jnp.where(_top_mask_t(score[:, free:], n_rows, n_top), 1.0, 0.0)
    return jnp.concatenate([jnp.ones((score.shape[0], free), F32), ranked], axis=1)


BAND_SLOTS = 4


def _loop_pairs(lo, hi, body):
    n = jnp.maximum(hi - lo, 0)

    def pair(p, c):
        body(lo + 2 * p)
        body(lo + 2 * p + 1)
        return c
    lax.fori_loop(0, n // 2, pair, 0)
    pl.when(n % 2 == 1)(lambda: body(hi - 1))


def _two_pass_attention(i, streams, band_streams=()):
    groups = ((streams, lambda j: j), (band_streams, lambda j: j % BAND_SLOTS))

    def key_rows(j):
        return pl.ds(pl.multiple_of(j * T_TILE, T_TILE), T_TILE)

    def logits(group, j, d):
        some, slot = group
        for k_tile, q_t, _, bias_tile, s_ref, rmp_ref, _ in some:
            s = jnp.dot(k_tile(key_rows(j)), q_t(), preferred_element_type=F32)
            if d is not None:
                s = s + bias_tile(d)
            s_ref[slot(j)] = s
            fold = jnp.max(s.reshape(s.shape[0] // 8, 8, s.shape[1]), axis=0)
            rmp_ref[...] = fold if d == 0 else jnp.maximum(rmp_ref[...], fold)

    def near(n_full, n_band):
        for d in range(n_full):
            logits(groups[0], i - d, d)
        for d in range(n_band):
            logits(groups[1], i - d, d)

    if band_streams:
        pl.when(i == 0)(lambda: near(1, 1))
        pl.when(i == 1)(lambda: near(2, 2))
        pl.when(i >= 2)(lambda: near(2, 3))
    else:
        logits(groups[0], i, 0)
        pl.when(i >= 1)(lambda: logits(groups[0], i - 1, 1))

    _loop_pairs(0, i - 1, lambda j: logits(groups[0], j, None))

    tops = {}
    for some, _ in groups:
        for stream in some:
            rmp_ref, acc_ref = stream[-2:]
            tops[id(rmp_ref)] = jnp.max(rmp_ref[...], axis=0, keepdims=True)
            acc_ref[...] = jnp.zeros(acc_ref.shape, F32)

    def pv(active):
        def body(j):
            for some, slot in active:
                for _, _, vt_ref, _, s_ref, rmp_ref, acc_ref in some:
                    e = jnp.exp(s_ref[slot(j)] - tops[id(rmp_ref)])
                    acc_ref[...] += jnp.dot(vt_ref[j], e.astype(BF16), preferred_element_type=F32)
        return body

    if band_streams:
        band_lo = jnp.maximum(i - 2, 0)
        _loop_pairs(0, band_lo, pv(groups[:1]))
        pv_all = pv(groups)

        def last_tiles(n):
            for d in reversed(range(n)):
                pv_all(i - d)

        pl.when(i == 0)(lambda: last_tiles(1))
        pl.when(i == 1)(lambda: last_tiles(2))
        pl.when(i >= 2)(lambda: last_tiles(3))
    else:
        _loop_pairs(0, i + 1, pv(groups[:1]))

    outs = []
    for some, _ in groups:
        for *_, acc_ref in some:
            acc = acc_ref[...]
            outs.append(acc[:HEAD_DIM, :] / acc[HEAD_DIM:HEAD_DIM + 1, :])
    return outs


def _eye(n, dtype=BF16):
    return (lax.broadcasted_iota(jnp.int32, (n, n), 0)
            == lax.broadcasted_iota(jnp.int32, (n, n), 1)).astype(dtype)


def _transposed_values(v_ref_rows, n_tiles):
    rows = lax.broadcasted_iota(jnp.int32, (LANE, HEAD_DIM), 0)
    cols = lax.broadcasted_iota(jnp.int32, (LANE, HEAD_DIM), 1)
    place = (rows == cols).astype(BF16)
    ones_row = (lax.broadcasted_iota(jnp.int32, (LANE, T_TILE), 0) == HEAD_DIM).astype(F32)
    for j in range(n_tiles):
        v = v_ref_rows(slice(j * T_TILE, (j + 1) * T_TILE))
        yield j, (_nt(place, v) + ones_row).astype(BF16)


def _with_block_onehot(k, block):
    n = k.shape[0]
    rows = lax.broadcasted_iota(jnp.int32, (HEAD_DIM, LANE), 0)
    cols = lax.broadcasted_iota(jnp.int32, (HEAD_DIM, LANE), 1)
    place = (rows == cols).astype(BF16)
    lane = lax.broadcasted_iota(jnp.int32, (n, LANE), 1)
    hot = (lane - HEAD_DIM == lax.broadcasted_iota(jnp.int32, (n, LANE), 0) // block).astype(F32)
    return (jnp.dot(k, place, preferred_element_type=F32) + hot).astype(BF16)


def _augment_q_t(q_t, allowed_t):
    pen = (allowed_t - 1.0) * MASK_BIG
    return jnp.concatenate([q_t, pen, jnp.zeros(pen.shape, F32)], axis=0).astype(BF16)


_N_SLC = 32
_G_ROWS = B_GROUP * T_TILE


def _nsa_kernel(pb_ref, cmp_ref, bg_ref, bias_ref, o_ref,
                kaug_ref, vst_ref, vwt_ref, eye_ref, sig_ref, q3t_ref, qaugt_ref, s_ref, sw_ref,
                rmp_ref, acc_ref, ocmp_ref, *, seq):
    n_tiles = seq // T_TILE
    n_cmp_rows = seq // NSA_CMP_STRIDE
    q_cols = B_HEADS * HEAD_DIM
    ks0, vs0, kw0, vw0 = (q_cols + i * B_KV_HEADS * HEAD_DIM for i in range(4))

    def head_cols(c0, h):
        return slice(c0 + h * HEAD_DIM, c0 + (h + 1) * HEAD_DIM)

    j_id = lax.broadcasted_iota(jnp.int32, (_N_SLC, n_cmp_rows), 0)
    c_id = lax.broadcasted_iota(jnp.int32, (_N_SLC, n_cmp_rows), 1)
    overlap_t = ((c_id * NSA_CMP_STRIDE < (j_id + 1) * NSA_SLC_BLOCK)
                 & (c_id * NSA_CMP_STRIDE + NSA_CMP_BLOCK > j_id * NSA_SLC_BLOCK)).astype(F32)
    eye_ref[...] = _eye(T_TILE)
    for h in range(B_KV_HEADS):
        kaug_ref[h] = _with_block_onehot(pb_ref[0, :, head_cols(ks0, h)], NSA_SLC_BLOCK)
        for j, vt in _transposed_values(lambda r, h=h: pb_ref[0, r, head_cols(vs0, h)], n_tiles):
            vst_ref[h, j] = vt
        for j, vt in _transposed_values(lambda r, h=h: pb_ref[0, r, head_cols(vw0, h)], n_tiles):
            vwt_ref[h, j] = vt

    def tiles(a):
        return [a[:, j * T_TILE:(j + 1) * T_TILE] for j in range(n_tiles)]

    pick = (lax.broadcasted_iota(jnp.int32, (SEL_PAD, _BG_COLS), 0)
            == lax.broadcasted_iota(jnp.int32, (SEL_PAD, _BG_COLS), 1)).astype(F32)
    sig_all = jax.nn.sigmoid(_nt(pick, bg_ref[0], precision=lax.Precision.HIGHEST))
    for j, t in enumerate(tiles(sig_all)):
        sig_ref[j] = t

    eye_h = _eye(HEAD_DIM)
    t_lane = lax.broadcasted_iota(jnp.int32, (n_cmp_rows, seq), 1)
    c_end = (lax.broadcasted_iota(jnp.int32, (n_cmp_rows, seq), 0) * NSA_CMP_STRIDE
             + NSA_CMP_BLOCK - 1)
    ok = (c_end <= t_lane) & (c_end < seq)
    ok_f = ok.astype(F32)
    j_sub = lax.broadcasted_iota(jnp.int32, (_N_SLC, seq), 0)
    jt = lax.broadcasted_iota(jnp.int32, (_N_SLC, seq), 1) // NSA_SLC_BLOCK
    forced = ((j_sub == 0) | (j_sub == jt) | (j_sub == jt - 1)).astype(F32)
    valid = j_sub <= jt

    for h in range(B_KV_HEADS):
        kc = cmp_ref[0, h]
        vct = _nt(eye_h, cmp_ref[0, B_KV_HEADS + h]).astype(BF16)
        q_ts, p_sum = [], None
        for g in range(B_GROUP):
            q_t = _nt(eye_h, pb_ref[0, :, head_cols(0, h * B_GROUP + g)]) * ATTN_SCALE
            q_ts.append(q_t)
            lc = jnp.where(ok, jnp.dot(kc, q_t.astype(BF16), preferred_element_type=F32), NEG_INF)
            ec = jnp.exp(lc - jnp.max(lc, axis=0, keepdims=True))
            pc = ec / jnp.sum(ec, axis=0, keepdims=True) * ok_f
            o_cmp = jnp.dot(vct, pc.astype(BF16), preferred_element_type=F32)
            for j, t in enumerate(tiles(o_cmp)):
                ocmp_ref[h, j, :, g * T_TILE:(g + 1) * T_TILE] = t
            p_sum = pc if p_sum is None else p_sum + pc
        imp_t = jnp.dot(overlap_t, p_sum, precision=lax.Precision.HIGHEST,
                        preferred_element_type=F32)
        score = jnp.where(valid, imp_t + FORCE_BONUS * forced, NEG_INF)
        sel_t = jnp.where(valid, _top_keep(score, _N_SLC, NSA_SLC_TOPK, NSA_SLC_BLOCK), 0.0)
        pen_tiles = tiles((sel_t - 1.0) * MASK_BIG)
        q_tiles = [tiles(q_t) for q_t in q_ts]
        for j in range(n_tiles):
            q3t = jnp.concatenate([q_tiles[g][j] for g in range(B_GROUP)], axis=1)
            pen = jnp.concatenate([pen_tiles[j]] * B_GROUP, axis=1)
            q3t_ref[h, j] = q3t.astype(BF16)
            qaugt_ref[h, j] = jnp.concatenate([q3t, pen, jnp.zeros(pen.shape, F32)],
                                              axis=0).astype(BF16)

    def tile_body(i, carry):
        rows = pl.ds(pl.multiple_of(i * T_TILE, T_TILE), T_TILE)

        outs = _two_pass_attention(
            i,
            [(lambda kr, h=h: kaug_ref[h, kr, :], lambda h=h: qaugt_ref[h, i], vst_ref.at[h],
              lambda d, h=h: bias_ref[h, d], s_ref.at[h], rmp_ref.at[0, h], acc_ref.at[0, h])
             for h in range(B_KV_HEADS)],
            [(lambda kr, h=h: pb_ref[0, kr, head_cols(kw0, h)], lambda h=h: q3t_ref[h, i],
              vwt_ref.at[h], lambda d, h=h: bias_ref[h, d], sw_ref.at[h], rmp_ref.at[1, h],
              acc_ref.at[1, h]) for h in range(B_KV_HEADS)])
        oslc, owin = outs[:B_KV_HEADS], outs[B_KV_HEADS:]

        sig_t = sig_ref[i]
        heads = []
        for h in range(B_KV_HEADS):
            for g in range(B_GROUP):
                head = h * B_GROUP + g
                gc = slice(g * T_TILE, (g + 1) * T_TILE)
                heads.append(sig_t[3 * head:3 * head + 1, :] * ocmp_ref[h, i, :, gc]
                             + sig_t[3 * head + 1:3 * head + 2, :] * oslc[h][:, gc]
                             + sig_t[3 * head + 2:3 * head + 3, :] * owin[h][:, gc])
        o_t = jnp.concatenate(heads, axis=0).astype(BF16)
        o_ref[0, rows, :] = _nt(eye_ref[...], o_t).astype(o_ref.dtype)
        return carry

    lax.fori_loop(0, n_tiles, tile_body, 0)


def _nsa(pb, cmp, bg, bias):
    b, seq, _ = pb.shape
    n_tiles = seq // T_TILE
    return pl.pallas_call(
        functools.partial(_nsa_kernel, seq=seq),
        out_shape=jax.ShapeDtypeStruct((b, seq, B_HEADS * HEAD_DIM), BF16),
        grid=(b,),
        in_specs=[pl.BlockSpec((1, seq, _PB_COLS), lambda i: (i, 0, 0)),
                  pl.BlockSpec((1,) + cmp.shape[1:], lambda i: (i, 0, 0, 0)),
                  pl.BlockSpec((1, seq, _BG_COLS), lambda i: (i, 0, 0)),
                  pl.BlockSpec(bias.shape, lambda i: (0, 0, 0, 0))],
        out_specs=pl.BlockSpec((1, seq, B_HEADS * HEAD_DIM), lambda i: (i, 0, 0)),
        scratch_shapes=[pltpu.VMEM((B_KV_HEADS, seq, LANE), BF16),
                        pltpu.VMEM((B_KV_HEADS, n_tiles, LANE, T_TILE), BF16),
                        pltpu.VMEM((B_KV_HEADS, n_tiles, LANE, T_TILE), BF16),
                        pltpu.VMEM((T_TILE, T_TILE), BF16),
                        pltpu.VMEM((n_tiles, SEL_PAD, T_TILE), F32),
                        pltpu.VMEM((B_KV_HEADS, n_tiles, HEAD_DIM, _G_ROWS), BF16),
                        pltpu.VMEM((B_KV_HEADS, n_tiles, LANE, _G_ROWS), BF16),
                        pltpu.VMEM((B_KV_HEADS, n_tiles, T_TILE, _G_ROWS), F32),
                        pltpu.VMEM((B_KV_HEADS, BAND_SLOTS, T_TILE, _G_ROWS), F32),
                        pltpu.VMEM((2, B_KV_HEADS, 8, _G_ROWS), F32),
                        pltpu.VMEM((2, B_KV_HEADS, LANE, _G_ROWS), F32),
                        pltpu.VMEM((B_KV_HEADS, n_tiles, HEAD_DIM, _G_ROWS), F32)],
        compiler_params=_params(1),
        name="nsa",
    )(pb, cmp, bg, bias)


def _moba_kernel(pc_ref, bias_ref, o_ref, kaug_ref, vt_ref, eye_ref, qaugt_ref, s_ref,
                 rmp_ref, acc_ref, *, seq):
    n_blk = seq // MOBA_BLOCK
    width = C_HEADS * HEAD_DIM
    eye_ref[...] = _eye(T_TILE)

    def cols(j, h):
        return slice(j * width + h * HEAD_DIM, j * width + (h + 1) * HEAD_DIM)

    n_sub = lax.broadcasted_iota(jnp.int32, (SEL_PAD, seq), 0)
    own = lax.broadcasted_iota(jnp.int32, (SEL_PAD, seq), 1) // MOBA_BLOCK
    past = n_sub < own
    eye_h = _eye(HEAD_DIM)
    for h in range(C_HEADS):
        kaug_ref[h] = _with_block_onehot(pc_ref[0, :, cols(1, h)], MOBA_BLOCK)
        for j, vt in _transposed_values(lambda r, h=h: pc_ref[0, r, cols(2, h)], n_blk):
            vt_ref[h, j] = vt
        kmean = jnp.concatenate(
            [jnp.mean(pc_ref[0, n * MOBA_BLOCK:(n + 1) * MOBA_BLOCK, cols(1, h)].astype(F32), axis=0,
                      keepdims=True) for n in range(n_blk)]
            + [jnp.zeros((SEL_PAD - n_blk, HEAD_DIM), F32)], axis=0)
        q_t = _nt(eye_h, pc_ref[0, :, cols(0, h)])
        gate_t = jnp.dot(kmean, q_t, precision=lax.Precision.HIGHEST,
                         preferred_element_type=F32)
        score = jnp.where(past, gate_t, NEG_INF)
        picked = past & (_top_keep(score, n_blk, MOBA_TOPK, MOBA_BLOCK) > 0.5)
        allowed_t = jnp.where(picked | (n_sub == own), 1.0, 0.0)
        qaug_t = _augment_q_t(q_t * ATTN_SCALE, allowed_t)
        for j in range(n_blk):
            qaugt_ref[h, j] = qaug_t[:, j * T_TILE:(j + 1) * T_TILE]

    def tile_body(i, carry):
        rows = pl.ds(pl.multiple_of(i * MOBA_BLOCK, MOBA_BLOCK), MOBA_BLOCK)
        outs = _two_pass_attention(
            i, [(lambda kr, h=h: kaug_ref[h, kr, :], lambda h=h: qaugt_ref[h, i], vt_ref.at[h],
                 lambda d, h=h: bias_ref[h, d], s_ref.at[h], rmp_ref.at[h], acc_ref.at[h])
                for h in range(C_HEADS)])
        o_t = jnp.concatenate(outs, axis=0).astype(BF16)
        o_ref[0, rows, :] = _nt(eye_ref[...], o_t).astype(o_ref.dtype)
        return carry

    lax.fori_loop(0, n_blk, tile_body, 0)


def _moba(pc, bias):
    b, seq, _ = pc.shape
    width = C_HEADS * HEAD_DIM
    n_tiles = seq // T_TILE
    return pl.pallas_call(
        functools.partial(_moba_kernel, seq=seq),
        out_shape=jax.ShapeDtypeStruct((b, seq, width), BF16),
        grid=(b,),
        in_specs=[pl.BlockSpec((1, seq, _C_COLS), lambda i: (i, 0, 0)),
                  pl.BlockSpec(bias.shape, lambda i: (0, 0, 0, 0))],
        out_specs=pl.BlockSpec((1, seq, width), lambda i: (i, 0, 0)),
        scratch_shapes=[pltpu.VMEM((C_HEADS, seq, LANE), BF16),
                        pltpu.VMEM((C_HEADS, n_tiles, LANE, T_TILE), BF16),
                        pltpu.VMEM((T_TILE, T_TILE), BF16),
                        pltpu.VMEM((C_HEADS, n_tiles, LANE, T_TILE), BF16),
                        pltpu.VMEM((C_HEADS, n_tiles, T_TILE, T_TILE), F32),
                        pltpu.VMEM((C_HEADS, 8, T_TILE), F32),
                        pltpu.VMEM((C_HEADS, LANE, T_TILE), F32)],
        compiler_params=_params(1),
        name="moba",
    )(pc, bias)


def _merge_kernel(oa_ref, ob_ref, oc_ref, mg_ref, x_ref, wb_ref, wo_ref, y_ref):
    r0 = A_HEADS * HEAD_DIM
    r1 = r0 + B_HEADS * HEAD_DIM
    ya = jnp.dot(oa_ref[...], wb_ref[0:r0, :], preferred_element_type=F32)
    yb = jnp.dot(ob_ref[...], wb_ref[r0:r1, :], preferred_element_type=F32)
    yc = jnp.dot(oc_ref[...], wb_ref[r1:, :], preferred_element_type=F32)
    merged = (jax.nn.sigmoid(mg_ref[:, 0:D_MODEL].astype(F32)) * ya
              + jax.nn.sigmoid(mg_ref[:, D_MODEL:2 * D_MODEL].astype(F32)) * yb
              + jax.nn.sigmoid(mg_ref[:, 2 * D_MODEL:].astype(F32)) * yc)
    y_ref[...] = x_ref[...] + jnp.dot(merged.astype(BF16), wo_ref[...], preferred_element_type=F32)


def _merge(oa, ob, oc, mg, x2, wb, wo):
    t = x2.shape[0]
    tm = ROW_TILE
    row = lambda i: (i, 0)
    return pl.pallas_call(
        _merge_kernel,
        out_shape=jax.ShapeDtypeStruct((t, D_MODEL), F32),
        grid=(t // tm,),
        in_specs=[pl.BlockSpec((tm, oa.shape[1]), row),
                  pl.BlockSpec((tm, ob.shape[1]), row), pl.BlockSpec((tm, oc.shape[1]), row),
                  pl.BlockSpec((tm, _MG_COLS), row), pl.BlockSpec((tm, D_MODEL), row),
                  pl.BlockSpec(wb.shape, lambda i: (0, 0)), pl.BlockSpec(wo.shape, lambda i: (0, 0))],
        out_specs=pl.BlockSpec((tm, D_MODEL), row),
        compiler_params=_params(1),
        name="merge",
    )(oa, ob, oc, mg, x2, wb, wo)


MXU_WIDTH = 256
_FF_CHUNK_TILES = 6
_FF_BOUNDS = tuple(range(0, D_FF, _FF_CHUNK_TILES * MXU_WIDTH)) + (D_FF,)
_FF_CHUNKS = tuple(zip(_FF_BOUNDS[:-1], _FF_BOUNDS[1:]))


def _ffn_kernel(x_ref, halo_ref, g_ref, wu_ref, cw_ref, cb_ref, wd_ref, gf_ref, y_ref, xn_ref,
                *, tiles_per_seq, final_norm):
    i = pl.program_id(0)
    x = x_ref[...]
    g = g_ref[...]
    keep = (i % tiles_per_seq != 0).astype(F32)
    xn_ref[0:FFN_HALO, :] = (_rmsnorm_rows(halo_ref[...], g) * keep).astype(BF16)
    xn_ref[FFN_HALO:, :] = _rmsnorm_rows(x, g).astype(BF16)
    xn = xn_ref[...]
    rows = xn.shape[0]

    def conv(c0, c1):
        hcol = jnp.dot(xn, wu_ref[:, c0:c1], preferred_element_type=F32)
        out = (cw_ref[2:3, c0:c1] * hcol
               + cw_ref[1:2, c0:c1] * pltpu.roll(hcol, 1, axis=0)
               + cw_ref[0:1, c0:c1] * pltpu.roll(hcol, 2, axis=0)
               + cb_ref[:, c0:c1])
        return out[FFN_HALO:rows]

    acc = x
    for c0, c1 in _FF_CHUNKS:
        a = conv(c0, c1)
        u = conv(D_FF + c0, D_FF + c1)
        act = (a * jax.nn.sigmoid(a) * u).astype(BF16)
        acc = acc + jnp.dot(act, wd_ref[c0:c1, :], preferred_element_type=F32)
    if final_norm:
        acc = _rmsnorm_rows(acc, gf_ref[...])
    y_ref[...] = acc


def _ffn(x2, g, wu, cw, cb, wd, gf, *, seq, final_norm):
    t = x2.shape[0]
    tm = ROW_TILE
    const = lambda i: (0, 0)
    halo_blocks = tm // FFN_HALO
    return pl.pallas_call(
        functools.partial(_ffn_kernel, tiles_per_seq=seq // tm, final_norm=final_norm),
        out_shape=jax.ShapeDtypeStruct((t, D_MODEL), F32),
        grid=(t // tm,),
        in_specs=[pl.BlockSpec((tm, D_MODEL), lambda i: (i, 0)),
                  pl.BlockSpec((FFN_HALO, D_MODEL), lambda i: (jnp.maximum(i * halo_blocks - 1, 0), 0)),
                  pl.BlockSpec((1, D_MODEL), const),
                  pl.BlockSpec(wu.shape, const), pl.BlockSpec(cw.shape, const),
                  pl.BlockSpec(cb.shape, const), pl.BlockSpec(wd.shape, const),
                  pl.BlockSpec((1, D_MODEL), const)],
        out_specs=pl.BlockSpec((tm, D_MODEL), lambda i: (i, 0)),
        scratch_shapes=[pltpu.VMEM((tm + FFN_HALO, D_MODEL), BF16)],
        compiler_params=_params(1),
        name="conv_ffn",
    )(x2, x2, g, wu, cw, cb, wd, gf)


def kernel(x, rel_bias, norm_mix, w_in, cmp_pe_k, cmp_w1_k, cmp_w2_k, cmp_pe_v, cmp_w1_v, cmp_w2_v,
           w_branch, w_out, norm_ffn, w_up, conv_w, conv_b, w_down, norm_final):
    b, s, d = x.shape
    depth = w_in.shape[0]
    t = b * s
    assert d == D_MODEL and s % T_TILE == 0 and s // NSA_SLC_BLOCK == _N_SLC and t % ROW_TILE == 0
    assert s % ROW_TILE == 0 and all(s % (dil * A_TILE) == 0 for _, dil in A_GROUPS)
    assert all(win // dil == A_TILE for win, dil in A_GROUPS) and A_GROUPS[0][1] == 1
    assert all(ROW_TILE % dil == 0 for _, dil in A_GROUPS)

    hg = A_HEADS_PER_GROUP
    bias_a = jnp.concatenate([
        _bias_tiles(rel_bias,
                    np.concatenate([_bucket_tile(A_TILE, 1, dil, win // dil),
                                    _bucket_tile(A_TILE, 0, dil, win // dil)], axis=1),
                    gi * hg, hg)
        for gi, (win, dil) in enumerate(A_GROUPS)], axis=0)
    assert (_bucket_tile(T_TILE, 2, 1, 3 * T_TILE) == REL_BUCKETS - 1).all()
    bias_b = _bias_tiles_t(rel_bias, [_bucket_tile(T_TILE, d, 1, NSA_WINDOW - 1) for d in (0, 1, 2)],
                           A_HEADS, B_HEADS)
    bias_b = bias_b.reshape(B_KV_HEADS, B_GROUP, 3, T_TILE, T_TILE).transpose(0, 2, 3, 1, 4).reshape(
        B_KV_HEADS, 3, T_TILE, _G_ROWS)
    bias_c = _bias_tiles_t(rel_bias, [_bucket_tile(T_TILE, d, 1, s) for d in (0, 1)],
                           A_HEADS + B_HEADS, C_HEADS)

    w_in_k = _prep_w_in(w_in)
    x2 = x.reshape(t, d)
    for i in range(depth):
        pa0, pa1, pa2, pb, pcmp, pc, mg, bg = _proj_in(x2, norm_mix[i][None, :], w_in_k, layer=i,
                                                       seq=s)

        oa = _mixer_a(pa0.reshape(b, s, _GRP_COLS), pa1, pa2, bias_a)

        cmp = _compress(pcmp.reshape(4, b, s // NSA_CMP_STRIDE, NSA_CMP_STRIDE * HEAD_DIM),
                        cmp_pe_k[i].reshape(1, -1), cmp_pe_v[i].reshape(1, -1),
                        cmp_w1_k[i].astype(BF16), cmp_w1_v[i].astype(BF16),
                        cmp_w2_k[i].astype(BF16), cmp_w2_v[i].astype(BF16))
        ob = _nsa(pb.reshape(b, s, _PB_COLS), cmp, bg.reshape(b, s, _BG_COLS), bias_b)

        oc = _moba(pc.reshape(b, s, _C_COLS), bias_c)

        x2 = _merge(oa.reshape(t, -1), ob.reshape(t, -1), oc.reshape(t, -1), mg, x2,
                    w_branch[i].astype(BF16), w_out[i].astype(BF16))
        x2 = _ffn(x2, norm_ffn[i][None, :], w_up[i].astype(BF16), conv_w[i], conv_b[i][None, :],
                  w_down[i].astype(BF16), norm_final[None, :], seq=s, final_norm=(i == depth - 1))
    return x2.reshape(b, s, d)
```

```python
import functools
import math

import jax
import jax.numpy as jnp
import numpy as np
from jax import lax
from jax.experimental import pallas as pl
from jax.experimental.pallas import tpu as pltpu

F32 = jnp.float32
BF16 = jnp.bfloat16

D_MODEL = 1024
HEAD_DIM = 64
A_GROUPS = ((128, 1), (512, 4), (2048, 16))
A_HEADS_PER_GROUP = 2
A_HEADS = 6
B_HEADS = 6
B_KV_HEADS = 2
B_GROUP = 3
C_HEADS = 4
NSA_CMP_BLOCK = 32
NSA_CMP_STRIDE = 16
NSA_SLC_BLOCK = 64
NSA_SLC_TOPK = 16
NSA_WINDOW = 512
MOBA_BLOCK = 256
MOBA_TOPK = 3
REL_BUCKETS = 32
REL_MAX_DIST = 128
D_FF = 2816
RMS_EPS = 1e-6
NEG_INF = -1e30
FORCE_BONUS = 1e4
ATTN_SCALE = HEAD_DIM ** -0.5

LANE = 128
A_TILE = 128
T_TILE = 256
ROW_TILE = 512
FFN_HALO = 8
VMEM_LIMIT = 56 * 1024 * 1024

_GRP_COLS = 3 * A_HEADS_PER_GROUP * HEAD_DIM
_A_COLS = len(A_GROUPS) * _GRP_COLS
_PB_COLS = B_HEADS * HEAD_DIM + 4 * B_KV_HEADS * HEAD_DIM
_CMP_COLS = 2 * B_KV_HEADS * HEAD_DIM
_C_COLS = 3 * C_HEADS * HEAD_DIM
_MG_COLS = 3 * D_MODEL
_BG_COLS = LANE
_OFF_A = 0
_OFF_PB = _OFF_A + _A_COLS
_OFF_CMP = _OFF_PB + _PB_COLS
_OFF_C = _OFF_CMP + _CMP_COLS
_OFF_MG = _OFF_C + _C_COLS
_OFF_BG = _OFF_MG + _MG_COLS
_W_COLS = _OFF_BG + _BG_COLS
_PROJ_CHUNK = 4 * LANE


def _params(n_grid):
    return pltpu.CompilerParams(dimension_semantics=("arbitrary",) * n_grid,
                                vmem_limit_bytes=VMEM_LIMIT)


def _rel_bucket_np(dist):
    n = np.maximum(dist, 0)
    exact = REL_BUCKETS // 2
    nf = np.maximum(n, 1).astype(np.float32)
    large = exact + (np.log(nf / np.float32(exact)) / np.float32(math.log(REL_MAX_DIST / exact))
                     * np.float32(REL_BUCKETS - exact)).astype(np.int32)
    return np.where(n < exact, n, np.minimum(large, REL_BUCKETS - 1)).astype(np.int32)


def _bucket_tile(tile, block_offset, dil, max_rel):
    rel = block_offset * tile + np.arange(tile)[:, None] - np.arange(tile)[None, :]
    ok = (rel >= 0) & (rel <= max_rel)
    return np.where(ok, _rel_bucket_np(rel * dil), -1).astype(np.int32)


def _bias_tiles_kernel(tbl_ref, idx_ref, o_ref, *, head0, relative):
    h = pl.program_id(0) + head0
    idx = idx_ref[...]
    acc = jnp.full(idx.shape, NEG_INF, F32)
    for b in range(REL_BUCKETS):
        acc = jnp.where(idx == b, tbl_ref[b, h], acc)
    if relative:
        acc = acc - tbl_ref[REL_BUCKETS - 1, h]
    o_ref[0] = acc


def _bias_tiles(rel_bias, idx, head0, n_heads, relative=False):
    r, c = idx.shape
    return pl.pallas_call(
        functools.partial(_bias_tiles_kernel, head0=head0, relative=relative),
        out_shape=jax.ShapeDtypeStruct((n_heads, r, c), F32),
        grid=(n_heads,),
        in_specs=[pl.BlockSpec(memory_space=pltpu.SMEM),
                  pl.BlockSpec((r, c), lambda h: (0, 0))],
        out_specs=pl.BlockSpec((1, r, c), lambda h: (h, 0, 0)),
        compiler_params=_params(1),
        name="bias_tiles",
    )(rel_bias, jnp.asarray(idx))


def _bias_tiles_t(rel_bias, idx_tiles, head0, n_heads):
    r, c = idx_tiles[0].shape
    flat = _bias_tiles(rel_bias, np.concatenate([t.T for t in idx_tiles], axis=1), head0, n_heads,
                       relative=True)
    return flat.reshape(n_heads, c, len(idx_tiles), r).transpose(0, 2, 1, 3)


def _rmsnorm_rows(x, g):
    return x * lax.rsqrt(jnp.mean(x * x, axis=-1, keepdims=True) + RMS_EPS) * g


def _proj_in_kernel(x_ref, g_ref, w_ref, pa0_ref, pa1_ref, pa2_ref, pb_ref, pcmp_ref, pc_ref,
                    mg_ref, bg_ref, regroup_ref):
    xn = _rmsnorm_rows(x_ref[...], g_ref[...]).astype(BF16)
    tm = xn.shape[0]

    def direct(out_ref):
        def sink(lo, hi, val):
            out_ref[:, lo:hi] = val.astype(out_ref.dtype)
        return sink

    def staged(slot0):
        def sink(lo, hi, val):
            for c in range(lo, hi, LANE):
                regroup_ref[slot0 + c // LANE] = val[:, c - lo:c - lo + LANE]
        return sink

    g1_slot, g2_slot, cmp_slot = 0, 3, 6
    sections = ((_OFF_A, _GRP_COLS, direct(pa0_ref)),
                (_OFF_A + _GRP_COLS, _GRP_COLS, staged(g1_slot)),
                (_OFF_A + 2 * _GRP_COLS, _GRP_COLS, staged(g2_slot)),
                (_OFF_PB, _PB_COLS, direct(pb_ref)), (_OFF_CMP, _CMP_COLS, staged(cmp_slot)),
                (_OFF_C, _C_COLS, direct(pc_ref)), (_OFF_MG, _MG_COLS, direct(mg_ref)),
                (_OFF_BG, _BG_COLS, direct(bg_ref)))
    for c0 in range(0, _W_COLS, _PROJ_CHUNK):
        c1 = min(c0 + _PROJ_CHUNK, _W_COLS)
        res = jnp.dot(xn, w_ref[:, c0:c1], preferred_element_type=F32)
        for off, width, sink in sections:
            lo, hi = max(c0, off), min(c1, off + width)
            if lo < hi:
                sink(lo - off, hi - off, res[:, lo - c0:hi - c0])

    for gi, slot0, out_ref in ((1, g1_slot, pa1_ref), (2, g2_slot, pa2_ref)):
        dil = A_GROUPS[gi][1]
        for r in range(dil):
            for j in range(3):
                out_ref[0, r, :, j * LANE:(j + 1) * LANE] = (
                    regroup_ref[slot0 + j, pl.ds(r, tm // dil, stride=dil), :].astype(BF16))
    for l in range(NSA_CMP_STRIDE):
        for kv in range(2):
            a = regroup_ref[cmp_slot + kv, pl.ds(l, tm // NSA_CMP_STRIDE, stride=NSA_CMP_STRIDE),
                            :].astype(BF16)
            for h in range(B_KV_HEADS):
                pcmp_ref[kv * B_KV_HEADS + h, :, l * HEAD_DIM:(l + 1) * HEAD_DIM] = (
                    a[:, h * HEAD_DIM:(h + 1) * HEAD_DIM])


def _proj_in(x2, g, w, *, layer, seq):
    t = x2.shape[0]
    tm = ROW_TILE
    b = t // seq
    tps = seq // tm
    row = lambda i: (i, 0)
    d1, d2 = A_GROUPS[1][1], A_GROUPS[2][1]
    sub = lambda i: (i // tps, 0, i % tps, 0)
    return pl.pallas_call(
        _proj_in_kernel,
        out_shape=(jax.ShapeDtypeStruct((t, _GRP_COLS), BF16),
                   jax.ShapeDtypeStruct((b, d1, seq // d1, _GRP_COLS), BF16),
                   jax.ShapeDtypeStruct((b, d2, seq // d2, _GRP_COLS), BF16),
                   jax.ShapeDtypeStruct((t, _PB_COLS), BF16),
                   jax.ShapeDtypeStruct((4, t // NSA_CMP_STRIDE, NSA_CMP_STRIDE * HEAD_DIM), BF16),
                   jax.ShapeDtypeStruct((t, _C_COLS), BF16),
                   jax.ShapeDtypeStruct((t, _MG_COLS), BF16),
                   jax.ShapeDtypeStruct((t, _BG_COLS), F32)),
        grid=(t // tm,),
        in_specs=[pl.BlockSpec((tm, D_MODEL), row),
                  pl.BlockSpec((1, D_MODEL), lambda i: (0, 0)),
                  pl.BlockSpec((None, D_MODEL, _W_COLS), lambda i: (layer, 0, 0))],
        out_specs=(pl.BlockSpec((tm, _GRP_COLS), row),
                   pl.BlockSpec((1, d1, tm // d1, _GRP_COLS), sub),
                   pl.BlockSpec((1, d2, tm // d2, _GRP_COLS), sub),
                   pl.BlockSpec((tm, _PB_COLS), row),
                   pl.BlockSpec((4, tm // NSA_CMP_STRIDE, NSA_CMP_STRIDE * HEAD_DIM),
                                lambda i: (0, i, 0)),
                   pl.BlockSpec((tm, _C_COLS), row),
                   pl.BlockSpec((tm, _MG_COLS), row),
                   pl.BlockSpec((tm, _BG_COLS), row)),
        scratch_shapes=[pltpu.VMEM((8, tm, LANE), F32)],
        compiler_params=_params(1),
        name="proj_in",
    )(x2, g, w)


def _w_in_segments():
    a = A_HEADS * HEAD_DIM
    bq0 = 3 * a
    bkc0 = bq0 + B_HEADS * HEAD_DIM
    bks0 = bkc0 + 2 * B_KV_HEADS * HEAD_DIM
    bg0 = bks0 + 4 * B_KV_HEADS * HEAD_DIM
    c0 = bg0 + B_HEADS * 3
    mg0 = c0 + _C_COLS
    gw = A_HEADS_PER_GROUP * HEAD_DIM
    segs = [(_OFF_A + (gi * 3 + j) * gw, j * a + gi * gw, gw)
            for gi in range(len(A_GROUPS)) for j in range(3)]
    segs += [(_OFF_PB, bq0, B_HEADS * HEAD_DIM),
             (_OFF_PB + B_HEADS * HEAD_DIM, bks0, bg0 - bks0),
             (_OFF_CMP, bkc0, _CMP_COLS), (_OFF_C, c0, _C_COLS), (_OFF_MG, mg0, _MG_COLS),
             (_OFF_BG, bg0, B_HEADS * 3)]
    return segs


def _prep_w_in_kernel(w_ref, o_ref):
    o_ref[0, :, _OFF_BG:] = jnp.zeros((o_ref.shape[1], _BG_COLS), BF16)
    for dst, src, width in _w_in_segments():
        for c in range(0, width, _PROJ_CHUNK):
            n = min(_PROJ_CHUNK, width - c)
            o_ref[0, :, dst + c:dst + c + n] = w_ref[0, :, src + c:src + c + n].astype(BF16)


def _prep_w_in(w):
    depth, rows, cols = w.shape
    rb = 128
    return pl.pallas_call(
        _prep_w_in_kernel,
        out_shape=jax.ShapeDtypeStruct((depth, rows, _W_COLS), BF16),
        grid=(depth, rows // rb),
        in_specs=[pl.BlockSpec((1, rb, cols), lambda l, r: (l, r, 0))],
        out_specs=pl.BlockSpec((1, rb, _W_COLS), lambda l, r: (l, r, 0)),
        compiler_params=_params(2),
        name="prep_w_in",
    )(w)


_A_UNROLL = 3


def _mixer_a_kernel(pa0_ref, pa1_ref, pa2_ref, bias_ref, o_ref, on_ref, ln_ref, *, seq):
    gw = A_HEADS_PER_GROUP * HEAD_DIM

    def band_block(src, l0, first, gi, rows, bias_head0=None):
        if bias_head0 is None:
            bias_head0 = A_HEADS_PER_GROUP * gi + (A_HEADS + A_HEADS_PER_GROUP if first else 0)
        kr = pl.ds(l0 if first else l0 - A_TILE, 2 * A_TILE)
        q = src[pl.ds(l0, A_TILE), 0:gw] * ATTN_SCALE
        k, v = src[kr, gw:2 * gw], src[kr, 2 * gw:3 * gw]
        lane = lax.broadcasted_iota(jnp.int32, (A_TILE, gw), 1)
        out = lse = None
        for h in range(A_HEADS_PER_GROUP):
            own = (lane >= h * HEAD_DIM) & (lane < (h + 1) * HEAD_DIM)
            bias = bias_ref[bias_head0 + h]
            s = _nt(q * own[0:1, :].astype(BF16), k) + bias
            m = jnp.max(s, axis=-1, keepdims=True)
            e = jnp.exp(s - m)
            l = jnp.sum(e, axis=-1, keepdims=True)
            o_h = jnp.dot(e.astype(BF16), v, preferred_element_type=F32) / l
            lse_h = jnp.broadcast_to(m + jnp.log(l), (A_TILE, gw))
            out = o_h if out is None else jnp.where(own, o_h, out)
            lse = lse_h if lse is None else jnp.where(own, lse_h, lse)
        on_ref[gi, rows, :] = out
        ln_ref[gi, rows, :] = lse

    for gi, ((_, dil), pa_ref) in enumerate(zip(A_GROUPS, (pa0_ref, pa1_ref, pa2_ref))):
        n_blocks = seq // dil // A_TILE
        if dil == 1:
            band_block(pa_ref.at[0], 0, True, gi, pl.ds(0, A_TILE))
            assert (n_blocks - 1) % _A_UNROLL == 0

            def body0(it, c, pa_ref=pa_ref, gi=gi):
                for u in range(_A_UNROLL):
                    l0 = pl.multiple_of((1 + it * _A_UNROLL + u) * A_TILE, A_TILE)
                    band_block(pa_ref.at[0], l0, False, gi, pl.ds(l0, A_TILE))
                return c
            lax.fori_loop(0, (n_blocks - 1) // _A_UNROLL, body0, 0)
        elif n_blocks > 1:
            for r in range(dil):
                band_block(pa_ref.at[0, r], 0, True, gi, pl.ds(r, A_TILE, stride=dil))

            def body1(i, c, pa_ref=pa_ref, gi=gi, dil=dil):
                l0 = pl.multiple_of(i * A_TILE, A_TILE)
                for r in range(dil):
                    band_block(pa_ref.at[0, r], l0, False, gi, pl.ds(l0 * dil + r, A_TILE, stride=dil))
                return c
            lax.fori_loop(1, n_blocks, body1, 0)
        else:
            assert (dil - 1) % _A_UNROLL == 0
            band_block(pa_ref.at[0], 0, True, gi, pl.ds(0, A_TILE, stride=dil))

            def body2(it, c, pa_ref=pa_ref, gi=gi, dil=dil):
                for u in range(_A_UNROLL):
                    r = 1 + it * _A_UNROLL + u
                    band_block(pa_ref.at[0], pl.multiple_of(r * A_TILE, A_TILE), False, gi,
                               pl.ds(r, A_TILE, stride=dil), bias_head0=A_HEADS)
                return c
            lax.fori_loop(0, (dil - 1) // _A_UNROLL, body2, 0)

    chunk = T_TILE
    for c0 in range(0, seq, chunk):
        rows = slice(c0, c0 + chunk)
        lse = [ln_ref[g, rows, :] for g in range(len(A_GROUPS))]
        top = functools.reduce(jnp.maximum, lse)
        ex = [jnp.exp(v - top) for v in lse]
        den = functools.reduce(lambda a, b: a + b, ex)
        for g in range(len(A_GROUPS)):
            o_ref[0, rows, g * gw:(g + 1) * gw] = (ex[g] / den * on_ref[g, rows, :]).astype(BF16)


def _mixer_a(pa0, pa1, pa2, bias):
    b, seq, _ = pa0.shape
    return pl.pallas_call(
        functools.partial(_mixer_a_kernel, seq=seq),
        out_shape=jax.ShapeDtypeStruct((b, seq, _GRP_COLS), BF16),
        grid=(b,),
        in_specs=[pl.BlockSpec((1, seq, _GRP_COLS), lambda i: (i, 0, 0)),
                  pl.BlockSpec((1,) + pa1.shape[1:], lambda i: (i, 0, 0, 0)),
                  pl.BlockSpec((1,) + pa2.shape[1:], lambda i: (i, 0, 0)),
                  pl.BlockSpec(bias.shape, lambda i: (0, 0, 0))],
        out_specs=pl.BlockSpec((1, seq, _GRP_COLS), lambda i: (i, 0, 0)),
        scratch_shapes=[pltpu.VMEM((len(A_GROUPS), seq, LANE), F32),
                        pltpu.VMEM((len(A_GROUPS), seq, LANE), F32)],
        compiler_params=_params(1),
        name="mixer_a",
    )(pa0, pa1, pa2, bias)


def _compress_kernel(r_ref, pek_ref, pev_ref, w1k_ref, w1v_ref, w2k_ref, w2v_ref, o_ref):
    half = NSA_CMP_STRIDE * HEAD_DIM
    for kv, (pe_ref, w1_ref, w2_ref) in enumerate(((pek_ref, w1k_ref, w2k_ref),
                                                   (pev_ref, w1v_ref, w2v_ref))):
        for h in range(B_KV_HEADS):
            r = r_ref[kv * B_KV_HEADS + h, 0].astype(F32)
            lo = jnp.dot((r + pe_ref[:, :half]).astype(BF16), w1_ref[:half, :],
                         preferred_element_type=F32)
            hi = jnp.dot((r + pe_ref[:, half:]).astype(BF16), w1_ref[half:, :],
                         preferred_element_type=F32)
            hid = lo + pltpu.roll(hi, hi.shape[0] - 1, axis=0)
            act = jax.nn.gelu(hid).astype(BF16)
            o_ref[0, kv * B_KV_HEADS + h] = jnp.dot(act, w2_ref[...],
                                                    preferred_element_type=F32).astype(BF16)


def _compress(r, pe_k, pe_v, w1_k, w1_v, w2_k, w2_v):
    _, b, m, c = r.shape
    const2 = lambda i: (0, 0)
    return pl.pallas_call(
        _compress_kernel,
        out_shape=jax.ShapeDtypeStruct((b, 4, m, HEAD_DIM), BF16),
        grid=(b,),
        in_specs=[pl.BlockSpec((4, 1, m, c), lambda i: (0, i, 0, 0)),
                  pl.BlockSpec(pe_k.shape, const2), pl.BlockSpec(pe_v.shape, const2),
                  pl.BlockSpec(w1_k.shape, const2), pl.BlockSpec(w1_v.shape, const2),
                  pl.BlockSpec(w2_k.shape, const2), pl.BlockSpec(w2_v.shape, const2)],
        out_specs=pl.BlockSpec((1, 4, m, HEAD_DIM), lambda i: (i, 0, 0, 0)),
        compiler_params=_params(1),
        name="nsa_compress",
    )(r, pe_k, pe_v, w1_k, w1_v, w2_k, w2_v)


MASK_BIG = 2.0 ** 100
SEL_PAD = 32


def _nt(a, b, **kw):
    return lax.dot_general(a, b, (((1,), (1,)), ((), ())), preferred_element_type=F32, **kw)


def _top_mask_t(score, n_rows, n_top):
    sub = lax.broadcasted_iota(jnp.int32, score.shape, 0)
    rank = jnp.zeros(score.shape, F32)
    for j in range(n_rows):
        row = score[j:j + 1, :]
        ge = jnp.where(row >= score, 1.0, 0.0)
        gt = jnp.where(row > score, 1.0, 0.0)
        rank = rank + jnp.where(sub > j, ge, gt)
    return rank < n_top


def _top_keep(score, n_rows, n_top, block):
    free = n_top * block
    ranked = jnp.where(_top_mask_t(score[:, free:], n_rows, n_top), 1.0, 0.0)
    return jnp.concatenate([jnp.ones((score.shape[0], free), F32), ranked], axis=1)


BAND_SLOTS = 4


def _loop_pairs(lo, hi, body):
    n = jnp.maximum(hi - lo, 0)

    def pair(p, c):
        body(lo + 2 * p)
        body(lo + 2 * p + 1)
        return c
    lax.fori_loop(0, n // 2, pair, 0)
    pl.when(n % 2 == 1)(lambda: body(hi - 1))


def _two_pass_attention(i, streams, band_streams=()):
    groups = ((streams, lambda j: j), (band_streams, lambda j: j % BAND_SLOTS))

    def key_rows(j):
        return pl.ds(pl.multiple_of(j * T_TILE, T_TILE), T_TILE)

    def logits(group, j, d):
        some, slot = group
        for k_tile, q_t, _, bias_tile, s_ref, rmp_ref, _ in some:
            s = jnp.dot(k_tile(key_rows(j)), q_t(), preferred_element_type=F32)
            if d is not None:
                s = s + bias_tile(d)
            s_ref[slot(j)] = s
            fold = jnp.max(s.reshape(s.shape[0] // 8, 8, s.shape[1]), axis=0)
            rmp_ref[...] = fold if d == 0 else jnp.maximum(rmp_ref[...], fold)

    def near(n_full, n_band):
        for d in range(n_full):
            logits(groups[0], i - d, d)
        for d in range(n_band):
            logits(groups[1], i - d, d)

    if band_streams:
        pl.when(i == 0)(lambda: near(1, 1))
        pl.when(i == 1)(lambda: near(2, 2))
        pl.when(i >= 2)(lambda: near(2, 3))
    else:
        logits(groups[0], i, 0)
        pl.when(i >= 1)(lambda: logits(groups[0], i - 1, 1))

    _loop_pairs(0, i - 1, lambda j: logits(groups[0], j, None))

    tops = {}
    for some, _ in groups:
        for stream in some:
            rmp_ref, acc_ref = stream[-2:]
            tops[id(rmp_ref)] = jnp.max(rmp_ref[...], axis=0, keepdims=True)
            acc_ref[...] = jnp.zeros(acc_ref.shape, F32)

    def pv(active):
        def body(j):
            for some, slot in active:
                for _, _, vt_ref, _, s_ref, rmp_ref, acc_ref in some:
                    e = jnp.exp(s_ref[slot(j)] - tops[id(rmp_ref)])
                    acc_ref[...] += jnp.dot(vt_ref[j], e.astype(BF16), preferred_element_type=F32)
        return body

    if band_streams:
        band_lo = jnp.maximum(i - 2, 0)
        _loop_pairs(0, band_lo, pv(groups[:1]))
        pv_all = pv(groups)

        def last_tiles(n):
            for d in reversed(range(n)):
                pv_all(i - d)

        pl.when(i == 0)(lambda: last_tiles(1))
        pl.when(i == 1)(lambda: last_tiles(2))
        pl.when(i >= 2)(lambda: last_tiles(3))
    else:
        _loop_pairs(0, i + 1, pv(groups[:1]))

    outs = []
    for some, _ in groups:
        for *_, acc_ref in some:
            acc = acc_ref[...]
            outs.append(acc[:HEAD_DIM, :] / acc[HEAD_DIM:HEAD_DIM + 1, :])
    return outs


def _eye(n, dtype=BF16):
    return (lax.broadcasted_iota(jnp.int32, (n, n), 0)
            == lax.broadcasted_iota(jnp.int32, (n, n), 1)).astype(dtype)


def _transposed_values(v_ref_rows, n_tiles):
    rows = lax.broadcasted_iota(jnp.int32, (LANE, HEAD_DIM), 0)
    cols = lax.broadcasted_iota(jnp.int32, (LANE, HEAD_DIM), 1)
    place = (rows == cols).astype(BF16)
    ones_row = (lax.broadcasted_iota(jnp.int32, (LANE, T_TILE), 0) == HEAD_DIM).astype(F32)
    for j in range(n_tiles):
        v = v_ref_rows(slice(j * T_TILE, (j + 1) * T_TILE))
        yield j, (_nt(place, v) + ones_row).astype(BF16)


def _with_block_onehot(k, block):
    n = k.shape[0]
    rows = lax.broadcasted_iota(jnp.int32, (HEAD_DIM, LANE), 0)
    cols = lax.broadcasted_iota(jnp.int32, (HEAD_DIM, LANE), 1)
    place = (rows == cols).astype(BF16)
    lane = lax.broadcasted_iota(jnp.int32, (n, LANE), 1)
    hot = (lane - HEAD_DIM == lax.broadcasted_iota(jnp.int32, (n, LANE), 0) // block).astype(F32)
    return (jnp.dot(k, place, preferred_element_type=F32) + hot).astype(BF16)


def _augment_q_t(q_t, allowed_t):
    pen = (allowed_t - 1.0) * MASK_BIG
    return jnp.concatenate([q_t, pen, jnp.zeros(pen.shape, F32)], axis=0).astype(BF16)


_N_SLC = 32
_G_ROWS = B_GROUP * T_TILE


def _nsa_kernel(pb_ref, cmp_ref, bg_ref, bias_ref, o_ref,
                kaug_ref, vst_ref, vwt_ref, eye_ref, sig_ref, q3t_ref, qaugt_ref, s_ref, sw_ref,
                rmp_ref, acc_ref, ocmp_ref, *, seq):
    n_tiles = seq // T_TILE
    n_cmp_rows = seq // NSA_CMP_STRIDE
    q_cols = B_HEADS * HEAD_DIM
    ks0, vs0, kw0, vw0 = (q_cols + i * B_KV_HEADS * HEAD_DIM for i in range(4))

    def head_cols(c0, h):
        return slice(c0 + h * HEAD_DIM, c0 + (h + 1) * HEAD_DIM)

    j_id = lax.broadcasted_iota(jnp.int32, (_N_SLC, n_cmp_rows), 0)
    c_id = lax.broadcasted_iota(jnp.int32, (_N_SLC, n_cmp_rows), 1)
    overlap_t = ((c_id * NSA_CMP_STRIDE < (j_id + 1) * NSA_SLC_BLOCK)
                 & (c_id * NSA_CMP_STRIDE + NSA_CMP_BLOCK > j_id * NSA_SLC_BLOCK)).astype(F32)
    eye_ref[...] = _eye(T_TILE)
    for h in range(B_KV_HEADS):
        kaug_ref[h] = _with_block_onehot(pb_ref[0, :, head_cols(ks0, h)], NSA_SLC_BLOCK)
        for j, vt in _transposed_values(lambda r, h=h: pb_ref[0, r, head_cols(vs0, h)], n_tiles):
            vst_ref[h, j] = vt
        for j, vt in _transposed_values(lambda r, h=h: pb_ref[0, r, head_cols(vw0, h)], n_tiles):
            vwt_ref[h, j] = vt

    def tiles(a):
        return [a[:, j * T_TILE:(j + 1) * T_TILE] for j in range(n_tiles)]

    pick = (lax.broadcasted_iota(jnp.int32, (SEL_PAD, _BG_COLS), 0)
            == lax.broadcasted_iota(jnp.int32, (SEL_PAD, _BG_COLS), 1)).astype(F32)
    sig_all = jax.nn.sigmoid(_nt(pick, bg_ref[0], precision=lax.Precision.HIGHEST))
    for j, t in enumerate(tiles(sig_all)):
        sig_ref[j] = t

    eye_h = _eye(HEAD_DIM)
    t_lane = lax.broadcasted_iota(jnp.int32, (n_cmp_rows, seq), 1)
    c_end = (lax.broadcasted_iota(jnp.int32, (n_cmp_rows, seq), 0) * NSA_CMP_STRIDE
             + NSA_CMP_BLOCK - 1)
    ok = (c_end <= t_lane) & (c_end < seq)
    ok_f = ok.astype(F32)
    j_sub = lax.broadcasted_iota(jnp.int32, (_N_SLC, seq), 0)
    jt = lax.broadcasted_iota(jnp.int32, (_N_SLC, seq), 1) // NSA_SLC_BLOCK
    forced = ((j_sub == 0) | (j_sub == jt) | (j_sub == jt - 1)).astype(F32)
    valid = j_sub <= jt

    for h in range(B_KV_HEADS):
        kc = cmp_ref[0, h]
        vct = _nt(eye_h, cmp_ref[0, B_KV_HEADS + h]).astype(BF16)
        q_ts, p_sum = [], None
        for g in range(B_GROUP):
            q_t = _nt(eye_h, pb_ref[0, :, head_cols(0, h * B_GROUP + g)]) * ATTN_SCALE
            q_ts.append(q_t)
            lc = jnp.where(ok, jnp.dot(kc, q_t.astype(BF16), preferred_element_type=F32), NEG_INF)
            ec = jnp.exp(lc - jnp.max(lc, axis=0, keepdims=True))
            pc = ec / jnp.sum(ec, axis=0, keepdims=True) * ok_f
            o_cmp = jnp.dot(vct, pc.astype(BF16), preferred_element_type=F32)
            for j, t in enumerate(tiles(o_cmp)):
                ocmp_ref[h, j, :, g * T_TILE:(g + 1) * T_TILE] = t
            p_sum = pc if p_sum is None else p_sum + pc
        imp_t = jnp.dot(overlap_t, p_sum, precision=lax.Precision.HIGHEST,
                        preferred_element_type=F32)
        score = jnp.where(valid, imp_t + FORCE_BONUS * forced, NEG_INF)
        sel_t = jnp.where(valid, _top_keep(score, _N_SLC, NSA_SLC_TOPK, NSA_SLC_BLOCK), 0.0)
        pen_tiles = tiles((sel_t - 1.0) * MASK_BIG)
        q_tiles = [tiles(q_t) for q_t in q_ts]
        for j in range(n_tiles):
            q3t = jnp.concatenate([q_tiles[g][j] for g in range(B_GROUP)], axis=1)
            pen = jnp.concatenate([pen_tiles[j]] * B_GROUP, axis=1)
            q3t_ref[h, j] = q3t.astype(BF16)
            qaugt_ref[h, j] = jnp.concatenate([q3t, pen, jnp.zeros(pen.shape, F32)],
                                              axis=0).astype(BF16)

    def tile_body(i, carry):
        rows = pl.ds(pl.multiple_of(i * T_TILE, T_TILE), T_TILE)

        outs = _two_pass_attention(
            i,
            [(lambda kr, h=h: kaug_ref[h, kr, :], lambda h=h: qaugt_ref[h, i], vst_ref.at[h],
              lambda d, h=h: bias_ref[h, d], s_ref.at[h], rmp_ref.at[0, h], acc_ref.at[0, h])
             for h in range(B_KV_HEADS)],
            [(lambda kr, h=h: pb_ref[0, kr, head_cols(kw0, h)], lambda h=h: q3t_ref[h, i],
              vwt_ref.at[h], lambda d, h=h: bias_ref[h, d], sw_ref.at[h], rmp_ref.at[1, h],
              acc_ref.at[1, h]) for h in range(B_KV_HEADS)])
        oslc, owin = outs[:B_KV_HEADS], outs[B_KV_HEADS:]

        sig_t = sig_ref[i]
        heads = []
        for h in range(B_KV_HEADS):
            for g in range(B_GROUP):
                head = h * B_GROUP + g
                gc = slice(g * T_TILE, (g + 1) * T_TILE)
                heads.append(sig_t[3 * head:3 * head + 1, :] * ocmp_ref[h, i, :, gc]
                             + sig_t[3 * head + 1:3 * head + 2, :] * oslc[h][:, gc]
                             + sig_t[3 * head + 2:3 * head + 3, :] * owin[h][:, gc])
        o_t = jnp.concatenate(heads, axis=0).astype(BF16)
        o_ref[0, rows, :] = _nt(eye_ref[...], o_t).astype(o_ref.dtype)
        return carry

    lax.fori_loop(0, n_tiles, tile_body, 0)


def _nsa(pb, cmp, bg, bias):
    b, seq, _ = pb.shape
    n_tiles = seq // T_TILE
    return pl.pallas_call(
        functools.partial(_nsa_kernel, seq=seq),
        out_shape=jax.ShapeDtypeStruct((b, seq, B_HEADS * HEAD_DIM), BF16),
        grid=(b,),
        in_specs=[pl.BlockSpec((1, seq, _PB_COLS), lambda i: (i, 0, 0)),
                  pl.BlockSpec((1,) + cmp.shape[1:], lambda i: (i, 0, 0, 0)),
                  pl.BlockSpec((1, seq, _BG_COLS), lambda i: (i, 0, 0)),
                  pl.BlockSpec(bias.shape, lambda i: (0, 0, 0, 0))],
        out_specs=pl.BlockSpec((1, seq, B_HEADS * HEAD_DIM), lambda i: (i, 0, 0)),
        scratch_shapes=[pltpu.VMEM((B_KV_HEADS, seq, LANE), BF16),
                        pltpu.VMEM((B_KV_HEADS, n_tiles, LANE, T_TILE), BF16),
                        pltpu.VMEM((B_KV_HEADS, n_tiles, LANE, T_TILE), BF16),
                        pltpu.VMEM((T_TILE, T_TILE), BF16),
                        pltpu.VMEM((n_tiles, SEL_PAD, T_TILE), F32),
                        pltpu.VMEM((B_KV_HEADS, n_tiles, HEAD_DIM, _G_ROWS), BF16),
                        pltpu.VMEM((B_KV_HEADS, n_tiles, LANE, _G_ROWS), BF16),
                        pltpu.VMEM((B_KV_HEADS, n_tiles, T_TILE, _G_ROWS), F32),
                        pltpu.VMEM((B_KV_HEADS, BAND_SLOTS, T_TILE, _G_ROWS), F32),
                        pltpu.VMEM((2, B_KV_HEADS, 8, _G_ROWS), F32),
                        pltpu.VMEM((2, B_KV_HEADS, LANE, _G_ROWS), F32),
                        pltpu.VMEM((B_KV_HEADS, n_tiles, HEAD_DIM, _G_ROWS), F32)],
        compiler_params=_params(1),
        name="nsa",
    )(pb, cmp, bg, bias)


def _moba_kernel(pc_ref, bias_ref, o_ref, kaug_ref, vt_ref, eye_ref, qaugt_ref, s_ref,
                 rmp_ref, acc_ref, *, seq):
    n_blk = seq // MOBA_BLOCK
    width = C_HEADS * HEAD_DIM
    eye_ref[...] = _eye(T_TILE)

    def cols(j, h):
        return slice(j * width + h * HEAD_DIM, j * width + (h + 1) * HEAD_DIM)

    n_sub = lax.broadcasted_iota(jnp.int32, (SEL_PAD, seq), 0)
    own = lax.broadcasted_iota(jnp.int32, (SEL_PAD, seq), 1) // MOBA_BLOCK
    past = n_sub < own
    eye_h = _eye(HEAD_DIM)
    for h in range(C_HEADS):
        kaug_ref[h] = _with_block_onehot(pc_ref[0, :, cols(1, h)], MOBA_BLOCK)
        for j, vt in _transposed_values(lambda r, h=h: pc_ref[0, r, cols(2, h)], n_blk):
            vt_ref[h, j] = vt
        kmean = jnp.concatenate(
            [jnp.mean(pc_ref[0, n * MOBA_BLOCK:(n + 1) * MOBA_BLOCK, cols(1, h)].astype(F32), axis=0,
                      keepdims=True) for n in range(n_blk)]
            + [jnp.zeros((SEL_PAD - n_blk, HEAD_DIM), F32)], axis=0)
        q_t = _nt(eye_h, pc_ref[0, :, cols(0, h)])
        gate_t = jnp.dot(kmean, q_t, precision=lax.Precision.HIGHEST,
                         preferred_element_type=F32)
        score = jnp.where(past, gate_t, NEG_INF)
        picked = past & (_top_keep(score, n_blk, MOBA_TOPK, MOBA_BLOCK) > 0.5)
        allowed_t = jnp.where(picked | (n_sub == own), 1.0, 0.0)
        qaug_t = _augment_q_t(q_t * ATTN_SCALE, allowed_t)
        for j in range(n_blk):
            qaugt_ref[h, j] = qaug_t[:, j * T_TILE:(j + 1) * T_TILE]

    def tile_body(i, carry):
        rows = pl.ds(pl.multiple_of(i * MOBA_BLOCK, MOBA_BLOCK), MOBA_BLOCK)
        outs = _two_pass_attention(
            i, [(lambda kr, h=h: kaug_ref[h, kr, :], lambda h=h: qaugt_ref[h, i], vt_ref.at[h],
                 lambda d, h=h: bias_ref[h, d], s_ref.at[h], rmp_ref.at[h], acc_ref.at[h])
                for h in range(C_HEADS)])
        o_t = jnp.concatenate(outs, axis=0).astype(BF16)
        o_ref[0, rows, :] = _nt(eye_ref[...], o_t).astype(o_ref.dtype)
        return carry

    lax.fori_loop(0, n_blk, tile_body, 0)


def _moba(pc, bias):
    b, seq, _ = pc.shape
    width = C_HEADS * HEAD_DIM
    n_tiles = seq // T_TILE
    return pl.pallas_call(
        functools.partial(_moba_kernel, seq=seq),
        out_shape=jax.ShapeDtypeStruct((b, seq, width), BF16),
        grid=(b,),
        in_specs=[pl.BlockSpec((1, seq, _C_COLS), lambda i: (i, 0, 0)),
                  pl.BlockSpec(bias.shape, lambda i: (0, 0, 0, 0))],
        out_specs=pl.BlockSpec((1, seq, width), lambda i: (i, 0, 0)),
        scratch_shapes=[pltpu.VMEM((C_HEADS, seq, LANE), BF16),
                        pltpu.VMEM((C_HEADS, n_tiles, LANE, T_TILE), BF16),
                        pltpu.VMEM((T_TILE, T_TILE), BF16),
                        pltpu.VMEM((C_HEADS, n_tiles, LANE, T_TILE), BF16),
                        pltpu.VMEM((C_HEADS, n_tiles, T_TILE, T_TILE), F32),
                        pltpu.VMEM((C_HEADS, 8, T_TILE), F32),
                        pltpu.VMEM((C_HEADS, LANE, T_TILE), F32)],
        compiler_params=_params(1),
        name="moba",
    )(pc, bias)


def _merge_kernel(oa_ref, ob_ref, oc_ref, mg_ref, x_ref, wb_ref, wo_ref, y_ref):
    r0 = A_HEADS * HEAD_DIM
    r1 = r0 + B_HEADS * HEAD_DIM
    ya = jnp.dot(oa_ref[...], wb_ref[0:r0, :], preferred_element_type=F32)
    yb = jnp.dot(ob_ref[...], wb_ref[r0:r1, :], preferred_element_type=F32)
    yc = jnp.dot(oc_ref[...], wb_ref[r1:, :], preferred_element_type=F32)
    merged = (jax.nn.sigmoid(mg_ref[:, 0:D_MODEL].astype(F32)) * ya
              + jax.nn.sigmoid(mg_ref[:, D_MODEL:2 * D_MODEL].astype(F32)) * yb
              + jax.nn.sigmoid(mg_ref[:, 2 * D_MODEL:].astype(F32)) * yc)
    y_ref[...] = x_ref[...] + jnp.dot(merged.astype(BF16), wo_ref[...], preferred_element_type=F32)


def _merge(oa, ob, oc, mg, x2, wb, wo):
    t = x2.shape[0]
    tm = ROW_TILE
    row = lambda i: (i, 0)
    return pl.pallas_call(
        _merge_kernel,
        out_shape=jax.ShapeDtypeStruct((t, D_MODEL), F32),
        grid=(t // tm,),
        in_specs=[pl.BlockSpec((tm, oa.shape[1]), row),
                  pl.BlockSpec((tm, ob.shape[1]), row), pl.BlockSpec((tm, oc.shape[1]), row),
                  pl.BlockSpec((tm, _MG_COLS), row), pl.BlockSpec((tm, D_MODEL), row),
                  pl.BlockSpec(wb.shape, lambda i: (0, 0)), pl.BlockSpec(wo.shape, lambda i: (0, 0))],
        out_specs=pl.BlockSpec((tm, D_MODEL), row),
        compiler_params=_params(1),
        name="merge",
    )(oa, ob, oc, mg, x2, wb, wo)


MXU_WIDTH = 256
_FF_CHUNK_TILES = 6
_FF_BOUNDS = tuple(range(0, D_FF, _FF_CHUNK_TILES * MXU_WIDTH)) + (D_FF,)
_FF_CHUNKS = tuple(zip(_FF_BOUNDS[:-1], _FF_BOUNDS[1:]))


def _ffn_kernel(x_ref, halo_ref, g_ref, wu_ref, cw_ref, cb_ref, wd_ref, gf_ref, y_ref, xn_ref,
                *, tiles_per_seq, final_norm):
    i = pl.program_id(0)
    x = x_ref[...]
    g = g_ref[...]
    keep = (i % tiles_per_seq != 0).astype(F32)
    xn_ref[0:FFN_HALO, :] = (_rmsnorm_rows(halo_ref[...], g) * keep).astype(BF16)
    xn_ref[FFN_HALO:, :] = _rmsnorm_rows(x, g).astype(BF16)
    xn = xn_ref[...]
    rows = xn.shape[0]

    def conv(c0, c1):
        hcol = jnp.dot(xn, wu_ref[:, c0:c1], preferred_element_type=F32)
        out = (cw_ref[2:3, c0:c1] * hcol
               + cw_ref[1:2, c0:c1] * pltpu.roll(hcol, 1, axis=0)
               + cw_ref[0:1, c0:c1] * pltpu.roll(hcol, 2, axis=0)
               + cb_ref[:, c0:c1])
        return out[FFN_HALO:rows]

    acc = x
    for c0, c1 in _FF_CHUNKS:
        a = conv(c0, c1)
        u = conv(D_FF + c0, D_FF + c1)
        act = (a * jax.nn.sigmoid(a) * u).astype(BF16)
        acc = acc + jnp.dot(act, wd_ref[c0:c1, :], preferred_element_type=F32)
    if final_norm:
        acc = _rmsnorm_rows(acc, gf_ref[...])
    y_ref[...] = acc


def _ffn(x2, g, wu, cw, cb, wd, gf, *, seq, final_norm):
    t = x2.shape[0]
    tm = ROW_TILE
    const = lambda i: (0, 0)
    halo_blocks = tm // FFN_HALO
    return pl.pallas_call(
        functools.partial(_ffn_kernel, tiles_per_seq=seq // tm, final_norm=final_norm),
        out_shape=jax.ShapeDtypeStruct((t, D_MODEL), F32),
        grid=(t // tm,),
        in_specs=[pl.BlockSpec((tm, D_MODEL), lambda i: (i, 0)),
                  pl.BlockSpec((FFN_HALO, D_MODEL), lambda i: (jnp.maximum(i * halo_blocks - 1, 0), 0)),
                  pl.BlockSpec((1, D_MODEL), const),
                  pl.BlockSpec(wu.shape, const), pl.BlockSpec(cw.shape, const),
                  pl.BlockSpec(cb.shape, const), pl.BlockSpec(wd.shape, const),
                  pl.BlockSpec((1, D_MODEL), const)],
        out_specs=pl.BlockSpec((tm, D_MODEL), lambda i: (i, 0)),
        scratch_shapes=[pltpu.VMEM((tm + FFN_HALO, D_MODEL), BF16)],
        compiler_params=_params(1),
        name="conv_ffn",
    )(x2, x2, g, wu, cw, cb, wd, gf)


def kernel(x, rel_bias, norm_mix, w_in, cmp_pe_k, cmp_w1_k, cmp_w2_k, cmp_pe_v, cmp_w1_v, cmp_w2_v,
           w_branch, w_out, norm_ffn, w_up, conv_w, conv_b, w_down, norm_final):
    b, s, d = x.shape
    depth = w_in.shape[0]
    t = b * s
    assert d == D_MODEL and s % T_TILE == 0 and s // NSA_SLC_BLOCK == _N_SLC and t % ROW_TILE == 0
    assert s % ROW_TILE == 0 and all(s % (dil * A_TILE) == 0 for _, dil in A_GROUPS)
    assert all(win // dil == A_TILE for win, dil in A_GROUPS) and A_GROUPS[0][1] == 1
    assert all(ROW_TILE % dil == 0 for _, dil in A_GROUPS)

    hg = A_HEADS_PER_GROUP
    bias_a = jnp.concatenate([
        _bias_tiles(rel_bias,
                    np.concatenate([_bucket_tile(A_TILE, 1, dil, win // dil),
                                    _bucket_tile(A_TILE, 0, dil, win // dil)], axis=1),
                    gi * hg, hg)
        for gi, (win, dil) in enumerate(A_GROUPS)]
        + [_bias_tiles(rel_bias,
                       np.concatenate([np.full((A_TILE, A_TILE), -1, np.int32),
                                       _bucket_tile(A_TILE, 0, A_GROUPS[-1][1], A_TILE)], axis=1),
                       (len(A_GROUPS) - 1) * hg, hg)]
        + [_bias_tiles(rel_bias,
                       np.concatenate([_bucket_tile(A_TILE, 0, dil, win // dil),
                                       np.full((A_TILE, A_TILE), -1, np.int32)], axis=1),
                       gi * hg, hg)
           for gi, (win, dil) in enumerate(A_GROUPS)], axis=0)
    assert (_bucket_tile(T_TILE, 2, 1, 3 * T_TILE) == REL_BUCKETS - 1).all()
    bias_b = _bias_tiles_t(rel_bias, [_bucket_tile(T_TILE, d, 1, NSA_WINDOW - 1) for d in (0, 1, 2)],
                           A_HEADS, B_HEADS)
    bias_b = bias_b.reshape(B_KV_HEADS, B_GROUP, 3, T_TILE, T_TILE).transpose(0, 2, 3, 1, 4).reshape(
        B_KV_HEADS, 3, T_TILE, _G_ROWS)
    bias_c = _bias_tiles_t(rel_bias, [_bucket_tile(T_TILE, d, 1, s) for d in (0, 1)],
                           A_HEADS + B_HEADS, C_HEADS)

    w_in_k = _prep_w_in(w_in)
    x2 = x.reshape(t, d)
    for i in range(depth):
        pa0, pa1, pa2, pb, pcmp, pc, mg, bg = _proj_in(x2, norm_mix[i][None, :], w_in_k, layer=i,
                                                       seq=s)

        oa = _mixer_a(pa0.reshape(b, s, _GRP_COLS), pa1, pa2.reshape(b, s, _GRP_COLS), bias_a)

        cmp = _compress(pcmp.reshape(4, b, s // NSA_CMP_STRIDE, NSA_CMP_STRIDE * HEAD_DIM),
                        cmp_pe_k[i].reshape(1, -1), cmp_pe_v[i].reshape(1, -1),
                        cmp_w1_k[i].astype(BF16), cmp_w1_v[i].astype(BF16),
                        cmp_w2_k[i].astype(BF16), cmp_w2_v[i].astype(BF16))
        ob = _nsa(pb.reshape(b, s, _PB_COLS), cmp, bg.reshape(b, s, _BG_COLS), bias_b)

        oc = _moba(pc.reshape(b, s, _C_COLS), bias_c)

        x2 = _merge(oa.reshape(t, -1), ob.reshape(t, -1), oc.reshape(t, -1), mg, x2,
                    w_branch[i].astype(BF16), w_out[i].astype(BF16))
        x2 = _ffn(x2, norm_ffn[i][None, :], w_up[i].astype(BF16), conv_w[i], conv_b[i][None, :],
                  w_down[i].astype(BF16), norm_final[None, :], seq=s, final_norm=(i == depth - 1))
    return x2.reshape(b, s, d)
```

```python
import functools
import math

import jax
import jax.numpy as jnp
import numpy as np
from jax import lax
from jax.experimental import pallas as pl
from jax.experimental.pallas import tpu as pltpu

F32 = jnp.float32
BF16 = jnp.bfloat16

D_MODEL = 1024
HEAD_DIM = 64
A_GROUPS = ((128, 1), (512, 4), (2048, 16))
A_HEADS_PER_GROUP = 2
A_HEADS = 6
B_HEADS = 6
B_KV_HEADS = 2
B_GROUP = 3
C_HEADS = 4
NSA_CMP_BLOCK = 32
NSA_CMP_STRIDE = 16
NSA_SLC_BLOCK = 64
NSA_SLC_TOPK = 16
NSA_WINDOW = 512
MOBA_BLOCK = 256
MOBA_TOPK = 3
REL_BUCKETS = 32
REL_MAX_DIST = 128
D_FF = 2816
RMS_EPS = 1e-6
NEG_INF = -1e30
FORCE_BONUS = 1e4
ATTN_SCALE = HEAD_DIM ** -0.5

LANE = 128
A_TILE = 128
T_TILE = 256
ROW_TILE = 512
FFN_HALO = 8
VMEM_LIMIT = 56 * 1024 * 1024

_GRP_COLS = 3 * A_HEADS_PER_GROUP * HEAD_DIM
_A_COLS = len(A_GROUPS) * _GRP_COLS
_PB_COLS = B_HEADS * HEAD_DIM + 4 * B_KV_HEADS * HEAD_DIM
_CMP_COLS = 2 * B_KV_HEADS * HEAD_DIM
_C_COLS = 3 * C_HEADS * HEAD_DIM
_MG_COLS = 3 * D_MODEL
_BG_COLS = LANE
_OFF_A = 0
_OFF_PB = _OFF_A + _A_COLS
_OFF_CMP = _OFF_PB + _PB_COLS
_OFF_C = _OFF_CMP + _CMP_COLS
_OFF_MG = _OFF_C + _C_COLS
_OFF_BG = _OFF_MG + _MG_COLS
_W_COLS = _OFF_BG + _BG_COLS
_PROJ_CHUNK = 4 * LANE


def _params(n_grid):
    return pltpu.CompilerParams(dimension_semantics=("arbitrary",) * n_grid,
                                vmem_limit_bytes=VMEM_LIMIT)


def _rel_bucket_np(dist):
    n = np.maximum(dist, 0)
    exact = REL_BUCKETS // 2
    nf = np.maximum(n, 1).astype(np.float32)
    large = exact + (np.log(nf / np.float32(exact)) / np.float32(math.log(REL_MAX_DIST / exact))
                     * np.float32(REL_BUCKETS - exact)).astype(np.int32)
    return np.where(n < exact, n, np.minimum(large, REL_BUCKETS - 1)).astype(np.int32)


def _bucket_tile(tile, block_offset, dil, max_rel):
    rel = block_offset * tile + np.arange(tile)[:, None] - np.arange(tile)[None, :]
    ok = (rel >= 0) & (rel <= max_rel)
    return np.where(ok, _rel_bucket_np(rel * dil), -1).astype(np.int32)


def _bias_tiles_kernel(tbl_ref, idx_ref, o_ref, *, head0, relative):
    h = pl.program_id(0) + head0
    idx = idx_ref[...]
    acc = jnp.full(idx.shape, NEG_INF, F32)
    for b in range(REL_BUCKETS):
        acc = jnp.where(idx == b, tbl_ref[b, h], acc)
    if relative:
        acc = acc - tbl_ref[REL_BUCKETS - 1, h]
    o_ref[0] = acc


def _bias_tiles(rel_bias, idx, head0, n_heads, relative=False):
    r, c = idx.shape
    return pl.pallas_call(
        functools.partial(_bias_tiles_kernel, head0=head0, relative=relative),
        out_shape=jax.ShapeDtypeStruct((n_heads, r, c), F32),
        grid=(n_heads,),
        in_specs=[pl.BlockSpec(memory_space=pltpu.SMEM),
                  pl.BlockSpec((r, c), lambda h: (0, 0))],
        out_specs=pl.BlockSpec((1, r, c), lambda h: (h, 0, 0)),
        compiler_params=_params(1),
        name="bias_tiles",
    )(rel_bias, jnp.asarray(idx))


def _bias_tiles_t(rel_bias, idx_tiles, head0, n_heads):
    r, c = idx_tiles[0].shape
    flat = _bias_tiles(rel_bias, np.concatenate([t.T for t in idx_tiles], axis=1), head0, n_heads,
                       relative=True)
    return flat.reshape(n_heads, c, len(idx_tiles), r).transpose(0, 2, 1, 3)


def _rmsnorm_rows(x, g):
    return x * lax.rsqrt(jnp.mean(x * x, axis=-1, keepdims=True) + RMS_EPS) * g


def _proj_in_kernel(x_ref, g_ref, w_ref, pa0_ref, pa1_ref, pa2_ref, pb_ref, pcmp_ref, pc_ref,
                    mg_ref, bg_ref, regroup_ref):
    xn = _rmsnorm_rows(x_ref[...], g_ref[...]).astype(BF16)
    tm = xn.shape[0]

    def direct(out_ref):
        def sink(lo, hi, val):
            out_ref[:, lo:hi] = val.astype(out_ref.dtype)
        return sink

    def staged(slot0):
        def sink(lo, hi, val):
            for c in range(lo, hi, LANE):
                regroup_ref[slot0 + c // LANE] = val[:, c - lo:c - lo + LANE]
        return sink

    g1_slot, g2_slot, cmp_slot = 0, 3, 6
    sections = ((_OFF_A, _GRP_COLS, direct(pa0_ref)),
                (_OFF_A + _GRP_COLS, _GRP_COLS, staged(g1_slot)),
                (_OFF_A + 2 * _GRP_COLS, _GRP_COLS, staged(g2_slot)),
                (_OFF_PB, _PB_COLS, direct(pb_ref)), (_OFF_CMP, _CMP_COLS, staged(cmp_slot)),
                (_OFF_C, _C_COLS, direct(pc_ref)), (_OFF_MG, _MG_COLS, direct(mg_ref)),
                (_OFF_BG, _BG_COLS, direct(bg_ref)))
    for c0 in range(0, _W_COLS, _PROJ_CHUNK):
        c1 = min(c0 + _PROJ_CHUNK, _W_COLS)
        res = jnp.dot(xn, w_ref[:, c0:c1], preferred_element_type=F32)
        for off, width, sink in sections:
            lo, hi = max(c0, off), min(c1, off + width)
            if lo < hi:
                sink(lo - off, hi - off, res[:, lo - c0:hi - c0])

    for gi, slot0, out_ref in ((1, g1_slot, pa1_ref), (2, g2_slot, pa2_ref)):
        dil = A_GROUPS[gi][1]
        for r in range(dil):
            for j in range(3):
                out_ref[0, r, :, j * LANE:(j + 1) * LANE] = (
                    regroup_ref[slot0 + j, pl.ds(r, tm // dil, stride=dil), :].astype(BF16))
    for l in range(NSA_CMP_STRIDE):
        for kv in range(2):
            a = regroup_ref[cmp_slot + kv, pl.ds(l, tm // NSA_CMP_STRIDE, stride=NSA_CMP_STRIDE),
                            :].astype(BF16)
            for h in range(B_KV_HEADS):
                pcmp_ref[kv * B_KV_HEADS + h, :, l * HEAD_DIM:(l + 1) * HEAD_DIM] = (
                    a[:, h * HEAD_DIM:(h + 1) * HEAD_DIM])


def _proj_in(x2, g, w, *, layer, seq):
    t = x2.shape[0]
    tm = ROW_TILE
    b = t // seq
    tps = seq // tm
    row = lambda i: (i, 0)
    d1, d2 = A_GROUPS[1][1], A_GROUPS[2][1]
    sub = lambda i: (i // tps, 0, i % tps, 0)
    return pl.pallas_call(
        _proj_in_kernel,
        out_shape=(jax.ShapeDtypeStruct((t, _GRP_COLS), BF16),
                   jax.ShapeDtypeStruct((b, d1, seq // d1, _GRP_COLS), BF16),
                   jax.ShapeDtypeStruct((b, d2, seq // d2, _GRP_COLS), BF16),
                   jax.ShapeDtypeStruct((t, _PB_COLS), BF16),
                   jax.ShapeDtypeStruct((4, t // NSA_CMP_STRIDE, NSA_CMP_STRIDE * HEAD_DIM), BF16),
                   jax.ShapeDtypeStruct((t, _C_COLS), BF16),
                   jax.ShapeDtypeStruct((t, _MG_COLS), BF16),
                   jax.ShapeDtypeStruct((t, _BG_COLS), F32)),
        grid=(t // tm,),
        in_specs=[pl.BlockSpec((tm, D_MODEL), row),
                  pl.BlockSpec((1, D_MODEL), lambda i: (0, 0)),
                  pl.BlockSpec((None, D_MODEL, _W_COLS), lambda i: (layer, 0, 0))],
        out_specs=(pl.BlockSpec((tm, _GRP_COLS), row),
                   pl.BlockSpec((1, d1, tm // d1, _GRP_COLS), sub),
                   pl.BlockSpec((1, d2, tm // d2, _GRP_COLS), sub),
                   pl.BlockSpec((tm, _PB_COLS), row),
                   pl.BlockSpec((4, tm // NSA_CMP_STRIDE, NSA_CMP_STRIDE * HEAD_DIM),
                                lambda i: (0, i, 0)),
                   pl.BlockSpec((tm, _C_COLS), row),
                   pl.BlockSpec((tm, _MG_COLS), row),
                   pl.BlockSpec((tm, _BG_COLS), row)),
        scratch_shapes=[pltpu.VMEM((8, tm, LANE), F32)],
        compiler_params=_params(1),
        name="proj_in",
    )(x2, g, w)


def _w_in_segments():
    a = A_HEADS * HEAD_DIM
    bq0 = 3 * a
    bkc0 = bq0 + B_HEADS * HEAD_DIM
    bks0 = bkc0 + 2 * B_KV_HEADS * HEAD_DIM
    bg0 = bks0 + 4 * B_KV_HEADS * HEAD_DIM
    c0 = bg0 + B_HEADS * 3
    mg0 = c0 + _C_COLS
    gw = A_HEADS_PER_GROUP * HEAD_DIM
    segs = [(_OFF_A + (gi * 3 + j) * gw, j * a + gi * gw, gw)
            for gi in range(len(A_GROUPS)) for j in range(3)]
    segs += [(_OFF_PB, bq0, B_HEADS * HEAD_DIM),
             (_OFF_PB + B_HEADS * HEAD_DIM, bks0, bg0 - bks0),
             (_OFF_CMP, bkc0, _CMP_COLS), (_OFF_C, c0, _C_COLS), (_OFF_MG, mg0, _MG_COLS),
             (_OFF_BG, bg0, B_HEADS * 3)]
    return segs


def _prep_w_in_kernel(w_ref, o_ref):
    o_ref[0, :, _OFF_BG:] = jnp.zeros((o_ref.shape[1], _BG_COLS), BF16)
    for dst, src, width in _w_in_segments():
        for c in range(0, width, _PROJ_CHUNK):
            n = min(_PROJ_CHUNK, width - c)
            o_ref[0, :, dst + c:dst + c + n] = w_ref[0, :, src + c:src + c + n].astype(BF16)


def _prep_w_in(w):
    depth, rows, cols = w.shape
    rb = 128
    return pl.pallas_call(
        _prep_w_in_kernel,
        out_shape=jax.ShapeDtypeStruct((depth, rows, _W_COLS), BF16),
        grid=(depth, rows // rb),
        in_specs=[pl.BlockSpec((1, rb, cols), lambda l, r: (l, r, 0))],
        out_specs=pl.BlockSpec((1, rb, _W_COLS), lambda l, r: (l, r, 0)),
        compiler_params=_params(2),
        name="prep_w_in",
    )(w)


_A_UNROLL = 3


def _mixer_a_kernel(pa0_ref, pa1_ref, pa2_ref, bias_ref, o_ref, on_ref, ln_ref, *, seq):
    gw = A_HEADS_PER_GROUP * HEAD_DIM

    def band_block(src, l0, first, gi, rows, bias_head0=None):
        if bias_head0 is None:
            bias_head0 = A_HEADS_PER_GROUP * gi + (A_HEADS + A_HEADS_PER_GROUP if first else 0)
        kr = pl.ds(l0 if first else l0 - A_TILE, 2 * A_TILE)
        q = src[pl.ds(l0, A_TILE), 0:gw] * ATTN_SCALE
        k, v = src[kr, gw:2 * gw], src[kr, 2 * gw:3 * gw]
        lane = lax.broadcasted_iota(jnp.int32, (A_TILE, gw), 1)
        out = lse = None
        for h in range(A_HEADS_PER_GROUP):
            own = (lane >= h * HEAD_DIM) & (lane < (h + 1) * HEAD_DIM)
            bias = bias_ref[bias_head0 + h]
            s = _nt(q * own[0:1, :].astype(BF16), k) + bias
            m = jnp.max(s, axis=-1, keepdims=True)
            e = jnp.exp(s - m)
            l = jnp.sum(e, axis=-1, keepdims=True)
            o_h = jnp.dot(e.astype(BF16), v, preferred_element_type=F32) / l
            lse_h = jnp.broadcast_to(m + jnp.log(l), (A_TILE, gw))
            out = o_h if out is None else jnp.where(own, o_h, out)
            lse = lse_h if lse is None else jnp.where(own, lse_h, lse)
        on_ref[gi, rows, :] = out
        ln_ref[gi, rows, :] = lse

    for gi, ((_, dil), pa_ref) in enumerate(zip(A_GROUPS, (pa0_ref, pa1_ref, pa2_ref))):
        n_blocks = seq // dil // A_TILE
        if dil == 1:
            band_block(pa_ref.at[0], 0, True, gi, pl.ds(0, A_TILE))
            assert (n_blocks - 1) % _A_UNROLL == 0

            def body0(it, c, pa_ref=pa_ref, gi=gi):
                for u in range(_A_UNROLL):
                    l0 = pl.multiple_of((1 + it * _A_UNROLL + u) * A_TILE, A_TILE)
                    band_block(pa_ref.at[0], l0, False, gi, pl.ds(l0, A_TILE))
                return c
            lax.fori_loop(0, (n_blocks - 1) // _A_UNROLL, body0, 0)
        elif n_blocks > 1:
            for r in range(dil):
                band_block(pa_ref.at[0, r], 0, True, gi, pl.ds(r, A_TILE, stride=dil))

            def body1(i, c, pa_ref=pa_ref, gi=gi, dil=dil):
                l0 = pl.multiple_of(i * A_TILE, A_TILE)
                for r in range(dil):
                    band_block(pa_ref.at[0, r], l0, False, gi, pl.ds(l0 * dil + r, A_TILE, stride=dil))
                return c
            lax.fori_loop(1, n_blocks, body1, 0)
        else:
            assert (dil - 1) % _A_UNROLL == 0
            band_block(pa_ref.at[0], 0, True, gi, pl.ds(0, A_TILE, stride=dil))

            def body2(it, c, pa_ref=pa_ref, gi=gi, dil=dil):
                for u in range(_A_UNROLL):
                    r = 1 + it * _A_UNROLL + u
                    band_block(pa_ref.at[0], pl.multiple_of(r * A_TILE, A_TILE), False, gi,
                               pl.ds(r, A_TILE, stride=dil), bias_head0=A_HEADS)
                return c
            lax.fori_loop(0, (dil - 1) // _A_UNROLL, body2, 0)

    chunk = T_TILE
    for c0 in range(0, seq, chunk):
        rows = slice(c0, c0 + chunk)
        lse = [ln_ref[g, rows, :] for g in range(len(A_GROUPS))]
        top = functools.reduce(jnp.maximum, lse)
        ex = [jnp.exp(v - top) for v in lse]
        den = functools.reduce(lambda a, b: a + b, ex)
        for g in range(len(A_GROUPS)):
            o_ref[0, rows, g * gw:(g + 1) * gw] = (ex[g] / den * on_ref[g, rows, :]).astype(BF16)


def _mixer_a(pa0, pa1, pa2, bias):
    b, seq, _ = pa0.shape
    return pl.pallas_call(
        functools.partial(_mixer_a_kernel, seq=seq),
        out_shape=jax.ShapeDtypeStruct((b, seq, _GRP_COLS), BF16),
        grid=(b,),
        in_specs=[pl.BlockSpec((1, seq, _GRP_COLS), lambda i: (i, 0, 0)),
                  pl.BlockSpec((1,) + pa1.shape[1:], lambda i: (i, 0, 0, 0)),
                  pl.BlockSpec((1,) + pa2.shape[1:], lambda i: (i, 0, 0)),
                  pl.BlockSpec(bias.shape, lambda i: (0, 0, 0))],
        out_specs=pl.BlockSpec((1, seq, _GRP_COLS), lambda i: (i, 0, 0)),
        scratch_shapes=[pltpu.VMEM((len(A_GROUPS), seq, LANE), F32),
                        pltpu.VMEM((len(A_GROUPS), seq, LANE), F32)],
        compiler_params=_params(1),
        name="mixer_a",
    )(pa0, pa1, pa2, bias)


def _compress_kernel(r_ref, pek_ref, pev_ref, w1k_ref, w1v_ref, w2k_ref, w2v_ref, o_ref):
    half = NSA_CMP_STRIDE * HEAD_DIM
    for kv, (pe_ref, w1_ref, w2_ref) in enumerate(((pek_ref, w1k_ref, w2k_ref),
                                                   (pev_ref, w1v_ref, w2v_ref))):
        for h in range(B_KV_HEADS):
            r = r_ref[kv * B_KV_HEADS + h, 0].astype(F32)
            lo = jnp.dot((r + pe_ref[:, :half]).astype(BF16), w1_ref[:half, :],
                         preferred_element_type=F32)
            hi = jnp.dot((r + pe_ref[:, half:]).astype(BF16), w1_ref[half:, :],
                         preferred_element_type=F32)
            hid = lo + pltpu.roll(hi, hi.shape[0] - 1, axis=0)
            act = jax.nn.gelu(hid).astype(BF16)
            o_ref[0, kv * B_KV_HEADS + h] = jnp.dot(act, w2_ref[...],
                                                    preferred_element_type=F32).astype(BF16)


def _compress(r, pe_k, pe_v, w1_k, w1_v, w2_k, w2_v):
    _, b, m, c = r.shape
    const2 = lambda i: (0, 0)
    return pl.pallas_call(
        _compress_kernel,
        out_shape=jax.ShapeDtypeStruct((b, 4, m, HEAD_DIM), BF16),
        grid=(b,),
        in_specs=[pl.BlockSpec((4, 1, m, c), lambda i: (0, i, 0, 0)),
                  pl.BlockSpec(pe_k.shape, const2), pl.BlockSpec(pe_v.shape, const2),
                  pl.BlockSpec(w1_k.shape, const2), pl.BlockSpec(w1_v.shape, const2),
                  pl.BlockSpec(w2_k.shape, const2), pl.BlockSpec(w2_v.shape, const2)],
        out_specs=pl.BlockSpec((1, 4, m, HEAD_DIM), lambda i: (i, 0, 0, 0)),
        compiler_params=_params(1),
        name="nsa_compress",
    )(r, pe_k, pe_v, w1_k, w1_v, w2_k, w2_v)


MASK_BIG = 2.0 ** 100
SEL_PAD = 32


def _nt(a, b, **kw):
    return lax.dot_general(a, b, (((1,), (1,)), ((), ())), preferred_element_type=F32, **kw)


def _top_mask_t(score, n_rows, n_top):
    sub = lax.broadcasted_iota(jnp.int32, score.shape, 0)
    rank = jnp.zeros(score.shape, F32)
    for j in range(n_rows):
        row = score[j:j + 1, :]
        ge = jnp.where(row >= score, 1.0, 0.0)
        gt = jnp.where(row > score, 1.0, 0.0)
        rank = rank + jnp.where(sub > j, ge, gt)
    return rank < n_top


def _top_keep(score, n_rows, n_top, block):
    free = n_top * block
    ranked = jnp.where(_top_mask_t(score[:, free:], n_rows, n_top), 1.0, 0.0)
    return jnp.concatenate([jnp.ones((score.shape[0], free), F32), ranked], axis=1)


BAND_SLOTS = 4


def _loop_pairs(lo, hi, body):
    n = jnp.maximum(hi - lo, 0)

    def pair(p, c):
        body(lo + 2 * p)
        body(lo + 2 * p + 1)
        return c
    lax.fori_loop(0, n // 2, pair, 0)
    pl.when(n % 2 == 1)(lambda: body(hi - 1))


def _two_pass_attention(i, streams, band_streams=()):
    groups = ((streams, lambda j: j), (band_streams, lambda j: j % BAND_SLOTS))

    def key_rows(j):
        return pl.ds(pl.multiple_of(j * T_TILE, T_TILE), T_TILE)

    def logits(group, j, d):
        some, slot = group
        for k_tile, q_t, _, bias_tile, s_ref, rmp_ref, _ in some:
            s = jnp.dot(k_tile(key_rows(j)), q_t(), preferred_element_type=F32)
            if d is not None:
                s = s + bias_tile(d)
            s_ref[slot(j)] = s
            fold = jnp.max(s.reshape(s.shape[0] // 8, 8, s.shape[1]), axis=0)
            rmp_ref[...] = fold if d == 0 else jnp.maximum(rmp_ref[...], fold)

    def near(n_full, n_band):
        for d in range(n_full):
            logits(groups[0], i - d, d)
        for d in range(n_band):
            logits(groups[1], i - d, d)

    if band_streams:
        pl.when(i == 0)(lambda: near(1, 1))
        pl.when(i == 1)(lambda: near(2, 2))
        pl.when(i >= 2)(lambda: near(2, 3))
    else:
        logits(groups[0], i, 0)
        pl.when(i >= 1)(lambda: logits(groups[0], i - 1, 1))

    _loop_pairs(0, i - 1, lambda j: logits(groups[0], j, None))

    tops = {}
    for some, _ in groups:
        for stream in some:
            rmp_ref, acc_ref = stream[-2:]
            tops[id(rmp_ref)] = jnp.max(rmp_ref[...], axis=0, keepdims=True)
            acc_ref[...] = jnp.zeros(acc_ref.shape, F32)

    def pv(active):
        def body(j):
            for some, slot in active:
                for _, _, vt_ref, _, s_ref, rmp_ref, acc_ref in some:
                    e = jnp.exp(s_ref[slot(j)] - tops[id(rmp_ref)])
                    acc_ref[...] += jnp.dot(vt_ref[j], e.astype(BF16), preferred_element_type=F32)
        return body

    if band_streams:
        band_lo = jnp.maximum(i - 2, 0)
        _loop_pairs(0, band_lo, pv(groups[:1]))
        pv_all = pv(groups)

        def last_tiles(n):
            for d in reversed(range(n)):
                pv_all(i - d)

        pl.when(i == 0)(lambda: last_tiles(1))
        pl.when(i == 1)(lambda: last_tiles(2))
        pl.when(i >= 2)(lambda: last_tiles(3))
    else:
        _loop_pairs(0, i + 1, pv(groups[:1]))

    outs = []
    for some, _ in groups:
        for *_, acc_ref in some:
            acc = acc_ref[...]
            outs.append(acc[:HEAD_DIM, :] / acc[HEAD_DIM:HEAD_DIM + 1, :])
    return outs


def _eye(n, dtype=BF16):
    return (lax.broadcasted_iota(jnp.int32, (n, n), 0)
            == lax.broadcasted_iota(jnp.int32, (n, n), 1)).astype(dtype)


def _transposed_values(v_ref_rows, n_tiles):
    rows = lax.broadcasted_iota(jnp.int32, (LANE, HEAD_DIM), 0)
    cols = lax.broadcasted_iota(jnp.int32, (LANE, HEAD_DIM), 1)
    place = (rows == cols).astype(BF16)
    ones_row = (lax.broadcasted_iota(jnp.int32, (LANE, T_TILE), 0) == HEAD_DIM).astype(F32)
    for j in range(n_tiles):
        v = v_ref_rows(slice(j * T_TILE, (j + 1) * T_TILE))
        yield j, (_nt(place, v) + ones_row).astype(BF16)


def _with_block_onehot(k, block):
    n = k.shape[0]
    rows = lax.broadcasted_iota(jnp.int32, (HEAD_DIM, LANE), 0)
    cols = lax.broadcasted_iota(jnp.int32, (HEAD_DIM, LANE), 1)
    place = (rows == cols).astype(BF16)
    lane = lax.broadcasted_iota(jnp.int32, (n, LANE), 1)
    hot = (lane - HEAD_DIM == lax.broadcasted_iota(jnp.int32, (n, LANE), 0) // block).astype(F32)
    return (jnp.dot(k, place, preferred_element_type=F32) + hot).astype(BF16)


def _augment_q_t(q_t, allowed_t):
    pen = (allowed_t - 1.0) * MASK_BIG
    return jnp.concatenate([q_t, pen, jnp.zeros(pen.shape, F32)], axis=0).astype(BF16)


_N_SLC = 32
_G_ROWS = B_GROUP * T_TILE


def _nsa_kernel(pb_ref, cmp_ref, bg_ref, bias_ref, o_ref,
                kaug_ref, vst_ref, vwt_ref, eye_ref, sig_ref, q3t_ref, qaugt_ref, s_ref, sw_ref,
                rmp_ref, acc_ref, ocmp_ref, *, seq):
    n_tiles = seq // T_TILE
    n_cmp_rows = seq // NSA_CMP_STRIDE
    q_cols = B_HEADS * HEAD_DIM
    ks0, vs0, kw0, vw0 = (q_cols + i * B_KV_HEADS * HEAD_DIM for i in range(4))

    def head_cols(c0, h):
        return slice(c0 + h * HEAD_DIM, c0 + (h + 1) * HEAD_DIM)

    j_id = lax.broadcasted_iota(jnp.int32, (_N_SLC, n_cmp_rows), 0)
    c_id = lax.broadcasted_iota(jnp.int32, (_N_SLC, n_cmp_rows), 1)
    overlap_t = ((c_id * NSA_CMP_STRIDE < (j_id + 1) * NSA_SLC_BLOCK)
                 & (c_id * NSA_CMP_STRIDE + NSA_CMP_BLOCK > j_id * NSA_SLC_BLOCK)).astype(F32)
    eye_ref[...] = _eye(T_TILE)
    for h in range(B_KV_HEADS):
        kaug_ref[h] = _with_block_onehot(pb_ref[0, :, head_cols(ks0, h)], NSA_SLC_BLOCK)
        for j, vt in _transposed_values(lambda r, h=h: pb_ref[0, r, head_cols(vs0, h)], n_tiles):
            vst_ref[h, j] = vt
        for j, vt in _transposed_values(lambda r, h=h: pb_ref[0, r, head_cols(vw0, h)], n_tiles):
            vwt_ref[h, j] = vt

    def tiles(a):
        return [a[:, j * T_TILE:(j + 1) * T_TILE] for j in range(n_tiles)]

    pick = (lax.broadcasted_iota(jnp.int32, (SEL_PAD, _BG_COLS), 0)
            == lax.broadcasted_iota(jnp.int32, (SEL_PAD, _BG_COLS), 1)).astype(F32)
    sig_all = jax.nn.sigmoid(_nt(pick, bg_ref[0], precision=lax.Precision.HIGHEST))
    for j, t in enumerate(tiles(sig_all)):
        sig_ref[j] = t

    eye_h = _eye(HEAD_DIM)
    t_lane = lax.broadcasted_iota(jnp.int32, (n_cmp_rows, seq), 1)
    c_end = (lax.broadcasted_iota(jnp.int32, (n_cmp_rows, seq), 0) * NSA_CMP_STRIDE
             + NSA_CMP_BLOCK - 1)
    ok = (c_end <= t_lane) & (c_end < seq)
    ok_f = ok.astype(F32)
    j_sub = lax.broadcasted_iota(jnp.int32, (_N_SLC, seq), 0)
    jt = lax.broadcasted_iota(jnp.int32, (_N_SLC, seq), 1) // NSA_SLC_BLOCK
    forced = ((j_sub == 0) | (j_sub == jt) | (j_sub == jt - 1)).astype(F32)
    valid = j_sub <= jt

    for h in range(B_KV_HEADS):
        kc = cmp_ref[0, h]
        vct = _nt(eye_h, cmp_ref[0, B_KV_HEADS + h]).astype(BF16)
        q_ts, p_sum = [], None
        for g in range(B_GROUP):
            q_t = _nt(eye_h, pb_ref[0, :, head_cols(0, h * B_GROUP + g)]) * ATTN_SCALE
            q_ts.append(q_t)
            lc = jnp.where(ok, jnp.dot(kc, q_t.astype(BF16), preferred_element_type=F32), NEG_INF)
            ec = jnp.exp(lc - jnp.max(lc, axis=0, keepdims=True))
            pc = ec / jnp.sum(ec, axis=0, keepdims=True) * ok_f
            o_cmp = jnp.dot(vct, pc.astype(BF16), preferred_element_type=F32)
            for j, t in enumerate(tiles(o_cmp)):
                ocmp_ref[h, j, :, g * T_TILE:(g + 1) * T_TILE] = t
            p_sum = pc if p_sum is None else p_sum + pc
        imp_t = jnp.dot(overlap_t, p_sum, precision=lax.Precision.HIGHEST,
                        preferred_element_type=F32)
        score = jnp.where(valid, imp_t + FORCE_BONUS * forced, NEG_INF)
        sel_t = jnp.where(valid, _top_keep(score, _N_SLC, NSA_SLC_TOPK, NSA_SLC_BLOCK), 0.0)
        pen_tiles = tiles((sel_t - 1.0) * MASK_BIG)
        q_tiles = [tiles(q_t) for q_t in q_ts]
        for j in range(n_tiles):
            q3t = jnp.concatenate([q_tiles[g][j] for g in range(B_GROUP)], axis=1)
            pen = jnp.concatenate([pen_tiles[j]] * B_GROUP, axis=1)
            q3t_ref[h, j] = q3t.astype(BF16)
            qaugt_ref[h, j] = jnp.concatenate([q3t, pen, jnp.zeros(pen.shape, F32)],
                                              axis=0).astype(BF16)

    def tile_body(i, carry):
        rows = pl.ds(pl.multiple_of(i * T_TILE, T_TILE), T_TILE)

        outs = _two_pass_attention(
            i,
            [(lambda kr, h=h: kaug_ref[h, kr, :], lambda h=h: qaugt_ref[h, i], vst_ref.at[h],
              lambda d, h=h: bias_ref[h, d], s_ref.at[h], rmp_ref.at[0, h], acc_ref.at[0, h])
             for h in range(B_KV_HEADS)],
            [(lambda kr, h=h: pb_ref[0, kr, head_cols(kw0, h)], lambda h=h: q3t_ref[h, i],
              vwt_ref.at[h], lambda d, h=h: bias_ref[h, d], sw_ref.at[h], rmp_ref.at[1, h],
              acc_ref.at[1, h]) for h in range(B_KV_HEADS)])
        oslc, owin = outs[:B_KV_HEADS], outs[B_KV_HEADS:]

        sig_t = sig_ref[i]
        heads = []
        for h in range(B_KV_HEADS):
            for g in range(B_GROUP):
                head = h * B_GROUP + g
                gc = slice(g * T_TILE, (g + 1) * T_TILE)
                heads.append(sig_t[3 * head:3 * head + 1, :] * ocmp_ref[h, i, :, gc]
                             + sig_t[3 * head + 1:3 * head + 2, :] * oslc[h][:, gc]
                             + sig_t[3 * head + 2:3 * head + 3, :] * owin[h][:, gc])
        o_t = jnp.concatenate(heads, axis=0).astype(BF16)
        o_ref[0, rows, :] = _nt(eye_ref[...], o_t).astype(o_ref.dtype)
        return carry

    lax.fori_loop(0, n_tiles, tile_body, 0)


def _nsa(pb, cmp, bg, bias):
    b, seq, _ = pb.shape
    n_tiles = seq // T_TILE
    return pl.pallas_call(
        functools.partial(_nsa_kernel, seq=seq),
        out_shape=jax.ShapeDtypeStruct((b, seq, B_HEADS * HEAD_DIM), BF16),
        grid=(b,),
        in_specs=[pl.BlockSpec((1, seq, _PB_COLS), lambda i: (i, 0, 0)),
                  pl.BlockSpec((1,) + cmp.shape[1:], lambda i: (i, 0, 0, 0)),
                  pl.BlockSpec((1, seq, _BG_COLS), lambda i: (i, 0, 0)),
                  pl.BlockSpec(bias.shape, lambda i: (0, 0, 0, 0))],
        out_specs=pl.BlockSpec((1, seq, B_HEADS * HEAD_DIM), lambda i: (i, 0, 0)),
        scratch_shapes=[pltpu.VMEM((B_KV_HEADS, seq, LANE), BF16),
                        pltpu.VMEM((B_KV_HEADS, n_tiles, LANE, T_TILE), BF16),
                        pltpu.VMEM((B_KV_HEADS, n_tiles, LANE, T_TILE), BF16),
                        pltpu.VMEM((T_TILE, T_TILE), BF16),
                        pltpu.VMEM((n_tiles, SEL_PAD, T_TILE), F32),
                        pltpu.VMEM((B_KV_HEADS, n_tiles, HEAD_DIM, _G_ROWS), BF16),
                        pltpu.VMEM((B_KV_HEADS, n_tiles, LANE, _G_ROWS), BF16),
                        pltpu.VMEM((B_KV_HEADS, n_tiles, T_TILE, _G_ROWS), F32),
                        pltpu.VMEM((B_KV_HEADS, BAND_SLOTS, T_TILE, _G_ROWS), F32),
                        pltpu.VMEM((2, B_KV_HEADS, 8, _G_ROWS), F32),
                        pltpu.VMEM((2, B_KV_HEADS, LANE, _G_ROWS), F32),
                        pltpu.VMEM((B_KV_HEADS, n_tiles, HEAD_DIM, _G_ROWS), F32)],
        compiler_params=_params(1),
        name="nsa",
    )(pb, cmp, bg, bias)


def _moba_kernel(pc_ref, bias_ref, o_ref, kaug_ref, vt_ref, eye_ref, qaugt_ref, s_ref,
                 rmp_ref, acc_ref, *, seq):
    n_blk = seq // MOBA_BLOCK
    width = C_HEADS * HEAD_DIM
    eye_ref[...] = _eye(T_TILE)

    def cols(j, h):
        return slice(j * width + h * HEAD_DIM, j * width + (h + 1) * HEAD_DIM)

    n_sub = lax.broadcasted_iota(jnp.int32, (SEL_PAD, seq), 0)
    own = lax.broadcasted_iota(jnp.int32, (SEL_PAD, seq), 1) // MOBA_BLOCK
    past = n_sub < own
    eye_h = _eye(HEAD_DIM)
    for h in range(C_HEADS):
        kaug_ref[h] = _with_block_onehot(pc_ref[0, :, cols(1, h)], MOBA_BLOCK)
        for j, vt in _transposed_values(lambda r, h=h: pc_ref[0, r, cols(2, h)], n_blk):
            vt_ref[h, j] = vt
        kmean = jnp.concatenate(
            [jnp.mean(pc_ref[0, n * MOBA_BLOCK:(n + 1) * MOBA_BLOCK, cols(1, h)].astype(F32), axis=0,
                      keepdims=True) for n in range(n_blk)]
            + [jnp.zeros((SEL_PAD - n_blk, HEAD_DIM), F32)], axis=0)
        q_t = _nt(eye_h, pc_ref[0, :, cols(0, h)])
        gate_t = jnp.dot(kmean, q_t, precision=lax.Precision.HIGHEST,
                         preferred_element_type=F32)
        score = jnp.where(past, gate_t, NEG_INF)
        picked = past & (_top_keep(score, n_blk, MOBA_TOPK, MOBA_BLOCK) > 0.5)
        allowed_t = jnp.where(picked | (n_sub == own), 1.0, 0.0)
        qaug_t = _augment_q_t(q_t * ATTN_SCALE, allowed_t)
        for j in range(n_blk):
            qaugt_ref[h, j] = qaug_t[:, j * T_TILE:(j + 1) * T_TILE]

    def tile_body(i, carry):
        rows = pl.ds(pl.multiple_of(i * MOBA_BLOCK, MOBA_BLOCK), MOBA_BLOCK)
        outs = _two_pass_attention(
            i, [(lambda kr, h=h: kaug_ref[h, kr, :], lambda h=h: qaugt_ref[h, i], vt_ref.at[h],
                 lambda d, h=h: bias_ref[h, d], s_ref.at[h], rmp_ref.at[h], acc_ref.at[h])
                for h in range(C_HEADS)])
        o_t = jnp.concatenate(outs, axis=0).astype(BF16)
        o_ref[0, rows, :] = _nt(eye_ref[...], o_t).astype(o_ref.dtype)
        return carry

    lax.fori_loop(0, n_blk, tile_body, 0)


def _moba(pc, bias):
    b, seq, _ = pc.shape
    width = C_HEADS * HEAD_DIM
    n_tiles = seq // T_TILE
    return pl.pallas_call(
        functools.partial(_moba_kernel, seq=seq),
        out_shape=jax.ShapeDtypeStruct((b, seq, width), BF16),
        grid=(b,),
        in_specs=[pl.BlockSpec((1, seq, _C_COLS), lambda i: (i, 0, 0)),
                  pl.BlockSpec(bias.shape, lambda i: (0, 0, 0, 0))],
        out_specs=pl.BlockSpec((1, seq, width), lambda i: (i, 0, 0)),
        scratch_shapes=[pltpu.VMEM((C_HEADS, seq, LANE), BF16),
                        pltpu.VMEM((C_HEADS, n_tiles, LANE, T_TILE), BF16),
                        pltpu.VMEM((T_TILE, T_TILE), BF16),
                        pltpu.VMEM((C_HEADS, n_tiles, LANE, T_TILE), BF16),
                        pltpu.VMEM((C_HEADS, n_tiles, T_TILE, T_TILE), F32),
                        pltpu.VMEM((C_HEADS, 8, T_TILE), F32),
                        pltpu.VMEM((C_HEADS, LANE, T_TILE), F32)],
        compiler_params=_params(1),
        name="moba",
    )(pc, bias)


def _merge_kernel(oa_ref, ob_ref, oc_ref, mg_ref, x_ref, wb_ref, wo_ref, y_ref):
    r0 = A_HEADS * HEAD_DIM
    r1 = r0 + B_HEADS * HEAD_DIM
    ya = jnp.dot(oa_ref[...], wb_ref[0:r0, :], preferred_element_type=F32)
    yb = jnp.dot(ob_ref[...], wb_ref[r0:r1, :], preferred_element_type=F32)
    yc = jnp.dot(oc_ref[...], wb_ref[r1:, :], preferred_element_type=F32)
    merged = (jax.nn.sigmoid(mg_ref[:, 0:D_MODEL].astype(F32)) * ya
              + jax.nn.sigmoid(mg_ref[:, D_MODEL:2 * D_MODEL].astype(F32)) * yb
              + jax.nn.sigmoid(mg_ref[:, 2 * D_MODEL:].astype(F32)) * yc)
    y_ref[...] = x_ref[...] + jnp.dot(merged.astype(BF16), wo_ref[...], preferred_element_type=F32)


def _merge(oa, ob, oc, mg, x2, wb, wo, *, layer):
    t = x2.shape[0]
    tm = ROW_TILE
    row = lambda i: (i, 0)
    weight = pl.BlockSpec((None,) + wb.shape[1:], lambda i: (layer, 0, 0))
    return pl.pallas_call(
        _merge_kernel,
        out_shape=jax.ShapeDtypeStruct((t, D_MODEL), F32),
        grid=(t // tm,),
        in_specs=[pl.BlockSpec((tm, oa.shape[1]), row),
                  pl.BlockSpec((tm, ob.shape[1]), row), pl.BlockSpec((tm, oc.shape[1]), row),
                  pl.BlockSpec((tm, _MG_COLS), row), pl.BlockSpec((tm, D_MODEL), row),
                  weight, weight],
        out_specs=pl.BlockSpec((tm, D_MODEL), row),
        compiler_params=_params(1),
        name="merge",
    )(oa, ob, oc, mg, x2, wb, wo)


MXU_WIDTH = 256
_FF_CHUNK_TILES = 6
_FF_BOUNDS = tuple(range(0, D_FF, _FF_CHUNK_TILES * MXU_WIDTH)) + (D_FF,)
_FF_CHUNKS = tuple(zip(_FF_BOUNDS[:-1], _FF_BOUNDS[1:]))


def _ffn_kernel(x_ref, halo_ref, g_ref, wu_ref, cw_ref, cb_ref, wd_ref, gf_ref, y_ref, xn_ref,
                *, tiles_per_seq, final_norm):
    i = pl.program_id(0)
    x = x_ref[...]
    g = g_ref[...]
    keep = (i % tiles_per_seq != 0).astype(F32)
    xn_ref[0:FFN_HALO, :] = (_rmsnorm_rows(halo_ref[...], g) * keep).astype(BF16)
    xn_ref[FFN_HALO:, :] = _rmsnorm_rows(x, g).astype(BF16)
    xn = xn_ref[...]
    rows = xn.shape[0]

    def conv(c0, c1):
        hcol = jnp.dot(xn, wu_ref[:, c0:c1], preferred_element_type=F32)
        out = (cw_ref[2:3, c0:c1] * hcol
               + cw_ref[1:2, c0:c1] * pltpu.roll(hcol, 1, axis=0)
               + cw_ref[0:1, c0:c1] * pltpu.roll(hcol, 2, axis=0)
               + cb_ref[:, c0:c1])
        return out[FFN_HALO:rows]

    acc = x
    for c0, c1 in _FF_CHUNKS:
        a = conv(c0, c1)
        u = conv(D_FF + c0, D_FF + c1)
        act = (a * jax.nn.sigmoid(a) * u).astype(BF16)
        acc = acc + jnp.dot(act, wd_ref[c0:c1, :], preferred_element_type=F32)
    if final_norm:
        acc = _rmsnorm_rows(acc, gf_ref[...])
    y_ref[...] = acc


def _ffn(x2, g, wu, cw, cb, wd, gf, *, layer, seq, final_norm):
    t = x2.shape[0]
    tm = ROW_TILE
    const = lambda i: (0, 0)
    per_layer = lambda w: pl.BlockSpec((None,) + w.shape[1:], lambda i: (layer, 0, 0))
    halo_blocks = tm // FFN_HALO
    return pl.pallas_call(
        functools.partial(_ffn_kernel, tiles_per_seq=seq // tm, final_norm=final_norm),
        out_shape=jax.ShapeDtypeStruct((t, D_MODEL), F32),
        grid=(t // tm,),
        in_specs=[pl.BlockSpec((tm, D_MODEL), lambda i: (i, 0)),
                  pl.BlockSpec((FFN_HALO, D_MODEL), lambda i: (jnp.maximum(i * halo_blocks - 1, 0), 0)),
                  pl.BlockSpec((1, D_MODEL), const),
                  per_layer(wu), pl.BlockSpec(cw.shape, const),
                  pl.BlockSpec(cb.shape, const), per_layer(wd),
                  pl.BlockSpec((1, D_MODEL), const)],
        out_specs=pl.BlockSpec((tm, D_MODEL), lambda i: (i, 0)),
        scratch_shapes=[pltpu.VMEM((tm + FFN_HALO, D_MODEL), BF16)],
        compiler_params=_params(1),
        name="conv_ffn",
    )(x2, x2, g, wu, cw, cb, wd, gf)


def kernel(x, rel_bias, norm_mix, w_in, cmp_pe_k, cmp_w1_k, cmp_w2_k, cmp_pe_v, cmp_w1_v, cmp_w2_v,
           w_branch, w_out, norm_ffn, w_up, conv_w, conv_b, w_down, norm_final):
    b, s, d = x.shape
    depth = w_in.shape[0]
    t = b * s
    assert d == D_MODEL and s % T_TILE == 0 and s // NSA_SLC_BLOCK == _N_SLC and t % ROW_TILE == 0
    assert s % ROW_TILE == 0 and all(s % (dil * A_TILE) == 0 for _, dil in A_GROUPS)
    assert all(win // dil == A_TILE for win, dil in A_GROUPS) and A_GROUPS[0][1] == 1
    assert all(ROW_TILE % dil == 0 for _, dil in A_GROUPS)

    hg = A_HEADS_PER_GROUP
    bias_a = jnp.concatenate([
        _bias_tiles(rel_bias,
                    np.concatenate([_bucket_tile(A_TILE, 1, dil, win // dil),
                                    _bucket_tile(A_TILE, 0, dil, win // dil)], axis=1),
                    gi * hg, hg)
        for gi, (win, dil) in enumerate(A_GROUPS)]
        + [_bias_tiles(rel_bias,
                       np.concatenate([np.full((A_TILE, A_TILE), -1, np.int32),
                                       _bucket_tile(A_TILE, 0, A_GROUPS[-1][1], A_TILE)], axis=1),
                       (len(A_GROUPS) - 1) * hg, hg)]
        + [_bias_tiles(rel_bias,
                       np.concatenate([_bucket_tile(A_TILE, 0, dil, win // dil),
                                       np.full((A_TILE, A_TILE), -1, np.int32)], axis=1),
                       gi * hg, hg)
           for gi, (win, dil) in enumerate(A_GROUPS)], axis=0)
    assert (_bucket_tile(T_TILE, 2, 1, 3 * T_TILE) == REL_BUCKETS - 1).all()
    bias_b = _bias_tiles_t(rel_bias, [_bucket_tile(T_TILE, d, 1, NSA_WINDOW - 1) for d in (0, 1, 2)],
                           A_HEADS, B_HEADS)
    bias_b = bias_b.reshape(B_KV_HEADS, B_GROUP, 3, T_TILE, T_TILE).transpose(0, 2, 3, 1, 4).reshape(
        B_KV_HEADS, 3, T_TILE, _G_ROWS)
    bias_c = _bias_tiles_t(rel_bias, [_bucket_tile(T_TILE, d, 1, s) for d in (0, 1)],
                           A_HEADS + B_HEADS, C_HEADS)

    w_in_k = _prep_w_in(w_in)
    w_branch_k, w_out_k, w_up_k, w_down_k = (w.astype(BF16) for w in (w_branch, w_out, w_up, w_down))
    x2 = x.reshape(t, d)
    for i in range(depth):
        pa0, pa1, pa2, pb, pcmp, pc, mg, bg = _proj_in(x2, norm_mix[i][None, :], w_in_k, layer=i,
                                                       seq=s)

        oa = _mixer_a(pa0.reshape(b, s, _GRP_COLS), pa1, pa2.reshape(b, s, _GRP_COLS), bias_a)

        cmp = _compress(pcmp.reshape(4, b, s // NSA_CMP_STRIDE, NSA_CMP_STRIDE * HEAD_DIM),
                        cmp_pe_k[i].reshape(1, -1), cmp_pe_v[i].reshape(1, -1),
                        cmp_w1_k[i].astype(BF16), cmp_w1_v[i].astype(BF16),
                        cmp_w2_k[i].astype(BF16), cmp_w2_v[i].astype(BF16))
        ob = _nsa(pb.reshape(b, s, _PB_COLS), cmp, bg.reshape(b, s, _BG_COLS), bias_b)

        oc = _moba(pc.reshape(b, s, _C_COLS), bias_c)

        x2 = _merge(oa.reshape(t, -1), ob.reshape(t, -1), oc.reshape(t, -1), mg, x2,
                    w_branch_k, w_out_k, layer=i)
        x2 = _ffn(x2, norm_ffn[i][None, :], w_up_k, conv_w[i], conv_b[i][None, :], w_down_k,
                  norm_final[None, :], layer=i, seq=s, final_norm=(i == depth - 1))
    return x2.reshape(b, s, d)
```

```python
import functools
import math

import jax
import jax.numpy as jnp
import numpy as np
from jax import lax
from jax.experimental import pallas as pl
from jax.experimental.pallas import tpu as pltpu

F32 = jnp.float32
BF16 = jnp.bfloat16

D_MODEL = 1024
HEAD_DIM = 64
A_GROUPS = ((128, 1), (512, 4), (2048, 16))
A_HEADS_PER_GROUP = 2
A_HEADS = 6
B_HEADS = 6
B_KV_HEADS = 2
B_GROUP = 3
C_HEADS = 4
NSA_CMP_BLOCK = 32
NSA_CMP_STRIDE = 16
NSA_SLC_BLOCK = 64
NSA_SLC_TOPK = 16
NSA_WINDOW = 512
MOBA_BLOCK = 256
MOBA_TOPK = 3
REL_BUCKETS = 32
REL_MAX_DIST = 128
D_FF = 2816
RMS_EPS = 1e-6
NEG_INF = -1e30
FORCE_BONUS = 1e4
ATTN_SCALE = HEAD_DIM ** -0.5

LANE = 128
A_TILE = 128
T_TILE = 256
ROW_TILE = 512
FFN_HALO = 8
VMEM_LIMIT = 56 * 1024 * 1024

_GRP_COLS = 3 * A_HEADS_PER_GROUP * HEAD_DIM
_A_COLS = len(A_GROUPS) * _GRP_COLS
_PB_COLS = B_HEADS * HEAD_DIM + 4 * B_KV_HEADS * HEAD_DIM
_CMP_COLS = 2 * B_KV_HEADS * HEAD_DIM
_C_COLS = 3 * C_HEADS * HEAD_DIM
_MG_COLS = 3 * D_MODEL
_BG_COLS = LANE
_OFF_A = 0
_OFF_PB = _OFF_A + _A_COLS
_OFF_CMP = _OFF_PB + _PB_COLS
_OFF_C = _OFF_CMP + _CMP_COLS
_OFF_MG = _OFF_C + _C_COLS
_OFF_BG = _OFF_MG + _MG_COLS
_W_COLS = _OFF_BG + _BG_COLS
_PROJ_CHUNK = 4 * LANE


def _params(n_grid):
    return pltpu.CompilerParams(dimension_semantics=("arbitrary",) * n_grid,
                                vmem_limit_bytes=VMEM_LIMIT)


def _rel_bucket_np(dist):
    n = np.maximum(dist, 0)
    exact = REL_BUCKETS // 2
    nf = np.maximum(n, 1).astype(np.float32)
    large = exact + (np.log(nf / np.float32(exact)) / np.float32(math.log(REL_MAX_DIST / exact))
                     * np.float32(REL_BUCKETS - exact)).astype(np.int32)
    return np.where(n < exact, n, np.minimum(large, REL_BUCKETS - 1)).astype(np.int32)


def _bucket_tile(tile, block_offset, dil, max_rel):
    rel = block_offset * tile + np.arange(tile)[:, None] - np.arange(tile)[None, :]
    ok = (rel >= 0) & (rel <= max_rel)
    return np.where(ok, _rel_bucket_np(rel * dil), -1).astype(np.int32)


def _bias_tiles_kernel(tbl_ref, idx_ref, o_ref, *, head0, relative):
    h = pl.program_id(0) + head0
    idx = idx_ref[...]
    acc = jnp.full(idx.shape, NEG_INF, F32)
    for b in range(REL_BUCKETS):
        acc = jnp.where(idx == b, tbl_ref[b, h], acc)
    if relative:
        acc = acc - tbl_ref[REL_BUCKETS - 1, h]
    o_ref[0] = acc


def _bias_tiles(rel_bias, idx, head0, n_heads, relative=False):
    r, c = idx.shape
    return pl.pallas_call(
        functools.partial(_bias_tiles_kernel, head0=head0, relative=relative),
        out_shape=jax.ShapeDtypeStruct((n_heads, r, c), F32),
        grid=(n_heads,),
        in_specs=[pl.BlockSpec(memory_space=pltpu.SMEM),
                  pl.BlockSpec((r, c), lambda h: (0, 0))],
        out_specs=pl.BlockSpec((1, r, c), lambda h: (h, 0, 0)),
        compiler_params=_params(1),
        name="bias_tiles",
    )(rel_bias, jnp.asarray(idx))


def _bias_tiles_t(rel_bias, idx_tiles, head0, n_heads):
    r, c = idx_tiles[0].shape
    flat = _bias_tiles(rel_bias, np.concatenate([t.T for t in idx_tiles], axis=1), head0, n_heads,
                       relative=True)
    return flat.reshape(n_heads, c, len(idx_tiles), r).transpose(0, 2, 1, 3)


def _rmsnorm_rows(x, g):
    return x * lax.rsqrt(jnp.mean(x * x, axis=-1, keepdims=True) + RMS_EPS) * g


def _proj_in_kernel(x_ref, g_ref, w_ref, pa0_ref, pa1_ref, pa2_ref, pb_ref, pcmp_ref, pc_ref,
                    mg_ref, bg_ref, regroup_ref):
    xn = _rmsnorm_rows(x_ref[...], g_ref[...]).astype(BF16)
    tm = xn.shape[0]

    def direct(out_ref):
        def sink(lo, hi, val):
            out_ref[:, lo:hi] = val.astype(out_ref.dtype)
        return sink

    def staged(slot0):
        def sink(lo, hi, val):
            for c in range(lo, hi, LANE):
                regroup_ref[slot0 + c // LANE] = val[:, c - lo:c - lo + LANE]
        return sink

    g1_slot, g2_slot, cmp_slot = 0, 3, 6
    sections = ((_OFF_A, _GRP_COLS, direct(pa0_ref)),
                (_OFF_A + _GRP_COLS, _GRP_COLS, staged(g1_slot)),
                (_OFF_A + 2 * _GRP_COLS, _GRP_COLS, staged(g2_slot)),
                (_OFF_PB, _PB_COLS, direct(pb_ref)), (_OFF_CMP, _CMP_COLS, staged(cmp_slot)),
                (_OFF_C, _C_COLS, direct(pc_ref)), (_OFF_MG, _MG_COLS, direct(mg_ref)),
                (_OFF_BG, _BG_COLS, direct(bg_ref)))
    for c0 in range(0, _W_COLS, _PROJ_CHUNK):
        c1 = min(c0 + _PROJ_CHUNK, _W_COLS)
        res = jnp.dot(xn, w_ref[:, c0:c1], preferred_element_type=F32)
        for off, width, sink in sections:
            lo, hi = max(c0, off), min(c1, off + width)
            if lo < hi:
                sink(lo - off, hi - off, res[:, lo - c0:hi - c0])

    for gi, slot0, out_ref in ((1, g1_slot, pa1_ref), (2, g2_slot, pa2_ref)):
        dil = A_GROUPS[gi][1]
        for r in range(dil):
            for j in range(3):
                out_ref[0, r, :, j * LANE:(j + 1) * LANE] = (
                    regroup_ref[slot0 + j, pl.ds(r, tm // dil, stride=dil), :].astype(BF16))
    for l in range(NSA_CMP_STRIDE):
        for kv in range(2):
            a = regroup_ref[cmp_slot + kv, pl.ds(l, tm // NSA_CMP_STRIDE, stride=NSA_CMP_STRIDE),
                            :].astype(BF16)
            for h in range(B_KV_HEADS):
                pcmp_ref[kv * B_KV_HEADS + h, :, l * HEAD_DIM:(l + 1) * HEAD_DIM] = (
                    a[:, h * HEAD_DIM:(h + 1) * HEAD_DIM])


def _proj_in(x2, g, w, *, layer, seq):
    t = x2.shape[0]
    tm = ROW_TILE
    b = t // seq
    tps = seq // tm
    row = lambda i: (i, 0)
    d1, d2 = A_GROUPS[1][1], A_GROUPS[2][1]
    sub = lambda i: (i // tps, 0, i % tps, 0)
    return pl.pallas_call(
        _proj_in_kernel,
        out_shape=(jax.ShapeDtypeStruct((t, _GRP_COLS), BF16),
                   jax.ShapeDtypeStruct((b, d1, seq // d1, _GRP_COLS), BF16),
                   jax.ShapeDtypeStruct((b, d2, seq // d2, _GRP_COLS), BF16),
                   jax.ShapeDtypeStruct((t, _PB_COLS), BF16),
                   jax.ShapeDtypeStruct((4, t // NSA_CMP_STRIDE, NSA_CMP_STRIDE * HEAD_DIM), BF16),
                   jax.ShapeDtypeStruct((t, _C_COLS), BF16),
                   jax.ShapeDtypeStruct((t, _MG_COLS), BF16),
                   jax.ShapeDtypeStruct((t, _BG_COLS), F32)),
        grid=(t // tm,),
        in_specs=[pl.BlockSpec((tm, D_MODEL), row),
                  pl.BlockSpec((1, D_MODEL), lambda i: (0, 0)),
                  pl.BlockSpec((None, D_MODEL, _W_COLS), lambda i: (layer, 0, 0))],
        out_specs=(pl.BlockSpec((tm, _GRP_COLS), row),
                   pl.BlockSpec((1, d1, tm // d1, _GRP_COLS), sub),
                   pl.BlockSpec((1, d2, tm // d2, _GRP_COLS), sub),
                   pl.BlockSpec((tm, _PB_COLS), row),
                   pl.BlockSpec((4, tm // NSA_CMP_STRIDE, NSA_CMP_STRIDE * HEAD_DIM),
                                lambda i: (0, i, 0)),
                   pl.BlockSpec((tm, _C_COLS), row),
                   pl.BlockSpec((tm, _MG_COLS), row),
                   pl.BlockSpec((tm, _BG_COLS), row)),
        scratch_shapes=[pltpu.VMEM((8, tm, LANE), F32)],
        compiler_params=_params(1),
        name="proj_in",
    )(x2, g, w)


def _w_in_segments():
    a = A_HEADS * HEAD_DIM
    bq0 = 3 * a
    bkc0 = bq0 + B_HEADS * HEAD_DIM
    bks0 = bkc0 + 2 * B_KV_HEADS * HEAD_DIM
    bg0 = bks0 + 4 * B_KV_HEADS * HEAD_DIM
    c0 = bg0 + B_HEADS * 3
    mg0 = c0 + _C_COLS
    gw = A_HEADS_PER_GROUP * HEAD_DIM
    segs = [(_OFF_A + (gi * 3 + j) * gw, j * a + gi * gw, gw)
            for gi in range(len(A_GROUPS)) for j in range(3)]
    segs += [(_OFF_PB, bq0, B_HEADS * HEAD_DIM),
             (_OFF_PB + B_HEADS * HEAD_DIM, bks0, bg0 - bks0),
             (_OFF_CMP, bkc0, _CMP_COLS), (_OFF_C, c0, _C_COLS), (_OFF_MG, mg0, _MG_COLS),
             (_OFF_BG, bg0, B_HEADS * 3)]
    return segs


def _prep_w_in_kernel(w_ref, o_ref):
    o_ref[0, :, _OFF_BG:] = jnp.zeros((o_ref.shape[1], _BG_COLS), BF16)
    for dst, src, width in _w_in_segments():
        for c in range(0, width, _PROJ_CHUNK):
            n = min(_PROJ_CHUNK, width - c)
            o_ref[0, :, dst + c:dst + c + n] = w_ref[0, :, src + c:src + c + n].astype(BF16)


def _prep_w_in(w):
    depth, rows, cols = w.shape
    rb = 128
    return pl.pallas_call(
        _prep_w_in_kernel,
        out_shape=jax.ShapeDtypeStruct((depth, rows, _W_COLS), BF16),
        grid=(depth, rows // rb),
        in_specs=[pl.BlockSpec((1, rb, cols), lambda l, r: (l, r, 0))],
        out_specs=pl.BlockSpec((1, rb, _W_COLS), lambda l, r: (l, r, 0)),
        compiler_params=_params(2),
        name="prep_w_in",
    )(w)


_A_UNROLL = 15


def _mixer_a_kernel(pa0_ref, pa1_ref, pa2_ref, bias_ref, o_ref, on_ref, ln_ref, *, seq):
    gw = A_HEADS_PER_GROUP * HEAD_DIM

    def band_block(src, l0, first, gi, rows, bias_head0=None):
        if bias_head0 is None:
            bias_head0 = A_HEADS_PER_GROUP * gi + (A_HEADS + A_HEADS_PER_GROUP if first else 0)
        kr = pl.ds(l0 if first else l0 - A_TILE, 2 * A_TILE)
        q = src[pl.ds(l0, A_TILE), 0:gw] * ATTN_SCALE
        k, v = src[kr, gw:2 * gw], src[kr, 2 * gw:3 * gw]
        lane = lax.broadcasted_iota(jnp.int32, (A_TILE, gw), 1)
        out = lse = None
        for h in range(A_HEADS_PER_GROUP):
            own = (lane >= h * HEAD_DIM) & (lane < (h + 1) * HEAD_DIM)
            bias = bias_ref[bias_head0 + h]
            s = _nt(q * own[0:1, :].astype(BF16), k) + bias
            m = jnp.max(s, axis=-1, keepdims=True)
            e = jnp.exp(s - m)
            l = jnp.sum(e, axis=-1, keepdims=True)
            o_h = jnp.dot(e.astype(BF16), v, preferred_element_type=F32) / l
            lse_h = jnp.broadcast_to(m + jnp.log(l), (A_TILE, gw))
            out = o_h if out is None else jnp.where(own, o_h, out)
            lse = lse_h if lse is None else jnp.where(own, lse_h, lse)
        on_ref[gi, rows, :] = out
        ln_ref[gi, rows, :] = lse

    for gi, ((_, dil), pa_ref) in enumerate(zip(A_GROUPS, (pa0_ref, pa1_ref, pa2_ref))):
        n_blocks = seq // dil // A_TILE
        if dil == 1:
            band_block(pa_ref.at[0], 0, True, gi, pl.ds(0, A_TILE))
            assert (n_blocks - 1) % _A_UNROLL == 0

            def body0(it, c, pa_ref=pa_ref, gi=gi):
                for u in range(_A_UNROLL):
                    l0 = pl.multiple_of((1 + it * _A_UNROLL + u) * A_TILE, A_TILE)
                    band_block(pa_ref.at[0], l0, False, gi, pl.ds(l0, A_TILE))
                return c
            lax.fori_loop(0, (n_blocks - 1) // _A_UNROLL, body0, 0)
        elif n_blocks > 1:
            for r in range(dil):
                band_block(pa_ref.at[0, r], 0, True, gi, pl.ds(r, A_TILE, stride=dil))

            for i in range(1, n_blocks):
                l0 = i * A_TILE
                for r in range(dil):
                    band_block(pa_ref.at[0, r], l0, False, gi, pl.ds(l0 * dil + r, A_TILE, stride=dil))
        else:
            assert (dil - 1) % _A_UNROLL == 0
            band_block(pa_ref.at[0], 0, True, gi, pl.ds(0, A_TILE, stride=dil))

            def body2(it, c, pa_ref=pa_ref, gi=gi, dil=dil):
                for u in range(_A_UNROLL):
                    r = 1 + it * _A_UNROLL + u
                    band_block(pa_ref.at[0], pl.multiple_of(r * A_TILE, A_TILE), False, gi,
                               pl.ds(r, A_TILE, stride=dil), bias_head0=A_HEADS)
                return c
            lax.fori_loop(0, (dil - 1) // _A_UNROLL, body2, 0)

    chunk = T_TILE
    for c0 in range(0, seq, chunk):
        rows = slice(c0, c0 + chunk)
        lse = [ln_ref[g, rows, :] for g in range(len(A_GROUPS))]
        top = functools.reduce(jnp.maximum, lse)
        ex = [jnp.exp(v - top) for v in lse]
        den = functools.reduce(lambda a, b: a + b, ex)
        for g in range(len(A_GROUPS)):
            o_ref[0, rows, g * gw:(g + 1) * gw] = (ex[g] / den * on_ref[g, rows, :]).astype(BF16)


def _mixer_a(pa0, pa1, pa2, bias):
    b, seq, _ = pa0.shape
    return pl.pallas_call(
        functools.partial(_mixer_a_kernel, seq=seq),
        out_shape=jax.ShapeDtypeStruct((b, seq, _GRP_COLS), BF16),
        grid=(b,),
        in_specs=[pl.BlockSpec((1, seq, _GRP_COLS), lambda i: (i, 0, 0)),
                  pl.BlockSpec((1,) + pa1.shape[1:], lambda i: (i, 0, 0, 0)),
                  pl.BlockSpec((1,) + pa2.shape[1:], lambda i: (i, 0, 0)),
                  pl.BlockSpec(bias.shape, lambda i: (0, 0, 0))],
        out_specs=pl.BlockSpec((1, seq, _GRP_COLS), lambda i: (i, 0, 0)),
        scratch_shapes=[pltpu.VMEM((len(A_GROUPS), seq, LANE), F32),
                        pltpu.VMEM((len(A_GROUPS), seq, LANE), F32)],
        compiler_params=_params(1),
        name="mixer_a",
    )(pa0, pa1, pa2, bias)


def _compress_kernel(r_ref, pek_ref, pev_ref, w1k_ref, w1v_ref, w2k_ref, w2v_ref, o_ref):
    half = NSA_CMP_STRIDE * HEAD_DIM
    for kv, (pe_ref, w1_ref, w2_ref) in enumerate(((pek_ref, w1k_ref, w2k_ref),
                                                   (pev_ref, w1v_ref, w2v_ref))):
        for h in range(B_KV_HEADS):
            r = r_ref[kv * B_KV_HEADS + h, 0].astype(F32)
            lo = jnp.dot((r + pe_ref[:, :half]).astype(BF16), w1_ref[:half, :],
                         preferred_element_type=F32)
            hi = jnp.dot((r + pe_ref[:, half:]).astype(BF16), w1_ref[half:, :],
                         preferred_element_type=F32)
            hid = lo + pltpu.roll(hi, hi.shape[0] - 1, axis=0)
            act = jax.nn.gelu(hid).astype(BF16)
            o_ref[0, kv * B_KV_HEADS + h] = jnp.dot(act, w2_ref[...],
                                                    preferred_element_type=F32).astype(BF16)


def _compress(r, pe_k, pe_v, w1_k, w1_v, w2_k, w2_v):
    _, b, m, c = r.shape
    const2 = lambda i: (0, 0)
    return pl.pallas_call(
        _compress_kernel,
        out_shape=jax.ShapeDtypeStruct((b, 4, m, HEAD_DIM), BF16),
        grid=(b,),
        in_specs=[pl.BlockSpec((4, 1, m, c), lambda i: (0, i, 0, 0)),
                  pl.BlockSpec(pe_k.shape, const2), pl.BlockSpec(pe_v.shape, const2),
                  pl.BlockSpec(w1_k.shape, const2), pl.BlockSpec(w1_v.shape, const2),
                  pl.BlockSpec(w2_k.shape, const2), pl.BlockSpec(w2_v.shape, const2)],
        out_specs=pl.BlockSpec((1, 4, m, HEAD_DIM), lambda i: (i, 0, 0, 0)),
        compiler_params=_params(1),
        name="nsa_compress",
    )(r, pe_k, pe_v, w1_k, w1_v, w2_k, w2_v)


MASK_BIG = 2.0 ** 100
SEL_PAD = 32


def _nt(a, b, **kw):
    return lax.dot_general(a, b, (((1,), (1,)), ((), ())), preferred_element_type=F32, **kw)


def _top_mask_t(score, n_rows, n_top):
    sub = lax.broadcasted_iota(jnp.int32, score.shape, 0)
    rank = jnp.zeros(score.shape, F32)
    for j in range(n_rows):
        row = score[j:j + 1, :]
        ge = jnp.where(row >= score, 1.0, 0.0)
        gt = jnp.where(row > score, 1.0, 0.0)
        rank = rank + jnp.where(sub > j, ge, gt)
    return rank < n_top


def _top_keep(score, n_rows, n_top, block):
    free = n_top * block
    ranked = jnp.where(_top_mask_t(score[:, free:], n_rows, n_top), 1.0, 0.0)
    return jnp.concatenate([jnp.ones((score.shape[0], free), F32), ranked], axis=1)


BAND_SLOTS = 4


def _loop_pairs(lo, hi, body):
    n = jnp.maximum(hi - lo, 0)

    def pair(p, c):
        body(lo + 2 * p)
        body(lo + 2 * p + 1)
        return c
    lax.fori_loop(0, n // 2, pair, 0)
    pl.when(n % 2 == 1)(lambda: body(hi - 1))


def _two_pass_attention(i, streams, band_streams=()):
    groups = ((streams, lambda j: j), (band_streams, lambda j: j % BAND_SLOTS))

    def key_rows(j):
        return pl.ds(pl.multiple_of(j * T_TILE, T_TILE), T_TILE)

    def logits(group, j, d):
        some, slot = group
        for k_tile, q_t, _, bias_tile, s_ref, rmp_ref, _ in some:
            s = jnp.dot(k_tile(key_rows(j)), q_t(), preferred_element_type=F32)
            if d is not None:
                s = s + bias_tile(d)
            s_ref[slot(j)] = s
            fold = jnp.max(s.reshape(s.shape[0] // 8, 8, s.shape[1]), axis=0)
            rmp_ref[...] = fold if d == 0 else jnp.maximum(rmp_ref[...], fold)

    def near(n_full, n_band):
        for d in range(n_full):
            logits(groups[0], i - d, d)
        for d in range(n_band):
            logits(groups[1], i - d, d)

    if band_streams:
        pl.when(i == 0)(lambda: near(1, 1))
        pl.when(i == 1)(lambda: near(2, 2))
        pl.when(i >= 2)(lambda: near(2, 3))
    else:
        logits(groups[0], i, 0)
        pl.when(i >= 1)(lambda: logits(groups[0], i - 1, 1))

    _loop_pairs(0, i - 1, lambda j: logits(groups[0], j, None))

    tops = {}
    for some, _ in groups:
        for stream in some:
            rmp_ref, acc_ref = stream[-2:]
            tops[id(rmp_ref)] = jnp.max(rmp_ref[...], axis=0, keepdims=True)
            acc_ref[...] = jnp.zeros(acc_ref.shape, F32)

    def pv(active):
        def body(j):
            for some, slot in active:
                for _, _, vt_ref, _, s_ref, rmp_ref, acc_ref in some:
                    e = jnp.exp(s_ref[slot(j)] - tops[id(rmp_ref)])
                    acc_ref[...] += jnp.dot(vt_ref[j], e.astype(BF16), preferred_element_type=F32)
        return body

    if band_streams:
        band_lo = jnp.maximum(i - 2, 0)
        _loop_pairs(0, band_lo, pv(groups[:1]))
        pv_all = pv(groups)

        def last_tiles(n):
            for d in reversed(range(n)):
                pv_all(i - d)

        pl.when(i == 0)(lambda: last_tiles(1))
        pl.when(i == 1)(lambda: last_tiles(2))
        pl.when(i >= 2)(lambda: last_tiles(3))
    else:
        _loop_pairs(0, i + 1, pv(groups[:1]))

    outs = []
    for some, _ in groups:
        for *_, acc_ref in some:
            acc = acc_ref[...]
            outs.append(acc[:HEAD_DIM, :] / acc[HEAD_DIM:HEAD_DIM + 1, :])
    return outs


def _eye(n, dtype=BF16):
    return (lax.broadcasted_iota(jnp.int32, (n, n), 0)
            == lax.broadcasted_iota(jnp.int32, (n, n), 1)).astype(dtype)


def _transposed_values(v_ref_rows, n_tiles):
    rows = lax.broadcasted_iota(jnp.int32, (LANE, HEAD_DIM), 0)
    cols = lax.broadcasted_iota(jnp.int32, (LANE, HEAD_DIM), 1)
    place = (rows == cols).astype(BF16)
    ones_row = (lax.broadcasted_iota(jnp.int32, (LANE, T_TILE), 0) == HEAD_DIM).astype(F32)
    for j in range(n_tiles):
        v = v_ref_rows(slice(j * T_TILE, (j + 1) * T_TILE))
        yield j, (_nt(place, v) + ones_row).astype(BF16)


def _with_block_onehot(k, block):
    n = k.shape[0]
    rows = lax.broadcasted_iota(jnp.int32, (HEAD_DIM, LANE), 0)
    cols = lax.broadcasted_iota(jnp.int32, (HEAD_DIM, LANE), 1)
    place = (rows == cols).astype(BF16)
    lane = lax.broadcasted_iota(jnp.int32, (n, LANE), 1)
    hot = (lane - HEAD_DIM == lax.broadcasted_iota(jnp.int32, (n, LANE), 0) // block).astype(F32)
    return (jnp.dot(k, place, preferred_element_type=F32) + hot).astype(BF16)


def _augment_q_t(q_t, allowed_t):
    pen = (allowed_t - 1.0) * MASK_BIG
    return jnp.concatenate([q_t, pen, jnp.zeros(pen.shape, F32)], axis=0).astype(BF16)


_N_SLC = 32
_G_ROWS = B_GROUP * T_TILE


def _nsa_kernel(pb_ref, cmp_ref, bg_ref, bias_ref, o_ref,
                kaug_ref, vst_ref, vwt_ref, eye_ref, sig_ref, q3t_ref, qaugt_ref, s_ref, sw_ref,
                rmp_ref, acc_ref, ocmp_ref, *, seq):
    n_tiles = seq // T_TILE
    n_cmp_rows = seq // NSA_CMP_STRIDE
    q_cols = B_HEADS * HEAD_DIM
    ks0, vs0, kw0, vw0 = (q_cols + i * B_KV_HEADS * HEAD_DIM for i in range(4))

    def head_cols(c0, h):
        return slice(c0 + h * HEAD_DIM, c0 + (h + 1) * HEAD_DIM)

    j_id = lax.broadcasted_iota(jnp.int32, (_N_SLC, n_cmp_rows), 0)
    c_id = lax.broadcasted_iota(jnp.int32, (_N_SLC, n_cmp_rows), 1)
    overlap_t = ((c_id * NSA_CMP_STRIDE < (j_id + 1) * NSA_SLC_BLOCK)
                 & (c_id * NSA_CMP_STRIDE + NSA_CMP_BLOCK > j_id * NSA_SLC_BLOCK)).astype(F32)
    eye_ref[...] = _eye(T_TILE)
    for h in range(B_KV_HEADS):
        kaug_ref[h] = _with_block_onehot(pb_ref[0, :, head_cols(ks0, h)], NSA_SLC_BLOCK)
        for j, vt in _transposed_values(lambda r, h=h: pb_ref[0, r, head_cols(vs0, h)], n_tiles):
            vst_ref[h, j] = vt
        for j, vt in _transposed_values(lambda r, h=h: pb_ref[0, r, head_cols(vw0, h)], n_tiles):
            vwt_ref[h, j] = vt

    def tiles(a):
        return [a[:, j * T_TILE:(j + 1) * T_TILE] for j in range(n_tiles)]

    pick = (lax.broadcasted_iota(jnp.int32, (SEL_PAD, _BG_COLS), 0)
            == lax.broadcasted_iota(jnp.int32, (SEL_PAD, _BG_COLS), 1)).astype(F32)
    sig_all = jax.nn.sigmoid(_nt(pick, bg_ref[0], precision=lax.Precision.HIGHEST))
    for j, t in enumerate(tiles(sig_all)):
        sig_ref[j] = t

    eye_h = _eye(HEAD_DIM)
    t_lane = lax.broadcasted_iota(jnp.int32, (n_cmp_rows, seq), 1)
    c_end = (lax.broadcasted_iota(jnp.int32, (n_cmp_rows, seq), 0) * NSA_CMP_STRIDE
             + NSA_CMP_BLOCK - 1)
    ok = (c_end <= t_lane) & (c_end < seq)
    ok_f = ok.astype(F32)
    j_sub = lax.broadcasted_iota(jnp.int32, (_N_SLC, seq), 0)
    jt = lax.broadcasted_iota(jnp.int32, (_N_SLC, seq), 1) // NSA_SLC_BLOCK
    forced = ((j_sub == 0) | (j_sub == jt) | (j_sub == jt - 1)).astype(F32)
    valid = j_sub <= jt

    for h in range(B_KV_HEADS):
        kc = cmp_ref[0, h]
        vct = _nt(eye_h, cmp_ref[0, B_KV_HEADS + h]).astype(BF16)
        q_ts, p_sum = [], None
        for g in range(B_GROUP):
            q_t = _nt(eye_h, pb_ref[0, :, head_cols(0, h * B_GROUP + g)]) * ATTN_SCALE
            q_ts.append(q_t)
            lc = jnp.where(ok, jnp.dot(kc, q_t.astype(BF16), preferred_element_type=F32), NEG_INF)
            ec = jnp.exp(lc - jnp.max(lc, axis=0, keepdims=True))
            pc = ec / jnp.sum(ec, axis=0, keepdims=True) * ok_f
            o_cmp = jnp.dot(vct, pc.astype(BF16), preferred_element_type=F32)
            for j, t in enumerate(tiles(o_cmp)):
                ocmp_ref[h, j, :, g * T_TILE:(g + 1) * T_TILE] = t
            p_sum = pc if p_sum is None else p_sum + pc
        imp_t = jnp.dot(overlap_t, p_sum, precision=lax.Precision.HIGHEST,
                        preferred_element_type=F32)
        score = jnp.where(valid, imp_t + FORCE_BONUS * forced, NEG_INF)
        sel_t = jnp.where(valid, _top_keep(score, _N_SLC, NSA_SLC_TOPK, NSA_SLC_BLOCK), 0.0)
        pen_tiles = tiles((sel_t - 1.0) * MASK_BIG)
        q_tiles = [tiles(q_t) for q_t in q_ts]
        for j in range(n_tiles):
            q3t = jnp.concatenate([q_tiles[g][j] for g in range(B_GROUP)], axis=1)
            pen = jnp.concatenate([pen_tiles[j]] * B_GROUP, axis=1)
            q3t_ref[h, j] = q3t.astype(BF16)
            qaugt_ref[h, j] = jnp.concatenate([q3t, pen, jnp.zeros(pen.shape, F32)],
                                              axis=0).astype(BF16)

    def tile_body(i, carry):
        rows = pl.ds(pl.multiple_of(i * T_TILE, T_TILE), T_TILE)

        outs = _two_pass_attention(
            i,
            [(lambda kr, h=h: kaug_ref[h, kr, :], lambda h=h: qaugt_ref[h, i], vst_ref.at[h],
              lambda d, h=h: bias_ref[h, d], s_ref.at[h], rmp_ref.at[0, h], acc_ref.at[0, h])
             for h in range(B_KV_HEADS)],
            [(lambda kr, h=h: pb_ref[0, kr, head_cols(kw0, h)], lambda h=h: q3t_ref[h, i],
              vwt_ref.at[h], lambda d, h=h: bias_ref[h, d], sw_ref.at[h], rmp_ref.at[1, h],
              acc_ref.at[1, h]) for h in range(B_KV_HEADS)])
        oslc, owin = outs[:B_KV_HEADS], outs[B_KV_HEADS:]

        sig_t = sig_ref[i]
        heads = []
        for h in range(B_KV_HEADS):
            for g in range(B_GROUP):
                head = h * B_GROUP + g
                gc = slice(g * T_TILE, (g + 1) * T_TILE)
                heads.append(sig_t[3 * head:3 * head + 1, :] * ocmp_ref[h, i, :, gc]
                             + sig_t[3 * head + 1:3 * head + 2, :] * oslc[h][:, gc]
                             + sig_t[3 * head + 2:3 * head + 3, :] * owin[h][:, gc])
        o_t = jnp.concatenate(heads, axis=0).astype(BF16)
        o_ref[0, rows, :] = _nt(eye_ref[...], o_t).astype(o_ref.dtype)
        return carry

    lax.fori_loop(0, n_tiles, tile_body, 0)


def _nsa(pb, cmp, bg, bias):
    b, seq, _ = pb.shape
    n_tiles = seq // T_TILE
    return pl.pallas_call(
        functools.partial(_nsa_kernel, seq=seq),
        out_shape=jax.ShapeDtypeStruct((b, seq, B_HEADS * HEAD_DIM), BF16),
        grid=(b,),
        in_specs=[pl.BlockSpec((1, seq, _PB_COLS), lambda i: (i, 0, 0)),
                  pl.BlockSpec((1,) + cmp.shape[1:], lambda i: (i, 0, 0, 0)),
                  pl.BlockSpec((1, seq, _BG_COLS), lambda i: (i, 0, 0)),
                  pl.BlockSpec(bias.shape, lambda i: (0, 0, 0, 0))],
        out_specs=pl.BlockSpec((1, seq, B_HEADS * HEAD_DIM), lambda i: (i, 0, 0)),
        scratch_shapes=[pltpu.VMEM((B_KV_HEADS, seq, LANE), BF16),
                        pltpu.VMEM((B_KV_HEADS, n_tiles, LANE, T_TILE), BF16),
                        pltpu.VMEM((B_KV_HEADS, n_tiles, LANE, T_TILE), BF16),
                        pltpu.VMEM((T_TILE, T_TILE), BF16),
                        pltpu.VMEM((n_tiles, SEL_PAD, T_TILE), F32),
                        pltpu.VMEM((B_KV_HEADS, n_tiles, HEAD_DIM, _G_ROWS), BF16),
                        pltpu.VMEM((B_KV_HEADS, n_tiles, LANE, _G_ROWS), BF16),
                        pltpu.VMEM((B_KV_HEADS, n_tiles, T_TILE, _G_ROWS), F32),
                        pltpu.VMEM((B_KV_HEADS, BAND_SLOTS, T_TILE, _G_ROWS), F32),
                        pltpu.VMEM((2, B_KV_HEADS, 8, _G_ROWS), F32),
                        pltpu.VMEM((2, B_KV_HEADS, LANE, _G_ROWS), F32),
                        pltpu.VMEM((B_KV_HEADS, n_tiles, HEAD_DIM, _G_ROWS), F32)],
        compiler_params=_params(1),
        name="nsa",
    )(pb, cmp, bg, bias)


def _moba_kernel(pc_ref, bias_ref, o_ref, kaug_ref, vt_ref, eye_ref, qaugt_ref, s_ref,
                 rmp_ref, acc_ref, *, seq):
    n_blk = seq // MOBA_BLOCK
    width = C_HEADS * HEAD_DIM
    eye_ref[...] = _eye(T_TILE)

    def cols(j, h):
        return slice(j * width + h * HEAD_DIM, j * width + (h + 1) * HEAD_DIM)

    n_sub = lax.broadcasted_iota(jnp.int32, (SEL_PAD, seq), 0)
    own = lax.broadcasted_iota(jnp.int32, (SEL_PAD, seq), 1) // MOBA_BLOCK
    past = n_sub < own
    eye_h = _eye(HEAD_DIM)
    for h in range(C_HEADS):
        kaug_ref[h] = _with_block_onehot(pc_ref[0, :, cols(1, h)], MOBA_BLOCK)
        for j, vt in _transposed_values(lambda r, h=h: pc_ref[0, r, cols(2, h)], n_blk):
            vt_ref[h, j] = vt
        kmean = jnp.concatenate(
            [jnp.mean(pc_ref[0, n * MOBA_BLOCK:(n + 1) * MOBA_BLOCK, cols(1, h)].astype(F32), axis=0,
                      keepdims=True) for n in range(n_blk)]
            + [jnp.zeros((SEL_PAD - n_blk, HEAD_DIM), F32)], axis=0)
        q_t = _nt(eye_h, pc_ref[0, :, cols(0, h)])
        gate_t = jnp.dot(kmean, q_t, precision=lax.Precision.HIGHEST,
                         preferred_element_type=F32)
        score = jnp.where(past, gate_t, NEG_INF)
        picked = past & (_top_keep(score, n_blk, MOBA_TOPK, MOBA_BLOCK) > 0.5)
        allowed_t = jnp.where(picked | (n_sub == own), 1.0, 0.0)
        qaug_t = _augment_q_t(q_t * ATTN_SCALE, allowed_t)
        for j in range(n_blk):
            qaugt_ref[h, j] = qaug_t[:, j * T_TILE:(j + 1) * T_TILE]

    def tile_body(i, carry):
        rows = pl.ds(pl.multiple_of(i * MOBA_BLOCK, MOBA_BLOCK), MOBA_BLOCK)
        outs = _two_pass_attention(
            i, [(lambda kr, h=h: kaug_ref[h, kr, :], lambda h=h: qaugt_ref[h, i], vt_ref.at[h],
                 lambda d, h=h: bias_ref[h, d], s_ref.at[h], rmp_ref.at[h], acc_ref.at[h])
                for h in range(C_HEADS)])
        o_t = jnp.concatenate(outs, axis=0).astype(BF16)
        o_ref[0, rows, :] = _nt(eye_ref[...], o_t).astype(o_ref.dtype)
        return carry

    lax.fori_loop(0, n_blk, tile_body, 0)


def _moba(pc, bias):
    b, seq, _ = pc.shape
    width = C_HEADS * HEAD_DIM
    n_tiles = seq // T_TILE
    return pl.pallas_call(
        functools.partial(_moba_kernel, seq=seq),
        out_shape=jax.ShapeDtypeStruct((b, seq, width), BF16),
        grid=(b,),
        in_specs=[pl.BlockSpec((1, seq, _C_COLS), lambda i: (i, 0, 0)),
                  pl.BlockSpec(bias.shape, lambda i: (0, 0, 0, 0))],
        out_specs=pl.BlockSpec((1, seq, width), lambda i: (i, 0, 0)),
        scratch_shapes=[pltpu.VMEM((C_HEADS, seq, LANE), BF16),
                        pltpu.VMEM((C_HEADS, n_tiles, LANE, T_TILE), BF16),
                        pltpu.VMEM((T_TILE, T_TILE), BF16),
                        pltpu.VMEM((C_HEADS, n_tiles, LANE, T_TILE), BF16),
                        pltpu.VMEM((C_HEADS, n_tiles, T_TILE, T_TILE), F32),
                        pltpu.VMEM((C_HEADS, 8, T_TILE), F32),
                        pltpu.VMEM((C_HEADS, LANE, T_TILE), F32)],
        compiler_params=_params(1),
        name="moba",
    )(pc, bias)


def _merge_kernel(oa_ref, ob_ref, oc_ref, mg_ref, x_ref, wb_ref, wo_ref, y_ref):
    r0 = A_HEADS * HEAD_DIM
    r1 = r0 + B_HEADS * HEAD_DIM
    ya = jnp.dot(oa_ref[...], wb_ref[0:r0, :], preferred_element_type=F32)
    yb = jnp.dot(ob_ref[...], wb_ref[r0:r1, :], preferred_element_type=F32)
    yc = jnp.dot(oc_ref[...], wb_ref[r1:, :], preferred_element_type=F32)
    merged = (jax.nn.sigmoid(mg_ref[:, 0:D_MODEL].astype(F32)) * ya
              + jax.nn.sigmoid(mg_ref[:, D_MODEL:2 * D_MODEL].astype(F32)) * yb
              + jax.nn.sigmoid(mg_ref[:, 2 * D_MODEL:].astype(F32)) * yc)
    y_ref[...] = x_ref[...] + jnp.dot(merged.astype(BF16), wo_ref[...], preferred_element_type=F32)


def _merge(oa, ob, oc, mg, x2, wb, wo, *, layer):
    t = x2.shape[0]
    tm = ROW_TILE
    row = lambda i: (i, 0)
    weight = pl.BlockSpec((None,) + wb.shape[1:], lambda i: (layer, 0, 0))
    return pl.pallas_call(
        _merge_kernel,
        out_shape=jax.ShapeDtypeStruct((t, D_MODEL), F32),
        grid=(t // tm,),
        in_specs=[pl.BlockSpec((tm, oa.shape[1]), row),
                  pl.BlockSpec((tm, ob.shape[1]), row), pl.BlockSpec((tm, oc.shape[1]), row),
                  pl.BlockSpec((tm, _MG_COLS), row), pl.BlockSpec((tm, D_MODEL), row),
                  weight, weight],
        out_specs=pl.BlockSpec((tm, D_MODEL), row),
        compiler_params=_params(1),
        name="merge",
    )(oa, ob, oc, mg, x2, wb, wo)


MXU_WIDTH = 256
_FF_CHUNK_TILES = 6
_FF_BOUNDS = tuple(range(0, D_FF, _FF_CHUNK_TILES * MXU_WIDTH)) + (D_FF,)
_FF_CHUNKS = tuple(zip(_FF_BOUNDS[:-1], _FF_BOUNDS[1:]))


def _ffn_kernel(x_ref, halo_ref, g_ref, wu_ref, cw_ref, cb_ref, wd_ref, gf_ref, y_ref, xn_ref,
                *, tiles_per_seq, final_norm):
    i = pl.program_id(0)
    x = x_ref[...]
    g = g_ref[...]
    keep = (i % tiles_per_seq != 0).astype(F32)
    xn_ref[0:FFN_HALO, :] = (_rmsnorm_rows(halo_ref[...], g) * keep).astype(BF16)
    xn_ref[FFN_HALO:, :] = _rmsnorm_rows(x, g).astype(BF16)
    xn = xn_ref[...]
    rows = xn.shape[0]

    def conv(c0, c1):
        hcol = jnp.dot(xn, wu_ref[:, c0:c1], preferred_element_type=F32)
        out = (cw_ref[2:3, c0:c1] * hcol
               + cw_ref[1:2, c0:c1] * pltpu.roll(hcol, 1, axis=0)
               + cw_ref[0:1, c0:c1] * pltpu.roll(hcol, 2, axis=0)
               + cb_ref[:, c0:c1])
        return out[FFN_HALO:rows]

    acc = x
    for c0, c1 in _FF_CHUNKS:
        a = conv(c0, c1)
        u = conv(D_FF + c0, D_FF + c1)
        act = (a * jax.nn.sigmoid(a) * u).astype(BF16)
        acc = acc + jnp.dot(act, wd_ref[c0:c1, :], preferred_element_type=F32)
    if final_norm:
        acc = _rmsnorm_rows(acc, gf_ref[...])
    y_ref[...] = acc


def _ffn(x2, g, wu, cw, cb, wd, gf, *, layer, seq, final_norm):
    t = x2.shape[0]
    tm = ROW_TILE
    const = lambda i: (0, 0)
    per_layer = lambda w: pl.BlockSpec((None,) + w.shape[1:], lambda i: (layer, 0, 0))
    halo_blocks = tm // FFN_HALO
    return pl.pallas_call(
        functools.partial(_ffn_kernel, tiles_per_seq=seq // tm, final_norm=final_norm),
        out_shape=jax.ShapeDtypeStruct((t, D_MODEL), F32),
        grid=(t // tm,),
        in_specs=[pl.BlockSpec((tm, D_MODEL), lambda i: (i, 0)),
                  pl.BlockSpec((FFN_HALO, D_MODEL), lambda i: (jnp.maximum(i * halo_blocks - 1, 0), 0)),
                  pl.BlockSpec((1, D_MODEL), const),
                  per_layer(wu), pl.BlockSpec(cw.shape, const),
                  pl.BlockSpec(cb.shape, const), per_layer(wd),
                  pl.BlockSpec((1, D_MODEL), const)],
        out_specs=pl.BlockSpec((tm, D_MODEL), lambda i: (i, 0)),
        scratch_shapes=[pltpu.VMEM((tm + FFN_HALO, D_MODEL), BF16)],
        compiler_params=_params(1),
        name="conv_ffn",
    )(x2, x2, g, wu, cw, cb, wd, gf)


def kernel(x, rel_bias, norm_mix, w_in, cmp_pe_k, cmp_w1_k, cmp_w2_k, cmp_pe_v, cmp_w1_v, cmp_w2_v,
           w_branch, w_out, norm_ffn, w_up, conv_w, conv_b, w_down, norm_final):
    b, s, d = x.shape
    depth = w_in.shape[0]
    t = b * s
    assert d == D_MODEL and s % T_TILE == 0 and s // NSA_SLC_BLOCK == _N_SLC and t % ROW_TILE == 0
    assert s % ROW_TILE == 0 and all(s % (dil * A_TILE) == 0 for _, dil in A_GROUPS)
    assert all(win // dil == A_TILE for win, dil in A_GROUPS) and A_GROUPS[0][1] == 1
    assert all(ROW_TILE % dil == 0 for _, dil in A_GROUPS)

    hg = A_HEADS_PER_GROUP
    bias_a = jnp.concatenate([
        _bias_tiles(rel_bias,
                    np.concatenate([_bucket_tile(A_TILE, 1, dil, win // dil),
                                    _bucket_tile(A_TILE, 0, dil, win // dil)], axis=1),
                    gi * hg, hg)
        for gi, (win, dil) in enumerate(A_GROUPS)]
        + [_bias_tiles(rel_bias,
                       np.concatenate([np.full((A_TILE, A_TILE), -1, np.int32),
                                       _bucket_tile(A_TILE, 0, A_GROUPS[-1][1], A_TILE)], axis=1),
                       (len(A_GROUPS) - 1) * hg, hg)]
        + [_bias_tiles(rel_bias,
                       np.concatenate([_bucket_tile(A_TILE, 0, dil, win // dil),
                                       np.full((A_TILE, A_TILE), -1, np.int32)], axis=1),
                       gi * hg, hg)
           for gi, (win, dil) in enumerate(A_GROUPS)], axis=0)
    assert (_bucket_tile(T_TILE, 2, 1, 3 * T_TILE) == REL_BUCKETS - 1).all()
    bias_b = _bias_tiles_t(rel_bias, [_bucket_tile(T_TILE, d, 1, NSA_WINDOW - 1) for d in (0, 1, 2)],
                           A_HEADS, B_HEADS)
    bias_b = bias_b.reshape(B_KV_HEADS, B_GROUP, 3, T_TILE, T_TILE).transpose(0, 2, 3, 1, 4).reshape(
        B_KV_HEADS, 3, T_TILE, _G_ROWS)
    bias_c = _bias_tiles_t(rel_bias, [_bucket_tile(T_TILE, d, 1, s) for d in (0, 1)],
                           A_HEADS + B_HEADS, C_HEADS)

    w_in_k = _prep_w_in(w_in)
    w_branch_k, w_out_k, w_up_k, w_down_k = (w.astype(BF16) for w in (w_branch, w_out, w_up, w_down))
    x2 = x.reshape(t, d)
    for i in range(depth):
        pa0, pa1, pa2, pb, pcmp, pc, mg, bg = _proj_in(x2, norm_mix[i][None, :], w_in_k, layer=i,
                                                       seq=s)

        oa = _mixer_a(pa0.reshape(b, s, _GRP_COLS), pa1, pa2.reshape(b, s, _GRP_COLS), bias_a)

        cmp = _compress(pcmp.reshape(4, b, s // NSA_CMP_STRIDE, NSA_CMP_STRIDE * HEAD_DIM),
                        cmp_pe_k[i].reshape(1, -1), cmp_pe_v[i].reshape(1, -1),
                        cmp_w1_k[i].astype(BF16), cmp_w1_v[i].astype(BF16),
                        cmp_w2_k[i].astype(BF16), cmp_w2_v[i].astype(BF16))
        ob = _nsa(pb.reshape(b, s, _PB_COLS), cmp, bg.reshape(b, s, _BG_COLS), bias_b)

        oc = _moba(pc.reshape(b, s, _C_COLS), bias_c)

        x2 = _merge(oa.reshape(t, -1), ob.reshape(t, -1), oc.reshape(t, -1), mg, x2,
                    w_branch_k, w_out_k, layer=i)
        x2 = _ffn(x2, norm_ffn[i][None, :], w_up_k, conv_w[i], conv_b[i][None, :], w_down_k,
                  norm_final[None, :], layer=i, seq=s, final_norm=(i == depth - 1))
    return x2.reshape(b, s, d)
```

```python
import functools
import math

import jax
import jax.numpy as jnp
import numpy as np
from jax import lax
from jax.experimental import pallas as pl
from jax.experimental.pallas import tpu as pltpu

F32 = jnp.float32
BF16 = jnp.bfloat16

D_MODEL = 1024
HEAD_DIM = 64
A_GROUPS = ((128, 1), (512, 4), (2048, 16))
A_HEADS_PER_GROUP = 2
A_HEADS = 6
B_HEADS = 6
B_KV_HEADS = 2
B_GROUP = 3
C_HEADS = 4
NSA_CMP_BLOCK = 32
NSA_CMP_STRIDE = 16
NSA_SLC_BLOCK = 64
NSA_SLC_TOPK = 16
NSA_WINDOW = 512
MOBA_BLOCK = 256
MOBA_TOPK = 3
REL_BUCKETS = 32
REL_MAX_DIST = 128
D_FF = 2816
RMS_EPS = 1e-6
NEG_INF = -1e30
FORCE_BONUS = 1e4
ATTN_SCALE = HEAD_DIM ** -0.5

LANE = 128
A_TILE = 128
T_TILE = 256
ROW_TILE = 512
FFN_HALO = 8
VMEM_LIMIT = 56 * 1024 * 1024

_GRP_COLS = 3 * A_HEADS_PER_GROUP * HEAD_DIM
_A_COLS = len(A_GROUPS) * _GRP_COLS
_PB_COLS = B_HEADS * HEAD_DIM + 4 * B_KV_HEADS * HEAD_DIM
_CMP_COLS = 2 * B_KV_HEADS * HEAD_DIM
_C_COLS = 3 * C_HEADS * HEAD_DIM
_MG_COLS = 3 * D_MODEL
_BG_COLS = LANE
_OFF_A = 0
_OFF_PB = _OFF_A + _A_COLS
_OFF_CMP = _OFF_PB + _PB_COLS
_OFF_C = _OFF_CMP + _CMP_COLS
_OFF_MG = _OFF_C + _C_COLS
_OFF_BG = _OFF_MG + _MG_COLS
_W_COLS = _OFF_BG + _BG_COLS
_PROJ_CHUNK = 4 * LANE


def _params(n_grid):
    return pltpu.CompilerParams(dimension_semantics=("arbitrary",) * n_grid,
                                vmem_limit_bytes=VMEM_LIMIT)


def _rel_bucket_np(dist):
    n = np.maximum(dist, 0)
    exact = REL_BUCKETS // 2
    nf = np.maximum(n, 1).astype(np.float32)
    large = exact + (np.log(nf / np.float32(exact)) / np.float32(math.log(REL_MAX_DIST / exact))
                     * np.float32(REL_BUCKETS - exact)).astype(np.int32)
    return np.where(n < exact, n, np.minimum(large, REL_BUCKETS - 1)).astype(np.int32)


def _bucket_tile(tile, block_offset, dil, max_rel):
    rel = block_offset * tile + np.arange(tile)[:, None] - np.arange(tile)[None, :]
    ok = (rel >= 0) & (rel <= max_rel)
    return np.where(ok, _rel_bucket_np(rel * dil), -1).astype(np.int32)


def _bias_tiles_kernel(tbl_ref, idx_ref, o_ref, *, head0, relative):
    h = pl.program_id(0) + head0
    idx = idx_ref[...]
    acc = jnp.full(idx.shape, NEG_INF, F32)
    for b in range(REL_BUCKETS):
        acc = jnp.where(idx == b, tbl_ref[b, h], acc)
    if relative:
        acc = acc - tbl_ref[REL_BUCKETS - 1, h]
    o_ref[0] = acc


def _bias_tiles(rel_bias, idx, head0, n_heads, relative=False):
    r, c = idx.shape
    return pl.pallas_call(
        functools.partial(_bias_tiles_kernel, head0=head0, relative=relative),
        out_shape=jax.ShapeDtypeStruct((n_heads, r, c), F32),
        grid=(n_heads,),
        in_specs=[pl.BlockSpec(memory_space=pltpu.SMEM),
                  pl.BlockSpec((r, c), lambda h: (0, 0))],
        out_specs=pl.BlockSpec((1, r, c), lambda h: (h, 0, 0)),
        compiler_params=_params(1),
        name="bias_tiles",
    )(rel_bias, jnp.asarray(idx))


def _bias_tiles_t(rel_bias, idx_tiles, head0, n_heads):
    r, c = idx_tiles[0].shape
    flat = _bias_tiles(rel_bias, np.concatenate([t.T for t in idx_tiles], axis=1), head0, n_heads,
                       relative=True)
    return flat.reshape(n_heads, c, len(idx_tiles), r).transpose(0, 2, 1, 3)


def _rmsnorm_rows(x, g):
    return x * lax.rsqrt(jnp.mean(x * x, axis=-1, keepdims=True) + RMS_EPS) * g


def _proj_in_kernel(x_ref, g_ref, w_ref, pa0_ref, pa1_ref, pa2_ref, pb_ref, pcmp_ref, pc_ref,
                    mg_ref, bg_ref, regroup_ref):
    xn = _rmsnorm_rows(x_ref[...], g_ref[...]).astype(BF16)
    tm = xn.shape[0]

    def direct(out_ref):
        def sink(lo, hi, val):
            out_ref[:, lo:hi] = val.astype(out_ref.dtype)
        return sink

    def staged(slot0):
        def sink(lo, hi, val):
            for c in range(lo, hi, LANE):
                regroup_ref[slot0 + c // LANE] = val[:, c - lo:c - lo + LANE]
        return sink

    g1_slot, g2_slot, cmp_slot = 0, 3, 6
    sections = ((_OFF_A, _GRP_COLS, direct(pa0_ref)),
                (_OFF_A + _GRP_COLS, _GRP_COLS, staged(g1_slot)),
                (_OFF_A + 2 * _GRP_COLS, _GRP_COLS, staged(g2_slot)),
                (_OFF_PB, _PB_COLS, direct(pb_ref)), (_OFF_CMP, _CMP_COLS, staged(cmp_slot)),
                (_OFF_C, _C_COLS, direct(pc_ref)), (_OFF_MG, _MG_COLS, direct(mg_ref)),
                (_OFF_BG, _BG_COLS, direct(bg_ref)))
    for c0 in range(0, _W_COLS, _PROJ_CHUNK):
        c1 = min(c0 + _PROJ_CHUNK, _W_COLS)
        res = jnp.dot(xn, w_ref[:, c0:c1], preferred_element_type=F32)
        for off, width, sink in sections:
            lo, hi = max(c0, off), min(c1, off + width)
            if lo < hi:
                sink(lo - off, hi - off, res[:, lo - c0:hi - c0])

    for gi, slot0, out_ref in ((1, g1_slot, pa1_ref), (2, g2_slot, pa2_ref)):
        dil = A_GROUPS[gi][1]
        for r in range(dil):
            for j in range(3):
                out_ref[0, r, :, j * LANE:(j + 1) * LANE] = (
                    regroup_ref[slot0 + j, pl.ds(r, tm // dil, stride=dil), :].astype(BF16))
    for l in range(NSA_CMP_STRIDE):
        for kv in range(2):
            a = regroup_ref[cmp_slot + kv, pl.ds(l, tm // NSA_CMP_STRIDE, stride=NSA_CMP_STRIDE),
                            :].astype(BF16)
            for h in range(B_KV_HEADS):
                pcmp_ref[kv * B_KV_HEADS + h, :, l * HEAD_DIM:(l + 1) * HEAD_DIM] = (
                    a[:, h * HEAD_DIM:(h + 1) * HEAD_DIM])


def _proj_in(x2, g, w, *, layer, seq):
    t = x2.shape[0]
    tm = ROW_TILE
    b = t // seq
    tps = seq // tm
    row = lambda i: (i, 0)
    d1, d2 = A_GROUPS[1][1], A_GROUPS[2][1]
    sub = lambda i: (i // tps, 0, i % tps, 0)
    return pl.pallas_call(
        _proj_in_kernel,
        out_shape=(jax.ShapeDtypeStruct((t, _GRP_COLS), BF16),
                   jax.ShapeDtypeStruct((b, d1, seq // d1, _GRP_COLS), BF16),
                   jax.ShapeDtypeStruct((b, d2, seq // d2, _GRP_COLS), BF16),
                   jax.ShapeDtypeStruct((t, _PB_COLS), BF16),
                   jax.ShapeDtypeStruct((4, t // NSA_CMP_STRIDE, NSA_CMP_STRIDE * HEAD_DIM), BF16),
                   jax.ShapeDtypeStruct((t, _C_COLS), BF16),
                   jax.ShapeDtypeStruct((t, _MG_COLS), BF16),
                   jax.ShapeDtypeStruct((t, _BG_COLS), F32)),
        grid=(t // tm,),
        in_specs=[pl.BlockSpec((tm, D_MODEL), row),
                  pl.BlockSpec((1, D_MODEL), lambda i: (0, 0)),
                  pl.BlockSpec((None, D_MODEL, _W_COLS), lambda i: (layer, 0, 0))],
        out_specs=(pl.BlockSpec((tm, _GRP_COLS), row),
                   pl.BlockSpec((1, d1, tm // d1, _GRP_COLS), sub),
                   pl.BlockSpec((1, d2, tm // d2, _GRP_COLS), sub),
                   pl.BlockSpec((tm, _PB_COLS), row),
                   pl.BlockSpec((4, tm // NSA_CMP_STRIDE, NSA_CMP_STRIDE * HEAD_DIM),
                                lambda i: (0, i, 0)),
                   pl.BlockSpec((tm, _C_COLS), row),
                   pl.BlockSpec((tm, _MG_COLS), row),
                   pl.BlockSpec((tm, _BG_COLS), row)),
        scratch_shapes=[pltpu.VMEM((8, tm, LANE), F32)],
        compiler_params=_params(1),
        name="proj_in",
    )(x2, g, w)


def _w_in_segments():
    a = A_HEADS * HEAD_DIM
    bq0 = 3 * a
    bkc0 = bq0 + B_HEADS * HEAD_DIM
    bks0 = bkc0 + 2 * B_KV_HEADS * HEAD_DIM
    bg0 = bks0 + 4 * B_KV_HEADS * HEAD_DIM
    c0 = bg0 + B_HEADS * 3
    mg0 = c0 + _C_COLS
    gw = A_HEADS_PER_GROUP * HEAD_DIM
    segs = [(_OFF_A + (gi * 3 + j) * gw, j * a + gi * gw, gw)
            for gi in range(len(A_GROUPS)) for j in range(3)]
    segs += [(_OFF_PB, bq0, B_HEADS * HEAD_DIM),
             (_OFF_PB + B_HEADS * HEAD_DIM, bks0, bg0 - bks0),
             (_OFF_CMP, bkc0, _CMP_COLS), (_OFF_C, c0, _C_COLS), (_OFF_MG, mg0, _MG_COLS),
             (_OFF_BG, bg0, B_HEADS * 3)]
    return segs


def _prep_w_in_kernel(w_ref, o_ref):
    o_ref[0, :, _OFF_BG:] = jnp.zeros((o_ref.shape[1], _BG_COLS), BF16)
    for dst, src, width in _w_in_segments():
        for c in range(0, width, _PROJ_CHUNK):
            n = min(_PROJ_CHUNK, width - c)
            o_ref[0, :, dst + c:dst + c + n] = w_ref[0, :, src + c:src + c + n].astype(BF16)


def _prep_w_in(w):
    depth, rows, cols = w.shape
    rb = 128
    return pl.pallas_call(
        _prep_w_in_kernel,
        out_shape=jax.ShapeDtypeStruct((depth, rows, _W_COLS), BF16),
        grid=(depth, rows // rb),
        in_specs=[pl.BlockSpec((1, rb, cols), lambda l, r: (l, r, 0))],
        out_specs=pl.BlockSpec((1, rb, _W_COLS), lambda l, r: (l, r, 0)),
        compiler_params=_params(2),
        name="prep_w_in",
    )(w)


_A_UNROLL = 15


def _mixer_a_kernel(pa0_ref, pa1_ref, pa2_ref, bias_ref, o_ref, on_ref, ln_ref, *, seq):
    gw = A_HEADS_PER_GROUP * HEAD_DIM

    def band_block(src, l0, first, gi, rows, bias_head0=None):
        if bias_head0 is None:
            bias_head0 = A_HEADS_PER_GROUP * gi + (A_HEADS + A_HEADS_PER_GROUP if first else 0)
        kr = pl.ds(l0 if first else l0 - A_TILE, 2 * A_TILE)
        q = src[pl.ds(l0, A_TILE), 0:gw] * ATTN_SCALE
        k, v = src[kr, gw:2 * gw], src[kr, 2 * gw:3 * gw]
        lane = lax.broadcasted_iota(jnp.int32, (A_TILE, gw), 1)
        out = lse = None
        for h in range(A_HEADS_PER_GROUP):
            own = (lane >= h * HEAD_DIM) & (lane < (h + 1) * HEAD_DIM)
            bias = bias_ref[bias_head0 + h]
            s = _nt(q * own[0:1, :].astype(BF16), k) + bias
            m = jnp.max(s, axis=-1, keepdims=True)
            e = jnp.exp(s - m)
            l = jnp.sum(e, axis=-1, keepdims=True)
            o_h = jnp.dot(e.astype(BF16), v, preferred_element_type=F32) / l
            lse_h = jnp.broadcast_to(m + jnp.log(l), (A_TILE, gw))
            out = o_h if out is None else jnp.where(own, o_h, out)
            lse = lse_h if lse is None else jnp.where(own, lse_h, lse)
        on_ref[gi, rows, :] = out
        ln_ref[gi, rows, :] = lse

    for gi, ((_, dil), pa_ref) in enumerate(zip(A_GROUPS, (pa0_ref, pa1_ref, pa2_ref))):
        n_blocks = seq // dil // A_TILE
        if dil == 1:
            band_block(pa_ref.at[0], 0, True, gi, pl.ds(0, A_TILE))
            assert (n_blocks - 1) % _A_UNROLL == 0

            def body0(it, c, pa_ref=pa_ref, gi=gi):
                for u in range(_A_UNROLL):
                    l0 = pl.multiple_of((1 + it * _A_UNROLL + u) * A_TILE, A_TILE)
                    band_block(pa_ref.at[0], l0, False, gi, pl.ds(l0, A_TILE))
                return c
            lax.fori_loop(0, (n_blocks - 1) // _A_UNROLL, body0, 0)
        elif n_blocks > 1:
            for r in range(dil):
                band_block(pa_ref.at[0, r], 0, True, gi, pl.ds(r, A_TILE, stride=dil))

            for i in range(1, n_blocks):
                l0 = i * A_TILE
                for r in range(dil):
                    band_block(pa_ref.at[0, r], l0, False, gi, pl.ds(l0 * dil + r, A_TILE, stride=dil))
        else:
            assert (dil - 1) % _A_UNROLL == 0
            band_block(pa_ref.at[0], 0, True, gi, pl.ds(0, A_TILE, stride=dil))

            def body2(it, c, pa_ref=pa_ref, gi=gi, dil=dil):
                for u in range(_A_UNROLL):
                    r = 1 + it * _A_UNROLL + u
                    band_block(pa_ref.at[0], pl.multiple_of(r * A_TILE, A_TILE), False, gi,
                               pl.ds(r, A_TILE, stride=dil), bias_head0=A_HEADS)
                return c
            lax.fori_loop(0, (dil - 1) // _A_UNROLL, body2, 0)

    chunk = T_TILE
    for c0 in range(0, seq, chunk):
        rows = slice(c0, c0 + chunk)
        lse = [ln_ref[g, rows, :] for g in range(len(A_GROUPS))]
        top = functools.reduce(jnp.maximum, lse)
        ex = [jnp.exp(v - top) for v in lse]
        den = functools.reduce(lambda a, b: a + b, ex)
        for g in range(len(A_GROUPS)):
            o_ref[0, rows, g * gw:(g + 1) * gw] = (ex[g] / den * on_ref[g, rows, :]).astype(BF16)


def _mixer_a(pa0, pa1, pa2, bias):
    b, seq, _ = pa0.shape
    return pl.pallas_call(
        functools.partial(_mixer_a_kernel, seq=seq),
        out_shape=jax.ShapeDtypeStruct((b, seq, _GRP_COLS), BF16),
        grid=(b,),
        in_specs=[pl.BlockSpec((1, seq, _GRP_COLS), lambda i: (i, 0, 0)),
                  pl.BlockSpec((1,) + pa1.shape[1:], lambda i: (i, 0, 0, 0)),
                  pl.BlockSpec((1,) + pa2.shape[1:], lambda i: (i, 0, 0)),
                  pl.BlockSpec(bias.shape, lambda i: (0, 0, 0))],
        out_specs=pl.BlockSpec((1, seq, _GRP_COLS), lambda i: (i, 0, 0)),
        scratch_shapes=[pltpu.VMEM((len(A_GROUPS), seq, LANE), F32),
                        pltpu.VMEM((len(A_GROUPS), seq, LANE), F32)],
        compiler_params=_params(1),
        name="mixer_a",
    )(pa0, pa1, pa2, bias)


def _compress_kernel(r_ref, pek_ref, pev_ref, w1k_ref, w1v_ref, w2k_ref, w2v_ref, o_ref):
    half = NSA_CMP_STRIDE * HEAD_DIM
    for kv, (pe_ref, w1_ref, w2_ref) in enumerate(((pek_ref, w1k_ref, w2k_ref),
                                                   (pev_ref, w1v_ref, w2v_ref))):
        for h in range(B_KV_HEADS):
            r = r_ref[kv * B_KV_HEADS + h, 0].astype(F32)
            lo = jnp.dot((r + pe_ref[:, :half]).astype(BF16), w1_ref[:half, :],
                         preferred_element_type=F32)
            hi = jnp.dot((r + pe_ref[:, half:]).astype(BF16), w1_ref[half:, :],
                         preferred_element_type=F32)
            hid = lo + pltpu.roll(hi, hi.shape[0] - 1, axis=0)
            act = jax.nn.gelu(hid).astype(BF16)
            o_ref[0, kv * B_KV_HEADS + h] = jnp.dot(act, w2_ref[...],
                                                    preferred_element_type=F32).astype(BF16)


def _compress(r, pe_k, pe_v, w1_k, w1_v, w2_k, w2_v):
    _, b, m, c = r.shape
    const2 = lambda i: (0, 0)
    return pl.pallas_call(
        _compress_kernel,
        out_shape=jax.ShapeDtypeStruct((b, 4, m, HEAD_DIM), BF16),
        grid=(b,),
        in_specs=[pl.BlockSpec((4, 1, m, c), lambda i: (0, i, 0, 0)),
                  pl.BlockSpec(pe_k.shape, const2), pl.BlockSpec(pe_v.shape, const2),
                  pl.BlockSpec(w1_k.shape, const2), pl.BlockSpec(w1_v.shape, const2),
                  pl.BlockSpec(w2_k.shape, const2), pl.BlockSpec(w2_v.shape, const2)],
        out_specs=pl.BlockSpec((1, 4, m, HEAD_DIM), lambda i: (i, 0, 0, 0)),
        compiler_params=_params(1),
        name="nsa_compress",
    )(r, pe_k, pe_v, w1_k, w1_v, w2_k, w2_v)


MASK_BIG = 2.0 ** 100
SEL_PAD = 32


def _nt(a, b, **kw):
    return lax.dot_general(a, b, (((1,), (1,)), ((), ())), preferred_element_type=F32, **kw)


def _top_mask_t(score, n_rows, n_top):
    sub = lax.broadcasted_iota(jnp.int32, score.shape, 0)
    rank = jnp.zeros(score.shape, F32)
    for j in range(n_rows):
        row = score[j:j + 1, :]
        ge = jnp.where(row >= score, 1.0, 0.0)
        gt = jnp.where(row > score, 1.0, 0.0)
        rank = rank + jnp.where(sub > j, ge, gt)
    return rank < n_top


def _top_keep(score, n_rows, n_top, block):
    free = n_top * block
    ranked = jnp.where(_top_mask_t(score[:, free:], n_rows, n_top), 1.0, 0.0)
    return jnp.concatenate([jnp.ones((score.shape[0], free), F32), ranked], axis=1)


BAND_SLOTS = 4


def _when(cond, fn):
    if isinstance(cond, bool):
        if cond:
            fn()
    else:
        pl.when(cond)(fn)


def _loop_pairs(lo, hi, body):
    if isinstance(lo, int) and isinstance(hi, int):
        for j in range(lo, hi):
            body(j)
        return
    n = jnp.maximum(hi - lo, 0)

    def pair(p, c):
        body(lo + 2 * p)
        body(lo + 2 * p + 1)
        return c
    lax.fori_loop(0, n // 2, pair, 0)
    pl.when(n % 2 == 1)(lambda: body(hi - 1))


def _two_pass_attention(i, streams, band_streams=()):
    groups = ((streams, lambda j: j), (band_streams, lambda j: j % BAND_SLOTS))

    def key_rows(j):
        return pl.ds(j * T_TILE if isinstance(j, int) else pl.multiple_of(j * T_TILE, T_TILE), T_TILE)

    def logits(group, j, d):
        some, slot = group
        for k_tile, q_t, _, bias_tile, s_ref, rmp_ref, _ in some:
            s = jnp.dot(k_tile(key_rows(j)), q_t(), preferred_element_type=F32)
            if d is not None:
                s = s + bias_tile(d)
            s_ref[slot(j)] = s
            fold = jnp.max(s.reshape(s.shape[0] // 8, 8, s.shape[1]), axis=0)
            rmp_ref[...] = fold if d == 0 else jnp.maximum(rmp_ref[...], fold)

    def near(n_full, n_band):
        for d in range(n_full):
            logits(groups[0], i - d, d)
        for d in range(n_band):
            logits(groups[1], i - d, d)

    if band_streams:
        _when(i == 0, lambda: near(1, 1))
        _when(i == 1, lambda: near(2, 2))
        _when(i >= 2, lambda: near(2, 3))
    else:
        logits(groups[0], i, 0)
        _when(i >= 1, lambda: logits(groups[0], i - 1, 1))

    _loop_pairs(0, i - 1, lambda j: logits(groups[0], j, None))

    tops = {}
    for some, _ in groups:
        for stream in some:
            rmp_ref, acc_ref = stream[-2:]
            tops[id(rmp_ref)] = jnp.max(rmp_ref[...], axis=0, keepdims=True)
            acc_ref[...] = jnp.zeros(acc_ref.shape, F32)

    def pv(active):
        def body(j):
            for some, slot in active:
                for _, _, vt_ref, _, s_ref, rmp_ref, acc_ref in some:
                    e = jnp.exp(s_ref[slot(j)] - tops[id(rmp_ref)])
                    acc_ref[...] += jnp.dot(vt_ref[j], e.astype(BF16), preferred_element_type=F32)
        return body

    if band_streams:
        band_lo = max(i - 2, 0) if isinstance(i, int) else jnp.maximum(i - 2, 0)
        _loop_pairs(0, band_lo, pv(groups[:1]))
        pv_all = pv(groups)

        def last_tiles(n):
            for d in reversed(range(n)):
                pv_all(i - d)

        _when(i == 0, lambda: last_tiles(1))
        _when(i == 1, lambda: last_tiles(2))
        _when(i >= 2, lambda: last_tiles(3))
    else:
        _loop_pairs(0, i + 1, pv(groups[:1]))

    outs = []
    for some, _ in groups:
        for *_, acc_ref in some:
            acc = acc_ref[...]
            outs.append(acc[:HEAD_DIM, :] / acc[HEAD_DIM:HEAD_DIM + 1, :])
    return outs


def _eye(n, dtype=BF16):
    return (lax.broadcasted_iota(jnp.int32, (n, n), 0)
            == lax.broadcasted_iota(jnp.int32, (n, n), 1)).astype(dtype)


def _transposed_values(v_ref_rows, n_tiles):
    rows = lax.broadcasted_iota(jnp.int32, (LANE, HEAD_DIM), 0)
    cols = lax.broadcasted_iota(jnp.int32, (LANE, HEAD_DIM), 1)
    place = (rows == cols).astype(BF16)
    ones_row = (lax.broadcasted_iota(jnp.int32, (LANE, T_TILE), 0) == HEAD_DIM).astype(F32)
    for j in range(n_tiles):
        v = v_ref_rows(slice(j * T_TILE, (j + 1) * T_TILE))
        yield j, (_nt(place, v) + ones_row).astype(BF16)


def _with_block_onehot(k, block):
    n = k.shape[0]
    rows = lax.broadcasted_iota(jnp.int32, (HEAD_DIM, LANE), 0)
    cols = lax.broadcasted_iota(jnp.int32, (HEAD_DIM, LANE), 1)
    place = (rows == cols).astype(BF16)
    lane = lax.broadcasted_iota(jnp.int32, (n, LANE), 1)
    hot = (lane - HEAD_DIM == lax.broadcasted_iota(jnp.int32, (n, LANE), 0) // block).astype(F32)
    return (jnp.dot(k, place, preferred_element_type=F32) + hot).astype(BF16)


def _augment_q_t(q_t, allowed_t):
    pen = (allowed_t - 1.0) * MASK_BIG
    return jnp.concatenate([q_t, pen, jnp.zeros(pen.shape, F32)], axis=0).astype(BF16)


_N_SLC = 32
_G_ROWS = B_GROUP * T_TILE


def _nsa_kernel(pb_ref, cmp_ref, bg_ref, bias_ref, o_ref,
                kaug_ref, vst_ref, vwt_ref, eye_ref, sig_ref, q3t_ref, qaugt_ref, s_ref, sw_ref,
                rmp_ref, acc_ref, ocmp_ref, *, seq):
    n_tiles = seq // T_TILE
    n_cmp_rows = seq // NSA_CMP_STRIDE
    q_cols = B_HEADS * HEAD_DIM
    ks0, vs0, kw0, vw0 = (q_cols + i * B_KV_HEADS * HEAD_DIM for i in range(4))

    def head_cols(c0, h):
        return slice(c0 + h * HEAD_DIM, c0 + (h + 1) * HEAD_DIM)

    j_id = lax.broadcasted_iota(jnp.int32, (_N_SLC, n_cmp_rows), 0)
    c_id = lax.broadcasted_iota(jnp.int32, (_N_SLC, n_cmp_rows), 1)
    overlap_t = ((c_id * NSA_CMP_STRIDE < (j_id + 1) * NSA_SLC_BLOCK)
                 & (c_id * NSA_CMP_STRIDE + NSA_CMP_BLOCK > j_id * NSA_SLC_BLOCK)).astype(F32)
    eye_ref[...] = _eye(T_TILE)
    for h in range(B_KV_HEADS):
        kaug_ref[h] = _with_block_onehot(pb_ref[0, :, head_cols(ks0, h)], NSA_SLC_BLOCK)
        for j, vt in _transposed_values(lambda r, h=h: pb_ref[0, r, head_cols(vs0, h)], n_tiles):
            vst_ref[h, j] = vt
        for j, vt in _transposed_values(lambda r, h=h: pb_ref[0, r, head_cols(vw0, h)], n_tiles):
            vwt_ref[h, j] = vt

    def tiles(a):
        return [a[:, j * T_TILE:(j + 1) * T_TILE] for j in range(n_tiles)]

    pick = (lax.broadcasted_iota(jnp.int32, (SEL_PAD, _BG_COLS), 0)
            == lax.broadcasted_iota(jnp.int32, (SEL_PAD, _BG_COLS), 1)).astype(F32)
    sig_all = jax.nn.sigmoid(_nt(pick, bg_ref[0], precision=lax.Precision.HIGHEST))
    for j, t in enumerate(tiles(sig_all)):
        sig_ref[j] = t

    eye_h = _eye(HEAD_DIM)
    t_lane = lax.broadcasted_iota(jnp.int32, (n_cmp_rows, seq), 1)
    c_end = (lax.broadcasted_iota(jnp.int32, (n_cmp_rows, seq), 0) * NSA_CMP_STRIDE
             + NSA_CMP_BLOCK - 1)
    ok = (c_end <= t_lane) & (c_end < seq)
    ok_f = ok.astype(F32)
    j_sub = lax.broadcasted_iota(jnp.int32, (_N_SLC, seq), 0)
    jt = lax.broadcasted_iota(jnp.int32, (_N_SLC, seq), 1) // NSA_SLC_BLOCK
    forced = ((j_sub == 0) | (j_sub == jt) | (j_sub == jt - 1)).astype(F32)
    valid = j_sub <= jt

    for h in range(B_KV_HEADS):
        kc = cmp_ref[0, h]
        vct = _nt(eye_h, cmp_ref[0, B_KV_HEADS + h]).astype(BF16)
        q_ts, p_sum = [], None
        for g in range(B_GROUP):
            q_t = _nt(eye_h, pb_ref[0, :, head_cols(0, h * B_GROUP + g)]) * ATTN_SCALE
            q_ts.append(q_t)
            lc = jnp.where(ok, jnp.dot(kc, q_t.astype(BF16), preferred_element_type=F32), NEG_INF)
            ec = jnp.exp(lc - jnp.max(lc, axis=0, keepdims=True))
            pc = ec / jnp.sum(ec, axis=0, keepdims=True) * ok_f
            o_cmp = jnp.dot(vct, pc.astype(BF16), preferred_element_type=F32)
            for j, t in enumerate(tiles(o_cmp)):
                ocmp_ref[h, j, :, g * T_TILE:(g + 1) * T_TILE] = t
            p_sum = pc if p_sum is None else p_sum + pc
        imp_t = jnp.dot(overlap_t, p_sum, precision=lax.Precision.HIGHEST,
                        preferred_element_type=F32)
        score = jnp.where(valid, imp_t + FORCE_BONUS * forced, NEG_INF)
        sel_t = jnp.where(valid, _top_keep(score, _N_SLC, NSA_SLC_TOPK, NSA_SLC_BLOCK), 0.0)
        pen_tiles = tiles((sel_t - 1.0) * MASK_BIG)
        q_tiles = [tiles(q_t) for q_t in q_ts]
        for j in range(n_tiles):
            q3t = jnp.concatenate([q_tiles[g][j] for g in range(B_GROUP)], axis=1)
            pen = jnp.concatenate([pen_tiles[j]] * B_GROUP, axis=1)
            q3t_ref[h, j] = q3t.astype(BF16)
            qaugt_ref[h, j] = jnp.concatenate([q3t, pen, jnp.zeros(pen.shape, F32)],
                                              axis=0).astype(BF16)

    def tile_body(i, carry):
        rows = pl.ds(pl.multiple_of(i * T_TILE, T_TILE), T_TILE)

        outs = _two_pass_attention(
            i,
            [(lambda kr, h=h: kaug_ref[h, kr, :], lambda h=h: qaugt_ref[h, i], vst_ref.at[h],
              lambda d, h=h: bias_ref[h, d], s_ref.at[h], rmp_ref.at[0, h], acc_ref.at[0, h])
             for h in range(B_KV_HEADS)],
            [(lambda kr, h=h: pb_ref[0, kr, head_cols(kw0, h)], lambda h=h: q3t_ref[h, i],
              vwt_ref.at[h], lambda d, h=h: bias_ref[h, d], sw_ref.at[h], rmp_ref.at[1, h],
              acc_ref.at[1, h]) for h in range(B_KV_HEADS)])
        oslc, owin = outs[:B_KV_HEADS], outs[B_KV_HEADS:]

        sig_t = sig_ref[i]
        heads = []
        for h in range(B_KV_HEADS):
            for g in range(B_GROUP):
                head = h * B_GROUP + g
                gc = slice(g * T_TILE, (g + 1) * T_TILE)
                heads.append(sig_t[3 * head:3 * head + 1, :] * ocmp_ref[h, i, :, gc]
                             + sig_t[3 * head + 1:3 * head + 2, :] * oslc[h][:, gc]
                             + sig_t[3 * head + 2:3 * head + 3, :] * owin[h][:, gc])
        o_t = jnp.concatenate(heads, axis=0).astype(BF16)
        o_ref[0, rows, :] = _nt(eye_ref[...], o_t).astype(o_ref.dtype)
        return carry

    lax.fori_loop(0, n_tiles, tile_body, 0)


def _nsa(pb, cmp, bg, bias):
    b, seq, _ = pb.shape
    n_tiles = seq // T_TILE
    return pl.pallas_call(
        functools.partial(_nsa_kernel, seq=seq),
        out_shape=jax.ShapeDtypeStruct((b, seq, B_HEADS * HEAD_DIM), BF16),
        grid=(b,),
        in_specs=[pl.BlockSpec((1, seq, _PB_COLS), lambda i: (i, 0, 0)),
                  pl.BlockSpec((1,) + cmp.shape[1:], lambda i: (i, 0, 0, 0)),
                  pl.BlockSpec((1, seq, _BG_COLS), lambda i: (i, 0, 0)),
                  pl.BlockSpec(bias.shape, lambda i: (0, 0, 0, 0))],
        out_specs=pl.BlockSpec((1, seq, B_HEADS * HEAD_DIM), lambda i: (i, 0, 0)),
        scratch_shapes=[pltpu.VMEM((B_KV_HEADS, seq, LANE), BF16),
                        pltpu.VMEM((B_KV_HEADS, n_tiles, LANE, T_TILE), BF16),
                        pltpu.VMEM((B_KV_HEADS, n_tiles, LANE, T_TILE), BF16),
                        pltpu.VMEM((T_TILE, T_TILE), BF16),
                        pltpu.VMEM((n_tiles, SEL_PAD, T_TILE), F32),
                        pltpu.VMEM((B_KV_HEADS, n_tiles, HEAD_DIM, _G_ROWS), BF16),
                        pltpu.VMEM((B_KV_HEADS, n_tiles, LANE, _G_ROWS), BF16),
                        pltpu.VMEM((B_KV_HEADS, n_tiles, T_TILE, _G_ROWS), F32),
                        pltpu.VMEM((B_KV_HEADS, BAND_SLOTS, T_TILE, _G_ROWS), F32),
                        pltpu.VMEM((2, B_KV_HEADS, 8, _G_ROWS), F32),
                        pltpu.VMEM((2, B_KV_HEADS, LANE, _G_ROWS), F32),
                        pltpu.VMEM((B_KV_HEADS, n_tiles, HEAD_DIM, _G_ROWS), F32)],
        compiler_params=_params(1),
        name="nsa",
    )(pb, cmp, bg, bias)


def _moba_kernel(pc_ref, bias_ref, o_ref, kaug_ref, vt_ref, eye_ref, qaugt_ref, s_ref,
                 rmp_ref, acc_ref, *, seq):
    n_blk = seq // MOBA_BLOCK
    width = C_HEADS * HEAD_DIM
    eye_ref[...] = _eye(T_TILE)

    def cols(j, h):
        return slice(j * width + h * HEAD_DIM, j * width + (h + 1) * HEAD_DIM)

    n_sub = lax.broadcasted_iota(jnp.int32, (SEL_PAD, seq), 0)
    own = lax.broadcasted_iota(jnp.int32, (SEL_PAD, seq), 1) // MOBA_BLOCK
    past = n_sub < own
    eye_h = _eye(HEAD_DIM)
    for h in range(C_HEADS):
        kaug_ref[h] = _with_block_onehot(pc_ref[0, :, cols(1, h)], MOBA_BLOCK)
        for j, vt in _transposed_values(lambda r, h=h: pc_ref[0, r, cols(2, h)], n_blk):
            vt_ref[h, j] = vt
        kmean = jnp.concatenate(
            [jnp.mean(pc_ref[0, n * MOBA_BLOCK:(n + 1) * MOBA_BLOCK, cols(1, h)].astype(F32), axis=0,
                      keepdims=True) for n in range(n_blk)]
            + [jnp.zeros((SEL_PAD - n_blk, HEAD_DIM), F32)], axis=0)
        q_t = _nt(eye_h, pc_ref[0, :, cols(0, h)])
        gate_t = jnp.dot(kmean, q_t, precision=lax.Precision.HIGHEST,
                         preferred_element_type=F32)
        score = jnp.where(past, gate_t, NEG_INF)
        picked = past & (_top_keep(score, n_blk, MOBA_TOPK, MOBA_BLOCK) > 0.5)
        allowed_t = jnp.where(picked | (n_sub == own), 1.0, 0.0)
        qaug_t = _augment_q_t(q_t * ATTN_SCALE, allowed_t)
        for j in range(n_blk):
            qaugt_ref[h, j] = qaug_t[:, j * T_TILE:(j + 1) * T_TILE]

    for i in range(n_blk):
        rows = slice(i * MOBA_BLOCK, (i + 1) * MOBA_BLOCK)
        outs = _two_pass_attention(
            i, [(lambda kr, h=h: kaug_ref[h, kr, :], lambda h=h, i=i: qaugt_ref[h, i], vt_ref.at[h],
                 lambda d, h=h: bias_ref[h, d], s_ref.at[h], rmp_ref.at[h], acc_ref.at[h])
                for h in range(C_HEADS)])
        o_t = jnp.concatenate(outs, axis=0).astype(BF16)
        o_ref[0, rows, :] = _nt(eye_ref[...], o_t).astype(o_ref.dtype)


def _moba(pc, bias):
    b, seq, _ = pc.shape
    width = C_HEADS * HEAD_DIM
    n_tiles = seq // T_TILE
    return pl.pallas_call(
        functools.partial(_moba_kernel, seq=seq),
        out_shape=jax.ShapeDtypeStruct((b, seq, width), BF16),
        grid=(b,),
        in_specs=[pl.BlockSpec((1, seq, _C_COLS), lambda i: (i, 0, 0)),
                  pl.BlockSpec(bias.shape, lambda i: (0, 0, 0, 0))],
        out_specs=pl.BlockSpec((1, seq, width), lambda i: (i, 0, 0)),
        scratch_shapes=[pltpu.VMEM((C_HEADS, seq, LANE), BF16),
                        pltpu.VMEM((C_HEADS, n_tiles, LANE, T_TILE), BF16),
                        pltpu.VMEM((T_TILE, T_TILE), BF16),
                        pltpu.VMEM((C_HEADS, n_tiles, LANE, T_TILE), BF16),
                        pltpu.VMEM((C_HEADS, n_tiles, T_TILE, T_TILE), F32),
                        pltpu.VMEM((C_HEADS, 8, T_TILE), F32),
                        pltpu.VMEM((C_HEADS, LANE, T_TILE), F32)],
        compiler_params=_params(1),
        name="moba",
    )(pc, bias)


def _merge_kernel(oa_ref, ob_ref, oc_ref, mg_ref, x_ref, wb_ref, wo_ref, y_ref):
    r0 = A_HEADS * HEAD_DIM
    r1 = r0 + B_HEADS * HEAD_DIM
    ya = jnp.dot(oa_ref[...], wb_ref[0:r0, :], preferred_element_type=F32)
    yb = jnp.dot(ob_ref[...], wb_ref[r0:r1, :], preferred_element_type=F32)
    yc = jnp.dot(oc_ref[...], wb_ref[r1:, :], preferred_element_type=F32)
    merged = (jax.nn.sigmoid(mg_ref[:, 0:D_MODEL].astype(F32)) * ya
              + jax.nn.sigmoid(mg_ref[:, D_MODEL:2 * D_MODEL].astype(F32)) * yb
              + jax.nn.sigmoid(mg_ref[:, 2 * D_MODEL:].astype(F32)) * yc)
    y_ref[...] = x_ref[...] + jnp.dot(merged.astype(BF16), wo_ref[...], preferred_element_type=F32)


def _merge(oa, ob, oc, mg, x2, wb, wo, *, layer):
    t = x2.shape[0]
    tm = ROW_TILE
    row = lambda i: (i, 0)
    weight = pl.BlockSpec((None,) + wb.shape[1:], lambda i: (layer, 0, 0))
    return pl.pallas_call(
        _merge_kernel,
        out_shape=jax.ShapeDtypeStruct((t, D_MODEL), F32),
        grid=(t // tm,),
        in_specs=[pl.BlockSpec((tm, oa.shape[1]), row),
                  pl.BlockSpec((tm, ob.shape[1]), row), pl.BlockSpec((tm, oc.shape[1]), row),
                  pl.BlockSpec((tm, _MG_COLS), row), pl.BlockSpec((tm, D_MODEL), row),
                  weight, weight],
        out_specs=pl.BlockSpec((tm, D_MODEL), row),
        compiler_params=_params(1),
        name="merge",
    )(oa, ob, oc, mg, x2, wb, wo)


MXU_WIDTH = 256
_FF_CHUNK_TILES = 6
_FF_BOUNDS = tuple(range(0, D_FF, _FF_CHUNK_TILES * MXU_WIDTH)) + (D_FF,)
_FF_CHUNKS = tuple(zip(_FF_BOUNDS[:-1], _FF_BOUNDS[1:]))


def _ffn_kernel(x_ref, halo_ref, g_ref, wu_ref, cw_ref, cb_ref, wd_ref, gf_ref, y_ref, xn_ref,
                *, tiles_per_seq, final_norm):
    i = pl.program_id(0)
    x = x_ref[...]
    g = g_ref[...]
    keep = (i % tiles_per_seq != 0).astype(F32)
    xn_ref[0:FFN_HALO, :] = (_rmsnorm_rows(halo_ref[...], g) * keep).astype(BF16)
    xn_ref[FFN_HALO:, :] = _rmsnorm_rows(x, g).astype(BF16)
    xn = xn_ref[...]
    rows = xn.shape[0]

    def conv(c0, c1):
        hcol = jnp.dot(xn, wu_ref[:, c0:c1], preferred_element_type=F32)
        out = (cw_ref[2:3, c0:c1] * hcol
               + cw_ref[1:2, c0:c1] * pltpu.roll(hcol, 1, axis=0)
               + cw_ref[0:1, c0:c1] * pltpu.roll(hcol, 2, axis=0)
               + cb_ref[:, c0:c1])
        return out[FFN_HALO:rows]

    acc = x
    for c0, c1 in _FF_CHUNKS:
        a = conv(c0, c1)
        u = conv(D_FF + c0, D_FF + c1)
        act = (a * jax.nn.sigmoid(a) * u).astype(BF16)
        acc = acc + jnp.dot(act, wd_ref[c0:c1, :], preferred_element_type=F32)
    if final_norm:
        acc = _rmsnorm_rows(acc, gf_ref[...])
    y_ref[...] = acc


def _ffn(x2, g, wu, cw, cb, wd, gf, *, layer, seq, final_norm):
    t = x2.shape[0]
    tm = ROW_TILE
    const = lambda i: (0, 0)
    per_layer = lambda w: pl.BlockSpec((None,) + w.shape[1:], lambda i: (layer, 0, 0))
    halo_blocks = tm // FFN_HALO
    return pl.pallas_call(
        functools.partial(_ffn_kernel, tiles_per_seq=seq // tm, final_norm=final_norm),
        out_shape=jax.ShapeDtypeStruct((t, D_MODEL), F32),
        grid=(t // tm,),
        in_specs=[pl.BlockSpec((tm, D_MODEL), lambda i: (i, 0)),
                  pl.BlockSpec((FFN_HALO, D_MODEL), lambda i: (jnp.maximum(i * halo_blocks - 1, 0), 0)),
                  pl.BlockSpec((1, D_MODEL), const),
                  per_layer(wu), pl.BlockSpec(cw.shape, const),
                  pl.BlockSpec(cb.shape, const), per_layer(wd),
                  pl.BlockSpec((1, D_MODEL), const)],
        out_specs=pl.BlockSpec((tm, D_MODEL), lambda i: (i, 0)),
        scratch_shapes=[pltpu.VMEM((tm + FFN_HALO, D_MODEL), BF16)],
        compiler_params=_params(1),
        name="conv_ffn",
    )(x2, x2, g, wu, cw, cb, wd, gf)


def kernel(x, rel_bias, norm_mix, w_in, cmp_pe_k, cmp_w1_k, cmp_w2_k, cmp_pe_v, cmp_w1_v, cmp_w2_v,
           w_branch, w_out, norm_ffn, w_up, conv_w, conv_b, w_down, norm_final):
    b, s, d = x.shape
    depth = w_in.shape[0]
    t = b * s
    assert d == D_MODEL and s % T_TILE == 0 and s // NSA_SLC_BLOCK == _N_SLC and t % ROW_TILE == 0
    assert s % ROW_TILE == 0 and all(s % (dil * A_TILE) == 0 for _, dil in A_GROUPS)
    assert all(win // dil == A_TILE for win, dil in A_GROUPS) and A_GROUPS[0][1] == 1
    assert all(ROW_TILE % dil == 0 for _, dil in A_GROUPS)

    hg = A_HEADS_PER_GROUP
    bias_a = jnp.concatenate([
        _bias_tiles(rel_bias,
                    np.concatenate([_bucket_tile(A_TILE, 1, dil, win // dil),
                                    _bucket_tile(A_TILE, 0, dil, win // dil)], axis=1),
                    gi * hg, hg)
        for gi, (win, dil) in enumerate(A_GROUPS)]
        + [_bias_tiles(rel_bias,
                       np.concatenate([np.full((A_TILE, A_TILE), -1, np.int32),
                                       _bucket_tile(A_TILE, 0, A_GROUPS[-1][1], A_TILE)], axis=1),
                       (len(A_GROUPS) - 1) * hg, hg)]
        + [_bias_tiles(rel_bias,
                       np.concatenate([_bucket_tile(A_TILE, 0, dil, win // dil),
                                       np.full((A_TILE, A_TILE), -1, np.int32)], axis=1),
                       gi * hg, hg)
           for gi, (win, dil) in enumerate(A_GROUPS)], axis=0)
    assert (_bucket_tile(T_TILE, 2, 1, 3 * T_TILE) == REL_BUCKETS - 1).all()
    bias_b = _bias_tiles_t(rel_bias, [_bucket_tile(T_TILE, d, 1, NSA_WINDOW - 1) for d in (0, 1, 2)],
                           A_HEADS, B_HEADS)
    bias_b = bias_b.reshape(B_KV_HEADS, B_GROUP, 3, T_TILE, T_TILE).transpose(0, 2, 3, 1, 4).reshape(
        B_KV_HEADS, 3, T_TILE, _G_ROWS)
    bias_c = _bias_tiles_t(rel_bias, [_bucket_tile(T_TILE, d, 1, s) for d in (0, 1)],
                           A_HEADS + B_HEADS, C_HEADS)

    w_in_k = _prep_w_in(w_in)
    w_branch_k, w_out_k, w_up_k, w_down_k = (w.astype(BF16) for w in (w_branch, w_out, w_up, w_down))
    x2 = x.reshape(t, d)
    for i in range(depth):
        pa0, pa1, pa2, pb, pcmp, pc, mg, bg = _proj_in(x2, norm_mix[i][None, :], w_in_k, layer=i,
                                                       seq=s)

        oa = _mixer_a(pa0.reshape(b, s, _GRP_COLS), pa1, pa2.reshape(b, s, _GRP_COLS), bias_a)

        cmp = _compress(pcmp.reshape(4, b, s // NSA_CMP_STRIDE, NSA_CMP_STRIDE * HEAD_DIM),
                        cmp_pe_k[i].reshape(1, -1), cmp_pe_v[i].reshape(1, -1),
                        cmp_w1_k[i].astype(BF16), cmp_w1_v[i].astype(BF16),
                        cmp_w2_k[i].astype(BF16), cmp_w2_v[i].astype(BF16))
        ob = _nsa(pb.reshape(b, s, _PB_COLS), cmp, bg.reshape(b, s, _BG_COLS), bias_b)

        oc = _moba(pc.reshape(b, s, _C_COLS), bias_c)

        x2 = _merge(oa.reshape(t, -1), ob.reshape(t, -1), oc.reshape(t, -1), mg, x2,
                    w_branch_k, w_out_k, layer=i)
        x2 = _ffn(x2, norm_ffn[i][None, :], w_up_k, conv_w[i], conv_b[i][None, :], w_down_k,
                  norm_final[None, :], layer=i, seq=s, final_norm=(i == depth - 1))
    return x2.reshape(b, s, d)
```
